```python
import jax
import jax.numpy as jnp
from jax import lax
import numpy as np

D_MODEL = 1024
BATCH = 32
SEQ = 2048
DEPTH = 2

CHUNK = 64
N_MEM = 256
GROUP_WIDTH = D_MODEL // 4
N_GROUPS = 5
D_MIX = N_GROUPS * GROUP_WIDTH
N_HEADS = 4
HEAD_DIM = GROUP_WIDTH // N_HEADS
POOL_WINDOWS = (2, 4, 8, 16)
POOL_CH = GROUP_WIDTH // len(POOL_WINDOWS)
Q_BLOCK = 128
EPS = 1e-6
NEG_BIG = -1e30
LB_FLOOR = 1e-30
IN_SPLITS = (GROUP_WIDTH,) * 4 + (N_HEADS,) + (GROUP_WIDTH,) * 12
D_IN = sum(IN_SPLITS)

kernel_name = "hybrid_fox_stickbreak_hgrn2_pool_memory"

F32 = jnp.float32


def _rms(x, g):
    xf = x.astype(F32)
    y = xf * lax.rsqrt(jnp.mean(xf * xf, axis=-1, keepdims=True) + EPS)
    return (y * g.astype(F32)).astype(x.dtype)


def _split_heads(t):
    b, s, _ = t.shape
    return t.reshape(b, s, N_HEADS, HEAD_DIM).transpose(0, 2, 1, 3)


def _merge_heads(t):
    b, h, s, d = t.shape
    return t.transpose(0, 2, 1, 3).reshape(b, s, h * d)


def _forgetting_attention(q, k, v, log_f):
    s_len = q.shape[2]
    c = jnp.cumsum(log_f, axis=-1)
    scale = HEAD_DIM ** -0.5
    outs = []
    for i in range(s_len // Q_BLOCK):
        t0, t1 = i * Q_BLOCK, (i + 1) * Q_BLOCK
        logits = jnp.einsum('bhtd,bhsd->bhts', q[:, :, t0:t1], k[:, :, :t1]).astype(F32) * scale
        logits = logits + c[:, :, t0:t1, None] - c[:, :, None, :t1]
        mask = jnp.arange(t1)[None, :] <= jnp.arange(t0, t1)[:, None]
        probs = jax.nn.softmax(jnp.where(mask, logits, NEG_BIG), axis=-1)
        outs.append(jnp.einsum('bhts,bhsd->bhtd', probs.astype(v.dtype), v[:, :, :t1]))
    return jnp.concatenate(outs, axis=2)


def _stick_breaking_attention(q, k, v):
    s_len = q.shape[2]
    scale = HEAD_DIM ** -0.5
    outs = []
    for i in range(s_len // Q_BLOCK):
        t0, t1 = i * Q_BLOCK, (i + 1) * Q_BLOCK
        z = jnp.einsum('bhtd,bhsd->bhts', q[:, :, t0:t1], k[:, :, :t1]).astype(F32) * scale
        mask = jnp.arange(t1)[None, :] < jnp.arange(t0, t1)[:, None]
        log_one_minus = jnp.where(mask, jax.nn.log_sigmoid(-z), 0.0)
        log_between = lax.cumsum(log_one_minus, axis=3, reverse=True) - log_one_minus
        log_w = jnp.where(mask, jax.nn.log_sigmoid(z) + log_between, NEG_BIG)
        weights = jnp.where(mask, jnp.exp(log_w), 0.0)
        outs.append(jnp.einsum('bhts,bhsd->bhtd', weights.astype(v.dtype), v[:, :, :t1]))
    return jnp.concatenate(outs, axis=2)


def _hgrn2(q, k, v, log_f):
    b, h, s_len, dk = q.shape
    dv = v.shape[-1]
    n_chunks = s_len // CHUNK

    def chunks(t):
        return t.astype(F32).reshape(b, h, n_chunks, CHUNK, t.shape[-1]).transpose(2, 0, 1, 3, 4)

    causal = jnp.tril(jnp.ones((CHUNK, CHUNK), dtype=bool))[:, :, None]

    def step(state, inp):
        qc, kc, vc, gc = inp
        bcum = jnp.cumsum(gc, axis=2)
        o_inter = jnp.einsum('bhtd,bhde->bhte', qc * jnp.exp(bcum), state)
        diff = bcum[:, :, :, None, :] - bcum[:, :, None, :, :]
        decay = jnp.where(causal, jnp.exp(jnp.where(causal, diff, 0.0)), 0.0)
        scores = jnp.einsum('bhtd,bhsd,bhtsd->bhts', qc, kc, decay)
        o_intra = jnp.einsum('bhts,bhse->bhte', scores, vc)
        b_last = bcum[:, :, -1, :]
        state = jnp.exp(b_last)[..., None] * state + jnp.einsum(
            'bhsd,bhse->bhde', kc * jnp.exp(b_last[:, :, None, :] - bcum), vc)
        return state, o_inter + o_intra

    state0 = jnp.zeros((b, h, dk, dv), F32)
    _, o = lax.scan(step, state0, (chunks(q), chunks(k), chunks(v), chunks(log_f)))
    return o.transpose(1, 2, 0, 3, 4).reshape(b, h, s_len, dv)


def _pool_mixer(u, w, scale):
    b, s_len, _ = u.shape
    n_g = len(POOL_WINDOWS)
    uf = u.astype(F32).reshape(b, s_len, n_g, POOL_CH)
    cs = jnp.cumsum(uf, axis=1)
    cs = jnp.concatenate([jnp.zeros_like(cs[:, :1]), cs], axis=1)
    pos = jnp.arange(1, s_len + 1, dtype=F32)
    means = []
    for gi, win in enumerate(POOL_WINDOWS):
        c = cs[:, :, gi]
        hi = c[:, 1:]
        lo = jnp.pad(c[:, :s_len + 1 - win], ((0, 0), (win - 1, 0), (0, 0)))
        means.append((hi - lo) / jnp.minimum(pos, win)[None, :, None])
    pooled = jnp.stack(means, axis=2)
    y = jnp.einsum('bsgc,gcd->bsgd', pooled - uf, w.astype(F32))
    y = y * scale.astype(F32).reshape(n_g, POOL_CH)
    return y.reshape(b, s_len, GROUP_WIDTH)


def _memory_attention(q, mem, mem_norm_g, mem_w_kv, q_norm, k_norm):
    mn = _rms(mem, mem_norm_g)
    kv = jnp.einsum('bmd,dn->bmn', mn, mem_w_kv)
    k, v = jnp.split(kv, 2, axis=-1)
    qh = _rms(_split_heads(q), q_norm)
    kh = _rms(_split_heads(k), k_norm)
    vh = _split_heads(v)
    logits = jnp.einsum('bhtd,bhmd->bhtm', qh, kh).astype(F32) * (HEAD_DIM ** -0.5)
    probs = jax.nn.softmax(logits, axis=-1)
    return _merge_heads(jnp.einsum('bhtm,bhmd->bhtd', probs.astype(vh.dtype), vh))


def _hybrid_layer(x, mem, norm_g, w_in, fox_f_bias, fox_q_norm, fox_k_norm, lower_bound,
                  hgrn_out_norm, pool_w, pool_scale, mem_norm_g, mem_w_kv, mem_q_norm,
                  mem_k_norm, w_out):
    h = _rms(x, norm_g)
    proj = jnp.einsum('bsd,dn->bsn', h, w_in)
    split_points = np.cumsum(IN_SPLITS)[:-1].tolist()
    (fq, fk, fv, fg, ff, sq, sk, sv, sg, hq, hf, hi, hg, pv, pg, mq, mg) = jnp.split(
        proj, split_points, axis=-1)

    log_f_fox = jax.nn.log_sigmoid((ff + fox_f_bias).astype(F32)).transpose(0, 2, 1)
    qa = _rms(_split_heads(fq), fox_q_norm)
    ka = _rms(_split_heads(fk), fox_k_norm)
    out_a = _merge_heads(_forgetting_attention(qa, ka, _split_heads(fv), log_f_fox))

    out_b = _merge_heads(_stick_breaking_attention(_split_heads(sq), _split_heads(sk), _split_heads(sv)))

    lb = lower_bound.astype(F32)
    hf32 = hf.astype(F32)
    log_lb = jnp.log(jnp.maximum(lb, LB_FLOOR))
    log_f_h = jnp.logaddexp(log_lb, jnp.log1p(-lb) + jax.nn.log_sigmoid(hf32))
    k_h = (1.0 - lb) * jax.nn.sigmoid(-hf32)
    oc = _hgrn2(_split_heads(jax.nn.silu(hq)), _split_heads(k_h), _split_heads(hi), _split_heads(log_f_h))
    oc = _rms(oc, hgrn_out_norm.reshape(N_HEADS, 1, HEAD_DIM))
    out_c = _merge_heads(oc)

    out_d = _pool_mixer(pv, pool_w, pool_scale)

    out_e = _memory_attention(mq, mem, mem_norm_g, mem_w_kv, mem_q_norm, mem_k_norm)

    mixed = jnp.concatenate([
        out_a.astype(x.dtype) * jax.nn.silu(fg),
        out_b.astype(x.dtype) * jax.nn.silu(sg),
        out_c.astype(x.dtype) * jax.nn.silu(hg),
        out_d.astype(x.dtype) * jax.nn.silu(pg),
        out_e.astype(x.dtype) * jax.nn.silu(mg),
    ], axis=-1)
    return x + jnp.einsum('bsn,nd->bsd', mixed, w_out).astype(x.dtype)


def _fwd_setup_inputs(seed: int = 0) -> dict:
    key = jax.random.key(seed)
    ks = jax.random.split(key, 16)

    def nrm(k, shape, scale):
        return scale * jax.random.normal(k, shape, F32)

    return {
        "x": nrm(ks[0], (BATCH, SEQ, D_MODEL), 1.0),
        "mem": nrm(ks[1], (BATCH, N_MEM, D_MODEL), 1.0),
        "norm_g": 1.0 + nrm(ks[2], (DEPTH, D_MODEL), 0.02),
        "w_in": nrm(ks[3], (DEPTH, D_MODEL, D_IN), D_MODEL ** -0.5),
        "fox_f_bias": nrm(ks[4], (DEPTH, N_HEADS), 0.1),
        "fox_q_norm": 1.0 + nrm(ks[5], (DEPTH, HEAD_DIM), 0.02),
        "fox_k_norm": 1.0 + nrm(ks[6], (DEPTH, HEAD_DIM), 0.02),
        "hgrn_lb_logits": nrm(ks[7], (DEPTH, GROUP_WIDTH), 0.5),
        "hgrn_out_norm": 1.0 + nrm(ks[8], (DEPTH, GROUP_WIDTH), 0.02),
        "pool_w": nrm(ks[9], (DEPTH, len(POOL_WINDOWS), POOL_CH, POOL_CH), POOL_CH ** -0.5),
        "pool_scale": 1.0 + nrm(ks[10], (DEPTH, GROUP_WIDTH), 0.1),
        "mem_norm_g": 1.0 + nrm(ks[11], (DEPTH, D_MODEL), 0.02),
        "mem_w_kv": nrm(ks[12], (DEPTH, D_MODEL, 2 * GROUP_WIDTH), D_MODEL ** -0.5),
        "mem_q_norm": 1.0 + nrm(ks[13], (DEPTH, HEAD_DIM), 0.02),
        "mem_k_norm": 1.0 + nrm(ks[14], (DEPTH, HEAD_DIM), 0.02),
        "w_out": nrm(ks[15], (DEPTH, D_MIX, D_MODEL), D_MIX ** -0.5),
    }


def _fwd_reference(x, mem, norm_g, w_in, fox_f_bias, fox_q_norm, fox_k_norm, hgrn_lb_logits,
              hgrn_out_norm, pool_w, pool_scale, mem_norm_g, mem_w_kv, mem_q_norm,
              mem_k_norm, w_out):
    p = jax.nn.softmax(hgrn_lb_logits.astype(F32), axis=0)
    lower_bounds = jnp.clip(jnp.cumsum(p, axis=0) - p[0:1], 0.0, 1.0 - 1e-6)
    for l in range(DEPTH):
        x = _hybrid_layer(x, mem, norm_g[l], w_in[l], fox_f_bias[l], fox_q_norm[l], fox_k_norm[l],
                          lower_bounds[l], hgrn_out_norm[l], pool_w[l], pool_scale[l],
                          mem_norm_g[l], mem_w_kv[l], mem_q_norm[l], mem_k_norm[l], w_out[l])
    return x


import jax as _jax
import jax.numpy as _jnp

TWIN_FORMAT = 'train_step'
FWD_PARAMS = ['x', 'mem', 'norm_g', 'w_in', 'fox_f_bias', 'fox_q_norm', 'fox_k_norm', 'hgrn_lb_logits', 'hgrn_out_norm', 'pool_w', 'pool_scale', 'mem_norm_g', 'mem_w_kv', 'mem_q_norm', 'mem_k_norm', 'w_out']
TWIN_WEIGHTS = ['norm_g', 'w_in', 'fox_f_bias', 'fox_q_norm', 'fox_k_norm', 'hgrn_lb_logits', 'hgrn_out_norm', 'pool_w', 'pool_scale', 'mem_norm_g', 'mem_w_kv', 'mem_q_norm', 'mem_k_norm', 'w_out']
TWIN_DIFF_INPUT = 'x'
TWIN_INPUTS = ['x', 'mem', 'norm_g', 'w_in', 'fox_f_bias', 'fox_q_norm', 'fox_k_norm', 'hgrn_lb_logits', 'hgrn_out_norm', 'pool_w', 'pool_scale', 'mem_norm_g', 'mem_w_kv', 'mem_q_norm', 'mem_k_norm', 'w_out', 'loss_target', 'm_norm_g', 'm_w_in', 'm_fox_f_bias', 'm_fox_q_norm', 'm_fox_k_norm', 'm_hgrn_lb_logits', 'm_hgrn_out_norm', 'm_pool_w', 'm_pool_scale', 'm_mem_norm_g', 'm_mem_w_kv', 'm_mem_q_norm', 'm_mem_k_norm', 'm_w_out', 'v_norm_g', 'v_w_in', 'v_fox_f_bias', 'v_fox_q_norm', 'v_fox_k_norm', 'v_hgrn_lb_logits', 'v_hgrn_out_norm', 'v_pool_w', 'v_pool_scale', 'v_mem_norm_g', 'v_mem_w_kv', 'v_mem_q_norm', 'v_mem_k_norm', 'v_w_out']
TWIN_OUTPUTS = ['loss', 'grad_x', 'grad_norm_g', 'grad_w_in', 'grad_fox_f_bias', 'grad_fox_q_norm', 'grad_fox_k_norm', 'grad_hgrn_lb_logits', 'grad_hgrn_out_norm', 'grad_pool_w', 'grad_pool_scale', 'grad_mem_norm_g', 'grad_mem_w_kv', 'grad_mem_q_norm', 'grad_mem_k_norm', 'grad_w_out', 'delta_norm_g', 'delta_w_in', 'delta_fox_f_bias', 'delta_fox_q_norm', 'delta_fox_k_norm', 'delta_hgrn_lb_logits', 'delta_hgrn_out_norm', 'delta_pool_w', 'delta_pool_scale', 'delta_mem_norm_g', 'delta_mem_w_kv', 'delta_mem_q_norm', 'delta_mem_k_norm', 'delta_w_out', 'new_m_norm_g', 'new_m_w_in', 'new_m_fox_f_bias', 'new_m_fox_q_norm', 'new_m_fox_k_norm', 'new_m_hgrn_lb_logits', 'new_m_hgrn_out_norm', 'new_m_pool_w', 'new_m_pool_scale', 'new_m_mem_norm_g', 'new_m_mem_w_kv', 'new_m_mem_q_norm', 'new_m_mem_k_norm', 'new_m_w_out', 'new_v_norm_g', 'new_v_w_in', 'new_v_fox_f_bias', 'new_v_fox_q_norm', 'new_v_fox_k_norm', 'new_v_hgrn_lb_logits', 'new_v_hgrn_out_norm', 'new_v_pool_w', 'new_v_pool_scale', 'new_v_mem_norm_g', 'new_v_mem_w_kv', 'new_v_mem_q_norm', 'new_v_mem_k_norm', 'new_v_w_out']
TWIN_LEAF_KINDS = {'loss': 'loss', 'grad_x': 'grad_x', 'grad_norm_g': 'grad_w', 'grad_w_in': 'grad_w', 'grad_fox_f_bias': 'grad_w', 'grad_fox_q_norm': 'grad_w', 'grad_fox_k_norm': 'grad_w', 'grad_hgrn_lb_logits': 'grad_w', 'grad_hgrn_out_norm': 'grad_w', 'grad_pool_w': 'grad_w', 'grad_pool_scale': 'grad_w', 'grad_mem_norm_g': 'grad_w', 'grad_mem_w_kv': 'grad_w', 'grad_mem_q_norm': 'grad_w', 'grad_mem_k_norm': 'grad_w', 'grad_w_out': 'grad_w', 'delta_norm_g': 'delta_w', 'delta_w_in': 'delta_w', 'delta_fox_f_bias': 'delta_w', 'delta_fox_q_norm': 'delta_w', 'delta_fox_k_norm': 'delta_w', 'delta_hgrn_lb_logits': 'delta_w', 'delta_hgrn_out_norm': 'delta_w', 'delta_pool_w': 'delta_w', 'delta_pool_scale': 'delta_w', 'delta_mem_norm_g': 'delta_w', 'delta_mem_w_kv': 'delta_w', 'delta_mem_q_norm': 'delta_w', 'delta_mem_k_norm': 'delta_w', 'delta_w_out': 'delta_w', 'new_m_norm_g': 'new_m', 'new_m_w_in': 'new_m', 'new_m_fox_f_bias': 'new_m', 'new_m_fox_q_norm': 'new_m', 'new_m_fox_k_norm': 'new_m', 'new_m_hgrn_lb_logits': 'new_m', 'new_m_hgrn_out_norm': 'new_m', 'new_m_pool_w': 'new_m', 'new_m_pool_scale': 'new_m', 'new_m_mem_norm_g': 'new_m', 'new_m_mem_w_kv': 'new_m', 'new_m_mem_q_norm': 'new_m', 'new_m_mem_k_norm': 'new_m', 'new_m_w_out': 'new_m', 'new_v_norm_g': 'new_v', 'new_v_w_in': 'new_v', 'new_v_fox_f_bias': 'new_v', 'new_v_fox_q_norm': 'new_v', 'new_v_fox_k_norm': 'new_v', 'new_v_hgrn_lb_logits': 'new_v', 'new_v_hgrn_out_norm': 'new_v', 'new_v_pool_w': 'new_v', 'new_v_pool_scale': 'new_v', 'new_v_mem_norm_g': 'new_v', 'new_v_mem_w_kv': 'new_v', 'new_v_mem_q_norm': 'new_v', 'new_v_mem_k_norm': 'new_v', 'new_v_w_out': 'new_v'}


def _forward(args):
    return _fwd_reference(*[args[k] for k in FWD_PARAMS])


def _output_shape():
    out = _jax.eval_shape(lambda: _forward(_fwd_setup_inputs(0)))
    return out.shape, out.dtype

N_MICROBATCH = 1
ADAM_LR = 0.001
ADAM_B1 = 0.9
ADAM_B2 = 0.999
ADAM_EPS = 1e-08
ADAM_WD = 0.01
ADAM_STEP = 10
PER_EXAMPLE_BATCH_AXIS = {'x': 0, 'mem': 0, 'loss_target': 0}
SHARED_INPUTS = []
_WEIGHT_DTYPES = {'norm_g': _jnp.float32, 'w_in': _jnp.float32, 'fox_f_bias': _jnp.float32, 'fox_q_norm': _jnp.float32, 'fox_k_norm': _jnp.float32, 'hgrn_lb_logits': _jnp.float32, 'hgrn_out_norm': _jnp.float32, 'pool_w': _jnp.float32, 'pool_scale': _jnp.float32, 'mem_norm_g': _jnp.float32, 'mem_w_kv': _jnp.float32, 'mem_q_norm': _jnp.float32, 'mem_k_norm': _jnp.float32, 'w_out': _jnp.float32}
MOMENT_SCALE = {'norm_g': 2.192977e+01, 'w_in': 2.328232e-01, 'fox_f_bias': 9.334033e+01, 'fox_q_norm': 3.059125e+00, 'fox_k_norm': 3.084710e+00, 'hgrn_lb_logits': 3.903215e-02, 'hgrn_out_norm': 1.793547e+01, 'pool_w': 1.290107e+00, 'pool_scale': 1.430878e+01, 'mem_norm_g': 3.489425e-02, 'mem_w_kv': 3.247271e-02, 'mem_q_norm': 6.053505e-01, 'mem_k_norm': 5.991098e-01, 'w_out': 2.901224e-01}


def _to_microbatches(a, axis):
    t = _jnp.moveaxis(a, axis, 0)
    t = t.reshape((N_MICROBATCH, t.shape[0] // N_MICROBATCH) + t.shape[1:])
    return _jnp.moveaxis(t, 1, axis + 1)


def setup_inputs(seed: int = 0) -> dict:
    inp = _fwd_setup_inputs(seed)
    key = _jax.random.fold_in(_jax.random.key(seed), 7919)
    shape, _ = _output_shape()
    out = dict(inp)
    out["loss_target"] = _jax.random.normal(_jax.random.fold_in(key, 0), shape, _jnp.float32)
    for i, name in enumerate(TWIN_WEIGHTS):
        w = inp[name].astype(_jnp.float32)
        if MOMENT_SCALE is None:
            s = _jnp.sqrt(_jnp.mean(_jnp.square(w)) + 1e-30)
        else:
            s = MOMENT_SCALE[name]
        km, kv = _jax.random.split(_jax.random.fold_in(key, i + 1))
        out[name] = w
        out["m_" + name] = s * _jax.random.normal(km, w.shape, _jnp.float32)
        out["v_" + name] = (s * s) * _jax.random.uniform(kv, w.shape, _jnp.float32, 0.5, 1.5)
    if N_MICROBATCH > 1:
        for name, axis in PER_EXAMPLE_BATCH_AXIS.items():
            out[name] = _to_microbatches(out[name], axis)
    return {'x': out['x'], 'mem': out['mem'], 'norm_g': out['norm_g'], 'w_in': out['w_in'], 'fox_f_bias': out['fox_f_bias'], 'fox_q_norm': out['fox_q_norm'], 'fox_k_norm': out['fox_k_norm'], 'hgrn_lb_logits': out['hgrn_lb_logits'], 'hgrn_out_norm': out['hgrn_out_norm'], 'pool_w': out['pool_w'], 'pool_scale': out['pool_scale'], 'mem_norm_g': out['mem_norm_g'], 'mem_w_kv': out['mem_w_kv'], 'mem_q_norm': out['mem_q_norm'], 'mem_k_norm': out['mem_k_norm'], 'w_out': out['w_out'], 'loss_target': out['loss_target'], 'm_norm_g': out['m_norm_g'], 'm_w_in': out['m_w_in'], 'm_fox_f_bias': out['m_fox_f_bias'], 'm_fox_q_norm': out['m_fox_q_norm'], 'm_fox_k_norm': out['m_fox_k_norm'], 'm_hgrn_lb_logits': out['m_hgrn_lb_logits'], 'm_hgrn_out_norm': out['m_hgrn_out_norm'], 'm_pool_w': out['m_pool_w'], 'm_pool_scale': out['m_pool_scale'], 'm_mem_norm_g': out['m_mem_norm_g'], 'm_mem_w_kv': out['m_mem_w_kv'], 'm_mem_q_norm': out['m_mem_q_norm'], 'm_mem_k_norm': out['m_mem_k_norm'], 'm_w_out': out['m_w_out'], 'v_norm_g': out['v_norm_g'], 'v_w_in': out['v_w_in'], 'v_fox_f_bias': out['v_fox_f_bias'], 'v_fox_q_norm': out['v_fox_q_norm'], 'v_fox_k_norm': out['v_fox_k_norm'], 'v_hgrn_lb_logits': out['v_hgrn_lb_logits'], 'v_hgrn_out_norm': out['v_hgrn_out_norm'], 'v_pool_w': out['v_pool_w'], 'v_pool_scale': out['v_pool_scale'], 'v_mem_norm_g': out['v_mem_norm_g'], 'v_mem_w_kv': out['v_mem_w_kv'], 'v_mem_q_norm': out['v_mem_q_norm'], 'v_mem_k_norm': out['v_mem_k_norm'], 'v_w_out': out['v_w_out']}


def _loss(weights, diff, rest, loss_target):
    with _jax.named_scope("forward"):
        args = {**rest, TWIN_DIFF_INPUT: diff, **{k: w.astype(_WEIGHT_DTYPES[k]) for k, w in weights.items()}}
        y = _forward(args)
    with _jax.named_scope("loss_head"):
        err = _jnp.square(y.astype(_jnp.float32) - loss_target)
        return 0.5 * _jnp.sum(_jnp.mean(err, axis=-1)) if err.ndim else 0.5 * err


def _adamw(w, g, m, v):
    m = ADAM_B1 * m + (1.0 - ADAM_B1) * g
    v = ADAM_B2 * v + (1.0 - ADAM_B2) * _jnp.square(g)
    m_hat = m / (1.0 - ADAM_B1 ** ADAM_STEP)
    v_hat = v / (1.0 - ADAM_B2 ** ADAM_STEP)
    delta = -ADAM_LR * (m_hat / (_jnp.sqrt(v_hat) + ADAM_EPS) + ADAM_WD * w)
    return delta, m, v


def reference(x, mem, norm_g, w_in, fox_f_bias, fox_q_norm, fox_k_norm, hgrn_lb_logits, hgrn_out_norm, pool_w, pool_scale, mem_norm_g, mem_w_kv, mem_q_norm, mem_k_norm, w_out, loss_target, m_norm_g, m_w_in, m_fox_f_bias, m_fox_q_norm, m_fox_k_norm, m_hgrn_lb_logits, m_hgrn_out_norm, m_pool_w, m_pool_scale, m_mem_norm_g, m_mem_w_kv, m_mem_q_norm, m_mem_k_norm, m_w_out, v_norm_g, v_w_in, v_fox_f_bias, v_fox_q_norm, v_fox_k_norm, v_hgrn_lb_logits, v_hgrn_out_norm, v_pool_w, v_pool_scale, v_mem_norm_g, v_mem_w_kv, v_mem_q_norm, v_mem_k_norm, v_w_out):
    given = dict(x=x, mem=mem, norm_g=norm_g, w_in=w_in, fox_f_bias=fox_f_bias, fox_q_norm=fox_q_norm, fox_k_norm=fox_k_norm, hgrn_lb_logits=hgrn_lb_logits, hgrn_out_norm=hgrn_out_norm, pool_w=pool_w, pool_scale=pool_scale, mem_norm_g=mem_norm_g, mem_w_kv=mem_w_kv, mem_q_norm=mem_q_norm, mem_k_norm=mem_k_norm, w_out=w_out, loss_target=loss_target, m_norm_g=m_norm_g, m_w_in=m_w_in, m_fox_f_bias=m_fox_f_bias, m_fox_q_norm=m_fox_q_norm, m_fox_k_norm=m_fox_k_norm, m_hgrn_lb_logits=m_hgrn_lb_logits, m_hgrn_out_norm=m_hgrn_out_norm, m_pool_w=m_pool_w, m_pool_scale=m_pool_scale, m_mem_norm_g=m_mem_norm_g, m_mem_w_kv=m_mem_w_kv, m_mem_q_norm=m_mem_q_norm, m_mem_k_norm=m_mem_k_norm, m_w_out=m_w_out, v_norm_g=v_norm_g, v_w_in=v_w_in, v_fox_f_bias=v_fox_f_bias, v_fox_q_norm=v_fox_q_norm, v_fox_k_norm=v_fox_k_norm, v_hgrn_lb_logits=v_hgrn_lb_logits, v_hgrn_out_norm=v_hgrn_out_norm, v_pool_w=v_pool_w, v_pool_scale=v_pool_scale, v_mem_norm_g=v_mem_norm_g, v_mem_w_kv=v_mem_w_kv, v_mem_q_norm=v_mem_q_norm, v_mem_k_norm=v_mem_k_norm, v_w_out=v_w_out)
    weights = {n: given[n] for n in TWIN_WEIGHTS}
    shared = {n: given[n] for n in SHARED_INPUTS}
    per_example = {n: given[n] for n in ['x', 'mem']}
    grad_fn = _jax.value_and_grad(_loss, argnums=(0, 1))

    def one_microbatch(ex, loss_target):
        ex = dict(ex)
        diff = ex.pop(TWIN_DIFF_INPUT)
        return grad_fn(weights, diff, {**shared, **ex}, loss_target)

    if N_MICROBATCH == 1:
        loss, (grad_w, grad_x) = one_microbatch(per_example, given["loss_target"])
    else:
        def body(carry, xs):
            loss_sum, grad_sum = carry
            l_k, (gw_k, gx_k) = one_microbatch(xs[0], xs[1])
            with _jax.named_scope("update"):
                return (loss_sum + l_k, _jax.tree.map(_jnp.add, grad_sum, gw_k)), gx_k

        init = (_jnp.zeros((), _jnp.float32), _jax.tree.map(_jnp.zeros_like, weights))
        (loss, grad_w), grad_x = _jax.lax.scan(body, init, (per_example, given["loss_target"]))
    with _jax.named_scope("update"):
        delta_w, new_m, new_v = {}, {}, {}
        for n in TWIN_WEIGHTS:
            delta_w[n], new_m[n], new_v[n] = _adamw(weights[n], grad_w[n], given["m_" + n], given["v_" + n])
    return (loss, grad_x, *[grad_w[n] for n in TWIN_WEIGHTS], *[delta_w[n] for n in TWIN_WEIGHTS],
            *[new_m[n] for n in TWIN_WEIGHTS], *[new_v[n] for n in TWIN_WEIGHTS])
```

```python
import functools

import jax
import jax.numpy as jnp
from jax import lax
from jax.experimental import pallas as pl
from jax.experimental.pallas import tpu as pltpu

F32 = jnp.float32
BF = jnp.bfloat16
_MMT = BF

D_MODEL = 1024
GW = 256
HD = 64
NH = 4
CH = 64
N_MEM = 256
D_IN = 4100
D_INP = 4224
D_MIX = 1280
EPS = 1e-6
NEG_BIG = -1e30
LB_FLOOR = 1e-30
SCALE = HD ** -0.5
TQ = 256
N_DEV = 8
VMEM_LIMIT_BYTES = 56 * 1024 * 1024

ADAM_LR = 0.001
ADAM_B1 = 0.9
ADAM_B2 = 0.999
ADAM_EPS = 1e-08
ADAM_WD = 0.01
ADAM_STEP = 10

PIECES = (("A", 0, 768), ("B", 768, 768), ("C", 1536, 768), ("D", 2304, 256), ("E", 2560, 256),
          ("G", 2816, 1280), ("F", 4096, 128))
BWD_PIECES = (("A", 0, 768), ("Bq", 768, 256), ("Bk", 1024, 256), ("Bv", 1280, 256), ("C", 1536, 768),
              ("D", 2304, 256), ("E", 2560, 256), ("G", 2816, 1280), ("F", 4096, 128))
_ORIG = dict(fq=(0, 256), fk=(256, 512), fv=(512, 768), fg=(768, 1024), ff=(1024, 1028), sq=(1028, 1284),
             sk=(1284, 1540), sv=(1540, 1796), sg=(1796, 2052), hq=(2052, 2308), hf=(2308, 2564),
             hi=(2564, 2820), hg=(2820, 3076), pv=(3076, 3332), pg=(3332, 3588), mq=(3588, 3844), mg=(3844, 4100))
_PERM_ORDER = ("fq", "fk", "fv", "sq", "sk", "sv", "hq", "hf", "hi", "pv", "mq", "fg", "sg", "hg", "pg", "mg", "ff")
_ORIG_ORDER = ("fq", "fk", "fv", "fg", "ff", "sq", "sk", "sv", "sg", "hq", "hf", "hi", "hg", "pv", "pg", "mq", "mg")


def permute_cols(w):
    parts = [w[..., _ORIG[n][0]:_ORIG[n][1]] for n in _PERM_ORDER]
    parts.append(jnp.zeros(w.shape[:-1] + (D_INP - D_IN,), w.dtype))
    return jnp.concatenate(parts, axis=-1)


def unpermute_cols(g):
    start, off = {}, 0
    for n in _PERM_ORDER:
        start[n] = off
        off += _ORIG[n][1] - _ORIG[n][0]
    return jnp.concatenate([g[..., start[n]:start[n] + _ORIG[n][1] - _ORIG[n][0]] for n in _ORIG_ORDER], axis=-1)


def _cast(a):
    return a.astype(_MMT)


def _dg(a, b, ca, cb):
    return lax.dot_general(a, b, (((ca,), (cb,)), ((), ())), preferred_element_type=F32)


@jax.custom_vjp
def mm(a, b):
    return _dg(_cast(a), _cast(b), 1, 0)


@jax.custom_vjp
def mm_nt(a, b):
    return _dg(_cast(a), _cast(b), 1, 1)


@jax.custom_vjp
def mm_tn(a, b):
    return _dg(_cast(a), _cast(b), 0, 0)


mm.defvjp(lambda a, b: (mm(a, b), (a, b)),
          lambda r, g: (mm_nt(g, r[1]).astype(r[0].dtype), mm_tn(r[0], g).astype(r[1].dtype)))
mm_nt.defvjp(lambda a, b: (mm_nt(a, b), (a, b)),
             lambda r, g: (mm(g, r[1]).astype(r[0].dtype), mm_tn(g, r[0]).astype(r[1].dtype)))
mm_tn.defvjp(lambda a, b: (mm_tn(a, b), (a, b)),
             lambda r, g: (mm_nt(r[1], g).astype(r[0].dtype), mm(r[0], g).astype(r[1].dtype)))


def _split(a):
    hi = a.astype(_MMT)
    lo = (a - hi.astype(F32)).astype(_MMT)
    return hi, lo


@jax.custom_vjp
def xr(a, c, ct):
    hi, lo = _split(a)
    cc = _cast(c)
    return _dg(hi, cc, 1, 0) + _dg(lo, cc, 1, 0)


@jax.custom_vjp
def xl(c, ct, a):
    hi, lo = _split(a)
    cc = _cast(c)
    return _dg(cc, hi, 1, 0) + _dg(cc, lo, 1, 0)


xr.defvjp(lambda a, c, ct: (xr(a, c, ct), (c, ct)),
          lambda r, g: (xr(g, r[1], r[0]), jnp.zeros_like(r[0]), jnp.zeros_like(r[1])))
xl.defvjp(lambda c, ct, a: (xl(c, ct, a), (c, ct)),
          lambda r, g: (jnp.zeros_like(r[0]), jnp.zeros_like(r[1]), xl(r[1], r[0], g)))


def _iota(shape, dim):
    return lax.broadcasted_iota(jnp.int32, shape, dim)


def _hmask(h, n=GW):
    lane = _iota((1, n), 1)
    return ((lane >= h * HD) & (lane < (h + 1) * HD)).astype(F32)


def _bdmask(n=GW):
    return ((_iota((n, n), 0) >> 6) == (_iota((n, n), 1) >> 6)).astype(F32)


def _tri(n, kind="le"):
    r, c = _iota((n, n), 0), _iota((n, n), 1)
    return {"le": c <= r, "ge": c >= r, "gt": c > r, "lt": c < r}[kind].astype(F32)


def _onehot_lane(h, n=128):
    return (_iota((1, n), 1) == h).astype(F32)


def _logsig(x):
    return jnp.minimum(x, 0.0) - jnp.log1p(jnp.exp(-jnp.abs(x)))


def _sigmoid(x):
    return 0.5 * (jnp.tanh(0.5 * x) + 1.0)


def _silu(x):
    return x * _sigmoid(x)


def _rms(x, g):
    return x * lax.rsqrt(jnp.mean(x * x, axis=-1, keepdims=True) + EPS) * g


def _headrms(x, w, bd64):
    ms = xr(x * x, bd64, bd64)
    return x * lax.rsqrt(ms + EPS) * w


def _call(body, name, grid, in_specs, out_specs, out_shape, scratch=()):
    return pl.pallas_call(
        body, name=name, grid=grid, in_specs=in_specs, out_specs=out_specs, out_shape=out_shape,
        scratch_shapes=list(scratch),
        compiler_params=pltpu.CompilerParams(dimension_semantics=("arbitrary",) * len(grid),
                                             vmem_limit_bytes=VMEM_LIMIT_BYTES))


def _sds(shape, dtype=F32):
    return jax.ShapeDtypeStruct(shape, dtype)


def _acc(ref, val, first):
    @pl.when(first)
    def _():
        ref[...] = val

    @pl.when(jnp.logical_not(first))
    def _():
        ref[...] += val


def inproj_fwd(x2, g, w, layer, tag):
    t = x2.shape[0]

    def body(x_ref, g_ref, w_ref, h_ref, *outs):
        hb = _cast(_rms(x_ref[...], g_ref[...]))
        h_ref[...] = hb
        for (_, c0, wd), o in zip(PIECES, outs):
            o[...] = _dg(hb, _cast(w_ref[:, c0:c0 + wd]), 1, 0)

    return _call(
        body, f"inproj_fwd_{tag}", (t // TQ,),
        [pl.BlockSpec((TQ, D_MODEL), lambda i: (i, 0)),
         pl.BlockSpec((None, 1, D_MODEL), lambda i: (layer, 0, 0)),
         pl.BlockSpec((None, D_MODEL, D_INP), lambda i: (layer, 0, 0))],
        [pl.BlockSpec((TQ, D_MODEL), lambda i: (i, 0))] + [pl.BlockSpec((TQ, wd), lambda i: (i, 0)) for _, _, wd in PIECES],
        [_sds((t, D_MODEL), _MMT)] + [_sds((t, wd)) for _, _, wd in PIECES],
    )(x2, g, w)


def inproj_bwd_dx(x2, g, w, dy, dpieces, layer, tag):
    t = x2.shape[0]

    def body(x_ref, g_ref, w_ref, dy_ref, *rest):
        dps, (dx_ref, dg_ref) = rest[:len(BWD_PIECES)], rest[len(BWD_PIECES):]
        dh = None
        for (_, c0, wd), dp in zip(BWD_PIECES, dps):
            part = _dg(_cast(dp[...]), _cast(w_ref[:, c0:c0 + wd]), 1, 1)
            dh = part if dh is None else dh + part
        _, vjp = jax.vjp(_rms, x_ref[...], g_ref[...])
        dx, dg = vjp(dh)
        dx_ref[...] = dy_ref[...] + dx
        _acc(dg_ref, dg, pl.program_id(0) == 0)

    return _call(
        body, f"inproj_bwd_dx_{tag}", (t // TQ,),
        [pl.BlockSpec((TQ, D_MODEL), lambda i: (i, 0)),
         pl.BlockSpec((None, 1, D_MODEL), lambda i: (layer, 0, 0)),
         pl.BlockSpec((None, D_MODEL, D_INP), lambda i: (layer, 0, 0)),
         pl.BlockSpec((TQ, D_MODEL), lambda i: (i, 0))] + [pl.BlockSpec((TQ, wd), lambda i: (i, 0)) for _, _, wd in BWD_PIECES],
        [pl.BlockSpec((TQ, D_MODEL), lambda i: (i, 0)), pl.BlockSpec((1, D_MODEL), lambda i: (0, 0))],
        [_sds((t, D_MODEL)), _sds((1, D_MODEL))],
    )(x2, g, w, dy, *dpieces)


def matmul_tn(a, b, tag, tk=512):
    t, m = a.shape
    n = b.shape[1]
    tn = n if n <= 512 else (256 if n % 256 == 0 else 128)
    if n == 768:
        tn = 384

    def body(a_ref, b_ref, o_ref):
        _acc(o_ref, _dg(_cast(a_ref[...]), _cast(b_ref[...]), 0, 0), pl.program_id(1) == 0)

    return _call(
        body, f"matmul_tn_{tag}", (n // tn, t // tk),
        [pl.BlockSpec((tk, m), lambda j, i: (i, 0)), pl.BlockSpec((tk, tn), lambda j, i: (i, j))],
        pl.BlockSpec((m, tn), lambda j, i: (0, j)),
        _sds((m, n)),
    )(a, b)


def _fox_prep_fn(q, k, ff, qw, kw, bias, carry, bd64, tri, trit, last):
    qn = _headrms(q, qw, bd64)
    kn = _headrms(k, kw, bd64)
    lf = _logsig(ff + bias)
    c = xl(tri, trit, lf) + carry
    return qn, kn, c, jnp.sum(c * last, axis=0, keepdims=True)


def _prep_consts():
    return _bdmask() * (1.0 / HD), _tri(TQ), _tri(TQ, "ge"), (_iota((TQ, 1), 0) == TQ - 1).astype(F32)


def fox_prep_fwd(pa, pf, qw, kw, bias, bl, s, layer, tag):
    nq = s // TQ

    def body(q_ref, k_ref, v_ref, f_ref, qw_ref, kw_ref, b_ref, qn_ref, kn_ref, vb_ref, cq_ref, ck_ref, carry):
        @pl.when(pl.program_id(1) == 0)
        def _():
            carry[...] = jnp.zeros_like(carry)

        qn, kn, c, cl = _fox_prep_fn(q_ref[...], k_ref[...], f_ref[...], qw_ref[...], kw_ref[...], b_ref[...],
                                     carry[...], *_prep_consts())
        carry[...] = cl
        qn_ref[...] = _cast(qn)
        kn_ref[...] = _cast(kn)
        vb_ref[...] = _cast(v_ref[...])
        cq_ref[...] = c
        ck_ref[...] = c.T[0:8, :]

    tok = lambda j: pl.BlockSpec((TQ, GW), lambda b, i: (b * nq + i, j))
    par = lambda n: pl.BlockSpec((None, 1, n), lambda b, i: (layer, 0, 0))
    return _call(
        body, f"fox_prep_fwd_{tag}", (bl, nq),
        [tok(0), tok(1), tok(2), pl.BlockSpec((TQ, 128), lambda b, i: (b * nq + i, 0)), par(GW), par(GW), par(128)],
        [tok(0), tok(0), tok(0), pl.BlockSpec((TQ, 128), lambda b, i: (b * nq + i, 0)),
         pl.BlockSpec((None, 8, TQ), lambda b, i: (b, 0, i))],
        [_sds((bl * s, GW), _MMT)] * 3 + [_sds((bl * s, 128)), _sds((bl, 8, s))],
        [pltpu.VMEM((1, 128), F32)],
    )(pa, pa, pa, pf, qw, kw, bias)


def fox_prep_bwd(pa, pf, qw, kw, bias, cq, dqn, dkn, dv, dck, bl, s, layer, tag):
    nq = s // TQ

    def body(q_ref, k_ref, f_ref, qw_ref, kw_ref, b_ref, cq_ref, cprev_ref, dqn_ref, dkn_ref, dv_ref, dck_ref,
             da_ref, df_ref, dqw_ref, dkw_ref, db_ref, dcarry):
        i = pl.program_id(1)
        first = jnp.logical_and(pl.program_id(0) == 0, i == 0)

        @pl.when(i == 0)
        def _():
            dcarry[...] = jnp.zeros_like(dcarry)

        last = (_iota((TQ, 1), 0) == TQ - 1).astype(F32)
        carry_in = jnp.where(i == nq - 1, 0.0, jnp.sum(cprev_ref[...] * last, axis=0, keepdims=True))
        consts = _prep_consts()
        _, vjp = jax.vjp(lambda *a: _fox_prep_fn(*a, *consts), q_ref[...], k_ref[...], f_ref[...], qw_ref[...],
                         kw_ref[...], b_ref[...], carry_in)
        dc = dck_ref[...].T
        dq, dk, dff, dqw, dkw, dbias, dcin = vjp((dqn_ref[...], dkn_ref[...], dc, dcarry[...]))
        dcarry[...] = dcin
        da_ref[:, 0:GW] = dq
        da_ref[:, GW:2 * GW] = dk
        da_ref[:, 2 * GW:3 * GW] = dv_ref[...]
        df_ref[...] = dff
        _acc(dqw_ref, dqw, first)
        _acc(dkw_ref, dkw, first)
        _acc(db_ref, dbias, first)

    rv = lambda b, i: b * nq + (nq - 1 - i)
    tok = lambda j: pl.BlockSpec((TQ, GW), lambda b, i: (rv(b, i), j))
    tok0 = pl.BlockSpec((TQ, GW), lambda b, i: (rv(b, i), 0))
    t128 = pl.BlockSpec((TQ, 128), lambda b, i: (rv(b, i), 0))
    prev = pl.BlockSpec((TQ, 128), lambda b, i: (jnp.maximum(rv(b, i) - 1, 0), 0))
    par = lambda n: pl.BlockSpec((None, 1, n), lambda b, i: (layer, 0, 0))
    acc = lambda n: pl.BlockSpec((1, n), lambda b, i: (0, 0))
    return _call(
        body, f"fox_prep_bwd_{tag}", (bl, nq),
        [tok(0), tok(1), t128, par(GW), par(GW), par(128), t128, prev, tok0, tok0, tok0,
         pl.BlockSpec((None, 128, TQ), lambda b, i: (b, 0, nq - 1 - i))],
        [pl.BlockSpec((TQ, 3 * GW), lambda b, i: (rv(b, i), 0)), t128, acc(GW), acc(GW), acc(128)],
        [_sds((bl * s, 3 * GW)), _sds((bl * s, 128)), _sds((1, GW)), _sds((1, GW)), _sds((1, 128))],
        [pltpu.VMEM((1, 128), F32)],
    )(pa, pa, pf, qw, kw, bias, cq, cq, dqn, dkn, dv, dck)


def _lane_pick(x, h):
    return jnp.sum(x * _onehot_lane(h), axis=-1, keepdims=True)


def fox_attn_fwd(qn, kn, vb, cq, ck, bl, s, tag):
    nq = s // TQ

    def body(q_ref, k_ref, v_ref, cq_ref, ck_ref, o_ref, lse_ref, acc):
        i = pl.program_id(1)
        q = q_ref[...].astype(F32)
        cq_t = cq_ref[...]
        row = i * TQ + _iota((TQ, 1), 0)
        out = jnp.zeros((TQ, GW), F32)
        lse = jnp.zeros((TQ, 128), F32)
        for h in range(NH):
            hm = _hmask(h)
            qh = _cast(q * hm)
            cqh = _lane_pick(cq_t, h)
            acc[...] = jnp.zeros_like(acc)

            def step(j, ml):
                m, l = ml
                ks = pl.ds(pl.multiple_of(j * TQ, TQ), TQ)
                sc = _dg(qh, k_ref[ks, :], 1, 1) * SCALE + cqh - ck_ref[h:h + 1, ks]
                col = j * TQ + _iota((1, TQ), 1)
                sc = jnp.where(col <= row, sc, NEG_BIG)
                m_new = jnp.maximum(m, jnp.max(sc, axis=-1, keepdims=True))
                alpha = jnp.exp(m - m_new)
                p = jnp.exp(sc - m_new)
                acc[...] = alpha * acc[...] + _dg(_cast(p), v_ref[ks, :], 1, 0)
                return m_new, alpha * l + jnp.sum(p, axis=-1, keepdims=True)

            m, l = lax.fori_loop(0, i + 1, step, (jnp.full((TQ, 1), NEG_BIG, F32), jnp.zeros((TQ, 1), F32)))
            out = out + acc[...] * hm / l
            lse = lse + (m + jnp.log(l)) * _onehot_lane(h)
        o_ref[...] = out
        lse_ref[...] = lse

    tok = pl.BlockSpec((TQ, GW), lambda b, i: (b * nq + i, 0))
    seq = pl.BlockSpec((s, GW), lambda b, i: (b, 0))
    t128 = pl.BlockSpec((TQ, 128), lambda b, i: (b * nq + i, 0))
    return _call(
        body, f"fox_attn_fwd_{tag}", (bl, nq),
        [tok, seq, seq, t128, pl.BlockSpec((None, 8, s), lambda b, i: (b, 0, 0))],
        [tok, t128], [_sds((bl * s, GW)), _sds((bl * s, 128))],
        [pltpu.VMEM((TQ, GW), F32)],
    )(qn, kn, vb, cq, ck)


def fox_attn_bwd(qn, kn, vb, cq, ck, lse, do, bl, s, tag):
    nq = s // TQ

    def body(q_ref, k_ref, v_ref, cq_ref, ck_ref, lse_ref, do_ref, dq_ref, dk_ref, dv_ref, dck_ref, dqa, p_s, dp_s):
        i = pl.program_id(1)

        @pl.when(i == 0)
        def _():
            dk_ref[...] = jnp.zeros_like(dk_ref)
            dv_ref[...] = jnp.zeros_like(dv_ref)
            dck_ref[...] = jnp.zeros_like(dck_ref)

        q = q_ref[...].astype(F32)
        do = do_ref[...]
        cq_t, lse_t = cq_ref[...], lse_ref[...]
        row = i * TQ + _iota((TQ, 1), 0)
        dqa[...] = jnp.zeros_like(dqa)
        for h in range(NH):
            hm = _hmask(h)
            qh = _cast(q * hm)
            doh = _cast(do * hm)
            cqh = _lane_pick(cq_t, h)
            lseh = _lane_pick(lse_t, h)

            def probs(j, delta):
                ks = pl.ds(pl.multiple_of(j * TQ, TQ), TQ)
                sc = _dg(qh, k_ref[ks, :], 1, 1) * SCALE + cqh - ck_ref[h:h + 1, ks]
                col = j * TQ + _iota((1, TQ), 1)
                p = jnp.where(col <= row, jnp.exp(sc - lseh), 0.0)
                dp = _dg(doh, v_ref[ks, :], 1, 1)
                p_s[:, ks] = p
                dp_s[:, ks] = dp
                return delta + jnp.sum(p * dp, axis=-1, keepdims=True)

            delta = lax.fori_loop(0, i + 1, probs, jnp.zeros((TQ, 1), F32))

            def step(j, carry):
                ks = pl.ds(pl.multiple_of(j * TQ, TQ), TQ)
                p = p_s[:, ks]
                ds = p * (dp_s[:, ks] - delta)
                dsb = _cast(ds)
                dqa[...] += _dg(dsb, k_ref[ks, :], 1, 0) * (hm * SCALE)
                dk_ref[ks, :] += _dg(dsb, qh, 0, 0) * SCALE
                dv_ref[ks, :] += _dg(_cast(p), doh, 0, 0)
                dck_ref[h:h + 1, ks] -= jnp.sum(ds, axis=0, keepdims=True)
                return carry

            lax.fori_loop(0, i + 1, step, 0)
        dq_ref[...] = dqa[...]

    tok = pl.BlockSpec((TQ, GW), lambda b, i: (b * nq + i, 0))
    seq = pl.BlockSpec((s, GW), lambda b, i: (b, 0))
    t128 = pl.BlockSpec((TQ, 128), lambda b, i: (b * nq + i, 0))
    return _call(
        body, f"fox_attn_bwd_{tag}", (bl, nq),
        [tok, seq, seq, t128, pl.BlockSpec((None, 8, s), lambda b, i: (b, 0, 0)), t128, tok],
        [tok, seq, seq, pl.BlockSpec((None, 128, s), lambda b, i: (b, 0, 0))],
        [_sds((bl * s, GW)), _sds((bl * s, GW)), _sds((bl * s, GW)), _sds((bl, 128, s))],
        [pltpu.VMEM((TQ, GW), F32), pltpu.VMEM((TQ, s), F32), pltpu.VMEM((TQ, s), F32)],
    )(qn, kn, vb, cq, ck, lse, do)


def _sb_block(qh, kb, row, col, upper, r_carry):
    z = _dg(qh, kb, 1, 1) * SCALE
    valid = col < row
    ls = _logsig(z)
    lom = jnp.where(valid, ls - z, 0.0)
    between = xr(lom, upper, upper) + r_carry
    w = jnp.where(valid, jnp.exp(ls + between), 0.0)
    return z, ls, lom, w, valid


def sb_attn_fwd(pb, bl, s, tag):
    nq = s // TQ

    def body(q_ref, k_ref, v_ref, o_ref, acc):
        i = pl.program_id(1)
        q = q_ref[...]
        row = i * TQ + _iota((TQ, 1), 0)
        upper = _tri(TQ, "lt")
        out = jnp.zeros((TQ, GW), F32)
        for h in range(NH):
            hm = _hmask(h)
            qh = _cast(q * hm)
            acc[...] = jnp.zeros_like(acc)

            def step(jj, r):
                j = i - jj
                ks = pl.ds(pl.multiple_of(j * TQ, TQ), TQ)
                col = j * TQ + _iota((1, TQ), 1)
                _, _, lom, w, _ = _sb_block(qh, _cast(k_ref[ks, :]), row, col, upper, r)
                acc[...] += _dg(_cast(w), _cast(v_ref[ks, :]), 1, 0)
                return r + jnp.sum(lom, axis=-1, keepdims=True)

            lax.fori_loop(0, i + 1, step, jnp.zeros((TQ, 1), F32))
            out = out + acc[...] * hm
        o_ref[...] = out

    tok = lambda j: pl.BlockSpec((TQ, GW), lambda b, i: (b * nq + i, j))
    seq = lambda j: pl.BlockSpec((s, GW), lambda b, i: (b, j))
    return _call(
        body, f"sb_attn_fwd_{tag}", (bl, nq), [tok(0), seq(1), seq(2)],
        pl.BlockSpec((TQ, GW), lambda b, i: (b * nq + i, 0)), _sds((bl * s, GW)),
        [pltpu.VMEM((TQ, GW), F32)],
    )(pb, pb, pb)


def sb_attn_bwd(pb, do, bl, s, tag):
    nq = s // TQ

    def body(q_ref, k_ref, v_ref, do_ref, dq_ref, dk_ref, dv_ref, dqa, ls_s, lom_s, w_s, g_s):
        i = pl.program_id(1)

        @pl.when(i == 0)
        def _():
            dk_ref[...] = jnp.zeros_like(dk_ref)
            dv_ref[...] = jnp.zeros_like(dv_ref)

        q = q_ref[...]
        do = do_ref[...]
        row = i * TQ + _iota((TQ, 1), 0)
        upper = _tri(TQ, "lt")
        before = _tri(TQ, "gt")
        dqa[...] = jnp.zeros_like(dqa)
        for h in range(NH):
            hm = _hmask(h)
            qh = _cast(q * hm)
            doh = _cast(do * hm)

            def weights(jj, r):
                j = i - jj
                ks = pl.ds(pl.multiple_of(j * TQ, TQ), TQ)
                col = j * TQ + _iota((1, TQ), 1)
                _, ls, lom, w, _ = _sb_block(qh, _cast(k_ref[ks, :]), row, col, upper, r)
                ls_s[:, ks] = ls
                lom_s[:, ks] = lom
                w_s[:, ks] = w
                g_s[:, ks] = _dg(doh, _cast(v_ref[ks, :]), 1, 1) * w
                return r + jnp.sum(lom, axis=-1, keepdims=True)

            lax.fori_loop(0, i + 1, weights, jnp.zeros((TQ, 1), F32))

            def step(j, cpre):
                ks = pl.ds(pl.multiple_of(j * TQ, TQ), TQ)
                col = j * TQ + _iota((1, TQ), 1)
                g = g_s[:, ks]
                pre = cpre + xr(g, before, before)
                dz = jnp.where(col < row, g * jnp.exp(lom_s[:, ks]) - jnp.exp(ls_s[:, ks]) * pre, 0.0)
                dzb = _cast(dz)
                dqa[...] += _dg(dzb, _cast(k_ref[ks, :]), 1, 0) * (hm * SCALE)
                dk_ref[ks, :] += _dg(dzb, qh, 0, 0) * SCALE
                dv_ref[ks, :] += _dg(_cast(w_s[:, ks]), doh, 0, 0)
                return cpre + jnp.sum(g, axis=-1, keepdims=True)

            lax.fori_loop(0, i + 1, step, jnp.zeros((TQ, 1), F32))
        dq_ref[...] = dqa[...]

    tok = lambda j: pl.BlockSpec((TQ, GW), lambda b, i: (b * nq + i, j))
    seq = lambda j: pl.BlockSpec((s, GW), lambda b, i: (b, j))
    return _call(
        body, f"sb_attn_bwd_{tag}", (bl, nq), [tok(0), seq(1), seq(2), tok(0)],
        [tok(0), seq(0), seq(0)], [_sds((bl * s, GW))] * 3,
        [pltpu.VMEM((TQ, GW), F32)] + [pltpu.VMEM((TQ, s), F32)] * 4,
    )(pb, pb, pb, do)


def _hgrn_consts():
    r, c = _iota((CH, CH), 0), _iota((CH, CH), 1)
    rr = _iota((CH, 1), 0)
    tri = (c <= r).astype(F32)
    lv = []
    for m in (8, 4, 2, 1):
        up = ((rr & (2 * m - 1)) >= m).astype(F32)
        lo = 1.0 - up
        selq = (((r & (2 * m - 1)) >= m) & (c == (r & ~(m - 1)) - 1)).astype(F32)
        selk = (((r & (2 * m - 1)) < m) & (c == (r & ~(m - 1)) + m - 1)).astype(F32)
        pm = (((r & ~(2 * m - 1)) == (c & ~(2 * m - 1))) & ((r & (2 * m - 1)) >= m) & ((c & (2 * m - 1)) < m)).astype(F32)
        lv.append((up, lo, selq, selq.T, selk, selk.T, pm))
    return dict(tri=tri, trit=tri.T, rr=rr, lv=lv, bd=_bdmask(), bd64=_bdmask() * (1.0 / HD),
                hm=[_hmask(h) for h in range(NH)])


def _hgrn_chunk_fn(hq, hf, hi, lb, wn, st, cs):
    q = _silu(hq)
    log_lb = jnp.log(jnp.maximum(lb, LB_FLOOR))
    a, bb = log_lb, jnp.log1p(-lb) + _logsig(hf)
    g = jnp.maximum(a, bb) + jnp.log1p(jnp.exp(-jnp.abs(a - bb)))
    k = (1.0 - lb) * _sigmoid(-hf)
    v = hi
    rr = cs["rr"]
    b = xl(cs["tri"], cs["trit"], g)
    row_of = lambda n: jnp.sum(b * (rr == n).astype(F32), axis=0, keepdims=True)
    o = mm_nt(q * jnp.exp(b), st)
    qs, ks = [], []
    for ib in (1, 2, 3):
        ref = row_of(16 * ib - 1)
        inq = ((rr >= 16 * ib) & (rr < 16 * ib + 16)).astype(F32)
        ink = (rr < 16 * ib).astype(F32)
        qs.append(q * jnp.exp((b - ref) * inq) * inq)
        ks.append(k * jnp.exp((ref - b) * ink) * ink)
    qcat, kcat = jnp.concatenate(qs, axis=1), jnp.concatenate(ks, axis=1)
    lvl = []
    for up, lo, selq, selqt, selk, selkt, pm in cs["lv"]:
        qe = q * jnp.exp((b - xl(selq, selqt, b)) * up) * up
        ke = k * jnp.exp((xl(selk, selkt, b) - b) * lo) * lo
        lvl.append((qe, ke, pm))
    for h in range(NH):
        hm = cs["hm"][h]
        a_h = mm_nt(qcat * jnp.concatenate([hm] * 3, axis=1), kcat)
        for qe, ke, pm in lvl:
            a_h = a_h + mm_nt(qe * hm, ke) * pm
        o = o + mm(a_h, v * hm)
    o = o + xr(q * k, cs["bd"], cs["bd"]) * v
    b_last = row_of(CH - 1)
    st_new = st * jnp.exp(b_last) + mm_tn(v, k * jnp.exp(b_last - b)) * cs["bd"]
    return _headrms(o, wn, cs["bd64"]), st_new


def hgrn_fwd(pc, lb, wn, bl, s, layer, tag):
    nc = s // CH

    def body(q_ref, f_ref, i_ref, lb_ref, wn_ref, o_ref, st_ref, st):
        @pl.when(pl.program_id(1) == 0)
        def _():
            st[...] = jnp.zeros_like(st)

        st_ref[...] = st[...]
        o, st_new = _hgrn_chunk_fn(q_ref[...], f_ref[...], i_ref[...], lb_ref[...], wn_ref[...], st[...], _hgrn_consts())
        o_ref[...] = o
        st[...] = st_new

    tok = lambda j: pl.BlockSpec((CH, GW), lambda b, c: (b * nc + c, j))
    par = pl.BlockSpec((None, 1, GW), lambda b, c: (layer, 0, 0))
    return _call(
        body, f"hgrn_fwd_{tag}", (bl, nc), [tok(0), tok(1), tok(2), par, par],
        [tok(0), pl.BlockSpec((None, None, GW, GW), lambda b, c: (b, c, 0, 0))],
        [_sds((bl * s, GW)), _sds((bl, nc, GW, GW))],
        [pltpu.VMEM((GW, GW), F32)],
    )(pc, pc, pc, lb, wn)


def hgrn_bwd(pc, lb, wn, states, do, bl, s, layer, tag):
    nc = s // CH

    def body(q_ref, f_ref, i_ref, lb_ref, wn_ref, st_ref, do_ref, dc_ref, dlb_ref, dwn_ref, dst):
        c = pl.program_id(1)
        first = jnp.logical_and(pl.program_id(0) == 0, c == 0)

        @pl.when(c == 0)
        def _():
            dst[...] = jnp.zeros_like(dst)

        cs = _hgrn_consts()
        _, vjp = jax.vjp(lambda *a: _hgrn_chunk_fn(*a, cs), q_ref[...], f_ref[...], i_ref[...], lb_ref[...],
                         wn_ref[...], st_ref[...])
        dq, df, di, dlb, dwn, dst_in = vjp((do_ref[...], dst[...]))
        dst[...] = dst_in
        dc_ref[:, 0:GW] = dq
        dc_ref[:, GW:2 * GW] = df
        dc_ref[:, 2 * GW:3 * GW] = di
        _acc(dlb_ref, dlb, first)
        _acc(dwn_ref, dwn, first)

    rv = lambda b, c: b * nc + (nc - 1 - c)
    tok = lambda j: pl.BlockSpec((CH, GW), lambda b, c: (rv(b, c), j))
    par = pl.BlockSpec((None, 1, GW), lambda b, c: (layer, 0, 0))
    acc = pl.BlockSpec((1, GW), lambda b, c: (0, 0))
    return _call(
        body, f"hgrn_bwd_{tag}", (bl, nc),
        [tok(0), tok(1), tok(2), par, par, pl.BlockSpec((None, None, GW, GW), lambda b, c: (b, nc - 1 - c, 0, 0)), tok(0)],
        [pl.BlockSpec((CH, 3 * GW), lambda b, c: (rv(b, c), 0)), acc, acc],
        [_sds((bl * s, 3 * GW)), _sds((1, GW)), _sds((1, GW))],
        [pltpu.VMEM((GW, GW), F32)],
    )(pc, pc, pc, lb, wn, states, do)


def _shift_rows(x, k, up):
    n = x.shape[0]
    rr = _iota((n, 1), 0)
    if up:
        return jnp.where(rr < n - k, pltpu.roll(x, n - k, 0), 0.0)
    return jnp.where(rr >= k, pltpu.roll(x, k, 0), 0.0)


def _window_sums(x, up):
    s2 = x + _shift_rows(x, 1, up)
    s4 = s2 + _shift_rows(s2, 2, up)
    s8 = s4 + _shift_rows(s4, 4, up)
    s16 = s8 + _shift_rows(s8, 8, up)
    return s2, s4, s8, s16


def _pool_div(n):
    pos = (_iota((n, 1), 0) + 1).astype(F32)
    return [jnp.minimum(pos, float(w)) for w in (2, 4, 8, 16)]


def _pool_mix(sums, scaled):
    out = None
    for gi, sw in enumerate(sums):
        part = (sw if scaled is None else sw / scaled[gi]) * _hmask(gi)
        out = part if out is None else out + part
    return out


def pool_fwd(pd, wbd, scale, bl, s, layer, tag):
    def body(u_ref, w_ref, sc_ref, o_ref):
        u = u_ref[...]
        pm = _pool_mix(_window_sums(u, False), _pool_div(s)) - u
        o_ref[...] = _dg(_cast(pm), _cast(w_ref[...]), 1, 0) * sc_ref[...]

    seq = pl.BlockSpec((s, GW), lambda b: (b, 0))
    return _call(
        body, f"pool_fwd_{tag}", (bl,),
        [seq, pl.BlockSpec((None, GW, GW), lambda b: (layer, 0, 0)), pl.BlockSpec((None, 1, GW), lambda b: (layer, 0, 0))],
        seq, _sds((bl * s, GW)),
    )(pd, wbd, scale)


def pool_bwd(pd, wbd, scale, do, bl, s, layer, tag):
    def body(u_ref, w_ref, sc_ref, do_ref, du_ref, dw_ref, dsc_ref):
        first = pl.program_id(0) == 0
        u, do = u_ref[...], do_ref[...]
        div = _pool_div(s)
        pm = _pool_mix(_window_sums(u, False), div) - u
        ypre = _dg(_cast(pm), _cast(w_ref[...]), 1, 0)
        dys = do * sc_ref[...]
        _acc(dsc_ref, jnp.sum(do * ypre, axis=0, keepdims=True), first)
        _acc(dw_ref, _dg(_cast(pm), _cast(dys), 0, 0), first)
        dpm = _dg(_cast(dys), _cast(w_ref[...]), 1, 1)
        dsc = [dpm / d for d in div]
        adj = None
        for gi in range(4):
            part = _window_sums(dsc[gi] * _hmask(gi), True)[gi]
            adj = part if adj is None else adj + part
        du_ref[...] = adj - dpm

    seq = pl.BlockSpec((s, GW), lambda b: (b, 0))
    return _call(
        body, f"pool_bwd_{tag}", (bl,),
        [seq, pl.BlockSpec((None, GW, GW), lambda b: (layer, 0, 0)), pl.BlockSpec((None, 1, GW), lambda b: (layer, 0, 0)), seq],
        [seq, pl.BlockSpec((GW, GW), lambda b: (0, 0)), pl.BlockSpec((1, GW), lambda b: (0, 0))],
        [_sds((bl * s, GW)), _sds((GW, GW)), _sds((1, GW))],
    )(pd, wbd, scale, do)


def _mem_prep_fn(mem, g, wk, wv, kw, bd64):
    mn = _rms(mem, g)
    return _headrms(mm(mn, wk), kw, bd64), mm(mn, wv)


def mem_prep_fwd(mem2, g, wkv, kw, bl, layer, tag):
    def body(m_ref, g_ref, wk_ref, wv_ref, kw_ref, k_ref, v_ref):
        k, v = _mem_prep_fn(m_ref[...], g_ref[...], wk_ref[...], wv_ref[...], kw_ref[...], _bdmask() * (1.0 / HD))
        k_ref[...] = k
        v_ref[...] = v

    blk = pl.BlockSpec((N_MEM, GW), lambda b: (b, 0))
    return _call(
        body, f"mem_prep_fwd_{tag}", (bl,),
        [pl.BlockSpec((N_MEM, D_MODEL), lambda b: (b, 0)), pl.BlockSpec((None, 1, D_MODEL), lambda b: (layer, 0, 0)),
         pl.BlockSpec((None, D_MODEL, GW), lambda b: (layer, 0, 0)), pl.BlockSpec((None, D_MODEL, GW), lambda b: (layer, 0, 1)),
         pl.BlockSpec((None, 1, GW), lambda b: (layer, 0, 0))],
        [blk, blk], [_sds((bl * N_MEM, GW))] * 2,
    )(mem2, g, wkv, wkv, kw)


def mem_prep_bwd(mem2, g, wkv, kw, dk, dv, bl, layer, tag):
    def body(m_ref, g_ref, wk_ref, wv_ref, kw_ref, dk_ref, dv_ref, dwk_ref, dwv_ref, dg_ref, dkw_ref):
        first = pl.program_id(0) == 0
        bd64 = _bdmask() * (1.0 / HD)
        _, vjp = jax.vjp(lambda g_, wk, wv, kw_: _mem_prep_fn(m_ref[...], g_, wk, wv, kw_, bd64),
                         g_ref[...], wk_ref[...].astype(F32), wv_ref[...].astype(F32), kw_ref[...])
        dg, dwk, dwv, dkw = vjp((dk_ref[...], dv_ref[...]))
        _acc(dwk_ref, dwk, first)
        _acc(dwv_ref, dwv, first)
        _acc(dg_ref, dg, first)
        _acc(dkw_ref, dkw, first)

    blk = pl.BlockSpec((N_MEM, GW), lambda b: (b, 0))
    return _call(
        body, f"mem_prep_bwd_{tag}", (bl,),
        [pl.BlockSpec((N_MEM, D_MODEL), lambda b: (b, 0)), pl.BlockSpec((None, 1, D_MODEL), lambda b: (layer, 0, 0)),
         pl.BlockSpec((None, D_MODEL, GW), lambda b: (layer, 0, 0)), pl.BlockSpec((None, D_MODEL, GW), lambda b: (layer, 0, 1)),
         pl.BlockSpec((None, 1, GW), lambda b: (layer, 0, 0)), blk, blk],
        [pl.BlockSpec((D_MODEL, GW), lambda b: (0, 0)), pl.BlockSpec((D_MODEL, GW), lambda b: (0, 0)),
         pl.BlockSpec((1, D_MODEL), lambda b: (0, 0)), pl.BlockSpec((1, GW), lambda b: (0, 0))],
        [_sds((D_MODEL, GW)), _sds((D_MODEL, GW)), _sds((1, D_MODEL)), _sds((1, GW))],
    )(mem2, g, wkv, wkv, kw, dk, dv)


def _mem_attn_fn(mq, qw, k, v, bd64):
    qn = _headrms(mq, qw, bd64)
    out = None
    for h in range(NH):
        hm = _hmask(h)
        lg = mm_nt(qn * hm, k) * SCALE
        e = jnp.exp(lg - lax.stop_gradient(jnp.max(lg, axis=-1, keepdims=True)))
        p = e / jnp.sum(e, axis=-1, keepdims=True)
        part = mm(p, v) * hm
        out = part if out is None else out + part
    return out


def mem_attn_fwd(pe, qw, k, v, bl, s, layer, tag):
    nq = s // TQ

    def body(q_ref, qw_ref, k_ref, v_ref, o_ref):
        o_ref[...] = _mem_attn_fn(q_ref[...], qw_ref[...], k_ref[...], v_ref[...], _bdmask() * (1.0 / HD))

    tok = pl.BlockSpec((TQ, GW), lambda b, i: (b * nq + i, 0))
    kv = pl.BlockSpec((N_MEM, GW), lambda b, i: (b, 0))
    return _call(
        body, f"mem_attn_fwd_{tag}", (bl, nq), [tok, pl.BlockSpec((None, 1, GW), lambda b, i: (layer, 0, 0)), kv, kv],
        tok, _sds((bl * s, GW)),
    )(pe, qw, k, v)


def mem_attn_bwd(pe, qw, k, v, do, bl, s, layer, tag):
    nq = s // TQ

    def body(q_ref, qw_ref, k_ref, v_ref, do_ref, dq_ref, dk_ref, dv_ref, dqw_ref):
        i = pl.program_id(1)
        bd64 = _bdmask() * (1.0 / HD)
        _, vjp = jax.vjp(lambda *a: _mem_attn_fn(*a, bd64), q_ref[...], qw_ref[...], k_ref[...], v_ref[...])
        dq, dqw, dk, dv = vjp(do_ref[...])
        dq_ref[...] = dq
        _acc(dk_ref, dk, i == 0)
        _acc(dv_ref, dv, i == 0)
        _acc(dqw_ref, dqw, jnp.logical_and(pl.program_id(0) == 0, i == 0))

    tok = pl.BlockSpec((TQ, GW), lambda b, i: (b * nq + i, 0))
    kv = pl.BlockSpec((N_MEM, GW), lambda b, i: (b, 0))
    return _call(
        body, f"mem_attn_bwd_{tag}", (bl, nq),
        [tok, pl.BlockSpec((None, 1, GW), lambda b, i: (layer, 0, 0)), kv, kv, tok],
        [tok, kv, kv, pl.BlockSpec((1, GW), lambda b, i: (0, 0))],
        [_sds((bl * s, GW)), _sds((bl * N_MEM, GW)), _sds((bl * N_MEM, GW)), _sds((1, GW))],
    )(pe, qw, k, v, do)


def _gate_out_fn(outs, gates, wparts):
    y = None
    for o, g, w in zip(outs, gates, wparts):
        part = mm(o * _silu(g), w)
        y = part if y is None else y + part
    return y


def outproj_fwd(x2, outs, pg, wout, layer, tag):
    t = x2.shape[0]

    def body(x_ref, oa, ob, oc, od, oe, g_ref, w_ref, y_ref):
        outs_ = [r[...] for r in (oa, ob, oc, od, oe)]
        gates = [g_ref[:, j * GW:(j + 1) * GW] for j in range(5)]
        wparts = [w_ref[j * GW:(j + 1) * GW, :] for j in range(5)]
        y_ref[...] = x_ref[...] + _gate_out_fn(outs_, gates, wparts)

    tok = pl.BlockSpec((TQ, GW), lambda i: (i, 0))
    big = pl.BlockSpec((TQ, D_MODEL), lambda i: (i, 0))
    return _call(
        body, f"outproj_fwd_{tag}", (t // TQ,),
        [big] + [tok] * 5 + [pl.BlockSpec((TQ, D_MIX), lambda i: (i, 0)),
                            pl.BlockSpec((None, D_MIX, D_MODEL), lambda i: (layer, 0, 0))],
        big, _sds((t, D_MODEL)),
    )(x2, *outs, pg, wout)


def outproj_bwd(outs, pg, wout, dy, layer, tag):
    t = dy.shape[0]

    def body(oa, ob, oc, od, oe, g_ref, w_ref, dy_ref, da, db, dc, dd, de, dg_ref, dw_ref):
        outs_ = [r[...] for r in (oa, ob, oc, od, oe)]
        gates = [g_ref[:, j * GW:(j + 1) * GW] for j in range(5)]
        wparts = [w_ref[j * GW:(j + 1) * GW, :].astype(F32) for j in range(5)]
        _, vjp = jax.vjp(_gate_out_fn, outs_, gates, wparts)
        douts, dgates, dws = vjp(dy_ref[...])
        for r, val in zip((da, db, dc, dd, de), douts):
            r[...] = val
        first = pl.program_id(0) == 0
        for j in range(5):
            dg_ref[:, j * GW:(j + 1) * GW] = dgates[j]

        @pl.when(first)
        def _():
            for j in range(5):
                dw_ref[j * GW:(j + 1) * GW, :] = dws[j]

        @pl.when(jnp.logical_not(first))
        def _():
            for j in range(5):
                dw_ref[j * GW:(j + 1) * GW, :] += dws[j]

    tok = pl.BlockSpec((TQ, GW), lambda i: (i, 0))
    return _call(
        body, f"outproj_bwd_{tag}", (t // TQ,),
        [tok] * 5 + [pl.BlockSpec((TQ, D_MIX), lambda i: (i, 0)), pl.BlockSpec((None, D_MIX, D_MODEL), lambda i: (layer, 0, 0)),
                     pl.BlockSpec((TQ, D_MODEL), lambda i: (i, 0))],
        [tok] * 5 + [pl.BlockSpec((TQ, D_MIX), lambda i: (i, 0)), pl.BlockSpec((D_MIX, D_MODEL), lambda i: (0, 0))],
        [_sds((t, GW))] * 5 + [_sds((t, D_MIX)), _sds((D_MIX, D_MODEL))],
    )(*outs, pg, wout, dy)


def loss_head(y, tgt):
    t = y.shape[0]

    def body(y_ref, t_ref, l_ref, dy_ref):
        diff = y_ref[...] - t_ref[...]
        dy_ref[...] = diff * (1.0 / D_MODEL)
        part = 0.5 * jnp.sum(jnp.sum(diff * diff, axis=-1, keepdims=True) * (1.0 / D_MODEL), axis=0, keepdims=True)
        _acc(l_ref, jnp.broadcast_to(part, (8, 128)), pl.program_id(0) == 0)

    big = pl.BlockSpec((TQ, D_MODEL), lambda i: (i, 0))
    return _call(body, "loss_head", (t // TQ,), [big, big], [pl.BlockSpec((8, 128), lambda i: (0, 0)), big],
                 [_sds((8, 128)), _sds((t, D_MODEL))])(y, tgt)


def layer_fwd(x2, mem2, p, layer, bl, s):
    tag = f"l{layer}"
    h, pa, pb, pc, pd, pe, pg, pf = inproj_fwd(x2, p["norm_g"], p["w_in"], layer, tag)
    qn, kn, vb, cq, ck = fox_prep_fwd(pa, pf, p["fox_q_norm"], p["fox_k_norm"], p["fox_f_bias"], bl, s, layer, tag)
    oa, lse = fox_attn_fwd(qn, kn, vb, cq, ck, bl, s, tag)
    ob = sb_attn_fwd(pb, bl, s, tag)
    oc, states = hgrn_fwd(pc, p["lb"], p["hgrn_out_norm"], bl, s, layer, tag)
    od = pool_fwd(pd, p["pool_wbd"], p["pool_scale"], bl, s, layer, tag)
    mk, mv = mem_prep_fwd(mem2, p["mem_norm_g"], p["mem_w_kv"], p["mem_k_norm"], bl, layer, tag)
    oe = mem_attn_fwd(pe, p["mem_q_norm"], mk, mv, bl, s, layer, tag)
    y = outproj_fwd(x2, (oa, ob, oc, od, oe), pg, p["w_out"], layer, tag)
    saved = dict(x2=x2, h=h, pa=pa, pb=pb, pc=pc, pd=pd, pe=pe, pg=pg, pf=pf, qn=qn, kn=kn, vb=vb, cq=cq, ck=ck,
                 oa=oa, lse=lse, ob=ob, oc=oc, states=states, od=od, mk=mk, mv=mv, oe=oe)
    return y, saved


def layer_bwd(dy, mem2, p, sv, layer, bl, s):
    tag = f"l{layer}"
    (doa, dob, doc, dod, doe, dg_gates, dwout) = outproj_bwd((sv["oa"], sv["ob"], sv["oc"], sv["od"], sv["oe"]), sv["pg"],
                                                              p["w_out"], dy, layer, tag)
    dqn, dkn, dv, dck = fox_attn_bwd(sv["qn"], sv["kn"], sv["vb"], sv["cq"], sv["ck"], sv["lse"], doa, bl, s, tag)
    d_a, d_f, dqw, dkw, dbias = fox_prep_bwd(sv["pa"], sv["pf"], p["fox_q_norm"], p["fox_k_norm"], p["fox_f_bias"], sv["cq"],
                                             dqn, dkn, dv, dck, bl, s, layer, tag)
    dsq, dsk, dsv = sb_attn_bwd(sv["pb"], dob, bl, s, tag)
    d_c, dlb, dwn = hgrn_bwd(sv["pc"], p["lb"], p["hgrn_out_norm"], sv["states"], doc, bl, s, layer, tag)
    d_d, dwbd, dpscale = pool_bwd(sv["pd"], p["pool_wbd"], p["pool_scale"], dod, bl, s, layer, tag)
    d_e, dmk, dmv, dmqw = mem_attn_bwd(sv["pe"], p["mem_q_norm"], sv["mk"], sv["mv"], doe, bl, s, layer, tag)
    dwk, dwv, dmg, dmkw = mem_prep_bwd(mem2, p["mem_norm_g"], p["mem_w_kv"], p["mem_k_norm"], dmk, dmv, bl, layer, tag)
    dpieces = (d_a, dsq, dsk, dsv, d_c, d_d, d_e, dg_gates, d_f)
    dx, dng = inproj_bwd_dx(sv["x2"], p["norm_g"], p["w_in"], dy, dpieces, layer, tag)
    dwin = jnp.concatenate([matmul_tn(sv["h"], dp, f"{tag}_{nm}") for (nm, _, _), dp in zip(BWD_PIECES, dpieces)], axis=1)
    grads = dict(norm_g=dng, w_in=dwin, fox_f_bias=dbias, fox_q_norm=dqw, fox_k_norm=dkw, lb=dlb, hgrn_out_norm=dwn,
                 pool_wbd=dwbd, pool_scale=dpscale, mem_norm_g=dmg, mem_w_kv=jnp.concatenate([dwk, dwv], axis=1),
                 mem_q_norm=dmqw, mem_k_norm=dmkw, w_out=dwout)
    return dx, grads


def _tile4(w):
    return jnp.tile(w, (1, NH))[:, None, :]


def prepare_params(norm_g, w_in_p, fox_f_bias, fox_q_norm, fox_k_norm, hgrn_lb_logits, hgrn_out_norm, pool_w, pool_scale,
                   mem_norm_g, mem_w_kv, mem_q_norm, mem_k_norm, w_out):
    p1 = jax.nn.sigmoid(hgrn_lb_logits[1] - hgrn_lb_logits[0])
    lb = jnp.stack([jnp.zeros_like(p1), jnp.clip(p1, 0.0, 1.0 - 1e-6)])
    eye = jnp.eye(4, dtype=F32)
    wbd = jnp.einsum("lgcd,gh->lgchd", pool_w, eye).reshape(2, GW, GW)
    return dict(norm_g=norm_g[:, None, :], w_in=w_in_p, fox_f_bias=jnp.pad(fox_f_bias, ((0, 0), (0, 124)))[:, None, :],
                fox_q_norm=_tile4(fox_q_norm), fox_k_norm=_tile4(fox_k_norm), lb=lb[:, None, :],
                hgrn_out_norm=hgrn_out_norm[:, None, :], pool_wbd=wbd, pool_scale=pool_scale[:, None, :],
                mem_norm_g=mem_norm_g[:, None, :], mem_w_kv=mem_w_kv, mem_q_norm=_tile4(mem_q_norm),
                mem_k_norm=_tile4(mem_k_norm), w_out=w_out)


def local_step(x, mem, tgt, p):
    bl, s, _ = x.shape
    x2, mem2, tgt2 = x.reshape(bl * s, D_MODEL), mem.reshape(bl * N_MEM, D_MODEL), tgt.reshape(bl * s, D_MODEL)
    y0, sv0 = layer_fwd(x2, mem2, p, 0, bl, s)
    y1, sv1 = layer_fwd(y0, mem2, p, 1, bl, s)
    lpart, dy = loss_head(y1, tgt2)
    dx1, g1 = layer_bwd(dy, mem2, p, sv1, 1, bl, s)
    dx0, g0 = layer_bwd(dx1, mem2, p, sv0, 0, bl, s)
    return lpart[0, 0], dx0.reshape(bl, s, D_MODEL), g0, g1


_ANY = pl.BlockSpec(memory_space=pl.ANY)


def _me_and_peers():
    x, y, c = lax.axis_index("x"), lax.axis_index("y"), lax.axis_index("c")
    peers = []
    for k in range(1, N_DEV):
        px = 1 - x if (k >> 2) & 1 else x
        py = 1 - y if (k >> 1) & 1 else y
        pc = 1 - c if k & 1 else c
        peers.append(((px, py, pc), 4 * px + 2 * py + pc))
    return 4 * x + 2 * y + c, peers


def all_gather_rows(xs, tag):
    nl, r, c = xs.shape

    def body(x_ref, o_ref, send_sems, recv_sems, local_sem):
        me, peers = _me_and_peers()
        rows = lambda idx: o_ref.at[:, pl.ds(idx * r, r), :]
        mine = pltpu.make_async_copy(x_ref, rows(me), local_sem)
        mine.start()
        copies = [pltpu.make_async_remote_copy(src_ref=x_ref, dst_ref=rows(me), send_sem=send_sems.at[k], recv_sem=recv_sems.at[k],
                                               device_id=dev, device_id_type=pl.DeviceIdType.MESH)
                  for k, (dev, _) in enumerate(peers)]
        for cp in copies:
            cp.start()
        for cp in copies:
            cp.wait()
        mine.wait()

    return pl.pallas_call(
        body, name=f"all_gather_{tag}", in_specs=[_ANY], out_specs=_ANY, out_shape=_sds((nl, N_DEV * r, c), xs.dtype),
        scratch_shapes=[pltpu.SemaphoreType.DMA((N_DEV - 1,)), pltpu.SemaphoreType.DMA((N_DEV - 1,)), pltpu.SemaphoreType.DMA],
    )(xs)


def exchange_row_blocks(part, tag):
    nl, r8, c = part.shape
    r = r8 // N_DEV

    def body(p_ref, o_ref, send_sems, recv_sems, local_sem):
        me, peers = _me_and_peers()
        rows = lambda idx: p_ref.at[:, pl.ds(idx * r, r), :]
        mine = pltpu.make_async_copy(rows(me), o_ref.at[0], local_sem)
        mine.start()
        copies = [pltpu.make_async_remote_copy(src_ref=rows(idx), dst_ref=o_ref.at[k + 1], send_sem=send_sems.at[k],
                                               recv_sem=recv_sems.at[k], device_id=dev, device_id_type=pl.DeviceIdType.MESH)
                  for k, (dev, idx) in enumerate(peers)]
        for cp in copies:
            cp.start()
        for cp in copies:
            cp.wait()
        mine.wait()

    return pl.pallas_call(
        body, name=f"exchange_{tag}", in_specs=[_ANY], out_specs=_ANY, out_shape=_sds((N_DEV, nl, r, c), part.dtype),
        scratch_shapes=[pltpu.SemaphoreType.DMA((N_DEV - 1,)), pltpu.SemaphoreType.DMA((N_DEV - 1,)), pltpu.SemaphoreType.DMA],
    )(part)


def _row_tile(rows):
    if rows <= 512 and rows % 64:
        return rows
    for t in (64, 40, 32, 16, 8):
        if rows % t == 0:
            return t
    return rows


def sum_slots(slots, tag):
    _, rows, c = slots.shape
    tr = _row_tile(rows)

    def body(s_ref, o_ref):
        acc = s_ref[0]
        for k in range(1, N_DEV):
            acc = acc + s_ref[k]
        o_ref[...] = acc

    return _call(body, f"sum_slots_{tag}", (rows // tr,), [pl.BlockSpec((N_DEV, tr, c), lambda i: (0, i, 0))],
                 pl.BlockSpec((tr, c), lambda i: (i, 0)), _sds((rows, c)))(slots)


def _adamw(w, g, m, v):
    m = ADAM_B1 * m + (1.0 - ADAM_B1) * g
    v = ADAM_B2 * v + (1.0 - ADAM_B2) * (g * g)
    m_hat = m / (1.0 - ADAM_B1 ** ADAM_STEP)
    v_hat = v / (1.0 - ADAM_B2 ** ADAM_STEP)
    delta = -ADAM_LR * (m_hat / (jnp.sqrt(v_hat) + ADAM_EPS) + ADAM_WD * w)
    return delta, m, v


def adam_update(w, m, v, g, tag, slots=False):
    rows, c = w.shape
    tr = _row_tile(rows)

    def body(w_ref, m_ref, v_ref, g_ref, go_ref, d_ref, mo_ref, vo_ref):
        if slots:
            g = g_ref[0]
            for k in range(1, N_DEV):
                g = g + g_ref[k]
        else:
            g = g_ref[...]
        d, mn, vn = _adamw(w_ref[...], g, m_ref[...], v_ref[...])
        go_ref[...] = g
        d_ref[...] = d
        mo_ref[...] = mn
        vo_ref[...] = vn

    blk = pl.BlockSpec((tr, c), lambda i: (i, 0))
    gspec = pl.BlockSpec((N_DEV, tr, c), lambda i: (0, i, 0)) if slots else blk
    return _call(body, f"adam_{tag}", (rows // tr,), [blk, blk, blk, gspec], [blk] * 4, [_sds((rows, c))] * 4)(w, m, v, g)


_SMALL = (("norm_g", (2, 1024)), ("fox_f_bias", (2, 4)), ("fox_q_norm", (2, 64)), ("fox_k_norm", (2, 64)),
          ("hgrn_lb_logits", (2, 256)), ("hgrn_out_norm", (2, 256)), ("pool_w", (2, 4, 64, 64)), ("pool_scale", (2, 256)),
          ("mem_norm_g", (2, 1024)), ("mem_q_norm", (2, 64)), ("mem_k_norm", (2, 64)))
_SLAB_ROWS = 312


def pack_small(d):
    flat = jnp.concatenate([d[n].reshape(-1) for n, _ in _SMALL])
    return jnp.pad(flat, (0, _SLAB_ROWS * 128 - flat.shape[0])).reshape(_SLAB_ROWS, 128)


def unpack_small(slab):
    flat, out, off = slab.reshape(-1), {}, 0
    for n, shp in _SMALL:
        size = 1
        for e in shp:
            size *= e
        out[n] = flat[off:off + size].reshape(shp)
        off += size
    return out


def small_grads(g0, g1, lb_logits):
    st = lambda f: jnp.stack([f(g0), f(g1)])
    heads = lambda a: a.reshape(NH, HD).sum(0)
    p1 = jax.nn.sigmoid(lb_logits[1] - lb_logits[0])
    inside = (p1 > 0.0) & (p1 < 1.0 - 1e-6)
    dl1 = jnp.where(inside, g1["lb"][0] * p1 * (1.0 - p1), 0.0)
    diag = lambda a: jnp.stack([a.reshape(4, HD, 4, HD)[i, :, i, :] for i in range(4)])
    return dict(norm_g=st(lambda g: g["norm_g"][0]), fox_f_bias=st(lambda g: g["fox_f_bias"][0, :NH]),
                fox_q_norm=st(lambda g: heads(g["fox_q_norm"])), fox_k_norm=st(lambda g: heads(g["fox_k_norm"])),
                hgrn_lb_logits=jnp.stack([-dl1, dl1]), hgrn_out_norm=st(lambda g: g["hgrn_out_norm"][0]),
                pool_w=st(lambda g: diag(g["pool_wbd"])), pool_scale=st(lambda g: g["pool_scale"][0]),
                mem_norm_g=st(lambda g: g["mem_norm_g"][0]), mem_q_norm=st(lambda g: heads(g["mem_q_norm"])),
                mem_k_norm=st(lambda g: heads(g["mem_k_norm"])))


def kernel(x, mem, norm_g, w_in, fox_f_bias, fox_q_norm, fox_k_norm, hgrn_lb_logits, hgrn_out_norm, pool_w, pool_scale, mem_norm_g, mem_w_kv, mem_q_norm, mem_k_norm, w_out, loss_target, m_norm_g, m_w_in, m_fox_f_bias, m_fox_q_norm, m_fox_k_norm, m_hgrn_lb_logits, m_hgrn_out_norm, m_pool_w, m_pool_scale, m_mem_norm_g, m_mem_w_kv, m_mem_q_norm, m_mem_k_norm, m_w_out, v_norm_g, v_w_in, v_fox_f_bias, v_fox_q_norm, v_fox_k_norm, v_hgrn_lb_logits, v_hgrn_out_norm, v_pool_w, v_pool_scale, v_mem_norm_g, v_mem_w_kv, v_mem_q_norm, v_mem_k_norm, v_w_out):
    given = dict(norm_g=(norm_g, m_norm_g, v_norm_g), w_in=(w_in, m_w_in, v_w_in), fox_f_bias=(fox_f_bias, m_fox_f_bias, v_fox_f_bias),
                 fox_q_norm=(fox_q_norm, m_fox_q_norm, v_fox_q_norm), fox_k_norm=(fox_k_norm, m_fox_k_norm, v_fox_k_norm),
                 hgrn_lb_logits=(hgrn_lb_logits, m_hgrn_lb_logits, v_hgrn_lb_logits),
                 hgrn_out_norm=(hgrn_out_norm, m_hgrn_out_norm, v_hgrn_out_norm), pool_w=(pool_w, m_pool_w, v_pool_w),
                 pool_scale=(pool_scale, m_pool_scale, v_pool_scale), mem_norm_g=(mem_norm_g, m_mem_norm_g, v_mem_norm_g),
                 mem_w_kv=(mem_w_kv, m_mem_w_kv, v_mem_w_kv), mem_q_norm=(mem_q_norm, m_mem_q_norm, v_mem_q_norm),
                 mem_k_norm=(mem_k_norm, m_mem_k_norm, v_mem_k_norm), w_out=(w_out, m_w_out, v_w_out))
    order = ("norm_g", "w_in", "fox_f_bias", "fox_q_norm", "fox_k_norm", "hgrn_lb_logits", "hgrn_out_norm", "pool_w",
             "pool_scale", "mem_norm_g", "mem_w_kv", "mem_q_norm", "mem_k_norm", "w_out")

    w_in_full = all_gather_rows(_cast(permute_cols(w_in)), "w_in")
    w_out_full = all_gather_rows(_cast(w_out), "w_out")
    w_kv_full = all_gather_rows(_cast(mem_w_kv), "w_kv")
    p = prepare_params(norm_g, w_in_full, fox_f_bias, fox_q_norm, fox_k_norm, hgrn_lb_logits, hgrn_out_norm, pool_w,
                       pool_scale, mem_norm_g, w_kv_full, mem_q_norm, mem_k_norm, w_out_full)

    loss_part, grad_x, g0, g1 = local_step(x, mem, loss_target, p)
    loss = lax.psum(loss_part, ("x", "y", "c"))

    res = {}

    def sharded(name, g2, unperm=False):
        w, m, v = given[name]
        nl, r, c = w.shape
        slots = exchange_row_blocks(g2, name)
        if unperm:
            g = sum_slots(slots.reshape(N_DEV, nl * r, slots.shape[-1]), name).reshape(nl, r, -1)
            out = adam_update(w.reshape(nl * r, c), m.reshape(nl * r, c), v.reshape(nl * r, c),
                              unpermute_cols(g).reshape(nl * r, c), name)
        else:
            out = adam_update(w.reshape(nl * r, c), m.reshape(nl * r, c), v.reshape(nl * r, c),
                              slots.reshape(N_DEV, nl * r, c), name, slots=True)
        res[name] = tuple(o.reshape(nl, r, c) for o in out)

    sharded("w_in", jnp.stack([g0["w_in"], g1["w_in"]]), unperm=True)
    sharded("w_out", jnp.stack([g0["w_out"], g1["w_out"]]))
    sharded("mem_w_kv", jnp.stack([g0["mem_w_kv"], g1["mem_w_kv"]]))

    gsmall = pack_small(small_grads(g0, g1, hgrn_lb_logits))
    gathered = all_gather_rows(gsmall[None], "small").reshape(N_DEV, _SLAB_ROWS, 128)
    slabs = adam_update(*[pack_small({n: given[n][j] for n, _ in _SMALL}) for j in range(3)], gathered, "small", slots=True)
    small = [unpack_small(sl) for sl in slabs]
    for n, _ in _SMALL:
        res[n] = tuple(small[j][n] for j in range(4))

    return (loss, grad_x, *[res[n][0] for n in order], *[res[n][1] for n in order], *[res[n][2] for n in order],
            *[res[n][3] for n in order])
```

```python
import functools

import jax
import jax.numpy as jnp
from jax import lax
from jax.experimental import pallas as pl
from jax.experimental.pallas import tpu as pltpu

F32 = jnp.float32
BF = jnp.bfloat16
_MMT = BF

D_MODEL = 1024
GW = 256
HD = 64
NH = 4
CH = 64
N_MEM = 256
D_IN = 4100
D_INP = 4224
D_MIX = 1280
EPS = 1e-6
NEG_BIG = -1e30
LB_FLOOR = 1e-30
SCALE = HD ** -0.5
TQ = 256
N_DEV = 8
VMEM_LIMIT_BYTES = 56 * 1024 * 1024

ADAM_LR = 0.001
ADAM_B1 = 0.9
ADAM_B2 = 0.999
ADAM_EPS = 1e-08
ADAM_WD = 0.01
ADAM_STEP = 10

PIECES = (("A", 0, 768), ("B", 768, 768), ("C", 1536, 768), ("D", 2304, 256), ("E", 2560, 256),
          ("G", 2816, 1280), ("F", 4096, 128))
BWD_PIECES = (("A", 0, 768), ("Bq", 768, 256), ("Bk", 1024, 256), ("Bv", 1280, 256), ("C", 1536, 768),
              ("D", 2304, 256), ("E", 2560, 256), ("G", 2816, 1280), ("F", 4096, 128))
_ORIG = dict(fq=(0, 256), fk=(256, 512), fv=(512, 768), fg=(768, 1024), ff=(1024, 1028), sq=(1028, 1284),
             sk=(1284, 1540), sv=(1540, 1796), sg=(1796, 2052), hq=(2052, 2308), hf=(2308, 2564),
             hi=(2564, 2820), hg=(2820, 3076), pv=(3076, 3332), pg=(3332, 3588), mq=(3588, 3844), mg=(3844, 4100))
_PERM_ORDER = ("fq", "fk", "fv", "sq", "sk", "sv", "hq", "hf", "hi", "pv", "mq", "fg", "sg", "hg", "pg", "mg", "ff")
_ORIG_ORDER = ("fq", "fk", "fv", "fg", "ff", "sq", "sk", "sv", "sg", "hq", "hf", "hi", "hg", "pv", "pg", "mq", "mg")


def permute_cols(w):
    parts = [w[..., _ORIG[n][0]:_ORIG[n][1]] for n in _PERM_ORDER]
    parts.append(jnp.zeros(w.shape[:-1] + (D_INP - D_IN,), w.dtype))
    return jnp.concatenate(parts, axis=-1)


def unpermute_cols(g):
    start, off = {}, 0
    for n in _PERM_ORDER:
        start[n] = off
        off += _ORIG[n][1] - _ORIG[n][0]
    return jnp.concatenate([g[..., start[n]:start[n] + _ORIG[n][1] - _ORIG[n][0]] for n in _ORIG_ORDER], axis=-1)


def _cast(a):
    return a.astype(_MMT)


def _dg(a, b, ca, cb):
    return lax.dot_general(a, b, (((ca,), (cb,)), ((), ())), preferred_element_type=F32)


@jax.custom_vjp
def mm(a, b):
    return _dg(_cast(a), _cast(b), 1, 0)


@jax.custom_vjp
def mm_nt(a, b):
    return _dg(_cast(a), _cast(b), 1, 1)


@jax.custom_vjp
def mm_tn(a, b):
    return _dg(_cast(a), _cast(b), 0, 0)


mm.defvjp(lambda a, b: (mm(a, b), (a, b)),
          lambda r, g: (mm_nt(g, r[1]).astype(r[0].dtype), mm_tn(r[0], g).astype(r[1].dtype)))
mm_nt.defvjp(lambda a, b: (mm_nt(a, b), (a, b)),
             lambda r, g: (mm(g, r[1]).astype(r[0].dtype), mm_tn(g, r[0]).astype(r[1].dtype)))
mm_tn.defvjp(lambda a, b: (mm_tn(a, b), (a, b)),
             lambda r, g: (mm_nt(r[1], g).astype(r[0].dtype), mm(r[0], g).astype(r[1].dtype)))


def _split(a):
    hi = a.astype(_MMT)
    lo = (a - hi.astype(F32)).astype(_MMT)
    return hi, lo


@jax.custom_vjp
def xr(a, c, ct):
    hi, lo = _split(a)
    cc = _cast(c)
    return _dg(hi, cc, 1, 0) + _dg(lo, cc, 1, 0)


@jax.custom_vjp
def xl(c, ct, a):
    hi, lo = _split(a)
    cc = _cast(c)
    return _dg(cc, hi, 1, 0) + _dg(cc, lo, 1, 0)


xr.defvjp(lambda a, c, ct: (xr(a, c, ct), (c, ct)),
          lambda r, g: (xr(g, r[1], r[0]), jnp.zeros_like(r[0]), jnp.zeros_like(r[1])))
xl.defvjp(lambda c, ct, a: (xl(c, ct, a), (c, ct)),
          lambda r, g: (jnp.zeros_like(r[0]), jnp.zeros_like(r[1]), xl(r[1], r[0], g)))


def _iota(shape, dim):
    return lax.broadcasted_iota(jnp.int32, shape, dim)


def _hmask(h, n=GW):
    lane = _iota((1, n), 1)
    return ((lane >= h * HD) & (lane < (h + 1) * HD)).astype(F32)


def _bdmask(n=GW):
    return ((_iota((n, n), 0) >> 6) == (_iota((n, n), 1) >> 6)).astype(F32)


def _tri(n, kind="le"):
    r, c = _iota((n, n), 0), _iota((n, n), 1)
    return {"le": c <= r, "ge": c >= r, "gt": c > r, "lt": c < r}[kind].astype(F32)


def _onehot_lane(h, n=128):
    return (_iota((1, n), 1) == h).astype(F32)


def _logsig(x):
    return jnp.minimum(x, 0.0) - jnp.log1p(jnp.exp(-jnp.abs(x)))


def _sigmoid(x):
    return 0.5 * (jnp.tanh(0.5 * x) + 1.0)


def _silu(x):
    return x * _sigmoid(x)


def _rms(x, g):
    return x * lax.rsqrt(jnp.mean(x * x, axis=-1, keepdims=True) + EPS) * g


def _headrms(x, w, bd64):
    ms = xr(x * x, bd64, bd64)
    return x * lax.rsqrt(ms + EPS) * w


def _call(body, name, grid, in_specs, out_specs, out_shape, scratch=()):
    return pl.pallas_call(
        body, name=name, grid=grid, in_specs=in_specs, out_specs=out_specs, out_shape=out_shape,
        scratch_shapes=list(scratch),
        compiler_params=pltpu.CompilerParams(dimension_semantics=("arbitrary",) * len(grid),
                                             vmem_limit_bytes=VMEM_LIMIT_BYTES))


def _sds(shape, dtype=F32):
    return jax.ShapeDtypeStruct(shape, dtype)


def _acc(ref, val, first):
    @pl.when(first)
    def _():
        ref[...] = val

    @pl.when(jnp.logical_not(first))
    def _():
        ref[...] += val


def inproj_fwd(x2, g, w, layer, tag):
    t = x2.shape[0]

    def body(x_ref, g_ref, w_ref, h_ref, *outs):
        hb = _cast(_rms(x_ref[...], g_ref[...]))
        h_ref[...] = hb
        for (_, c0, wd), o in zip(PIECES, outs):
            o[...] = _dg(hb, _cast(w_ref[:, c0:c0 + wd]), 1, 0)

    return _call(
        body, f"inproj_fwd_{tag}", (t // TQ,),
        [pl.BlockSpec((TQ, D_MODEL), lambda i: (i, 0)),
         pl.BlockSpec((None, 1, D_MODEL), lambda i: (layer, 0, 0)),
         pl.BlockSpec((None, D_MODEL, D_INP), lambda i: (layer, 0, 0))],
        [pl.BlockSpec((TQ, D_MODEL), lambda i: (i, 0))] + [pl.BlockSpec((TQ, wd), lambda i: (i, 0)) for _, _, wd in PIECES],
        [_sds((t, D_MODEL), _MMT)] + [_sds((t, wd)) for _, _, wd in PIECES],
    )(x2, g, w)


def inproj_bwd_dx(x2, g, w, dy, dpieces, layer, tag):
    t = x2.shape[0]

    def body(x_ref, g_ref, w_ref, dy_ref, *rest):
        dps, (dx_ref, dg_ref) = rest[:len(BWD_PIECES)], rest[len(BWD_PIECES):]
        dh = None
        for (_, c0, wd), dp in zip(BWD_PIECES, dps):
            part = _dg(_cast(dp[...]), _cast(w_ref[:, c0:c0 + wd]), 1, 1)
            dh = part if dh is None else dh + part
        _, vjp = jax.vjp(_rms, x_ref[...], g_ref[...])
        dx, dg = vjp(dh)
        dx_ref[...] = dy_ref[...] + dx
        _acc(dg_ref, dg, pl.program_id(0) == 0)

    return _call(
        body, f"inproj_bwd_dx_{tag}", (t // TQ,),
        [pl.BlockSpec((TQ, D_MODEL), lambda i: (i, 0)),
         pl.BlockSpec((None, 1, D_MODEL), lambda i: (layer, 0, 0)),
         pl.BlockSpec((None, D_MODEL, D_INP), lambda i: (layer, 0, 0)),
         pl.BlockSpec((TQ, D_MODEL), lambda i: (i, 0))] + [pl.BlockSpec((TQ, wd), lambda i: (i, 0)) for _, _, wd in BWD_PIECES],
        [pl.BlockSpec((TQ, D_MODEL), lambda i: (i, 0)), pl.BlockSpec((1, D_MODEL), lambda i: (0, 0))],
        [_sds((t, D_MODEL)), _sds((1, D_MODEL))],
    )(x2, g, w, dy, *dpieces)


def matmul_tn(a, b, tag, tk=512):
    t, m = a.shape
    n = b.shape[1]
    tn = n if n <= 512 else (256 if n % 256 == 0 else 128)
    if n == 768:
        tn = 384

    def body(a_ref, b_ref, o_ref):
        _acc(o_ref, _dg(_cast(a_ref[...]), _cast(b_ref[...]), 0, 0), pl.program_id(1) == 0)

    return _call(
        body, f"matmul_tn_{tag}", (n // tn, t // tk),
        [pl.BlockSpec((tk, m), lambda j, i: (i, 0)), pl.BlockSpec((tk, tn), lambda j, i: (i, j))],
        pl.BlockSpec((m, tn), lambda j, i: (0, j)),
        _sds((m, n)),
    )(a, b)


def _fox_prep_fn(q, k, ff, qw, kw, bias, carry, bd64, tri, trit, last):
    qn = _headrms(q, qw, bd64)
    kn = _headrms(k, kw, bd64)
    lf = _logsig(ff + bias)
    c = xl(tri, trit, lf) + carry
    return qn, kn, c, jnp.sum(c * last, axis=0, keepdims=True)


def _prep_consts():
    return _bdmask() * (1.0 / HD), _tri(TQ), _tri(TQ, "ge"), (_iota((TQ, 1), 0) == TQ - 1).astype(F32)


def fox_prep_fwd(pa, pf, qw, kw, bias, bl, s, layer, tag):
    nq = s // TQ

    def body(q_ref, k_ref, v_ref, f_ref, qw_ref, kw_ref, b_ref, qn_ref, kn_ref, vb_ref, cq_ref, ck_ref, carry):
        @pl.when(pl.program_id(1) == 0)
        def _():
            carry[...] = jnp.zeros_like(carry)

        qn, kn, c, cl = _fox_prep_fn(q_ref[...], k_ref[...], f_ref[...], qw_ref[...], kw_ref[...], b_ref[...],
                                     carry[...], *_prep_consts())
        carry[...] = cl
        qn_ref[...] = _cast(qn)
        kn_ref[...] = _cast(kn)
        vb_ref[...] = _cast(v_ref[...])
        cq_ref[...] = c
        ck_ref[...] = c.T[0:8, :]

    tok = lambda j: pl.BlockSpec((TQ, GW), lambda b, i: (b * nq + i, j))
    par = lambda n: pl.BlockSpec((None, 1, n), lambda b, i: (layer, 0, 0))
    return _call(
        body, f"fox_prep_fwd_{tag}", (bl, nq),
        [tok(0), tok(1), tok(2), pl.BlockSpec((TQ, 128), lambda b, i: (b * nq + i, 0)), par(GW), par(GW), par(128)],
        [tok(0), tok(0), tok(0), pl.BlockSpec((TQ, 128), lambda b, i: (b * nq + i, 0)),
         pl.BlockSpec((None, 8, TQ), lambda b, i: (b, 0, i))],
        [_sds((bl * s, GW), _MMT)] * 3 + [_sds((bl * s, 128)), _sds((bl, 8, s))],
        [pltpu.VMEM((1, 128), F32)],
    )(pa, pa, pa, pf, qw, kw, bias)


def fox_prep_bwd(pa, pf, qw, kw, bias, cq, dqn, dkn, dv, dck, bl, s, layer, tag):
    nq = s // TQ

    def body(q_ref, k_ref, f_ref, qw_ref, kw_ref, b_ref, cq_ref, cprev_ref, dqn_ref, dkn_ref, dv_ref, dck_ref,
             da_ref, df_ref, dqw_ref, dkw_ref, db_ref, dcarry):
        i = pl.program_id(1)
        first = jnp.logical_and(pl.program_id(0) == 0, i == 0)

        @pl.when(i == 0)
        def _():
            dcarry[...] = jnp.zeros_like(dcarry)

        last = (_iota((TQ, 1), 0) == TQ - 1).astype(F32)
        carry_in = jnp.where(i == nq - 1, 0.0, jnp.sum(cprev_ref[...] * last, axis=0, keepdims=True))
        consts = _prep_consts()
        _, vjp = jax.vjp(lambda *a: _fox_prep_fn(*a, *consts), q_ref[...], k_ref[...], f_ref[...], qw_ref[...],
                         kw_ref[...], b_ref[...], carry_in)
        dc = dck_ref[...].T
        dq, dk, dff, dqw, dkw, dbias, dcin = vjp((dqn_ref[...], dkn_ref[...], dc, dcarry[...]))
        dcarry[...] = dcin
        da_ref[:, 0:GW] = dq
        da_ref[:, GW:2 * GW] = dk
        da_ref[:, 2 * GW:3 * GW] = dv_ref[...]
        df_ref[...] = dff
        _acc(dqw_ref, dqw, first)
        _acc(dkw_ref, dkw, first)
        _acc(db_ref, dbias, first)

    rv = lambda b, i: b * nq + (nq - 1 - i)
    tok = lambda j: pl.BlockSpec((TQ, GW), lambda b, i: (rv(b, i), j))
    tok0 = pl.BlockSpec((TQ, GW), lambda b, i: (rv(b, i), 0))
    t128 = pl.BlockSpec((TQ, 128), lambda b, i: (rv(b, i), 0))
    prev = pl.BlockSpec((TQ, 128), lambda b, i: (jnp.maximum(rv(b, i) - 1, 0), 0))
    par = lambda n: pl.BlockSpec((None, 1, n), lambda b, i: (layer, 0, 0))
    acc = lambda n: pl.BlockSpec((1, n), lambda b, i: (0, 0))
    return _call(
        body, f"fox_prep_bwd_{tag}", (bl, nq),
        [tok(0), tok(1), t128, par(GW), par(GW), par(128), t128, prev, tok0, tok0, tok0,
         pl.BlockSpec((None, 128, TQ), lambda b, i: (b, 0, nq - 1 - i))],
        [pl.BlockSpec((TQ, 3 * GW), lambda b, i: (rv(b, i), 0)), t128, acc(GW), acc(GW), acc(128)],
        [_sds((bl * s, 3 * GW)), _sds((bl * s, 128)), _sds((1, GW)), _sds((1, GW)), _sds((1, 128))],
        [pltpu.VMEM((1, 128), F32)],
    )(pa, pa, pf, qw, kw, bias, cq, cq, dqn, dkn, dv, dck)


def _lane_pick(x, h):
    return jnp.sum(x * _onehot_lane(h), axis=-1, keepdims=True)


TA = 128
SROWS = NH * TA


def _stack_heads(x, scale=1.0):
    return _cast(jnp.concatenate([x * (_hmask(h) * scale) for h in range(NH)], axis=0))


def _stack_cols(x):
    return jnp.concatenate([_lane_pick(x, h) for h in range(NH)], axis=0)


def _spread_heads(col):
    return sum(col[h * TA:(h + 1) * TA] * _hmask(h) for h in range(NH))


def _lanes_cat(w):
    return jnp.concatenate([w[h * TA:(h + 1) * TA] for h in range(NH)], axis=1)


def _mask_stack(x):
    return _cast(jnp.concatenate([x * _hmask(h).astype(x.dtype) for h in range(NH)], axis=0))


def _stack_rows(i):
    return i * TA + (_iota((SROWS, 1), 0) & (TA - 1))


def _n_key_tiles(i):
    return lax.shift_right_logical(i * TA, TQ.bit_length() - 1) + 1


def fox_attn_fwd(qn, kn, vb, cq, ck, bl, s, tag):
    nq = s // TA

    def body(q_ref, k_ref, v_ref, cq_ref, ck_ref, o_ref, lse_ref, acc):
        i = pl.program_id(1)
        qs = _stack_heads(q_ref[...].astype(F32), SCALE)
        cqs = _stack_cols(cq_ref[...])
        row = _stack_rows(i)
        acc[...] = jnp.zeros_like(acc)

        def step(j, ml):
            m, l = ml
            ks = pl.ds(pl.multiple_of(j * TQ, TQ), TQ)
            ckb = jnp.concatenate([jnp.broadcast_to(ck_ref[h:h + 1, ks], (TA, TQ)) for h in range(NH)], axis=0)
            sc = _dg(qs, k_ref[ks, :], 1, 1) + cqs - ckb
            col = j * TQ + _iota((1, TQ), 1)
            sc = jnp.where(col <= row, sc, NEG_BIG)
            m_new = jnp.maximum(m, jnp.max(sc, axis=-1, keepdims=True))
            alpha = jnp.exp(m - m_new)
            p = jnp.exp(sc - m_new)
            acc[...] = _spread_heads(alpha) * acc[...] + _dg(_lanes_cat(_cast(p)), _mask_stack(v_ref[ks, :]), 1, 0)
            return m_new, alpha * l + jnp.sum(p, axis=-1, keepdims=True)

        m, l = lax.fori_loop(0, _n_key_tiles(i), step, (jnp.full((SROWS, 1), NEG_BIG, F32), jnp.zeros((SROWS, 1), F32)))
        o_ref[...] = acc[...] / _spread_heads(l)
        lse_h = m + jnp.log(l)
        lse_ref[...] = sum(lse_h[h * TA:(h + 1) * TA] * _onehot_lane(h) for h in range(NH))

    tok = pl.BlockSpec((TA, GW), lambda b, i: (b * nq + i, 0))
    seq = pl.BlockSpec((s, GW), lambda b, i: (b, 0))
    t128 = pl.BlockSpec((TA, 128), lambda b, i: (b * nq + i, 0))
    return _call(
        body, f"fox_attn_fwd_{tag}", (bl, nq),
        [tok, seq, seq, t128, pl.BlockSpec((None, 8, s), lambda b, i: (b, 0, 0))],
        [tok, t128], [_sds((bl * s, GW)), _sds((bl * s, 128))],
        [pltpu.VMEM((TA, GW), F32)],
    )(qn, kn, vb, cq, ck)


def fox_attn_bwd(qn, kn, vb, cq, ck, lse, do, bl, s, tag):
    nq = s // TA

    def body(q_ref, k_ref, v_ref, cq_ref, ck_ref, lse_ref, do_ref, dq_ref, dk_ref, dv_ref, dck_ref, dqa, p_s, dp_s):
        i = pl.program_id(1)

        @pl.when(i == 0)
        def _():
            dk_ref[...] = jnp.zeros_like(dk_ref)
            dv_ref[...] = jnp.zeros_like(dv_ref)
            dck_ref[...] = jnp.zeros_like(dck_ref)

        qs = _stack_heads(q_ref[...].astype(F32), SCALE)
        dos = _stack_heads(do_ref[...])
        cqs, lses = _stack_cols(cq_ref[...]), _stack_cols(lse_ref[...])
        row = _stack_rows(i)
        dqa[...] = jnp.zeros_like(dqa)
        nk = _n_key_tiles(i)

        def probs(j, delta):
            ks = pl.ds(pl.multiple_of(j * TQ, TQ), TQ)
            ckb = jnp.concatenate([jnp.broadcast_to(ck_ref[h:h + 1, ks], (TA, TQ)) for h in range(NH)], axis=0)
            sc = _dg(qs, k_ref[ks, :], 1, 1) + cqs - ckb
            col = j * TQ + _iota((1, TQ), 1)
            p = jnp.where(col <= row, jnp.exp(sc - lses), 0.0)
            dp = _dg(dos, v_ref[ks, :], 1, 1)
            p_s[:, ks] = p
            dp_s[:, ks] = dp
            return delta + jnp.sum(p * dp, axis=-1, keepdims=True)

        delta = lax.fori_loop(0, nk, probs, jnp.zeros((SROWS, 1), F32))

        def step(j, carry):
            ks = pl.ds(pl.multiple_of(j * TQ, TQ), TQ)
            p = p_s[:, ks]
            ds = p * (dp_s[:, ks] - delta)
            dsb = _cast(ds)
            dqa[...] += _dg(_lanes_cat(dsb), _mask_stack(k_ref[ks, :]), 1, 0) * SCALE
            dk_ref[ks, :] += _dg(dsb, qs, 0, 0)
            dv_ref[ks, :] += _dg(_cast(p), dos, 0, 0)
            for h in range(NH):
                dck_ref[h:h + 1, ks] -= jnp.sum(ds[h * TA:(h + 1) * TA], axis=0, keepdims=True)
            return carry

        lax.fori_loop(0, nk, step, 0)
        dq_ref[...] = dqa[...]

    tok = pl.BlockSpec((TA, GW), lambda b, i: (b * nq + i, 0))
    seq = pl.BlockSpec((s, GW), lambda b, i: (b, 0))
    t128 = pl.BlockSpec((TA, 128), lambda b, i: (b * nq + i, 0))
    return _call(
        body, f"fox_attn_bwd_{tag}", (bl, nq),
        [tok, seq, seq, t128, pl.BlockSpec((None, 8, s), lambda b, i: (b, 0, 0)), t128, tok],
        [tok, seq, seq, pl.BlockSpec((None, 128, s), lambda b, i: (b, 0, 0))],
        [_sds((bl * s, GW)), _sds((bl * s, GW)), _sds((bl * s, GW)), _sds((bl, 128, s))],
        [pltpu.VMEM((TA, GW), F32), pltpu.VMEM((SROWS, s), F32), pltpu.VMEM((SROWS, s), F32)],
    )(qn, kn, vb, cq, ck, lse, do)


def _sb_block(qh, kb, row, col, upper, r_carry):
    z = _dg(qh, kb, 1, 1)
    valid = col < row
    ls = _logsig(z)
    lom = jnp.where(valid, ls - z, 0.0)
    between = xr(lom, upper, upper) + r_carry
    w = jnp.where(valid, jnp.exp(ls + between), 0.0)
    return z, ls, lom, w, valid


def sb_attn_fwd(pb, bl, s, tag):
    nq = s // TA

    def body(q_ref, k_ref, v_ref, o_ref, acc):
        i = pl.program_id(1)
        qs = _stack_heads(q_ref[...], SCALE)
        row = _stack_rows(i)
        upper = _tri(TQ, "lt")
        acc[...] = jnp.zeros_like(acc)
        last = _n_key_tiles(i) - 1

        def step(jj, r):
            j = last - jj
            ks = pl.ds(pl.multiple_of(j * TQ, TQ), TQ)
            col = j * TQ + _iota((1, TQ), 1)
            _, _, lom, w, _ = _sb_block(qs, _cast(k_ref[ks, :]), row, col, upper, r)
            acc[...] += _dg(_lanes_cat(_cast(w)), _mask_stack(v_ref[ks, :]), 1, 0)
            return r + jnp.sum(lom, axis=-1, keepdims=True)

        lax.fori_loop(0, last + 1, step, jnp.zeros((SROWS, 1), F32))
        o_ref[...] = acc[...]

    tok = lambda j: pl.BlockSpec((TA, GW), lambda b, i: (b * nq + i, j))
    seq = lambda j: pl.BlockSpec((s, GW), lambda b, i: (b, j))
    return _call(
        body, f"sb_attn_fwd_{tag}", (bl, nq), [tok(0), seq(1), seq(2)],
        pl.BlockSpec((TA, GW), lambda b, i: (b * nq + i, 0)), _sds((bl * s, GW)),
        [pltpu.VMEM((TA, GW), F32)],
    )(pb, pb, pb)


def sb_attn_bwd(pb, do, bl, s, tag):
    nq = s // TA

    def body(q_ref, k_ref, v_ref, do_ref, dq_ref, dk_ref, dv_ref, dqa, ls_s, lom_s, w_s, g_s):
        i = pl.program_id(1)

        @pl.when(i == 0)
        def _():
            dk_ref[...] = jnp.zeros_like(dk_ref)
            dv_ref[...] = jnp.zeros_like(dv_ref)

        qs = _stack_heads(q_ref[...], SCALE)
        dos = _stack_heads(do_ref[...])
        row = _stack_rows(i)
        upper = _tri(TQ, "lt")
        before = _tri(TQ, "gt")
        dqa[...] = jnp.zeros_like(dqa)
        last = _n_key_tiles(i) - 1

        def weights(jj, r):
            j = last - jj
            ks = pl.ds(pl.multiple_of(j * TQ, TQ), TQ)
            col = j * TQ + _iota((1, TQ), 1)
            _, ls, lom, w, _ = _sb_block(qs, _cast(k_ref[ks, :]), row, col, upper, r)
            ls_s[:, ks] = ls
            lom_s[:, ks] = lom
            w_s[:, ks] = _cast(w)
            g_s[:, ks] = _dg(dos, _cast(v_ref[ks, :]), 1, 1) * w
            return r + jnp.sum(lom, axis=-1, keepdims=True)

        lax.fori_loop(0, last + 1, weights, jnp.zeros((SROWS, 1), F32))

        def step(j, cpre):
            ks = pl.ds(pl.multiple_of(j * TQ, TQ), TQ)
            col = j * TQ + _iota((1, TQ), 1)
            g = g_s[:, ks]
            pre = cpre + xr(g, before, before)
            dz = jnp.where(col < row, g * jnp.exp(lom_s[:, ks]) - jnp.exp(ls_s[:, ks]) * pre, 0.0)
            dzb = _cast(dz)
            dqa[...] += _dg(_lanes_cat(dzb), _mask_stack(k_ref[ks, :]), 1, 0) * SCALE
            dk_ref[ks, :] += _dg(dzb, qs, 0, 0)
            dv_ref[ks, :] += _dg(w_s[:, ks], dos, 0, 0)
            return cpre + jnp.sum(g, axis=-1, keepdims=True)

        lax.fori_loop(0, last + 1, step, jnp.zeros((SROWS, 1), F32))
        dq_ref[...] = dqa[...]

    tok = lambda j: pl.BlockSpec((TA, GW), lambda b, i: (b * nq + i, j))
    seq = lambda j: pl.BlockSpec((s, GW), lambda b, i: (b, j))
    return _call(
        body, f"sb_attn_bwd_{tag}", (bl, nq), [tok(0), seq(1), seq(2), tok(0)],
        [tok(0), seq(0), seq(0)], [_sds((bl * s, GW))] * 3,
        [pltpu.VMEM((TA, GW), F32), pltpu.VMEM((SROWS, s), F32), pltpu.VMEM((SROWS, s), F32),
         pltpu.VMEM((SROWS, s), _MMT), pltpu.VMEM((SROWS, s), F32)],
    )(pb, pb, pb, do)


def _hgrn_consts():
    r, c = _iota((CH, CH), 0), _iota((CH, CH), 1)
    rr = _iota((CH, 1), 0)
    tri = (c <= r).astype(F32)
    lv = []
    for m in (8, 4, 2, 1):
        up = ((rr & (2 * m - 1)) >= m).astype(F32)
        lo = 1.0 - up
        selq = (((r & (2 * m - 1)) >= m) & (c == (r & ~(m - 1)) - 1)).astype(F32)
        selk = (((r & (2 * m - 1)) < m) & (c == (r & ~(m - 1)) + m - 1)).astype(F32)
        pm = (((r & ~(2 * m - 1)) == (c & ~(2 * m - 1))) & ((r & (2 * m - 1)) >= m) & ((c & (2 * m - 1)) < m)).astype(F32)
        lv.append((up, lo, selq, selq.T, selk, selk.T, pm))
    return dict(tri=tri, trit=tri.T, rr=rr, lv=lv, bd=_bdmask(), bd64=_bdmask() * (1.0 / HD),
                hm=[_hmask(h) for h in range(NH)])


def _hgrn_chunk_fn(hq, hf, hi, lb, wn, st, cs):
    q = _silu(hq)
    log_lb = jnp.log(jnp.maximum(lb, LB_FLOOR))
    a, bb = log_lb, jnp.log1p(-lb) + _logsig(hf)
    g = jnp.maximum(a, bb) + jnp.log1p(jnp.exp(-jnp.abs(a - bb)))
    k = (1.0 - lb) * _sigmoid(-hf)
    v = hi
    rr = cs["rr"]
    b = xl(cs["tri"], cs["trit"], g)
    row_of = lambda n: jnp.sum(b * (rr == n).astype(F32), axis=0, keepdims=True)
    o = mm_nt(q * jnp.exp(b), st)
    qs, ks = [], []
    for ib in (1, 2, 3):
        ref = row_of(16 * ib - 1)
        inq = ((rr >= 16 * ib) & (rr < 16 * ib + 16)).astype(F32)
        ink = (rr < 16 * ib).astype(F32)
        qs.append(q * jnp.exp((b - ref) * inq) * inq)
        ks.append(k * jnp.exp((ref - b) * ink) * ink)
    qcat, kcat = jnp.concatenate(qs, axis=1), jnp.concatenate(ks, axis=1)
    lvl = []
    for up, lo, selq, selqt, selk, selkt, pm in cs["lv"]:
        qe = q * jnp.exp((b - xl(selq, selqt, b)) * up) * up
        ke = k * jnp.exp((xl(selk, selkt, b) - b) * lo) * lo
        lvl.append((qe, ke, pm))
    for h in range(NH):
        hm = cs["hm"][h]
        a_h = mm_nt(qcat * jnp.concatenate([hm] * 3, axis=1), kcat)
        for qe, ke, pm in lvl:
            a_h = a_h + mm_nt(qe * hm, ke) * pm
        o = o + mm(a_h, v * hm)
    o = o + xr(q * k, cs["bd"], cs["bd"]) * v
    b_last = row_of(CH - 1)
    st_new = st * jnp.exp(b_last) + mm_tn(v, k * jnp.exp(b_last - b)) * cs["bd"]
    return _headrms(o, wn, cs["bd64"]), st_new


def hgrn_fwd(pc, lb, wn, bl, s, layer, tag):
    nc = s // CH

    def body(q_ref, f_ref, i_ref, lb_ref, wn_ref, o_ref, st_ref, st):
        @pl.when(pl.program_id(0) == 0)
        def _():
            st[...] = jnp.zeros_like(st)

        cs = _hgrn_consts()
        for b in range(bl):
            st_ref[b] = st[b]
            o, st_new = _hgrn_chunk_fn(q_ref[b], f_ref[b], i_ref[b], lb_ref[...], wn_ref[...], st[b], cs)
            o_ref[b] = o
            st[b] = st_new

    tok = lambda j: pl.BlockSpec((bl, CH, GW), lambda c: (0, c, j))
    par = pl.BlockSpec((None, 1, GW), lambda c: (layer, 0, 0))
    pc3 = pc.reshape(bl, s, 3 * GW)
    o, states = _call(
        body, f"hgrn_fwd_{tag}", (nc,), [tok(0), tok(1), tok(2), par, par],
        [tok(0), pl.BlockSpec((bl, None, GW, GW), lambda c: (0, c, 0, 0))],
        [_sds((bl, s, GW)), _sds((bl, nc, GW, GW))],
        [pltpu.VMEM((bl, GW, GW), F32)],
    )(pc3, pc3, pc3, lb, wn)
    return o.reshape(bl * s, GW), states


def hgrn_bwd(pc, lb, wn, states, do, bl, s, layer, tag):
    nc = s // CH

    def body(q_ref, f_ref, i_ref, lb_ref, wn_ref, st_ref, do_ref, dc_ref, dlb_ref, dwn_ref, dst):
        c = pl.program_id(0)

        @pl.when(c == 0)
        def _():
            dst[...] = jnp.zeros_like(dst)

        cs = _hgrn_consts()
        dlb_sum = dwn_sum = None
        for b in range(bl):
            _, vjp = jax.vjp(lambda *a: _hgrn_chunk_fn(*a, cs), q_ref[b], f_ref[b], i_ref[b], lb_ref[...],
                             wn_ref[...], st_ref[b])
            dq, df, di, dlb, dwn, dst_in = vjp((do_ref[b], dst[b]))
            dst[b] = dst_in
            dc_ref[b, :, 0:GW] = dq
            dc_ref[b, :, GW:2 * GW] = df
            dc_ref[b, :, 2 * GW:3 * GW] = di
            dlb_sum = dlb if dlb_sum is None else dlb_sum + dlb
            dwn_sum = dwn if dwn_sum is None else dwn_sum + dwn
        _acc(dlb_ref, dlb_sum, c == 0)
        _acc(dwn_ref, dwn_sum, c == 0)

    tok = lambda j: pl.BlockSpec((bl, CH, GW), lambda c: (0, nc - 1 - c, j))
    par = pl.BlockSpec((None, 1, GW), lambda c: (layer, 0, 0))
    acc = pl.BlockSpec((1, GW), lambda c: (0, 0))
    pc3 = pc.reshape(bl, s, 3 * GW)
    dc, dlb, dwn = _call(
        body, f"hgrn_bwd_{tag}", (nc,),
        [tok(0), tok(1), tok(2), par, par, pl.BlockSpec((bl, None, GW, GW), lambda c: (0, nc - 1 - c, 0, 0)), tok(0)],
        [pl.BlockSpec((bl, CH, 3 * GW), lambda c: (0, nc - 1 - c, 0)), acc, acc],
        [_sds((bl, s, 3 * GW)), _sds((1, GW)), _sds((1, GW))],
        [pltpu.VMEM((bl, GW, GW), F32)],
    )(pc3, pc3, pc3, lb, wn, states, do.reshape(bl, s, GW))
    return dc.reshape(bl * s, 3 * GW), dlb, dwn


def _shift_rows(x, k, up):
    n = x.shape[0]
    rr = _iota((n, 1), 0)
    if up:
        return jnp.where(rr < n - k, pltpu.roll(x, n - k, 0), 0.0)
    return jnp.where(rr >= k, pltpu.roll(x, k, 0), 0.0)


def _window_sums(x, up):
    s2 = x + _shift_rows(x, 1, up)
    s4 = s2 + _shift_rows(s2, 2, up)
    s8 = s4 + _shift_rows(s4, 4, up)
    s16 = s8 + _shift_rows(s8, 8, up)
    return s2, s4, s8, s16


def _pool_div(n):
    pos = (_iota((n, 1), 0) + 1).astype(F32)
    return [jnp.minimum(pos, float(w)) for w in (2, 4, 8, 16)]


def _pool_mix(sums, scaled):
    out = None
    for gi, sw in enumerate(sums):
        part = (sw if scaled is None else sw / scaled[gi]) * _hmask(gi)
        out = part if out is None else out + part
    return out


def pool_fwd(pd, wbd, scale, bl, s, layer, tag):
    def body(u_ref, w_ref, sc_ref, o_ref):
        u = u_ref[...]
        pm = _pool_mix(_window_sums(u, False), _pool_div(s)) - u
        o_ref[...] = _dg(_cast(pm), _cast(w_ref[...]), 1, 0) * sc_ref[...]

    seq = pl.BlockSpec((s, GW), lambda b: (b, 0))
    return _call(
        body, f"pool_fwd_{tag}", (bl,),
        [seq, pl.BlockSpec((None, GW, GW), lambda b: (layer, 0, 0)), pl.BlockSpec((None, 1, GW), lambda b: (layer, 0, 0))],
        seq, _sds((bl * s, GW)),
    )(pd, wbd, scale)


def pool_bwd(pd, wbd, scale, do, bl, s, layer, tag):
    def body(u_ref, w_ref, sc_ref, do_ref, du_ref, dw_ref, dsc_ref):
        first = pl.program_id(0) == 0
        u, do = u_ref[...], do_ref[...]
        div = _pool_div(s)
        pm = _pool_mix(_window_sums(u, False), div) - u
        ypre = _dg(_cast(pm), _cast(w_ref[...]), 1, 0)
        dys = do * sc_ref[...]
        _acc(dsc_ref, jnp.sum(do * ypre, axis=0, keepdims=True), first)
        _acc(dw_ref, _dg(_cast(pm), _cast(dys), 0, 0), first)
        dpm = _dg(_cast(dys), _cast(w_ref[...]), 1, 1)
        dsc = [dpm / d for d in div]
        adj = None
        for gi in range(4):
            part = _window_sums(dsc[gi] * _hmask(gi), True)[gi]
            adj = part if adj is None else adj + part
        du_ref[...] = adj - dpm

    seq = pl.BlockSpec((s, GW), lambda b: (b, 0))
    return _call(
        body, f"pool_bwd_{tag}", (bl,),
        [seq, pl.BlockSpec((None, GW, GW), lambda b: (layer, 0, 0)), pl.BlockSpec((None, 1, GW), lambda b: (layer, 0, 0)), seq],
        [seq, pl.BlockSpec((GW, GW), lambda b: (0, 0)), pl.BlockSpec((1, GW), lambda b: (0, 0))],
        [_sds((bl * s, GW)), _sds((GW, GW)), _sds((1, GW))],
    )(pd, wbd, scale, do)


def _mem_prep_fn(mem, g, wk, wv, kw, bd64):
    mn = _rms(mem, g)
    return _headrms(mm(mn, wk), kw, bd64), mm(mn, wv)


def mem_prep_fwd(mem2, g, wkv, kw, bl, layer, tag):
    def body(m_ref, g_ref, wk_ref, wv_ref, kw_ref, k_ref, v_ref):
        k, v = _mem_prep_fn(m_ref[...], g_ref[...], wk_ref[...], wv_ref[...], kw_ref[...], _bdmask() * (1.0 / HD))
        k_ref[...] = k
        v_ref[...] = v

    blk = pl.BlockSpec((N_MEM, GW), lambda b: (b, 0))
    return _call(
        body, f"mem_prep_fwd_{tag}", (bl,),
        [pl.BlockSpec((N_MEM, D_MODEL), lambda b: (b, 0)), pl.BlockSpec((None, 1, D_MODEL), lambda b: (layer, 0, 0)),
         pl.BlockSpec((None, D_MODEL, GW), lambda b: (layer, 0, 0)), pl.BlockSpec((None, D_MODEL, GW), lambda b: (layer, 0, 1)),
         pl.BlockSpec((None, 1, GW), lambda b: (layer, 0, 0))],
        [blk, blk], [_sds((bl * N_MEM, GW))] * 2,
    )(mem2, g, wkv, wkv, kw)


def mem_prep_bwd(mem2, g, wkv, kw, dk, dv, bl, layer, tag):
    def body(m_ref, g_ref, wk_ref, wv_ref, kw_ref, dk_ref, dv_ref, dwk_ref, dwv_ref, dg_ref, dkw_ref):
        first = pl.program_id(0) == 0
        bd64 = _bdmask() * (1.0 / HD)
        _, vjp = jax.vjp(lambda g_, wk, wv, kw_: _mem_prep_fn(m_ref[...], g_, wk, wv, kw_, bd64),
                         g_ref[...], wk_ref[...].astype(F32), wv_ref[...].astype(F32), kw_ref[...])
        dg, dwk, dwv, dkw = vjp((dk_ref[...], dv_ref[...]))
        _acc(dwk_ref, dwk, first)
        _acc(dwv_ref, dwv, first)
        _acc(dg_ref, dg, first)
        _acc(dkw_ref, dkw, first)

    blk = pl.BlockSpec((N_MEM, GW), lambda b: (b, 0))
    return _call(
        body, f"mem_prep_bwd_{tag}", (bl,),
        [pl.BlockSpec((N_MEM, D_MODEL), lambda b: (b, 0)), pl.BlockSpec((None, 1, D_MODEL), lambda b: (layer, 0, 0)),
         pl.BlockSpec((None, D_MODEL, GW), lambda b: (layer, 0, 0)), pl.BlockSpec((None, D_MODEL, GW), lambda b: (layer, 0, 1)),
         pl.BlockSpec((None, 1, GW), lambda b: (layer, 0, 0)), blk, blk],
        [pl.BlockSpec((D_MODEL, GW), lambda b: (0, 0)), pl.BlockSpec((D_MODEL, GW), lambda b: (0, 0)),
         pl.BlockSpec((1, D_MODEL), lambda b: (0, 0)), pl.BlockSpec((1, GW), lambda b: (0, 0))],
        [_sds((D_MODEL, GW)), _sds((D_MODEL, GW)), _sds((1, D_MODEL)), _sds((1, GW))],
    )(mem2, g, wkv, wkv, kw, dk, dv)


def _mem_attn_fn(mq, qw, k, v, bd64):
    qn = _headrms(mq, qw, bd64)
    out = None
    for h in range(NH):
        hm = _hmask(h)
        lg = mm_nt(qn * hm, k) * SCALE
        e = jnp.exp(lg - lax.stop_gradient(jnp.max(lg, axis=-1, keepdims=True)))
        p = e / jnp.sum(e, axis=-1, keepdims=True)
        part = mm(p, v) * hm
        out = part if out is None else out + part
    return out


def mem_attn_fwd(pe, qw, k, v, bl, s, layer, tag):
    nq = s // TQ

    def body(q_ref, qw_ref, k_ref, v_ref, o_ref):
        o_ref[...] = _mem_attn_fn(q_ref[...], qw_ref[...], k_ref[...], v_ref[...], _bdmask() * (1.0 / HD))

    tok = pl.BlockSpec((TQ, GW), lambda b, i: (b * nq + i, 0))
    kv = pl.BlockSpec((N_MEM, GW), lambda b, i: (b, 0))
    return _call(
        body, f"mem_attn_fwd_{tag}", (bl, nq), [tok, pl.BlockSpec((None, 1, GW), lambda b, i: (layer, 0, 0)), kv, kv],
        tok, _sds((bl * s, GW)),
    )(pe, qw, k, v)


def mem_attn_bwd(pe, qw, k, v, do, bl, s, layer, tag):
    nq = s // TQ

    def body(q_ref, qw_ref, k_ref, v_ref, do_ref, dq_ref, dk_ref, dv_ref, dqw_ref):
        i = pl.program_id(1)
        bd64 = _bdmask() * (1.0 / HD)
        _, vjp = jax.vjp(lambda *a: _mem_attn_fn(*a, bd64), q_ref[...], qw_ref[...], k_ref[...], v_ref[...])
        dq, dqw, dk, dv = vjp(do_ref[...])
        dq_ref[...] = dq
        _acc(dk_ref, dk, i == 0)
        _acc(dv_ref, dv, i == 0)
        _acc(dqw_ref, dqw, jnp.logical_and(pl.program_id(0) == 0, i == 0))

    tok = pl.BlockSpec((TQ, GW), lambda b, i: (b * nq + i, 0))
    kv = pl.BlockSpec((N_MEM, GW), lambda b, i: (b, 0))
    return _call(
        body, f"mem_attn_bwd_{tag}", (bl, nq),
        [tok, pl.BlockSpec((None, 1, GW), lambda b, i: (layer, 0, 0)), kv, kv, tok],
        [tok, kv, kv, pl.BlockSpec((1, GW), lambda b, i: (0, 0))],
        [_sds((bl * s, GW)), _sds((bl * N_MEM, GW)), _sds((bl * N_MEM, GW)), _sds((1, GW))],
    )(pe, qw, k, v, do)


def _gate_out_fn(outs, gates, wparts):
    y = None
    for o, g, w in zip(outs, gates, wparts):
        part = mm(o * _silu(g), w)
        y = part if y is None else y + part
    return y


def outproj_fwd(x2, outs, pg, wout, layer, tag):
    t = x2.shape[0]

    def body(x_ref, oa, ob, oc, od, oe, g_ref, w_ref, y_ref):
        outs_ = [r[...] for r in (oa, ob, oc, od, oe)]
        gates = [g_ref[:, j * GW:(j + 1) * GW] for j in range(5)]
        wparts = [w_ref[j * GW:(j + 1) * GW, :] for j in range(5)]
        y_ref[...] = x_ref[...] + _gate_out_fn(outs_, gates, wparts)

    tok = pl.BlockSpec((TQ, GW), lambda i: (i, 0))
    big = pl.BlockSpec((TQ, D_MODEL), lambda i: (i, 0))
    return _call(
        body, f"outproj_fwd_{tag}", (t // TQ,),
        [big] + [tok] * 5 + [pl.BlockSpec((TQ, D_MIX), lambda i: (i, 0)),
                            pl.BlockSpec((None, D_MIX, D_MODEL), lambda i: (layer, 0, 0))],
        big, _sds((t, D_MODEL)),
    )(x2, *outs, pg, wout)


def outproj_bwd(outs, pg, wout, dy, layer, tag):
    t = dy.shape[0]

    def body(oa, ob, oc, od, oe, g_ref, w_ref, dy_ref, da, db, dc, dd, de, dg_ref, dw_ref):
        outs_ = [r[...] for r in (oa, ob, oc, od, oe)]
        gates = [g_ref[:, j * GW:(j + 1) * GW] for j in range(5)]
        wparts = [w_ref[j * GW:(j + 1) * GW, :].astype(F32) for j in range(5)]
        _, vjp = jax.vjp(_gate_out_fn, outs_, gates, wparts)
        douts, dgates, dws = vjp(dy_ref[...])
        for r, val in zip((da, db, dc, dd, de), douts):
            r[...] = val
        first = pl.program_id(0) == 0
        for j in range(5):
            dg_ref[:, j * GW:(j + 1) * GW] = dgates[j]

        @pl.when(first)
        def _():
            for j in range(5):
                dw_ref[j * GW:(j + 1) * GW, :] = dws[j]

        @pl.when(jnp.logical_not(first))
        def _():
            for j in range(5):
                dw_ref[j * GW:(j + 1) * GW, :] += dws[j]

    tok = pl.BlockSpec((TQ, GW), lambda i: (i, 0))
    return _call(
        body, f"outproj_bwd_{tag}", (t // TQ,),
        [tok] * 5 + [pl.BlockSpec((TQ, D_MIX), lambda i: (i, 0)), pl.BlockSpec((None, D_MIX, D_MODEL), lambda i: (layer, 0, 0)),
                     pl.BlockSpec((TQ, D_MODEL), lambda i: (i, 0))],
        [tok] * 5 + [pl.BlockSpec((TQ, D_MIX), lambda i: (i, 0)), pl.BlockSpec((D_MIX, D_MODEL), lambda i: (0, 0))],
        [_sds((t, GW))] * 5 + [_sds((t, D_MIX)), _sds((D_MIX, D_MODEL))],
    )(*outs, pg, wout, dy)


def loss_head(y, tgt):
    t = y.shape[0]

    def body(y_ref, t_ref, l_ref, dy_ref):
        diff = y_ref[...] - t_ref[...]
        dy_ref[...] = diff * (1.0 / D_MODEL)
        part = 0.5 * jnp.sum(jnp.sum(diff * diff, axis=-1, keepdims=True) * (1.0 / D_MODEL), axis=0, keepdims=True)
        _acc(l_ref, jnp.broadcast_to(part, (8, 128)), pl.program_id(0) == 0)

    big = pl.BlockSpec((TQ, D_MODEL), lambda i: (i, 0))
    return _call(body, "loss_head", (t // TQ,), [big, big], [pl.BlockSpec((8, 128), lambda i: (0, 0)), big],
                 [_sds((8, 128)), _sds((t, D_MODEL))])(y, tgt)


def layer_fwd(x2, mem2, p, layer, bl, s):
    tag = f"l{layer}"
    h, pa, pb, pc, pd, pe, pg, pf = inproj_fwd(x2, p["norm_g"], p["w_in"], layer, tag)
    qn, kn, vb, cq, ck = fox_prep_fwd(pa, pf, p["fox_q_norm"], p["fox_k_norm"], p["fox_f_bias"], bl, s, layer, tag)
    oa, lse = fox_attn_fwd(qn, kn, vb, cq, ck, bl, s, tag)
    ob = sb_attn_fwd(pb, bl, s, tag)
    oc, states = hgrn_fwd(pc, p["lb"], p["hgrn_out_norm"], bl, s, layer, tag)
    od = pool_fwd(pd, p["pool_wbd"], p["pool_scale"], bl, s, layer, tag)
    mk, mv = mem_prep_fwd(mem2, p["mem_norm_g"], p["mem_w_kv"], p["mem_k_norm"], bl, layer, tag)
    oe = mem_attn_fwd(pe, p["mem_q_norm"], mk, mv, bl, s, layer, tag)
    y = outproj_fwd(x2, (oa, ob, oc, od, oe), pg, p["w_out"], layer, tag)
    saved = dict(x2=x2, h=h, pa=pa, pb=pb, pc=pc, pd=pd, pe=pe, pg=pg, pf=pf, qn=qn, kn=kn, vb=vb, cq=cq, ck=ck,
                 oa=oa, lse=lse, ob=ob, oc=oc, states=states, od=od, mk=mk, mv=mv, oe=oe)
    return y, saved


def layer_bwd(dy, mem2, p, sv, layer, bl, s):
    tag = f"l{layer}"
    (doa, dob, doc, dod, doe, dg_gates, dwout) = outproj_bwd((sv["oa"], sv["ob"], sv["oc"], sv["od"], sv["oe"]), sv["pg"],
                                                              p["w_out"], dy, layer, tag)
    dqn, dkn, dv, dck = fox_attn_bwd(sv["qn"], sv["kn"], sv["vb"], sv["cq"], sv["ck"], sv["lse"], doa, bl, s, tag)
    d_a, d_f, dqw, dkw, dbias = fox_prep_bwd(sv["pa"], sv["pf"], p["fox_q_norm"], p["fox_k_norm"], p["fox_f_bias"], sv["cq"],
                                             dqn, dkn, dv, dck, bl, s, layer, tag)
    dsq, dsk, dsv = sb_attn_bwd(sv["pb"], dob, bl, s, tag)
    d_c, dlb, dwn = hgrn_bwd(sv["pc"], p["lb"], p["hgrn_out_norm"], sv["states"], doc, bl, s, layer, tag)
    d_d, dwbd, dpscale = pool_bwd(sv["pd"], p["pool_wbd"], p["pool_scale"], dod, bl, s, layer, tag)
    d_e, dmk, dmv, dmqw = mem_attn_bwd(sv["pe"], p["mem_q_norm"], sv["mk"], sv["mv"], doe, bl, s, layer, tag)
    dwk, dwv, dmg, dmkw = mem_prep_bwd(mem2, p["mem_norm_g"], p["mem_w_kv"], p["mem_k_norm"], dmk, dmv, bl, layer, tag)
    dpieces = (d_a, dsq, dsk, dsv, d_c, d_d, d_e, dg_gates, d_f)
    dx, dng = inproj_bwd_dx(sv["x2"], p["norm_g"], p["w_in"], dy, dpieces, layer, tag)
    dwin = jnp.concatenate([matmul_tn(sv["h"], dp, f"{tag}_{nm}") for (nm, _, _), dp in zip(BWD_PIECES, dpieces)], axis=1)
    grads = dict(norm_g=dng, w_in=dwin, fox_f_bias=dbias, fox_q_norm=dqw, fox_k_norm=dkw, lb=dlb, hgrn_out_norm=dwn,
                 pool_wbd=dwbd, pool_scale=dpscale, mem_norm_g=dmg, mem_w_kv=jnp.concatenate([dwk, dwv], axis=1),
                 mem_q_norm=dmqw, mem_k_norm=dmkw, w_out=dwout)
    return dx, grads


def _tile4(w):
    return jnp.tile(w, (1, NH))[:, None, :]


def prepare_params(norm_g, w_in_p, fox_f_bias, fox_q_norm, fox_k_norm, hgrn_lb_logits, hgrn_out_norm, pool_w, pool_scale,
                   mem_norm_g, mem_w_kv, mem_q_norm, mem_k_norm, w_out):
    p1 = jax.nn.sigmoid(hgrn_lb_logits[1] - hgrn_lb_logits[0])
    lb = jnp.stack([jnp.zeros_like(p1), jnp.clip(p1, 0.0, 1.0 - 1e-6)])
    eye = jnp.eye(4, dtype=F32)
    wbd = jnp.einsum("lgcd,gh->lgchd", pool_w, eye).reshape(2, GW, GW)
    return dict(norm_g=norm_g[:, None, :], w_in=w_in_p, fox_f_bias=jnp.pad(fox_f_bias, ((0, 0), (0, 124)))[:, None, :],
                fox_q_norm=_tile4(fox_q_norm), fox_k_norm=_tile4(fox_k_norm), lb=lb[:, None, :],
                hgrn_out_norm=hgrn_out_norm[:, None, :], pool_wbd=wbd, pool_scale=pool_scale[:, None, :],
                mem_norm_g=mem_norm_g[:, None, :], mem_w_kv=mem_w_kv, mem_q_norm=_tile4(mem_q_norm),
                mem_k_norm=_tile4(mem_k_norm), w_out=w_out)


def local_step(x, mem, tgt, p):
    bl, s, _ = x.shape
    x2, mem2, tgt2 = x.reshape(bl * s, D_MODEL), mem.reshape(bl * N_MEM, D_MODEL), tgt.reshape(bl * s, D_MODEL)
    y0, sv0 = layer_fwd(x2, mem2, p, 0, bl, s)
    y1, sv1 = layer_fwd(y0, mem2, p, 1, bl, s)
    lpart, dy = loss_head(y1, tgt2)
    dx1, g1 = layer_bwd(dy, mem2, p, sv1, 1, bl, s)
    dx0, g0 = layer_bwd(dx1, mem2, p, sv0, 0, bl, s)
    return lpart[0, 0], dx0.reshape(bl, s, D_MODEL), g0, g1


_ANY = pl.BlockSpec(memory_space=pl.ANY)


def _me_and_peers():
    x, y, c = lax.axis_index("x"), lax.axis_index("y"), lax.axis_index("c")
    peers = []
    for k in range(1, N_DEV):
        px = 1 - x if (k >> 2) & 1 else x
        py = 1 - y if (k >> 1) & 1 else y
        pc = 1 - c if k & 1 else c
        peers.append(((px, py, pc), 4 * px + 2 * py + pc))
    return 4 * x + 2 * y + c, peers


def all_gather_rows(xs, tag):
    nl, r, c = xs.shape

    def body(x_ref, o_ref, send_sems, recv_sems, local_sem):
        me, peers = _me_and_peers()
        rows = lambda idx: o_ref.at[:, pl.ds(idx * r, r), :]
        mine = pltpu.make_async_copy(x_ref, rows(me), local_sem)
        mine.start()
        copies = [pltpu.make_async_remote_copy(src_ref=x_ref, dst_ref=rows(me), send_sem=send_sems.at[k], recv_sem=recv_sems.at[k],
                                               device_id=dev, device_id_type=pl.DeviceIdType.MESH)
                  for k, (dev, _) in enumerate(peers)]
        for cp in copies:
            cp.start()
        for cp in copies:
            cp.wait()
        mine.wait()

    return pl.pallas_call(
        body, name=f"all_gather_{tag}", in_specs=[_ANY], out_specs=_ANY, out_shape=_sds((nl, N_DEV * r, c), xs.dtype),
        scratch_shapes=[pltpu.SemaphoreType.DMA((N_DEV - 1,)), pltpu.SemaphoreType.DMA((N_DEV - 1,)), pltpu.SemaphoreType.DMA],
    )(xs)


def exchange_row_blocks(part, tag):
    nl, r8, c = part.shape
    r = r8 // N_DEV

    def body(p_ref, o_ref, send_sems, recv_sems, local_sem):
        me, peers = _me_and_peers()
        rows = lambda idx: p_ref.at[:, pl.ds(idx * r, r), :]
        mine = pltpu.make_async_copy(rows(me), o_ref.at[0], local_sem)
        mine.start()
        copies = [pltpu.make_async_remote_copy(src_ref=rows(idx), dst_ref=o_ref.at[k + 1], send_sem=send_sems.at[k],
                                               recv_sem=recv_sems.at[k], device_id=dev, device_id_type=pl.DeviceIdType.MESH)
                  for k, (dev, idx) in enumerate(peers)]
        for cp in copies:
            cp.start()
        for cp in copies:
            cp.wait()
        mine.wait()

    return pl.pallas_call(
        body, name=f"exchange_{tag}", in_specs=[_ANY], out_specs=_ANY, out_shape=_sds((N_DEV, nl, r, c), part.dtype),
        scratch_shapes=[pltpu.SemaphoreType.DMA((N_DEV - 1,)), pltpu.SemaphoreType.DMA((N_DEV - 1,)), pltpu.SemaphoreType.DMA],
    )(part)


def _row_tile(rows):
    if rows <= 512 and rows % 64:
        return rows
    for t in (64, 40, 32, 16, 8):
        if rows % t == 0:
            return t
    return rows


def sum_slots(slots, tag):
    _, rows, c = slots.shape
    tr = _row_tile(rows)

    def body(s_ref, o_ref):
        acc = s_ref[0]
        for k in range(1, N_DEV):
            acc = acc + s_ref[k]
        o_ref[...] = acc

    return _call(body, f"sum_slots_{tag}", (rows // tr,), [pl.BlockSpec((N_DEV, tr, c), lambda i: (0, i, 0))],
                 pl.BlockSpec((tr, c), lambda i: (i, 0)), _sds((rows, c)))(slots)


def _adamw(w, g, m, v):
    m = ADAM_B1 * m + (1.0 - ADAM_B1) * g
    v = ADAM_B2 * v + (1.0 - ADAM_B2) * (g * g)
    m_hat = m / (1.0 - ADAM_B1 ** ADAM_STEP)
    v_hat = v / (1.0 - ADAM_B2 ** ADAM_STEP)
    delta = -ADAM_LR * (m_hat / (jnp.sqrt(v_hat) + ADAM_EPS) + ADAM_WD * w)
    return delta, m, v


def adam_update(w, m, v, g, tag, slots=False):
    rows, c = w.shape
    tr = _row_tile(rows)

    def body(w_ref, m_ref, v_ref, g_ref, go_ref, d_ref, mo_ref, vo_ref):
        if slots:
            g = g_ref[0]
            for k in range(1, N_DEV):
                g = g + g_ref[k]
        else:
            g = g_ref[...]
        d, mn, vn = _adamw(w_ref[...], g, m_ref[...], v_ref[...])
        go_ref[...] = g
        d_ref[...] = d
        mo_ref[...] = mn
        vo_ref[...] = vn

    blk = pl.BlockSpec((tr, c), lambda i: (i, 0))
    gspec = pl.BlockSpec((N_DEV, tr, c), lambda i: (0, i, 0)) if slots else blk
    return _call(body, f"adam_{tag}", (rows // tr,), [blk, blk, blk, gspec], [blk] * 4, [_sds((rows, c))] * 4)(w, m, v, g)


_SMALL = (("norm_g", (2, 1024)), ("fox_f_bias", (2, 4)), ("fox_q_norm", (2, 64)), ("fox_k_norm", (2, 64)),
          ("hgrn_lb_logits", (2, 256)), ("hgrn_out_norm", (2, 256)), ("pool_w", (2, 4, 64, 64)), ("pool_scale", (2, 256)),
          ("mem_norm_g", (2, 1024)), ("mem_q_norm", (2, 64)), ("mem_k_norm", (2, 64)))
_SLAB_ROWS = 312


def pack_small(d):
    flat = jnp.concatenate([d[n].reshape(-1) for n, _ in _SMALL])
    return jnp.pad(flat, (0, _SLAB_ROWS * 128 - flat.shape[0])).reshape(_SLAB_ROWS, 128)


def unpack_small(slab):
    flat, out, off = slab.reshape(-1), {}, 0
    for n, shp in _SMALL:
        size = 1
        for e in shp:
            size *= e
        out[n] = flat[off:off + size].reshape(shp)
        off += size
    return out


def small_grads(g0, g1, lb_logits):
    st = lambda f: jnp.stack([f(g0), f(g1)])
    heads = lambda a: a.reshape(NH, HD).sum(0)
    p1 = jax.nn.sigmoid(lb_logits[1] - lb_logits[0])
    inside = (p1 > 0.0) & (p1 < 1.0 - 1e-6)
    dl1 = jnp.where(inside, g1["lb"][0] * p1 * (1.0 - p1), 0.0)
    diag = lambda a: jnp.stack([a.reshape(4, HD, 4, HD)[i, :, i, :] for i in range(4)])
    return dict(norm_g=st(lambda g: g["norm_g"][0]), fox_f_bias=st(lambda g: g["fox_f_bias"][0, :NH]),
                fox_q_norm=st(lambda g: heads(g["fox_q_norm"])), fox_k_norm=st(lambda g: heads(g["fox_k_norm"])),
                hgrn_lb_logits=jnp.stack([-dl1, dl1]), hgrn_out_norm=st(lambda g: g["hgrn_out_norm"][0]),
                pool_w=st(lambda g: diag(g["pool_wbd"])), pool_scale=st(lambda g: g["pool_scale"][0]),
                mem_norm_g=st(lambda g: g["mem_norm_g"][0]), mem_q_norm=st(lambda g: heads(g["mem_q_norm"])),
                mem_k_norm=st(lambda g: heads(g["mem_k_norm"])))


def kernel(x, mem, norm_g, w_in, fox_f_bias, fox_q_norm, fox_k_norm, hgrn_lb_logits, hgrn_out_norm, pool_w, pool_scale, mem_norm_g, mem_w_kv, mem_q_norm, mem_k_norm, w_out, loss_target, m_norm_g, m_w_in, m_fox_f_bias, m_fox_q_norm, m_fox_k_norm, m_hgrn_lb_logits, m_hgrn_out_norm, m_pool_w, m_pool_scale, m_mem_norm_g, m_mem_w_kv, m_mem_q_norm, m_mem_k_norm, m_w_out, v_norm_g, v_w_in, v_fox_f_bias, v_fox_q_norm, v_fox_k_norm, v_hgrn_lb_logits, v_hgrn_out_norm, v_pool_w, v_pool_scale, v_mem_norm_g, v_mem_w_kv, v_mem_q_norm, v_mem_k_norm, v_w_out):
    given = dict(norm_g=(norm_g, m_norm_g, v_norm_g), w_in=(w_in, m_w_in, v_w_in), fox_f_bias=(fox_f_bias, m_fox_f_bias, v_fox_f_bias),
                 fox_q_norm=(fox_q_norm, m_fox_q_norm, v_fox_q_norm), fox_k_norm=(fox_k_norm, m_fox_k_norm, v_fox_k_norm),
                 hgrn_lb_logits=(hgrn_lb_logits, m_hgrn_lb_logits, v_hgrn_lb_logits),
                 hgrn_out_norm=(hgrn_out_norm, m_hgrn_out_norm, v_hgrn_out_norm), pool_w=(pool_w, m_pool_w, v_pool_w),
                 pool_scale=(pool_scale, m_pool_scale, v_pool_scale), mem_norm_g=(mem_norm_g, m_mem_norm_g, v_mem_norm_g),
                 mem_w_kv=(mem_w_kv, m_mem_w_kv, v_mem_w_kv), mem_q_norm=(mem_q_norm, m_mem_q_norm, v_mem_q_norm),
                 mem_k_norm=(mem_k_norm, m_mem_k_norm, v_mem_k_norm), w_out=(w_out, m_w_out, v_w_out))
    order = ("norm_g", "w_in", "fox_f_bias", "fox_q_norm", "fox_k_norm", "hgrn_lb_logits", "hgrn_out_norm", "pool_w",
             "pool_scale", "mem_norm_g", "mem_w_kv", "mem_q_norm", "mem_k_norm", "w_out")

    w_in_full = all_gather_rows(_cast(permute_cols(w_in)), "w_in")
    w_out_full = all_gather_rows(_cast(w_out), "w_out")
    w_kv_full = all_gather_rows(_cast(mem_w_kv), "w_kv")
    p = prepare_params(norm_g, w_in_full, fox_f_bias, fox_q_norm, fox_k_norm, hgrn_lb_logits, hgrn_out_norm, pool_w,
                       pool_scale, mem_norm_g, w_kv_full, mem_q_norm, mem_k_norm, w_out_full)

    loss_part, grad_x, g0, g1 = local_step(x, mem, loss_target, p)
    loss = lax.psum(loss_part, ("x", "y", "c"))

    res = {}

    def sharded(name, g2, unperm=False):
        w, m, v = given[name]
        nl, r, c = w.shape
        slots = exchange_row_blocks(g2, name)
        if unperm:
            g = sum_slots(slots.reshape(N_DEV, nl * r, slots.shape[-1]), name).reshape(nl, r, -1)
            out = adam_update(w.reshape(nl * r, c), m.reshape(nl * r, c), v.reshape(nl * r, c),
                              unpermute_cols(g).reshape(nl * r, c), name)
        else:
            out = adam_update(w.reshape(nl * r, c), m.reshape(nl * r, c), v.reshape(nl * r, c),
                              slots.reshape(N_DEV, nl * r, c), name, slots=True)
        res[name] = tuple(o.reshape(nl, r, c) for o in out)

    sharded("w_in", jnp.stack([g0["w_in"], g1["w_in"]]), unperm=True)
    sharded("w_out", jnp.stack([g0["w_out"], g1["w_out"]]))
    sharded("mem_w_kv", jnp.stack([g0["mem_w_kv"], g1["mem_w_kv"]]))

    gsmall = pack_small(small_grads(g0, g1, hgrn_lb_logits))
    gathered = all_gather_rows(gsmall[None], "small").reshape(N_DEV, _SLAB_ROWS, 128)
    slabs = adam_update(*[pack_small({n: given[n][j] for n, _ in _SMALL}) for j in range(3)], gathered, "small", slots=True)
    small = [unpack_small(sl) for sl in slabs]
    for n, _ in _SMALL:
        res[n] = tuple(small[j][n] for j in range(4))

    return (loss, grad_x, *[res[n][0] for n in order], *[res[n][1] for n in order], *[res[n][2] for n in order],
            *[res[n][3] for n in order])
```

```python
import functools

import jax
import jax.numpy as jnp
from jax import lax
from jax.experimental import pallas as pl
from jax.experimental.pallas import tpu as pltpu

F32 = jnp.float32
BF = jnp.bfloat16
_MMT = BF

D_MODEL = 1024
GW = 256
HD = 64
NH = 4
CH = 64
N_MEM = 256
D_IN = 4100
D_INP = 4224
D_MIX = 1280
EPS = 1e-6
NEG_BIG = -1e30
LB_FLOOR = 1e-30
SCALE = HD ** -0.5
TQ = 256
N_DEV = 8
VMEM_LIMIT_BYTES = 56 * 1024 * 1024

ADAM_LR = 0.001
ADAM_B1 = 0.9
ADAM_B2 = 0.999
ADAM_EPS = 1e-08
ADAM_WD = 0.01
ADAM_STEP = 10

PIECES = (("A", 0, 768), ("B", 768, 768), ("C", 1536, 768), ("D", 2304, 256), ("E", 2560, 256),
          ("G", 2816, 1280), ("F", 4096, 128))
BWD_PIECES = (("A", 0, 768), ("Bq", 768, 256), ("Bk", 1024, 256), ("Bv", 1280, 256), ("C", 1536, 768),
              ("D", 2304, 256), ("E", 2560, 256), ("G", 2816, 1280), ("F", 4096, 128))
_ORIG = dict(fq=(0, 256), fk=(256, 512), fv=(512, 768), fg=(768, 1024), ff=(1024, 1028), sq=(1028, 1284),
             sk=(1284, 1540), sv=(1540, 1796), sg=(1796, 2052), hq=(2052, 2308), hf=(2308, 2564),
             hi=(2564, 2820), hg=(2820, 3076), pv=(3076, 3332), pg=(3332, 3588), mq=(3588, 3844), mg=(3844, 4100))
_PERM_ORDER = ("fq", "fk", "fv", "sq", "sk", "sv", "hq", "hf", "hi", "pv", "mq", "fg", "sg", "hg", "pg", "mg", "ff")
_ORIG_ORDER = ("fq", "fk", "fv", "fg", "ff", "sq", "sk", "sv", "sg", "hq", "hf", "hi", "hg", "pv", "pg", "mq", "mg")


def permute_cols(w):
    parts = [w[..., _ORIG[n][0]:_ORIG[n][1]] for n in _PERM_ORDER]
    parts.append(jnp.zeros(w.shape[:-1] + (D_INP - D_IN,), w.dtype))
    return jnp.concatenate(parts, axis=-1)


def unpermute_cols(g):
    start, off = {}, 0
    for n in _PERM_ORDER:
        start[n] = off
        off += _ORIG[n][1] - _ORIG[n][0]
    return jnp.concatenate([g[..., start[n]:start[n] + _ORIG[n][1] - _ORIG[n][0]] for n in _ORIG_ORDER], axis=-1)


def _cast(a):
    return a.astype(_MMT)


def _dg(a, b, ca, cb):
    return lax.dot_general(a, b, (((ca,), (cb,)), ((), ())), preferred_element_type=F32)


@jax.custom_vjp
def mm(a, b):
    return _dg(_cast(a), _cast(b), 1, 0)


@jax.custom_vjp
def mm_nt(a, b):
    return _dg(_cast(a), _cast(b), 1, 1)


@jax.custom_vjp
def mm_tn(a, b):
    return _dg(_cast(a), _cast(b), 0, 0)


mm.defvjp(lambda a, b: (mm(a, b), (a, b)),
          lambda r, g: (mm_nt(g, r[1]).astype(r[0].dtype), mm_tn(r[0], g).astype(r[1].dtype)))
mm_nt.defvjp(lambda a, b: (mm_nt(a, b), (a, b)),
             lambda r, g: (mm(g, r[1]).astype(r[0].dtype), mm_tn(g, r[0]).astype(r[1].dtype)))
mm_tn.defvjp(lambda a, b: (mm_tn(a, b), (a, b)),
             lambda r, g: (mm_nt(r[1], g).astype(r[0].dtype), mm(r[0], g).astype(r[1].dtype)))


def _split(a):
    hi = a.astype(_MMT)
    lo = (a - hi.astype(F32)).astype(_MMT)
    return hi, lo


@jax.custom_vjp
def xr(a, c, ct):
    hi, lo = _split(a)
    cc = _cast(c)
    return _dg(hi, cc, 1, 0) + _dg(lo, cc, 1, 0)


@jax.custom_vjp
def xl(c, ct, a):
    hi, lo = _split(a)
    cc = _cast(c)
    return _dg(cc, hi, 1, 0) + _dg(cc, lo, 1, 0)


xr.defvjp(lambda a, c, ct: (xr(a, c, ct), (c, ct)),
          lambda r, g: (xr(g, r[1], r[0]), jnp.zeros_like(r[0]), jnp.zeros_like(r[1])))
xl.defvjp(lambda c, ct, a: (xl(c, ct, a), (c, ct)),
          lambda r, g: (jnp.zeros_like(r[0]), jnp.zeros_like(r[1]), xl(r[1], r[0], g)))


def _iota(shape, dim):
    return lax.broadcasted_iota(jnp.int32, shape, dim)


def _hmask(h, n=GW):
    lane = _iota((1, n), 1)
    return ((lane >= h * HD) & (lane < (h + 1) * HD)).astype(F32)


def _bdmask(n=GW):
    return ((_iota((n, n), 0) >> 6) == (_iota((n, n), 1) >> 6)).astype(F32)


def _tri(n, kind="le"):
    r, c = _iota((n, n), 0), _iota((n, n), 1)
    return {"le": c <= r, "ge": c >= r, "gt": c > r, "lt": c < r}[kind].astype(F32)


def _onehot_lane(h, n=128):
    return (_iota((1, n), 1) == h).astype(F32)


def _logsig(x):
    return jnp.minimum(x, 0.0) - jnp.log1p(jnp.exp(-jnp.abs(x)))


def _sigmoid(x):
    return 0.5 * (jnp.tanh(0.5 * x) + 1.0)


def _silu(x):
    return x * _sigmoid(x)


def _rms(x, g):
    return x * lax.rsqrt(jnp.mean(x * x, axis=-1, keepdims=True) + EPS) * g


def _headrms(x, w, bd64):
    ms = xr(x * x, bd64, bd64)
    return x * lax.rsqrt(ms + EPS) * w


def _call(body, name, grid, in_specs, out_specs, out_shape, scratch=()):
    return pl.pallas_call(
        body, name=name, grid=grid, in_specs=in_specs, out_specs=out_specs, out_shape=out_shape,
        scratch_shapes=list(scratch),
        compiler_params=pltpu.CompilerParams(dimension_semantics=("arbitrary",) * len(grid),
                                             vmem_limit_bytes=VMEM_LIMIT_BYTES))


def _sds(shape, dtype=F32):
    return jax.ShapeDtypeStruct(shape, dtype)


def _acc(ref, val, first):
    @pl.when(first)
    def _():
        ref[...] = val

    @pl.when(jnp.logical_not(first))
    def _():
        ref[...] += val


def inproj_fwd(x2, g, w, layer, tag):
    t = x2.shape[0]

    def body(x_ref, g_ref, w_ref, ht_ref, *outs):
        h = _rms(x_ref[...], g_ref[...])
        hb = _cast(h)
        ht_ref[...] = _cast(h.T)
        for (_, c0, wd), o in zip(PIECES, outs):
            o[...] = _dg(hb, _cast(w_ref[:, c0:c0 + wd]), 1, 0)

    return _call(
        body, f"inproj_fwd_{tag}", (t // TQ,),
        [pl.BlockSpec((TQ, D_MODEL), lambda i: (i, 0)),
         pl.BlockSpec((None, 1, D_MODEL), lambda i: (layer, 0, 0)),
         pl.BlockSpec((None, D_MODEL, D_INP), lambda i: (layer, 0, 0))],
        [pl.BlockSpec((D_MODEL, TQ), lambda i: (0, i))] + [pl.BlockSpec((TQ, wd), lambda i: (i, 0)) for _, _, wd in PIECES],
        [_sds((D_MODEL, t), _MMT)] + [_sds((t, wd)) for _, _, wd in PIECES],
    )(x2, g, w)


def inproj_bwd_dx(x2, g, w, dy, dpieces, layer, tag):
    t = x2.shape[0]

    def body(x_ref, g_ref, w_ref, dy_ref, *rest):
        dps, (dx_ref, dg_ref) = rest[:len(BWD_PIECES)], rest[len(BWD_PIECES):]
        dh = None
        for (_, c0, wd), dp in zip(BWD_PIECES, dps):
            part = _dg(_cast(dp[...]), _cast(w_ref[:, c0:c0 + wd]), 1, 1)
            dh = part if dh is None else dh + part
        _, vjp = jax.vjp(_rms, x_ref[...], g_ref[...])
        dx, dg = vjp(dh)
        dx_ref[...] = dy_ref[...] + dx
        _acc(dg_ref, dg, pl.program_id(0) == 0)

    return _call(
        body, f"inproj_bwd_dx_{tag}", (t // TQ,),
        [pl.BlockSpec((TQ, D_MODEL), lambda i: (i, 0)),
         pl.BlockSpec((None, 1, D_MODEL), lambda i: (layer, 0, 0)),
         pl.BlockSpec((None, D_MODEL, D_INP), lambda i: (layer, 0, 0)),
         pl.BlockSpec((TQ, D_MODEL), lambda i: (i, 0))] + [pl.BlockSpec((TQ, wd), lambda i: (i, 0)) for _, _, wd in BWD_PIECES],
        [pl.BlockSpec((TQ, D_MODEL), lambda i: (i, 0)), pl.BlockSpec((1, D_MODEL), lambda i: (0, 0))],
        [_sds((t, D_MODEL)), _sds((1, D_MODEL))],
    )(x2, g, w, dy, *dpieces)


def matmul_acc(at, b, tag, tk=512):
    m, t = at.shape
    n = b.shape[1]
    tn = n if n <= 512 else (256 if n % 256 == 0 else 128)
    if n == 768:
        tn = 384

    def body(a_ref, b_ref, o_ref):
        _acc(o_ref, _dg(_cast(a_ref[...]), _cast(b_ref[...]), 1, 0), pl.program_id(1) == 0)

    return _call(
        body, f"matmul_acc_{tag}", (n // tn, t // tk),
        [pl.BlockSpec((m, tk), lambda j, i: (0, i)), pl.BlockSpec((tk, tn), lambda j, i: (i, j))],
        pl.BlockSpec((m, tn), lambda j, i: (0, j)),
        _sds((m, n)),
    )(at, b)


def _fox_prep_fn(q, k, ff, qw, kw, bias, carry, bd64, tri, trit, last):
    qn = _headrms(q, qw, bd64)
    kn = _headrms(k, kw, bd64)
    lf = _logsig(ff + bias)
    c = xl(tri, trit, lf) + carry
    return qn, kn, c, jnp.sum(c * last, axis=0, keepdims=True)


def _prep_consts():
    return _bdmask() * (1.0 / HD), _tri(TQ), _tri(TQ, "ge"), (_iota((TQ, 1), 0) == TQ - 1).astype(F32)


def fox_prep_fwd(pa, pf, qw, kw, bias, bl, s, layer, tag):
    nq = s // TQ

    def body(q_ref, k_ref, v_ref, f_ref, qw_ref, kw_ref, b_ref, qn_ref, kn_ref, vb_ref, cq_ref, ck_ref, carry):
        @pl.when(pl.program_id(1) == 0)
        def _():
            carry[...] = jnp.zeros_like(carry)

        qn, kn, c, cl = _fox_prep_fn(q_ref[...], k_ref[...], f_ref[...], qw_ref[...], kw_ref[...], b_ref[...],
                                     carry[...], *_prep_consts())
        carry[...] = cl
        qn_ref[...] = _cast(qn)
        kn_ref[...] = _cast(kn)
        vb_ref[...] = _cast(v_ref[...])
        cq_ref[...] = c
        ck_ref[...] = c.T[0:8, :]

    tok = lambda j: pl.BlockSpec((TQ, GW), lambda b, i: (b * nq + i, j))
    par = lambda n: pl.BlockSpec((None, 1, n), lambda b, i: (layer, 0, 0))
    return _call(
        body, f"fox_prep_fwd_{tag}", (bl, nq),
        [tok(0), tok(1), tok(2), pl.BlockSpec((TQ, 128), lambda b, i: (b * nq + i, 0)), par(GW), par(GW), par(128)],
        [tok(0), tok(0), tok(0), pl.BlockSpec((TQ, 128), lambda b, i: (b * nq + i, 0)),
         pl.BlockSpec((None, 8, TQ), lambda b, i: (b, 0, i))],
        [_sds((bl * s, GW), _MMT)] * 3 + [_sds((bl * s, 128)), _sds((bl, 8, s))],
        [pltpu.VMEM((1, 128), F32)],
    )(pa, pa, pa, pf, qw, kw, bias)


def fox_prep_bwd(pa, pf, qw, kw, bias, cq, dqn, dkn, dv, dck, bl, s, layer, tag):
    nq = s // TQ

    def body(q_ref, k_ref, f_ref, qw_ref, kw_ref, b_ref, cq_ref, cprev_ref, dqn_ref, dkn_ref, dv_ref, dck_ref,
             da_ref, df_ref, dqw_ref, dkw_ref, db_ref, dcarry):
        i = pl.program_id(1)
        first = jnp.logical_and(pl.program_id(0) == 0, i == 0)

        @pl.when(i == 0)
        def _():
            dcarry[...] = jnp.zeros_like(dcarry)

        last = (_iota((TQ, 1), 0) == TQ - 1).astype(F32)
        carry_in = jnp.where(i == nq - 1, 0.0, jnp.sum(cprev_ref[...] * last, axis=0, keepdims=True))
        consts = _prep_consts()
        _, vjp = jax.vjp(lambda *a: _fox_prep_fn(*a, *consts), q_ref[...], k_ref[...], f_ref[...], qw_ref[...],
                         kw_ref[...], b_ref[...], carry_in)
        dc = dck_ref[...].T
        dq, dk, dff, dqw, dkw, dbias, dcin = vjp((dqn_ref[...], dkn_ref[...], dc, dcarry[...]))
        dcarry[...] = dcin
        da_ref[:, 0:GW] = dq
        da_ref[:, GW:2 * GW] = dk
        da_ref[:, 2 * GW:3 * GW] = dv_ref[...]
        df_ref[...] = dff
        _acc(dqw_ref, dqw, first)
        _acc(dkw_ref, dkw, first)
        _acc(db_ref, dbias, first)

    rv = lambda b, i: b * nq + (nq - 1 - i)
    tok = lambda j: pl.BlockSpec((TQ, GW), lambda b, i: (rv(b, i), j))
    tok0 = pl.BlockSpec((TQ, GW), lambda b, i: (rv(b, i), 0))
    t128 = pl.BlockSpec((TQ, 128), lambda b, i: (rv(b, i), 0))
    prev = pl.BlockSpec((TQ, 128), lambda b, i: (jnp.maximum(rv(b, i) - 1, 0), 0))
    par = lambda n: pl.BlockSpec((None, 1, n), lambda b, i: (layer, 0, 0))
    acc = lambda n: pl.BlockSpec((1, n), lambda b, i: (0, 0))
    return _call(
        body, f"fox_prep_bwd_{tag}", (bl, nq),
        [tok(0), tok(1), t128, par(GW), par(GW), par(128), t128, prev, tok0, tok0, tok0,
         pl.BlockSpec((None, 128, TQ), lambda b, i: (b, 0, nq - 1 - i))],
        [pl.BlockSpec((TQ, 3 * GW), lambda b, i: (rv(b, i), 0)), t128, acc(GW), acc(GW), acc(128)],
        [_sds((bl * s, 3 * GW)), _sds((bl * s, 128)), _sds((1, GW)), _sds((1, GW)), _sds((1, 128))],
        [pltpu.VMEM((1, 128), F32)],
    )(pa, pa, pf, qw, kw, bias, cq, cq, dqn, dkn, dv, dck)


def _lane_pick(x, h):
    return jnp.sum(x * _onehot_lane(h), axis=-1, keepdims=True)


TA = 128
SROWS = NH * TA


def _stack_heads(x, scale=1.0):
    return _cast(jnp.concatenate([x * (_hmask(h) * scale) for h in range(NH)], axis=0))


def _stack_cols(x):
    return jnp.concatenate([_lane_pick(x, h) for h in range(NH)], axis=0)


def _spread_heads(col):
    return sum(col[h * TA:(h + 1) * TA] * _hmask(h) for h in range(NH))


def _lanes_cat(w):
    return jnp.concatenate([w[h * TA:(h + 1) * TA] for h in range(NH)], axis=1)


def _mask_stack(x):
    return _cast(jnp.concatenate([x * _hmask(h).astype(x.dtype) for h in range(NH)], axis=0))


def _stack_rows(i):
    return i * TA + (_iota((SROWS, 1), 0) & (TA - 1))


def _n_key_tiles(i):
    return lax.shift_right_logical(i * TA, TQ.bit_length() - 1) + 1


def fox_attn_fwd(qn, kn, vb, cq, ck, bl, s, tag):
    nq = s // TA

    def body(q_ref, k_ref, v_ref, cq_ref, ck_ref, o_ref, lse_ref, acc):
        i = pl.program_id(1)
        qs = _stack_heads(q_ref[...].astype(F32), SCALE)
        cqs = _stack_cols(cq_ref[...])
        row = _stack_rows(i)
        acc[...] = jnp.zeros_like(acc)

        def step(j, ml):
            m, l = ml
            ks = pl.ds(pl.multiple_of(j * TQ, TQ), TQ)
            ckb = jnp.concatenate([jnp.broadcast_to(ck_ref[h:h + 1, ks], (TA, TQ)) for h in range(NH)], axis=0)
            sc = _dg(qs, k_ref[ks, :], 1, 1) + cqs - ckb
            col = j * TQ + _iota((1, TQ), 1)
            sc = jnp.where(col <= row, sc, NEG_BIG)
            m_new = jnp.maximum(m, jnp.max(sc, axis=-1, keepdims=True))
            alpha = jnp.exp(m - m_new)
            p = jnp.exp(sc - m_new)
            acc[...] = _spread_heads(alpha) * acc[...] + _dg(_lanes_cat(_cast(p)), _mask_stack(v_ref[ks, :]), 1, 0)
            return m_new, alpha * l + jnp.sum(p, axis=-1, keepdims=True)

        m, l = lax.fori_loop(0, _n_key_tiles(i), step, (jnp.full((SROWS, 1), NEG_BIG, F32), jnp.zeros((SROWS, 1), F32)))
        o_ref[...] = acc[...] / _spread_heads(l)
        lse_h = m + jnp.log(l)
        lse_ref[...] = sum(lse_h[h * TA:(h + 1) * TA] * _onehot_lane(h) for h in range(NH))

    tok = pl.BlockSpec((TA, GW), lambda b, i: (b * nq + i, 0))
    seq = pl.BlockSpec((s, GW), lambda b, i: (b, 0))
    t128 = pl.BlockSpec((TA, 128), lambda b, i: (b * nq + i, 0))
    return _call(
        body, f"fox_attn_fwd_{tag}", (bl, nq),
        [tok, seq, seq, t128, pl.BlockSpec((None, 8, s), lambda b, i: (b, 0, 0))],
        [tok, t128], [_sds((bl * s, GW)), _sds((bl * s, 128))],
        [pltpu.VMEM((TA, GW), F32)],
    )(qn, kn, vb, cq, ck)


def fox_attn_bwd(qn, kn, vb, cq, ck, lse, do, bl, s, tag):
    nq = s // TA

    def body(q_ref, k_ref, v_ref, cq_ref, ck_ref, lse_ref, do_ref, dq_ref, dk_ref, dv_ref, dck_ref, dqa, p_s, dp_s):
        i = pl.program_id(1)

        @pl.when(i == 0)
        def _():
            dk_ref[...] = jnp.zeros_like(dk_ref)
            dv_ref[...] = jnp.zeros_like(dv_ref)
            dck_ref[...] = jnp.zeros_like(dck_ref)

        qs = _stack_heads(q_ref[...].astype(F32), SCALE)
        dos = _stack_heads(do_ref[...])
        cqs, lses = _stack_cols(cq_ref[...]), _stack_cols(lse_ref[...])
        row = _stack_rows(i)
        dqa[...] = jnp.zeros_like(dqa)
        nk = _n_key_tiles(i)

        def probs(j, delta):
            ks = pl.ds(pl.multiple_of(j * TQ, TQ), TQ)
            ckb = jnp.concatenate([jnp.broadcast_to(ck_ref[h:h + 1, ks], (TA, TQ)) for h in range(NH)], axis=0)
            sc = _dg(qs, k_ref[ks, :], 1, 1) + cqs - ckb
            col = j * TQ + _iota((1, TQ), 1)
            p = jnp.where(col <= row, jnp.exp(sc - lses), 0.0)
            dp = _dg(dos, v_ref[ks, :], 1, 1)
            p_s[:, ks] = p
            dp_s[:, ks] = dp
            return delta + jnp.sum(p * dp, axis=-1, keepdims=True)

        delta = lax.fori_loop(0, nk, probs, jnp.zeros((SROWS, 1), F32))

        def step(j, carry):
            ks = pl.ds(pl.multiple_of(j * TQ, TQ), TQ)
            p = p_s[:, ks]
            ds = p * (dp_s[:, ks] - delta)
            dsb = _cast(ds)
            dqa[...] += _dg(_lanes_cat(dsb), _mask_stack(k_ref[ks, :]), 1, 0) * SCALE
            dk_ref[ks, :] += _dg(dsb, qs, 0, 0)
            dv_ref[ks, :] += _dg(_cast(p), dos, 0, 0)
            for h in range(NH):
                dck_ref[h:h + 1, ks] -= jnp.sum(ds[h * TA:(h + 1) * TA], axis=0, keepdims=True)
            return carry

        lax.fori_loop(0, nk, step, 0)
        dq_ref[...] = dqa[...]

    tok = pl.BlockSpec((TA, GW), lambda b, i: (b * nq + i, 0))
    seq = pl.BlockSpec((s, GW), lambda b, i: (b, 0))
    t128 = pl.BlockSpec((TA, 128), lambda b, i: (b * nq + i, 0))
    return _call(
        body, f"fox_attn_bwd_{tag}", (bl, nq),
        [tok, seq, seq, t128, pl.BlockSpec((None, 8, s), lambda b, i: (b, 0, 0)), t128, tok],
        [tok, seq, seq, pl.BlockSpec((None, 128, s), lambda b, i: (b, 0, 0))],
        [_sds((bl * s, GW)), _sds((bl * s, GW)), _sds((bl * s, GW)), _sds((bl, 128, s))],
        [pltpu.VMEM((TA, GW), F32), pltpu.VMEM((SROWS, s), F32), pltpu.VMEM((SROWS, s), F32)],
    )(qn, kn, vb, cq, ck, lse, do)


def _sb_block(qh, kb, row, col, upper, r_carry):
    z = _dg(qh, kb, 1, 1)
    valid = col < row
    ls = _logsig(z)
    lom = jnp.where(valid, ls - z, 0.0)
    between = xr(lom, upper, upper) + r_carry
    w = jnp.where(valid, jnp.exp(ls + between), 0.0)
    return z, ls, lom, w, valid


def sb_attn_fwd(pb, bl, s, tag):
    nq = s // TA

    def body(q_ref, k_ref, v_ref, o_ref, acc):
        i = pl.program_id(1)
        qs = _stack_heads(q_ref[...], SCALE)
        row = _stack_rows(i)
        upper = _tri(TQ, "lt")
        acc[...] = jnp.zeros_like(acc)
        last = _n_key_tiles(i) - 1

        def step(jj, r):
            j = last - jj
            ks = pl.ds(pl.multiple_of(j * TQ, TQ), TQ)
            col = j * TQ + _iota((1, TQ), 1)
            _, _, lom, w, _ = _sb_block(qs, _cast(k_ref[ks, :]), row, col, upper, r)
            acc[...] += _dg(_lanes_cat(_cast(w)), _mask_stack(v_ref[ks, :]), 1, 0)
            return r + jnp.sum(lom, axis=-1, keepdims=True)

        lax.fori_loop(0, last + 1, step, jnp.zeros((SROWS, 1), F32))
        o_ref[...] = acc[...]

    tok = lambda j: pl.BlockSpec((TA, GW), lambda b, i: (b * nq + i, j))
    seq = lambda j: pl.BlockSpec((s, GW), lambda b, i: (b, j))
    return _call(
        body, f"sb_attn_fwd_{tag}", (bl, nq), [tok(0), seq(1), seq(2)],
        pl.BlockSpec((TA, GW), lambda b, i: (b * nq + i, 0)), _sds((bl * s, GW)),
        [pltpu.VMEM((TA, GW), F32)],
    )(pb, pb, pb)


def sb_attn_bwd(pb, do, bl, s, tag):
    nq = s // TA

    def body(q_ref, k_ref, v_ref, do_ref, dq_ref, dk_ref, dv_ref, dqa, ls_s, lom_s, w_s, g_s):
        i = pl.program_id(1)

        @pl.when(i == 0)
        def _():
            dk_ref[...] = jnp.zeros_like(dk_ref)
            dv_ref[...] = jnp.zeros_like(dv_ref)

        qs = _stack_heads(q_ref[...], SCALE)
        dos = _stack_heads(do_ref[...])
        row = _stack_rows(i)
        upper = _tri(TQ, "lt")
        before = _tri(TQ, "gt")
        dqa[...] = jnp.zeros_like(dqa)
        last = _n_key_tiles(i) - 1

        def weights(jj, r):
            j = last - jj
            ks = pl.ds(pl.multiple_of(j * TQ, TQ), TQ)
            col = j * TQ + _iota((1, TQ), 1)
            _, ls, lom, w, _ = _sb_block(qs, _cast(k_ref[ks, :]), row, col, upper, r)
            ls_s[:, ks] = ls
            lom_s[:, ks] = lom
            w_s[:, ks] = _cast(w)
            g_s[:, ks] = _dg(dos, _cast(v_ref[ks, :]), 1, 1) * w
            return r + jnp.sum(lom, axis=-1, keepdims=True)

        lax.fori_loop(0, last + 1, weights, jnp.zeros((SROWS, 1), F32))

        def step(j, cpre):
            ks = pl.ds(pl.multiple_of(j * TQ, TQ), TQ)
            col = j * TQ + _iota((1, TQ), 1)
            g = g_s[:, ks]
            pre = cpre + xr(g, before, before)
            dz = jnp.where(col < row, g * jnp.exp(lom_s[:, ks]) - jnp.exp(ls_s[:, ks]) * pre, 0.0)
            dzb = _cast(dz)
            dqa[...] += _dg(_lanes_cat(dzb), _mask_stack(k_ref[ks, :]), 1, 0) * SCALE
            dk_ref[ks, :] += _dg(dzb, qs, 0, 0)
            dv_ref[ks, :] += _dg(w_s[:, ks], dos, 0, 0)
            return cpre + jnp.sum(g, axis=-1, keepdims=True)

        lax.fori_loop(0, last + 1, step, jnp.zeros((SROWS, 1), F32))
        dq_ref[...] = dqa[...]

    tok = lambda j: pl.BlockSpec((TA, GW), lambda b, i: (b * nq + i, j))
    seq = lambda j: pl.BlockSpec((s, GW), lambda b, i: (b, j))
    return _call(
        body, f"sb_attn_bwd_{tag}", (bl, nq), [tok(0), seq(1), seq(2), tok(0)],
        [tok(0), seq(0), seq(0)], [_sds((bl * s, GW))] * 3,
        [pltpu.VMEM((TA, GW), F32), pltpu.VMEM((SROWS, s), F32), pltpu.VMEM((SROWS, s), F32),
         pltpu.VMEM((SROWS, s), _MMT), pltpu.VMEM((SROWS, s), F32)],
    )(pb, pb, pb, do)


def _hgrn_consts():
    r, c = _iota((CH, CH), 0), _iota((CH, CH), 1)
    rr = _iota((CH, 1), 0)
    tri = (c <= r).astype(F32)
    lv = []
    for m in (8, 4, 2, 1):
        up = ((rr & (2 * m - 1)) >= m).astype(F32)
        lo = 1.0 - up
        selq = (((r & (2 * m - 1)) >= m) & (c == (r & ~(m - 1)) - 1)).astype(F32)
        selk = (((r & (2 * m - 1)) < m) & (c == (r & ~(m - 1)) + m - 1)).astype(F32)
        pm = (((r & ~(2 * m - 1)) == (c & ~(2 * m - 1))) & ((r & (2 * m - 1)) >= m) & ((c & (2 * m - 1)) < m)).astype(F32)
        lv.append((up, lo, selq, selq.T, selk, selk.T, jnp.concatenate([pm] * NH, axis=0)))
    hm4 = lambda n: (((_iota((NH, 1, n), 2) & (GW - 1)) >> 6) == _iota((NH, 1, n), 0)).astype(F32)
    return dict(tri=tri, trit=tri.T, rr=rr, lv=lv, bd=_bdmask(), bd64=_bdmask() * (1.0 / HD),
                hm4={GW: hm4(GW), 3 * GW: hm4(3 * GW)})


def _hgrn_chunk_fn(hq, hf, hi, lb, wn, st, cs):
    q = _silu(hq)
    log_lb = jnp.log(jnp.maximum(lb, LB_FLOOR))
    a, bb = log_lb, jnp.log1p(-lb) + _logsig(hf)
    g = jnp.maximum(a, bb) + jnp.log1p(jnp.exp(-jnp.abs(a - bb)))
    k = (1.0 - lb) * _sigmoid(-hf)
    v = hi
    rr = cs["rr"]
    b = xl(cs["tri"], cs["trit"], g)
    row_of = lambda n: jnp.sum(b * (rr == n).astype(F32), axis=0, keepdims=True)
    o = mm_nt(q * jnp.exp(b), st)
    qs, ks = [], []
    for ib in (1, 2, 3):
        ref = row_of(16 * ib - 1)
        inq = ((rr >= 16 * ib) & (rr < 16 * ib + 16)).astype(F32)
        ink = (rr < 16 * ib).astype(F32)
        qs.append(q * jnp.exp((b - ref) * inq) * inq)
        ks.append(k * jnp.exp((ref - b) * ink) * ink)
    qcat, kcat = jnp.concatenate(qs, axis=1), jnp.concatenate(ks, axis=1)
    lvl = []
    for up, lo, selq, selqt, selk, selkt, pm in cs["lv"]:
        qe = q * jnp.exp((b - xl(selq, selqt, b)) * up) * up
        ke = k * jnp.exp((xl(selk, selkt, b) - b) * lo) * lo
        lvl.append((qe, ke, pm))
    stack = lambda x: (x[None] * cs["hm4"][x.shape[1]]).reshape(NH * CH, x.shape[1])
    a_all = mm_nt(stack(qcat), kcat)
    for qe, ke, pm4 in lvl:
        a_all = a_all + mm_nt(stack(qe), ke) * pm4
    o = o + jnp.sum(mm(a_all, v).reshape(NH, CH, GW) * cs["hm4"][GW], axis=0)
    o = o + xr(q * k, cs["bd"], cs["bd"]) * v
    b_last = row_of(CH - 1)
    st_new = st * jnp.exp(b_last) + mm_tn(v, k * jnp.exp(b_last - b)) * cs["bd"]
    return _headrms(o, wn, cs["bd64"]), st_new


def hgrn_fwd(pc, lb, wn, bl, s, layer, tag):
    nc = s // CH

    def body(q_ref, f_ref, i_ref, lb_ref, wn_ref, o_ref, st_ref, st):
        @pl.when(pl.program_id(0) == 0)
        def _():
            st[...] = jnp.zeros_like(st)

        cs = _hgrn_consts()
        for b in range(bl):
            st_ref[b] = st[b]
            o, st_new = _hgrn_chunk_fn(q_ref[b], f_ref[b], i_ref[b], lb_ref[...], wn_ref[...], st[b], cs)
            o_ref[b] = o
            st[b] = st_new

    tok = lambda j: pl.BlockSpec((bl, CH, GW), lambda c: (0, c, j))
    par = pl.BlockSpec((None, 1, GW), lambda c: (layer, 0, 0))
    pc3 = pc.reshape(bl, s, 3 * GW)
    o, states = _call(
        body, f"hgrn_fwd_{tag}", (nc,), [tok(0), tok(1), tok(2), par, par],
        [tok(0), pl.BlockSpec((bl, None, GW, GW), lambda c: (0, c, 0, 0))],
        [_sds((bl, s, GW)), _sds((bl, nc, GW, GW))],
        [pltpu.VMEM((bl, GW, GW), F32)],
    )(pc3, pc3, pc3, lb, wn)
    return o.reshape(bl * s, GW), states


def hgrn_bwd(pc, lb, wn, states, do, bl, s, layer, tag):
    nc = s // CH

    def body(q_ref, f_ref, i_ref, lb_ref, wn_ref, st_ref, do_ref, dc_ref, dlb_ref, dwn_ref, dst):
        c = pl.program_id(0)

        @pl.when(c == 0)
        def _():
            dst[...] = jnp.zeros_like(dst)

        cs = _hgrn_consts()
        dlb_sum = dwn_sum = None
        for b in range(bl):
            _, vjp = jax.vjp(lambda *a: _hgrn_chunk_fn(*a, cs), q_ref[b], f_ref[b], i_ref[b], lb_ref[...],
                             wn_ref[...], st_ref[b])
            dq, df, di, dlb, dwn, dst_in = vjp((do_ref[b], dst[b]))
            dst[b] = dst_in
            dc_ref[b, :, 0:GW] = dq
            dc_ref[b, :, GW:2 * GW] = df
            dc_ref[b, :, 2 * GW:3 * GW] = di
            dlb_sum = dlb if dlb_sum is None else dlb_sum + dlb
            dwn_sum = dwn if dwn_sum is None else dwn_sum + dwn
        _acc(dlb_ref, dlb_sum, c == 0)
        _acc(dwn_ref, dwn_sum, c == 0)

    tok = lambda j: pl.BlockSpec((bl, CH, GW), lambda c: (0, nc - 1 - c, j))
    par = pl.BlockSpec((None, 1, GW), lambda c: (layer, 0, 0))
    acc = pl.BlockSpec((1, GW), lambda c: (0, 0))
    pc3 = pc.reshape(bl, s, 3 * GW)
    dc, dlb, dwn = _call(
        body, f"hgrn_bwd_{tag}", (nc,),
        [tok(0), tok(1), tok(2), par, par, pl.BlockSpec((bl, None, GW, GW), lambda c: (0, nc - 1 - c, 0, 0)), tok(0)],
        [pl.BlockSpec((bl, CH, 3 * GW), lambda c: (0, nc - 1 - c, 0)), acc, acc],
        [_sds((bl, s, 3 * GW)), _sds((1, GW)), _sds((1, GW))],
        [pltpu.VMEM((bl, GW, GW), F32)],
    )(pc3, pc3, pc3, lb, wn, states, do.reshape(bl, s, GW))
    return dc.reshape(bl * s, 3 * GW), dlb, dwn


def _shift_rows(x, k, up):
    n = x.shape[0]
    rr = _iota((n, 1), 0)
    if up:
        return jnp.where(rr < n - k, pltpu.roll(x, n - k, 0), 0.0)
    return jnp.where(rr >= k, pltpu.roll(x, k, 0), 0.0)


def _window_sums(x, up):
    s2 = x + _shift_rows(x, 1, up)
    s4 = s2 + _shift_rows(s2, 2, up)
    s8 = s4 + _shift_rows(s4, 4, up)
    s16 = s8 + _shift_rows(s8, 8, up)
    return s2, s4, s8, s16


def _pool_div(n):
    pos = (_iota((n, 1), 0) + 1).astype(F32)
    return [jnp.minimum(pos, float(w)) for w in (2, 4, 8, 16)]


def _pool_mix(sums, scaled):
    out = None
    for gi, sw in enumerate(sums):
        part = (sw if scaled is None else sw / scaled[gi]) * _hmask(gi)
        out = part if out is None else out + part
    return out


def pool_fwd(pd, wbd, scale, bl, s, layer, tag):
    def body(u_ref, w_ref, sc_ref, o_ref):
        u = u_ref[...]
        pm = _pool_mix(_window_sums(u, False), _pool_div(s)) - u
        o_ref[...] = _dg(_cast(pm), _cast(w_ref[...]), 1, 0) * sc_ref[...]

    seq = pl.BlockSpec((s, GW), lambda b: (b, 0))
    return _call(
        body, f"pool_fwd_{tag}", (bl,),
        [seq, pl.BlockSpec((None, GW, GW), lambda b: (layer, 0, 0)), pl.BlockSpec((None, 1, GW), lambda b: (layer, 0, 0))],
        seq, _sds((bl * s, GW)),
    )(pd, wbd, scale)


def pool_bwd(pd, wbd, scale, do, bl, s, layer, tag):
    def body(u_ref, w_ref, sc_ref, do_ref, du_ref, dw_ref, dsc_ref):
        first = pl.program_id(0) == 0
        u, do = u_ref[...], do_ref[...]
        div = _pool_div(s)
        pm = _pool_mix(_window_sums(u, False), div) - u
        ypre = _dg(_cast(pm), _cast(w_ref[...]), 1, 0)
        dys = do * sc_ref[...]
        _acc(dsc_ref, jnp.sum(do * ypre, axis=0, keepdims=True), first)
        _acc(dw_ref, _dg(_cast(pm), _cast(dys), 0, 0), first)
        dpm = _dg(_cast(dys), _cast(w_ref[...]), 1, 1)
        dsc = [dpm / d for d in div]
        adj = None
        for gi in range(4):
            part = _window_sums(dsc[gi] * _hmask(gi), True)[gi]
            adj = part if adj is None else adj + part
        du_ref[...] = adj - dpm

    seq = pl.BlockSpec((s, GW), lambda b: (b, 0))
    return _call(
        body, f"pool_bwd_{tag}", (bl,),
        [seq, pl.BlockSpec((None, GW, GW), lambda b: (layer, 0, 0)), pl.BlockSpec((None, 1, GW), lambda b: (layer, 0, 0)), seq],
        [seq, pl.BlockSpec((GW, GW), lambda b: (0, 0)), pl.BlockSpec((1, GW), lambda b: (0, 0))],
        [_sds((bl * s, GW)), _sds((GW, GW)), _sds((1, GW))],
    )(pd, wbd, scale, do)


def _mem_prep_fn(mem, g, wk, wv, kw, bd64):
    mn = _rms(mem, g)
    return _headrms(mm(mn, wk), kw, bd64), mm(mn, wv)


def mem_prep_fwd(mem2, g, wkv, kw, bl, layer, tag):
    def body(m_ref, g_ref, wk_ref, wv_ref, kw_ref, k_ref, v_ref):
        k, v = _mem_prep_fn(m_ref[...], g_ref[...], wk_ref[...], wv_ref[...], kw_ref[...], _bdmask() * (1.0 / HD))
        k_ref[...] = k
        v_ref[...] = v

    blk = pl.BlockSpec((N_MEM, GW), lambda b: (b, 0))
    return _call(
        body, f"mem_prep_fwd_{tag}", (bl,),
        [pl.BlockSpec((N_MEM, D_MODEL), lambda b: (b, 0)), pl.BlockSpec((None, 1, D_MODEL), lambda b: (layer, 0, 0)),
         pl.BlockSpec((None, D_MODEL, GW), lambda b: (layer, 0, 0)), pl.BlockSpec((None, D_MODEL, GW), lambda b: (layer, 0, 1)),
         pl.BlockSpec((None, 1, GW), lambda b: (layer, 0, 0))],
        [blk, blk], [_sds((bl * N_MEM, GW))] * 2,
    )(mem2, g, wkv, wkv, kw)


def mem_prep_bwd(mem2, g, wkv, kw, dk, dv, bl, layer, tag):
    def body(m_ref, g_ref, wk_ref, wv_ref, kw_ref, dk_ref, dv_ref, dwk_ref, dwv_ref, dg_ref, dkw_ref):
        first = pl.program_id(0) == 0
        bd64 = _bdmask() * (1.0 / HD)
        _, vjp = jax.vjp(lambda g_, wk, wv, kw_: _mem_prep_fn(m_ref[...], g_, wk, wv, kw_, bd64),
                         g_ref[...], wk_ref[...].astype(F32), wv_ref[...].astype(F32), kw_ref[...])
        dg, dwk, dwv, dkw = vjp((dk_ref[...], dv_ref[...]))
        _acc(dwk_ref, dwk, first)
        _acc(dwv_ref, dwv, first)
        _acc(dg_ref, dg, first)
        _acc(dkw_ref, dkw, first)

    blk = pl.BlockSpec((N_MEM, GW), lambda b: (b, 0))
    return _call(
        body, f"mem_prep_bwd_{tag}", (bl,),
        [pl.BlockSpec((N_MEM, D_MODEL), lambda b: (b, 0)), pl.BlockSpec((None, 1, D_MODEL), lambda b: (layer, 0, 0)),
         pl.BlockSpec((None, D_MODEL, GW), lambda b: (layer, 0, 0)), pl.BlockSpec((None, D_MODEL, GW), lambda b: (layer, 0, 1)),
         pl.BlockSpec((None, 1, GW), lambda b: (layer, 0, 0)), blk, blk],
        [pl.BlockSpec((D_MODEL, GW), lambda b: (0, 0)), pl.BlockSpec((D_MODEL, GW), lambda b: (0, 0)),
         pl.BlockSpec((1, D_MODEL), lambda b: (0, 0)), pl.BlockSpec((1, GW), lambda b: (0, 0))],
        [_sds((D_MODEL, GW)), _sds((D_MODEL, GW)), _sds((1, D_MODEL)), _sds((1, GW))],
    )(mem2, g, wkv, wkv, kw, dk, dv)


def _mem_attn_fn(mq, qw, k, v, bd64):
    qn = _headrms(mq, qw, bd64)
    out = None
    for h in range(NH):
        hm = _hmask(h)
        lg = mm_nt(qn * hm, k) * SCALE
        e = jnp.exp(lg - lax.stop_gradient(jnp.max(lg, axis=-1, keepdims=True)))
        p = e / jnp.sum(e, axis=-1, keepdims=True)
        part = mm(p, v) * hm
        out = part if out is None else out + part
    return out


def mem_attn_fwd(pe, qw, k, v, bl, s, layer, tag):
    nq = s // TQ

    def body(q_ref, qw_ref, k_ref, v_ref, o_ref):
        o_ref[...] = _mem_attn_fn(q_ref[...], qw_ref[...], k_ref[...], v_ref[...], _bdmask() * (1.0 / HD))

    tok = pl.BlockSpec((TQ, GW), lambda b, i: (b * nq + i, 0))
    kv = pl.BlockSpec((N_MEM, GW), lambda b, i: (b, 0))
    return _call(
        body, f"mem_attn_fwd_{tag}", (bl, nq), [tok, pl.BlockSpec((None, 1, GW), lambda b, i: (layer, 0, 0)), kv, kv],
        tok, _sds((bl * s, GW)),
    )(pe, qw, k, v)


def mem_attn_bwd(pe, qw, k, v, do, bl, s, layer, tag):
    nq = s // TQ

    def body(q_ref, qw_ref, k_ref, v_ref, do_ref, dq_ref, dk_ref, dv_ref, dqw_ref):
        i = pl.program_id(1)
        bd64 = _bdmask() * (1.0 / HD)
        _, vjp = jax.vjp(lambda *a: _mem_attn_fn(*a, bd64), q_ref[...], qw_ref[...], k_ref[...], v_ref[...])
        dq, dqw, dk, dv = vjp(do_ref[...])
        dq_ref[...] = dq
        _acc(dk_ref, dk, i == 0)
        _acc(dv_ref, dv, i == 0)
        _acc(dqw_ref, dqw, jnp.logical_and(pl.program_id(0) == 0, i == 0))

    tok = pl.BlockSpec((TQ, GW), lambda b, i: (b * nq + i, 0))
    kv = pl.BlockSpec((N_MEM, GW), lambda b, i: (b, 0))
    return _call(
        body, f"mem_attn_bwd_{tag}", (bl, nq),
        [tok, pl.BlockSpec((None, 1, GW), lambda b, i: (layer, 0, 0)), kv, kv, tok],
        [tok, kv, kv, pl.BlockSpec((1, GW), lambda b, i: (0, 0))],
        [_sds((bl * s, GW)), _sds((bl * N_MEM, GW)), _sds((bl * N_MEM, GW)), _sds((1, GW))],
    )(pe, qw, k, v, do)


def _gate_out_fn(outs, gates, wparts):
    y = None
    for o, g, w in zip(outs, gates, wparts):
        part = mm(o * _silu(g), w)
        y = part if y is None else y + part
    return y


def outproj_fwd(x2, outs, pg, wout, layer, tag):
    t = x2.shape[0]

    def body(x_ref, oa, ob, oc, od, oe, g_ref, w_ref, y_ref):
        outs_ = [r[...] for r in (oa, ob, oc, od, oe)]
        gates = [g_ref[:, j * GW:(j + 1) * GW] for j in range(5)]
        wparts = [w_ref[j * GW:(j + 1) * GW, :] for j in range(5)]
        y_ref[...] = x_ref[...] + _gate_out_fn(outs_, gates, wparts)

    tok = pl.BlockSpec((TQ, GW), lambda i: (i, 0))
    big = pl.BlockSpec((TQ, D_MODEL), lambda i: (i, 0))
    return _call(
        body, f"outproj_fwd_{tag}", (t // TQ,),
        [big] + [tok] * 5 + [pl.BlockSpec((TQ, D_MIX), lambda i: (i, 0)),
                            pl.BlockSpec((None, D_MIX, D_MODEL), lambda i: (layer, 0, 0))],
        big, _sds((t, D_MODEL)),
    )(x2, *outs, pg, wout)


def outproj_bwd(outs, pg, wout, dy, layer, tag):
    t = dy.shape[0]

    def body(oa, ob, oc, od, oe, g_ref, w_ref, dy_ref, da, db, dc, dd, de, dg_ref, dw_ref):
        outs_ = [r[...] for r in (oa, ob, oc, od, oe)]
        gates = [g_ref[:, j * GW:(j + 1) * GW] for j in range(5)]
        wparts = [w_ref[j * GW:(j + 1) * GW, :].astype(F32) for j in range(5)]
        _, vjp = jax.vjp(_gate_out_fn, outs_, gates, wparts)
        douts, dgates, dws = vjp(dy_ref[...])
        for r, val in zip((da, db, dc, dd, de), douts):
            r[...] = val
        first = pl.program_id(0) == 0
        for j in range(5):
            dg_ref[:, j * GW:(j + 1) * GW] = dgates[j]

        @pl.when(first)
        def _():
            for j in range(5):
                dw_ref[j * GW:(j + 1) * GW, :] = dws[j]

        @pl.when(jnp.logical_not(first))
        def _():
            for j in range(5):
                dw_ref[j * GW:(j + 1) * GW, :] += dws[j]

    tok = pl.BlockSpec((TQ, GW), lambda i: (i, 0))
    return _call(
        body, f"outproj_bwd_{tag}", (t // TQ,),
        [tok] * 5 + [pl.BlockSpec((TQ, D_MIX), lambda i: (i, 0)), pl.BlockSpec((None, D_MIX, D_MODEL), lambda i: (layer, 0, 0)),
                     pl.BlockSpec((TQ, D_MODEL), lambda i: (i, 0))],
        [tok] * 5 + [pl.BlockSpec((TQ, D_MIX), lambda i: (i, 0)), pl.BlockSpec((D_MIX, D_MODEL), lambda i: (0, 0))],
        [_sds((t, GW))] * 5 + [_sds((t, D_MIX)), _sds((D_MIX, D_MODEL))],
    )(*outs, pg, wout, dy)


def loss_head(y, tgt):
    t = y.shape[0]

    def body(y_ref, t_ref, l_ref, dy_ref):
        diff = y_ref[...] - t_ref[...]
        dy_ref[...] = diff * (1.0 / D_MODEL)
        part = 0.5 * jnp.sum(jnp.sum(diff * diff, axis=-1, keepdims=True) * (1.0 / D_MODEL), axis=0, keepdims=True)
        _acc(l_ref, jnp.broadcast_to(part, (8, 128)), pl.program_id(0) == 0)

    big = pl.BlockSpec((TQ, D_MODEL), lambda i: (i, 0))
    return _call(body, "loss_head", (t // TQ,), [big, big], [pl.BlockSpec((8, 128), lambda i: (0, 0)), big],
                 [_sds((8, 128)), _sds((t, D_MODEL))])(y, tgt)


def layer_fwd(x2, mem2, p, layer, bl, s):
    tag = f"l{layer}"
    ht, pa, pb, pc, pd, pe, pg, pf = inproj_fwd(x2, p["norm_g"], p["w_in"], layer, tag)
    qn, kn, vb, cq, ck = fox_prep_fwd(pa, pf, p["fox_q_norm"], p["fox_k_norm"], p["fox_f_bias"], bl, s, layer, tag)
    oa, lse = fox_attn_fwd(qn, kn, vb, cq, ck, bl, s, tag)
    ob = sb_attn_fwd(pb, bl, s, tag)
    oc, states = hgrn_fwd(pc, p["lb"], p["hgrn_out_norm"], bl, s, layer, tag)
    od = pool_fwd(pd, p["pool_wbd"], p["pool_scale"], bl, s, layer, tag)
    mk, mv = mem_prep_fwd(mem2, p["mem_norm_g"], p["mem_w_kv"], p["mem_k_norm"], bl, layer, tag)
    oe = mem_attn_fwd(pe, p["mem_q_norm"], mk, mv, bl, s, layer, tag)
    y = outproj_fwd(x2, (oa, ob, oc, od, oe), pg, p["w_out"], layer, tag)
    saved = dict(x2=x2, ht=ht, pa=pa, pb=pb, pc=pc, pd=pd, pe=pe, pg=pg, pf=pf, qn=qn, kn=kn, vb=vb, cq=cq, ck=ck,
                 oa=oa, lse=lse, ob=ob, oc=oc, states=states, od=od, mk=mk, mv=mv, oe=oe)
    return y, saved


def layer_bwd(dy, mem2, p, sv, layer, bl, s):
    tag = f"l{layer}"
    (doa, dob, doc, dod, doe, dg_gates, dwout) = outproj_bwd((sv["oa"], sv["ob"], sv["oc"], sv["od"], sv["oe"]), sv["pg"],
                                                              p["w_out"], dy, layer, tag)
    dqn, dkn, dv, dck = fox_attn_bwd(sv["qn"], sv["kn"], sv["vb"], sv["cq"], sv["ck"], sv["lse"], doa, bl, s, tag)
    d_a, d_f, dqw, dkw, dbias = fox_prep_bwd(sv["pa"], sv["pf"], p["fox_q_norm"], p["fox_k_norm"], p["fox_f_bias"], sv["cq"],
                                             dqn, dkn, dv, dck, bl, s, layer, tag)
    dsq, dsk, dsv = sb_attn_bwd(sv["pb"], dob, bl, s, tag)
    d_c, dlb, dwn = hgrn_bwd(sv["pc"], p["lb"], p["hgrn_out_norm"], sv["states"], doc, bl, s, layer, tag)
    d_d, dwbd, dpscale = pool_bwd(sv["pd"], p["pool_wbd"], p["pool_scale"], dod, bl, s, layer, tag)
    d_e, dmk, dmv, dmqw = mem_attn_bwd(sv["pe"], p["mem_q_norm"], sv["mk"], sv["mv"], doe, bl, s, layer, tag)
    dwk, dwv, dmg, dmkw = mem_prep_bwd(mem2, p["mem_norm_g"], p["mem_w_kv"], p["mem_k_norm"], dmk, dmv, bl, layer, tag)
    dpieces = (d_a, dsq, dsk, dsv, d_c, d_d, d_e, dg_gates, d_f)
    dx, dng = inproj_bwd_dx(sv["x2"], p["norm_g"], p["w_in"], dy, dpieces, layer, tag)
    dwin = jnp.concatenate([matmul_acc(sv["ht"], dp, f"{tag}_{nm}") for (nm, _, _), dp in zip(BWD_PIECES, dpieces)], axis=1)
    grads = dict(norm_g=dng, w_in=dwin, fox_f_bias=dbias, fox_q_norm=dqw, fox_k_norm=dkw, lb=dlb, hgrn_out_norm=dwn,
                 pool_wbd=dwbd, pool_scale=dpscale, mem_norm_g=dmg, mem_w_kv=jnp.concatenate([dwk, dwv], axis=1),
                 mem_q_norm=dmqw, mem_k_norm=dmkw, w_out=dwout)
    return dx, grads


def _tile4(w):
    return jnp.tile(w, (1, NH))[:, None, :]


def prepare_params(norm_g, w_in_p, fox_f_bias, fox_q_norm, fox_k_norm, hgrn_lb_logits, hgrn_out_norm, pool_w, pool_scale,
                   mem_norm_g, mem_w_kv, mem_q_norm, mem_k_norm, w_out):
    p1 = jax.nn.sigmoid(hgrn_lb_logits[1] - hgrn_lb_logits[0])
    lb = jnp.stack([jnp.zeros_like(p1), jnp.clip(p1, 0.0, 1.0 - 1e-6)])
    eye = jnp.eye(4, dtype=F32)
    wbd = jnp.einsum("lgcd,gh->lgchd", pool_w, eye).reshape(2, GW, GW)
    return dict(norm_g=norm_g[:, None, :], w_in=w_in_p, fox_f_bias=jnp.pad(fox_f_bias, ((0, 0), (0, 124)))[:, None, :],
                fox_q_norm=_tile4(fox_q_norm), fox_k_norm=_tile4(fox_k_norm), lb=lb[:, None, :],
                hgrn_out_norm=hgrn_out_norm[:, None, :], pool_wbd=wbd, pool_scale=pool_scale[:, None, :],
                mem_norm_g=mem_norm_g[:, None, :], mem_w_kv=mem_w_kv, mem_q_norm=_tile4(mem_q_norm),
                mem_k_norm=_tile4(mem_k_norm), w_out=w_out)


def local_step(x, mem, tgt, p):
    bl, s, _ = x.shape
    x2, mem2, tgt2 = x.reshape(bl * s, D_MODEL), mem.reshape(bl * N_MEM, D_MODEL), tgt.reshape(bl * s, D_MODEL)
    y0, sv0 = layer_fwd(x2, mem2, p, 0, bl, s)
    y1, sv1 = layer_fwd(y0, mem2, p, 1, bl, s)
    lpart, dy = loss_head(y1, tgt2)
    dx1, g1 = layer_bwd(dy, mem2, p, sv1, 1, bl, s)
    dx0, g0 = layer_bwd(dx1, mem2, p, sv0, 0, bl, s)
    return lpart[0, 0], dx0.reshape(bl, s, D_MODEL), g0, g1


_ANY = pl.BlockSpec(memory_space=pl.ANY)


def _me_and_peers():
    x, y, c = lax.axis_index("x"), lax.axis_index("y"), lax.axis_index("c")
    peers = []
    for k in range(1, N_DEV):
        px = 1 - x if (k >> 2) & 1 else x
        py = 1 - y if (k >> 1) & 1 else y
        pc = 1 - c if k & 1 else c
        peers.append(((px, py, pc), 4 * px + 2 * py + pc))
    return 4 * x + 2 * y + c, peers


def all_gather_rows(xs, tag):
    nl, r, c = xs.shape

    def body(x_ref, o_ref, send_sems, recv_sems, local_sem):
        x, y, cc = lax.axis_index("x"), lax.axis_index("y"), lax.axis_index("c")
        me, sibling = (x, y, cc), (x, y, 1 - cc)
        chips = [(1 - x, y), (x, 1 - y), (1 - x, 1 - y)]

        def rows(px, py, pc):
            return o_ref.at[:, pl.ds((4 * px + 2 * py + pc) * r, r), :]

        def copy(k, block, to, src=None):
            return pltpu.make_async_remote_copy(src_ref=rows(*block) if src is None else src, dst_ref=rows(*block),
                                                send_sem=send_sems.at[k], recv_sem=recv_sems.at[k], device_id=to,
                                                device_id_type=pl.DeviceIdType.MESH)

        mine = pltpu.make_async_copy(x_ref, rows(*me), local_sem)
        mine.start()
        first = [copy(0, me, sibling, src=x_ref)] + [copy(1 + j, me, (*chip, cc), src=x_ref) for j, chip in enumerate(chips)]
        for cp in first:
            cp.start()
        passed = [copy(4 + j, (*chip, cc), sibling) for j, chip in enumerate(chips)]
        for j, chip in enumerate(chips):
            copy(1 + j, (*chip, cc), me).wait_recv()
            passed[j].start()
        copy(0, sibling, me).wait_recv()
        for j, chip in enumerate(chips):
            copy(4 + j, (*chip, 1 - cc), me).wait_recv()
        for cp in first + passed:
            cp.wait_send()
        mine.wait()

    return pl.pallas_call(
        body, name=f"all_gather_{tag}", in_specs=[_ANY], out_specs=_ANY, out_shape=_sds((nl, N_DEV * r, c), xs.dtype),
        scratch_shapes=[pltpu.SemaphoreType.DMA((N_DEV - 1,)), pltpu.SemaphoreType.DMA((N_DEV - 1,)), pltpu.SemaphoreType.DMA],
    )(xs)


def exchange_cores(part, tag):
    nl, r8, c = part.shape
    r = r8 // N_DEV

    def body(p_ref, mine_ref, theirs_ref, send_sem, recv_sem, local_sem):
        x, y, cc = lax.axis_index("x"), lax.axis_index("y"), lax.axis_index("c")
        local = pltpu.make_async_copy(p_ref.at[:, :, pl.ds(cc, 1)], mine_ref, local_sem)
        swap = pltpu.make_async_remote_copy(src_ref=p_ref.at[:, :, pl.ds(1 - cc, 1)], dst_ref=theirs_ref, send_sem=send_sem,
                                            recv_sem=recv_sem, device_id=(x, y, 1 - cc), device_id_type=pl.DeviceIdType.MESH)
        local.start()
        swap.start()
        swap.wait()
        local.wait()

    shp = _sds((nl, 4, 1, r, c), part.dtype)
    return pl.pallas_call(
        body, name=f"exchange_cores_{tag}", in_specs=[_ANY], out_specs=[_ANY, _ANY], out_shape=[shp, shp],
        scratch_shapes=[pltpu.SemaphoreType.DMA, pltpu.SemaphoreType.DMA, pltpu.SemaphoreType.DMA],
    )(part.reshape(nl, 4, 2, r, c))


def exchange_chips(s4, tag):
    nl, _, _, r, c = s4.shape

    def body(s_ref, o_ref, send_sems, recv_sems, local_sem):
        x, y, cc = lax.axis_index("x"), lax.axis_index("y"), lax.axis_index("c")
        local = pltpu.make_async_copy(s_ref.at[:, pl.ds(2 * x + y, 1)], o_ref.at[0], local_sem)
        local.start()
        copies = []
        for k in range(1, 4):
            px = 1 - x if (k >> 1) & 1 else x
            py = 1 - y if k & 1 else y
            copies.append(pltpu.make_async_remote_copy(
                src_ref=s_ref.at[:, pl.ds(2 * px + py, 1)], dst_ref=o_ref.at[k], send_sem=send_sems.at[k - 1],
                recv_sem=recv_sems.at[k - 1], device_id=(px, py, cc), device_id_type=pl.DeviceIdType.MESH))
        for cp in copies:
            cp.start()
        for cp in copies:
            cp.wait()
        local.wait()

    return pl.pallas_call(
        body, name=f"exchange_chips_{tag}", in_specs=[_ANY], out_specs=_ANY, out_shape=_sds((4, nl, 1, 1, r, c), s4.dtype),
        scratch_shapes=[pltpu.SemaphoreType.DMA((3,)), pltpu.SemaphoreType.DMA((3,)), pltpu.SemaphoreType.DMA],
    )(s4)


def add_pair(a, b, tag):
    rows, c = a.shape
    tr = _row_tile(rows)

    def body(a_ref, b_ref, o_ref):
        o_ref[...] = a_ref[...] + b_ref[...]

    blk = pl.BlockSpec((tr, c), lambda i: (i, 0))
    return _call(body, f"add_pair_{tag}", (rows // tr,), [blk, blk], blk, _sds((rows, c)))(a, b)


def exchange_row_blocks(part, tag):
    nl, r8, c = part.shape
    r = r8 // N_DEV

    def body(p_ref, o_ref, send_sems, recv_sems, local_sem):
        me, peers = _me_and_peers()
        rows = lambda idx: p_ref.at[:, pl.ds(idx * r, r), :]
        mine = pltpu.make_async_copy(rows(me), o_ref.at[0], local_sem)
        mine.start()
        copies = [pltpu.make_async_remote_copy(src_ref=rows(idx), dst_ref=o_ref.at[k + 1], send_sem=send_sems.at[k],
                                               recv_sem=recv_sems.at[k], device_id=dev, device_id_type=pl.DeviceIdType.MESH)
                  for k, (dev, idx) in enumerate(peers)]
        for cp in copies:
            cp.start()
        for cp in copies:
            cp.wait()
        mine.wait()

    return pl.pallas_call(
        body, name=f"exchange_{tag}", in_specs=[_ANY], out_specs=_ANY, out_shape=_sds((N_DEV, nl, r, c), part.dtype),
        scratch_shapes=[pltpu.SemaphoreType.DMA((N_DEV - 1,)), pltpu.SemaphoreType.DMA((N_DEV - 1,)), pltpu.SemaphoreType.DMA],
    )(part)


def _row_tile(rows):
    if rows <= 512 and rows % 64:
        return rows
    for t in (64, 40, 32, 16, 8):
        if rows % t == 0:
            return t
    return rows


def sum_slots(slots, tag):
    ns, rows, c = slots.shape
    tr = _row_tile(rows)

    def body(s_ref, o_ref):
        acc = s_ref[0]
        for k in range(1, ns):
            acc = acc + s_ref[k]
        o_ref[...] = acc

    return _call(body, f"sum_slots_{tag}", (rows // tr,), [pl.BlockSpec((ns, tr, c), lambda i: (0, i, 0))],
                 pl.BlockSpec((tr, c), lambda i: (i, 0)), _sds((rows, c)))(slots)


def _adamw(w, g, m, v):
    m = ADAM_B1 * m + (1.0 - ADAM_B1) * g
    v = ADAM_B2 * v + (1.0 - ADAM_B2) * (g * g)
    m_hat = m / (1.0 - ADAM_B1 ** ADAM_STEP)
    v_hat = v / (1.0 - ADAM_B2 ** ADAM_STEP)
    delta = -ADAM_LR * (m_hat / (jnp.sqrt(v_hat) + ADAM_EPS) + ADAM_WD * w)
    return delta, m, v


def adam_update(w, m, v, g, tag, slots=False):
    rows, c = w.shape
    tr = _row_tile(rows)
    ns = g.shape[0] if slots else 0

    def body(w_ref, m_ref, v_ref, g_ref, go_ref, d_ref, mo_ref, vo_ref):
        if slots:
            g = g_ref[0]
            for k in range(1, ns):
                g = g + g_ref[k]
        else:
            g = g_ref[...]
        d, mn, vn = _adamw(w_ref[...], g, m_ref[...], v_ref[...])
        go_ref[...] = g
        d_ref[...] = d
        mo_ref[...] = mn
        vo_ref[...] = vn

    blk = pl.BlockSpec((tr, c), lambda i: (i, 0))
    gspec = pl.BlockSpec((ns, tr, c), lambda i: (0, i, 0)) if slots else blk
    return _call(body, f"adam_{tag}", (rows // tr,), [blk, blk, blk, gspec], [blk] * 4, [_sds((rows, c))] * 4)(w, m, v, g)


_SMALL = (("norm_g", (2, 1024)), ("fox_f_bias", (2, 4)), ("fox_q_norm", (2, 64)), ("fox_k_norm", (2, 64)),
          ("hgrn_lb_logits", (2, 256)), ("hgrn_out_norm", (2, 256)), ("pool_w", (2, 4, 64, 64)), ("pool_scale", (2, 256)),
          ("mem_norm_g", (2, 1024)), ("mem_q_norm", (2, 64)), ("mem_k_norm", (2, 64)))
_SLAB_ROWS = 312


def pack_small(d):
    flat = jnp.concatenate([d[n].reshape(-1) for n, _ in _SMALL])
    return jnp.pad(flat, (0, _SLAB_ROWS * 128 - flat.shape[0])).reshape(_SLAB_ROWS, 128)


def unpack_small(slab):
    flat, out, off = slab.reshape(-1), {}, 0
    for n, shp in _SMALL:
        size = 1
        for e in shp:
            size *= e
        out[n] = flat[off:off + size].reshape(shp)
        off += size
    return out


def small_grads(g0, g1, lb_logits):
    st = lambda f: jnp.stack([f(g0), f(g1)])
    heads = lambda a: a.reshape(NH, HD).sum(0)
    p1 = jax.nn.sigmoid(lb_logits[1] - lb_logits[0])
    inside = (p1 > 0.0) & (p1 < 1.0 - 1e-6)
    dl1 = jnp.where(inside, g1["lb"][0] * p1 * (1.0 - p1), 0.0)
    diag = lambda a: jnp.stack([a.reshape(4, HD, 4, HD)[i, :, i, :] for i in range(4)])
    return dict(norm_g=st(lambda g: g["norm_g"][0]), fox_f_bias=st(lambda g: g["fox_f_bias"][0, :NH]),
                fox_q_norm=st(lambda g: heads(g["fox_q_norm"])), fox_k_norm=st(lambda g: heads(g["fox_k_norm"])),
                hgrn_lb_logits=jnp.stack([-dl1, dl1]), hgrn_out_norm=st(lambda g: g["hgrn_out_norm"][0]),
                pool_w=st(lambda g: diag(g["pool_wbd"])), pool_scale=st(lambda g: g["pool_scale"][0]),
                mem_norm_g=st(lambda g: g["mem_norm_g"][0]), mem_q_norm=st(lambda g: heads(g["mem_q_norm"])),
                mem_k_norm=st(lambda g: heads(g["mem_k_norm"])))


def kernel(x, mem, norm_g, w_in, fox_f_bias, fox_q_norm, fox_k_norm, hgrn_lb_logits, hgrn_out_norm, pool_w, pool_scale, mem_norm_g, mem_w_kv, mem_q_norm, mem_k_norm, w_out, loss_target, m_norm_g, m_w_in, m_fox_f_bias, m_fox_q_norm, m_fox_k_norm, m_hgrn_lb_logits, m_hgrn_out_norm, m_pool_w, m_pool_scale, m_mem_norm_g, m_mem_w_kv, m_mem_q_norm, m_mem_k_norm, m_w_out, v_norm_g, v_w_in, v_fox_f_bias, v_fox_q_norm, v_fox_k_norm, v_hgrn_lb_logits, v_hgrn_out_norm, v_pool_w, v_pool_scale, v_mem_norm_g, v_mem_w_kv, v_mem_q_norm, v_mem_k_norm, v_w_out):
    given = dict(norm_g=(norm_g, m_norm_g, v_norm_g), w_in=(w_in, m_w_in, v_w_in), fox_f_bias=(fox_f_bias, m_fox_f_bias, v_fox_f_bias),
                 fox_q_norm=(fox_q_norm, m_fox_q_norm, v_fox_q_norm), fox_k_norm=(fox_k_norm, m_fox_k_norm, v_fox_k_norm),
                 hgrn_lb_logits=(hgrn_lb_logits, m_hgrn_lb_logits, v_hgrn_lb_logits),
                 hgrn_out_norm=(hgrn_out_norm, m_hgrn_out_norm, v_hgrn_out_norm), pool_w=(pool_w, m_pool_w, v_pool_w),
                 pool_scale=(pool_scale, m_pool_scale, v_pool_scale), mem_norm_g=(mem_norm_g, m_mem_norm_g, v_mem_norm_g),
                 mem_w_kv=(mem_w_kv, m_mem_w_kv, v_mem_w_kv), mem_q_norm=(mem_q_norm, m_mem_q_norm, v_mem_q_norm),
                 mem_k_norm=(mem_k_norm, m_mem_k_norm, v_mem_k_norm), w_out=(w_out, m_w_out, v_w_out))
    order = ("norm_g", "w_in", "fox_f_bias", "fox_q_norm", "fox_k_norm", "hgrn_lb_logits", "hgrn_out_norm", "pool_w",
             "pool_scale", "mem_norm_g", "mem_w_kv", "mem_q_norm", "mem_k_norm", "w_out")

    w_in_full = all_gather_rows(_cast(permute_cols(w_in)), "w_in")
    w_out_full = all_gather_rows(_cast(w_out), "w_out")
    w_kv_full = all_gather_rows(_cast(mem_w_kv), "w_kv")
    p = prepare_params(norm_g, w_in_full, fox_f_bias, fox_q_norm, fox_k_norm, hgrn_lb_logits, hgrn_out_norm, pool_w,
                       pool_scale, mem_norm_g, w_kv_full, mem_q_norm, mem_k_norm, w_out_full)

    loss_part, grad_x, g0, g1 = local_step(x, mem, loss_target, p)
    loss = lax.psum(loss_part, ("x", "y", "c"))

    res = {}

    def sharded(name, g2, unperm=False, two_stage=True):
        w, m, v = given[name]
        nl, r, c = w.shape
        cp = g2.shape[-1]
        if two_stage:
            mine, theirs = exchange_cores(g2, name)
            s4 = add_pair(mine.reshape(nl * 4 * r, cp), theirs.reshape(nl * 4 * r, cp), name)
            slots = exchange_chips(s4.reshape(nl, 4, 1, r, cp), name).reshape(4, nl * r, cp)
        else:
            slots = exchange_row_blocks(g2, name).reshape(N_DEV, nl * r, cp)
        if unperm:
            g = sum_slots(slots, name).reshape(nl, r, cp)
            out = adam_update(w.reshape(nl * r, c), m.reshape(nl * r, c), v.reshape(nl * r, c),
                              unpermute_cols(g).reshape(nl * r, c), name)
        else:
            out = adam_update(w.reshape(nl * r, c), m.reshape(nl * r, c), v.reshape(nl * r, c), slots, name, slots=True)
        res[name] = tuple(o.reshape(nl, r, c) for o in out)

    sharded("w_in", jnp.stack([g0["w_in"], g1["w_in"]]), unperm=True)
    sharded("w_out", jnp.stack([g0["w_out"], g1["w_out"]]))
    sharded("mem_w_kv", jnp.stack([g0["mem_w_kv"], g1["mem_w_kv"]]), two_stage=False)

    gsmall = pack_small(small_grads(g0, g1, hgrn_lb_logits))
    gathered = all_gather_rows(gsmall[None], "small").reshape(N_DEV, _SLAB_ROWS, 128)
    slabs = adam_update(*[pack_small({n: given[n][j] for n, _ in _SMALL}) for j in range(3)], gathered, "small", slots=True)
    small = [unpack_small(sl) for sl in slabs]
    for n, _ in _SMALL:
        res[n] = tuple(small[j][n] for j in range(4))

    return (loss, grad_x, *[res[n][0] for n in order], *[res[n][1] for n in order], *[res[n][2] for n in order],
            *[res[n][3] for n in order])
```

```python
import functools

import jax
import jax.numpy as jnp
from jax import lax
from jax.experimental import pallas as pl
from jax.experimental.pallas import tpu as pltpu

F32 = jnp.float32
BF = jnp.bfloat16
_MMT = BF

D_MODEL = 1024
GW = 256
HD = 64
NH = 4
CH = 64
N_MEM = 256
D_IN = 4100
D_INP = 4224
D_MIX = 1280
EPS = 1e-6
NEG_BIG = -1e30
LB_FLOOR = 1e-30
SCALE = HD ** -0.5
TQ = 256
N_DEV = 8
VMEM_LIMIT_BYTES = 56 * 1024 * 1024

ADAM_LR = 0.001
ADAM_B1 = 0.9
ADAM_B2 = 0.999
ADAM_EPS = 1e-08
ADAM_WD = 0.01
ADAM_STEP = 10

PIECES = (("A", 0, 768), ("B", 768, 768), ("C", 1536, 768), ("D", 2304, 256), ("E", 2560, 256),
          ("G", 2816, 1280), ("F", 4096, 128))
BWD_PIECES = (("A", 0, 768), ("Bq", 768, 256), ("Bk", 1024, 256), ("Bv", 1280, 256), ("C", 1536, 768),
              ("D", 2304, 256), ("E", 2560, 256), ("G", 2816, 1280), ("F", 4096, 128))
_ORIG = dict(fq=(0, 256), fk=(256, 512), fv=(512, 768), fg=(768, 1024), ff=(1024, 1028), sq=(1028, 1284),
             sk=(1284, 1540), sv=(1540, 1796), sg=(1796, 2052), hq=(2052, 2308), hf=(2308, 2564),
             hi=(2564, 2820), hg=(2820, 3076), pv=(3076, 3332), pg=(3332, 3588), mq=(3588, 3844), mg=(3844, 4100))
_PERM_ORDER = ("fq", "fk", "fv", "sq", "sk", "sv", "hq", "hf", "hi", "pv", "mq", "fg", "sg", "hg", "pg", "mg", "ff")
_ORIG_ORDER = ("fq", "fk", "fv", "fg", "ff", "sq", "sk", "sv", "sg", "hq", "hf", "hi", "hg", "pv", "pg", "mq", "mg")


def permute_cols(w):
    parts = [w[..., _ORIG[n][0]:_ORIG[n][1]] for n in _PERM_ORDER]
    parts.append(jnp.zeros(w.shape[:-1] + (D_INP - D_IN,), w.dtype))
    return jnp.concatenate(parts, axis=-1)


def unpermute_cols(g):
    start, off = {}, 0
    for n in _PERM_ORDER:
        start[n] = off
        off += _ORIG[n][1] - _ORIG[n][0]
    return jnp.concatenate([g[..., start[n]:start[n] + _ORIG[n][1] - _ORIG[n][0]] for n in _ORIG_ORDER], axis=-1)


def _cast(a):
    return a.astype(_MMT)


def _dg(a, b, ca, cb):
    return lax.dot_general(a, b, (((ca,), (cb,)), ((), ())), preferred_element_type=F32)


@jax.custom_vjp
def mm(a, b):
    return _dg(_cast(a), _cast(b), 1, 0)


@jax.custom_vjp
def mm_nt(a, b):
    return _dg(_cast(a), _cast(b), 1, 1)


@jax.custom_vjp
def mm_tn(a, b):
    return _dg(_cast(a), _cast(b), 0, 0)


mm.defvjp(lambda a, b: (mm(a, b), (a, b)),
          lambda r, g: (mm_nt(g, r[1]).astype(r[0].dtype), mm_tn(r[0], g).astype(r[1].dtype)))
mm_nt.defvjp(lambda a, b: (mm_nt(a, b), (a, b)),
             lambda r, g: (mm(g, r[1]).astype(r[0].dtype), mm_tn(g, r[0]).astype(r[1].dtype)))
mm_tn.defvjp(lambda a, b: (mm_tn(a, b), (a, b)),
             lambda r, g: (mm_nt(r[1], g).astype(r[0].dtype), mm(r[0], g).astype(r[1].dtype)))


def _split(a):
    hi = a.astype(_MMT)
    lo = (a - hi.astype(F32)).astype(_MMT)
    return hi, lo


@jax.custom_vjp
def xr(a, c, ct):
    hi, lo = _split(a)
    cc = _cast(c)
    return _dg(hi, cc, 1, 0) + _dg(lo, cc, 1, 0)


@jax.custom_vjp
def xl(c, ct, a):
    hi, lo = _split(a)
    cc = _cast(c)
    return _dg(cc, hi, 1, 0) + _dg(cc, lo, 1, 0)


xr.defvjp(lambda a, c, ct: (xr(a, c, ct), (c, ct)),
          lambda r, g: (xr(g, r[1], r[0]), jnp.zeros_like(r[0]), jnp.zeros_like(r[1])))
xl.defvjp(lambda c, ct, a: (xl(c, ct, a), (c, ct)),
          lambda r, g: (jnp.zeros_like(r[0]), jnp.zeros_like(r[1]), xl(r[1], r[0], g)))


def _iota(shape, dim):
    return lax.broadcasted_iota(jnp.int32, shape, dim)


def _hmask(h, n=GW):
    lane = _iota((1, n), 1)
    return ((lane >= h * HD) & (lane < (h + 1) * HD)).astype(F32)


def _bdmask(n=GW):
    return ((_iota((n, n), 0) >> 6) == (_iota((n, n), 1) >> 6)).astype(F32)


def _tri(n, kind="le"):
    r, c = _iota((n, n), 0), _iota((n, n), 1)
    return {"le": c <= r, "ge": c >= r, "gt": c > r, "lt": c < r}[kind].astype(F32)


def _onehot_lane(h, n=128):
    return (_iota((1, n), 1) == h).astype(F32)


def _logsig(x):
    return jnp.minimum(x, 0.0) - jnp.log1p(jnp.exp(-jnp.abs(x)))


def _sigmoid(x):
    return 0.5 * (jnp.tanh(0.5 * x) + 1.0)


def _silu(x):
    return x * _sigmoid(x)


def _rms(x, g):
    return x * lax.rsqrt(jnp.mean(x * x, axis=-1, keepdims=True) + EPS) * g


def _headrms(x, w, bd64):
    ms = xr(x * x, bd64, bd64)
    return x * lax.rsqrt(ms + EPS) * w


def _call(body, name, grid, in_specs, out_specs, out_shape, scratch=()):
    return pl.pallas_call(
        body, name=name, grid=grid, in_specs=in_specs, out_specs=out_specs, out_shape=out_shape,
        scratch_shapes=list(scratch),
        compiler_params=pltpu.CompilerParams(dimension_semantics=("arbitrary",) * len(grid),
                                             vmem_limit_bytes=VMEM_LIMIT_BYTES))


def _sds(shape, dtype=F32):
    return jax.ShapeDtypeStruct(shape, dtype)


def _acc(ref, val, first):
    @pl.when(first)
    def _():
        ref[...] = val

    @pl.when(jnp.logical_not(first))
    def _():
        ref[...] += val


def inproj_fwd(x2, g, w, layer, tag):
    t = x2.shape[0]

    def body(x_ref, g_ref, w_ref, ht_ref, *outs):
        h = _rms(x_ref[...], g_ref[...])
        hb = _cast(h)
        ht_ref[...] = _cast(h.T)
        for (_, c0, wd), o in zip(PIECES, outs):
            o[...] = _dg(hb, _cast(w_ref[:, c0:c0 + wd]), 1, 0)

    return _call(
        body, f"inproj_fwd_{tag}", (t // TQ,),
        [pl.BlockSpec((TQ, D_MODEL), lambda i: (i, 0)),
         pl.BlockSpec((None, 1, D_MODEL), lambda i: (layer, 0, 0)),
         pl.BlockSpec((None, D_MODEL, D_INP), lambda i: (layer, 0, 0))],
        [pl.BlockSpec((D_MODEL, TQ), lambda i: (0, i))] + [pl.BlockSpec((TQ, wd), lambda i: (i, 0)) for _, _, wd in PIECES],
        [_sds((D_MODEL, t), _MMT)] + [_sds((t, wd)) for _, _, wd in PIECES],
    )(x2, g, w)


def inproj_bwd_dx(x2, g, w, dy, dpieces, layer, tag):
    t = x2.shape[0]

    def body(x_ref, g_ref, w_ref, dy_ref, *rest):
        dps, (dx_ref, dg_ref) = rest[:len(BWD_PIECES)], rest[len(BWD_PIECES):]
        dh = None
        for (_, c0, wd), dp in zip(BWD_PIECES, dps):
            part = _dg(_cast(dp[...]), _cast(w_ref[:, c0:c0 + wd]), 1, 1)
            dh = part if dh is None else dh + part
        _, vjp = jax.vjp(_rms, x_ref[...], g_ref[...])
        dx, dg = vjp(dh)
        dx_ref[...] = dy_ref[...] + dx
        _acc(dg_ref, dg, pl.program_id(0) == 0)

    return _call(
        body, f"inproj_bwd_dx_{tag}", (t // TQ,),
        [pl.BlockSpec((TQ, D_MODEL), lambda i: (i, 0)),
         pl.BlockSpec((None, 1, D_MODEL), lambda i: (layer, 0, 0)),
         pl.BlockSpec((None, D_MODEL, D_INP), lambda i: (layer, 0, 0)),
         pl.BlockSpec((TQ, D_MODEL), lambda i: (i, 0))] + [pl.BlockSpec((TQ, wd), lambda i: (i, 0)) for _, _, wd in BWD_PIECES],
        [pl.BlockSpec((TQ, D_MODEL), lambda i: (i, 0)), pl.BlockSpec((1, D_MODEL), lambda i: (0, 0))],
        [_sds((t, D_MODEL)), _sds((1, D_MODEL))],
    )(x2, g, w, dy, *dpieces)


def matmul_acc(at, b, tag):
    m, t = at.shape
    n = b.shape[1]
    tn = {1280: 640, 768: 768}.get(n, n)
    tk = 2048 if t % 2048 == 0 else (512 if t % 512 == 0 else TQ)

    def body(a_ref, b_ref, o_ref):
        _acc(o_ref, _dg(_cast(a_ref[...]), _cast(b_ref[...]), 1, 0), pl.program_id(1) == 0)

    return _call(
        body, f"matmul_acc_{tag}", (n // tn, t // tk),
        [pl.BlockSpec((m, tk), lambda j, i: (0, i)), pl.BlockSpec((tk, tn), lambda j, i: (i, j))],
        pl.BlockSpec((m, tn), lambda j, i: (0, j)),
        _sds((m, n)),
    )(at, b)


def _fox_prep_fn(q, k, ff, qw, kw, bias, carry, bd64, tri, trit, last):
    qn = _headrms(q, qw, bd64)
    kn = _headrms(k, kw, bd64)
    lf = _logsig(ff + bias)
    c = xl(tri, trit, lf) + carry
    return qn, kn, c, jnp.sum(c * last, axis=0, keepdims=True)


def _prep_consts():
    return _bdmask() * (1.0 / HD), _tri(TQ), _tri(TQ, "ge"), (_iota((TQ, 1), 0) == TQ - 1).astype(F32)


def fox_prep_fwd(pa, pf, qw, kw, bias, bl, s, layer, tag):
    nq = s // TQ

    def body(q_ref, k_ref, v_ref, f_ref, qw_ref, kw_ref, b_ref, qn_ref, kn_ref, vb_ref, cq_ref, ck_ref, carry):
        @pl.when(pl.program_id(1) == 0)
        def _():
            carry[...] = jnp.zeros_like(carry)

        qn, kn, c, cl = _fox_prep_fn(q_ref[...], k_ref[...], f_ref[...], qw_ref[...], kw_ref[...], b_ref[...],
                                     carry[...], *_prep_consts())
        carry[...] = cl
        qn_ref[...] = _cast(qn)
        kn_ref[...] = _cast(kn)
        vb_ref[...] = _cast(v_ref[...])
        cq_ref[...] = c
        ck_ref[...] = c.T[0:8, :]

    tok = lambda j: pl.BlockSpec((TQ, GW), lambda b, i: (b * nq + i, j))
    par = lambda n: pl.BlockSpec((None, 1, n), lambda b, i: (layer, 0, 0))
    return _call(
        body, f"fox_prep_fwd_{tag}", (bl, nq),
        [tok(0), tok(1), tok(2), pl.BlockSpec((TQ, 128), lambda b, i: (b * nq + i, 0)), par(GW), par(GW), par(128)],
        [tok(0), tok(0), tok(0), pl.BlockSpec((TQ, 128), lambda b, i: (b * nq + i, 0)),
         pl.BlockSpec((None, 8, TQ), lambda b, i: (b, 0, i))],
        [_sds((bl * s, GW), _MMT)] * 3 + [_sds((bl * s, 128)), _sds((bl, 8, s))],
        [pltpu.VMEM((1, 128), F32)],
    )(pa, pa, pa, pf, qw, kw, bias)


def fox_prep_bwd(pa, pf, qw, kw, bias, cq, dqn, dkn, dv, dck, bl, s, layer, tag):
    nq = s // TQ

    def body(q_ref, k_ref, f_ref, qw_ref, kw_ref, b_ref, cq_ref, cprev_ref, dqn_ref, dkn_ref, dv_ref, dck_ref,
             da_ref, df_ref, dqw_ref, dkw_ref, db_ref, dcarry):
        i = pl.program_id(1)
        first = jnp.logical_and(pl.program_id(0) == 0, i == 0)

        @pl.when(i == 0)
        def _():
            dcarry[...] = jnp.zeros_like(dcarry)

        last = (_iota((TQ, 1), 0) == TQ - 1).astype(F32)
        carry_in = jnp.where(i == nq - 1, 0.0, jnp.sum(cprev_ref[...] * last, axis=0, keepdims=True))
        consts = _prep_consts()
        _, vjp = jax.vjp(lambda *a: _fox_prep_fn(*a, *consts), q_ref[...], k_ref[...], f_ref[...], qw_ref[...],
                         kw_ref[...], b_ref[...], carry_in)
        dc = dck_ref[...].T
        dq, dk, dff, dqw, dkw, dbias, dcin = vjp((dqn_ref[...], dkn_ref[...], dc, dcarry[...]))
        dcarry[...] = dcin
        da_ref[:, 0:GW] = dq
        da_ref[:, GW:2 * GW] = dk
        da_ref[:, 2 * GW:3 * GW] = dv_ref[...]
        df_ref[...] = dff
        _acc(dqw_ref, dqw, first)
        _acc(dkw_ref, dkw, first)
        _acc(db_ref, dbias, first)

    rv = lambda b, i: b * nq + (nq - 1 - i)
    tok = lambda j: pl.BlockSpec((TQ, GW), lambda b, i: (rv(b, i), j))
    tok0 = pl.BlockSpec((TQ, GW), lambda b, i: (rv(b, i), 0))
    t128 = pl.BlockSpec((TQ, 128), lambda b, i: (rv(b, i), 0))
    prev = pl.BlockSpec((TQ, 128), lambda b, i: (jnp.maximum(rv(b, i) - 1, 0), 0))
    par = lambda n: pl.BlockSpec((None, 1, n), lambda b, i: (layer, 0, 0))
    acc = lambda n: pl.BlockSpec((1, n), lambda b, i: (0, 0))
    return _call(
        body, f"fox_prep_bwd_{tag}", (bl, nq),
        [tok(0), tok(1), t128, par(GW), par(GW), par(128), t128, prev, tok0, tok0, tok0,
         pl.BlockSpec((None, 128, TQ), lambda b, i: (b, 0, nq - 1 - i))],
        [pl.BlockSpec((TQ, 3 * GW), lambda b, i: (rv(b, i), 0)), t128, acc(GW), acc(GW), acc(128)],
        [_sds((bl * s, 3 * GW)), _sds((bl * s, 128)), _sds((1, GW)), _sds((1, GW)), _sds((1, 128))],
        [pltpu.VMEM((1, 128), F32)],
    )(pa, pa, pf, qw, kw, bias, cq, cq, dqn, dkn, dv, dck)


def _lane_pick(x, h):
    return jnp.sum(x * _onehot_lane(h), axis=-1, keepdims=True)


TA = 128
SROWS = NH * TA


def _stack_heads(x, scale=1.0):
    return _cast(jnp.concatenate([x * (_hmask(h) * scale) for h in range(NH)], axis=0))


def _stack_cols(x):
    return jnp.concatenate([_lane_pick(x, h) for h in range(NH)], axis=0)


def _spread_heads(col):
    return sum(col[h * TA:(h + 1) * TA] * _hmask(h) for h in range(NH))


def _lanes_cat(w):
    return jnp.concatenate([w[h * TA:(h + 1) * TA] for h in range(NH)], axis=1)


def _mask_stack(x):
    return _cast(jnp.concatenate([x * _hmask(h).astype(x.dtype) for h in range(NH)], axis=0))


def _stack_rows(i):
    return i * TA + (_iota((SROWS, 1), 0) & (TA - 1))


def _n_key_tiles(i):
    return lax.shift_right_logical(i * TA, TQ.bit_length() - 1) + 1


def fox_attn_fwd(qn, kn, vb, cq, ck, bl, s, tag):
    nq = s // TA

    def body(q_ref, k_ref, v_ref, cq_ref, ck_ref, o_ref, lse_ref, acc):
        i = pl.program_id(1)
        qs = _stack_heads(q_ref[...].astype(F32), SCALE)
        cqs = _stack_cols(cq_ref[...])
        row = _stack_rows(i)
        acc[...] = jnp.zeros_like(acc)

        def step(j, ml):
            m, l = ml
            ks = pl.ds(pl.multiple_of(j * TQ, TQ), TQ)
            ckb = jnp.concatenate([jnp.broadcast_to(ck_ref[h:h + 1, ks], (TA, TQ)) for h in range(NH)], axis=0)
            sc = _dg(qs, k_ref[ks, :], 1, 1) + cqs - ckb
            col = j * TQ + _iota((1, TQ), 1)
            sc = jnp.where(col <= row, sc, NEG_BIG)
            m_new = jnp.maximum(m, jnp.max(sc, axis=-1, keepdims=True))
            alpha = jnp.exp(m - m_new)
            p = jnp.exp(sc - m_new)
            acc[...] = _spread_heads(alpha) * acc[...] + _dg(_lanes_cat(_cast(p)), _mask_stack(v_ref[ks, :]), 1, 0)
            return m_new, alpha * l + jnp.sum(p, axis=-1, keepdims=True)

        m, l = lax.fori_loop(0, _n_key_tiles(i), step, (jnp.full((SROWS, 1), NEG_BIG, F32), jnp.zeros((SROWS, 1), F32)))
        o_ref[...] = acc[...] / _spread_heads(l)
        lse_h = m + jnp.log(l)
        lse_ref[...] = sum(lse_h[h * TA:(h + 1) * TA] * _onehot_lane(h) for h in range(NH))

    tok = pl.BlockSpec((TA, GW), lambda b, i: (b * nq + i, 0))
    seq = pl.BlockSpec((s, GW), lambda b, i: (b, 0))
    t128 = pl.BlockSpec((TA, 128), lambda b, i: (b * nq + i, 0))
    return _call(
        body, f"fox_attn_fwd_{tag}", (bl, nq),
        [tok, seq, seq, t128, pl.BlockSpec((None, 8, s), lambda b, i: (b, 0, 0))],
        [tok, t128], [_sds((bl * s, GW)), _sds((bl * s, 128))],
        [pltpu.VMEM((TA, GW), F32)],
    )(qn, kn, vb, cq, ck)


def fox_attn_bwd(qn, kn, vb, cq, ck, lse, do, bl, s, tag):
    nq = s // TA

    def body(q_ref, k_ref, v_ref, cq_ref, ck_ref, lse_ref, do_ref, dq_ref, dk_ref, dv_ref, dck_ref, dqa, p_s, dp_s):
        i = pl.program_id(1)

        @pl.when(i == 0)
        def _():
            dk_ref[...] = jnp.zeros_like(dk_ref)
            dv_ref[...] = jnp.zeros_like(dv_ref)
            dck_ref[...] = jnp.zeros_like(dck_ref)

        qs = _stack_heads(q_ref[...].astype(F32), SCALE)
        dos = _stack_heads(do_ref[...])
        cqs, lses = _stack_cols(cq_ref[...]), _stack_cols(lse_ref[...])
        row = _stack_rows(i)
        dqa[...] = jnp.zeros_like(dqa)
        nk = _n_key_tiles(i)

        def probs(j, delta):
            ks = pl.ds(pl.multiple_of(j * TQ, TQ), TQ)
            ckb = jnp.concatenate([jnp.broadcast_to(ck_ref[h:h + 1, ks], (TA, TQ)) for h in range(NH)], axis=0)
            sc = _dg(qs, k_ref[ks, :], 1, 1) + cqs - ckb
            col = j * TQ + _iota((1, TQ), 1)
            p = jnp.where(col <= row, jnp.exp(sc - lses), 0.0)
            dp = _dg(dos, v_ref[ks, :], 1, 1)
            p_s[:, ks] = p
            dp_s[:, ks] = dp
            return delta + jnp.sum(p * dp, axis=-1, keepdims=True)

        delta = lax.fori_loop(0, nk, probs, jnp.zeros((SROWS, 1), F32))

        def step(j, carry):
            ks = pl.ds(pl.multiple_of(j * TQ, TQ), TQ)
            p = p_s[:, ks]
            ds = p * (dp_s[:, ks] - delta)
            dsb = _cast(ds)
            dqa[...] += _dg(_lanes_cat(dsb), _mask_stack(k_ref[ks, :]), 1, 0) * SCALE
            dk_ref[ks, :] += _dg(dsb, qs, 0, 0)
            dv_ref[ks, :] += _dg(_cast(p), dos, 0, 0)
            for h in range(NH):
                dck_ref[h:h + 1, ks] -= jnp.sum(ds[h * TA:(h + 1) * TA], axis=0, keepdims=True)
            return carry

        lax.fori_loop(0, nk, step, 0)
        dq_ref[...] = dqa[...]

    tok = pl.BlockSpec((TA, GW), lambda b, i: (b * nq + i, 0))
    seq = pl.BlockSpec((s, GW), lambda b, i: (b, 0))
    t128 = pl.BlockSpec((TA, 128), lambda b, i: (b * nq + i, 0))
    return _call(
        body, f"fox_attn_bwd_{tag}", (bl, nq),
        [tok, seq, seq, t128, pl.BlockSpec((None, 8, s), lambda b, i: (b, 0, 0)), t128, tok],
        [tok, seq, seq, pl.BlockSpec((None, 128, s), lambda b, i: (b, 0, 0))],
        [_sds((bl * s, GW)), _sds((bl * s, GW)), _sds((bl * s, GW)), _sds((bl, 128, s))],
        [pltpu.VMEM((TA, GW), F32), pltpu.VMEM((SROWS, s), F32), pltpu.VMEM((SROWS, s), F32)],
    )(qn, kn, vb, cq, ck, lse, do)


def _sb_block(qh, kb, row, col, upper, r_carry):
    z = _dg(qh, kb, 1, 1)
    valid = col < row
    ls = _logsig(z)
    lom = jnp.where(valid, ls - z, 0.0)
    between = xr(lom, upper, upper) + r_carry
    w = jnp.where(valid, jnp.exp(ls + between), 0.0)
    return z, ls, lom, w, valid


def sb_attn_fwd(pb, bl, s, tag):
    nq = s // TA

    def body(q_ref, k_ref, v_ref, o_ref, acc):
        i = pl.program_id(1)
        qs = _stack_heads(q_ref[...], SCALE)
        row = _stack_rows(i)
        upper = _tri(TQ, "lt")
        acc[...] = jnp.zeros_like(acc)
        last = _n_key_tiles(i) - 1

        def step(jj, r):
            j = last - jj
            ks = pl.ds(pl.multiple_of(j * TQ, TQ), TQ)
            col = j * TQ + _iota((1, TQ), 1)
            _, _, lom, w, _ = _sb_block(qs, _cast(k_ref[ks, :]), row, col, upper, r)
            acc[...] += _dg(_lanes_cat(_cast(w)), _mask_stack(v_ref[ks, :]), 1, 0)
            return r + jnp.sum(lom, axis=-1, keepdims=True)

        lax.fori_loop(0, last + 1, step, jnp.zeros((SROWS, 1), F32))
        o_ref[...] = acc[...]

    tok = lambda j: pl.BlockSpec((TA, GW), lambda b, i: (b * nq + i, j))
    seq = lambda j: pl.BlockSpec((s, GW), lambda b, i: (b, j))
    return _call(
        body, f"sb_attn_fwd_{tag}", (bl, nq), [tok(0), seq(1), seq(2)],
        pl.BlockSpec((TA, GW), lambda b, i: (b * nq + i, 0)), _sds((bl * s, GW)),
        [pltpu.VMEM((TA, GW), F32)],
    )(pb, pb, pb)


def sb_attn_bwd(pb, do, bl, s, tag):
    nq = s // TA

    def body(q_ref, k_ref, v_ref, do_ref, dq_ref, dk_ref, dv_ref, dqa, ls_s, lom_s, w_s, g_s):
        i = pl.program_id(1)

        @pl.when(i == 0)
        def _():
            dk_ref[...] = jnp.zeros_like(dk_ref)
            dv_ref[...] = jnp.zeros_like(dv_ref)

        qs = _stack_heads(q_ref[...], SCALE)
        dos = _stack_heads(do_ref[...])
        row = _stack_rows(i)
        upper = _tri(TQ, "lt")
        before = _tri(TQ, "gt")
        dqa[...] = jnp.zeros_like(dqa)
        last = _n_key_tiles(i) - 1

        def weights(jj, r):
            j = last - jj
            ks = pl.ds(pl.multiple_of(j * TQ, TQ), TQ)
            col = j * TQ + _iota((1, TQ), 1)
            _, ls, lom, w, _ = _sb_block(qs, _cast(k_ref[ks, :]), row, col, upper, r)
            ls_s[:, ks] = ls
            lom_s[:, ks] = lom
            w_s[:, ks] = _cast(w)
            g_s[:, ks] = _dg(dos, _cast(v_ref[ks, :]), 1, 1) * w
            return r + jnp.sum(lom, axis=-1, keepdims=True)

        lax.fori_loop(0, last + 1, weights, jnp.zeros((SROWS, 1), F32))

        def step(j, cpre):
            ks = pl.ds(pl.multiple_of(j * TQ, TQ), TQ)
            col = j * TQ + _iota((1, TQ), 1)
            g = g_s[:, ks]
            pre = cpre + xr(g, before, before)
            dz = jnp.where(col < row, g * jnp.exp(lom_s[:, ks]) - jnp.exp(ls_s[:, ks]) * pre, 0.0)
            dzb = _cast(dz)
            dqa[...] += _dg(_lanes_cat(dzb), _mask_stack(k_ref[ks, :]), 1, 0) * SCALE
            dk_ref[ks, :] += _dg(dzb, qs, 0, 0)
            dv_ref[ks, :] += _dg(w_s[:, ks], dos, 0, 0)
            return cpre + jnp.sum(g, axis=-1, keepdims=True)

        lax.fori_loop(0, last + 1, step, jnp.zeros((SROWS, 1), F32))
        dq_ref[...] = dqa[...]

    tok = lambda j: pl.BlockSpec((TA, GW), lambda b, i: (b * nq + i, j))
    seq = lambda j: pl.BlockSpec((s, GW), lambda b, i: (b, j))
    return _call(
        body, f"sb_attn_bwd_{tag}", (bl, nq), [tok(0), seq(1), seq(2), tok(0)],
        [tok(0), seq(0), seq(0)], [_sds((bl * s, GW))] * 3,
        [pltpu.VMEM((TA, GW), F32), pltpu.VMEM((SROWS, s), F32), pltpu.VMEM((SROWS, s), F32),
         pltpu.VMEM((SROWS, s), _MMT), pltpu.VMEM((SROWS, s), F32)],
    )(pb, pb, pb, do)


def _hgrn_consts():
    r, c = _iota((CH, CH), 0), _iota((CH, CH), 1)
    rr = _iota((CH, 1), 0)
    tri = (c <= r).astype(F32)
    lv = []
    for m in (8, 4, 2, 1):
        up = ((rr & (2 * m - 1)) >= m).astype(F32)
        lo = 1.0 - up
        selq = (((r & (2 * m - 1)) >= m) & (c == (r & ~(m - 1)) - 1)).astype(F32)
        selk = (((r & (2 * m - 1)) < m) & (c == (r & ~(m - 1)) + m - 1)).astype(F32)
        pm = (((r & ~(2 * m - 1)) == (c & ~(2 * m - 1))) & ((r & (2 * m - 1)) >= m) & ((c & (2 * m - 1)) < m)).astype(F32)
        lv.append((up, lo, selq, selq.T, selk, selk.T, jnp.concatenate([pm] * NH, axis=0)))
    hm4 = lambda n: (((_iota((NH, 1, n), 2) & (GW - 1)) >> 6) == _iota((NH, 1, n), 0)).astype(F32)
    return dict(tri=tri, trit=tri.T, rr=rr, lv=lv, bd=_bdmask(), bd64=_bdmask() * (1.0 / HD),
                hm4={GW: hm4(GW), 3 * GW: hm4(3 * GW)})


def _hgrn_chunk_fn(hq, hf, hi, lb, wn, st, cs):
    q = _silu(hq)
    log_lb = jnp.log(jnp.maximum(lb, LB_FLOOR))
    a, bb = log_lb, jnp.log1p(-lb) + _logsig(hf)
    g = jnp.maximum(a, bb) + jnp.log1p(jnp.exp(-jnp.abs(a - bb)))
    k = (1.0 - lb) * _sigmoid(-hf)
    v = hi
    rr = cs["rr"]
    b = xl(cs["tri"], cs["trit"], g)
    row_of = lambda n: jnp.sum(b * (rr == n).astype(F32), axis=0, keepdims=True)
    o = mm_nt(q * jnp.exp(b), st)
    qs, ks = [], []
    for ib in (1, 2, 3):
        ref = row_of(16 * ib - 1)
        inq = ((rr >= 16 * ib) & (rr < 16 * ib + 16)).astype(F32)
        ink = (rr < 16 * ib).astype(F32)
        qs.append(q * jnp.exp((b - ref) * inq) * inq)
        ks.append(k * jnp.exp((ref - b) * ink) * ink)
    qcat, kcat = jnp.concatenate(qs, axis=1), jnp.concatenate(ks, axis=1)
    lvl = []
    for up, lo, selq, selqt, selk, selkt, pm in cs["lv"]:
        qe = q * jnp.exp((b - xl(selq, selqt, b)) * up) * up
        ke = k * jnp.exp((xl(selk, selkt, b) - b) * lo) * lo
        lvl.append((qe, ke, pm))
    stack = lambda x: (x[None] * cs["hm4"][x.shape[1]]).reshape(NH * CH, x.shape[1])
    a_all = mm_nt(stack(qcat), kcat)
    for qe, ke, pm4 in lvl:
        a_all = a_all + mm_nt(stack(qe), ke) * pm4
    o = o + jnp.sum(mm(a_all, v).reshape(NH, CH, GW) * cs["hm4"][GW], axis=0)
    o = o + xr(q * k, cs["bd"], cs["bd"]) * v
    b_last = row_of(CH - 1)
    st_new = st * jnp.exp(b_last) + mm_tn(v, k * jnp.exp(b_last - b)) * cs["bd"]
    return _headrms(o, wn, cs["bd64"]), st_new


def hgrn_fwd(pc, lb, wn, bl, s, layer, tag):
    nc = s // CH

    def body(q_ref, f_ref, i_ref, lb_ref, wn_ref, o_ref, st_ref, st):
        @pl.when(pl.program_id(0) == 0)
        def _():
            st[...] = jnp.zeros_like(st)

        cs = _hgrn_consts()
        for b in range(bl):
            st_ref[b] = st[b]
            o, st_new = _hgrn_chunk_fn(q_ref[b], f_ref[b], i_ref[b], lb_ref[...], wn_ref[...], st[b], cs)
            o_ref[b] = o
            st[b] = st_new

    tok = lambda j: pl.BlockSpec((bl, CH, GW), lambda c: (0, c, j))
    par = pl.BlockSpec((None, 1, GW), lambda c: (layer, 0, 0))
    pc3 = pc.reshape(bl, s, 3 * GW)
    o, states = _call(
        body, f"hgrn_fwd_{tag}", (nc,), [tok(0), tok(1), tok(2), par, par],
        [tok(0), pl.BlockSpec((bl, None, GW, GW), lambda c: (0, c, 0, 0))],
        [_sds((bl, s, GW)), _sds((bl, nc, GW, GW))],
        [pltpu.VMEM((bl, GW, GW), F32)],
    )(pc3, pc3, pc3, lb, wn)
    return o.reshape(bl * s, GW), states


def hgrn_bwd(pc, lb, wn, states, do, bl, s, layer, tag):
    nc = s // CH

    def body(q_ref, f_ref, i_ref, lb_ref, wn_ref, st_ref, do_ref, dc_ref, dlb_ref, dwn_ref, dst):
        c = pl.program_id(0)

        @pl.when(c == 0)
        def _():
            dst[...] = jnp.zeros_like(dst)

        cs = _hgrn_consts()
        dlb_sum = dwn_sum = None
        for b in range(bl):
            _, vjp = jax.vjp(lambda *a: _hgrn_chunk_fn(*a, cs), q_ref[b], f_ref[b], i_ref[b], lb_ref[...],
                             wn_ref[...], st_ref[b])
            dq, df, di, dlb, dwn, dst_in = vjp((do_ref[b], dst[b]))
            dst[b] = dst_in
            dc_ref[b, :, 0:GW] = dq
            dc_ref[b, :, GW:2 * GW] = df
            dc_ref[b, :, 2 * GW:3 * GW] = di
            dlb_sum = dlb if dlb_sum is None else dlb_sum + dlb
            dwn_sum = dwn if dwn_sum is None else dwn_sum + dwn
        _acc(dlb_ref, dlb_sum, c == 0)
        _acc(dwn_ref, dwn_sum, c == 0)

    tok = lambda j: pl.BlockSpec((bl, CH, GW), lambda c: (0, nc - 1 - c, j))
    par = pl.BlockSpec((None, 1, GW), lambda c: (layer, 0, 0))
    acc = pl.BlockSpec((1, GW), lambda c: (0, 0))
    pc3 = pc.reshape(bl, s, 3 * GW)
    dc, dlb, dwn = _call(
        body, f"hgrn_bwd_{tag}", (nc,),
        [tok(0), tok(1), tok(2), par, par, pl.BlockSpec((bl, None, GW, GW), lambda c: (0, nc - 1 - c, 0, 0)), tok(0)],
        [pl.BlockSpec((bl, CH, 3 * GW), lambda c: (0, nc - 1 - c, 0)), acc, acc],
        [_sds((bl, s, 3 * GW)), _sds((1, GW)), _sds((1, GW))],
        [pltpu.VMEM((bl, GW, GW), F32)],
    )(pc3, pc3, pc3, lb, wn, states, do.reshape(bl, s, GW))
    return dc.reshape(bl * s, 3 * GW), dlb, dwn


def _shift_rows(x, k, up):
    n = x.shape[0]
    rr = _iota((n, 1), 0)
    if up:
        return jnp.where(rr < n - k, pltpu.roll(x, n - k, 0), 0.0)
    return jnp.where(rr >= k, pltpu.roll(x, k, 0), 0.0)


def _window_sums(x, up):
    s2 = x + _shift_rows(x, 1, up)
    s4 = s2 + _shift_rows(s2, 2, up)
    s8 = s4 + _shift_rows(s4, 4, up)
    s16 = s8 + _shift_rows(s8, 8, up)
    return s2, s4, s8, s16


def _pool_div(n):
    pos = (_iota((n, 1), 0) + 1).astype(F32)
    return [jnp.minimum(pos, float(w)) for w in (2, 4, 8, 16)]


def _pool_mix(sums, scaled):
    out = None
    for gi, sw in enumerate(sums):
        part = (sw if scaled is None else sw / scaled[gi]) * _hmask(gi)
        out = part if out is None else out + part
    return out


def pool_fwd(pd, wbd, scale, bl, s, layer, tag):
    def body(u_ref, w_ref, sc_ref, o_ref):
        u = u_ref[...]
        pm = _pool_mix(_window_sums(u, False), _pool_div(s)) - u
        o_ref[...] = _dg(_cast(pm), _cast(w_ref[...]), 1, 0) * sc_ref[...]

    seq = pl.BlockSpec((s, GW), lambda b: (b, 0))
    return _call(
        body, f"pool_fwd_{tag}", (bl,),
        [seq, pl.BlockSpec((None, GW, GW), lambda b: (layer, 0, 0)), pl.BlockSpec((None, 1, GW), lambda b: (layer, 0, 0))],
        seq, _sds((bl * s, GW)),
    )(pd, wbd, scale)


def pool_bwd(pd, wbd, scale, do, bl, s, layer, tag):
    def body(u_ref, w_ref, sc_ref, do_ref, du_ref, dw_ref, dsc_ref):
        first = pl.program_id(0) == 0
        u, do = u_ref[...], do_ref[...]
        div = _pool_div(s)
        pm = _pool_mix(_window_sums(u, False), div) - u
        ypre = _dg(_cast(pm), _cast(w_ref[...]), 1, 0)
        dys = do * sc_ref[...]
        _acc(dsc_ref, jnp.sum(do * ypre, axis=0, keepdims=True), first)
        _acc(dw_ref, _dg(_cast(pm), _cast(dys), 0, 0), first)
        dpm = _dg(_cast(dys), _cast(w_ref[...]), 1, 1)
        dsc = [dpm / d for d in div]
        adj = None
        for gi in range(4):
            part = _window_sums(dsc[gi] * _hmask(gi), True)[gi]
            adj = part if adj is None else adj + part
        du_ref[...] = adj - dpm

    seq = pl.BlockSpec((s, GW), lambda b: (b, 0))
    return _call(
        body, f"pool_bwd_{tag}", (bl,),
        [seq, pl.BlockSpec((None, GW, GW), lambda b: (layer, 0, 0)), pl.BlockSpec((None, 1, GW), lambda b: (layer, 0, 0)), seq],
        [seq, pl.BlockSpec((GW, GW), lambda b: (0, 0)), pl.BlockSpec((1, GW), lambda b: (0, 0))],
        [_sds((bl * s, GW)), _sds((GW, GW)), _sds((1, GW))],
    )(pd, wbd, scale, do)


def _mem_prep_fn(mem, g, wk, wv, kw, bd64):
    mn = _rms(mem, g)
    return _headrms(mm(mn, wk), kw, bd64), mm(mn, wv)


def mem_prep_fwd(mem2, g, wkv, kw, bl, layer, tag):
    def body(m_ref, g_ref, wk_ref, wv_ref, kw_ref, k_ref, v_ref):
        k, v = _mem_prep_fn(m_ref[...], g_ref[...], wk_ref[...], wv_ref[...], kw_ref[...], _bdmask() * (1.0 / HD))
        k_ref[...] = k
        v_ref[...] = v

    blk = pl.BlockSpec((N_MEM, GW), lambda b: (b, 0))
    return _call(
        body, f"mem_prep_fwd_{tag}", (bl,),
        [pl.BlockSpec((N_MEM, D_MODEL), lambda b: (b, 0)), pl.BlockSpec((None, 1, D_MODEL), lambda b: (layer, 0, 0)),
         pl.BlockSpec((None, D_MODEL, GW), lambda b: (layer, 0, 0)), pl.BlockSpec((None, D_MODEL, GW), lambda b: (layer, 0, 1)),
         pl.BlockSpec((None, 1, GW), lambda b: (layer, 0, 0))],
        [blk, blk], [_sds((bl * N_MEM, GW))] * 2,
    )(mem2, g, wkv, wkv, kw)


def mem_prep_bwd(mem2, g, wkv, kw, dk, dv, bl, layer, tag):
    def body(m_ref, g_ref, wk_ref, wv_ref, kw_ref, dk_ref, dv_ref, dwk_ref, dwv_ref, dg_ref, dkw_ref):
        first = pl.program_id(0) == 0
        bd64 = _bdmask() * (1.0 / HD)
        _, vjp = jax.vjp(lambda g_, wk, wv, kw_: _mem_prep_fn(m_ref[...], g_, wk, wv, kw_, bd64),
                         g_ref[...], wk_ref[...].astype(F32), wv_ref[...].astype(F32), kw_ref[...])
        dg, dwk, dwv, dkw = vjp((dk_ref[...], dv_ref[...]))
        _acc(dwk_ref, dwk, first)
        _acc(dwv_ref, dwv, first)
        _acc(dg_ref, dg, first)
        _acc(dkw_ref, dkw, first)

    blk = pl.BlockSpec((N_MEM, GW), lambda b: (b, 0))
    return _call(
        body, f"mem_prep_bwd_{tag}", (bl,),
        [pl.BlockSpec((N_MEM, D_MODEL), lambda b: (b, 0)), pl.BlockSpec((None, 1, D_MODEL), lambda b: (layer, 0, 0)),
         pl.BlockSpec((None, D_MODEL, GW), lambda b: (layer, 0, 0)), pl.BlockSpec((None, D_MODEL, GW), lambda b: (layer, 0, 1)),
         pl.BlockSpec((None, 1, GW), lambda b: (layer, 0, 0)), blk, blk],
        [pl.BlockSpec((D_MODEL, GW), lambda b: (0, 0)), pl.BlockSpec((D_MODEL, GW), lambda b: (0, 0)),
         pl.BlockSpec((1, D_MODEL), lambda b: (0, 0)), pl.BlockSpec((1, GW), lambda b: (0, 0))],
        [_sds((D_MODEL, GW)), _sds((D_MODEL, GW)), _sds((1, D_MODEL)), _sds((1, GW))],
    )(mem2, g, wkv, wkv, kw, dk, dv)


def _mem_attn_fn(mq, qw, k, v, bd64):
    qn = _headrms(mq, qw, bd64)
    out = None
    for h in range(NH):
        hm = _hmask(h)
        lg = mm_nt(qn * hm, k) * SCALE
        e = jnp.exp(lg - lax.stop_gradient(jnp.max(lg, axis=-1, keepdims=True)))
        p = e / jnp.sum(e, axis=-1, keepdims=True)
        part = mm(p, v) * hm
        out = part if out is None else out + part
    return out


def mem_attn_fwd(pe, qw, k, v, bl, s, layer, tag):
    nq = s // TQ

    def body(q_ref, qw_ref, k_ref, v_ref, o_ref):
        o_ref[...] = _mem_attn_fn(q_ref[...], qw_ref[...], k_ref[...], v_ref[...], _bdmask() * (1.0 / HD))

    tok = pl.BlockSpec((TQ, GW), lambda b, i: (b * nq + i, 0))
    kv = pl.BlockSpec((N_MEM, GW), lambda b, i: (b, 0))
    return _call(
        body, f"mem_attn_fwd_{tag}", (bl, nq), [tok, pl.BlockSpec((None, 1, GW), lambda b, i: (layer, 0, 0)), kv, kv],
        tok, _sds((bl * s, GW)),
    )(pe, qw, k, v)


def mem_attn_bwd(pe, qw, k, v, do, bl, s, layer, tag):
    nq = s // TQ

    def body(q_ref, qw_ref, k_ref, v_ref, do_ref, dq_ref, dk_ref, dv_ref, dqw_ref):
        i = pl.program_id(1)
        bd64 = _bdmask() * (1.0 / HD)
        _, vjp = jax.vjp(lambda *a: _mem_attn_fn(*a, bd64), q_ref[...], qw_ref[...], k_ref[...], v_ref[...])
        dq, dqw, dk, dv = vjp(do_ref[...])
        dq_ref[...] = dq
        _acc(dk_ref, dk, i == 0)
        _acc(dv_ref, dv, i == 0)
        _acc(dqw_ref, dqw, jnp.logical_and(pl.program_id(0) == 0, i == 0))

    tok = pl.BlockSpec((TQ, GW), lambda b, i: (b * nq + i, 0))
    kv = pl.BlockSpec((N_MEM, GW), lambda b, i: (b, 0))
    return _call(
        body, f"mem_attn_bwd_{tag}", (bl, nq),
        [tok, pl.BlockSpec((None, 1, GW), lambda b, i: (layer, 0, 0)), kv, kv, tok],
        [tok, kv, kv, pl.BlockSpec((1, GW), lambda b, i: (0, 0))],
        [_sds((bl * s, GW)), _sds((bl * N_MEM, GW)), _sds((bl * N_MEM, GW)), _sds((1, GW))],
    )(pe, qw, k, v, do)


def _gate_out_fn(outs, gates, wparts):
    y = None
    for o, g, w in zip(outs, gates, wparts):
        part = mm(o * _silu(g), w)
        y = part if y is None else y + part
    return y


def outproj_fwd(x2, outs, pg, wout, layer, tag):
    t = x2.shape[0]

    def body(x_ref, oa, ob, oc, od, oe, g_ref, w_ref, y_ref):
        outs_ = [r[...] for r in (oa, ob, oc, od, oe)]
        gates = [g_ref[:, j * GW:(j + 1) * GW] for j in range(5)]
        wparts = [w_ref[j * GW:(j + 1) * GW, :] for j in range(5)]
        y_ref[...] = x_ref[...] + _gate_out_fn(outs_, gates, wparts)

    tok = pl.BlockSpec((TQ, GW), lambda i: (i, 0))
    big = pl.BlockSpec((TQ, D_MODEL), lambda i: (i, 0))
    return _call(
        body, f"outproj_fwd_{tag}", (t // TQ,),
        [big] + [tok] * 5 + [pl.BlockSpec((TQ, D_MIX), lambda i: (i, 0)),
                            pl.BlockSpec((None, D_MIX, D_MODEL), lambda i: (layer, 0, 0))],
        big, _sds((t, D_MODEL)),
    )(x2, *outs, pg, wout)


def outproj_bwd(outs, pg, wout, dy, layer, tag):
    t = dy.shape[0]

    def body(oa, ob, oc, od, oe, g_ref, w_ref, dy_ref, da, db, dc, dd, de, dg_ref, dw_ref):
        outs_ = [r[...] for r in (oa, ob, oc, od, oe)]
        gates = [g_ref[:, j * GW:(j + 1) * GW] for j in range(5)]
        wparts = [w_ref[j * GW:(j + 1) * GW, :].astype(F32) for j in range(5)]
        _, vjp = jax.vjp(_gate_out_fn, outs_, gates, wparts)
        douts, dgates, dws = vjp(dy_ref[...])
        for r, val in zip((da, db, dc, dd, de), douts):
            r[...] = val
        first = pl.program_id(0) == 0
        for j in range(5):
            dg_ref[:, j * GW:(j + 1) * GW] = dgates[j]

        @pl.when(first)
        def _():
            for j in range(5):
                dw_ref[j * GW:(j + 1) * GW, :] = dws[j]

        @pl.when(jnp.logical_not(first))
        def _():
            for j in range(5):
                dw_ref[j * GW:(j + 1) * GW, :] += dws[j]

    tok = pl.BlockSpec((TQ, GW), lambda i: (i, 0))
    return _call(
        body, f"outproj_bwd_{tag}", (t // TQ,),
        [tok] * 5 + [pl.BlockSpec((TQ, D_MIX), lambda i: (i, 0)), pl.BlockSpec((None, D_MIX, D_MODEL), lambda i: (layer, 0, 0)),
                     pl.BlockSpec((TQ, D_MODEL), lambda i: (i, 0))],
        [tok] * 5 + [pl.BlockSpec((TQ, D_MIX), lambda i: (i, 0)), pl.BlockSpec((D_MIX, D_MODEL), lambda i: (0, 0))],
        [_sds((t, GW))] * 5 + [_sds((t, D_MIX)), _sds((D_MIX, D_MODEL))],
    )(*outs, pg, wout, dy)


def loss_head(y, tgt):
    t = y.shape[0]

    def body(y_ref, t_ref, l_ref, dy_ref):
        diff = y_ref[...] - t_ref[...]
        dy_ref[...] = diff * (1.0 / D_MODEL)
        part = 0.5 * jnp.sum(jnp.sum(diff * diff, axis=-1, keepdims=True) * (1.0 / D_MODEL), axis=0, keepdims=True)
        _acc(l_ref, jnp.broadcast_to(part, (8, 128)), pl.program_id(0) == 0)

    big = pl.BlockSpec((TQ, D_MODEL), lambda i: (i, 0))
    return _call(body, "loss_head", (t // TQ,), [big, big], [pl.BlockSpec((8, 128), lambda i: (0, 0)), big],
                 [_sds((8, 128)), _sds((t, D_MODEL))])(y, tgt)


def layer_fwd(x2, mem2, p, layer, bl, s):
    tag = f"l{layer}"
    ht, pa, pb, pc, pd, pe, pg, pf = inproj_fwd(x2, p["norm_g"], p["w_in"], layer, tag)
    qn, kn, vb, cq, ck = fox_prep_fwd(pa, pf, p["fox_q_norm"], p["fox_k_norm"], p["fox_f_bias"], bl, s, layer, tag)
    oa, lse = fox_attn_fwd(qn, kn, vb, cq, ck, bl, s, tag)
    ob = sb_attn_fwd(pb, bl, s, tag)
    oc, states = hgrn_fwd(pc, p["lb"], p["hgrn_out_norm"], bl, s, layer, tag)
    od = pool_fwd(pd, p["pool_wbd"], p["pool_scale"], bl, s, layer, tag)
    mk, mv = mem_prep_fwd(mem2, p["mem_norm_g"], p["mem_w_kv"], p["mem_k_norm"], bl, layer, tag)
    oe = mem_attn_fwd(pe, p["mem_q_norm"], mk, mv, bl, s, layer, tag)
    y = outproj_fwd(x2, (oa, ob, oc, od, oe), pg, p["w_out"], layer, tag)
    saved = dict(x2=x2, ht=ht, pa=pa, pb=pb, pc=pc, pd=pd, pe=pe, pg=pg, pf=pf, qn=qn, kn=kn, vb=vb, cq=cq, ck=ck,
                 oa=oa, lse=lse, ob=ob, oc=oc, states=states, od=od, mk=mk, mv=mv, oe=oe)
    return y, saved


def layer_bwd(dy, mem2, p, sv, layer, bl, s):
    tag = f"l{layer}"
    (doa, dob, doc, dod, doe, dg_gates, dwout) = outproj_bwd((sv["oa"], sv["ob"], sv["oc"], sv["od"], sv["oe"]), sv["pg"],
                                                              p["w_out"], dy, layer, tag)
    dqn, dkn, dv, dck = fox_attn_bwd(sv["qn"], sv["kn"], sv["vb"], sv["cq"], sv["ck"], sv["lse"], doa, bl, s, tag)
    d_a, d_f, dqw, dkw, dbias = fox_prep_bwd(sv["pa"], sv["pf"], p["fox_q_norm"], p["fox_k_norm"], p["fox_f_bias"], sv["cq"],
                                             dqn, dkn, dv, dck, bl, s, layer, tag)
    dsq, dsk, dsv = sb_attn_bwd(sv["pb"], dob, bl, s, tag)
    d_c, dlb, dwn = hgrn_bwd(sv["pc"], p["lb"], p["hgrn_out_norm"], sv["states"], doc, bl, s, layer, tag)
    d_d, dwbd, dpscale = pool_bwd(sv["pd"], p["pool_wbd"], p["pool_scale"], dod, bl, s, layer, tag)
    d_e, dmk, dmv, dmqw = mem_attn_bwd(sv["pe"], p["mem_q_norm"], sv["mk"], sv["mv"], doe, bl, s, layer, tag)
    dwk, dwv, dmg, dmkw = mem_prep_bwd(mem2, p["mem_norm_g"], p["mem_w_kv"], p["mem_k_norm"], dmk, dmv, bl, layer, tag)
    dpieces = (d_a, dsq, dsk, dsv, d_c, d_d, d_e, dg_gates, d_f)
    dx, dng = inproj_bwd_dx(sv["x2"], p["norm_g"], p["w_in"], dy, dpieces, layer, tag)
    dwin = jnp.concatenate([matmul_acc(sv["ht"], dp, f"{tag}_{nm}") for (nm, _, _), dp in zip(BWD_PIECES, dpieces)], axis=1)
    grads = dict(norm_g=dng, w_in=dwin, fox_f_bias=dbias, fox_q_norm=dqw, fox_k_norm=dkw, lb=dlb, hgrn_out_norm=dwn,
                 pool_wbd=dwbd, pool_scale=dpscale, mem_norm_g=dmg, mem_w_kv=jnp.concatenate([dwk, dwv], axis=1),
                 mem_q_norm=dmqw, mem_k_norm=dmkw, w_out=dwout)
    return dx, grads


def _tile4(w):
    return jnp.tile(w, (1, NH))[:, None, :]


def prepare_params(norm_g, w_in_p, fox_f_bias, fox_q_norm, fox_k_norm, hgrn_lb_logits, hgrn_out_norm, pool_w, pool_scale,
                   mem_norm_g, mem_w_kv, mem_q_norm, mem_k_norm, w_out):
    p1 = jax.nn.sigmoid(hgrn_lb_logits[1] - hgrn_lb_logits[0])
    lb = jnp.stack([jnp.zeros_like(p1), jnp.clip(p1, 0.0, 1.0 - 1e-6)])
    eye = jnp.eye(4, dtype=F32)
    wbd = jnp.einsum("lgcd,gh->lgchd", pool_w, eye).reshape(2, GW, GW)
    return dict(norm_g=norm_g[:, None, :], w_in=w_in_p, fox_f_bias=jnp.pad(fox_f_bias, ((0, 0), (0, 124)))[:, None, :],
                fox_q_norm=_tile4(fox_q_norm), fox_k_norm=_tile4(fox_k_norm), lb=lb[:, None, :],
                hgrn_out_norm=hgrn_out_norm[:, None, :], pool_wbd=wbd, pool_scale=pool_scale[:, None, :],
                mem_norm_g=mem_norm_g[:, None, :], mem_w_kv=mem_w_kv, mem_q_norm=_tile4(mem_q_norm),
                mem_k_norm=_tile4(mem_k_norm), w_out=w_out)


def local_step(x, mem, tgt, p):
    bl, s, _ = x.shape
    x2, mem2, tgt2 = x.reshape(bl * s, D_MODEL), mem.reshape(bl * N_MEM, D_MODEL), tgt.reshape(bl * s, D_MODEL)
    y0, sv0 = layer_fwd(x2, mem2, p, 0, bl, s)
    y1, sv1 = layer_fwd(y0, mem2, p, 1, bl, s)
    lpart, dy = loss_head(y1, tgt2)
    dx1, g1 = layer_bwd(dy, mem2, p, sv1, 1, bl, s)
    dx0, g0 = layer_bwd(dx1, mem2, p, sv0, 0, bl, s)
    return lpart[0, 0], dx0.reshape(bl, s, D_MODEL), g0, g1


_ANY = pl.BlockSpec(memory_space=pl.ANY)


def _me_and_peers():
    x, y, c = lax.axis_index("x"), lax.axis_index("y"), lax.axis_index("c")
    peers = []
    for k in range(1, N_DEV):
        px = 1 - x if (k >> 2) & 1 else x
        py = 1 - y if (k >> 1) & 1 else y
        pc = 1 - c if k & 1 else c
        peers.append(((px, py, pc), 4 * px + 2 * py + pc))
    return 4 * x + 2 * y + c, peers


def all_gather_rows(xs, tag):
    nl, r, c = xs.shape

    def body(x_ref, o_ref, send_sems, recv_sems, local_sem):
        x, y, cc = lax.axis_index("x"), lax.axis_index("y"), lax.axis_index("c")
        me, sibling = (x, y, cc), (x, y, 1 - cc)
        chips = [(1 - x, y), (x, 1 - y), (1 - x, 1 - y)]

        def rows(px, py, pc):
            return o_ref.at[:, pl.ds((4 * px + 2 * py + pc) * r, r), :]

        def copy(k, block, to, src=None):
            return pltpu.make_async_remote_copy(src_ref=rows(*block) if src is None else src, dst_ref=rows(*block),
                                                send_sem=send_sems.at[k], recv_sem=recv_sems.at[k], device_id=to,
                                                device_id_type=pl.DeviceIdType.MESH)

        mine = pltpu.make_async_copy(x_ref, rows(*me), local_sem)
        mine.start()
        first = [copy(0, me, sibling, src=x_ref)] + [copy(1 + j, me, (*chip, cc), src=x_ref) for j, chip in enumerate(chips)]
        for cp in first:
            cp.start()
        passed = [copy(4 + j, (*chip, cc), sibling) for j, chip in enumerate(chips)]
        for j, chip in enumerate(chips):
            copy(1 + j, (*chip, cc), me).wait_recv()
            passed[j].start()
        copy(0, sibling, me).wait_recv()
        for j, chip in enumerate(chips):
            copy(4 + j, (*chip, 1 - cc), me).wait_recv()
        for cp in first + passed:
            cp.wait_send()
        mine.wait()

    return pl.pallas_call(
        body, name=f"all_gather_{tag}", in_specs=[_ANY], out_specs=_ANY, out_shape=_sds((nl, N_DEV * r, c), xs.dtype),
        scratch_shapes=[pltpu.SemaphoreType.DMA((N_DEV - 1,)), pltpu.SemaphoreType.DMA((N_DEV - 1,)), pltpu.SemaphoreType.DMA],
    )(xs)


def exchange_cores(part, tag):
    nl, r8, c = part.shape
    r = r8 // N_DEV

    def body(p_ref, mine_ref, theirs_ref, send_sems, recv_sems, local_sems):
        x, y, cc = lax.axis_index("x"), lax.axis_index("y"), lax.axis_index("c")
        copies = []
        for l in range(nl):
            for q in range(4):
                k = l * 4 + q
                copies.append(pltpu.make_async_copy(p_ref.at[l, q, pl.ds(cc, 1)], mine_ref.at[l, q], local_sems.at[k]))
                copies.append(pltpu.make_async_remote_copy(
                    src_ref=p_ref.at[l, q, pl.ds(1 - cc, 1)], dst_ref=theirs_ref.at[l, q], send_sem=send_sems.at[k],
                    recv_sem=recv_sems.at[k], device_id=(x, y, 1 - cc), device_id_type=pl.DeviceIdType.MESH))
        for cp in copies:
            cp.start()
        for cp in copies:
            cp.wait()

    shp = _sds((nl, 4, 1, r, c), part.dtype)
    nsem = pltpu.SemaphoreType.DMA((nl * 4,))
    return pl.pallas_call(
        body, name=f"exchange_cores_{tag}", in_specs=[_ANY], out_specs=[_ANY, _ANY], out_shape=[shp, shp],
        scratch_shapes=[nsem, nsem, nsem],
    )(part.reshape(nl, 4, 2, r, c))


def exchange_chips(s4, tag):
    nl, _, _, r, c = s4.shape

    def body(s_ref, o_ref, send_sems, recv_sems, local_sem):
        x, y, cc = lax.axis_index("x"), lax.axis_index("y"), lax.axis_index("c")
        local = pltpu.make_async_copy(s_ref.at[:, pl.ds(2 * x + y, 1)], o_ref.at[0], local_sem)
        local.start()
        copies = []
        for k in range(1, 4):
            px = 1 - x if (k >> 1) & 1 else x
            py = 1 - y if k & 1 else y
            copies.append(pltpu.make_async_remote_copy(
                src_ref=s_ref.at[:, pl.ds(2 * px + py, 1)], dst_ref=o_ref.at[k], send_sem=send_sems.at[k - 1],
                recv_sem=recv_sems.at[k - 1], device_id=(px, py, cc), device_id_type=pl.DeviceIdType.MESH))
        for cp in copies:
            cp.start()
        for cp in copies:
            cp.wait()
        local.wait()

    return pl.pallas_call(
        body, name=f"exchange_chips_{tag}", in_specs=[_ANY], out_specs=_ANY, out_shape=_sds((4, nl, 1, 1, r, c), s4.dtype),
        scratch_shapes=[pltpu.SemaphoreType.DMA((3,)), pltpu.SemaphoreType.DMA((3,)), pltpu.SemaphoreType.DMA],
    )(s4)


def add_pair(a, b, tag):
    rows, c = a.shape
    tr = _row_tile(rows)

    def body(a_ref, b_ref, o_ref):
        o_ref[...] = a_ref[...] + b_ref[...]

    blk = pl.BlockSpec((tr, c), lambda i: (i, 0))
    return _call(body, f"add_pair_{tag}", (rows // tr,), [blk, blk], blk, _sds((rows, c)))(a, b)


def exchange_row_blocks(part, tag):
    nl, r8, c = part.shape
    r = r8 // N_DEV

    def body(p_ref, o_ref, send_sems, recv_sems, local_sem):
        me, peers = _me_and_peers()
        rows = lambda idx: p_ref.at[:, pl.ds(idx * r, r), :]
        mine = pltpu.make_async_copy(rows(me), o_ref.at[0], local_sem)
        mine.start()
        copies = [pltpu.make_async_remote_copy(src_ref=rows(idx), dst_ref=o_ref.at[k + 1], send_sem=send_sems.at[k],
                                               recv_sem=recv_sems.at[k], device_id=dev, device_id_type=pl.DeviceIdType.MESH)
                  for k, (dev, idx) in enumerate(peers)]
        for cp in copies:
            cp.start()
        for cp in copies:
            cp.wait()
        mine.wait()

    return pl.pallas_call(
        body, name=f"exchange_{tag}", in_specs=[_ANY], out_specs=_ANY, out_shape=_sds((N_DEV, nl, r, c), part.dtype),
        scratch_shapes=[pltpu.SemaphoreType.DMA((N_DEV - 1,)), pltpu.SemaphoreType.DMA((N_DEV - 1,)), pltpu.SemaphoreType.DMA],
    )(part)


def _row_tile(rows):
    if rows <= 512 and rows % 64:
        return rows
    for t in (64, 40, 32, 16, 8):
        if rows % t == 0:
            return t
    return rows


def sum_slots(slots, tag):
    ns, rows, c = slots.shape
    tr = _row_tile(rows)

    def body(s_ref, o_ref):
        acc = s_ref[0]
        for k in range(1, ns):
            acc = acc + s_ref[k]
        o_ref[...] = acc

    return _call(body, f"sum_slots_{tag}", (rows // tr,), [pl.BlockSpec((ns, tr, c), lambda i: (0, i, 0))],
                 pl.BlockSpec((tr, c), lambda i: (i, 0)), _sds((rows, c)))(slots)


def _adamw(w, g, m, v):
    m = ADAM_B1 * m + (1.0 - ADAM_B1) * g
    v = ADAM_B2 * v + (1.0 - ADAM_B2) * (g * g)
    m_hat = m / (1.0 - ADAM_B1 ** ADAM_STEP)
    v_hat = v / (1.0 - ADAM_B2 ** ADAM_STEP)
    delta = -ADAM_LR * (m_hat / (jnp.sqrt(v_hat) + ADAM_EPS) + ADAM_WD * w)
    return delta, m, v


def adam_update(w, m, v, g, tag, slots=False):
    rows, c = w.shape
    tr = _row_tile(rows)
    ns = g.shape[0] if slots else 0

    def body(w_ref, m_ref, v_ref, g_ref, go_ref, d_ref, mo_ref, vo_ref):
        if slots:
            g = g_ref[0]
            for k in range(1, ns):
                g = g + g_ref[k]
        else:
            g = g_ref[...]
        d, mn, vn = _adamw(w_ref[...], g, m_ref[...], v_ref[...])
        go_ref[...] = g
        d_ref[...] = d
        mo_ref[...] = mn
        vo_ref[...] = vn

    blk = pl.BlockSpec((tr, c), lambda i: (i, 0))
    gspec = pl.BlockSpec((ns, tr, c), lambda i: (0, i, 0)) if slots else blk
    return _call(body, f"adam_{tag}", (rows // tr,), [blk, blk, blk, gspec], [blk] * 4, [_sds((rows, c))] * 4)(w, m, v, g)


_SMALL = (("norm_g", (2, 1024)), ("fox_f_bias", (2, 4)), ("fox_q_norm", (2, 64)), ("fox_k_norm", (2, 64)),
          ("hgrn_lb_logits", (2, 256)), ("hgrn_out_norm", (2, 256)), ("pool_w", (2, 4, 64, 64)), ("pool_scale", (2, 256)),
          ("mem_norm_g", (2, 1024)), ("mem_q_norm", (2, 64)), ("mem_k_norm", (2, 64)))
_SLAB_ROWS = 312


def pack_small(d):
    flat = jnp.concatenate([d[n].reshape(-1) for n, _ in _SMALL])
    return jnp.pad(flat, (0, _SLAB_ROWS * 128 - flat.shape[0])).reshape(_SLAB_ROWS, 128)


def unpack_small(slab):
    flat, out, off = slab.reshape(-1), {}, 0
    for n, shp in _SMALL:
        size = 1
        for e in shp:
            size *= e
        out[n] = flat[off:off + size].reshape(shp)
        off += size
    return out


def small_grads(g0, g1, lb_logits):
    st = lambda f: jnp.stack([f(g0), f(g1)])
    heads = lambda a: a.reshape(NH, HD).sum(0)
    p1 = jax.nn.sigmoid(lb_logits[1] - lb_logits[0])
    inside = (p1 > 0.0) & (p1 < 1.0 - 1e-6)
    dl1 = jnp.where(inside, g1["lb"][0] * p1 * (1.0 - p1), 0.0)
    diag = lambda a: jnp.stack([a.reshape(4, HD, 4, HD)[i, :, i, :] for i in range(4)])
    return dict(norm_g=st(lambda g: g["norm_g"][0]), fox_f_bias=st(lambda g: g["fox_f_bias"][0, :NH]),
                fox_q_norm=st(lambda g: heads(g["fox_q_norm"])), fox_k_norm=st(lambda g: heads(g["fox_k_norm"])),
                hgrn_lb_logits=jnp.stack([-dl1, dl1]), hgrn_out_norm=st(lambda g: g["hgrn_out_norm"][0]),
                pool_w=st(lambda g: diag(g["pool_wbd"])), pool_scale=st(lambda g: g["pool_scale"][0]),
                mem_norm_g=st(lambda g: g["mem_norm_g"][0]), mem_q_norm=st(lambda g: heads(g["mem_q_norm"])),
                mem_k_norm=st(lambda g: heads(g["mem_k_norm"])))


def kernel(x, mem, norm_g, w_in, fox_f_bias, fox_q_norm, fox_k_norm, hgrn_lb_logits, hgrn_out_norm, pool_w, pool_scale, mem_norm_g, mem_w_kv, mem_q_norm, mem_k_norm, w_out, loss_target, m_norm_g, m_w_in, m_fox_f_bias, m_fox_q_norm, m_fox_k_norm, m_hgrn_lb_logits, m_hgrn_out_norm, m_pool_w, m_pool_scale, m_mem_norm_g, m_mem_w_kv, m_mem_q_norm, m_mem_k_norm, m_w_out, v_norm_g, v_w_in, v_fox_f_bias, v_fox_q_norm, v_fox_k_norm, v_hgrn_lb_logits, v_hgrn_out_norm, v_pool_w, v_pool_scale, v_mem_norm_g, v_mem_w_kv, v_mem_q_norm, v_mem_k_norm, v_w_out):
    given = dict(norm_g=(norm_g, m_norm_g, v_norm_g), w_in=(w_in, m_w_in, v_w_in), fox_f_bias=(fox_f_bias, m_fox_f_bias, v_fox_f_bias),
                 fox_q_norm=(fox_q_norm, m_fox_q_norm, v_fox_q_norm), fox_k_norm=(fox_k_norm, m_fox_k_norm, v_fox_k_norm),
                 hgrn_lb_logits=(hgrn_lb_logits, m_hgrn_lb_logits, v_hgrn_lb_logits),
                 hgrn_out_norm=(hgrn_out_norm, m_hgrn_out_norm, v_hgrn_out_norm), pool_w=(pool_w, m_pool_w, v_pool_w),
                 pool_scale=(pool_scale, m_pool_scale, v_pool_scale), mem_norm_g=(mem_norm_g, m_mem_norm_g, v_mem_norm_g),
                 mem_w_kv=(mem_w_kv, m_mem_w_kv, v_mem_w_kv), mem_q_norm=(mem_q_norm, m_mem_q_norm, v_mem_q_norm),
                 mem_k_norm=(mem_k_norm, m_mem_k_norm, v_mem_k_norm), w_out=(w_out, m_w_out, v_w_out))
    order = ("norm_g", "w_in", "fox_f_bias", "fox_q_norm", "fox_k_norm", "hgrn_lb_logits", "hgrn_out_norm", "pool_w",
             "pool_scale", "mem_norm_g", "mem_w_kv", "mem_q_norm", "mem_k_norm", "w_out")

    w_in_full = all_gather_rows(_cast(permute_cols(w_in)), "w_in")
    w_out_full = all_gather_rows(_cast(w_out), "w_out")
    w_kv_full = all_gather_rows(_cast(mem_w_kv), "w_kv")
    p = prepare_params(norm_g, w_in_full, fox_f_bias, fox_q_norm, fox_k_norm, hgrn_lb_logits, hgrn_out_norm, pool_w,
                       pool_scale, mem_norm_g, w_kv_full, mem_q_norm, mem_k_norm, w_out_full)

    loss_part, grad_x, g0, g1 = local_step(x, mem, loss_target, p)
    loss = lax.psum(loss_part, ("x", "y", "c"))

    res = {}

    def sharded(name, g2, unperm=False, two_stage=True):
        w, m, v = given[name]
        nl, r, c = w.shape
        cp = g2.shape[-1]
        if two_stage:
            mine, theirs = exchange_cores(g2, name)
            s4 = add_pair(mine.reshape(nl * 4 * r, cp), theirs.reshape(nl * 4 * r, cp), name)
            slots = exchange_chips(s4.reshape(nl, 4, 1, r, cp), name).reshape(4, nl * r, cp)
        else:
            slots = exchange_row_blocks(g2, name).reshape(N_DEV, nl * r, cp)
        if unperm:
            g = sum_slots(slots, name).reshape(nl, r, cp)
            out = adam_update(w.reshape(nl * r, c), m.reshape(nl * r, c), v.reshape(nl * r, c),
                              unpermute_cols(g).reshape(nl * r, c), name)
        else:
            out = adam_update(w.reshape(nl * r, c), m.reshape(nl * r, c), v.reshape(nl * r, c), slots, name, slots=True)
        res[name] = tuple(o.reshape(nl, r, c) for o in out)

    sharded("w_in", jnp.stack([g0["w_in"], g1["w_in"]]), unperm=True)
    sharded("w_out", jnp.stack([g0["w_out"], g1["w_out"]]))
    sharded("mem_w_kv", jnp.stack([g0["mem_w_kv"], g1["mem_w_kv"]]), two_stage=False)

    gsmall = pack_small(small_grads(g0, g1, hgrn_lb_logits))
    gathered = all_gather_rows(gsmall[None], "small").reshape(N_DEV, _SLAB_ROWS, 128)
    slabs = adam_update(*[pack_small({n: given[n][j] for n, _ in _SMALL}) for j in range(3)], gathered, "small", slots=True)
    small = [unpack_small(sl) for sl in slabs]
    for n, _ in _SMALL:
        res[n] = tuple(small[j][n] for j in range(4))

    return (loss, grad_x, *[res[n][0] for n in order], *[res[n][1] for n in order], *[res[n][2] for n in order],
            *[res[n][3] for n in order])
```

```python
import functools

import jax
import jax.numpy as jnp
from jax import lax
from jax.experimental import pallas as pl
from jax.experimental.pallas import tpu as pltpu

F32 = jnp.float32
BF = jnp.bfloat16
_MMT = BF

D_MODEL = 1024
GW = 256
HD = 64
NH = 4
CH = 64
N_MEM = 256
D_IN = 4100
D_INP = 4224
D_MIX = 1280
EPS = 1e-6
NEG_BIG = -1e30
LB_FLOOR = 1e-30
SCALE = HD ** -0.5
TQ = 256
N_DEV = 8
VMEM_LIMIT_BYTES = 56 * 1024 * 1024

ADAM_LR = 0.001
ADAM_B1 = 0.9
ADAM_B2 = 0.999
ADAM_EPS = 1e-08
ADAM_WD = 0.01
ADAM_STEP = 10

PIECES = (("A", 0, 768), ("B", 768, 768), ("C", 1536, 768), ("D", 2304, 256), ("E", 2560, 256),
          ("G", 2816, 1280), ("F", 4096, 128))
BWD_PIECES = (("A", 0, 768), ("Bq", 768, 256), ("Bk", 1024, 256), ("Bv", 1280, 256), ("C", 1536, 768),
              ("D", 2304, 256), ("E", 2560, 256), ("G", 2816, 1280), ("F", 4096, 128))
_ORIG = dict(fq=(0, 256), fk=(256, 512), fv=(512, 768), fg=(768, 1024), ff=(1024, 1028), sq=(1028, 1284),
             sk=(1284, 1540), sv=(1540, 1796), sg=(1796, 2052), hq=(2052, 2308), hf=(2308, 2564),
             hi=(2564, 2820), hg=(2820, 3076), pv=(3076, 3332), pg=(3332, 3588), mq=(3588, 3844), mg=(3844, 4100))
_PERM_ORDER = ("fq", "fk", "fv", "sq", "sk", "sv", "hq", "hf", "hi", "pv", "mq", "fg", "sg", "hg", "pg", "mg", "ff")
_ORIG_ORDER = ("fq", "fk", "fv", "fg", "ff", "sq", "sk", "sv", "sg", "hq", "hf", "hi", "hg", "pv", "pg", "mq", "mg")


def permute_cols(w):
    parts = [w[..., _ORIG[n][0]:_ORIG[n][1]] for n in _PERM_ORDER]
    parts.append(jnp.zeros(w.shape[:-1] + (D_INP - D_IN,), w.dtype))
    return jnp.concatenate(parts, axis=-1)


def unpermute_cols(g):
    start, off = {}, 0
    for n in _PERM_ORDER:
        start[n] = off
        off += _ORIG[n][1] - _ORIG[n][0]
    return jnp.concatenate([g[..., start[n]:start[n] + _ORIG[n][1] - _ORIG[n][0]] for n in _ORIG_ORDER], axis=-1)


def _cast(a):
    return a.astype(_MMT)


def _dg(a, b, ca, cb):
    return lax.dot_general(a, b, (((ca,), (cb,)), ((), ())), preferred_element_type=F32)


@jax.custom_vjp
def mm(a, b):
    return _dg(_cast(a), _cast(b), 1, 0)


@jax.custom_vjp
def mm_nt(a, b):
    return _dg(_cast(a), _cast(b), 1, 1)


@jax.custom_vjp
def mm_tn(a, b):
    return _dg(_cast(a), _cast(b), 0, 0)


mm.defvjp(lambda a, b: (mm(a, b), (a, b)),
          lambda r, g: (mm_nt(g, r[1]).astype(r[0].dtype), mm_tn(r[0], g).astype(r[1].dtype)))
mm_nt.defvjp(lambda a, b: (mm_nt(a, b), (a, b)),
             lambda r, g: (mm(g, r[1]).astype(r[0].dtype), mm_tn(g, r[0]).astype(r[1].dtype)))
mm_tn.defvjp(lambda a, b: (mm_tn(a, b), (a, b)),
             lambda r, g: (mm_nt(r[1], g).astype(r[0].dtype), mm(r[0], g).astype(r[1].dtype)))


def _split(a):
    hi = a.astype(_MMT)
    lo = (a - hi.astype(F32)).astype(_MMT)
    return hi, lo


@jax.custom_vjp
def xr(a, c, ct):
    hi, lo = _split(a)
    cc = _cast(c)
    return _dg(hi, cc, 1, 0) + _dg(lo, cc, 1, 0)


@jax.custom_vjp
def xl(c, ct, a):
    hi, lo = _split(a)
    cc = _cast(c)
    return _dg(cc, hi, 1, 0) + _dg(cc, lo, 1, 0)


xr.defvjp(lambda a, c, ct: (xr(a, c, ct), (c, ct)),
          lambda r, g: (xr(g, r[1], r[0]), jnp.zeros_like(r[0]), jnp.zeros_like(r[1])))
xl.defvjp(lambda c, ct, a: (xl(c, ct, a), (c, ct)),
          lambda r, g: (jnp.zeros_like(r[0]), jnp.zeros_like(r[1]), xl(r[1], r[0], g)))


def _iota(shape, dim):
    return lax.broadcasted_iota(jnp.int32, shape, dim)


def _hmask(h, n=GW):
    lane = _iota((1, n), 1)
    return ((lane >= h * HD) & (lane < (h + 1) * HD)).astype(F32)


def _bdmask(n=GW):
    return ((_iota((n, n), 0) >> 6) == (_iota((n, n), 1) >> 6)).astype(F32)


def _tri(n, kind="le"):
    r, c = _iota((n, n), 0), _iota((n, n), 1)
    return {"le": c <= r, "ge": c >= r, "gt": c > r, "lt": c < r}[kind].astype(F32)


def _onehot_lane(h, n=128):
    return (_iota((1, n), 1) == h).astype(F32)


def _logsig(x):
    return jnp.minimum(x, 0.0) - jnp.log1p(jnp.exp(-jnp.abs(x)))


def _sigmoid(x):
    return 0.5 * (jnp.tanh(0.5 * x) + 1.0)


def _silu(x):
    return x * _sigmoid(x)


def _rms(x, g):
    return x * lax.rsqrt(jnp.mean(x * x, axis=-1, keepdims=True) + EPS) * g


def _headrms(x, w, bd64):
    ms = xr(x * x, bd64, bd64)
    return x * lax.rsqrt(ms + EPS) * w


def _call(body, name, grid, in_specs, out_specs, out_shape, scratch=()):
    return pl.pallas_call(
        body, name=name, grid=grid, in_specs=in_specs, out_specs=out_specs, out_shape=out_shape,
        scratch_shapes=list(scratch),
        compiler_params=pltpu.CompilerParams(dimension_semantics=("arbitrary",) * len(grid),
                                             vmem_limit_bytes=VMEM_LIMIT_BYTES))


def _sds(shape, dtype=F32):
    return jax.ShapeDtypeStruct(shape, dtype)


def _acc(ref, val, first):
    @pl.when(first)
    def _():
        ref[...] = val

    @pl.when(jnp.logical_not(first))
    def _():
        ref[...] += val


def inproj_fwd(x2, g, w, layer, tag):
    t = x2.shape[0]

    def body(x_ref, g_ref, w_ref, ht_ref, *outs):
        h = _rms(x_ref[...], g_ref[...])
        hb = _cast(h)
        ht_ref[...] = _cast(h.T)
        for (_, c0, wd), o in zip(PIECES, outs):
            o[...] = _dg(hb, _cast(w_ref[:, c0:c0 + wd]), 1, 0)

    return _call(
        body, f"inproj_fwd_{tag}", (t // TQ,),
        [pl.BlockSpec((TQ, D_MODEL), lambda i: (i, 0)),
         pl.BlockSpec((None, 1, D_MODEL), lambda i: (layer, 0, 0)),
         pl.BlockSpec((None, D_MODEL, D_INP), lambda i: (layer, 0, 0))],
        [pl.BlockSpec((D_MODEL, TQ), lambda i: (0, i))] + [pl.BlockSpec((TQ, wd), lambda i: (i, 0)) for _, _, wd in PIECES],
        [_sds((D_MODEL, t), _MMT)] + [_sds((t, wd)) for _, _, wd in PIECES],
    )(x2, g, w)


def inproj_bwd_dx(x2, g, w, dy, dpieces, layer, tag):
    t = x2.shape[0]

    def body(x_ref, g_ref, w_ref, dy_ref, *rest):
        dps, (dx_ref, dg_ref) = rest[:len(BWD_PIECES)], rest[len(BWD_PIECES):]
        dh = None
        for (_, c0, wd), dp in zip(BWD_PIECES, dps):
            part = _dg(_cast(dp[...]), _cast(w_ref[:, c0:c0 + wd]), 1, 1)
            dh = part if dh is None else dh + part
        _, vjp = jax.vjp(_rms, x_ref[...], g_ref[...])
        dx, dg = vjp(dh)
        dx_ref[...] = dy_ref[...] + dx
        _acc(dg_ref, dg, pl.program_id(0) == 0)

    return _call(
        body, f"inproj_bwd_dx_{tag}", (t // TQ,),
        [pl.BlockSpec((TQ, D_MODEL), lambda i: (i, 0)),
         pl.BlockSpec((None, 1, D_MODEL), lambda i: (layer, 0, 0)),
         pl.BlockSpec((None, D_MODEL, D_INP), lambda i: (layer, 0, 0)),
         pl.BlockSpec((TQ, D_MODEL), lambda i: (i, 0))] + [pl.BlockSpec((TQ, wd), lambda i: (i, 0)) for _, _, wd in BWD_PIECES],
        [pl.BlockSpec((TQ, D_MODEL), lambda i: (i, 0)), pl.BlockSpec((1, D_MODEL), lambda i: (0, 0))],
        [_sds((t, D_MODEL)), _sds((1, D_MODEL))],
    )(x2, g, w, dy, *dpieces)


def matmul_acc(at, b, tag):
    m, t = at.shape
    n = b.shape[1]
    tn = {1280: 640, 768: 768}.get(n, n)
    tk = 2048 if t % 2048 == 0 else (512 if t % 512 == 0 else TQ)

    def body(a_ref, b_ref, o_ref):
        _acc(o_ref, _dg(_cast(a_ref[...]), _cast(b_ref[...]), 1, 0), pl.program_id(1) == 0)

    return _call(
        body, f"matmul_acc_{tag}", (n // tn, t // tk),
        [pl.BlockSpec((m, tk), lambda j, i: (0, i)), pl.BlockSpec((tk, tn), lambda j, i: (i, j))],
        pl.BlockSpec((m, tn), lambda j, i: (0, j)),
        _sds((m, n)),
    )(at, b)


def _fox_prep_fn(q, k, ff, qw, kw, bias, carry, bd64, tri, trit, last):
    qn = _headrms(q, qw, bd64)
    kn = _headrms(k, kw, bd64)
    lf = _logsig(ff + bias)
    c = xl(tri, trit, lf) + carry
    return qn, kn, c, jnp.sum(c * last, axis=0, keepdims=True)


def _prep_consts():
    return _bdmask() * (1.0 / HD), _tri(TQ), _tri(TQ, "ge"), (_iota((TQ, 1), 0) == TQ - 1).astype(F32)


def fox_prep_fwd(pa, pf, qw, kw, bias, bl, s, layer, tag):
    nq = s // TQ

    def body(q_ref, k_ref, v_ref, f_ref, qw_ref, kw_ref, b_ref, qn_ref, kn_ref, vb_ref, cq_ref, ck_ref, carry):
        @pl.when(pl.program_id(1) == 0)
        def _():
            carry[...] = jnp.zeros_like(carry)

        qn, kn, c, cl = _fox_prep_fn(q_ref[...], k_ref[...], f_ref[...], qw_ref[...], kw_ref[...], b_ref[...],
                                     carry[...], *_prep_consts())
        carry[...] = cl
        qn_ref[...] = _cast(qn)
        kn_ref[...] = _cast(kn)
        vb_ref[...] = _cast(v_ref[...])
        cq_ref[...] = c
        ck_ref[...] = c.T[0:8, :]

    tok = lambda j: pl.BlockSpec((TQ, GW), lambda b, i: (b * nq + i, j))
    par = lambda n: pl.BlockSpec((None, 1, n), lambda b, i: (layer, 0, 0))
    return _call(
        body, f"fox_prep_fwd_{tag}", (bl, nq),
        [tok(0), tok(1), tok(2), pl.BlockSpec((TQ, 128), lambda b, i: (b * nq + i, 0)), par(GW), par(GW), par(128)],
        [tok(0), tok(0), tok(0), pl.BlockSpec((TQ, 128), lambda b, i: (b * nq + i, 0)),
         pl.BlockSpec((None, 8, TQ), lambda b, i: (b, 0, i))],
        [_sds((bl * s, GW), _MMT)] * 3 + [_sds((bl * s, 128)), _sds((bl, 8, s))],
        [pltpu.VMEM((1, 128), F32)],
    )(pa, pa, pa, pf, qw, kw, bias)


def fox_prep_bwd(pa, pf, qw, kw, bias, cq, dqn, dkn, dv, dck, bl, s, layer, tag):
    nq = s // TQ

    def body(q_ref, k_ref, f_ref, qw_ref, kw_ref, b_ref, cq_ref, cprev_ref, dqn_ref, dkn_ref, dv_ref, dck_ref,
             da_ref, df_ref, dqw_ref, dkw_ref, db_ref, dcarry):
        i = pl.program_id(1)
        first = jnp.logical_and(pl.program_id(0) == 0, i == 0)

        @pl.when(i == 0)
        def _():
            dcarry[...] = jnp.zeros_like(dcarry)

        last = (_iota((TQ, 1), 0) == TQ - 1).astype(F32)
        carry_in = jnp.where(i == nq - 1, 0.0, jnp.sum(cprev_ref[...] * last, axis=0, keepdims=True))
        consts = _prep_consts()
        _, vjp = jax.vjp(lambda *a: _fox_prep_fn(*a, *consts), q_ref[...], k_ref[...], f_ref[...], qw_ref[...],
                         kw_ref[...], b_ref[...], carry_in)
        dc = dck_ref[...].T
        dq, dk, dff, dqw, dkw, dbias, dcin = vjp((dqn_ref[...], dkn_ref[...], dc, dcarry[...]))
        dcarry[...] = dcin
        da_ref[:, 0:GW] = dq
        da_ref[:, GW:2 * GW] = dk
        da_ref[:, 2 * GW:3 * GW] = dv_ref[...]
        df_ref[...] = dff
        _acc(dqw_ref, dqw, first)
        _acc(dkw_ref, dkw, first)
        _acc(db_ref, dbias, first)

    rv = lambda b, i: b * nq + (nq - 1 - i)
    tok = lambda j: pl.BlockSpec((TQ, GW), lambda b, i: (rv(b, i), j))
    tok0 = pl.BlockSpec((TQ, GW), lambda b, i: (rv(b, i), 0))
    t128 = pl.BlockSpec((TQ, 128), lambda b, i: (rv(b, i), 0))
    prev = pl.BlockSpec((TQ, 128), lambda b, i: (jnp.maximum(rv(b, i) - 1, 0), 0))
    par = lambda n: pl.BlockSpec((None, 1, n), lambda b, i: (layer, 0, 0))
    acc = lambda n: pl.BlockSpec((1, n), lambda b, i: (0, 0))
    return _call(
        body, f"fox_prep_bwd_{tag}", (bl, nq),
        [tok(0), tok(1), t128, par(GW), par(GW), par(128), t128, prev, tok0, tok0, tok0,
         pl.BlockSpec((None, 128, TQ), lambda b, i: (b, 0, nq - 1 - i))],
        [pl.BlockSpec((TQ, 3 * GW), lambda b, i: (rv(b, i), 0)), t128, acc(GW), acc(GW), acc(128)],
        [_sds((bl * s, 3 * GW)), _sds((bl * s, 128)), _sds((1, GW)), _sds((1, GW)), _sds((1, 128))],
        [pltpu.VMEM((1, 128), F32)],
    )(pa, pa, pf, qw, kw, bias, cq, cq, dqn, dkn, dv, dck)


def _lane_pick(x, h):
    return jnp.sum(x * _onehot_lane(h), axis=-1, keepdims=True)


TA = 128
SROWS = NH * TA


def _stack_heads(x, scale=1.0):
    return _cast(jnp.concatenate([x * (_hmask(h) * scale) for h in range(NH)], axis=0))


def _stack_cols(x):
    return jnp.concatenate([_lane_pick(x, h) for h in range(NH)], axis=0)


def _spread_heads(col):
    return sum(col[h * TA:(h + 1) * TA] * _hmask(h) for h in range(NH))


def _lanes_cat(w):
    return jnp.concatenate([w[h * TA:(h + 1) * TA] for h in range(NH)], axis=1)


def _mask_stack(x):
    return _cast(jnp.concatenate([x * _hmask(h).astype(x.dtype) for h in range(NH)], axis=0))


def _stack_rows(i):
    return i * TA + (_iota((SROWS, 1), 0) & (TA - 1))


def _n_key_tiles(i):
    return lax.shift_right_logical(i * TA, TQ.bit_length() - 1) + 1


def fox_attn_fwd(qn, kn, vb, cq, ck, bl, s, tag):
    nq = s // TA

    def body(q_ref, k_ref, v_ref, cq_ref, ck_ref, o_ref, lse_ref, acc):
        i = pl.program_id(1)
        qs = _stack_heads(q_ref[...].astype(F32), SCALE)
        cqs = _stack_cols(cq_ref[...])
        row = _stack_rows(i)
        acc[...] = jnp.zeros_like(acc)

        def step(j, ml):
            m, l = ml
            ks = pl.ds(pl.multiple_of(j * TQ, TQ), TQ)
            ckb = jnp.concatenate([jnp.broadcast_to(ck_ref[h:h + 1, ks], (TA, TQ)) for h in range(NH)], axis=0)
            sc = _dg(qs, k_ref[ks, :], 1, 1) + cqs - ckb
            col = j * TQ + _iota((1, TQ), 1)
            sc = jnp.where(col <= row, sc, NEG_BIG)
            m_new = jnp.maximum(m, jnp.max(sc, axis=-1, keepdims=True))
            alpha = jnp.exp(m - m_new)
            p = jnp.exp(sc - m_new)
            acc[...] = _spread_heads(alpha) * acc[...] + _dg(_lanes_cat(_cast(p)), _mask_stack(v_ref[ks, :]), 1, 0)
            return m_new, alpha * l + jnp.sum(p, axis=-1, keepdims=True)

        m, l = lax.fori_loop(0, _n_key_tiles(i), step, (jnp.full((SROWS, 1), NEG_BIG, F32), jnp.zeros((SROWS, 1), F32)))
        o_ref[...] = acc[...] / _spread_heads(l)
        lse_h = m + jnp.log(l)
        lse_ref[...] = sum(lse_h[h * TA:(h + 1) * TA] * _onehot_lane(h) for h in range(NH))

    tok = pl.BlockSpec((TA, GW), lambda b, i: (b * nq + i, 0))
    seq = pl.BlockSpec((s, GW), lambda b, i: (b, 0))
    t128 = pl.BlockSpec((TA, 128), lambda b, i: (b * nq + i, 0))
    return _call(
        body, f"fox_attn_fwd_{tag}", (bl, nq),
        [tok, seq, seq, t128, pl.BlockSpec((None, 8, s), lambda b, i: (b, 0, 0))],
        [tok, t128], [_sds((bl * s, GW)), _sds((bl * s, 128))],
        [pltpu.VMEM((TA, GW), F32)],
    )(qn, kn, vb, cq, ck)


def fox_attn_bwd(qn, kn, vb, cq, ck, lse, do, bl, s, tag):
    nq = s // TA

    def body(q_ref, k_ref, v_ref, cq_ref, ck_ref, lse_ref, do_ref, dq_ref, dk_ref, dv_ref, dck_ref, dqa, p_s, dp_s):
        i = pl.program_id(1)

        @pl.when(i == 0)
        def _():
            dk_ref[...] = jnp.zeros_like(dk_ref)
            dv_ref[...] = jnp.zeros_like(dv_ref)
            dck_ref[...] = jnp.zeros_like(dck_ref)

        qs = _stack_heads(q_ref[...].astype(F32), SCALE)
        dos = _stack_heads(do_ref[...])
        cqs, lses = _stack_cols(cq_ref[...]), _stack_cols(lse_ref[...])
        row = _stack_rows(i)
        dqa[...] = jnp.zeros_like(dqa)
        nk = _n_key_tiles(i)

        def probs(j, delta):
            ks = pl.ds(pl.multiple_of(j * TQ, TQ), TQ)
            ckb = jnp.concatenate([jnp.broadcast_to(ck_ref[h:h + 1, ks], (TA, TQ)) for h in range(NH)], axis=0)
            sc = _dg(qs, k_ref[ks, :], 1, 1) + cqs - ckb
            col = j * TQ + _iota((1, TQ), 1)
            p = jnp.where(col <= row, jnp.exp(sc - lses), 0.0)
            dp = _dg(dos, v_ref[ks, :], 1, 1)
            p_s[:, ks] = p
            dp_s[:, ks] = dp
            return delta + jnp.sum(p * dp, axis=-1, keepdims=True)

        delta = lax.fori_loop(0, nk, probs, jnp.zeros((SROWS, 1), F32))

        def step(j, carry):
            ks = pl.ds(pl.multiple_of(j * TQ, TQ), TQ)
            p = p_s[:, ks]
            ds = p * (dp_s[:, ks] - delta)
            dsb = _cast(ds)
            dqa[...] += _dg(_lanes_cat(dsb), _mask_stack(k_ref[ks, :]), 1, 0) * SCALE
            dk_ref[ks, :] += _dg(dsb, qs, 0, 0)
            dv_ref[ks, :] += _dg(_cast(p), dos, 0, 0)
            for h in range(NH):
                dck_ref[h:h + 1, ks] -= jnp.sum(ds[h * TA:(h + 1) * TA], axis=0, keepdims=True)
            return carry

        lax.fori_loop(0, nk, step, 0)
        dq_ref[...] = dqa[...]

    tok = pl.BlockSpec((TA, GW), lambda b, i: (b * nq + i, 0))
    seq = pl.BlockSpec((s, GW), lambda b, i: (b, 0))
    t128 = pl.BlockSpec((TA, 128), lambda b, i: (b * nq + i, 0))
    return _call(
        body, f"fox_attn_bwd_{tag}", (bl, nq),
        [tok, seq, seq, t128, pl.BlockSpec((None, 8, s), lambda b, i: (b, 0, 0)), t128, tok],
        [tok, seq, seq, pl.BlockSpec((None, 128, s), lambda b, i: (b, 0, 0))],
        [_sds((bl * s, GW)), _sds((bl * s, GW)), _sds((bl * s, GW)), _sds((bl, 128, s))],
        [pltpu.VMEM((TA, GW), F32), pltpu.VMEM((SROWS, s), F32), pltpu.VMEM((SROWS, s), F32)],
    )(qn, kn, vb, cq, ck, lse, do)


def _sb_block(qh, kb, row, col, upper, r_carry):
    z = _dg(qh, kb, 1, 1)
    valid = col < row
    ls = _logsig(z)
    lom = jnp.where(valid, ls - z, 0.0)
    between = xr(lom, upper, upper) + r_carry
    w = jnp.where(valid, jnp.exp(ls + between), 0.0)
    return z, ls, lom, w, valid


def sb_attn_fwd(pb, bl, s, tag):
    nq = s // TA

    def body(q_ref, k_ref, v_ref, o_ref, acc):
        i = pl.program_id(1)
        qs = _stack_heads(q_ref[...], SCALE)
        row = _stack_rows(i)
        upper = _tri(TQ, "lt")
        acc[...] = jnp.zeros_like(acc)
        last = _n_key_tiles(i) - 1

        def step(jj, r):
            j = last - jj
            ks = pl.ds(pl.multiple_of(j * TQ, TQ), TQ)
            col = j * TQ + _iota((1, TQ), 1)
            _, _, lom, w, _ = _sb_block(qs, _cast(k_ref[ks, :]), row, col, upper, r)
            acc[...] += _dg(_lanes_cat(_cast(w)), _mask_stack(v_ref[ks, :]), 1, 0)
            return r + jnp.sum(lom, axis=-1, keepdims=True)

        lax.fori_loop(0, last + 1, step, jnp.zeros((SROWS, 1), F32))
        o_ref[...] = acc[...]

    tok = lambda j: pl.BlockSpec((TA, GW), lambda b, i: (b * nq + i, j))
    seq = lambda j: pl.BlockSpec((s, GW), lambda b, i: (b, j))
    return _call(
        body, f"sb_attn_fwd_{tag}", (bl, nq), [tok(0), seq(1), seq(2)],
        pl.BlockSpec((TA, GW), lambda b, i: (b * nq + i, 0)), _sds((bl * s, GW)),
        [pltpu.VMEM((TA, GW), F32)],
    )(pb, pb, pb)


def sb_attn_bwd(pb, do, bl, s, tag):
    nq = s // TA

    def body(q_ref, k_ref, v_ref, do_ref, dq_ref, dk_ref, dv_ref, dqa, ls_s, lom_s, w_s, g_s):
        i = pl.program_id(1)

        @pl.when(i == 0)
        def _():
            dk_ref[...] = jnp.zeros_like(dk_ref)
            dv_ref[...] = jnp.zeros_like(dv_ref)

        qs = _stack_heads(q_ref[...], SCALE)
        dos = _stack_heads(do_ref[...])
        row = _stack_rows(i)
        upper = _tri(TQ, "lt")
        before = _tri(TQ, "gt")
        dqa[...] = jnp.zeros_like(dqa)
        last = _n_key_tiles(i) - 1

        def weights(jj, r):
            j = last - jj
            ks = pl.ds(pl.multiple_of(j * TQ, TQ), TQ)
            col = j * TQ + _iota((1, TQ), 1)
            _, ls, lom, w, _ = _sb_block(qs, _cast(k_ref[ks, :]), row, col, upper, r)
            ls_s[:, ks] = ls
            lom_s[:, ks] = lom
            w_s[:, ks] = _cast(w)
            g_s[:, ks] = _dg(dos, _cast(v_ref[ks, :]), 1, 1) * w
            return r + jnp.sum(lom, axis=-1, keepdims=True)

        lax.fori_loop(0, last + 1, weights, jnp.zeros((SROWS, 1), F32))

        def step(j, cpre):
            ks = pl.ds(pl.multiple_of(j * TQ, TQ), TQ)
            col = j * TQ + _iota((1, TQ), 1)
            g = g_s[:, ks]
            pre = cpre + xr(g, before, before)
            dz = jnp.where(col < row, g * jnp.exp(lom_s[:, ks]) - jnp.exp(ls_s[:, ks]) * pre, 0.0)
            dzb = _cast(dz)
            dqa[...] += _dg(_lanes_cat(dzb), _mask_stack(k_ref[ks, :]), 1, 0) * SCALE
            dk_ref[ks, :] += _dg(dzb, qs, 0, 0)
            dv_ref[ks, :] += _dg(w_s[:, ks], dos, 0, 0)
            return cpre + jnp.sum(g, axis=-1, keepdims=True)

        lax.fori_loop(0, last + 1, step, jnp.zeros((SROWS, 1), F32))
        dq_ref[...] = dqa[...]

    tok = lambda j: pl.BlockSpec((TA, GW), lambda b, i: (b * nq + i, j))
    seq = lambda j: pl.BlockSpec((s, GW), lambda b, i: (b, j))
    return _call(
        body, f"sb_attn_bwd_{tag}", (bl, nq), [tok(0), seq(1), seq(2), tok(0)],
        [tok(0), seq(0), seq(0)], [_sds((bl * s, GW))] * 3,
        [pltpu.VMEM((TA, GW), F32), pltpu.VMEM((SROWS, s), F32), pltpu.VMEM((SROWS, s), F32),
         pltpu.VMEM((SROWS, s), _MMT), pltpu.VMEM((SROWS, s), F32)],
    )(pb, pb, pb, do)


def _hgrn_consts():
    r, c = _iota((CH, CH), 0), _iota((CH, CH), 1)
    rr = _iota((CH, 1), 0)
    tri = (c <= r).astype(F32)
    lv = []
    for m in (8, 4, 2, 1):
        up = ((rr & (2 * m - 1)) >= m).astype(F32)
        lo = 1.0 - up
        selq = (((r & (2 * m - 1)) >= m) & (c == (r & ~(m - 1)) - 1)).astype(F32)
        selk = (((r & (2 * m - 1)) < m) & (c == (r & ~(m - 1)) + m - 1)).astype(F32)
        pm = (((r & ~(2 * m - 1)) == (c & ~(2 * m - 1))) & ((r & (2 * m - 1)) >= m) & ((c & (2 * m - 1)) < m)).astype(F32)
        lv.append((up, lo, selq, selq.T, selk, selk.T, jnp.concatenate([pm] * NH, axis=0)))
    hm4 = lambda n: (((_iota((NH, 1, n), 2) & (GW - 1)) >> 6) == _iota((NH, 1, n), 0)).astype(F32)
    return dict(tri=tri, trit=tri.T, rr=rr, lv=lv, bd=_bdmask(), bd64=_bdmask() * (1.0 / HD),
                hm4={GW: hm4(GW), 3 * GW: hm4(3 * GW)})


def _hgrn_chunk_fn(hq, hf, hi, lb, wn, st, cs):
    q = _silu(hq)
    log_lb = jnp.log(jnp.maximum(lb, LB_FLOOR))
    a, bb = log_lb, jnp.log1p(-lb) + _logsig(hf)
    g = jnp.maximum(a, bb) + jnp.log1p(jnp.exp(-jnp.abs(a - bb)))
    k = (1.0 - lb) * _sigmoid(-hf)
    v = hi
    rr = cs["rr"]
    b = xl(cs["tri"], cs["trit"], g)
    row_of = lambda n: jnp.sum(b * (rr == n).astype(F32), axis=0, keepdims=True)
    o = mm_nt(q * jnp.exp(b), st)
    qs, ks = [], []
    for ib in (1, 2, 3):
        ref = row_of(16 * ib - 1)
        inq = ((rr >= 16 * ib) & (rr < 16 * ib + 16)).astype(F32)
        ink = (rr < 16 * ib).astype(F32)
        qs.append(q * jnp.exp((b - ref) * inq) * inq)
        ks.append(k * jnp.exp((ref - b) * ink) * ink)
    qcat, kcat = jnp.concatenate(qs, axis=1), jnp.concatenate(ks, axis=1)
    lvl = []
    for up, lo, selq, selqt, selk, selkt, pm in cs["lv"]:
        qe = q * jnp.exp((b - xl(selq, selqt, b)) * up) * up
        ke = k * jnp.exp((xl(selk, selkt, b) - b) * lo) * lo
        lvl.append((qe, ke, pm))
    stack = lambda x: (x[None] * cs["hm4"][x.shape[1]]).reshape(NH * CH, x.shape[1])
    a_all = mm_nt(stack(qcat), kcat)
    for qe, ke, pm4 in lvl:
        a_all = a_all + mm_nt(stack(qe), ke) * pm4
    o = o + jnp.sum(mm(a_all, v).reshape(NH, CH, GW) * cs["hm4"][GW], axis=0)
    o = o + xr(q * k, cs["bd"], cs["bd"]) * v
    b_last = row_of(CH - 1)
    st_new = st * jnp.exp(b_last) + mm_tn(v, k * jnp.exp(b_last - b)) * cs["bd"]
    return _headrms(o, wn, cs["bd64"]), st_new


def hgrn_fwd(pc, lb, wn, bl, s, layer, tag):
    nc = s // CH

    def body(q_ref, f_ref, i_ref, lb_ref, wn_ref, o_ref, st_ref, st):
        @pl.when(pl.program_id(0) == 0)
        def _():
            st[...] = jnp.zeros_like(st)

        cs = _hgrn_consts()
        for b in range(bl):
            st_ref[b] = st[b]
            o, st_new = _hgrn_chunk_fn(q_ref[b], f_ref[b], i_ref[b], lb_ref[...], wn_ref[...], st[b], cs)
            o_ref[b] = o
            st[b] = st_new

    tok = lambda j: pl.BlockSpec((bl, CH, GW), lambda c: (0, c, j))
    par = pl.BlockSpec((None, 1, GW), lambda c: (layer, 0, 0))
    pc3 = pc.reshape(bl, s, 3 * GW)
    o, states = _call(
        body, f"hgrn_fwd_{tag}", (nc,), [tok(0), tok(1), tok(2), par, par],
        [tok(0), pl.BlockSpec((bl, None, GW, GW), lambda c: (0, c, 0, 0))],
        [_sds((bl, s, GW)), _sds((bl, nc, GW, GW))],
        [pltpu.VMEM((bl, GW, GW), F32)],
    )(pc3, pc3, pc3, lb, wn)
    return o.reshape(bl * s, GW), states


def hgrn_bwd(pc, lb, wn, states, do, bl, s, layer, tag):
    nc = s // CH

    def body(q_ref, f_ref, i_ref, lb_ref, wn_ref, st_ref, do_ref, dc_ref, dlb_ref, dwn_ref, dst):
        c = pl.program_id(0)

        @pl.when(c == 0)
        def _():
            dst[...] = jnp.zeros_like(dst)

        cs = _hgrn_consts()
        dlb_sum = dwn_sum = None
        for b in range(bl):
            _, vjp = jax.vjp(lambda *a: _hgrn_chunk_fn(*a, cs), q_ref[b], f_ref[b], i_ref[b], lb_ref[...],
                             wn_ref[...], st_ref[b])
            dq, df, di, dlb, dwn, dst_in = vjp((do_ref[b], dst[b]))
            dst[b] = dst_in
            dc_ref[b, :, 0:GW] = dq
            dc_ref[b, :, GW:2 * GW] = df
            dc_ref[b, :, 2 * GW:3 * GW] = di
            dlb_sum = dlb if dlb_sum is None else dlb_sum + dlb
            dwn_sum = dwn if dwn_sum is None else dwn_sum + dwn
        _acc(dlb_ref, dlb_sum, c == 0)
        _acc(dwn_ref, dwn_sum, c == 0)

    tok = lambda j: pl.BlockSpec((bl, CH, GW), lambda c: (0, nc - 1 - c, j))
    par = pl.BlockSpec((None, 1, GW), lambda c: (layer, 0, 0))
    acc = pl.BlockSpec((1, GW), lambda c: (0, 0))
    pc3 = pc.reshape(bl, s, 3 * GW)
    dc, dlb, dwn = _call(
        body, f"hgrn_bwd_{tag}", (nc,),
        [tok(0), tok(1), tok(2), par, par, pl.BlockSpec((bl, None, GW, GW), lambda c: (0, nc - 1 - c, 0, 0)), tok(0)],
        [pl.BlockSpec((bl, CH, 3 * GW), lambda c: (0, nc - 1 - c, 0)), acc, acc],
        [_sds((bl, s, 3 * GW)), _sds((1, GW)), _sds((1, GW))],
        [pltpu.VMEM((bl, GW, GW), F32)],
    )(pc3, pc3, pc3, lb, wn, states, do.reshape(bl, s, GW))
    return dc.reshape(bl * s, 3 * GW), dlb, dwn


def _shift_rows(x, k, up):
    n = x.shape[0]
    rr = _iota((n, 1), 0)
    if up:
        return jnp.where(rr < n - k, pltpu.roll(x, n - k, 0), 0.0)
    return jnp.where(rr >= k, pltpu.roll(x, k, 0), 0.0)


def _window_sums(x, up):
    s2 = x + _shift_rows(x, 1, up)
    s4 = s2 + _shift_rows(s2, 2, up)
    s8 = s4 + _shift_rows(s4, 4, up)
    s16 = s8 + _shift_rows(s8, 8, up)
    return s2, s4, s8, s16


def _pool_div(n):
    pos = (_iota((n, 1), 0) + 1).astype(F32)
    return [jnp.minimum(pos, float(w)) for w in (2, 4, 8, 16)]


def _pool_mix(sums, scaled):
    out = None
    for gi, sw in enumerate(sums):
        part = (sw if scaled is None else sw / scaled[gi]) * _hmask(gi)
        out = part if out is None else out + part
    return out


def pool_fwd(pd, wbd, scale, bl, s, layer, tag):
    def body(u_ref, w_ref, sc_ref, o_ref):
        u = u_ref[...]
        pm = _pool_mix(_window_sums(u, False), _pool_div(s)) - u
        o_ref[...] = _dg(_cast(pm), _cast(w_ref[...]), 1, 0) * sc_ref[...]

    seq = pl.BlockSpec((s, GW), lambda b: (b, 0))
    return _call(
        body, f"pool_fwd_{tag}", (bl,),
        [seq, pl.BlockSpec((None, GW, GW), lambda b: (layer, 0, 0)), pl.BlockSpec((None, 1, GW), lambda b: (layer, 0, 0))],
        seq, _sds((bl * s, GW)),
    )(pd, wbd, scale)


def pool_bwd(pd, wbd, scale, do, bl, s, layer, tag):
    def body(u_ref, w_ref, sc_ref, do_ref, du_ref, dw_ref, dsc_ref):
        first = pl.program_id(0) == 0
        u, do = u_ref[...], do_ref[...]
        div = _pool_div(s)
        pm = _pool_mix(_window_sums(u, False), div) - u
        ypre = _dg(_cast(pm), _cast(w_ref[...]), 1, 0)
        dys = do * sc_ref[...]
        _acc(dsc_ref, jnp.sum(do * ypre, axis=0, keepdims=True), first)
        _acc(dw_ref, _dg(_cast(pm), _cast(dys), 0, 0), first)
        dpm = _dg(_cast(dys), _cast(w_ref[...]), 1, 1)
        dsc = [dpm / d for d in div]
        adj = None
        for gi in range(4):
            part = _window_sums(dsc[gi] * _hmask(gi), True)[gi]
            adj = part if adj is None else adj + part
        du_ref[...] = adj - dpm

    seq = pl.BlockSpec((s, GW), lambda b: (b, 0))
    return _call(
        body, f"pool_bwd_{tag}", (bl,),
        [seq, pl.BlockSpec((None, GW, GW), lambda b: (layer, 0, 0)), pl.BlockSpec((None, 1, GW), lambda b: (layer, 0, 0)), seq],
        [seq, pl.BlockSpec((GW, GW), lambda b: (0, 0)), pl.BlockSpec((1, GW), lambda b: (0, 0))],
        [_sds((bl * s, GW)), _sds((GW, GW)), _sds((1, GW))],
    )(pd, wbd, scale, do)


def _mem_prep_fn(mem, g, wk, wv, kw, bd64):
    mn = _rms(mem, g)
    return _headrms(mm(mn, wk), kw, bd64), mm(mn, wv)


def mem_prep_fwd(mem2, g, wkv, kw, bl, layer, tag):
    def body(m_ref, g_ref, wk_ref, wv_ref, kw_ref, k_ref, v_ref):
        k, v = _mem_prep_fn(m_ref[...], g_ref[...], wk_ref[...], wv_ref[...], kw_ref[...], _bdmask() * (1.0 / HD))
        k_ref[...] = k
        v_ref[...] = v

    blk = pl.BlockSpec((N_MEM, GW), lambda b: (b, 0))
    return _call(
        body, f"mem_prep_fwd_{tag}", (bl,),
        [pl.BlockSpec((N_MEM, D_MODEL), lambda b: (b, 0)), pl.BlockSpec((None, 1, D_MODEL), lambda b: (layer, 0, 0)),
         pl.BlockSpec((None, D_MODEL, GW), lambda b: (layer, 0, 0)), pl.BlockSpec((None, D_MODEL, GW), lambda b: (layer, 0, 1)),
         pl.BlockSpec((None, 1, GW), lambda b: (layer, 0, 0))],
        [blk, blk], [_sds((bl * N_MEM, GW))] * 2,
    )(mem2, g, wkv, wkv, kw)


def mem_prep_bwd(mem2, g, wkv, kw, dk, dv, bl, layer, tag):
    def body(m_ref, g_ref, wk_ref, wv_ref, kw_ref, dk_ref, dv_ref, dwk_ref, dwv_ref, dg_ref, dkw_ref):
        first = pl.program_id(0) == 0
        bd64 = _bdmask() * (1.0 / HD)
        _, vjp = jax.vjp(lambda g_, wk, wv, kw_: _mem_prep_fn(m_ref[...], g_, wk, wv, kw_, bd64),
                         g_ref[...], wk_ref[...].astype(F32), wv_ref[...].astype(F32), kw_ref[...])
        dg, dwk, dwv, dkw = vjp((dk_ref[...], dv_ref[...]))
        _acc(dwk_ref, dwk, first)
        _acc(dwv_ref, dwv, first)
        _acc(dg_ref, dg, first)
        _acc(dkw_ref, dkw, first)

    blk = pl.BlockSpec((N_MEM, GW), lambda b: (b, 0))
    return _call(
        body, f"mem_prep_bwd_{tag}", (bl,),
        [pl.BlockSpec((N_MEM, D_MODEL), lambda b: (b, 0)), pl.BlockSpec((None, 1, D_MODEL), lambda b: (layer, 0, 0)),
         pl.BlockSpec((None, D_MODEL, GW), lambda b: (layer, 0, 0)), pl.BlockSpec((None, D_MODEL, GW), lambda b: (layer, 0, 1)),
         pl.BlockSpec((None, 1, GW), lambda b: (layer, 0, 0)), blk, blk],
        [pl.BlockSpec((D_MODEL, GW), lambda b: (0, 0)), pl.BlockSpec((D_MODEL, GW), lambda b: (0, 0)),
         pl.BlockSpec((1, D_MODEL), lambda b: (0, 0)), pl.BlockSpec((1, GW), lambda b: (0, 0))],
        [_sds((D_MODEL, GW)), _sds((D_MODEL, GW)), _sds((1, D_MODEL)), _sds((1, GW))],
    )(mem2, g, wkv, wkv, kw, dk, dv)


def _mem_attn_fn(mq, qw, k, v, bd64):
    qn = _headrms(mq, qw, bd64)
    out = None
    for h in range(NH):
        hm = _hmask(h)
        lg = mm_nt(qn * hm, k) * SCALE
        e = jnp.exp(lg - lax.stop_gradient(jnp.max(lg, axis=-1, keepdims=True)))
        p = e / jnp.sum(e, axis=-1, keepdims=True)
        part = mm(p, v) * hm
        out = part if out is None else out + part
    return out


def mem_attn_fwd(pe, qw, k, v, bl, s, layer, tag):
    nq = s // TQ

    def body(q_ref, qw_ref, k_ref, v_ref, o_ref):
        o_ref[...] = _mem_attn_fn(q_ref[...], qw_ref[...], k_ref[...], v_ref[...], _bdmask() * (1.0 / HD))

    tok = pl.BlockSpec((TQ, GW), lambda b, i: (b * nq + i, 0))
    kv = pl.BlockSpec((N_MEM, GW), lambda b, i: (b, 0))
    return _call(
        body, f"mem_attn_fwd_{tag}", (bl, nq), [tok, pl.BlockSpec((None, 1, GW), lambda b, i: (layer, 0, 0)), kv, kv],
        tok, _sds((bl * s, GW)),
    )(pe, qw, k, v)


def mem_attn_bwd(pe, qw, k, v, do, bl, s, layer, tag):
    nq = s // TQ

    def body(q_ref, qw_ref, k_ref, v_ref, do_ref, dq_ref, dk_ref, dv_ref, dqw_ref):
        i = pl.program_id(1)
        bd64 = _bdmask() * (1.0 / HD)
        _, vjp = jax.vjp(lambda *a: _mem_attn_fn(*a, bd64), q_ref[...], qw_ref[...], k_ref[...], v_ref[...])
        dq, dqw, dk, dv = vjp(do_ref[...])
        dq_ref[...] = dq
        _acc(dk_ref, dk, i == 0)
        _acc(dv_ref, dv, i == 0)
        _acc(dqw_ref, dqw, jnp.logical_and(pl.program_id(0) == 0, i == 0))

    tok = pl.BlockSpec((TQ, GW), lambda b, i: (b * nq + i, 0))
    kv = pl.BlockSpec((N_MEM, GW), lambda b, i: (b, 0))
    return _call(
        body, f"mem_attn_bwd_{tag}", (bl, nq),
        [tok, pl.BlockSpec((None, 1, GW), lambda b, i: (layer, 0, 0)), kv, kv, tok],
        [tok, kv, kv, pl.BlockSpec((1, GW), lambda b, i: (0, 0))],
        [_sds((bl * s, GW)), _sds((bl * N_MEM, GW)), _sds((bl * N_MEM, GW)), _sds((1, GW))],
    )(pe, qw, k, v, do)


def _gate_out_fn(outs, gates, wparts):
    y = None
    for o, g, w in zip(outs, gates, wparts):
        part = mm(o * _silu(g), w)
        y = part if y is None else y + part
    return y


def outproj_fwd(x2, outs, pg, wout, layer, tag):
    t = x2.shape[0]

    def body(x_ref, oa, ob, oc, od, oe, g_ref, w_ref, y_ref):
        outs_ = [r[...] for r in (oa, ob, oc, od, oe)]
        gates = [g_ref[:, j * GW:(j + 1) * GW] for j in range(5)]
        wparts = [w_ref[j * GW:(j + 1) * GW, :] for j in range(5)]
        y_ref[...] = x_ref[...] + _gate_out_fn(outs_, gates, wparts)

    tok = pl.BlockSpec((TQ, GW), lambda i: (i, 0))
    big = pl.BlockSpec((TQ, D_MODEL), lambda i: (i, 0))
    return _call(
        body, f"outproj_fwd_{tag}", (t // TQ,),
        [big] + [tok] * 5 + [pl.BlockSpec((TQ, D_MIX), lambda i: (i, 0)),
                            pl.BlockSpec((None, D_MIX, D_MODEL), lambda i: (layer, 0, 0))],
        big, _sds((t, D_MODEL)),
    )(x2, *outs, pg, wout)


def outproj_bwd(outs, pg, wout, dy, layer, tag):
    t = dy.shape[0]

    def body(oa, ob, oc, od, oe, g_ref, w_ref, dy_ref, da, db, dc, dd, de, dg_ref, dw_ref):
        outs_ = [r[...] for r in (oa, ob, oc, od, oe)]
        gates = [g_ref[:, j * GW:(j + 1) * GW] for j in range(5)]
        wparts = [w_ref[j * GW:(j + 1) * GW, :].astype(F32) for j in range(5)]
        _, vjp = jax.vjp(_gate_out_fn, outs_, gates, wparts)
        douts, dgates, dws = vjp(dy_ref[...])
        for r, val in zip((da, db, dc, dd, de), douts):
            r[...] = val
        first = pl.program_id(0) == 0
        for j in range(5):
            dg_ref[:, j * GW:(j + 1) * GW] = dgates[j]

        @pl.when(first)
        def _():
            for j in range(5):
                dw_ref[j * GW:(j + 1) * GW, :] = dws[j]

        @pl.when(jnp.logical_not(first))
        def _():
            for j in range(5):
                dw_ref[j * GW:(j + 1) * GW, :] += dws[j]

    tok = pl.BlockSpec((TQ, GW), lambda i: (i, 0))
    return _call(
        body, f"outproj_bwd_{tag}", (t // TQ,),
        [tok] * 5 + [pl.BlockSpec((TQ, D_MIX), lambda i: (i, 0)), pl.BlockSpec((None, D_MIX, D_MODEL), lambda i: (layer, 0, 0)),
                     pl.BlockSpec((TQ, D_MODEL), lambda i: (i, 0))],
        [tok] * 5 + [pl.BlockSpec((TQ, D_MIX), lambda i: (i, 0)), pl.BlockSpec((D_MIX, D_MODEL), lambda i: (0, 0))],
        [_sds((t, GW))] * 5 + [_sds((t, D_MIX)), _sds((D_MIX, D_MODEL))],
    )(*outs, pg, wout, dy)


def loss_head(y, tgt):
    t = y.shape[0]

    def body(y_ref, t_ref, l_ref, dy_ref):
        diff = y_ref[...] - t_ref[...]
        dy_ref[...] = diff * (1.0 / D_MODEL)
        part = 0.5 * jnp.sum(jnp.sum(diff * diff, axis=-1, keepdims=True) * (1.0 / D_MODEL), axis=0, keepdims=True)
        _acc(l_ref, jnp.broadcast_to(part, (8, 128)), pl.program_id(0) == 0)

    big = pl.BlockSpec((TQ, D_MODEL), lambda i: (i, 0))
    return _call(body, "loss_head", (t // TQ,), [big, big], [pl.BlockSpec((8, 128), lambda i: (0, 0)), big],
                 [_sds((8, 128)), _sds((t, D_MODEL))])(y, tgt)


def layer_fwd(x2, mem2, p, layer, bl, s):
    tag = f"l{layer}"
    ht, pa, pb, pc, pd, pe, pg, pf = inproj_fwd(x2, p["norm_g"], p["w_in"], layer, tag)
    qn, kn, vb, cq, ck = fox_prep_fwd(pa, pf, p["fox_q_norm"], p["fox_k_norm"], p["fox_f_bias"], bl, s, layer, tag)
    oa, lse = fox_attn_fwd(qn, kn, vb, cq, ck, bl, s, tag)
    ob = sb_attn_fwd(pb, bl, s, tag)
    oc, states = hgrn_fwd(pc, p["lb"], p["hgrn_out_norm"], bl, s, layer, tag)
    od = pool_fwd(pd, p["pool_wbd"], p["pool_scale"], bl, s, layer, tag)
    mk, mv = mem_prep_fwd(mem2, p["mem_norm_g"], p["mem_w_kv"], p["mem_k_norm"], bl, layer, tag)
    oe = mem_attn_fwd(pe, p["mem_q_norm"], mk, mv, bl, s, layer, tag)
    y = outproj_fwd(x2, (oa, ob, oc, od, oe), pg, p["w_out"], layer, tag)
    saved = dict(x2=x2, ht=ht, pa=pa, pb=pb, pc=pc, pd=pd, pe=pe, pg=pg, pf=pf, qn=qn, kn=kn, vb=vb, cq=cq, ck=ck,
                 oa=oa, lse=lse, ob=ob, oc=oc, states=states, od=od, mk=mk, mv=mv, oe=oe)
    return y, saved


def layer_bwd(dy, mem2, p, sv, layer, bl, s):
    tag = f"l{layer}"
    (doa, dob, doc, dod, doe, dg_gates, dwout) = outproj_bwd((sv["oa"], sv["ob"], sv["oc"], sv["od"], sv["oe"]), sv["pg"],
                                                              p["w_out"], dy, layer, tag)
    dqn, dkn, dv, dck = fox_attn_bwd(sv["qn"], sv["kn"], sv["vb"], sv["cq"], sv["ck"], sv["lse"], doa, bl, s, tag)
    d_a, d_f, dqw, dkw, dbias = fox_prep_bwd(sv["pa"], sv["pf"], p["fox_q_norm"], p["fox_k_norm"], p["fox_f_bias"], sv["cq"],
                                             dqn, dkn, dv, dck, bl, s, layer, tag)
    dsq, dsk, dsv = sb_attn_bwd(sv["pb"], dob, bl, s, tag)
    d_c, dlb, dwn = hgrn_bwd(sv["pc"], p["lb"], p["hgrn_out_norm"], sv["states"], doc, bl, s, layer, tag)
    d_d, dwbd, dpscale = pool_bwd(sv["pd"], p["pool_wbd"], p["pool_scale"], dod, bl, s, layer, tag)
    d_e, dmk, dmv, dmqw = mem_attn_bwd(sv["pe"], p["mem_q_norm"], sv["mk"], sv["mv"], doe, bl, s, layer, tag)
    dwk, dwv, dmg, dmkw = mem_prep_bwd(mem2, p["mem_norm_g"], p["mem_w_kv"], p["mem_k_norm"], dmk, dmv, bl, layer, tag)
    dpieces = (d_a, dsq, dsk, dsv, d_c, d_d, d_e, dg_gates, d_f)
    dx, dng = inproj_bwd_dx(sv["x2"], p["norm_g"], p["w_in"], dy, dpieces, layer, tag)
    dwin = jnp.concatenate([matmul_acc(sv["ht"], dp, f"{tag}_{nm}") for (nm, _, _), dp in zip(BWD_PIECES, dpieces)], axis=1)
    grads = dict(norm_g=dng, w_in=dwin, fox_f_bias=dbias, fox_q_norm=dqw, fox_k_norm=dkw, lb=dlb, hgrn_out_norm=dwn,
                 pool_wbd=dwbd, pool_scale=dpscale, mem_norm_g=dmg, mem_w_kv=jnp.concatenate([dwk, dwv], axis=1),
                 mem_q_norm=dmqw, mem_k_norm=dmkw, w_out=dwout)
    return dx, grads


def _tile4(w):
    return jnp.tile(w, (1, NH))[:, None, :]


def prepare_params(norm_g, w_in_p, fox_f_bias, fox_q_norm, fox_k_norm, hgrn_lb_logits, hgrn_out_norm, pool_w, pool_scale,
                   mem_norm_g, mem_w_kv, mem_q_norm, mem_k_norm, w_out):
    p1 = jax.nn.sigmoid(hgrn_lb_logits[1] - hgrn_lb_logits[0])
    lb = jnp.stack([jnp.zeros_like(p1), jnp.clip(p1, 0.0, 1.0 - 1e-6)])
    eye = jnp.eye(4, dtype=F32)
    wbd = jnp.einsum("lgcd,gh->lgchd", pool_w, eye).reshape(2, GW, GW)
    return dict(norm_g=norm_g[:, None, :], w_in=w_in_p, fox_f_bias=jnp.pad(fox_f_bias, ((0, 0), (0, 124)))[:, None, :],
                fox_q_norm=_tile4(fox_q_norm), fox_k_norm=_tile4(fox_k_norm), lb=lb[:, None, :],
                hgrn_out_norm=hgrn_out_norm[:, None, :], pool_wbd=wbd, pool_scale=pool_scale[:, None, :],
                mem_norm_g=mem_norm_g[:, None, :], mem_w_kv=mem_w_kv, mem_q_norm=_tile4(mem_q_norm),
                mem_k_norm=_tile4(mem_k_norm), w_out=w_out)


def local_step(x, mem, tgt, p):
    bl, s, _ = x.shape
    x2, mem2, tgt2 = x.reshape(bl * s, D_MODEL), mem.reshape(bl * N_MEM, D_MODEL), tgt.reshape(bl * s, D_MODEL)
    y0, sv0 = layer_fwd(x2, mem2, p, 0, bl, s)
    y1, sv1 = layer_fwd(y0, mem2, p, 1, bl, s)
    lpart, dy = loss_head(y1, tgt2)
    dx1, g1 = layer_bwd(dy, mem2, p, sv1, 1, bl, s)
    dx0, g0 = layer_bwd(dx1, mem2, p, sv0, 0, bl, s)
    return lpart[0, 0], dx0.reshape(bl, s, D_MODEL), g0, g1


_ANY = pl.BlockSpec(memory_space=pl.ANY)


def _me_and_peers():
    x, y, c = lax.axis_index("x"), lax.axis_index("y"), lax.axis_index("c")
    peers = []
    for k in range(1, N_DEV):
        px = 1 - x if (k >> 2) & 1 else x
        py = 1 - y if (k >> 1) & 1 else y
        pc = 1 - c if k & 1 else c
        peers.append(((px, py, pc), 4 * px + 2 * py + pc))
    return 4 * x + 2 * y + c, peers


def all_gather_rows(xs, tag):
    nl, r, c = xs.shape

    def body(x_ref, o_ref, send_sems, recv_sems, local_sem):
        x, y, cc = lax.axis_index("x"), lax.axis_index("y"), lax.axis_index("c")
        me, sibling = (x, y, cc), (x, y, 1 - cc)
        chips = [(1 - x, y), (x, 1 - y), (1 - x, 1 - y)]

        def rows(px, py, pc):
            return o_ref.at[:, pl.ds((4 * px + 2 * py + pc) * r, r), :]

        def copy(k, block, to, src=None):
            return pltpu.make_async_remote_copy(src_ref=rows(*block) if src is None else src, dst_ref=rows(*block),
                                                send_sem=send_sems.at[k], recv_sem=recv_sems.at[k], device_id=to,
                                                device_id_type=pl.DeviceIdType.MESH)

        mine = pltpu.make_async_copy(x_ref, rows(*me), local_sem)
        mine.start()
        first = [copy(0, me, sibling, src=x_ref)] + [copy(1 + j, me, (*chip, cc), src=x_ref) for j, chip in enumerate(chips)]
        for cp in first:
            cp.start()
        passed = [copy(4 + j, (*chip, cc), sibling) for j, chip in enumerate(chips)]
        for j, chip in enumerate(chips):
            copy(1 + j, (*chip, cc), me).wait_recv()
            passed[j].start()
        copy(0, sibling, me).wait_recv()
        for j, chip in enumerate(chips):
            copy(4 + j, (*chip, 1 - cc), me).wait_recv()
        for cp in first + passed:
            cp.wait_send()
        mine.wait()

    return pl.pallas_call(
        body, name=f"all_gather_{tag}", in_specs=[_ANY], out_specs=_ANY, out_shape=_sds((nl, N_DEV * r, c), xs.dtype),
        scratch_shapes=[pltpu.SemaphoreType.DMA((N_DEV - 1,)), pltpu.SemaphoreType.DMA((N_DEV - 1,)), pltpu.SemaphoreType.DMA],
    )(xs)


def exchange_cores(part, tag):
    nl, _, _, r, c = part.shape

    def body(p_ref, theirs_ref, send_sems, recv_sems):
        x, y, cc = lax.axis_index("x"), lax.axis_index("y"), lax.axis_index("c")
        copies = []
        for l in range(nl):
            for q in range(4):
                k = l * 4 + q
                copies.append(pltpu.make_async_remote_copy(
                    src_ref=p_ref.at[l, q, pl.ds(1 - cc, 1)], dst_ref=theirs_ref.at[l, q], send_sem=send_sems.at[k],
                    recv_sem=recv_sems.at[k], device_id=(x, y, 1 - cc), device_id_type=pl.DeviceIdType.MESH))
        for cp in copies:
            cp.start()
        for cp in copies:
            cp.wait()

    nsem = pltpu.SemaphoreType.DMA((nl * 4,))
    return pl.pallas_call(
        body, name=f"exchange_cores_{tag}", in_specs=[_ANY], out_specs=_ANY, out_shape=_sds((nl, 4, 1, r, c), part.dtype),
        scratch_shapes=[nsem, nsem],
    )(part)


def add_core_halves(part5, theirs, core, tag):
    nl, _, _, r, c = part5.shape
    tr = 64 if r % 64 == 0 else 32

    def body(core_ref, a_ref, b_ref, o_ref):
        o_ref[...] = a_ref[...] + b_ref[...]

    blk = lambda which: pl.BlockSpec((None, None, None, tr, c), lambda l, q, i, cref: (l, q, cref[0] if which else 0, i, 0))
    return pl.pallas_call(
        body, name=f"add_core_halves_{tag}", out_shape=_sds((nl, 4, 1, r, c)),
        grid_spec=pltpu.PrefetchScalarGridSpec(num_scalar_prefetch=1, grid=(nl, 4, r // tr), in_specs=[blk(True), blk(False)],
                                               out_specs=blk(False)),
        compiler_params=pltpu.CompilerParams(dimension_semantics=("arbitrary",) * 3, vmem_limit_bytes=VMEM_LIMIT_BYTES),
    )(core, part5, theirs)


def exchange_chips(s4, tag):
    nl, _, _, r, c = s4.shape

    def body(s_ref, o_ref, send_sems, recv_sems, local_sem):
        x, y, cc = lax.axis_index("x"), lax.axis_index("y"), lax.axis_index("c")
        local = pltpu.make_async_copy(s_ref.at[:, pl.ds(2 * x + y, 1)], o_ref.at[0], local_sem)
        local.start()
        copies = []
        for k in range(1, 4):
            px = 1 - x if (k >> 1) & 1 else x
            py = 1 - y if k & 1 else y
            copies.append(pltpu.make_async_remote_copy(
                src_ref=s_ref.at[:, pl.ds(2 * px + py, 1)], dst_ref=o_ref.at[k], send_sem=send_sems.at[k - 1],
                recv_sem=recv_sems.at[k - 1], device_id=(px, py, cc), device_id_type=pl.DeviceIdType.MESH))
        for cp in copies:
            cp.start()
        for cp in copies:
            cp.wait()
        local.wait()

    return pl.pallas_call(
        body, name=f"exchange_chips_{tag}", in_specs=[_ANY], out_specs=_ANY, out_shape=_sds((4, nl, 1, 1, r, c), s4.dtype),
        scratch_shapes=[pltpu.SemaphoreType.DMA((3,)), pltpu.SemaphoreType.DMA((3,)), pltpu.SemaphoreType.DMA],
    )(s4)


def exchange_row_blocks(part, tag):
    nl, r8, c = part.shape
    r = r8 // N_DEV

    def body(p_ref, o_ref, send_sems, recv_sems, local_sem):
        me, peers = _me_and_peers()
        rows = lambda idx: p_ref.at[:, pl.ds(idx * r, r), :]
        mine = pltpu.make_async_copy(rows(me), o_ref.at[0], local_sem)
        mine.start()
        copies = [pltpu.make_async_remote_copy(src_ref=rows(idx), dst_ref=o_ref.at[k + 1], send_sem=send_sems.at[k],
                                               recv_sem=recv_sems.at[k], device_id=dev, device_id_type=pl.DeviceIdType.MESH)
                  for k, (dev, idx) in enumerate(peers)]
        for cp in copies:
            cp.start()
        for cp in copies:
            cp.wait()
        mine.wait()

    return pl.pallas_call(
        body, name=f"exchange_{tag}", in_specs=[_ANY], out_specs=_ANY, out_shape=_sds((N_DEV, nl, r, c), part.dtype),
        scratch_shapes=[pltpu.SemaphoreType.DMA((N_DEV - 1,)), pltpu.SemaphoreType.DMA((N_DEV - 1,)), pltpu.SemaphoreType.DMA],
    )(part)


def _row_tile(rows):
    if rows <= 512 and rows % 64:
        return rows
    for t in (64, 40, 32, 16, 8):
        if rows % t == 0:
            return t
    return rows


def sum_slots(slots, tag):
    ns, rows, c = slots.shape
    tr = _row_tile(rows)

    def body(s_ref, o_ref):
        acc = s_ref[0]
        for k in range(1, ns):
            acc = acc + s_ref[k]
        o_ref[...] = acc

    return _call(body, f"sum_slots_{tag}", (rows // tr,), [pl.BlockSpec((ns, tr, c), lambda i: (0, i, 0))],
                 pl.BlockSpec((tr, c), lambda i: (i, 0)), _sds((rows, c)))(slots)


def _adamw(w, g, m, v):
    m = ADAM_B1 * m + (1.0 - ADAM_B1) * g
    v = ADAM_B2 * v + (1.0 - ADAM_B2) * (g * g)
    m_hat = m / (1.0 - ADAM_B1 ** ADAM_STEP)
    v_hat = v / (1.0 - ADAM_B2 ** ADAM_STEP)
    delta = -ADAM_LR * (m_hat / (jnp.sqrt(v_hat) + ADAM_EPS) + ADAM_WD * w)
    return delta, m, v


def adam_update(w, m, v, g, tag, slots=False):
    rows, c = w.shape
    tr = _row_tile(rows)
    ns = g.shape[0] if slots else 0

    def body(w_ref, m_ref, v_ref, g_ref, go_ref, d_ref, mo_ref, vo_ref):
        if slots:
            g = g_ref[0]
            for k in range(1, ns):
                g = g + g_ref[k]
        else:
            g = g_ref[...]
        d, mn, vn = _adamw(w_ref[...], g, m_ref[...], v_ref[...])
        go_ref[...] = g
        d_ref[...] = d
        mo_ref[...] = mn
        vo_ref[...] = vn

    blk = pl.BlockSpec((tr, c), lambda i: (i, 0))
    gspec = pl.BlockSpec((ns, tr, c), lambda i: (0, i, 0)) if slots else blk
    return _call(body, f"adam_{tag}", (rows // tr,), [blk, blk, blk, gspec], [blk] * 4, [_sds((rows, c))] * 4)(w, m, v, g)


_SMALL = (("norm_g", (2, 1024)), ("fox_f_bias", (2, 4)), ("fox_q_norm", (2, 64)), ("fox_k_norm", (2, 64)),
          ("hgrn_lb_logits", (2, 256)), ("hgrn_out_norm", (2, 256)), ("pool_w", (2, 4, 64, 64)), ("pool_scale", (2, 256)),
          ("mem_norm_g", (2, 1024)), ("mem_q_norm", (2, 64)), ("mem_k_norm", (2, 64)))
_SLAB_ROWS = 312


def pack_small(d):
    flat = jnp.concatenate([d[n].reshape(-1) for n, _ in _SMALL])
    return jnp.pad(flat, (0, _SLAB_ROWS * 128 - flat.shape[0])).reshape(_SLAB_ROWS, 128)


def unpack_small(slab):
    flat, out, off = slab.reshape(-1), {}, 0
    for n, shp in _SMALL:
        size = 1
        for e in shp:
            size *= e
        out[n] = flat[off:off + size].reshape(shp)
        off += size
    return out


def small_grads(g0, g1, lb_logits):
    st = lambda f: jnp.stack([f(g0), f(g1)])
    heads = lambda a: a.reshape(NH, HD).sum(0)
    p1 = jax.nn.sigmoid(lb_logits[1] - lb_logits[0])
    inside = (p1 > 0.0) & (p1 < 1.0 - 1e-6)
    dl1 = jnp.where(inside, g1["lb"][0] * p1 * (1.0 - p1), 0.0)
    diag = lambda a: jnp.stack([a.reshape(4, HD, 4, HD)[i, :, i, :] for i in range(4)])
    return dict(norm_g=st(lambda g: g["norm_g"][0]), fox_f_bias=st(lambda g: g["fox_f_bias"][0, :NH]),
                fox_q_norm=st(lambda g: heads(g["fox_q_norm"])), fox_k_norm=st(lambda g: heads(g["fox_k_norm"])),
                hgrn_lb_logits=jnp.stack([-dl1, dl1]), hgrn_out_norm=st(lambda g: g["hgrn_out_norm"][0]),
                pool_w=st(lambda g: diag(g["pool_wbd"])), pool_scale=st(lambda g: g["pool_scale"][0]),
                mem_norm_g=st(lambda g: g["mem_norm_g"][0]), mem_q_norm=st(lambda g: heads(g["mem_q_norm"])),
                mem_k_norm=st(lambda g: heads(g["mem_k_norm"])))


def kernel(x, mem, norm_g, w_in, fox_f_bias, fox_q_norm, fox_k_norm, hgrn_lb_logits, hgrn_out_norm, pool_w, pool_scale, mem_norm_g, mem_w_kv, mem_q_norm, mem_k_norm, w_out, loss_target, m_norm_g, m_w_in, m_fox_f_bias, m_fox_q_norm, m_fox_k_norm, m_hgrn_lb_logits, m_hgrn_out_norm, m_pool_w, m_pool_scale, m_mem_norm_g, m_mem_w_kv, m_mem_q_norm, m_mem_k_norm, m_w_out, v_norm_g, v_w_in, v_fox_f_bias, v_fox_q_norm, v_fox_k_norm, v_hgrn_lb_logits, v_hgrn_out_norm, v_pool_w, v_pool_scale, v_mem_norm_g, v_mem_w_kv, v_mem_q_norm, v_mem_k_norm, v_w_out):
    given = dict(norm_g=(norm_g, m_norm_g, v_norm_g), w_in=(w_in, m_w_in, v_w_in), fox_f_bias=(fox_f_bias, m_fox_f_bias, v_fox_f_bias),
                 fox_q_norm=(fox_q_norm, m_fox_q_norm, v_fox_q_norm), fox_k_norm=(fox_k_norm, m_fox_k_norm, v_fox_k_norm),
                 hgrn_lb_logits=(hgrn_lb_logits, m_hgrn_lb_logits, v_hgrn_lb_logits),
                 hgrn_out_norm=(hgrn_out_norm, m_hgrn_out_norm, v_hgrn_out_norm), pool_w=(pool_w, m_pool_w, v_pool_w),
                 pool_scale=(pool_scale, m_pool_scale, v_pool_scale), mem_norm_g=(mem_norm_g, m_mem_norm_g, v_mem_norm_g),
                 mem_w_kv=(mem_w_kv, m_mem_w_kv, v_mem_w_kv), mem_q_norm=(mem_q_norm, m_mem_q_norm, v_mem_q_norm),
                 mem_k_norm=(mem_k_norm, m_mem_k_norm, v_mem_k_norm), w_out=(w_out, m_w_out, v_w_out))
    order = ("norm_g", "w_in", "fox_f_bias", "fox_q_norm", "fox_k_norm", "hgrn_lb_logits", "hgrn_out_norm", "pool_w",
             "pool_scale", "mem_norm_g", "mem_w_kv", "mem_q_norm", "mem_k_norm", "w_out")

    w_in_full = all_gather_rows(_cast(permute_cols(w_in)), "w_in")
    w_out_full = all_gather_rows(_cast(w_out), "w_out")
    w_kv_full = all_gather_rows(_cast(mem_w_kv), "w_kv")
    p = prepare_params(norm_g, w_in_full, fox_f_bias, fox_q_norm, fox_k_norm, hgrn_lb_logits, hgrn_out_norm, pool_w,
                       pool_scale, mem_norm_g, w_kv_full, mem_q_norm, mem_k_norm, w_out_full)

    loss_part, grad_x, g0, g1 = local_step(x, mem, loss_target, p)
    loss = lax.psum(loss_part, ("x", "y", "c"))

    res = {}
    core = lax.axis_index("c").astype(jnp.int32).reshape(1)

    def sharded(name, g2, unperm=False, two_stage=True):
        w, m, v = given[name]
        nl, r, c = w.shape
        cp = g2.shape[-1]
        if two_stage:
            part5 = g2.reshape(nl, 4, 2, r, cp)
            s4 = add_core_halves(part5, exchange_cores(part5, name), core, name)
            slots = exchange_chips(s4, name).reshape(4, nl * r, cp)
        else:
            slots = exchange_row_blocks(g2, name).reshape(N_DEV, nl * r, cp)
        if unperm:
            g = sum_slots(slots, name).reshape(nl, r, cp)
            out = adam_update(w.reshape(nl * r, c), m.reshape(nl * r, c), v.reshape(nl * r, c),
                              unpermute_cols(g).reshape(nl * r, c), name)
        else:
            out = adam_update(w.reshape(nl * r, c), m.reshape(nl * r, c), v.reshape(nl * r, c), slots, name, slots=True)
        res[name] = tuple(o.reshape(nl, r, c) for o in out)

    sharded("w_in", jnp.stack([g0["w_in"], g1["w_in"]]), unperm=True)
    sharded("w_out", jnp.stack([g0["w_out"], g1["w_out"]]))
    sharded("mem_w_kv", jnp.stack([g0["mem_w_kv"], g1["mem_w_kv"]]), two_stage=False)

    gsmall = pack_small(small_grads(g0, g1, hgrn_lb_logits))
    gathered = all_gather_rows(gsmall[None], "small").reshape(N_DEV, _SLAB_ROWS, 128)
    slabs = adam_update(*[pack_small({n: given[n][j] for n, _ in _SMALL}) for j in range(3)], gathered, "small", slots=True)
    small = [unpack_small(sl) for sl in slabs]
    for n, _ in _SMALL:
        res[n] = tuple(small[j][n] for j in range(4))

    return (loss, grad_x, *[res[n][0] for n in order], *[res[n][1] for n in order], *[res[n][2] for n in order],
            *[res[n][3] for n in order])
```

```python
import functools

import jax
import jax.numpy as jnp
from jax import lax
from jax.experimental import pallas as pl
from jax.experimental.pallas import tpu as pltpu

F32 = jnp.float32
BF = jnp.bfloat16
_MMT = BF

D_MODEL = 1024
GW = 256
HD = 64
NH = 4
CH = 64
N_MEM = 256
D_IN = 4100
D_INP = 4224
D_MIX = 1280
EPS = 1e-6
NEG_BIG = -1e30
LB_FLOOR = 1e-30
SCALE = HD ** -0.5
TQ = 256
N_DEV = 8
VMEM_LIMIT_BYTES = 56 * 1024 * 1024

ADAM_LR = 0.001
ADAM_B1 = 0.9
ADAM_B2 = 0.999
ADAM_EPS = 1e-08
ADAM_WD = 0.01
ADAM_STEP = 10

PIECES = (("A", 0, 768), ("B", 768, 768), ("C", 1536, 768), ("D", 2304, 256), ("E", 2560, 256),
          ("G", 2816, 1280), ("F", 4096, 128))
BWD_PIECES = (("A", 0, 768), ("Bq", 768, 256), ("Bk", 1024, 256), ("Bv", 1280, 256), ("C", 1536, 768),
              ("D", 2304, 256), ("E", 2560, 256), ("G", 2816, 1280), ("F", 4096, 128))
_ORIG = dict(fq=(0, 256), fk=(256, 512), fv=(512, 768), fg=(768, 1024), ff=(1024, 1028), sq=(1028, 1284),
             sk=(1284, 1540), sv=(1540, 1796), sg=(1796, 2052), hq=(2052, 2308), hf=(2308, 2564),
             hi=(2564, 2820), hg=(2820, 3076), pv=(3076, 3332), pg=(3332, 3588), mq=(3588, 3844), mg=(3844, 4100))
_PERM_ORDER = ("fq", "fk", "fv", "sq", "sk", "sv", "hq", "hf", "hi", "pv", "mq", "fg", "sg", "hg", "pg", "mg", "ff")
_ORIG_ORDER = ("fq", "fk", "fv", "fg", "ff", "sq", "sk", "sv", "sg", "hq", "hf", "hi", "hg", "pv", "pg", "mq", "mg")


def permute_cols(w):
    parts = [w[..., _ORIG[n][0]:_ORIG[n][1]] for n in _PERM_ORDER]
    parts.append(jnp.zeros(w.shape[:-1] + (D_INP - D_IN,), w.dtype))
    return jnp.concatenate(parts, axis=-1)


def unpermute_cols(g):
    start, off = {}, 0
    for n in _PERM_ORDER:
        start[n] = off
        off += _ORIG[n][1] - _ORIG[n][0]
    return jnp.concatenate([g[..., start[n]:start[n] + _ORIG[n][1] - _ORIG[n][0]] for n in _ORIG_ORDER], axis=-1)


def _cast(a):
    return a.astype(_MMT)


def _dg(a, b, ca, cb):
    return lax.dot_general(a, b, (((ca,), (cb,)), ((), ())), preferred_element_type=F32)


@jax.custom_vjp
def mm(a, b):
    return _dg(_cast(a), _cast(b), 1, 0)


@jax.custom_vjp
def mm_nt(a, b):
    return _dg(_cast(a), _cast(b), 1, 1)


@jax.custom_vjp
def mm_tn(a, b):
    return _dg(_cast(a), _cast(b), 0, 0)


mm.defvjp(lambda a, b: (mm(a, b), (a, b)),
          lambda r, g: (mm_nt(g, r[1]).astype(r[0].dtype), mm_tn(r[0], g).astype(r[1].dtype)))
mm_nt.defvjp(lambda a, b: (mm_nt(a, b), (a, b)),
             lambda r, g: (mm(g, r[1]).astype(r[0].dtype), mm_tn(g, r[0]).astype(r[1].dtype)))
mm_tn.defvjp(lambda a, b: (mm_tn(a, b), (a, b)),
             lambda r, g: (mm_nt(r[1], g).astype(r[0].dtype), mm(r[0], g).astype(r[1].dtype)))


def _split(a):
    hi = a.astype(_MMT)
    lo = (a - hi.astype(F32)).astype(_MMT)
    return hi, lo


@jax.custom_vjp
def xr(a, c, ct):
    hi, lo = _split(a)
    cc = _cast(c)
    return _dg(hi, cc, 1, 0) + _dg(lo, cc, 1, 0)


@jax.custom_vjp
def xl(c, ct, a):
    hi, lo = _split(a)
    cc = _cast(c)
    return _dg(cc, hi, 1, 0) + _dg(cc, lo, 1, 0)


xr.defvjp(lambda a, c, ct: (xr(a, c, ct), (c, ct)),
          lambda r, g: (xr(g, r[1], r[0]), jnp.zeros_like(r[0]), jnp.zeros_like(r[1])))
xl.defvjp(lambda c, ct, a: (xl(c, ct, a), (c, ct)),
          lambda r, g: (jnp.zeros_like(r[0]), jnp.zeros_like(r[1]), xl(r[1], r[0], g)))


def _iota(shape, dim):
    return lax.broadcasted_iota(jnp.int32, shape, dim)


def _hmask(h, n=GW):
    lane = _iota((1, n), 1)
    return ((lane >= h * HD) & (lane < (h + 1) * HD)).astype(F32)


def _bdmask(n=GW):
    return ((_iota((n, n), 0) >> 6) == (_iota((n, n), 1) >> 6)).astype(F32)


def _tri(n, kind="le"):
    r, c = _iota((n, n), 0), _iota((n, n), 1)
    return {"le": c <= r, "ge": c >= r, "gt": c > r, "lt": c < r}[kind].astype(F32)


def _onehot_lane(h, n=128):
    return (_iota((1, n), 1) == h).astype(F32)


def _logsig(x):
    return jnp.minimum(x, 0.0) - jnp.log1p(jnp.exp(-jnp.abs(x)))


def _sigmoid(x):
    return 0.5 * (jnp.tanh(0.5 * x) + 1.0)


def _silu(x):
    return x * _sigmoid(x)


def _rms(x, g):
    return x * lax.rsqrt(jnp.mean(x * x, axis=-1, keepdims=True) + EPS) * g


def _headrms(x, w, bd64):
    ms = xr(x * x, bd64, bd64)
    return x * lax.rsqrt(ms + EPS) * w


def _call(body, name, grid, in_specs, out_specs, out_shape, scratch=()):
    return pl.pallas_call(
        body, name=name, grid=grid, in_specs=in_specs, out_specs=out_specs, out_shape=out_shape,
        scratch_shapes=list(scratch),
        compiler_params=pltpu.CompilerParams(dimension_semantics=("arbitrary",) * len(grid),
                                             vmem_limit_bytes=VMEM_LIMIT_BYTES))


def _sds(shape, dtype=F32):
    return jax.ShapeDtypeStruct(shape, dtype)


def _acc(ref, val, first):
    @pl.when(first)
    def _():
        ref[...] = val

    @pl.when(jnp.logical_not(first))
    def _():
        ref[...] += val


def inproj_fwd(x2, g, w, layer, tag):
    t = x2.shape[0]

    def body(x_ref, g_ref, w_ref, ht_ref, *outs):
        h = _rms(x_ref[...], g_ref[...])
        hb = _cast(h)
        ht_ref[...] = _cast(h.T)
        for (_, c0, wd), o in zip(PIECES, outs):
            o[...] = _dg(hb, _cast(w_ref[:, c0:c0 + wd]), 1, 0)

    return _call(
        body, f"inproj_fwd_{tag}", (t // TQ,),
        [pl.BlockSpec((TQ, D_MODEL), lambda i: (i, 0)),
         pl.BlockSpec((None, 1, D_MODEL), lambda i: (layer, 0, 0)),
         pl.BlockSpec((None, D_MODEL, D_INP), lambda i: (layer, 0, 0))],
        [pl.BlockSpec((D_MODEL, TQ), lambda i: (0, i))] + [pl.BlockSpec((TQ, wd), lambda i: (i, 0)) for _, _, wd in PIECES],
        [_sds((D_MODEL, t), _MMT)] + [_sds((t, wd)) for _, _, wd in PIECES],
    )(x2, g, w)


def inproj_bwd_dx(x2, g, w, dy, dpieces, layer, tag):
    t = x2.shape[0]

    def body(x_ref, g_ref, w_ref, dy_ref, *rest):
        dps, (dx_ref, dg_ref) = rest[:len(BWD_PIECES)], rest[len(BWD_PIECES):]
        dh = None
        for (_, c0, wd), dp in zip(BWD_PIECES, dps):
            part = _dg(_cast(dp[...]), _cast(w_ref[:, c0:c0 + wd]), 1, 1)
            dh = part if dh is None else dh + part
        _, vjp = jax.vjp(_rms, x_ref[...], g_ref[...])
        dx, dg = vjp(dh)
        dx_ref[...] = dy_ref[...] + dx
        _acc(dg_ref, dg, pl.program_id(0) == 0)

    return _call(
        body, f"inproj_bwd_dx_{tag}", (t // TQ,),
        [pl.BlockSpec((TQ, D_MODEL), lambda i: (i, 0)),
         pl.BlockSpec((None, 1, D_MODEL), lambda i: (layer, 0, 0)),
         pl.BlockSpec((None, D_MODEL, D_INP), lambda i: (layer, 0, 0)),
         pl.BlockSpec((TQ, D_MODEL), lambda i: (i, 0))] + [pl.BlockSpec((TQ, wd), lambda i: (i, 0)) for _, _, wd in BWD_PIECES],
        [pl.BlockSpec((TQ, D_MODEL), lambda i: (i, 0)), pl.BlockSpec((1, D_MODEL), lambda i: (0, 0))],
        [_sds((t, D_MODEL)), _sds((1, D_MODEL))],
    )(x2, g, w, dy, *dpieces)


def matmul_acc(at, b, tag):
    m, t = at.shape
    n = b.shape[1]
    tn = {1280: 640, 768: 768}.get(n, n)
    tk = 2048 if t % 2048 == 0 else (512 if t % 512 == 0 else TQ)

    def body(a_ref, b_ref, o_ref):
        _acc(o_ref, _dg(_cast(a_ref[...]), _cast(b_ref[...]), 1, 0), pl.program_id(1) == 0)

    return _call(
        body, f"matmul_acc_{tag}", (n // tn, t // tk),
        [pl.BlockSpec((m, tk), lambda j, i: (0, i)), pl.BlockSpec((tk, tn), lambda j, i: (i, j))],
        pl.BlockSpec((m, tn), lambda j, i: (0, j)),
        _sds((m, n)),
    )(at, b)


def _fox_prep_fn(q, k, ff, qw, kw, bias, carry, bd64, tri, trit, last):
    qn = _headrms(q, qw, bd64)
    kn = _headrms(k, kw, bd64)
    lf = _logsig(ff + bias)
    c = xl(tri, trit, lf) + carry
    return qn, kn, c, jnp.sum(c * last, axis=0, keepdims=True)


def _prep_consts():
    return _bdmask() * (1.0 / HD), _tri(TQ), _tri(TQ, "ge"), (_iota((TQ, 1), 0) == TQ - 1).astype(F32)


def fox_prep_fwd(pa, pf, qw, kw, bias, bl, s, layer, tag):
    nq = s // TQ

    def body(q_ref, k_ref, v_ref, f_ref, qw_ref, kw_ref, b_ref, qn_ref, kn_ref, vb_ref, cq_ref, ck_ref, carry):
        @pl.when(pl.program_id(1) == 0)
        def _():
            carry[...] = jnp.zeros_like(carry)

        qn, kn, c, cl = _fox_prep_fn(q_ref[...], k_ref[...], f_ref[...], qw_ref[...], kw_ref[...], b_ref[...],
                                     carry[...], *_prep_consts())
        carry[...] = cl
        qn_ref[...] = _cast(qn)
        kn_ref[...] = _cast(kn)
        vb_ref[...] = _cast(v_ref[...])
        cq_ref[...] = c
        ck_ref[...] = c.T[0:8, :]

    tok = lambda j: pl.BlockSpec((TQ, GW), lambda b, i: (b * nq + i, j))
    par = lambda n: pl.BlockSpec((None, 1, n), lambda b, i: (layer, 0, 0))
    return _call(
        body, f"fox_prep_fwd_{tag}", (bl, nq),
        [tok(0), tok(1), tok(2), pl.BlockSpec((TQ, 128), lambda b, i: (b * nq + i, 0)), par(GW), par(GW), par(128)],
        [tok(0), tok(0), tok(0), pl.BlockSpec((TQ, 128), lambda b, i: (b * nq + i, 0)),
         pl.BlockSpec((None, 8, TQ), lambda b, i: (b, 0, i))],
        [_sds((bl * s, GW), _MMT)] * 3 + [_sds((bl * s, 128)), _sds((bl, 8, s))],
        [pltpu.VMEM((1, 128), F32)],
    )(pa, pa, pa, pf, qw, kw, bias)


def fox_prep_bwd(pa, pf, qw, kw, bias, cq, dqn, dkn, dv, dck, bl, s, layer, tag):
    nq = s // TQ

    def body(q_ref, k_ref, f_ref, qw_ref, kw_ref, b_ref, cq_ref, cprev_ref, dqn_ref, dkn_ref, dv_ref, dck_ref,
             da_ref, df_ref, dqw_ref, dkw_ref, db_ref, dcarry):
        i = pl.program_id(1)
        first = jnp.logical_and(pl.program_id(0) == 0, i == 0)

        @pl.when(i == 0)
        def _():
            dcarry[...] = jnp.zeros_like(dcarry)

        last = (_iota((TQ, 1), 0) == TQ - 1).astype(F32)
        carry_in = jnp.where(i == nq - 1, 0.0, jnp.sum(cprev_ref[...] * last, axis=0, keepdims=True))
        consts = _prep_consts()
        _, vjp = jax.vjp(lambda *a: _fox_prep_fn(*a, *consts), q_ref[...], k_ref[...], f_ref[...], qw_ref[...],
                         kw_ref[...], b_ref[...], carry_in)
        dc = dck_ref[...].T
        dq, dk, dff, dqw, dkw, dbias, dcin = vjp((dqn_ref[...], dkn_ref[...], dc, dcarry[...]))
        dcarry[...] = dcin
        da_ref[:, 0:GW] = dq
        da_ref[:, GW:2 * GW] = dk
        da_ref[:, 2 * GW:3 * GW] = dv_ref[...]
        df_ref[...] = dff
        _acc(dqw_ref, dqw, first)
        _acc(dkw_ref, dkw, first)
        _acc(db_ref, dbias, first)

    rv = lambda b, i: b * nq + (nq - 1 - i)
    tok = lambda j: pl.BlockSpec((TQ, GW), lambda b, i: (rv(b, i), j))
    tok0 = pl.BlockSpec((TQ, GW), lambda b, i: (rv(b, i), 0))
    t128 = pl.BlockSpec((TQ, 128), lambda b, i: (rv(b, i), 0))
    prev = pl.BlockSpec((TQ, 128), lambda b, i: (jnp.maximum(rv(b, i) - 1, 0), 0))
    par = lambda n: pl.BlockSpec((None, 1, n), lambda b, i: (layer, 0, 0))
    acc = lambda n: pl.BlockSpec((1, n), lambda b, i: (0, 0))
    return _call(
        body, f"fox_prep_bwd_{tag}", (bl, nq),
        [tok(0), tok(1), t128, par(GW), par(GW), par(128), t128, prev, tok0, tok0, tok0,
         pl.BlockSpec((None, 128, TQ), lambda b, i: (b, 0, nq - 1 - i))],
        [pl.BlockSpec((TQ, 3 * GW), lambda b, i: (rv(b, i), 0)), t128, acc(GW), acc(GW), acc(128)],
        [_sds((bl * s, 3 * GW)), _sds((bl * s, 128)), _sds((1, GW)), _sds((1, GW)), _sds((1, 128))],
        [pltpu.VMEM((1, 128), F32)],
    )(pa, pa, pf, qw, kw, bias, cq, cq, dqn, dkn, dv, dck)


def _lane_pick(x, h):
    return jnp.sum(x * _onehot_lane(h), axis=-1, keepdims=True)


TA = 128
SROWS = NH * TA


def _stack_heads(x, scale=1.0):
    return _cast(jnp.concatenate([x * (_hmask(h) * scale) for h in range(NH)], axis=0))


def _stack_cols(x):
    return jnp.concatenate([_lane_pick(x, h) for h in range(NH)], axis=0)


def _spread_heads(col):
    return sum(col[h * TA:(h + 1) * TA] * _hmask(h) for h in range(NH))


def _lanes_cat(w):
    return jnp.concatenate([w[h * TA:(h + 1) * TA] for h in range(NH)], axis=1)


def _mask_stack(x):
    return _cast(jnp.concatenate([x * _hmask(h).astype(x.dtype) for h in range(NH)], axis=0))


def _stack_rows(i):
    return i * TA + (_iota((SROWS, 1), 0) & (TA - 1))


def _n_key_tiles(i):
    return lax.shift_right_logical(i * TA, TQ.bit_length() - 1) + 1


def fox_attn_fwd(qn, kn, vb, cq, ck, bl, s, tag):
    nq = s // TA

    def body(q_ref, k_ref, v_ref, cq_ref, ck_ref, o_ref, lse_ref, acc, vst):
        i = pl.program_id(1)

        @pl.when(i == 0)
        def _():
            _fill_stacked(vst, v_ref, s)

        qs = _stack_heads(q_ref[...].astype(F32), SCALE)
        cqs = _stack_cols(cq_ref[...])
        row = _stack_rows(i)
        acc[...] = jnp.zeros_like(acc)

        def step(j, ml):
            m, l = ml
            ks = pl.ds(pl.multiple_of(j * TQ, TQ), TQ)
            ckb = jnp.concatenate([jnp.broadcast_to(ck_ref[h:h + 1, ks], (TA, TQ)) for h in range(NH)], axis=0)
            sc = _dg(qs, k_ref[ks, :], 1, 1) + cqs - ckb
            col = j * TQ + _iota((1, TQ), 1)
            sc = jnp.where(col <= row, sc, NEG_BIG)
            m_new = jnp.maximum(m, jnp.max(sc, axis=-1, keepdims=True))
            alpha = jnp.exp(m - m_new)
            p = jnp.exp(sc - m_new)
            vs = vst[pl.ds(pl.multiple_of(j * NH * TQ, NH * TQ), NH * TQ), :]
            acc[...] = _spread_heads(alpha) * acc[...] + _dg(_lanes_cat(_cast(p)), vs, 1, 0)
            return m_new, alpha * l + jnp.sum(p, axis=-1, keepdims=True)

        m, l = lax.fori_loop(0, _n_key_tiles(i), step, (jnp.full((SROWS, 1), NEG_BIG, F32), jnp.zeros((SROWS, 1), F32)))
        o_ref[...] = acc[...] / _spread_heads(l)
        lse_h = m + jnp.log(l)
        lse_ref[...] = sum(lse_h[h * TA:(h + 1) * TA] * _onehot_lane(h) for h in range(NH))

    tok = pl.BlockSpec((TA, GW), lambda b, i: (b * nq + i, 0))
    seq = pl.BlockSpec((s, GW), lambda b, i: (b, 0))
    t128 = pl.BlockSpec((TA, 128), lambda b, i: (b * nq + i, 0))
    return _call(
        body, f"fox_attn_fwd_{tag}", (bl, nq),
        [tok, seq, seq, t128, pl.BlockSpec((None, 8, s), lambda b, i: (b, 0, 0))],
        [tok, t128], [_sds((bl * s, GW)), _sds((bl * s, 128))],
        [pltpu.VMEM((TA, GW), F32), pltpu.VMEM((NH * s, GW), _MMT)],
    )(qn, kn, vb, cq, ck)


def fox_attn_bwd(qn, kn, vb, cq, ck, lse, do, bl, s, tag):
    nq = s // TA

    def body(q_ref, k_ref, v_ref, cq_ref, ck_ref, lse_ref, do_ref, dq_ref, dk_ref, dv_ref, dck_ref, dqa, p_s, dp_s, kst):
        i = pl.program_id(1)

        @pl.when(i == 0)
        def _():
            dk_ref[...] = jnp.zeros_like(dk_ref)
            dv_ref[...] = jnp.zeros_like(dv_ref)
            dck_ref[...] = jnp.zeros_like(dck_ref)
            _fill_stacked(kst, k_ref, s)

        qs = _stack_heads(q_ref[...].astype(F32), SCALE)
        dos = _stack_heads(do_ref[...])
        cqs, lses = _stack_cols(cq_ref[...]), _stack_cols(lse_ref[...])
        row = _stack_rows(i)
        dqa[...] = jnp.zeros_like(dqa)
        nk = _n_key_tiles(i)

        def probs(j, delta):
            ks = pl.ds(pl.multiple_of(j * TQ, TQ), TQ)
            ckb = jnp.concatenate([jnp.broadcast_to(ck_ref[h:h + 1, ks], (TA, TQ)) for h in range(NH)], axis=0)
            sc = _dg(qs, k_ref[ks, :], 1, 1) + cqs - ckb
            col = j * TQ + _iota((1, TQ), 1)
            p = jnp.where(col <= row, jnp.exp(sc - lses), 0.0)
            dp = _dg(dos, v_ref[ks, :], 1, 1)
            p_s[:, ks] = p
            dp_s[:, ks] = dp
            return delta + jnp.sum(p * dp, axis=-1, keepdims=True)

        delta = lax.fori_loop(0, nk, probs, jnp.zeros((SROWS, 1), F32))

        def step(j, carry):
            ks = pl.ds(pl.multiple_of(j * TQ, TQ), TQ)
            p = p_s[:, ks]
            ds = p * (dp_s[:, ks] - delta)
            dsb = _cast(ds)
            dqa[...] += _dg(_lanes_cat(dsb), kst[pl.ds(pl.multiple_of(j * NH * TQ, NH * TQ), NH * TQ), :], 1, 0) * SCALE
            dk_ref[ks, :] += _dg(dsb, qs, 0, 0)
            dv_ref[ks, :] += _dg(_cast(p), dos, 0, 0)
            for h in range(NH):
                dck_ref[h:h + 1, ks] -= jnp.sum(ds[h * TA:(h + 1) * TA], axis=0, keepdims=True)
            return carry

        lax.fori_loop(0, nk, step, 0)
        dq_ref[...] = dqa[...]

    tok = pl.BlockSpec((TA, GW), lambda b, i: (b * nq + i, 0))
    seq = pl.BlockSpec((s, GW), lambda b, i: (b, 0))
    t128 = pl.BlockSpec((TA, 128), lambda b, i: (b * nq + i, 0))
    return _call(
        body, f"fox_attn_bwd_{tag}", (bl, nq),
        [tok, seq, seq, t128, pl.BlockSpec((None, 8, s), lambda b, i: (b, 0, 0)), t128, tok],
        [tok, seq, seq, pl.BlockSpec((None, 128, s), lambda b, i: (b, 0, 0))],
        [_sds((bl * s, GW)), _sds((bl * s, GW)), _sds((bl * s, GW)), _sds((bl, 128, s))],
        [pltpu.VMEM((TA, GW), F32), pltpu.VMEM((SROWS, s), F32), pltpu.VMEM((SROWS, s), F32), pltpu.VMEM((NH * s, GW), _MMT)],
    )(qn, kn, vb, cq, ck, lse, do)


def _sb_block(qh, kb, valid, upper, r_carry):
    z = _dg(qh, kb, 1, 1)
    ls = _logsig(z)
    lom = ls - z if valid is None else jnp.where(valid, ls - z, 0.0)
    between = xr(lom, upper, upper) + r_carry
    w = jnp.exp(ls + between)
    return ls, lom, (w if valid is None else jnp.where(valid, w, 0.0))


def _fill_stacked(dst, src_ref, s):
    for j in range(s // TQ):
        dst[j * NH * TQ:(j + 1) * NH * TQ, :] = _mask_stack(src_ref[j * TQ:(j + 1) * TQ, :])


def sb_attn_fwd(pb, bl, s, tag):
    nq = s // TA

    def body(q_ref, k_ref, v_ref, o_ref, acc, vst):
        i = pl.program_id(1)

        @pl.when(i == 0)
        def _():
            _fill_stacked(vst, v_ref, s)

        qs = _stack_heads(q_ref[...], SCALE)
        upper = _tri(TQ, "lt")
        last = _n_key_tiles(i) - 1

        def step(j, r, valid):
            ks = pl.ds(pl.multiple_of(j * TQ, TQ), TQ)
            _, lom, w = _sb_block(qs, _cast(k_ref[ks, :]), valid, upper, r)
            acc[...] += _dg(_lanes_cat(_cast(w)), vst[pl.ds(pl.multiple_of(j * NH * TQ, NH * TQ), NH * TQ), :], 1, 0)
            return r + jnp.sum(lom, axis=-1, keepdims=True)

        acc[...] = jnp.zeros_like(acc)
        r = step(last, jnp.zeros((SROWS, 1), F32), last * TQ + _iota((1, TQ), 1) < _stack_rows(i))
        lax.fori_loop(0, last, lambda jj, r: step(last - 1 - jj, r, None), r)
        o_ref[...] = acc[...]

    tok = lambda j: pl.BlockSpec((TA, GW), lambda b, i: (b * nq + i, j))
    seq = lambda j: pl.BlockSpec((s, GW), lambda b, i: (b, j))
    return _call(
        body, f"sb_attn_fwd_{tag}", (bl, nq), [tok(0), seq(1), seq(2)],
        pl.BlockSpec((TA, GW), lambda b, i: (b * nq + i, 0)), _sds((bl * s, GW)),
        [pltpu.VMEM((TA, GW), F32), pltpu.VMEM((NH * s, GW), _MMT)],
    )(pb, pb, pb)


def sb_attn_bwd(pb, do, bl, s, tag):
    nq = s // TA

    def body(q_ref, k_ref, v_ref, do_ref, dq_ref, dk_ref, dv_ref, dqa, ls_s, lom_s, w_s, g_s, kst):
        i = pl.program_id(1)

        @pl.when(i == 0)
        def _():
            dk_ref[...] = jnp.zeros_like(dk_ref)
            dv_ref[...] = jnp.zeros_like(dv_ref)
            _fill_stacked(kst, k_ref, s)

        qs = _stack_heads(q_ref[...], SCALE)
        dos = _stack_heads(do_ref[...])
        upper = _tri(TQ, "lt")
        before = _tri(TQ, "gt")
        dqa[...] = jnp.zeros_like(dqa)
        last = _n_key_tiles(i) - 1
        diag = last * TQ + _iota((1, TQ), 1) < _stack_rows(i)

        def weights(j, r, valid):
            ks = pl.ds(pl.multiple_of(j * TQ, TQ), TQ)
            ls, lom, w = _sb_block(qs, _cast(k_ref[ks, :]), valid, upper, r)
            ls_s[:, ks] = ls
            lom_s[:, ks] = lom
            w_s[:, ks] = _cast(w)
            g_s[:, ks] = _dg(dos, _cast(v_ref[ks, :]), 1, 1) * w
            return r + jnp.sum(lom, axis=-1, keepdims=True)

        r = weights(last, jnp.zeros((SROWS, 1), F32), diag)
        lax.fori_loop(0, last, lambda jj, r: weights(last - 1 - jj, r, None), r)

        def step(j, cpre, valid):
            ks = pl.ds(pl.multiple_of(j * TQ, TQ), TQ)
            g = g_s[:, ks]
            pre = cpre + xr(g, before, before)
            dz = g * jnp.exp(lom_s[:, ks]) - jnp.exp(ls_s[:, ks]) * pre
            dzb = _cast(dz if valid is None else jnp.where(valid, dz, 0.0))
            dqa[...] += _dg(_lanes_cat(dzb), kst[pl.ds(pl.multiple_of(j * NH * TQ, NH * TQ), NH * TQ), :], 1, 0) * SCALE
            dk_ref[ks, :] += _dg(dzb, qs, 0, 0)
            dv_ref[ks, :] += _dg(w_s[:, ks], dos, 0, 0)
            return cpre + jnp.sum(g, axis=-1, keepdims=True)

        cpre = lax.fori_loop(0, last, lambda j, c: step(j, c, None), jnp.zeros((SROWS, 1), F32))
        step(last, cpre, diag)
        dq_ref[...] = dqa[...]

    tok = lambda j: pl.BlockSpec((TA, GW), lambda b, i: (b * nq + i, j))
    seq = lambda j: pl.BlockSpec((s, GW), lambda b, i: (b, j))
    return _call(
        body, f"sb_attn_bwd_{tag}", (bl, nq), [tok(0), seq(1), seq(2), tok(0)],
        [tok(0), seq(0), seq(0)], [_sds((bl * s, GW))] * 3,
        [pltpu.VMEM((TA, GW), F32), pltpu.VMEM((SROWS, s), F32), pltpu.VMEM((SROWS, s), F32),
         pltpu.VMEM((SROWS, s), _MMT), pltpu.VMEM((SROWS, s), F32), pltpu.VMEM((NH * s, GW), _MMT)],
    )(pb, pb, pb, do)


def _hgrn_consts():
    r, c = _iota((CH, CH), 0), _iota((CH, CH), 1)
    rr = _iota((CH, 1), 0)
    tri = (c <= r).astype(F32)
    lv = []
    for m in (8, 4, 2, 1):
        up = ((rr & (2 * m - 1)) >= m).astype(F32)
        lo = 1.0 - up
        selq = (((r & (2 * m - 1)) >= m) & (c == (r & ~(m - 1)) - 1)).astype(F32)
        selk = (((r & (2 * m - 1)) < m) & (c == (r & ~(m - 1)) + m - 1)).astype(F32)
        pm = (((r & ~(2 * m - 1)) == (c & ~(2 * m - 1))) & ((r & (2 * m - 1)) >= m) & ((c & (2 * m - 1)) < m)).astype(F32)
        lv.append((up, lo, selq, selq.T, selk, selk.T, jnp.concatenate([pm] * NH, axis=0)))
    hm4 = lambda n: (((_iota((NH, 1, n), 2) & (GW - 1)) >> 6) == _iota((NH, 1, n), 0)).astype(F32)
    return dict(tri=tri, trit=tri.T, rr=rr, lv=lv, bd=_bdmask(), bd64=_bdmask() * (1.0 / HD),
                hm4={GW: hm4(GW), 3 * GW: hm4(3 * GW)})


def _hgrn_chunk_fn(hq, hf, hi, lb, wn, st, cs):
    q = _silu(hq)
    log_lb = jnp.log(jnp.maximum(lb, LB_FLOOR))
    a, bb = log_lb, jnp.log1p(-lb) + _logsig(hf)
    g = jnp.maximum(a, bb) + jnp.log1p(jnp.exp(-jnp.abs(a - bb)))
    k = (1.0 - lb) * _sigmoid(-hf)
    v = hi
    rr = cs["rr"]
    b = xl(cs["tri"], cs["trit"], g)
    row_of = lambda n: jnp.sum(b * (rr == n).astype(F32), axis=0, keepdims=True)
    o = mm_nt(q * jnp.exp(b), st)
    qs, ks = [], []
    for ib in (1, 2, 3):
        ref = row_of(16 * ib - 1)
        inq = ((rr >= 16 * ib) & (rr < 16 * ib + 16)).astype(F32)
        ink = (rr < 16 * ib).astype(F32)
        qs.append(q * jnp.exp((b - ref) * inq) * inq)
        ks.append(k * jnp.exp((ref - b) * ink) * ink)
    qcat, kcat = jnp.concatenate(qs, axis=1), jnp.concatenate(ks, axis=1)
    lvl = []
    for up, lo, selq, selqt, selk, selkt, pm in cs["lv"]:
        qe = q * jnp.exp((b - xl(selq, selqt, b)) * up) * up
        ke = k * jnp.exp((xl(selk, selkt, b) - b) * lo) * lo
        lvl.append((qe, ke, pm))
    stack = lambda x: (x[None] * cs["hm4"][x.shape[1]]).reshape(NH * CH, x.shape[1])
    a_all = mm_nt(stack(qcat), kcat)
    for qe, ke, pm4 in lvl:
        a_all = a_all + mm_nt(stack(qe), ke) * pm4
    o = o + jnp.sum(mm(a_all, v).reshape(NH, CH, GW) * cs["hm4"][GW], axis=0)
    o = o + xr(q * k, cs["bd"], cs["bd"]) * v
    b_last = row_of(CH - 1)
    st_new = st * jnp.exp(b_last) + mm_tn(v, k * jnp.exp(b_last - b)) * cs["bd"]
    return _headrms(o, wn, cs["bd64"]), st_new


def hgrn_fwd(pc, lb, wn, bl, s, layer, tag):
    nc = s // CH

    def body(q_ref, f_ref, i_ref, lb_ref, wn_ref, o_ref, st_ref, st):
        @pl.when(pl.program_id(0) == 0)
        def _():
            st[...] = jnp.zeros_like(st)

        cs = _hgrn_consts()
        for b in range(bl):
            st_ref[b] = st[b]
            o, st_new = _hgrn_chunk_fn(q_ref[b], f_ref[b], i_ref[b], lb_ref[...], wn_ref[...], st[b], cs)
            o_ref[b] = o
            st[b] = st_new

    tok = lambda j: pl.BlockSpec((bl, CH, GW), lambda c: (0, c, j))
    par = pl.BlockSpec((None, 1, GW), lambda c: (layer, 0, 0))
    pc3 = pc.reshape(bl, s, 3 * GW)
    o, states = _call(
        body, f"hgrn_fwd_{tag}", (nc,), [tok(0), tok(1), tok(2), par, par],
        [tok(0), pl.BlockSpec((bl, None, GW, GW), lambda c: (0, c, 0, 0))],
        [_sds((bl, s, GW)), _sds((bl, nc, GW, GW))],
        [pltpu.VMEM((bl, GW, GW), F32)],
    )(pc3, pc3, pc3, lb, wn)
    return o.reshape(bl * s, GW), states


def hgrn_bwd(pc, lb, wn, states, do, bl, s, layer, tag):
    nc = s // CH

    def body(q_ref, f_ref, i_ref, lb_ref, wn_ref, st_ref, do_ref, dc_ref, dlb_ref, dwn_ref, dst):
        c = pl.program_id(0)

        @pl.when(c == 0)
        def _():
            dst[...] = jnp.zeros_like(dst)

        cs = _hgrn_consts()
        dlb_sum = dwn_sum = None
        for b in range(bl):
            _, vjp = jax.vjp(lambda *a: _hgrn_chunk_fn(*a, cs), q_ref[b], f_ref[b], i_ref[b], lb_ref[...],
                             wn_ref[...], st_ref[b])
            dq, df, di, dlb, dwn, dst_in = vjp((do_ref[b], dst[b]))
            dst[b] = dst_in
            dc_ref[b, :, 0:GW] = dq
            dc_ref[b, :, GW:2 * GW] = df
            dc_ref[b, :, 2 * GW:3 * GW] = di
            dlb_sum = dlb if dlb_sum is None else dlb_sum + dlb
            dwn_sum = dwn if dwn_sum is None else dwn_sum + dwn
        _acc(dlb_ref, dlb_sum, c == 0)
        _acc(dwn_ref, dwn_sum, c == 0)

    tok = lambda j: pl.BlockSpec((bl, CH, GW), lambda c: (0, nc - 1 - c, j))
    par = pl.BlockSpec((None, 1, GW), lambda c: (layer, 0, 0))
    acc = pl.BlockSpec((1, GW), lambda c: (0, 0))
    pc3 = pc.reshape(bl, s, 3 * GW)
    dc, dlb, dwn = _call(
        body, f"hgrn_bwd_{tag}", (nc,),
        [tok(0), tok(1), tok(2), par, par, pl.BlockSpec((bl, None, GW, GW), lambda c: (0, nc - 1 - c, 0, 0)), tok(0)],
        [pl.BlockSpec((bl, CH, 3 * GW), lambda c: (0, nc - 1 - c, 0)), acc, acc],
        [_sds((bl, s, 3 * GW)), _sds((1, GW)), _sds((1, GW))],
        [pltpu.VMEM((bl, GW, GW), F32)],
    )(pc3, pc3, pc3, lb, wn, states, do.reshape(bl, s, GW))
    return dc.reshape(bl * s, 3 * GW), dlb, dwn


def _shift_rows(x, k, up):
    n = x.shape[0]
    rr = _iota((n, 1), 0)
    if up:
        return jnp.where(rr < n - k, pltpu.roll(x, n - k, 0), 0.0)
    return jnp.where(rr >= k, pltpu.roll(x, k, 0), 0.0)


def _window_sums(x, up):
    s2 = x + _shift_rows(x, 1, up)
    s4 = s2 + _shift_rows(s2, 2, up)
    s8 = s4 + _shift_rows(s4, 4, up)
    s16 = s8 + _shift_rows(s8, 8, up)
    return s2, s4, s8, s16


def _pool_div(n):
    pos = (_iota((n, 1), 0) + 1).astype(F32)
    return [jnp.minimum(pos, float(w)) for w in (2, 4, 8, 16)]


def _pool_mix(sums, scaled):
    out = None
    for gi, sw in enumerate(sums):
        part = (sw if scaled is None else sw / scaled[gi]) * _hmask(gi)
        out = part if out is None else out + part
    return out


def pool_fwd(pd, wbd, scale, bl, s, layer, tag):
    def body(u_ref, w_ref, sc_ref, o_ref):
        u = u_ref[...]
        pm = _pool_mix(_window_sums(u, False), _pool_div(s)) - u
        o_ref[...] = _dg(_cast(pm), _cast(w_ref[...]), 1, 0) * sc_ref[...]

    seq = pl.BlockSpec((s, GW), lambda b: (b, 0))
    return _call(
        body, f"pool_fwd_{tag}", (bl,),
        [seq, pl.BlockSpec((None, GW, GW), lambda b: (layer, 0, 0)), pl.BlockSpec((None, 1, GW), lambda b: (layer, 0, 0))],
        seq, _sds((bl * s, GW)),
    )(pd, wbd, scale)


def pool_bwd(pd, wbd, scale, do, bl, s, layer, tag):
    def body(u_ref, w_ref, sc_ref, do_ref, du_ref, dw_ref, dsc_ref):
        first = pl.program_id(0) == 0
        u, do = u_ref[...], do_ref[...]
        div = _pool_div(s)
        pm = _pool_mix(_window_sums(u, False), div) - u
        ypre = _dg(_cast(pm), _cast(w_ref[...]), 1, 0)
        dys = do * sc_ref[...]
        _acc(dsc_ref, jnp.sum(do * ypre, axis=0, keepdims=True), first)
        _acc(dw_ref, _dg(_cast(pm), _cast(dys), 0, 0), first)
        dpm = _dg(_cast(dys), _cast(w_ref[...]), 1, 1)
        dsc = [dpm / d for d in div]
        adj = None
        for gi in range(4):
            part = _window_sums(dsc[gi] * _hmask(gi), True)[gi]
            adj = part if adj is None else adj + part
        du_ref[...] = adj - dpm

    seq = pl.BlockSpec((s, GW), lambda b: (b, 0))
    return _call(
        body, f"pool_bwd_{tag}", (bl,),
        [seq, pl.BlockSpec((None, GW, GW), lambda b: (layer, 0, 0)), pl.BlockSpec((None, 1, GW), lambda b: (layer, 0, 0)), seq],
        [seq, pl.BlockSpec((GW, GW), lambda b: (0, 0)), pl.BlockSpec((1, GW), lambda b: (0, 0))],
        [_sds((bl * s, GW)), _sds((GW, GW)), _sds((1, GW))],
    )(pd, wbd, scale, do)


def _mem_prep_fn(mem, g, wk, wv, kw, bd64):
    mn = _rms(mem, g)
    return _headrms(mm(mn, wk), kw, bd64), mm(mn, wv)


def mem_prep_fwd(mem2, g, wkv, kw, bl, layer, tag):
    def body(m_ref, g_ref, wk_ref, wv_ref, kw_ref, k_ref, v_ref):
        k, v = _mem_prep_fn(m_ref[...], g_ref[...], wk_ref[...], wv_ref[...], kw_ref[...], _bdmask() * (1.0 / HD))
        k_ref[...] = k
        v_ref[...] = v

    blk = pl.BlockSpec((N_MEM, GW), lambda b: (b, 0))
    return _call(
        body, f"mem_prep_fwd_{tag}", (bl,),
        [pl.BlockSpec((N_MEM, D_MODEL), lambda b: (b, 0)), pl.BlockSpec((None, 1, D_MODEL), lambda b: (layer, 0, 0)),
         pl.BlockSpec((None, D_MODEL, GW), lambda b: (layer, 0, 0)), pl.BlockSpec((None, D_MODEL, GW), lambda b: (layer, 0, 1)),
         pl.BlockSpec((None, 1, GW), lambda b: (layer, 0, 0))],
        [blk, blk], [_sds((bl * N_MEM, GW))] * 2,
    )(mem2, g, wkv, wkv, kw)


def mem_prep_bwd(mem2, g, wkv, kw, dk, dv, bl, layer, tag):
    def body(m_ref, g_ref, wk_ref, wv_ref, kw_ref, dk_ref, dv_ref, dwk_ref, dwv_ref, dg_ref, dkw_ref):
        first = pl.program_id(0) == 0
        bd64 = _bdmask() * (1.0 / HD)
        _, vjp = jax.vjp(lambda g_, wk, wv, kw_: _mem_prep_fn(m_ref[...], g_, wk, wv, kw_, bd64),
                         g_ref[...], wk_ref[...].astype(F32), wv_ref[...].astype(F32), kw_ref[...])
        dg, dwk, dwv, dkw = vjp((dk_ref[...], dv_ref[...]))
        _acc(dwk_ref, dwk, first)
        _acc(dwv_ref, dwv, first)
        _acc(dg_ref, dg, first)
        _acc(dkw_ref, dkw, first)

    blk = pl.BlockSpec((N_MEM, GW), lambda b: (b, 0))
    return _call(
        body, f"mem_prep_bwd_{tag}", (bl,),
        [pl.BlockSpec((N_MEM, D_MODEL), lambda b: (b, 0)), pl.BlockSpec((None, 1, D_MODEL), lambda b: (layer, 0, 0)),
         pl.BlockSpec((None, D_MODEL, GW), lambda b: (layer, 0, 0)), pl.BlockSpec((None, D_MODEL, GW), lambda b: (layer, 0, 1)),
         pl.BlockSpec((None, 1, GW), lambda b: (layer, 0, 0)), blk, blk],
        [pl.BlockSpec((D_MODEL, GW), lambda b: (0, 0)), pl.BlockSpec((D_MODEL, GW), lambda b: (0, 0)),
         pl.BlockSpec((1, D_MODEL), lambda b: (0, 0)), pl.BlockSpec((1, GW), lambda b: (0, 0))],
        [_sds((D_MODEL, GW)), _sds((D_MODEL, GW)), _sds((1, D_MODEL)), _sds((1, GW))],
    )(mem2, g, wkv, wkv, kw, dk, dv)


def _mem_attn_fn(mq, qw, k, v, bd64):
    qn = _headrms(mq, qw, bd64)
    out = None
    for h in range(NH):
        hm = _hmask(h)
        lg = mm_nt(qn * hm, k) * SCALE
        e = jnp.exp(lg - lax.stop_gradient(jnp.max(lg, axis=-1, keepdims=True)))
        p = e / jnp.sum(e, axis=-1, keepdims=True)
        part = mm(p, v) * hm
        out = part if out is None else out + part
    return out


def mem_attn_fwd(pe, qw, k, v, bl, s, layer, tag):
    nq = s // TQ

    def body(q_ref, qw_ref, k_ref, v_ref, o_ref):
        o_ref[...] = _mem_attn_fn(q_ref[...], qw_ref[...], k_ref[...], v_ref[...], _bdmask() * (1.0 / HD))

    tok = pl.BlockSpec((TQ, GW), lambda b, i: (b * nq + i, 0))
    kv = pl.BlockSpec((N_MEM, GW), lambda b, i: (b, 0))
    return _call(
        body, f"mem_attn_fwd_{tag}", (bl, nq), [tok, pl.BlockSpec((None, 1, GW), lambda b, i: (layer, 0, 0)), kv, kv],
        tok, _sds((bl * s, GW)),
    )(pe, qw, k, v)


def mem_attn_bwd(pe, qw, k, v, do, bl, s, layer, tag):
    nq = s // TQ

    def body(q_ref, qw_ref, k_ref, v_ref, do_ref, dq_ref, dk_ref, dv_ref, dqw_ref):
        i = pl.program_id(1)
        bd64 = _bdmask() * (1.0 / HD)
        _, vjp = jax.vjp(lambda *a: _mem_attn_fn(*a, bd64), q_ref[...], qw_ref[...], k_ref[...], v_ref[...])
        dq, dqw, dk, dv = vjp(do_ref[...])
        dq_ref[...] = dq
        _acc(dk_ref, dk, i == 0)
        _acc(dv_ref, dv, i == 0)
        _acc(dqw_ref, dqw, jnp.logical_and(pl.program_id(0) == 0, i == 0))

    tok = pl.BlockSpec((TQ, GW), lambda b, i: (b * nq + i, 0))
    kv = pl.BlockSpec((N_MEM, GW), lambda b, i: (b, 0))
    return _call(
        body, f"mem_attn_bwd_{tag}", (bl, nq),
        [tok, pl.BlockSpec((None, 1, GW), lambda b, i: (layer, 0, 0)), kv, kv, tok],
        [tok, kv, kv, pl.BlockSpec((1, GW), lambda b, i: (0, 0))],
        [_sds((bl * s, GW)), _sds((bl * N_MEM, GW)), _sds((bl * N_MEM, GW)), _sds((1, GW))],
    )(pe, qw, k, v, do)


def _gate_out_fn(outs, gates, wparts):
    y = None
    for o, g, w in zip(outs, gates, wparts):
        part = mm(o * _silu(g), w)
        y = part if y is None else y + part
    return y


def outproj_fwd(x2, outs, pg, wout, layer, tag):
    t = x2.shape[0]

    def body(x_ref, oa, ob, oc, od, oe, g_ref, w_ref, y_ref):
        outs_ = [r[...] for r in (oa, ob, oc, od, oe)]
        gates = [g_ref[:, j * GW:(j + 1) * GW] for j in range(5)]
        wparts = [w_ref[j * GW:(j + 1) * GW, :] for j in range(5)]
        y_ref[...] = x_ref[...] + _gate_out_fn(outs_, gates, wparts)

    tok = pl.BlockSpec((TQ, GW), lambda i: (i, 0))
    big = pl.BlockSpec((TQ, D_MODEL), lambda i: (i, 0))
    return _call(
        body, f"outproj_fwd_{tag}", (t // TQ,),
        [big] + [tok] * 5 + [pl.BlockSpec((TQ, D_MIX), lambda i: (i, 0)),
                            pl.BlockSpec((None, D_MIX, D_MODEL), lambda i: (layer, 0, 0))],
        big, _sds((t, D_MODEL)),
    )(x2, *outs, pg, wout)


def outproj_bwd(outs, pg, wout, dy, layer, tag):
    t = dy.shape[0]

    def body(oa, ob, oc, od, oe, g_ref, w_ref, dy_ref, da, db, dc, dd, de, dg_ref, dw_ref):
        outs_ = [r[...] for r in (oa, ob, oc, od, oe)]
        gates = [g_ref[:, j * GW:(j + 1) * GW] for j in range(5)]
        wparts = [w_ref[j * GW:(j + 1) * GW, :].astype(F32) for j in range(5)]
        _, vjp = jax.vjp(_gate_out_fn, outs_, gates, wparts)
        douts, dgates, dws = vjp(dy_ref[...])
        for r, val in zip((da, db, dc, dd, de), douts):
            r[...] = val
        first = pl.program_id(0) == 0
        for j in range(5):
            dg_ref[:, j * GW:(j + 1) * GW] = dgates[j]

        @pl.when(first)
        def _():
            for j in range(5):
                dw_ref[j * GW:(j + 1) * GW, :] = dws[j]

        @pl.when(jnp.logical_not(first))
        def _():
            for j in range(5):
                dw_ref[j * GW:(j + 1) * GW, :] += dws[j]

    tok = pl.BlockSpec((TQ, GW), lambda i: (i, 0))
    return _call(
        body, f"outproj_bwd_{tag}", (t // TQ,),
        [tok] * 5 + [pl.BlockSpec((TQ, D_MIX), lambda i: (i, 0)), pl.BlockSpec((None, D_MIX, D_MODEL), lambda i: (layer, 0, 0)),
                     pl.BlockSpec((TQ, D_MODEL), lambda i: (i, 0))],
        [tok] * 5 + [pl.BlockSpec((TQ, D_MIX), lambda i: (i, 0)), pl.BlockSpec((D_MIX, D_MODEL), lambda i: (0, 0))],
        [_sds((t, GW))] * 5 + [_sds((t, D_MIX)), _sds((D_MIX, D_MODEL))],
    )(*outs, pg, wout, dy)


def loss_head(y, tgt):
    t = y.shape[0]

    def body(y_ref, t_ref, l_ref, dy_ref):
        diff = y_ref[...] - t_ref[...]
        dy_ref[...] = diff * (1.0 / D_MODEL)
        part = 0.5 * jnp.sum(jnp.sum(diff * diff, axis=-1, keepdims=True) * (1.0 / D_MODEL), axis=0, keepdims=True)
        _acc(l_ref, jnp.broadcast_to(part, (8, 128)), pl.program_id(0) == 0)

    big = pl.BlockSpec((TQ, D_MODEL), lambda i: (i, 0))
    return _call(body, "loss_head", (t // TQ,), [big, big], [pl.BlockSpec((8, 128), lambda i: (0, 0)), big],
                 [_sds((8, 128)), _sds((t, D_MODEL))])(y, tgt)


def layer_fwd(x2, mem2, p, layer, bl, s):
    tag = f"l{layer}"
    ht, pa, pb, pc, pd, pe, pg, pf = inproj_fwd(x2, p["norm_g"], p["w_in"], layer, tag)
    qn, kn, vb, cq, ck = fox_prep_fwd(pa, pf, p["fox_q_norm"], p["fox_k_norm"], p["fox_f_bias"], bl, s, layer, tag)
    oa, lse = fox_attn_fwd(qn, kn, vb, cq, ck, bl, s, tag)
    ob = sb_attn_fwd(pb, bl, s, tag)
    oc, states = hgrn_fwd(pc, p["lb"], p["hgrn_out_norm"], bl, s, layer, tag)
    od = pool_fwd(pd, p["pool_wbd"], p["pool_scale"], bl, s, layer, tag)
    mk, mv = mem_prep_fwd(mem2, p["mem_norm_g"], p["mem_w_kv"], p["mem_k_norm"], bl, layer, tag)
    oe = mem_attn_fwd(pe, p["mem_q_norm"], mk, mv, bl, s, layer, tag)
    y = outproj_fwd(x2, (oa, ob, oc, od, oe), pg, p["w_out"], layer, tag)
    saved = dict(x2=x2, ht=ht, pa=pa, pb=pb, pc=pc, pd=pd, pe=pe, pg=pg, pf=pf, qn=qn, kn=kn, vb=vb, cq=cq, ck=ck,
                 oa=oa, lse=lse, ob=ob, oc=oc, states=states, od=od, mk=mk, mv=mv, oe=oe)
    return y, saved


def layer_bwd(dy, mem2, p, sv, layer, bl, s):
    tag = f"l{layer}"
    (doa, dob, doc, dod, doe, dg_gates, dwout) = outproj_bwd((sv["oa"], sv["ob"], sv["oc"], sv["od"], sv["oe"]), sv["pg"],
                                                              p["w_out"], dy, layer, tag)
    dqn, dkn, dv, dck = fox_attn_bwd(sv["qn"], sv["kn"], sv["vb"], sv["cq"], sv["ck"], sv["lse"], doa, bl, s, tag)
    d_a, d_f, dqw, dkw, dbias = fox_prep_bwd(sv["pa"], sv["pf"], p["fox_q_norm"], p["fox_k_norm"], p["fox_f_bias"], sv["cq"],
                                             dqn, dkn, dv, dck, bl, s, layer, tag)
    dsq, dsk, dsv = sb_attn_bwd(sv["pb"], dob, bl, s, tag)
    d_c, dlb, dwn = hgrn_bwd(sv["pc"], p["lb"], p["hgrn_out_norm"], sv["states"], doc, bl, s, layer, tag)
    d_d, dwbd, dpscale = pool_bwd(sv["pd"], p["pool_wbd"], p["pool_scale"], dod, bl, s, layer, tag)
    d_e, dmk, dmv, dmqw = mem_attn_bwd(sv["pe"], p["mem_q_norm"], sv["mk"], sv["mv"], doe, bl, s, layer, tag)
    dwk, dwv, dmg, dmkw = mem_prep_bwd(mem2, p["mem_norm_g"], p["mem_w_kv"], p["mem_k_norm"], dmk, dmv, bl, layer, tag)
    dpieces = (d_a, dsq, dsk, dsv, d_c, d_d, d_e, dg_gates, d_f)
    dx, dng = inproj_bwd_dx(sv["x2"], p["norm_g"], p["w_in"], dy, dpieces, layer, tag)
    dwin = jnp.concatenate([matmul_acc(sv["ht"], dp, f"{tag}_{nm}") for (nm, _, _), dp in zip(BWD_PIECES, dpieces)], axis=1)
    grads = dict(norm_g=dng, w_in=dwin, fox_f_bias=dbias, fox_q_norm=dqw, fox_k_norm=dkw, lb=dlb, hgrn_out_norm=dwn,
                 pool_wbd=dwbd, pool_scale=dpscale, mem_norm_g=dmg, mem_w_kv=jnp.concatenate([dwk, dwv], axis=1),
                 mem_q_norm=dmqw, mem_k_norm=dmkw, w_out=dwout)
    return dx, grads


def _tile4(w):
    return jnp.tile(w, (1, NH))[:, None, :]


def prepare_params(norm_g, w_in_p, fox_f_bias, fox_q_norm, fox_k_norm, hgrn_lb_logits, hgrn_out_norm, pool_w, pool_scale,
                   mem_norm_g, mem_w_kv, mem_q_norm, mem_k_norm, w_out):
    p1 = jax.nn.sigmoid(hgrn_lb_logits[1] - hgrn_lb_logits[0])
    lb = jnp.stack([jnp.zeros_like(p1), jnp.clip(p1, 0.0, 1.0 - 1e-6)])
    eye = jnp.eye(4, dtype=F32)
    wbd = jnp.einsum("lgcd,gh->lgchd", pool_w, eye).reshape(2, GW, GW)
    return dict(norm_g=norm_g[:, None, :], w_in=w_in_p, fox_f_bias=jnp.pad(fox_f_bias, ((0, 0), (0, 124)))[:, None, :],
                fox_q_norm=_tile4(fox_q_norm), fox_k_norm=_tile4(fox_k_norm), lb=lb[:, None, :],
                hgrn_out_norm=hgrn_out_norm[:, None, :], pool_wbd=wbd, pool_scale=pool_scale[:, None, :],
                mem_norm_g=mem_norm_g[:, None, :], mem_w_kv=mem_w_kv, mem_q_norm=_tile4(mem_q_norm),
                mem_k_norm=_tile4(mem_k_norm), w_out=w_out)


def local_step(x, mem, tgt, p):
    bl, s, _ = x.shape
    x2, mem2, tgt2 = x.reshape(bl * s, D_MODEL), mem.reshape(bl * N_MEM, D_MODEL), tgt.reshape(bl * s, D_MODEL)
    y0, sv0 = layer_fwd(x2, mem2, p, 0, bl, s)
    y1, sv1 = layer_fwd(y0, mem2, p, 1, bl, s)
    lpart, dy = loss_head(y1, tgt2)
    dx1, g1 = layer_bwd(dy, mem2, p, sv1, 1, bl, s)
    dx0, g0 = layer_bwd(dx1, mem2, p, sv0, 0, bl, s)
    return lpart[0, 0], dx0.reshape(bl, s, D_MODEL), g0, g1


_ANY = pl.BlockSpec(memory_space=pl.ANY)


def _me_and_peers():
    x, y, c = lax.axis_index("x"), lax.axis_index("y"), lax.axis_index("c")
    peers = []
    for k in range(1, N_DEV):
        px = 1 - x if (k >> 2) & 1 else x
        py = 1 - y if (k >> 1) & 1 else y
        pc = 1 - c if k & 1 else c
        peers.append(((px, py, pc), 4 * px + 2 * py + pc))
    return 4 * x + 2 * y + c, peers


def all_gather_rows(xs, tag):
    nl, r, c = xs.shape

    def body(x_ref, o_ref, send_sems, recv_sems, local_sem):
        x, y, cc = lax.axis_index("x"), lax.axis_index("y"), lax.axis_index("c")
        me, sibling = (x, y, cc), (x, y, 1 - cc)
        chips = [(1 - x, y), (x, 1 - y), (1 - x, 1 - y)]

        def rows(px, py, pc):
            return o_ref.at[:, pl.ds((4 * px + 2 * py + pc) * r, r), :]

        def copy(k, block, to, src=None):
            return pltpu.make_async_remote_copy(src_ref=rows(*block) if src is None else src, dst_ref=rows(*block),
                                                send_sem=send_sems.at[k], recv_sem=recv_sems.at[k], device_id=to,
                                                device_id_type=pl.DeviceIdType.MESH)

        mine = pltpu.make_async_copy(x_ref, rows(*me), local_sem)
        mine.start()
        first = [copy(0, me, sibling, src=x_ref)] + [copy(1 + j, me, (*chip, cc), src=x_ref) for j, chip in enumerate(chips)]
        for cp in first:
            cp.start()
        passed = [copy(4 + j, (*chip, cc), sibling) for j, chip in enumerate(chips)]
        for j, chip in enumerate(chips):
            copy(1 + j, (*chip, cc), me).wait_recv()
            passed[j].start()
        copy(0, sibling, me).wait_recv()
        for j, chip in enumerate(chips):
            copy(4 + j, (*chip, 1 - cc), me).wait_recv()
        for cp in first + passed:
            cp.wait_send()
        mine.wait()

    return pl.pallas_call(
        body, name=f"all_gather_{tag}", in_specs=[_ANY], out_specs=_ANY, out_shape=_sds((nl, N_DEV * r, c), xs.dtype),
        scratch_shapes=[pltpu.SemaphoreType.DMA((N_DEV - 1,)), pltpu.SemaphoreType.DMA((N_DEV - 1,)), pltpu.SemaphoreType.DMA],
    )(xs)


def exchange_cores(part, tag):
    nl, _, _, r, c = part.shape

    def body(p_ref, theirs_ref, send_sems, recv_sems):
        x, y, cc = lax.axis_index("x"), lax.axis_index("y"), lax.axis_index("c")
        copies = []
        for l in range(nl):
            for q in range(4):
                k = l * 4 + q
                copies.append(pltpu.make_async_remote_copy(
                    src_ref=p_ref.at[l, q, pl.ds(1 - cc, 1)], dst_ref=theirs_ref.at[l, q], send_sem=send_sems.at[k],
                    recv_sem=recv_sems.at[k], device_id=(x, y, 1 - cc), device_id_type=pl.DeviceIdType.MESH))
        for cp in copies:
            cp.start()
        for cp in copies:
            cp.wait()

    nsem = pltpu.SemaphoreType.DMA((nl * 4,))
    return pl.pallas_call(
        body, name=f"exchange_cores_{tag}", in_specs=[_ANY], out_specs=_ANY, out_shape=_sds((nl, 4, 1, r, c), part.dtype),
        scratch_shapes=[nsem, nsem],
    )(part)


def add_core_halves(part5, theirs, core, tag):
    nl, _, _, r, c = part5.shape
    tr = 64 if r % 64 == 0 else 32

    def body(core_ref, a_ref, b_ref, o_ref):
        o_ref[...] = _cast(a_ref[...] + b_ref[...])

    blk = lambda which: pl.BlockSpec((None, None, None, tr, c), lambda l, q, i, cref: (l, q, cref[0] if which else 0, i, 0))
    return pl.pallas_call(
        body, name=f"add_core_halves_{tag}", out_shape=_sds((nl, 4, 1, r, c), _MMT),
        grid_spec=pltpu.PrefetchScalarGridSpec(num_scalar_prefetch=1, grid=(nl, 4, r // tr), in_specs=[blk(True), blk(False)],
                                               out_specs=blk(False)),
        compiler_params=pltpu.CompilerParams(dimension_semantics=("arbitrary",) * 3, vmem_limit_bytes=VMEM_LIMIT_BYTES),
    )(core, part5, theirs)


def exchange_chips(s4, tag):
    nl, _, _, r, c = s4.shape

    def body(s_ref, o_ref, send_sems, recv_sems, local_sem):
        x, y, cc = lax.axis_index("x"), lax.axis_index("y"), lax.axis_index("c")
        local = pltpu.make_async_copy(s_ref.at[:, pl.ds(2 * x + y, 1)], o_ref.at[0], local_sem)
        local.start()
        copies = []
        for k in range(1, 4):
            px = 1 - x if (k >> 1) & 1 else x
            py = 1 - y if k & 1 else y
            copies.append(pltpu.make_async_remote_copy(
                src_ref=s_ref.at[:, pl.ds(2 * px + py, 1)], dst_ref=o_ref.at[k], send_sem=send_sems.at[k - 1],
                recv_sem=recv_sems.at[k - 1], device_id=(px, py, cc), device_id_type=pl.DeviceIdType.MESH))
        for cp in copies:
            cp.start()
        for cp in copies:
            cp.wait()
        local.wait()

    return pl.pallas_call(
        body, name=f"exchange_chips_{tag}", in_specs=[_ANY], out_specs=_ANY, out_shape=_sds((4, nl, 1, 1, r, c), s4.dtype),
        scratch_shapes=[pltpu.SemaphoreType.DMA((3,)), pltpu.SemaphoreType.DMA((3,)), pltpu.SemaphoreType.DMA],
    )(s4)


def exchange_row_blocks(part, tag):
    nl, r8, c = part.shape
    r = r8 // N_DEV

    def body(p_ref, o_ref, send_sems, recv_sems, local_sem):
        me, peers = _me_and_peers()
        rows = lambda idx: p_ref.at[:, pl.ds(idx * r, r), :]
        mine = pltpu.make_async_copy(rows(me), o_ref.at[0], local_sem)
        mine.start()
        copies = [pltpu.make_async_remote_copy(src_ref=rows(idx), dst_ref=o_ref.at[k + 1], send_sem=send_sems.at[k],
                                               recv_sem=recv_sems.at[k], device_id=dev, device_id_type=pl.DeviceIdType.MESH)
                  for k, (dev, idx) in enumerate(peers)]
        for cp in copies:
            cp.start()
        for cp in copies:
            cp.wait()
        mine.wait()

    return pl.pallas_call(
        body, name=f"exchange_{tag}", in_specs=[_ANY], out_specs=_ANY, out_shape=_sds((N_DEV, nl, r, c), part.dtype),
        scratch_shapes=[pltpu.SemaphoreType.DMA((N_DEV - 1,)), pltpu.SemaphoreType.DMA((N_DEV - 1,)), pltpu.SemaphoreType.DMA],
    )(part)


def _row_tile(rows):
    if rows <= 512 and rows % 64:
        return rows
    for t in (64, 40, 32, 16, 8):
        if rows % t == 0:
            return t
    return rows


def sum_slots(slots, tag):
    ns, rows, c = slots.shape
    tr = _row_tile(rows)

    def body(s_ref, o_ref):
        acc = s_ref[0].astype(F32)
        for k in range(1, ns):
            acc = acc + s_ref[k].astype(F32)
        o_ref[...] = acc

    return _call(body, f"sum_slots_{tag}", (rows // tr,), [pl.BlockSpec((ns, tr, c), lambda i: (0, i, 0))],
                 pl.BlockSpec((tr, c), lambda i: (i, 0)), _sds((rows, c)))(slots)


def _adamw(w, g, m, v):
    m = ADAM_B1 * m + (1.0 - ADAM_B1) * g
    v = ADAM_B2 * v + (1.0 - ADAM_B2) * (g * g)
    m_hat = m / (1.0 - ADAM_B1 ** ADAM_STEP)
    v_hat = v / (1.0 - ADAM_B2 ** ADAM_STEP)
    delta = -ADAM_LR * (m_hat / (jnp.sqrt(v_hat) + ADAM_EPS) + ADAM_WD * w)
    return delta, m, v


def adam_update(w, m, v, g, tag, slots=False):
    rows, c = w.shape
    tr = _row_tile(rows)
    ns = g.shape[0] if slots else 0

    def body(w_ref, m_ref, v_ref, g_ref, go_ref, d_ref, mo_ref, vo_ref):
        if slots:
            g = g_ref[0].astype(F32)
            for k in range(1, ns):
                g = g + g_ref[k].astype(F32)
        else:
            g = g_ref[...]
        d, mn, vn = _adamw(w_ref[...], g, m_ref[...], v_ref[...])
        go_ref[...] = g
        d_ref[...] = d
        mo_ref[...] = mn
        vo_ref[...] = vn

    blk = pl.BlockSpec((tr, c), lambda i: (i, 0))
    gspec = pl.BlockSpec((ns, tr, c), lambda i: (0, i, 0)) if slots else blk
    return _call(body, f"adam_{tag}", (rows // tr,), [blk, blk, blk, gspec], [blk] * 4, [_sds((rows, c))] * 4)(w, m, v, g)


_SMALL = (("norm_g", (2, 1024)), ("fox_f_bias", (2, 4)), ("fox_q_norm", (2, 64)), ("fox_k_norm", (2, 64)),
          ("hgrn_lb_logits", (2, 256)), ("hgrn_out_norm", (2, 256)), ("pool_w", (2, 4, 64, 64)), ("pool_scale", (2, 256)),
          ("mem_norm_g", (2, 1024)), ("mem_q_norm", (2, 64)), ("mem_k_norm", (2, 64)))
_SLAB_ROWS = 312


def pack_small(d):
    flat = jnp.concatenate([d[n].reshape(-1) for n, _ in _SMALL])
    return jnp.pad(flat, (0, _SLAB_ROWS * 128 - flat.shape[0])).reshape(_SLAB_ROWS, 128)


def unpack_small(slab):
    flat, out, off = slab.reshape(-1), {}, 0
    for n, shp in _SMALL:
        size = 1
        for e in shp:
            size *= e
        out[n] = flat[off:off + size].reshape(shp)
        off += size
    return out


def small_grads(g0, g1, lb_logits):
    st = lambda f: jnp.stack([f(g0), f(g1)])
    heads = lambda a: a.reshape(NH, HD).sum(0)
    p1 = jax.nn.sigmoid(lb_logits[1] - lb_logits[0])
    inside = (p1 > 0.0) & (p1 < 1.0 - 1e-6)
    dl1 = jnp.where(inside, g1["lb"][0] * p1 * (1.0 - p1), 0.0)
    diag = lambda a: jnp.stack([a.reshape(4, HD, 4, HD)[i, :, i, :] for i in range(4)])
    return dict(norm_g=st(lambda g: g["norm_g"][0]), fox_f_bias=st(lambda g: g["fox_f_bias"][0, :NH]),
                fox_q_norm=st(lambda g: heads(g["fox_q_norm"])), fox_k_norm=st(lambda g: heads(g["fox_k_norm"])),
                hgrn_lb_logits=jnp.stack([-dl1, dl1]), hgrn_out_norm=st(lambda g: g["hgrn_out_norm"][0]),
                pool_w=st(lambda g: diag(g["pool_wbd"])), pool_scale=st(lambda g: g["pool_scale"][0]),
                mem_norm_g=st(lambda g: g["mem_norm_g"][0]), mem_q_norm=st(lambda g: heads(g["mem_q_norm"])),
                mem_k_norm=st(lambda g: heads(g["mem_k_norm"])))


def kernel(x, mem, norm_g, w_in, fox_f_bias, fox_q_norm, fox_k_norm, hgrn_lb_logits, hgrn_out_norm, pool_w, pool_scale, mem_norm_g, mem_w_kv, mem_q_norm, mem_k_norm, w_out, loss_target, m_norm_g, m_w_in, m_fox_f_bias, m_fox_q_norm, m_fox_k_norm, m_hgrn_lb_logits, m_hgrn_out_norm, m_pool_w, m_pool_scale, m_mem_norm_g, m_mem_w_kv, m_mem_q_norm, m_mem_k_norm, m_w_out, v_norm_g, v_w_in, v_fox_f_bias, v_fox_q_norm, v_fox_k_norm, v_hgrn_lb_logits, v_hgrn_out_norm, v_pool_w, v_pool_scale, v_mem_norm_g, v_mem_w_kv, v_mem_q_norm, v_mem_k_norm, v_w_out):
    given = dict(norm_g=(norm_g, m_norm_g, v_norm_g), w_in=(w_in, m_w_in, v_w_in), fox_f_bias=(fox_f_bias, m_fox_f_bias, v_fox_f_bias),
                 fox_q_norm=(fox_q_norm, m_fox_q_norm, v_fox_q_norm), fox_k_norm=(fox_k_norm, m_fox_k_norm, v_fox_k_norm),
                 hgrn_lb_logits=(hgrn_lb_logits, m_hgrn_lb_logits, v_hgrn_lb_logits),
                 hgrn_out_norm=(hgrn_out_norm, m_hgrn_out_norm, v_hgrn_out_norm), pool_w=(pool_w, m_pool_w, v_pool_w),
                 pool_scale=(pool_scale, m_pool_scale, v_pool_scale), mem_norm_g=(mem_norm_g, m_mem_norm_g, v_mem_norm_g),
                 mem_w_kv=(mem_w_kv, m_mem_w_kv, v_mem_w_kv), mem_q_norm=(mem_q_norm, m_mem_q_norm, v_mem_q_norm),
                 mem_k_norm=(mem_k_norm, m_mem_k_norm, v_mem_k_norm), w_out=(w_out, m_w_out, v_w_out))
    order = ("norm_g", "w_in", "fox_f_bias", "fox_q_norm", "fox_k_norm", "hgrn_lb_logits", "hgrn_out_norm", "pool_w",
             "pool_scale", "mem_norm_g", "mem_w_kv", "mem_q_norm", "mem_k_norm", "w_out")

    w_in_full = all_gather_rows(_cast(permute_cols(w_in)), "w_in")
    w_out_full = all_gather_rows(_cast(w_out), "w_out")
    w_kv_full = all_gather_rows(_cast(mem_w_kv), "w_kv")
    p = prepare_params(norm_g, w_in_full, fox_f_bias, fox_q_norm, fox_k_norm, hgrn_lb_logits, hgrn_out_norm, pool_w,
                       pool_scale, mem_norm_g, w_kv_full, mem_q_norm, mem_k_norm, w_out_full)

    loss_part, grad_x, g0, g1 = local_step(x, mem, loss_target, p)
    loss = lax.psum(loss_part, ("x", "y", "c"))

    res = {}
    core = lax.axis_index("c").astype(jnp.int32).reshape(1)

    def sharded(name, g2, unperm=False, two_stage=True):
        w, m, v = given[name]
        nl, r, c = w.shape
        cp = g2.shape[-1]
        if two_stage:
            part5 = g2.reshape(nl, 4, 2, r, cp)
            s4 = add_core_halves(part5, exchange_cores(part5, name), core, name)
            slots = exchange_chips(s4, name).reshape(4, nl * r, cp)
        else:
            slots = exchange_row_blocks(g2, name).reshape(N_DEV, nl * r, cp)
        if unperm:
            g = sum_slots(slots, name).reshape(nl, r, cp)
            out = adam_update(w.reshape(nl * r, c), m.reshape(nl * r, c), v.reshape(nl * r, c),
                              unpermute_cols(g).reshape(nl * r, c), name)
        else:
            out = adam_update(w.reshape(nl * r, c), m.reshape(nl * r, c), v.reshape(nl * r, c), slots, name, slots=True)
        res[name] = tuple(o.reshape(nl, r, c) for o in out)

    sharded("w_in", jnp.stack([g0["w_in"], g1["w_in"]]), unperm=True)
    sharded("w_out", jnp.stack([g0["w_out"], g1["w_out"]]))
    sharded("mem_w_kv", _cast(jnp.stack([g0["mem_w_kv"], g1["mem_w_kv"]])), two_stage=False)

    gsmall = pack_small(small_grads(g0, g1, hgrn_lb_logits))
    gathered = all_gather_rows(gsmall[None], "small").reshape(N_DEV, _SLAB_ROWS, 128)
    slabs = adam_update(*[pack_small({n: given[n][j] for n, _ in _SMALL}) for j in range(3)], gathered, "small", slots=True)
    small = [unpack_small(sl) for sl in slabs]
    for n, _ in _SMALL:
        res[n] = tuple(small[j][n] for j in range(4))

    return (loss, grad_x, *[res[n][0] for n in order], *[res[n][1] for n in order], *[res[n][2] for n in order],
            *[res[n][3] for n in order])
```

```python
import functools

import jax
import jax.numpy as jnp
from jax import lax
from jax.experimental import pallas as pl
from jax.experimental.pallas import tpu as pltpu

F32 = jnp.float32
BF = jnp.bfloat16
_MMT = BF

D_MODEL = 1024
GW = 256
HD = 64
NH = 4
CH = 64
N_MEM = 256
D_IN = 4100
D_INP = 4224
D_MIX = 1280
EPS = 1e-6
NEG_BIG = -1e30
LB_FLOOR = 1e-30
SCALE = HD ** -0.5
TQ = 256
N_DEV = 8
VMEM_LIMIT_BYTES = 56 * 1024 * 1024

ADAM_LR = 0.001
ADAM_B1 = 0.9
ADAM_B2 = 0.999
ADAM_EPS = 1e-08
ADAM_WD = 0.01
ADAM_STEP = 10

PIECES = (("A", 0, 768), ("B", 768, 768), ("C", 1536, 768), ("D", 2304, 256), ("E", 2560, 256),
          ("G", 2816, 1280), ("F", 4096, 128))
BWD_PIECES = (("A", 0, 768), ("Bq", 768, 256), ("Bk", 1024, 256), ("Bv", 1280, 256), ("C", 1536, 768),
              ("D", 2304, 256), ("E", 2560, 256), ("G", 2816, 1280), ("F", 4096, 128))
_ORIG = dict(fq=(0, 256), fk=(256, 512), fv=(512, 768), fg=(768, 1024), ff=(1024, 1028), sq=(1028, 1284),
             sk=(1284, 1540), sv=(1540, 1796), sg=(1796, 2052), hq=(2052, 2308), hf=(2308, 2564),
             hi=(2564, 2820), hg=(2820, 3076), pv=(3076, 3332), pg=(3332, 3588), mq=(3588, 3844), mg=(3844, 4100))
_PERM_ORDER = ("fq", "fk", "fv", "sq", "sk", "sv", "hq", "hf", "hi", "pv", "mq", "fg", "sg", "hg", "pg", "mg", "ff")
_ORIG_ORDER = ("fq", "fk", "fv", "fg", "ff", "sq", "sk", "sv", "sg", "hq", "hf", "hi", "hg", "pv", "pg", "mq", "mg")


def permute_cols(w):
    parts = [w[..., _ORIG[n][0]:_ORIG[n][1]] for n in _PERM_ORDER]
    parts.append(jnp.zeros(w.shape[:-1] + (D_INP - D_IN,), w.dtype))
    return jnp.concatenate(parts, axis=-1)


def unpermute_cols(g):
    start, off = {}, 0
    for n in _PERM_ORDER:
        start[n] = off
        off += _ORIG[n][1] - _ORIG[n][0]
    return jnp.concatenate([g[..., start[n]:start[n] + _ORIG[n][1] - _ORIG[n][0]] for n in _ORIG_ORDER], axis=-1)


def _cast(a):
    return a.astype(_MMT)


def _dg(a, b, ca, cb):
    return lax.dot_general(a, b, (((ca,), (cb,)), ((), ())), preferred_element_type=F32)


@jax.custom_vjp
def mm(a, b):
    return _dg(_cast(a), _cast(b), 1, 0)


@jax.custom_vjp
def mm_nt(a, b):
    return _dg(_cast(a), _cast(b), 1, 1)


@jax.custom_vjp
def mm_tn(a, b):
    return _dg(_cast(a), _cast(b), 0, 0)


mm.defvjp(lambda a, b: (mm(a, b), (a, b)),
          lambda r, g: (mm_nt(g, r[1]).astype(r[0].dtype), mm_tn(r[0], g).astype(r[1].dtype)))
mm_nt.defvjp(lambda a, b: (mm_nt(a, b), (a, b)),
             lambda r, g: (mm(g, r[1]).astype(r[0].dtype), mm_tn(g, r[0]).astype(r[1].dtype)))
mm_tn.defvjp(lambda a, b: (mm_tn(a, b), (a, b)),
             lambda r, g: (mm_nt(r[1], g).astype(r[0].dtype), mm(r[0], g).astype(r[1].dtype)))


def _split(a):
    hi = a.astype(_MMT)
    lo = (a - hi.astype(F32)).astype(_MMT)
    return hi, lo


@jax.custom_vjp
def xr(a, c, ct):
    hi, lo = _split(a)
    cc = _cast(c)
    return _dg(hi, cc, 1, 0) + _dg(lo, cc, 1, 0)


@jax.custom_vjp
def xl(c, ct, a):
    hi, lo = _split(a)
    cc = _cast(c)
    return _dg(cc, hi, 1, 0) + _dg(cc, lo, 1, 0)


xr.defvjp(lambda a, c, ct: (xr(a, c, ct), (c, ct)),
          lambda r, g: (xr(g, r[1], r[0]), jnp.zeros_like(r[0]), jnp.zeros_like(r[1])))
xl.defvjp(lambda c, ct, a: (xl(c, ct, a), (c, ct)),
          lambda r, g: (jnp.zeros_like(r[0]), jnp.zeros_like(r[1]), xl(r[1], r[0], g)))


def _iota(shape, dim):
    return lax.broadcasted_iota(jnp.int32, shape, dim)


def _hmask(h, n=GW):
    lane = _iota((1, n), 1)
    return ((lane >= h * HD) & (lane < (h + 1) * HD)).astype(F32)


def _bdmask(n=GW):
    return ((_iota((n, n), 0) >> 6) == (_iota((n, n), 1) >> 6)).astype(F32)


def _tri(n, kind="le"):
    r, c = _iota((n, n), 0), _iota((n, n), 1)
    return {"le": c <= r, "ge": c >= r, "gt": c > r, "lt": c < r}[kind].astype(F32)


def _onehot_lane(h, n=128):
    return (_iota((1, n), 1) == h).astype(F32)


def _logsig(x):
    return jnp.minimum(x, 0.0) - jnp.log1p(jnp.exp(-jnp.abs(x)))


def _sigmoid(x):
    return 0.5 * (jnp.tanh(0.5 * x) + 1.0)


def _silu(x):
    return x * _sigmoid(x)


def _rms(x, g):
    return x * lax.rsqrt(jnp.mean(x * x, axis=-1, keepdims=True) + EPS) * g


def _headrms(x, w, bd64):
    ms = xr(x * x, bd64, bd64)
    return x * lax.rsqrt(ms + EPS) * w


def _call(body, name, grid, in_specs, out_specs, out_shape, scratch=()):
    return pl.pallas_call(
        body, name=name, grid=grid, in_specs=in_specs, out_specs=out_specs, out_shape=out_shape,
        scratch_shapes=list(scratch),
        compiler_params=pltpu.CompilerParams(dimension_semantics=("arbitrary",) * len(grid),
                                             vmem_limit_bytes=VMEM_LIMIT_BYTES))


def _sds(shape, dtype=F32):
    return jax.ShapeDtypeStruct(shape, dtype)


def _acc(ref, val, first):
    @pl.when(first)
    def _():
        ref[...] = val

    @pl.when(jnp.logical_not(first))
    def _():
        ref[...] += val


def inproj_fwd(x2, g, w, layer, tag):
    t = x2.shape[0]

    def body(x_ref, g_ref, w_ref, ht_ref, *outs):
        h = _rms(x_ref[...], g_ref[...])
        hb = _cast(h)
        ht_ref[...] = _cast(h.T)
        for (_, c0, wd), o in zip(PIECES, outs):
            o[...] = _dg(hb, _cast(w_ref[:, c0:c0 + wd]), 1, 0)

    return _call(
        body, f"inproj_fwd_{tag}", (t // TQ,),
        [pl.BlockSpec((TQ, D_MODEL), lambda i: (i, 0)),
         pl.BlockSpec((None, 1, D_MODEL), lambda i: (layer, 0, 0)),
         pl.BlockSpec((None, D_MODEL, D_INP), lambda i: (layer, 0, 0))],
        [pl.BlockSpec((D_MODEL, TQ), lambda i: (0, i))] + [pl.BlockSpec((TQ, wd), lambda i: (i, 0)) for _, _, wd in PIECES],
        [_sds((D_MODEL, t), _MMT)] + [_sds((t, wd)) for _, _, wd in PIECES],
    )(x2, g, w)


def inproj_bwd_dx(x2, g, w, dy, dpieces, layer, tag):
    t = x2.shape[0]

    def body(x_ref, g_ref, w_ref, dy_ref, *rest):
        dps, (dx_ref, dg_ref) = rest[:len(BWD_PIECES)], rest[len(BWD_PIECES):]
        dh = None
        for (_, c0, wd), dp in zip(BWD_PIECES, dps):
            part = _dg(_cast(dp[...]), _cast(w_ref[:, c0:c0 + wd]), 1, 1)
            dh = part if dh is None else dh + part
        _, vjp = jax.vjp(_rms, x_ref[...], g_ref[...])
        dx, dg = vjp(dh)
        dx_ref[...] = dy_ref[...] + dx
        _acc(dg_ref, dg, pl.program_id(0) == 0)

    return _call(
        body, f"inproj_bwd_dx_{tag}", (t // TQ,),
        [pl.BlockSpec((TQ, D_MODEL), lambda i: (i, 0)),
         pl.BlockSpec((None, 1, D_MODEL), lambda i: (layer, 0, 0)),
         pl.BlockSpec((None, D_MODEL, D_INP), lambda i: (layer, 0, 0)),
         pl.BlockSpec((TQ, D_MODEL), lambda i: (i, 0))] + [pl.BlockSpec((TQ, wd), lambda i: (i, 0)) for _, _, wd in BWD_PIECES],
        [pl.BlockSpec((TQ, D_MODEL), lambda i: (i, 0)), pl.BlockSpec((1, D_MODEL), lambda i: (0, 0))],
        [_sds((t, D_MODEL)), _sds((1, D_MODEL))],
    )(x2, g, w, dy, *dpieces)


def matmul_acc(at, b, tag):
    m, t = at.shape
    n = b.shape[1]
    tn = {1280: 640, 768: 768}.get(n, n)
    tk = 2048 if t % 2048 == 0 else (512 if t % 512 == 0 else TQ)

    def body(a_ref, b_ref, o_ref):
        _acc(o_ref, _dg(_cast(a_ref[...]), _cast(b_ref[...]), 1, 0), pl.program_id(1) == 0)

    return _call(
        body, f"matmul_acc_{tag}", (n // tn, t // tk),
        [pl.BlockSpec((m, tk), lambda j, i: (0, i)), pl.BlockSpec((tk, tn), lambda j, i: (i, j))],
        pl.BlockSpec((m, tn), lambda j, i: (0, j)),
        _sds((m, n)),
    )(at, b)


def _fox_prep_fn(q, k, ff, qw, kw, bias, carry, bd64, tri, trit, last):
    qn = _headrms(q, qw, bd64)
    kn = _headrms(k, kw, bd64)
    lf = _logsig(ff + bias)
    c = xl(tri, trit, lf) + carry
    return qn, kn, c, jnp.sum(c * last, axis=0, keepdims=True)


def _prep_consts():
    return _bdmask() * (1.0 / HD), _tri(TQ), _tri(TQ, "ge"), (_iota((TQ, 1), 0) == TQ - 1).astype(F32)


def fox_prep_fwd(pa, pf, qw, kw, bias, bl, s, layer, tag):
    nq = s // TQ

    def body(q_ref, k_ref, v_ref, f_ref, qw_ref, kw_ref, b_ref, qn_ref, kn_ref, vb_ref, cq_ref, ck_ref, carry):
        @pl.when(pl.program_id(1) == 0)
        def _():
            carry[...] = jnp.zeros_like(carry)

        qn, kn, c, cl = _fox_prep_fn(q_ref[...], k_ref[...], f_ref[...], qw_ref[...], kw_ref[...], b_ref[...],
                                     carry[...], *_prep_consts())
        carry[...] = cl
        qn_ref[...] = _cast(qn)
        kn_ref[...] = _cast(kn)
        vb_ref[...] = _cast(v_ref[...])
        cq_ref[...] = c
        ck_ref[...] = c.T[0:8, :]

    tok = lambda j: pl.BlockSpec((TQ, GW), lambda b, i: (b * nq + i, j))
    par = lambda n: pl.BlockSpec((None, 1, n), lambda b, i: (layer, 0, 0))
    return _call(
        body, f"fox_prep_fwd_{tag}", (bl, nq),
        [tok(0), tok(1), tok(2), pl.BlockSpec((TQ, 128), lambda b, i: (b * nq + i, 0)), par(GW), par(GW), par(128)],
        [tok(0), tok(0), tok(0), pl.BlockSpec((TQ, 128), lambda b, i: (b * nq + i, 0)),
         pl.BlockSpec((None, 8, TQ), lambda b, i: (b, 0, i))],
        [_sds((bl * s, GW), _MMT)] * 3 + [_sds((bl * s, 128)), _sds((bl, 8, s))],
        [pltpu.VMEM((1, 128), F32)],
    )(pa, pa, pa, pf, qw, kw, bias)


def fox_prep_bwd(pa, pf, qw, kw, bias, cq, dqn, dkn, dv, dck, bl, s, layer, tag):
    nq = s // TQ

    def body(q_ref, k_ref, f_ref, qw_ref, kw_ref, b_ref, cq_ref, cprev_ref, dqn_ref, dkn_ref, dv_ref, dck_ref,
             da_ref, df_ref, dqw_ref, dkw_ref, db_ref, dcarry):
        i = pl.program_id(1)
        first = jnp.logical_and(pl.program_id(0) == 0, i == 0)

        @pl.when(i == 0)
        def _():
            dcarry[...] = jnp.zeros_like(dcarry)

        last = (_iota((TQ, 1), 0) == TQ - 1).astype(F32)
        carry_in = jnp.where(i == nq - 1, 0.0, jnp.sum(cprev_ref[...] * last, axis=0, keepdims=True))
        consts = _prep_consts()
        _, vjp = jax.vjp(lambda *a: _fox_prep_fn(*a, *consts), q_ref[...], k_ref[...], f_ref[...], qw_ref[...],
                         kw_ref[...], b_ref[...], carry_in)
        dc = dck_ref[...].T
        dq, dk, dff, dqw, dkw, dbias, dcin = vjp((dqn_ref[...], dkn_ref[...], dc, dcarry[...]))
        dcarry[...] = dcin
        da_ref[:, 0:GW] = dq
        da_ref[:, GW:2 * GW] = dk
        da_ref[:, 2 * GW:3 * GW] = dv_ref[...]
        df_ref[...] = dff
        _acc(dqw_ref, dqw, first)
        _acc(dkw_ref, dkw, first)
        _acc(db_ref, dbias, first)

    rv = lambda b, i: b * nq + (nq - 1 - i)
    tok = lambda j: pl.BlockSpec((TQ, GW), lambda b, i: (rv(b, i), j))
    tok0 = pl.BlockSpec((TQ, GW), lambda b, i: (rv(b, i), 0))
    t128 = pl.BlockSpec((TQ, 128), lambda b, i: (rv(b, i), 0))
    prev = pl.BlockSpec((TQ, 128), lambda b, i: (jnp.maximum(rv(b, i) - 1, 0), 0))
    par = lambda n: pl.BlockSpec((None, 1, n), lambda b, i: (layer, 0, 0))
    acc = lambda n: pl.BlockSpec((1, n), lambda b, i: (0, 0))
    return _call(
        body, f"fox_prep_bwd_{tag}", (bl, nq),
        [tok(0), tok(1), t128, par(GW), par(GW), par(128), t128, prev, tok0, tok0, tok0,
         pl.BlockSpec((None, 128, TQ), lambda b, i: (b, 0, nq - 1 - i))],
        [pl.BlockSpec((TQ, 3 * GW), lambda b, i: (rv(b, i), 0)), t128, acc(GW), acc(GW), acc(128)],
        [_sds((bl * s, 3 * GW)), _sds((bl * s, 128)), _sds((1, GW)), _sds((1, GW)), _sds((1, 128))],
        [pltpu.VMEM((1, 128), F32)],
    )(pa, pa, pf, qw, kw, bias, cq, cq, dqn, dkn, dv, dck)


def _lane_pick(x, h):
    return jnp.sum(x * _onehot_lane(h), axis=-1, keepdims=True)


TA = 128
TK_FOX = 512
TK_SB = 256
SROWS = NH * TA


def _stack_heads(x, scale=1.0):
    return _cast(jnp.concatenate([x * (_hmask(h) * scale) for h in range(NH)], axis=0))


def _stack_cols(x):
    return jnp.concatenate([_lane_pick(x, h) for h in range(NH)], axis=0)


def _spread_heads(col):
    return sum(col[h * TA:(h + 1) * TA] * _hmask(h) for h in range(NH))


def _lanes_cat(w):
    return jnp.concatenate([w[h * TA:(h + 1) * TA] for h in range(NH)], axis=1)


def _mask_stack(x):
    return _cast(jnp.concatenate([x * _hmask(h).astype(x.dtype) for h in range(NH)], axis=0))


def _stack_rows(i):
    return i * TA + (_iota((SROWS, 1), 0) & (TA - 1))


def _n_key_tiles(i, tk):
    return lax.shift_right_logical(i * TA, tk.bit_length() - 1) + 1


def fox_attn_fwd(qn, kn, vb, cq, ck, bl, s, tag):
    nq, TK = s // TA, min(TK_FOX, s)

    def body(q_ref, k_ref, v_ref, cq_ref, ck_ref, o_ref, lse_ref, acc, vst):
        i = pl.program_id(1)

        @pl.when(i == 0)
        def _():
            _fill_stacked(vst, v_ref, s, TK)

        qs = _stack_heads(q_ref[...].astype(F32), SCALE)
        cqs = _stack_cols(cq_ref[...])
        row = _stack_rows(i)
        acc[...] = jnp.zeros_like(acc)

        def step(j, ml):
            m, l = ml
            ks = pl.ds(pl.multiple_of(j * TK, TK), TK)
            ckb = jnp.concatenate([jnp.broadcast_to(ck_ref[h:h + 1, ks], (TA, TK)) for h in range(NH)], axis=0)
            sc = _dg(qs, k_ref[ks, :], 1, 1) + cqs - ckb
            col = j * TK + _iota((1, TK), 1)
            sc = jnp.where(col <= row, sc, NEG_BIG)
            m_new = jnp.maximum(m, jnp.max(sc, axis=-1, keepdims=True))
            alpha = jnp.exp(m - m_new)
            p = jnp.exp(sc - m_new)
            vs = vst[pl.ds(pl.multiple_of(j * NH * TK, NH * TK), NH * TK), :]
            acc[...] = _spread_heads(alpha) * acc[...] + _dg(_lanes_cat(_cast(p)), vs, 1, 0)
            return m_new, alpha * l + jnp.sum(p, axis=-1, keepdims=True)

        m, l = lax.fori_loop(0, _n_key_tiles(i, TK), step, (jnp.full((SROWS, 1), NEG_BIG, F32), jnp.zeros((SROWS, 1), F32)))
        o_ref[...] = acc[...] / _spread_heads(l)
        lse_h = m + jnp.log(l)
        lse_ref[...] = sum(lse_h[h * TA:(h + 1) * TA] * _onehot_lane(h) for h in range(NH))

    tok = pl.BlockSpec((TA, GW), lambda b, i: (b * nq + i, 0))
    seq = pl.BlockSpec((s, GW), lambda b, i: (b, 0))
    t128 = pl.BlockSpec((TA, 128), lambda b, i: (b * nq + i, 0))
    return _call(
        body, f"fox_attn_fwd_{tag}", (bl, nq),
        [tok, seq, seq, t128, pl.BlockSpec((None, 8, s), lambda b, i: (b, 0, 0))],
        [tok, t128], [_sds((bl * s, GW)), _sds((bl * s, 128))],
        [pltpu.VMEM((TA, GW), F32), pltpu.VMEM((NH * s, GW), _MMT)],
    )(qn, kn, vb, cq, ck)


def fox_attn_bwd(qn, kn, vb, cq, ck, lse, do, bl, s, tag):
    nq, TK = s // TA, min(TK_FOX, s)

    def body(q_ref, k_ref, v_ref, cq_ref, ck_ref, lse_ref, do_ref, dq_ref, dk_ref, dv_ref, dck_ref, dqa, p_s, dp_s, kst):
        i = pl.program_id(1)

        @pl.when(i == 0)
        def _():
            dk_ref[...] = jnp.zeros_like(dk_ref)
            dv_ref[...] = jnp.zeros_like(dv_ref)
            dck_ref[...] = jnp.zeros_like(dck_ref)
            _fill_stacked(kst, k_ref, s, TK)

        qs = _stack_heads(q_ref[...].astype(F32), SCALE)
        dos = _stack_heads(do_ref[...])
        cqs, lses = _stack_cols(cq_ref[...]), _stack_cols(lse_ref[...])
        row = _stack_rows(i)
        dqa[...] = jnp.zeros_like(dqa)
        nk = _n_key_tiles(i, TK)

        def probs(j, delta):
            ks = pl.ds(pl.multiple_of(j * TK, TK), TK)
            ckb = jnp.concatenate([jnp.broadcast_to(ck_ref[h:h + 1, ks], (TA, TK)) for h in range(NH)], axis=0)
            sc = _dg(qs, k_ref[ks, :], 1, 1) + cqs - ckb
            col = j * TK + _iota((1, TK), 1)
            p = jnp.where(col <= row, jnp.exp(sc - lses), 0.0)
            dp = _dg(dos, v_ref[ks, :], 1, 1)
            p_s[:, ks] = p
            dp_s[:, ks] = dp
            return delta + jnp.sum(p * dp, axis=-1, keepdims=True)

        delta = lax.fori_loop(0, nk, probs, jnp.zeros((SROWS, 1), F32))

        def step(j, carry):
            ks = pl.ds(pl.multiple_of(j * TK, TK), TK)
            p = p_s[:, ks]
            ds = p * (dp_s[:, ks] - delta)
            dsb = _cast(ds)
            dqa[...] += _dg(_lanes_cat(dsb), kst[pl.ds(pl.multiple_of(j * NH * TK, NH * TK), NH * TK), :], 1, 0) * SCALE
            dk_ref[ks, :] += _dg(dsb, qs, 0, 0)
            dv_ref[ks, :] += _dg(_cast(p), dos, 0, 0)
            for h in range(NH):
                dck_ref[h:h + 1, ks] -= jnp.sum(ds[h * TA:(h + 1) * TA], axis=0, keepdims=True)
            return carry

        lax.fori_loop(0, nk, step, 0)
        dq_ref[...] = dqa[...]

    tok = pl.BlockSpec((TA, GW), lambda b, i: (b * nq + i, 0))
    seq = pl.BlockSpec((s, GW), lambda b, i: (b, 0))
    t128 = pl.BlockSpec((TA, 128), lambda b, i: (b * nq + i, 0))
    return _call(
        body, f"fox_attn_bwd_{tag}", (bl, nq),
        [tok, seq, seq, t128, pl.BlockSpec((None, 8, s), lambda b, i: (b, 0, 0)), t128, tok],
        [tok, seq, seq, pl.BlockSpec((None, 128, s), lambda b, i: (b, 0, 0))],
        [_sds((bl * s, GW)), _sds((bl * s, GW)), _sds((bl * s, GW)), _sds((bl, 128, s))],
        [pltpu.VMEM((TA, GW), F32), pltpu.VMEM((SROWS, s), F32), pltpu.VMEM((SROWS, s), F32), pltpu.VMEM((NH * s, GW), _MMT)],
    )(qn, kn, vb, cq, ck, lse, do)


def _sb_block(qh, kb, valid, upper, r_carry):
    z = _dg(qh, kb, 1, 1)
    ls = _logsig(z)
    lom = ls - z if valid is None else jnp.where(valid, ls - z, 0.0)
    between = xr(lom, upper, upper) + r_carry
    w = jnp.exp(ls + between)
    return ls, lom, (w if valid is None else jnp.where(valid, w, 0.0))


def _fill_stacked(dst, src_ref, s, tk):
    for j in range(s // tk):
        dst[j * NH * tk:(j + 1) * NH * tk, :] = _mask_stack(src_ref[j * tk:(j + 1) * tk, :])


def sb_attn_fwd(pb, bl, s, tag):
    nq, TK = s // TA, TK_SB

    def body(q_ref, k_ref, v_ref, o_ref, acc, vst):
        i = pl.program_id(1)

        @pl.when(i == 0)
        def _():
            _fill_stacked(vst, v_ref, s, TK)

        qs = _stack_heads(q_ref[...], SCALE)
        upper = _tri(TK, "lt")
        last = _n_key_tiles(i, TK) - 1

        def step(j, r, valid):
            ks = pl.ds(pl.multiple_of(j * TK, TK), TK)
            _, lom, w = _sb_block(qs, _cast(k_ref[ks, :]), valid, upper, r)
            acc[...] += _dg(_lanes_cat(_cast(w)), vst[pl.ds(pl.multiple_of(j * NH * TK, NH * TK), NH * TK), :], 1, 0)
            return r + jnp.sum(lom, axis=-1, keepdims=True)

        acc[...] = jnp.zeros_like(acc)
        r = step(last, jnp.zeros((SROWS, 1), F32), last * TK + _iota((1, TK), 1) < _stack_rows(i))
        lax.fori_loop(0, last, lambda jj, r: step(last - 1 - jj, r, None), r)
        o_ref[...] = acc[...]

    tok = lambda j: pl.BlockSpec((TA, GW), lambda b, i: (b * nq + i, j))
    seq = lambda j: pl.BlockSpec((s, GW), lambda b, i: (b, j))
    return _call(
        body, f"sb_attn_fwd_{tag}", (bl, nq), [tok(0), seq(1), seq(2)],
        pl.BlockSpec((TA, GW), lambda b, i: (b * nq + i, 0)), _sds((bl * s, GW)),
        [pltpu.VMEM((TA, GW), F32), pltpu.VMEM((NH * s, GW), _MMT)],
    )(pb, pb, pb)


def sb_attn_bwd(pb, do, bl, s, tag):
    nq, TK = s // TA, TK_SB

    def body(q_ref, k_ref, v_ref, do_ref, dq_ref, dk_ref, dv_ref, dqa, ls_s, lom_s, w_s, g_s, kst):
        i = pl.program_id(1)

        @pl.when(i == 0)
        def _():
            dk_ref[...] = jnp.zeros_like(dk_ref)
            dv_ref[...] = jnp.zeros_like(dv_ref)
            _fill_stacked(kst, k_ref, s, TK)

        qs = _stack_heads(q_ref[...], SCALE)
        dos = _stack_heads(do_ref[...])
        upper = _tri(TK, "lt")
        before = _tri(TK, "gt")
        dqa[...] = jnp.zeros_like(dqa)
        last = _n_key_tiles(i, TK) - 1
        diag = last * TK + _iota((1, TK), 1) < _stack_rows(i)

        def weights(j, r, valid):
            ks = pl.ds(pl.multiple_of(j * TK, TK), TK)
            ls, lom, w = _sb_block(qs, _cast(k_ref[ks, :]), valid, upper, r)
            ls_s[:, ks] = ls
            lom_s[:, ks] = lom
            w_s[:, ks] = _cast(w)
            g_s[:, ks] = _dg(dos, _cast(v_ref[ks, :]), 1, 1) * w
            return r + jnp.sum(lom, axis=-1, keepdims=True)

        r = weights(last, jnp.zeros((SROWS, 1), F32), diag)
        lax.fori_loop(0, last, lambda jj, r: weights(last - 1 - jj, r, None), r)

        def step(j, cpre, valid):
            ks = pl.ds(pl.multiple_of(j * TK, TK), TK)
            g = g_s[:, ks]
            pre = cpre + xr(g, before, before)
            dz = g * jnp.exp(lom_s[:, ks]) - jnp.exp(ls_s[:, ks]) * pre
            dzb = _cast(dz if valid is None else jnp.where(valid, dz, 0.0))
            dqa[...] += _dg(_lanes_cat(dzb), kst[pl.ds(pl.multiple_of(j * NH * TK, NH * TK), NH * TK), :], 1, 0) * SCALE
            dk_ref[ks, :] += _dg(dzb, qs, 0, 0)
            dv_ref[ks, :] += _dg(w_s[:, ks], dos, 0, 0)
            return cpre + jnp.sum(g, axis=-1, keepdims=True)

        cpre = lax.fori_loop(0, last, lambda j, c: step(j, c, None), jnp.zeros((SROWS, 1), F32))
        step(last, cpre, diag)
        dq_ref[...] = dqa[...]

    tok = lambda j: pl.BlockSpec((TA, GW), lambda b, i: (b * nq + i, j))
    seq = lambda j: pl.BlockSpec((s, GW), lambda b, i: (b, j))
    return _call(
        body, f"sb_attn_bwd_{tag}", (bl, nq), [tok(0), seq(1), seq(2), tok(0)],
        [tok(0), seq(0), seq(0)], [_sds((bl * s, GW))] * 3,
        [pltpu.VMEM((TA, GW), F32), pltpu.VMEM((SROWS, s), F32), pltpu.VMEM((SROWS, s), F32),
         pltpu.VMEM((SROWS, s), _MMT), pltpu.VMEM((SROWS, s), F32), pltpu.VMEM((NH * s, GW), _MMT)],
    )(pb, pb, pb, do)


def _hgrn_consts():
    r, c = _iota((CH, CH), 0), _iota((CH, CH), 1)
    rr = _iota((CH, 1), 0)
    tri = (c <= r).astype(F32)
    lv = []
    for m in (8, 4, 2, 1):
        up = ((rr & (2 * m - 1)) >= m).astype(F32)
        lo = 1.0 - up
        selq = (((r & (2 * m - 1)) >= m) & (c == (r & ~(m - 1)) - 1)).astype(F32)
        selk = (((r & (2 * m - 1)) < m) & (c == (r & ~(m - 1)) + m - 1)).astype(F32)
        pm = (((r & ~(2 * m - 1)) == (c & ~(2 * m - 1))) & ((r & (2 * m - 1)) >= m) & ((c & (2 * m - 1)) < m)).astype(F32)
        lv.append((up, lo, selq, selq.T, selk, selk.T, jnp.concatenate([pm] * NH, axis=0)))
    hm4 = lambda n: (((_iota((NH, 1, n), 2) & (GW - 1)) >> 6) == _iota((NH, 1, n), 0)).astype(F32)
    return dict(tri=tri, trit=tri.T, rr=rr, lv=lv, bd=_bdmask(), bd64=_bdmask() * (1.0 / HD),
                hm4={GW: hm4(GW), 3 * GW: hm4(3 * GW)})


def _hgrn_chunk_fn(hq, hf, hi, lb, wn, st, cs):
    q = _silu(hq)
    log_lb = jnp.log(jnp.maximum(lb, LB_FLOOR))
    a, bb = log_lb, jnp.log1p(-lb) + _logsig(hf)
    g = jnp.maximum(a, bb) + jnp.log1p(jnp.exp(-jnp.abs(a - bb)))
    k = (1.0 - lb) * _sigmoid(-hf)
    v = hi
    rr = cs["rr"]
    b = xl(cs["tri"], cs["trit"], g)
    row_of = lambda n: jnp.sum(b * (rr == n).astype(F32), axis=0, keepdims=True)
    o = mm_nt(q * jnp.exp(b), st)
    qs, ks = [], []
    for ib in (1, 2, 3):
        ref = row_of(16 * ib - 1)
        inq = ((rr >= 16 * ib) & (rr < 16 * ib + 16)).astype(F32)
        ink = (rr < 16 * ib).astype(F32)
        qs.append(q * jnp.exp((b - ref) * inq) * inq)
        ks.append(k * jnp.exp((ref - b) * ink) * ink)
    qcat, kcat = jnp.concatenate(qs, axis=1), jnp.concatenate(ks, axis=1)
    lvl = []
    for up, lo, selq, selqt, selk, selkt, pm in cs["lv"]:
        qe = q * jnp.exp((b - xl(selq, selqt, b)) * up) * up
        ke = k * jnp.exp((xl(selk, selkt, b) - b) * lo) * lo
        lvl.append((qe, ke, pm))
    stack = lambda x: (x[None] * cs["hm4"][x.shape[1]]).reshape(NH * CH, x.shape[1])
    a_all = mm_nt(stack(qcat), kcat)
    for qe, ke, pm4 in lvl:
        a_all = a_all + mm_nt(stack(qe), ke) * pm4
    o = o + jnp.sum(mm(a_all, v).reshape(NH, CH, GW) * cs["hm4"][GW], axis=0)
    o = o + xr(q * k, cs["bd"], cs["bd"]) * v
    b_last = row_of(CH - 1)
    st_new = st * jnp.exp(b_last) + mm_tn(v, k * jnp.exp(b_last - b)) * cs["bd"]
    return _headrms(o, wn, cs["bd64"]), st_new


def hgrn_fwd(pc, lb, wn, bl, s, layer, tag):
    nc = s // CH

    def body(q_ref, f_ref, i_ref, lb_ref, wn_ref, o_ref, st_ref, st):
        @pl.when(pl.program_id(0) == 0)
        def _():
            st[...] = jnp.zeros_like(st)

        cs = _hgrn_consts()
        for b in range(bl):
            st_ref[b] = st[b]
            o, st_new = _hgrn_chunk_fn(q_ref[b], f_ref[b], i_ref[b], lb_ref[...], wn_ref[...], st[b], cs)
            o_ref[b] = o
            st[b] = st_new

    tok = lambda j: pl.BlockSpec((bl, CH, GW), lambda c: (0, c, j))
    par = pl.BlockSpec((None, 1, GW), lambda c: (layer, 0, 0))
    pc3 = pc.reshape(bl, s, 3 * GW)
    o, states = _call(
        body, f"hgrn_fwd_{tag}", (nc,), [tok(0), tok(1), tok(2), par, par],
        [tok(0), pl.BlockSpec((bl, None, GW, GW), lambda c: (0, c, 0, 0))],
        [_sds((bl, s, GW)), _sds((bl, nc, GW, GW))],
        [pltpu.VMEM((bl, GW, GW), F32)],
    )(pc3, pc3, pc3, lb, wn)
    return o.reshape(bl * s, GW), states


def hgrn_bwd(pc, lb, wn, states, do, bl, s, layer, tag):
    nc = s // CH

    def body(q_ref, f_ref, i_ref, lb_ref, wn_ref, st_ref, do_ref, dc_ref, dlb_ref, dwn_ref, dst):
        c = pl.program_id(0)

        @pl.when(c == 0)
        def _():
            dst[...] = jnp.zeros_like(dst)

        cs = _hgrn_consts()
        dlb_sum = dwn_sum = None
        for b in range(bl):
            _, vjp = jax.vjp(lambda *a: _hgrn_chunk_fn(*a, cs), q_ref[b], f_ref[b], i_ref[b], lb_ref[...],
                             wn_ref[...], st_ref[b])
            dq, df, di, dlb, dwn, dst_in = vjp((do_ref[b], dst[b]))
            dst[b] = dst_in
            dc_ref[b, :, 0:GW] = dq
            dc_ref[b, :, GW:2 * GW] = df
            dc_ref[b, :, 2 * GW:3 * GW] = di
            dlb_sum = dlb if dlb_sum is None else dlb_sum + dlb
            dwn_sum = dwn if dwn_sum is None else dwn_sum + dwn
        _acc(dlb_ref, dlb_sum, c == 0)
        _acc(dwn_ref, dwn_sum, c == 0)

    tok = lambda j: pl.BlockSpec((bl, CH, GW), lambda c: (0, nc - 1 - c, j))
    par = pl.BlockSpec((None, 1, GW), lambda c: (layer, 0, 0))
    acc = pl.BlockSpec((1, GW), lambda c: (0, 0))
    pc3 = pc.reshape(bl, s, 3 * GW)
    dc, dlb, dwn = _call(
        body, f"hgrn_bwd_{tag}", (nc,),
        [tok(0), tok(1), tok(2), par, par, pl.BlockSpec((bl, None, GW, GW), lambda c: (0, nc - 1 - c, 0, 0)), tok(0)],
        [pl.BlockSpec((bl, CH, 3 * GW), lambda c: (0, nc - 1 - c, 0)), acc, acc],
        [_sds((bl, s, 3 * GW)), _sds((1, GW)), _sds((1, GW))],
        [pltpu.VMEM((bl, GW, GW), F32)],
    )(pc3, pc3, pc3, lb, wn, states, do.reshape(bl, s, GW))
    return dc.reshape(bl * s, 3 * GW), dlb, dwn


def _shift_rows(x, k, up):
    n = x.shape[0]
    rr = _iota((n, 1), 0)
    if up:
        return jnp.where(rr < n - k, pltpu.roll(x, n - k, 0), 0.0)
    return jnp.where(rr >= k, pltpu.roll(x, k, 0), 0.0)


def _window_sums(x, up):
    s2 = x + _shift_rows(x, 1, up)
    s4 = s2 + _shift_rows(s2, 2, up)
    s8 = s4 + _shift_rows(s4, 4, up)
    s16 = s8 + _shift_rows(s8, 8, up)
    return s2, s4, s8, s16


def _pool_div(n):
    pos = (_iota((n, 1), 0) + 1).astype(F32)
    return [jnp.minimum(pos, float(w)) for w in (2, 4, 8, 16)]


def _pool_mix(sums, scaled):
    out = None
    for gi, sw in enumerate(sums):
        part = (sw if scaled is None else sw / scaled[gi]) * _hmask(gi)
        out = part if out is None else out + part
    return out


def pool_fwd(pd, wbd, scale, bl, s, layer, tag):
    def body(u_ref, w_ref, sc_ref, o_ref):
        u = u_ref[...]
        pm = _pool_mix(_window_sums(u, False), _pool_div(s)) - u
        o_ref[...] = _dg(_cast(pm), _cast(w_ref[...]), 1, 0) * sc_ref[...]

    seq = pl.BlockSpec((s, GW), lambda b: (b, 0))
    return _call(
        body, f"pool_fwd_{tag}", (bl,),
        [seq, pl.BlockSpec((None, GW, GW), lambda b: (layer, 0, 0)), pl.BlockSpec((None, 1, GW), lambda b: (layer, 0, 0))],
        seq, _sds((bl * s, GW)),
    )(pd, wbd, scale)


def pool_bwd(pd, wbd, scale, do, bl, s, layer, tag):
    def body(u_ref, w_ref, sc_ref, do_ref, du_ref, dw_ref, dsc_ref):
        first = pl.program_id(0) == 0
        u, do = u_ref[...], do_ref[...]
        div = _pool_div(s)
        pm = _pool_mix(_window_sums(u, False), div) - u
        ypre = _dg(_cast(pm), _cast(w_ref[...]), 1, 0)
        dys = do * sc_ref[...]
        _acc(dsc_ref, jnp.sum(do * ypre, axis=0, keepdims=True), first)
        _acc(dw_ref, _dg(_cast(pm), _cast(dys), 0, 0), first)
        dpm = _dg(_cast(dys), _cast(w_ref[...]), 1, 1)
        dsc = [dpm / d for d in div]
        adj = None
        for gi in range(4):
            part = _window_sums(dsc[gi] * _hmask(gi), True)[gi]
            adj = part if adj is None else adj + part
        du_ref[...] = adj - dpm

    seq = pl.BlockSpec((s, GW), lambda b: (b, 0))
    return _call(
        body, f"pool_bwd_{tag}", (bl,),
        [seq, pl.BlockSpec((None, GW, GW), lambda b: (layer, 0, 0)), pl.BlockSpec((None, 1, GW), lambda b: (layer, 0, 0)), seq],
        [seq, pl.BlockSpec((GW, GW), lambda b: (0, 0)), pl.BlockSpec((1, GW), lambda b: (0, 0))],
        [_sds((bl * s, GW)), _sds((GW, GW)), _sds((1, GW))],
    )(pd, wbd, scale, do)


def _mem_prep_fn(mem, g, wk, wv, kw, bd64):
    mn = _rms(mem, g)
    return _headrms(mm(mn, wk), kw, bd64), mm(mn, wv)


def mem_prep_fwd(mem2, g, wkv, kw, bl, layer, tag):
    def body(m_ref, g_ref, wk_ref, wv_ref, kw_ref, k_ref, v_ref):
        k, v = _mem_prep_fn(m_ref[...], g_ref[...], wk_ref[...], wv_ref[...], kw_ref[...], _bdmask() * (1.0 / HD))
        k_ref[...] = k
        v_ref[...] = v

    blk = pl.BlockSpec((N_MEM, GW), lambda b: (b, 0))
    return _call(
        body, f"mem_prep_fwd_{tag}", (bl,),
        [pl.BlockSpec((N_MEM, D_MODEL), lambda b: (b, 0)), pl.BlockSpec((None, 1, D_MODEL), lambda b: (layer, 0, 0)),
         pl.BlockSpec((None, D_MODEL, GW), lambda b: (layer, 0, 0)), pl.BlockSpec((None, D_MODEL, GW), lambda b: (layer, 0, 1)),
         pl.BlockSpec((None, 1, GW), lambda b: (layer, 0, 0))],
        [blk, blk], [_sds((bl * N_MEM, GW))] * 2,
    )(mem2, g, wkv, wkv, kw)


def mem_prep_bwd(mem2, g, wkv, kw, dk, dv, bl, layer, tag):
    def body(m_ref, g_ref, wk_ref, wv_ref, kw_ref, dk_ref, dv_ref, dwk_ref, dwv_ref, dg_ref, dkw_ref):
        first = pl.program_id(0) == 0
        bd64 = _bdmask() * (1.0 / HD)
        _, vjp = jax.vjp(lambda g_, wk, wv, kw_: _mem_prep_fn(m_ref[...], g_, wk, wv, kw_, bd64),
                         g_ref[...], wk_ref[...].astype(F32), wv_ref[...].astype(F32), kw_ref[...])
        dg, dwk, dwv, dkw = vjp((dk_ref[...], dv_ref[...]))
        _acc(dwk_ref, dwk, first)
        _acc(dwv_ref, dwv, first)
        _acc(dg_ref, dg, first)
        _acc(dkw_ref, dkw, first)

    blk = pl.BlockSpec((N_MEM, GW), lambda b: (b, 0))
    return _call(
        body, f"mem_prep_bwd_{tag}", (bl,),
        [pl.BlockSpec((N_MEM, D_MODEL), lambda b: (b, 0)), pl.BlockSpec((None, 1, D_MODEL), lambda b: (layer, 0, 0)),
         pl.BlockSpec((None, D_MODEL, GW), lambda b: (layer, 0, 0)), pl.BlockSpec((None, D_MODEL, GW), lambda b: (layer, 0, 1)),
         pl.BlockSpec((None, 1, GW), lambda b: (layer, 0, 0)), blk, blk],
        [pl.BlockSpec((D_MODEL, GW), lambda b: (0, 0)), pl.BlockSpec((D_MODEL, GW), lambda b: (0, 0)),
         pl.BlockSpec((1, D_MODEL), lambda b: (0, 0)), pl.BlockSpec((1, GW), lambda b: (0, 0))],
        [_sds((D_MODEL, GW)), _sds((D_MODEL, GW)), _sds((1, D_MODEL)), _sds((1, GW))],
    )(mem2, g, wkv, wkv, kw, dk, dv)


def _mem_attn_fn(mq, qw, k, v, bd64):
    qn = _headrms(mq, qw, bd64)
    out = None
    for h in range(NH):
        hm = _hmask(h)
        lg = mm_nt(qn * hm, k) * SCALE
        e = jnp.exp(lg - lax.stop_gradient(jnp.max(lg, axis=-1, keepdims=True)))
        p = e / jnp.sum(e, axis=-1, keepdims=True)
        part = mm(p, v) * hm
        out = part if out is None else out + part
    return out


def mem_attn_fwd(pe, qw, k, v, bl, s, layer, tag):
    nq = s // TQ

    def body(q_ref, qw_ref, k_ref, v_ref, o_ref):
        o_ref[...] = _mem_attn_fn(q_ref[...], qw_ref[...], k_ref[...], v_ref[...], _bdmask() * (1.0 / HD))

    tok = pl.BlockSpec((TQ, GW), lambda b, i: (b * nq + i, 0))
    kv = pl.BlockSpec((N_MEM, GW), lambda b, i: (b, 0))
    return _call(
        body, f"mem_attn_fwd_{tag}", (bl, nq), [tok, pl.BlockSpec((None, 1, GW), lambda b, i: (layer, 0, 0)), kv, kv],
        tok, _sds((bl * s, GW)),
    )(pe, qw, k, v)


def mem_attn_bwd(pe, qw, k, v, do, bl, s, layer, tag):
    nq = s // TQ

    def body(q_ref, qw_ref, k_ref, v_ref, do_ref, dq_ref, dk_ref, dv_ref, dqw_ref):
        i = pl.program_id(1)
        bd64 = _bdmask() * (1.0 / HD)
        _, vjp = jax.vjp(lambda *a: _mem_attn_fn(*a, bd64), q_ref[...], qw_ref[...], k_ref[...], v_ref[...])
        dq, dqw, dk, dv = vjp(do_ref[...])
        dq_ref[...] = dq
        _acc(dk_ref, dk, i == 0)
        _acc(dv_ref, dv, i == 0)
        _acc(dqw_ref, dqw, jnp.logical_and(pl.program_id(0) == 0, i == 0))

    tok = pl.BlockSpec((TQ, GW), lambda b, i: (b * nq + i, 0))
    kv = pl.BlockSpec((N_MEM, GW), lambda b, i: (b, 0))
    return _call(
        body, f"mem_attn_bwd_{tag}", (bl, nq),
        [tok, pl.BlockSpec((None, 1, GW), lambda b, i: (layer, 0, 0)), kv, kv, tok],
        [tok, kv, kv, pl.BlockSpec((1, GW), lambda b, i: (0, 0))],
        [_sds((bl * s, GW)), _sds((bl * N_MEM, GW)), _sds((bl * N_MEM, GW)), _sds((1, GW))],
    )(pe, qw, k, v, do)


def _gate_out_fn(outs, gates, wparts):
    y = None
    for o, g, w in zip(outs, gates, wparts):
        part = mm(o * _silu(g), w)
        y = part if y is None else y + part
    return y


def outproj_fwd(x2, outs, pg, wout, layer, tag):
    t = x2.shape[0]

    def body(x_ref, oa, ob, oc, od, oe, g_ref, w_ref, y_ref):
        outs_ = [r[...] for r in (oa, ob, oc, od, oe)]
        gates = [g_ref[:, j * GW:(j + 1) * GW] for j in range(5)]
        wparts = [w_ref[j * GW:(j + 1) * GW, :] for j in range(5)]
        y_ref[...] = x_ref[...] + _gate_out_fn(outs_, gates, wparts)

    tok = pl.BlockSpec((TQ, GW), lambda i: (i, 0))
    big = pl.BlockSpec((TQ, D_MODEL), lambda i: (i, 0))
    return _call(
        body, f"outproj_fwd_{tag}", (t // TQ,),
        [big] + [tok] * 5 + [pl.BlockSpec((TQ, D_MIX), lambda i: (i, 0)),
                            pl.BlockSpec((None, D_MIX, D_MODEL), lambda i: (layer, 0, 0))],
        big, _sds((t, D_MODEL)),
    )(x2, *outs, pg, wout)


def outproj_bwd(outs, pg, wout, dy, layer, tag):
    t = dy.shape[0]

    def body(oa, ob, oc, od, oe, g_ref, w_ref, dy_ref, da, db, dc, dd, de, dg_ref, dw_ref):
        outs_ = [r[...] for r in (oa, ob, oc, od, oe)]
        gates = [g_ref[:, j * GW:(j + 1) * GW] for j in range(5)]
        wparts = [w_ref[j * GW:(j + 1) * GW, :].astype(F32) for j in range(5)]
        _, vjp = jax.vjp(_gate_out_fn, outs_, gates, wparts)
        douts, dgates, dws = vjp(dy_ref[...])
        for r, val in zip((da, db, dc, dd, de), douts):
            r[...] = val
        first = pl.program_id(0) == 0
        for j in range(5):
            dg_ref[:, j * GW:(j + 1) * GW] = dgates[j]

        @pl.when(first)
        def _():
            for j in range(5):
                dw_ref[j * GW:(j + 1) * GW, :] = dws[j]

        @pl.when(jnp.logical_not(first))
        def _():
            for j in range(5):
                dw_ref[j * GW:(j + 1) * GW, :] += dws[j]

    tok = pl.BlockSpec((TQ, GW), lambda i: (i, 0))
    return _call(
        body, f"outproj_bwd_{tag}", (t // TQ,),
        [tok] * 5 + [pl.BlockSpec((TQ, D_MIX), lambda i: (i, 0)), pl.BlockSpec((None, D_MIX, D_MODEL), lambda i: (layer, 0, 0)),
                     pl.BlockSpec((TQ, D_MODEL), lambda i: (i, 0))],
        [tok] * 5 + [pl.BlockSpec((TQ, D_MIX), lambda i: (i, 0)), pl.BlockSpec((D_MIX, D_MODEL), lambda i: (0, 0))],
        [_sds((t, GW))] * 5 + [_sds((t, D_MIX)), _sds((D_MIX, D_MODEL))],
    )(*outs, pg, wout, dy)


def loss_head(y, tgt):
    t = y.shape[0]

    def body(y_ref, t_ref, l_ref, dy_ref):
        diff = y_ref[...] - t_ref[...]
        dy_ref[...] = diff * (1.0 / D_MODEL)
        part = 0.5 * jnp.sum(jnp.sum(diff * diff, axis=-1, keepdims=True) * (1.0 / D_MODEL), axis=0, keepdims=True)
        _acc(l_ref, jnp.broadcast_to(part, (8, 128)), pl.program_id(0) == 0)

    big = pl.BlockSpec((TQ, D_MODEL), lambda i: (i, 0))
    return _call(body, "loss_head", (t // TQ,), [big, big], [pl.BlockSpec((8, 128), lambda i: (0, 0)), big],
                 [_sds((8, 128)), _sds((t, D_MODEL))])(y, tgt)


def layer_fwd(x2, mem2, p, layer, bl, s):
    tag = f"l{layer}"
    ht, pa, pb, pc, pd, pe, pg, pf = inproj_fwd(x2, p["norm_g"], p["w_in"], layer, tag)
    qn, kn, vb, cq, ck = fox_prep_fwd(pa, pf, p["fox_q_norm"], p["fox_k_norm"], p["fox_f_bias"], bl, s, layer, tag)
    oa, lse = fox_attn_fwd(qn, kn, vb, cq, ck, bl, s, tag)
    ob = sb_attn_fwd(pb, bl, s, tag)
    oc, states = hgrn_fwd(pc, p["lb"], p["hgrn_out_norm"], bl, s, layer, tag)
    od = pool_fwd(pd, p["pool_wbd"], p["pool_scale"], bl, s, layer, tag)
    mk, mv = mem_prep_fwd(mem2, p["mem_norm_g"], p["mem_w_kv"], p["mem_k_norm"], bl, layer, tag)
    oe = mem_attn_fwd(pe, p["mem_q_norm"], mk, mv, bl, s, layer, tag)
    y = outproj_fwd(x2, (oa, ob, oc, od, oe), pg, p["w_out"], layer, tag)
    saved = dict(x2=x2, ht=ht, pa=pa, pb=pb, pc=pc, pd=pd, pe=pe, pg=pg, pf=pf, qn=qn, kn=kn, vb=vb, cq=cq, ck=ck,
                 oa=oa, lse=lse, ob=ob, oc=oc, states=states, od=od, mk=mk, mv=mv, oe=oe)
    return y, saved


def layer_bwd(dy, mem2, p, sv, layer, bl, s):
    tag = f"l{layer}"
    (doa, dob, doc, dod, doe, dg_gates, dwout) = outproj_bwd((sv["oa"], sv["ob"], sv["oc"], sv["od"], sv["oe"]), sv["pg"],
                                                              p["w_out"], dy, layer, tag)
    dqn, dkn, dv, dck = fox_attn_bwd(sv["qn"], sv["kn"], sv["vb"], sv["cq"], sv["ck"], sv["lse"], doa, bl, s, tag)
    d_a, d_f, dqw, dkw, dbias = fox_prep_bwd(sv["pa"], sv["pf"], p["fox_q_norm"], p["fox_k_norm"], p["fox_f_bias"], sv["cq"],
                                             dqn, dkn, dv, dck, bl, s, layer, tag)
    dsq, dsk, dsv = sb_attn_bwd(sv["pb"], dob, bl, s, tag)
    d_c, dlb, dwn = hgrn_bwd(sv["pc"], p["lb"], p["hgrn_out_norm"], sv["states"], doc, bl, s, layer, tag)
    d_d, dwbd, dpscale = pool_bwd(sv["pd"], p["pool_wbd"], p["pool_scale"], dod, bl, s, layer, tag)
    d_e, dmk, dmv, dmqw = mem_attn_bwd(sv["pe"], p["mem_q_norm"], sv["mk"], sv["mv"], doe, bl, s, layer, tag)
    dwk, dwv, dmg, dmkw = mem_prep_bwd(mem2, p["mem_norm_g"], p["mem_w_kv"], p["mem_k_norm"], dmk, dmv, bl, layer, tag)
    dpieces = (d_a, dsq, dsk, dsv, d_c, d_d, d_e, dg_gates, d_f)
    dx, dng = inproj_bwd_dx(sv["x2"], p["norm_g"], p["w_in"], dy, dpieces, layer, tag)
    dwin = jnp.concatenate([matmul_acc(sv["ht"], dp, f"{tag}_{nm}") for (nm, _, _), dp in zip(BWD_PIECES, dpieces)], axis=1)
    grads = dict(norm_g=dng, w_in=dwin, fox_f_bias=dbias, fox_q_norm=dqw, fox_k_norm=dkw, lb=dlb, hgrn_out_norm=dwn,
                 pool_wbd=dwbd, pool_scale=dpscale, mem_norm_g=dmg, mem_w_kv=jnp.concatenate([dwk, dwv], axis=1),
                 mem_q_norm=dmqw, mem_k_norm=dmkw, w_out=dwout)
    return dx, grads


def _tile4(w):
    return jnp.tile(w, (1, NH))[:, None, :]


def prepare_params(norm_g, w_in_p, fox_f_bias, fox_q_norm, fox_k_norm, hgrn_lb_logits, hgrn_out_norm, pool_w, pool_scale,
                   mem_norm_g, mem_w_kv, mem_q_norm, mem_k_norm, w_out):
    p1 = jax.nn.sigmoid(hgrn_lb_logits[1] - hgrn_lb_logits[0])
    lb = jnp.stack([jnp.zeros_like(p1), jnp.clip(p1, 0.0, 1.0 - 1e-6)])
    eye = jnp.eye(4, dtype=F32)
    wbd = jnp.einsum("lgcd,gh->lgchd", pool_w, eye).reshape(2, GW, GW)
    return dict(norm_g=norm_g[:, None, :], w_in=w_in_p, fox_f_bias=jnp.pad(fox_f_bias, ((0, 0), (0, 124)))[:, None, :],
                fox_q_norm=_tile4(fox_q_norm), fox_k_norm=_tile4(fox_k_norm), lb=lb[:, None, :],
                hgrn_out_norm=hgrn_out_norm[:, None, :], pool_wbd=wbd, pool_scale=pool_scale[:, None, :],
                mem_norm_g=mem_norm_g[:, None, :], mem_w_kv=mem_w_kv, mem_q_norm=_tile4(mem_q_norm),
                mem_k_norm=_tile4(mem_k_norm), w_out=w_out)


def local_step(x, mem, tgt, p):
    bl, s, _ = x.shape
    x2, mem2, tgt2 = x.reshape(bl * s, D_MODEL), mem.reshape(bl * N_MEM, D_MODEL), tgt.reshape(bl * s, D_MODEL)
    y0, sv0 = layer_fwd(x2, mem2, p, 0, bl, s)
    y1, sv1 = layer_fwd(y0, mem2, p, 1, bl, s)
    lpart, dy = loss_head(y1, tgt2)
    dx1, g1 = layer_bwd(dy, mem2, p, sv1, 1, bl, s)
    dx0, g0 = layer_bwd(dx1, mem2, p, sv0, 0, bl, s)
    return lpart[0, 0], dx0.reshape(bl, s, D_MODEL), g0, g1


_ANY = pl.BlockSpec(memory_space=pl.ANY)


def _me_and_peers():
    x, y, c = lax.axis_index("x"), lax.axis_index("y"), lax.axis_index("c")
    peers = []
    for k in range(1, N_DEV):
        px = 1 - x if (k >> 2) & 1 else x
        py = 1 - y if (k >> 1) & 1 else y
        pc = 1 - c if k & 1 else c
        peers.append(((px, py, pc), 4 * px + 2 * py + pc))
    return 4 * x + 2 * y + c, peers


def all_gather_rows(xs, tag):
    nl, r, c = xs.shape

    def body(x_ref, o_ref, send_sems, recv_sems, local_sem):
        x, y, cc = lax.axis_index("x"), lax.axis_index("y"), lax.axis_index("c")
        me, sibling = (x, y, cc), (x, y, 1 - cc)
        chips = [(1 - x, y), (x, 1 - y), (1 - x, 1 - y)]

        def rows(px, py, pc):
            return o_ref.at[:, pl.ds((4 * px + 2 * py + pc) * r, r), :]

        def copy(k, block, to, src=None):
            return pltpu.make_async_remote_copy(src_ref=rows(*block) if src is None else src, dst_ref=rows(*block),
                                                send_sem=send_sems.at[k], recv_sem=recv_sems.at[k], device_id=to,
                                                device_id_type=pl.DeviceIdType.MESH)

        mine = pltpu.make_async_copy(x_ref, rows(*me), local_sem)
        mine.start()
        first = [copy(0, me, sibling, src=x_ref)] + [copy(1 + j, me, (*chip, cc), src=x_ref) for j, chip in enumerate(chips)]
        for cp in first:
            cp.start()
        passed = [copy(4 + j, (*chip, cc), sibling) for j, chip in enumerate(chips)]
        for j, chip in enumerate(chips):
            copy(1 + j, (*chip, cc), me).wait_recv()
            passed[j].start()
        copy(0, sibling, me).wait_recv()
        for j, chip in enumerate(chips):
            copy(4 + j, (*chip, 1 - cc), me).wait_recv()
        for cp in first + passed:
            cp.wait_send()
        mine.wait()

    return pl.pallas_call(
        body, name=f"all_gather_{tag}", in_specs=[_ANY], out_specs=_ANY, out_shape=_sds((nl, N_DEV * r, c), xs.dtype),
        scratch_shapes=[pltpu.SemaphoreType.DMA((N_DEV - 1,)), pltpu.SemaphoreType.DMA((N_DEV - 1,)), pltpu.SemaphoreType.DMA],
    )(xs)


def exchange_cores(part, tag):
    nl, _, _, r, c = part.shape

    def body(p_ref, theirs_ref, send_sems, recv_sems):
        x, y, cc = lax.axis_index("x"), lax.axis_index("y"), lax.axis_index("c")
        copies = []
        for l in range(nl):
            for q in range(4):
                k = l * 4 + q
                copies.append(pltpu.make_async_remote_copy(
                    src_ref=p_ref.at[l, q, pl.ds(1 - cc, 1)], dst_ref=theirs_ref.at[l, q], send_sem=send_sems.at[k],
                    recv_sem=recv_sems.at[k], device_id=(x, y, 1 - cc), device_id_type=pl.DeviceIdType.MESH))
        for cp in copies:
            cp.start()
        for cp in copies:
            cp.wait()

    nsem = pltpu.SemaphoreType.DMA((nl * 4,))
    return pl.pallas_call(
        body, name=f"exchange_cores_{tag}", in_specs=[_ANY], out_specs=_ANY, out_shape=_sds((nl, 4, 1, r, c), part.dtype),
        scratch_shapes=[nsem, nsem],
    )(part)


def add_core_halves(part5, theirs, core, tag):
    nl, _, _, r, c = part5.shape
    tr = 64 if r % 64 == 0 else 32

    def body(core_ref, a_ref, b_ref, o_ref):
        o_ref[...] = _cast(a_ref[...] + b_ref[...])

    blk = lambda which: pl.BlockSpec((None, None, None, tr, c), lambda l, q, i, cref: (l, q, cref[0] if which else 0, i, 0))
    return pl.pallas_call(
        body, name=f"add_core_halves_{tag}", out_shape=_sds((nl, 4, 1, r, c), _MMT),
        grid_spec=pltpu.PrefetchScalarGridSpec(num_scalar_prefetch=1, grid=(nl, 4, r // tr), in_specs=[blk(True), blk(False)],
                                               out_specs=blk(False)),
        compiler_params=pltpu.CompilerParams(dimension_semantics=("arbitrary",) * 3, vmem_limit_bytes=VMEM_LIMIT_BYTES),
    )(core, part5, theirs)


def exchange_chips(s4, tag):
    nl, _, _, r, c = s4.shape

    def body(s_ref, o_ref, send_sems, recv_sems, local_sem):
        x, y, cc = lax.axis_index("x"), lax.axis_index("y"), lax.axis_index("c")
        local = pltpu.make_async_copy(s_ref.at[:, pl.ds(2 * x + y, 1)], o_ref.at[0], local_sem)
        local.start()
        copies = []
        for k in range(1, 4):
            px = 1 - x if (k >> 1) & 1 else x
            py = 1 - y if k & 1 else y
            copies.append(pltpu.make_async_remote_copy(
                src_ref=s_ref.at[:, pl.ds(2 * px + py, 1)], dst_ref=o_ref.at[k], send_sem=send_sems.at[k - 1],
                recv_sem=recv_sems.at[k - 1], device_id=(px, py, cc), device_id_type=pl.DeviceIdType.MESH))
        for cp in copies:
            cp.start()
        for cp in copies:
            cp.wait()
        local.wait()

    return pl.pallas_call(
        body, name=f"exchange_chips_{tag}", in_specs=[_ANY], out_specs=_ANY, out_shape=_sds((4, nl, 1, 1, r, c), s4.dtype),
        scratch_shapes=[pltpu.SemaphoreType.DMA((3,)), pltpu.SemaphoreType.DMA((3,)), pltpu.SemaphoreType.DMA],
    )(s4)


def exchange_row_blocks(part, tag):
    nl, r8, c = part.shape
    r = r8 // N_DEV

    def body(p_ref, o_ref, send_sems, recv_sems, local_sem):
        me, peers = _me_and_peers()
        rows = lambda idx: p_ref.at[:, pl.ds(idx * r, r), :]
        mine = pltpu.make_async_copy(rows(me), o_ref.at[0], local_sem)
        mine.start()
        copies = [pltpu.make_async_remote_copy(src_ref=rows(idx), dst_ref=o_ref.at[k + 1], send_sem=send_sems.at[k],
                                               recv_sem=recv_sems.at[k], device_id=dev, device_id_type=pl.DeviceIdType.MESH)
                  for k, (dev, idx) in enumerate(peers)]
        for cp in copies:
            cp.start()
        for cp in copies:
            cp.wait()
        mine.wait()

    return pl.pallas_call(
        body, name=f"exchange_{tag}", in_specs=[_ANY], out_specs=_ANY, out_shape=_sds((N_DEV, nl, r, c), part.dtype),
        scratch_shapes=[pltpu.SemaphoreType.DMA((N_DEV - 1,)), pltpu.SemaphoreType.DMA((N_DEV - 1,)), pltpu.SemaphoreType.DMA],
    )(part)


def _row_tile(rows):
    if rows <= 512 and rows % 64:
        return rows
    for t in (64, 40, 32, 16, 8):
        if rows % t == 0:
            return t
    return rows


def sum_slots(slots, tag):
    ns, rows, c = slots.shape
    tr = _row_tile(rows)

    def body(s_ref, o_ref):
        acc = s_ref[0].astype(F32)
        for k in range(1, ns):
            acc = acc + s_ref[k].astype(F32)
        o_ref[...] = acc

    return _call(body, f"sum_slots_{tag}", (rows // tr,), [pl.BlockSpec((ns, tr, c), lambda i: (0, i, 0))],
                 pl.BlockSpec((tr, c), lambda i: (i, 0)), _sds((rows, c)))(slots)


def _adamw(w, g, m, v):
    m = ADAM_B1 * m + (1.0 - ADAM_B1) * g
    v = ADAM_B2 * v + (1.0 - ADAM_B2) * (g * g)
    m_hat = m / (1.0 - ADAM_B1 ** ADAM_STEP)
    v_hat = v / (1.0 - ADAM_B2 ** ADAM_STEP)
    delta = -ADAM_LR * (m_hat / (jnp.sqrt(v_hat) + ADAM_EPS) + ADAM_WD * w)
    return delta, m, v


def adam_update(w, m, v, g, tag, slots=False):
    rows, c = w.shape
    tr = _row_tile(rows)
    ns = g.shape[0] if slots else 0

    def body(w_ref, m_ref, v_ref, g_ref, go_ref, d_ref, mo_ref, vo_ref):
        if slots:
            g = g_ref[0].astype(F32)
            for k in range(1, ns):
                g = g + g_ref[k].astype(F32)
        else:
            g = g_ref[...]
        d, mn, vn = _adamw(w_ref[...], g, m_ref[...], v_ref[...])
        go_ref[...] = g
        d_ref[...] = d
        mo_ref[...] = mn
        vo_ref[...] = vn

    blk = pl.BlockSpec((tr, c), lambda i: (i, 0))
    gspec = pl.BlockSpec((ns, tr, c), lambda i: (0, i, 0)) if slots else blk
    return _call(body, f"adam_{tag}", (rows // tr,), [blk, blk, blk, gspec], [blk] * 4, [_sds((rows, c))] * 4)(w, m, v, g)


_SMALL = (("norm_g", (2, 1024)), ("fox_f_bias", (2, 4)), ("fox_q_norm", (2, 64)), ("fox_k_norm", (2, 64)),
          ("hgrn_lb_logits", (2, 256)), ("hgrn_out_norm", (2, 256)), ("pool_w", (2, 4, 64, 64)), ("pool_scale", (2, 256)),
          ("mem_norm_g", (2, 1024)), ("mem_q_norm", (2, 64)), ("mem_k_norm", (2, 64)))
_SLAB_ROWS = 312


def pack_small(d):
    flat = jnp.concatenate([d[n].reshape(-1) for n, _ in _SMALL])
    return jnp.pad(flat, (0, _SLAB_ROWS * 128 - flat.shape[0])).reshape(_SLAB_ROWS, 128)


def unpack_small(slab):
    flat, out, off = slab.reshape(-1), {}, 0
    for n, shp in _SMALL:
        size = 1
        for e in shp:
            size *= e
        out[n] = flat[off:off + size].reshape(shp)
        off += size
    return out


def small_grads(g0, g1, lb_logits):
    st = lambda f: jnp.stack([f(g0), f(g1)])
    heads = lambda a: a.reshape(NH, HD).sum(0)
    p1 = jax.nn.sigmoid(lb_logits[1] - lb_logits[0])
    inside = (p1 > 0.0) & (p1 < 1.0 - 1e-6)
    dl1 = jnp.where(inside, g1["lb"][0] * p1 * (1.0 - p1), 0.0)
    diag = lambda a: jnp.stack([a.reshape(4, HD, 4, HD)[i, :, i, :] for i in range(4)])
    return dict(norm_g=st(lambda g: g["norm_g"][0]), fox_f_bias=st(lambda g: g["fox_f_bias"][0, :NH]),
                fox_q_norm=st(lambda g: heads(g["fox_q_norm"])), fox_k_norm=st(lambda g: heads(g["fox_k_norm"])),
                hgrn_lb_logits=jnp.stack([-dl1, dl1]), hgrn_out_norm=st(lambda g: g["hgrn_out_norm"][0]),
                pool_w=st(lambda g: diag(g["pool_wbd"])), pool_scale=st(lambda g: g["pool_scale"][0]),
                mem_norm_g=st(lambda g: g["mem_norm_g"][0]), mem_q_norm=st(lambda g: heads(g["mem_q_norm"])),
                mem_k_norm=st(lambda g: heads(g["mem_k_norm"])))


def kernel(x, mem, norm_g, w_in, fox_f_bias, fox_q_norm, fox_k_norm, hgrn_lb_logits, hgrn_out_norm, pool_w, pool_scale, mem_norm_g, mem_w_kv, mem_q_norm, mem_k_norm, w_out, loss_target, m_norm_g, m_w_in, m_fox_f_bias, m_fox_q_norm, m_fox_k_norm, m_hgrn_lb_logits, m_hgrn_out_norm, m_pool_w, m_pool_scale, m_mem_norm_g, m_mem_w_kv, m_mem_q_norm, m_mem_k_norm, m_w_out, v_norm_g, v_w_in, v_fox_f_bias, v_fox_q_norm, v_fox_k_norm, v_hgrn_lb_logits, v_hgrn_out_norm, v_pool_w, v_pool_scale, v_mem_norm_g, v_mem_w_kv, v_mem_q_norm, v_mem_k_norm, v_w_out):
    given = dict(norm_g=(norm_g, m_norm_g, v_norm_g), w_in=(w_in, m_w_in, v_w_in), fox_f_bias=(fox_f_bias, m_fox_f_bias, v_fox_f_bias),
                 fox_q_norm=(fox_q_norm, m_fox_q_norm, v_fox_q_norm), fox_k_norm=(fox_k_norm, m_fox_k_norm, v_fox_k_norm),
                 hgrn_lb_logits=(hgrn_lb_logits, m_hgrn_lb_logits, v_hgrn_lb_logits),
                 hgrn_out_norm=(hgrn_out_norm, m_hgrn_out_norm, v_hgrn_out_norm), pool_w=(pool_w, m_pool_w, v_pool_w),
                 pool_scale=(pool_scale, m_pool_scale, v_pool_scale), mem_norm_g=(mem_norm_g, m_mem_norm_g, v_mem_norm_g),
                 mem_w_kv=(mem_w_kv, m_mem_w_kv, v_mem_w_kv), mem_q_norm=(mem_q_norm, m_mem_q_norm, v_mem_q_norm),
                 mem_k_norm=(mem_k_norm, m_mem_k_norm, v_mem_k_norm), w_out=(w_out, m_w_out, v_w_out))
    order = ("norm_g", "w_in", "fox_f_bias", "fox_q_norm", "fox_k_norm", "hgrn_lb_logits", "hgrn_out_norm", "pool_w",
             "pool_scale", "mem_norm_g", "mem_w_kv", "mem_q_norm", "mem_k_norm", "w_out")

    w_in_full = all_gather_rows(_cast(permute_cols(w_in)), "w_in")
    w_out_full = all_gather_rows(_cast(w_out), "w_out")
    w_kv_full = all_gather_rows(_cast(mem_w_kv), "w_kv")
    p = prepare_params(norm_g, w_in_full, fox_f_bias, fox_q_norm, fox_k_norm, hgrn_lb_logits, hgrn_out_norm, pool_w,
                       pool_scale, mem_norm_g, w_kv_full, mem_q_norm, mem_k_norm, w_out_full)

    loss_part, grad_x, g0, g1 = local_step(x, mem, loss_target, p)
    loss = lax.psum(loss_part, ("x", "y", "c"))

    res = {}
    core = lax.axis_index("c").astype(jnp.int32).reshape(1)

    def sharded(name, g2, unperm=False, two_stage=True):
        w, m, v = given[name]
        nl, r, c = w.shape
        cp = g2.shape[-1]
        if two_stage:
            part5 = g2.reshape(nl, 4, 2, r, cp)
            s4 = add_core_halves(part5, exchange_cores(part5, name), core, name)
            slots = exchange_chips(s4, name).reshape(4, nl * r, cp)
        else:
            slots = exchange_row_blocks(g2, name).reshape(N_DEV, nl * r, cp)
        if unperm:
            g = sum_slots(slots, name).reshape(nl, r, cp)
            out = adam_update(w.reshape(nl * r, c), m.reshape(nl * r, c), v.reshape(nl * r, c),
                              unpermute_cols(g).reshape(nl * r, c), name)
        else:
            out = adam_update(w.reshape(nl * r, c), m.reshape(nl * r, c), v.reshape(nl * r, c), slots, name, slots=True)
        res[name] = tuple(o.reshape(nl, r, c) for o in out)

    sharded("w_in", jnp.stack([g0["w_in"], g1["w_in"]]), unperm=True)
    sharded("w_out", jnp.stack([g0["w_out"], g1["w_out"]]))
    sharded("mem_w_kv", _cast(jnp.stack([g0["mem_w_kv"], g1["mem_w_kv"]])), two_stage=False)

    gsmall = pack_small(small_grads(g0, g1, hgrn_lb_logits))
    gathered = all_gather_rows(gsmall[None], "small").reshape(N_DEV, _SLAB_ROWS, 128)
    slabs = adam_update(*[pack_small({n: given[n][j] for n, _ in _SMALL}) for j in range(3)], gathered, "small", slots=True)
    small = [unpack_small(sl) for sl in slabs]
    for n, _ in _SMALL:
        res[n] = tuple(small[j][n] for j in range(4))

    return (loss, grad_x, *[res[n][0] for n in order], *[res[n][1] for n in order], *[res[n][2] for n in order],
            *[res[n][3] for n in order])
```

```python
import functools

import jax
import jax.numpy as jnp
from jax import lax
from jax.experimental import pallas as pl
from jax.experimental.pallas import tpu as pltpu

F32 = jnp.float32
BF = jnp.bfloat16
_MMT = BF

D_MODEL = 1024
GW = 256
HD = 64
NH = 4
CH = 64
N_MEM = 256
D_IN = 4100
D_INP = 4224
D_MIX = 1280
EPS = 1e-6
NEG_BIG = -1e30
LB_FLOOR = 1e-30
SCALE = HD ** -0.5
TQ = 256
N_DEV = 8
VMEM_LIMIT_BYTES = 56 * 1024 * 1024

ADAM_LR = 0.001
ADAM_B1 = 0.9
ADAM_B2 = 0.999
ADAM_EPS = 1e-08
ADAM_WD = 0.01
ADAM_STEP = 10

PIECES = (("A", 0, 768), ("B", 768, 768), ("C", 1536, 768), ("D", 2304, 256), ("E", 2560, 256),
          ("G", 2816, 1280), ("F", 4096, 128))
BWD_PIECES = (("A", 0, 768), ("Bq", 768, 256), ("Bk", 1024, 256), ("Bv", 1280, 256), ("C", 1536, 768),
              ("D", 2304, 256), ("E", 2560, 256), ("G", 2816, 1280), ("F", 4096, 128))
_ORIG = dict(fq=(0, 256), fk=(256, 512), fv=(512, 768), fg=(768, 1024), ff=(1024, 1028), sq=(1028, 1284),
             sk=(1284, 1540), sv=(1540, 1796), sg=(1796, 2052), hq=(2052, 2308), hf=(2308, 2564),
             hi=(2564, 2820), hg=(2820, 3076), pv=(3076, 3332), pg=(3332, 3588), mq=(3588, 3844), mg=(3844, 4100))
_PERM_ORDER = ("fq", "fk", "fv", "sq", "sk", "sv", "hq", "hf", "hi", "pv", "mq", "fg", "sg", "hg", "pg", "mg", "ff")
_ORIG_ORDER = ("fq", "fk", "fv", "fg", "ff", "sq", "sk", "sv", "sg", "hq", "hf", "hi", "hg", "pv", "pg", "mq", "mg")


def permute_cols(w):
    parts = [w[..., _ORIG[n][0]:_ORIG[n][1]] for n in _PERM_ORDER]
    parts.append(jnp.zeros(w.shape[:-1] + (D_INP - D_IN,), w.dtype))
    return jnp.concatenate(parts, axis=-1)


def unpermute_cols(g):
    start, off = {}, 0
    for n in _PERM_ORDER:
        start[n] = off
        off += _ORIG[n][1] - _ORIG[n][0]
    return jnp.concatenate([g[..., start[n]:start[n] + _ORIG[n][1] - _ORIG[n][0]] for n in _ORIG_ORDER], axis=-1)


def _cast(a):
    return a.astype(_MMT)


def _dg(a, b, ca, cb):
    return lax.dot_general(a, b, (((ca,), (cb,)), ((), ())), preferred_element_type=F32)


@jax.custom_vjp
def mm(a, b):
    return _dg(_cast(a), _cast(b), 1, 0)


@jax.custom_vjp
def mm_nt(a, b):
    return _dg(_cast(a), _cast(b), 1, 1)


@jax.custom_vjp
def mm_tn(a, b):
    return _dg(_cast(a), _cast(b), 0, 0)


mm.defvjp(lambda a, b: (mm(a, b), (a, b)),
          lambda r, g: (mm_nt(g, r[1]).astype(r[0].dtype), mm_tn(r[0], g).astype(r[1].dtype)))
mm_nt.defvjp(lambda a, b: (mm_nt(a, b), (a, b)),
             lambda r, g: (mm(g, r[1]).astype(r[0].dtype), mm_tn(g, r[0]).astype(r[1].dtype)))
mm_tn.defvjp(lambda a, b: (mm_tn(a, b), (a, b)),
             lambda r, g: (mm_nt(r[1], g).astype(r[0].dtype), mm(r[0], g).astype(r[1].dtype)))


def _split(a):
    hi = a.astype(_MMT)
    lo = (a - hi.astype(F32)).astype(_MMT)
    return hi, lo


@jax.custom_vjp
def xr(a, c, ct):
    hi, lo = _split(a)
    cc = _cast(c)
    return _dg(hi, cc, 1, 0) + _dg(lo, cc, 1, 0)


@jax.custom_vjp
def xl(c, ct, a):
    hi, lo = _split(a)
    cc = _cast(c)
    return _dg(cc, hi, 1, 0) + _dg(cc, lo, 1, 0)


xr.defvjp(lambda a, c, ct: (xr(a, c, ct), (c, ct)),
          lambda r, g: (xr(g, r[1], r[0]), jnp.zeros_like(r[0]), jnp.zeros_like(r[1])))
xl.defvjp(lambda c, ct, a: (xl(c, ct, a), (c, ct)),
          lambda r, g: (jnp.zeros_like(r[0]), jnp.zeros_like(r[1]), xl(r[1], r[0], g)))


def _iota(shape, dim):
    return lax.broadcasted_iota(jnp.int32, shape, dim)


def _hmask(h, n=GW):
    lane = _iota((1, n), 1)
    return ((lane >= h * HD) & (lane < (h + 1) * HD)).astype(F32)


def _bdmask(n=GW):
    return ((_iota((n, n), 0) >> 6) == (_iota((n, n), 1) >> 6)).astype(F32)


def _tri(n, kind="le"):
    r, c = _iota((n, n), 0), _iota((n, n), 1)
    return {"le": c <= r, "ge": c >= r, "gt": c > r, "lt": c < r}[kind].astype(F32)


def _onehot_lane(h, n=128):
    return (_iota((1, n), 1) == h).astype(F32)


def _logsig(x):
    return jnp.minimum(x, 0.0) - jnp.log1p(jnp.exp(-jnp.abs(x)))


def _sigmoid(x):
    return 0.5 * (jnp.tanh(0.5 * x) + 1.0)


def _silu(x):
    return x * _sigmoid(x)


def _rms(x, g):
    return x * lax.rsqrt(jnp.mean(x * x, axis=-1, keepdims=True) + EPS) * g


def _headrms(x, w, bd64):
    ms = xr(x * x, bd64, bd64)
    return x * lax.rsqrt(ms + EPS) * w


def _call(body, name, grid, in_specs, out_specs, out_shape, scratch=()):
    return pl.pallas_call(
        body, name=name, grid=grid, in_specs=in_specs, out_specs=out_specs, out_shape=out_shape,
        scratch_shapes=list(scratch),
        compiler_params=pltpu.CompilerParams(dimension_semantics=("arbitrary",) * len(grid),
                                             vmem_limit_bytes=VMEM_LIMIT_BYTES))


def _sds(shape, dtype=F32):
    return jax.ShapeDtypeStruct(shape, dtype)


def _acc(ref, val, first):
    @pl.when(first)
    def _():
        ref[...] = val

    @pl.when(jnp.logical_not(first))
    def _():
        ref[...] += val


def inproj_fwd(x2, g, w, layer, tag):
    t = x2.shape[0]

    def body(x_ref, g_ref, w_ref, ht_ref, *outs):
        h = _rms(x_ref[...], g_ref[...])
        hb = _cast(h)
        ht_ref[...] = _cast(h.T)
        for (_, c0, wd), o in zip(PIECES, outs):
            o[...] = _dg(hb, _cast(w_ref[:, c0:c0 + wd]), 1, 0)

    return _call(
        body, f"inproj_fwd_{tag}", (t // TQ,),
        [pl.BlockSpec((TQ, D_MODEL), lambda i: (i, 0)),
         pl.BlockSpec((None, 1, D_MODEL), lambda i: (layer, 0, 0)),
         pl.BlockSpec((None, D_MODEL, D_INP), lambda i: (layer, 0, 0))],
        [pl.BlockSpec((D_MODEL, TQ), lambda i: (0, i))] + [pl.BlockSpec((TQ, wd), lambda i: (i, 0)) for _, _, wd in PIECES],
        [_sds((D_MODEL, t), _MMT)] + [_sds((t, wd)) for _, _, wd in PIECES],
    )(x2, g, w)


def inproj_bwd_dx(x2, g, w, dy, dpieces, layer, tag):
    t = x2.shape[0]

    def body(x_ref, g_ref, w_ref, dy_ref, *rest):
        dps, (dx_ref, dg_ref) = rest[:len(BWD_PIECES)], rest[len(BWD_PIECES):]
        dh = None
        for (_, c0, wd), dp in zip(BWD_PIECES, dps):
            part = _dg(_cast(dp[...]), _cast(w_ref[:, c0:c0 + wd]), 1, 1)
            dh = part if dh is None else dh + part
        _, vjp = jax.vjp(_rms, x_ref[...], g_ref[...])
        dx, dg = vjp(dh)
        dx_ref[...] = dy_ref[...] + dx
        _acc(dg_ref, dg, pl.program_id(0) == 0)

    return _call(
        body, f"inproj_bwd_dx_{tag}", (t // TQ,),
        [pl.BlockSpec((TQ, D_MODEL), lambda i: (i, 0)),
         pl.BlockSpec((None, 1, D_MODEL), lambda i: (layer, 0, 0)),
         pl.BlockSpec((None, D_MODEL, D_INP), lambda i: (layer, 0, 0)),
         pl.BlockSpec((TQ, D_MODEL), lambda i: (i, 0))] + [pl.BlockSpec((TQ, wd), lambda i: (i, 0)) for _, _, wd in BWD_PIECES],
        [pl.BlockSpec((TQ, D_MODEL), lambda i: (i, 0)), pl.BlockSpec((1, D_MODEL), lambda i: (0, 0))],
        [_sds((t, D_MODEL)), _sds((1, D_MODEL))],
    )(x2, g, w, dy, *dpieces)


def matmul_acc(at, b, tag):
    m, t = at.shape
    n = b.shape[1]
    tn = {1280: 640, 768: 768}.get(n, n)
    tk = 2048 if t % 2048 == 0 else (512 if t % 512 == 0 else TQ)

    def body(a_ref, b_ref, o_ref):
        _acc(o_ref, _dg(_cast(a_ref[...]), _cast(b_ref[...]), 1, 0), pl.program_id(1) == 0)

    return _call(
        body, f"matmul_acc_{tag}", (n // tn, t // tk),
        [pl.BlockSpec((m, tk), lambda j, i: (0, i)), pl.BlockSpec((tk, tn), lambda j, i: (i, j))],
        pl.BlockSpec((m, tn), lambda j, i: (0, j)),
        _sds((m, n)),
    )(at, b)


def _fox_prep_fn(q, k, ff, qw, kw, bias, carry, bd64, tri, trit, last):
    qn = _headrms(q, qw, bd64)
    kn = _headrms(k, kw, bd64)
    lf = _logsig(ff + bias)
    c = xl(tri, trit, lf) + carry
    return qn, kn, c, jnp.sum(c * last, axis=0, keepdims=True)


def _prep_consts():
    return _bdmask() * (1.0 / HD), _tri(TQ), _tri(TQ, "ge"), (_iota((TQ, 1), 0) == TQ - 1).astype(F32)


def fox_prep_fwd(pa, pf, qw, kw, bias, bl, s, layer, tag):
    nq = s // TQ

    def body(q_ref, k_ref, v_ref, f_ref, qw_ref, kw_ref, b_ref, qn_ref, kn_ref, vb_ref, cq_ref, ck_ref, carry):
        @pl.when(pl.program_id(1) == 0)
        def _():
            carry[...] = jnp.zeros_like(carry)

        qn, kn, c, cl = _fox_prep_fn(q_ref[...], k_ref[...], f_ref[...], qw_ref[...], kw_ref[...], b_ref[...],
                                     carry[...], *_prep_consts())
        carry[...] = cl
        qn_ref[...] = _cast(qn)
        kn_ref[...] = _cast(kn)
        vb_ref[...] = _cast(v_ref[...])
        cq_ref[...] = c
        ck_ref[...] = c.T[0:8, :]

    tok = lambda j: pl.BlockSpec((TQ, GW), lambda b, i: (b * nq + i, j))
    par = lambda n: pl.BlockSpec((None, 1, n), lambda b, i: (layer, 0, 0))
    return _call(
        body, f"fox_prep_fwd_{tag}", (bl, nq),
        [tok(0), tok(1), tok(2), pl.BlockSpec((TQ, 128), lambda b, i: (b * nq + i, 0)), par(GW), par(GW), par(128)],
        [tok(0), tok(0), tok(0), pl.BlockSpec((TQ, 128), lambda b, i: (b * nq + i, 0)),
         pl.BlockSpec((None, 8, TQ), lambda b, i: (b, 0, i))],
        [_sds((bl * s, GW), _MMT)] * 3 + [_sds((bl * s, 128)), _sds((bl, 8, s))],
        [pltpu.VMEM((1, 128), F32)],
    )(pa, pa, pa, pf, qw, kw, bias)


def fox_prep_bwd(pa, pf, qw, kw, bias, cq, dqn, dkn, dv, dck, bl, s, layer, tag):
    nq = s // TQ

    def body(q_ref, k_ref, f_ref, qw_ref, kw_ref, b_ref, cq_ref, cprev_ref, dqn_ref, dkn_ref, dv_ref, dck_ref,
             da_ref, df_ref, dqw_ref, dkw_ref, db_ref, dcarry):
        i = pl.program_id(1)
        first = jnp.logical_and(pl.program_id(0) == 0, i == 0)

        @pl.when(i == 0)
        def _():
            dcarry[...] = jnp.zeros_like(dcarry)

        last = (_iota((TQ, 1), 0) == TQ - 1).astype(F32)
        carry_in = jnp.where(i == nq - 1, 0.0, jnp.sum(cprev_ref[...] * last, axis=0, keepdims=True))
        consts = _prep_consts()
        _, vjp = jax.vjp(lambda *a: _fox_prep_fn(*a, *consts), q_ref[...], k_ref[...], f_ref[...], qw_ref[...],
                         kw_ref[...], b_ref[...], carry_in)
        dc = dck_ref[...].T
        dq, dk, dff, dqw, dkw, dbias, dcin = vjp((dqn_ref[...], dkn_ref[...], dc, dcarry[...]))
        dcarry[...] = dcin
        da_ref[:, 0:GW] = dq
        da_ref[:, GW:2 * GW] = dk
        da_ref[:, 2 * GW:3 * GW] = dv_ref[...]
        df_ref[...] = dff
        _acc(dqw_ref, dqw, first)
        _acc(dkw_ref, dkw, first)
        _acc(db_ref, dbias, first)

    rv = lambda b, i: b * nq + (nq - 1 - i)
    tok = lambda j: pl.BlockSpec((TQ, GW), lambda b, i: (rv(b, i), j))
    tok0 = pl.BlockSpec((TQ, GW), lambda b, i: (rv(b, i), 0))
    t128 = pl.BlockSpec((TQ, 128), lambda b, i: (rv(b, i), 0))
    prev = pl.BlockSpec((TQ, 128), lambda b, i: (jnp.maximum(rv(b, i) - 1, 0), 0))
    par = lambda n: pl.BlockSpec((None, 1, n), lambda b, i: (layer, 0, 0))
    acc = lambda n: pl.BlockSpec((1, n), lambda b, i: (0, 0))
    return _call(
        body, f"fox_prep_bwd_{tag}", (bl, nq),
        [tok(0), tok(1), t128, par(GW), par(GW), par(128), t128, prev, tok0, tok0, tok0,
         pl.BlockSpec((None, 128, TQ), lambda b, i: (b, 0, nq - 1 - i))],
        [pl.BlockSpec((TQ, 3 * GW), lambda b, i: (rv(b, i), 0)), t128, acc(GW), acc(GW), acc(128)],
        [_sds((bl * s, 3 * GW)), _sds((bl * s, 128)), _sds((1, GW)), _sds((1, GW)), _sds((1, 128))],
        [pltpu.VMEM((1, 128), F32)],
    )(pa, pa, pf, qw, kw, bias, cq, cq, dqn, dkn, dv, dck)


def _lane_pick(x, h):
    return jnp.sum(x * _onehot_lane(h), axis=-1, keepdims=True)


TA_BIG = 256
TA_SB_BWD = 128
TK_FOX = 512
TK_SB = 256


def _stack_heads(x, scale=1.0):
    return _cast(jnp.concatenate([x * (_hmask(h) * scale) for h in range(NH)], axis=0))


def _stack_cols(x):
    return jnp.concatenate([_lane_pick(x, h) for h in range(NH)], axis=0)


def _spread_heads(col):
    ta = col.shape[0] // NH
    return sum(col[h * ta:(h + 1) * ta] * _hmask(h) for h in range(NH))


def _lanes_cat(w):
    ta = w.shape[0] // NH
    return jnp.concatenate([w[h * ta:(h + 1) * ta] for h in range(NH)], axis=1)


def _mask_stack(x):
    return _cast(jnp.concatenate([x * _hmask(h).astype(x.dtype) for h in range(NH)], axis=0))


def _stack_rows(i, ta):
    return i * ta + (_iota((NH * ta, 1), 0) & (ta - 1))


def _n_key_tiles(i, tk, ta):
    return lax.shift_right_logical(i * ta, tk.bit_length() - 1) + 1


def fox_attn_fwd(qn, kn, vb, cq, ck, bl, s, tag):
    TA, TK = min(TA_BIG, s), min(TK_FOX, s)
    nq, SROWS = s // TA, NH * TA

    def body(q_ref, k_ref, v_ref, cq_ref, ck_ref, o_ref, lse_ref, acc, vst):
        i = pl.program_id(1)

        @pl.when(i == 0)
        def _():
            _fill_stacked(vst, v_ref, s, TK)

        qs = _stack_heads(q_ref[...].astype(F32), SCALE)
        cqs = _stack_cols(cq_ref[...])
        row = _stack_rows(i, TA)
        acc[...] = jnp.zeros_like(acc)

        def step(j, ml):
            m, l = ml
            ks = pl.ds(pl.multiple_of(j * TK, TK), TK)
            ckb = jnp.concatenate([jnp.broadcast_to(ck_ref[h:h + 1, ks], (TA, TK)) for h in range(NH)], axis=0)
            sc = _dg(qs, k_ref[ks, :], 1, 1) + cqs - ckb
            col = j * TK + _iota((1, TK), 1)
            sc = jnp.where(col <= row, sc, NEG_BIG)
            m_new = jnp.maximum(m, jnp.max(sc, axis=-1, keepdims=True))
            alpha = jnp.exp(m - m_new)
            p = jnp.exp(sc - m_new)
            vs = vst[pl.ds(pl.multiple_of(j * NH * TK, NH * TK), NH * TK), :]
            acc[...] = _spread_heads(alpha) * acc[...] + _dg(_lanes_cat(_cast(p)), vs, 1, 0)
            return m_new, alpha * l + jnp.sum(p, axis=-1, keepdims=True)

        m, l = lax.fori_loop(0, _n_key_tiles(i, TK, TA), step, (jnp.full((SROWS, 1), NEG_BIG, F32), jnp.zeros((SROWS, 1), F32)))
        o_ref[...] = acc[...] / _spread_heads(l)
        lse_h = m + jnp.log(l)
        lse_ref[...] = sum(lse_h[h * TA:(h + 1) * TA] * _onehot_lane(h) for h in range(NH))

    tok = pl.BlockSpec((TA, GW), lambda b, i: (b * nq + i, 0))
    seq = pl.BlockSpec((s, GW), lambda b, i: (b, 0))
    t128 = pl.BlockSpec((TA, 128), lambda b, i: (b * nq + i, 0))
    return _call(
        body, f"fox_attn_fwd_{tag}", (bl, nq),
        [tok, seq, seq, t128, pl.BlockSpec((None, 8, s), lambda b, i: (b, 0, 0))],
        [tok, t128], [_sds((bl * s, GW)), _sds((bl * s, 128))],
        [pltpu.VMEM((TA, GW), F32), pltpu.VMEM((NH * s, GW), _MMT)],
    )(qn, kn, vb, cq, ck)


def fox_attn_bwd(qn, kn, vb, cq, ck, lse, do, bl, s, tag):
    TA, TK = min(TA_BIG, s), min(TK_FOX, s)
    nq, SROWS = s // TA, NH * TA

    def body(q_ref, k_ref, v_ref, cq_ref, ck_ref, lse_ref, do_ref, dq_ref, dk_ref, dv_ref, dck_ref, dqa, p_s, dp_s, kst):
        i = pl.program_id(1)

        @pl.when(i == 0)
        def _():
            dk_ref[...] = jnp.zeros_like(dk_ref)
            dv_ref[...] = jnp.zeros_like(dv_ref)
            dck_ref[...] = jnp.zeros_like(dck_ref)
            _fill_stacked(kst, k_ref, s, TK)

        qs = _stack_heads(q_ref[...].astype(F32), SCALE)
        dos = _stack_heads(do_ref[...])
        cqs, lses = _stack_cols(cq_ref[...]), _stack_cols(lse_ref[...])
        row = _stack_rows(i, TA)
        dqa[...] = jnp.zeros_like(dqa)
        nk = _n_key_tiles(i, TK, TA)

        def probs(j, delta):
            ks = pl.ds(pl.multiple_of(j * TK, TK), TK)
            ckb = jnp.concatenate([jnp.broadcast_to(ck_ref[h:h + 1, ks], (TA, TK)) for h in range(NH)], axis=0)
            sc = _dg(qs, k_ref[ks, :], 1, 1) + cqs - ckb
            col = j * TK + _iota((1, TK), 1)
            p = jnp.where(col <= row, jnp.exp(sc - lses), 0.0)
            dp = _dg(dos, v_ref[ks, :], 1, 1)
            p_s[:, ks] = p
            dp_s[:, ks] = dp
            return delta + jnp.sum(p * dp, axis=-1, keepdims=True)

        delta = lax.fori_loop(0, nk, probs, jnp.zeros((SROWS, 1), F32))

        def step(j, carry):
            ks = pl.ds(pl.multiple_of(j * TK, TK), TK)
            p = p_s[:, ks]
            ds = p * (dp_s[:, ks] - delta)
            dsb = _cast(ds)
            dqa[...] += _dg(_lanes_cat(dsb), kst[pl.ds(pl.multiple_of(j * NH * TK, NH * TK), NH * TK), :], 1, 0) * SCALE
            dk_ref[ks, :] += _dg(dsb, qs, 0, 0)
            dv_ref[ks, :] += _dg(_cast(p), dos, 0, 0)
            for h in range(NH):
                dck_ref[h:h + 1, ks] -= jnp.sum(ds[h * TA:(h + 1) * TA], axis=0, keepdims=True)
            return carry

        lax.fori_loop(0, nk, step, 0)
        dq_ref[...] = dqa[...]

    tok = pl.BlockSpec((TA, GW), lambda b, i: (b * nq + i, 0))
    seq = pl.BlockSpec((s, GW), lambda b, i: (b, 0))
    t128 = pl.BlockSpec((TA, 128), lambda b, i: (b * nq + i, 0))
    return _call(
        body, f"fox_attn_bwd_{tag}", (bl, nq),
        [tok, seq, seq, t128, pl.BlockSpec((None, 8, s), lambda b, i: (b, 0, 0)), t128, tok],
        [tok, seq, seq, pl.BlockSpec((None, 128, s), lambda b, i: (b, 0, 0))],
        [_sds((bl * s, GW)), _sds((bl * s, GW)), _sds((bl * s, GW)), _sds((bl, 128, s))],
        [pltpu.VMEM((TA, GW), F32), pltpu.VMEM((SROWS, s), F32), pltpu.VMEM((SROWS, s), F32), pltpu.VMEM((NH * s, GW), _MMT)],
    )(qn, kn, vb, cq, ck, lse, do)


def _sb_block(qh, kb, valid, upper, r_carry):
    z = _dg(qh, kb, 1, 1)
    ls = _logsig(z)
    lom = ls - z if valid is None else jnp.where(valid, ls - z, 0.0)
    between = xr(lom, upper, upper) + r_carry
    w = jnp.exp(ls + between)
    return ls, lom, (w if valid is None else jnp.where(valid, w, 0.0))


def _fill_stacked(dst, src_ref, s, tk):
    for j in range(s // tk):
        dst[j * NH * tk:(j + 1) * NH * tk, :] = _mask_stack(src_ref[j * tk:(j + 1) * tk, :])


def sb_attn_fwd(pb, bl, s, tag):
    TA, TK = TA_BIG, TK_SB
    nq, SROWS = s // TA, NH * TA

    def body(q_ref, k_ref, v_ref, o_ref, acc, vst):
        i = pl.program_id(1)

        @pl.when(i == 0)
        def _():
            _fill_stacked(vst, v_ref, s, TK)

        qs = _stack_heads(q_ref[...], SCALE)
        upper = _tri(TK, "lt")
        last = _n_key_tiles(i, TK, TA) - 1

        def step(j, r, valid):
            ks = pl.ds(pl.multiple_of(j * TK, TK), TK)
            _, lom, w = _sb_block(qs, _cast(k_ref[ks, :]), valid, upper, r)
            acc[...] += _dg(_lanes_cat(_cast(w)), vst[pl.ds(pl.multiple_of(j * NH * TK, NH * TK), NH * TK), :], 1, 0)
            return r + jnp.sum(lom, axis=-1, keepdims=True)

        acc[...] = jnp.zeros_like(acc)
        r = step(last, jnp.zeros((SROWS, 1), F32), last * TK + _iota((1, TK), 1) < _stack_rows(i, TA))
        lax.fori_loop(0, last, lambda jj, r: step(last - 1 - jj, r, None), r)
        o_ref[...] = acc[...]

    tok = lambda j: pl.BlockSpec((TA, GW), lambda b, i: (b * nq + i, j))
    seq = lambda j: pl.BlockSpec((s, GW), lambda b, i: (b, j))
    return _call(
        body, f"sb_attn_fwd_{tag}", (bl, nq), [tok(0), seq(1), seq(2)],
        pl.BlockSpec((TA, GW), lambda b, i: (b * nq + i, 0)), _sds((bl * s, GW)),
        [pltpu.VMEM((TA, GW), F32), pltpu.VMEM((NH * s, GW), _MMT)],
    )(pb, pb, pb)


def sb_attn_bwd(pb, do, bl, s, tag):
    TA, TK = TA_SB_BWD, TK_SB
    nq, SROWS = s // TA, NH * TA

    def body(q_ref, k_ref, v_ref, do_ref, dq_ref, dk_ref, dv_ref, dqa, ls_s, lom_s, w_s, g_s, kst):
        i = pl.program_id(1)

        @pl.when(i == 0)
        def _():
            dk_ref[...] = jnp.zeros_like(dk_ref)
            dv_ref[...] = jnp.zeros_like(dv_ref)
            _fill_stacked(kst, k_ref, s, TK)

        qs = _stack_heads(q_ref[...], SCALE)
        dos = _stack_heads(do_ref[...])
        upper = _tri(TK, "lt")
        before = _tri(TK, "gt")
        dqa[...] = jnp.zeros_like(dqa)
        last = _n_key_tiles(i, TK, TA) - 1
        diag = last * TK + _iota((1, TK), 1) < _stack_rows(i, TA)

        def weights(j, r, valid):
            ks = pl.ds(pl.multiple_of(j * TK, TK), TK)
            ls, lom, w = _sb_block(qs, _cast(k_ref[ks, :]), valid, upper, r)
            ls_s[:, ks] = ls
            lom_s[:, ks] = lom
            w_s[:, ks] = _cast(w)
            g_s[:, ks] = _dg(dos, _cast(v_ref[ks, :]), 1, 1) * w
            return r + jnp.sum(lom, axis=-1, keepdims=True)

        r = weights(last, jnp.zeros((SROWS, 1), F32), diag)
        lax.fori_loop(0, last, lambda jj, r: weights(last - 1 - jj, r, None), r)

        def step(j, cpre, valid):
            ks = pl.ds(pl.multiple_of(j * TK, TK), TK)
            g = g_s[:, ks]
            pre = cpre + xr(g, before, before)
            dz = g * jnp.exp(lom_s[:, ks]) - jnp.exp(ls_s[:, ks]) * pre
            dzb = _cast(dz if valid is None else jnp.where(valid, dz, 0.0))
            dqa[...] += _dg(_lanes_cat(dzb), kst[pl.ds(pl.multiple_of(j * NH * TK, NH * TK), NH * TK), :], 1, 0) * SCALE
            dk_ref[ks, :] += _dg(dzb, qs, 0, 0)
            dv_ref[ks, :] += _dg(w_s[:, ks], dos, 0, 0)
            return cpre + jnp.sum(g, axis=-1, keepdims=True)

        cpre = lax.fori_loop(0, last, lambda j, c: step(j, c, None), jnp.zeros((SROWS, 1), F32))
        step(last, cpre, diag)
        dq_ref[...] = dqa[...]

    tok = lambda j: pl.BlockSpec((TA, GW), lambda b, i: (b * nq + i, j))
    seq = lambda j: pl.BlockSpec((s, GW), lambda b, i: (b, j))
    return _call(
        body, f"sb_attn_bwd_{tag}", (bl, nq), [tok(0), seq(1), seq(2), tok(0)],
        [tok(0), seq(0), seq(0)], [_sds((bl * s, GW))] * 3,
        [pltpu.VMEM((TA, GW), F32), pltpu.VMEM((SROWS, s), F32), pltpu.VMEM((SROWS, s), F32),
         pltpu.VMEM((SROWS, s), _MMT), pltpu.VMEM((SROWS, s), F32), pltpu.VMEM((NH * s, GW), _MMT)],
    )(pb, pb, pb, do)


def _hgrn_consts():
    r, c = _iota((CH, CH), 0), _iota((CH, CH), 1)
    rr = _iota((CH, 1), 0)
    tri = (c <= r).astype(F32)
    lv = []
    for m in (8, 4, 2, 1):
        up = ((rr & (2 * m - 1)) >= m).astype(F32)
        lo = 1.0 - up
        selq = (((r & (2 * m - 1)) >= m) & (c == (r & ~(m - 1)) - 1)).astype(F32)
        selk = (((r & (2 * m - 1)) < m) & (c == (r & ~(m - 1)) + m - 1)).astype(F32)
        pm = (((r & ~(2 * m - 1)) == (c & ~(2 * m - 1))) & ((r & (2 * m - 1)) >= m) & ((c & (2 * m - 1)) < m)).astype(F32)
        lv.append((up, lo, selq, selq.T, selk, selk.T, jnp.concatenate([pm] * NH, axis=0)))
    hm4 = lambda n: (((_iota((NH, 1, n), 2) & (GW - 1)) >> 6) == _iota((NH, 1, n), 0)).astype(F32)
    return dict(tri=tri, trit=tri.T, rr=rr, lv=lv, bd=_bdmask(), bd64=_bdmask() * (1.0 / HD),
                hm4={GW: hm4(GW), 3 * GW: hm4(3 * GW)})


def _hgrn_chunk_fn(hq, hf, hi, lb, wn, st, cs):
    q = _silu(hq)
    log_lb = jnp.log(jnp.maximum(lb, LB_FLOOR))
    a, bb = log_lb, jnp.log1p(-lb) + _logsig(hf)
    g = jnp.maximum(a, bb) + jnp.log1p(jnp.exp(-jnp.abs(a - bb)))
    k = (1.0 - lb) * _sigmoid(-hf)
    v = hi
    rr = cs["rr"]
    b = xl(cs["tri"], cs["trit"], g)
    row_of = lambda n: jnp.sum(b * (rr == n).astype(F32), axis=0, keepdims=True)
    o = mm_nt(q * jnp.exp(b), st)
    qs, ks = [], []
    for ib in (1, 2, 3):
        ref = row_of(16 * ib - 1)
        inq = ((rr >= 16 * ib) & (rr < 16 * ib + 16)).astype(F32)
        ink = (rr < 16 * ib).astype(F32)
        qs.append(q * jnp.exp((b - ref) * inq) * inq)
        ks.append(k * jnp.exp((ref - b) * ink) * ink)
    qcat, kcat = jnp.concatenate(qs, axis=1), jnp.concatenate(ks, axis=1)
    lvl = []
    for up, lo, selq, selqt, selk, selkt, pm in cs["lv"]:
        qe = q * jnp.exp((b - xl(selq, selqt, b)) * up) * up
        ke = k * jnp.exp((xl(selk, selkt, b) - b) * lo) * lo
        lvl.append((qe, ke, pm))
    stack = lambda x: (x[None] * cs["hm4"][x.shape[1]]).reshape(NH * CH, x.shape[1])
    a_all = mm_nt(stack(qcat), kcat)
    for qe, ke, pm4 in lvl:
        a_all = a_all + mm_nt(stack(qe), ke) * pm4
    o = o + jnp.sum(mm(a_all, v).reshape(NH, CH, GW) * cs["hm4"][GW], axis=0)
    o = o + xr(q * k, cs["bd"], cs["bd"]) * v
    b_last = row_of(CH - 1)
    st_new = st * jnp.exp(b_last) + mm_tn(v, k * jnp.exp(b_last - b)) * cs["bd"]
    return _headrms(o, wn, cs["bd64"]), st_new


def hgrn_fwd(pc, lb, wn, bl, s, layer, tag):
    nc = s // CH

    def body(q_ref, f_ref, i_ref, lb_ref, wn_ref, o_ref, st_ref, st):
        @pl.when(pl.program_id(0) == 0)
        def _():
            st[...] = jnp.zeros_like(st)

        cs = _hgrn_consts()
        for b in range(bl):
            st_ref[b] = st[b]
            o, st_new = _hgrn_chunk_fn(q_ref[b], f_ref[b], i_ref[b], lb_ref[...], wn_ref[...], st[b], cs)
            o_ref[b] = o
            st[b] = st_new

    tok = lambda j: pl.BlockSpec((bl, CH, GW), lambda c: (0, c, j))
    par = pl.BlockSpec((None, 1, GW), lambda c: (layer, 0, 0))
    pc3 = pc.reshape(bl, s, 3 * GW)
    o, states = _call(
        body, f"hgrn_fwd_{tag}", (nc,), [tok(0), tok(1), tok(2), par, par],
        [tok(0), pl.BlockSpec((bl, None, GW, GW), lambda c: (0, c, 0, 0))],
        [_sds((bl, s, GW)), _sds((bl, nc, GW, GW))],
        [pltpu.VMEM((bl, GW, GW), F32)],
    )(pc3, pc3, pc3, lb, wn)
    return o.reshape(bl * s, GW), states


def hgrn_bwd(pc, lb, wn, states, do, bl, s, layer, tag):
    nc = s // CH

    def body(q_ref, f_ref, i_ref, lb_ref, wn_ref, st_ref, do_ref, dc_ref, dlb_ref, dwn_ref, dst):
        c = pl.program_id(0)

        @pl.when(c == 0)
        def _():
            dst[...] = jnp.zeros_like(dst)

        cs = _hgrn_consts()
        dlb_sum = dwn_sum = None
        for b in range(bl):
            _, vjp = jax.vjp(lambda *a: _hgrn_chunk_fn(*a, cs), q_ref[b], f_ref[b], i_ref[b], lb_ref[...],
                             wn_ref[...], st_ref[b])
            dq, df, di, dlb, dwn, dst_in = vjp((do_ref[b], dst[b]))
            dst[b] = dst_in
            dc_ref[b, :, 0:GW] = dq
            dc_ref[b, :, GW:2 * GW] = df
            dc_ref[b, :, 2 * GW:3 * GW] = di
            dlb_sum = dlb if dlb_sum is None else dlb_sum + dlb
            dwn_sum = dwn if dwn_sum is None else dwn_sum + dwn
        _acc(dlb_ref, dlb_sum, c == 0)
        _acc(dwn_ref, dwn_sum, c == 0)

    tok = lambda j: pl.BlockSpec((bl, CH, GW), lambda c: (0, nc - 1 - c, j))
    par = pl.BlockSpec((None, 1, GW), lambda c: (layer, 0, 0))
    acc = pl.BlockSpec((1, GW), lambda c: (0, 0))
    pc3 = pc.reshape(bl, s, 3 * GW)
    dc, dlb, dwn = _call(
        body, f"hgrn_bwd_{tag}", (nc,),
        [tok(0), tok(1), tok(2), par, par, pl.BlockSpec((bl, None, GW, GW), lambda c: (0, nc - 1 - c, 0, 0)), tok(0)],
        [pl.BlockSpec((bl, CH, 3 * GW), lambda c: (0, nc - 1 - c, 0)), acc, acc],
        [_sds((bl, s, 3 * GW)), _sds((1, GW)), _sds((1, GW))],
        [pltpu.VMEM((bl, GW, GW), F32)],
    )(pc3, pc3, pc3, lb, wn, states, do.reshape(bl, s, GW))
    return dc.reshape(bl * s, 3 * GW), dlb, dwn


def _shift_rows(x, k, up):
    n = x.shape[0]
    rr = _iota((n, 1), 0)
    if up:
        return jnp.where(rr < n - k, pltpu.roll(x, n - k, 0), 0.0)
    return jnp.where(rr >= k, pltpu.roll(x, k, 0), 0.0)


def _window_sums(x, up):
    s2 = x + _shift_rows(x, 1, up)
    s4 = s2 + _shift_rows(s2, 2, up)
    s8 = s4 + _shift_rows(s4, 4, up)
    s16 = s8 + _shift_rows(s8, 8, up)
    return s2, s4, s8, s16


def _pool_div(n):
    pos = (_iota((n, 1), 0) + 1).astype(F32)
    return [jnp.minimum(pos, float(w)) for w in (2, 4, 8, 16)]


def _pool_mix(sums, scaled):
    out = None
    for gi, sw in enumerate(sums):
        part = (sw if scaled is None else sw / scaled[gi]) * _hmask(gi)
        out = part if out is None else out + part
    return out


def pool_fwd(pd, wbd, scale, bl, s, layer, tag):
    def body(u_ref, w_ref, sc_ref, o_ref):
        u = u_ref[...]
        pm = _pool_mix(_window_sums(u, False), _pool_div(s)) - u
        o_ref[...] = _dg(_cast(pm), _cast(w_ref[...]), 1, 0) * sc_ref[...]

    seq = pl.BlockSpec((s, GW), lambda b: (b, 0))
    return _call(
        body, f"pool_fwd_{tag}", (bl,),
        [seq, pl.BlockSpec((None, GW, GW), lambda b: (layer, 0, 0)), pl.BlockSpec((None, 1, GW), lambda b: (layer, 0, 0))],
        seq, _sds((bl * s, GW)),
    )(pd, wbd, scale)


def pool_bwd(pd, wbd, scale, do, bl, s, layer, tag):
    def body(u_ref, w_ref, sc_ref, do_ref, du_ref, dw_ref, dsc_ref):
        first = pl.program_id(0) == 0
        u, do = u_ref[...], do_ref[...]
        div = _pool_div(s)
        pm = _pool_mix(_window_sums(u, False), div) - u
        ypre = _dg(_cast(pm), _cast(w_ref[...]), 1, 0)
        dys = do * sc_ref[...]
        _acc(dsc_ref, jnp.sum(do * ypre, axis=0, keepdims=True), first)
        _acc(dw_ref, _dg(_cast(pm), _cast(dys), 0, 0), first)
        dpm = _dg(_cast(dys), _cast(w_ref[...]), 1, 1)
        dsc = [dpm / d for d in div]
        adj = None
        for gi in range(4):
            part = _window_sums(dsc[gi] * _hmask(gi), True)[gi]
            adj = part if adj is None else adj + part
        du_ref[...] = adj - dpm

    seq = pl.BlockSpec((s, GW), lambda b: (b, 0))
    return _call(
        body, f"pool_bwd_{tag}", (bl,),
        [seq, pl.BlockSpec((None, GW, GW), lambda b: (layer, 0, 0)), pl.BlockSpec((None, 1, GW), lambda b: (layer, 0, 0)), seq],
        [seq, pl.BlockSpec((GW, GW), lambda b: (0, 0)), pl.BlockSpec((1, GW), lambda b: (0, 0))],
        [_sds((bl * s, GW)), _sds((GW, GW)), _sds((1, GW))],
    )(pd, wbd, scale, do)


def _mem_prep_fn(mem, g, wk, wv, kw, bd64):
    mn = _rms(mem, g)
    return _headrms(mm(mn, wk), kw, bd64), mm(mn, wv)


def mem_prep_fwd(mem2, g, wkv, kw, bl, layer, tag):
    def body(m_ref, g_ref, wk_ref, wv_ref, kw_ref, k_ref, v_ref):
        k, v = _mem_prep_fn(m_ref[...], g_ref[...], wk_ref[...], wv_ref[...], kw_ref[...], _bdmask() * (1.0 / HD))
        k_ref[...] = k
        v_ref[...] = v

    blk = pl.BlockSpec((N_MEM, GW), lambda b: (b, 0))
    return _call(
        body, f"mem_prep_fwd_{tag}", (bl,),
        [pl.BlockSpec((N_MEM, D_MODEL), lambda b: (b, 0)), pl.BlockSpec((None, 1, D_MODEL), lambda b: (layer, 0, 0)),
         pl.BlockSpec((None, D_MODEL, GW), lambda b: (layer, 0, 0)), pl.BlockSpec((None, D_MODEL, GW), lambda b: (layer, 0, 1)),
         pl.BlockSpec((None, 1, GW), lambda b: (layer, 0, 0))],
        [blk, blk], [_sds((bl * N_MEM, GW))] * 2,
    )(mem2, g, wkv, wkv, kw)


def mem_prep_bwd(mem2, g, wkv, kw, dk, dv, bl, layer, tag):
    def body(m_ref, g_ref, wk_ref, wv_ref, kw_ref, dk_ref, dv_ref, dwk_ref, dwv_ref, dg_ref, dkw_ref):
        first = pl.program_id(0) == 0
        bd64 = _bdmask() * (1.0 / HD)
        _, vjp = jax.vjp(lambda g_, wk, wv, kw_: _mem_prep_fn(m_ref[...], g_, wk, wv, kw_, bd64),
                         g_ref[...], wk_ref[...].astype(F32), wv_ref[...].astype(F32), kw_ref[...])
        dg, dwk, dwv, dkw = vjp((dk_ref[...], dv_ref[...]))
        _acc(dwk_ref, dwk, first)
        _acc(dwv_ref, dwv, first)
        _acc(dg_ref, dg, first)
        _acc(dkw_ref, dkw, first)

    blk = pl.BlockSpec((N_MEM, GW), lambda b: (b, 0))
    return _call(
        body, f"mem_prep_bwd_{tag}", (bl,),
        [pl.BlockSpec((N_MEM, D_MODEL), lambda b: (b, 0)), pl.BlockSpec((None, 1, D_MODEL), lambda b: (layer, 0, 0)),
         pl.BlockSpec((None, D_MODEL, GW), lambda b: (layer, 0, 0)), pl.BlockSpec((None, D_MODEL, GW), lambda b: (layer, 0, 1)),
         pl.BlockSpec((None, 1, GW), lambda b: (layer, 0, 0)), blk, blk],
        [pl.BlockSpec((D_MODEL, GW), lambda b: (0, 0)), pl.BlockSpec((D_MODEL, GW), lambda b: (0, 0)),
         pl.BlockSpec((1, D_MODEL), lambda b: (0, 0)), pl.BlockSpec((1, GW), lambda b: (0, 0))],
        [_sds((D_MODEL, GW)), _sds((D_MODEL, GW)), _sds((1, D_MODEL)), _sds((1, GW))],
    )(mem2, g, wkv, wkv, kw, dk, dv)


def _mem_attn_fn(mq, qw, k, v, bd64):
    qn = _headrms(mq, qw, bd64)
    out = None
    for h in range(NH):
        hm = _hmask(h)
        lg = mm_nt(qn * hm, k) * SCALE
        e = jnp.exp(lg - lax.stop_gradient(jnp.max(lg, axis=-1, keepdims=True)))
        p = e / jnp.sum(e, axis=-1, keepdims=True)
        part = mm(p, v) * hm
        out = part if out is None else out + part
    return out


def mem_attn_fwd(pe, qw, k, v, bl, s, layer, tag):
    nq = s // TQ

    def body(q_ref, qw_ref, k_ref, v_ref, o_ref):
        o_ref[...] = _mem_attn_fn(q_ref[...], qw_ref[...], k_ref[...], v_ref[...], _bdmask() * (1.0 / HD))

    tok = pl.BlockSpec((TQ, GW), lambda b, i: (b * nq + i, 0))
    kv = pl.BlockSpec((N_MEM, GW), lambda b, i: (b, 0))
    return _call(
        body, f"mem_attn_fwd_{tag}", (bl, nq), [tok, pl.BlockSpec((None, 1, GW), lambda b, i: (layer, 0, 0)), kv, kv],
        tok, _sds((bl * s, GW)),
    )(pe, qw, k, v)


def mem_attn_bwd(pe, qw, k, v, do, bl, s, layer, tag):
    nq = s // TQ

    def body(q_ref, qw_ref, k_ref, v_ref, do_ref, dq_ref, dk_ref, dv_ref, dqw_ref):
        i = pl.program_id(1)
        bd64 = _bdmask() * (1.0 / HD)
        _, vjp = jax.vjp(lambda *a: _mem_attn_fn(*a, bd64), q_ref[...], qw_ref[...], k_ref[...], v_ref[...])
        dq, dqw, dk, dv = vjp(do_ref[...])
        dq_ref[...] = dq
        _acc(dk_ref, dk, i == 0)
        _acc(dv_ref, dv, i == 0)
        _acc(dqw_ref, dqw, jnp.logical_and(pl.program_id(0) == 0, i == 0))

    tok = pl.BlockSpec((TQ, GW), lambda b, i: (b * nq + i, 0))
    kv = pl.BlockSpec((N_MEM, GW), lambda b, i: (b, 0))
    return _call(
        body, f"mem_attn_bwd_{tag}", (bl, nq),
        [tok, pl.BlockSpec((None, 1, GW), lambda b, i: (layer, 0, 0)), kv, kv, tok],
        [tok, kv, kv, pl.BlockSpec((1, GW), lambda b, i: (0, 0))],
        [_sds((bl * s, GW)), _sds((bl * N_MEM, GW)), _sds((bl * N_MEM, GW)), _sds((1, GW))],
    )(pe, qw, k, v, do)


def _gate_out_fn(outs, gates, wparts):
    y = None
    for o, g, w in zip(outs, gates, wparts):
        part = mm(o * _silu(g), w)
        y = part if y is None else y + part
    return y


def outproj_fwd(x2, outs, pg, wout, layer, tag):
    t = x2.shape[0]

    def body(x_ref, oa, ob, oc, od, oe, g_ref, w_ref, y_ref):
        outs_ = [r[...] for r in (oa, ob, oc, od, oe)]
        gates = [g_ref[:, j * GW:(j + 1) * GW] for j in range(5)]
        wparts = [w_ref[j * GW:(j + 1) * GW, :] for j in range(5)]
        y_ref[...] = x_ref[...] + _gate_out_fn(outs_, gates, wparts)

    tok = pl.BlockSpec((TQ, GW), lambda i: (i, 0))
    big = pl.BlockSpec((TQ, D_MODEL), lambda i: (i, 0))
    return _call(
        body, f"outproj_fwd_{tag}", (t // TQ,),
        [big] + [tok] * 5 + [pl.BlockSpec((TQ, D_MIX), lambda i: (i, 0)),
                            pl.BlockSpec((None, D_MIX, D_MODEL), lambda i: (layer, 0, 0))],
        big, _sds((t, D_MODEL)),
    )(x2, *outs, pg, wout)


def outproj_bwd(outs, pg, wout, dy, layer, tag):
    t = dy.shape[0]

    def body(oa, ob, oc, od, oe, g_ref, w_ref, dy_ref, da, db, dc, dd, de, dg_ref, dw_ref):
        outs_ = [r[...] for r in (oa, ob, oc, od, oe)]
        gates = [g_ref[:, j * GW:(j + 1) * GW] for j in range(5)]
        wparts = [w_ref[j * GW:(j + 1) * GW, :].astype(F32) for j in range(5)]
        _, vjp = jax.vjp(_gate_out_fn, outs_, gates, wparts)
        douts, dgates, dws = vjp(dy_ref[...])
        for r, val in zip((da, db, dc, dd, de), douts):
            r[...] = val
        first = pl.program_id(0) == 0
        for j in range(5):
            dg_ref[:, j * GW:(j + 1) * GW] = dgates[j]

        @pl.when(first)
        def _():
            for j in range(5):
                dw_ref[j * GW:(j + 1) * GW, :] = dws[j]

        @pl.when(jnp.logical_not(first))
        def _():
            for j in range(5):
                dw_ref[j * GW:(j + 1) * GW, :] += dws[j]

    tok = pl.BlockSpec((TQ, GW), lambda i: (i, 0))
    return _call(
        body, f"outproj_bwd_{tag}", (t // TQ,),
        [tok] * 5 + [pl.BlockSpec((TQ, D_MIX), lambda i: (i, 0)), pl.BlockSpec((None, D_MIX, D_MODEL), lambda i: (layer, 0, 0)),
                     pl.BlockSpec((TQ, D_MODEL), lambda i: (i, 0))],
        [tok] * 5 + [pl.BlockSpec((TQ, D_MIX), lambda i: (i, 0)), pl.BlockSpec((D_MIX, D_MODEL), lambda i: (0, 0))],
        [_sds((t, GW))] * 5 + [_sds((t, D_MIX)), _sds((D_MIX, D_MODEL))],
    )(*outs, pg, wout, dy)


def loss_head(y, tgt):
    t = y.shape[0]

    def body(y_ref, t_ref, l_ref, dy_ref):
        diff = y_ref[...] - t_ref[...]
        dy_ref[...] = diff * (1.0 / D_MODEL)
        part = 0.5 * jnp.sum(jnp.sum(diff * diff, axis=-1, keepdims=True) * (1.0 / D_MODEL), axis=0, keepdims=True)
        _acc(l_ref, jnp.broadcast_to(part, (8, 128)), pl.program_id(0) == 0)

    big = pl.BlockSpec((TQ, D_MODEL), lambda i: (i, 0))
    return _call(body, "loss_head", (t // TQ,), [big, big], [pl.BlockSpec((8, 128), lambda i: (0, 0)), big],
                 [_sds((8, 128)), _sds((t, D_MODEL))])(y, tgt)


def layer_fwd(x2, mem2, p, layer, bl, s):
    tag = f"l{layer}"
    ht, pa, pb, pc, pd, pe, pg, pf = inproj_fwd(x2, p["norm_g"], p["w_in"], layer, tag)
    qn, kn, vb, cq, ck = fox_prep_fwd(pa, pf, p["fox_q_norm"], p["fox_k_norm"], p["fox_f_bias"], bl, s, layer, tag)
    oa, lse = fox_attn_fwd(qn, kn, vb, cq, ck, bl, s, tag)
    ob = sb_attn_fwd(pb, bl, s, tag)
    oc, states = hgrn_fwd(pc, p["lb"], p["hgrn_out_norm"], bl, s, layer, tag)
    od = pool_fwd(pd, p["pool_wbd"], p["pool_scale"], bl, s, layer, tag)
    mk, mv = mem_prep_fwd(mem2, p["mem_norm_g"], p["mem_w_kv"], p["mem_k_norm"], bl, layer, tag)
    oe = mem_attn_fwd(pe, p["mem_q_norm"], mk, mv, bl, s, layer, tag)
    y = outproj_fwd(x2, (oa, ob, oc, od, oe), pg, p["w_out"], layer, tag)
    saved = dict(x2=x2, ht=ht, pa=pa, pb=pb, pc=pc, pd=pd, pe=pe, pg=pg, pf=pf, qn=qn, kn=kn, vb=vb, cq=cq, ck=ck,
                 oa=oa, lse=lse, ob=ob, oc=oc, states=states, od=od, mk=mk, mv=mv, oe=oe)
    return y, saved


def layer_bwd(dy, mem2, p, sv, layer, bl, s):
    tag = f"l{layer}"
    (doa, dob, doc, dod, doe, dg_gates, dwout) = outproj_bwd((sv["oa"], sv["ob"], sv["oc"], sv["od"], sv["oe"]), sv["pg"],
                                                              p["w_out"], dy, layer, tag)
    dqn, dkn, dv, dck = fox_attn_bwd(sv["qn"], sv["kn"], sv["vb"], sv["cq"], sv["ck"], sv["lse"], doa, bl, s, tag)
    d_a, d_f, dqw, dkw, dbias = fox_prep_bwd(sv["pa"], sv["pf"], p["fox_q_norm"], p["fox_k_norm"], p["fox_f_bias"], sv["cq"],
                                             dqn, dkn, dv, dck, bl, s, layer, tag)
    dsq, dsk, dsv = sb_attn_bwd(sv["pb"], dob, bl, s, tag)
    d_c, dlb, dwn = hgrn_bwd(sv["pc"], p["lb"], p["hgrn_out_norm"], sv["states"], doc, bl, s, layer, tag)
    d_d, dwbd, dpscale = pool_bwd(sv["pd"], p["pool_wbd"], p["pool_scale"], dod, bl, s, layer, tag)
    d_e, dmk, dmv, dmqw = mem_attn_bwd(sv["pe"], p["mem_q_norm"], sv["mk"], sv["mv"], doe, bl, s, layer, tag)
    dwk, dwv, dmg, dmkw = mem_prep_bwd(mem2, p["mem_norm_g"], p["mem_w_kv"], p["mem_k_norm"], dmk, dmv, bl, layer, tag)
    dpieces = (d_a, dsq, dsk, dsv, d_c, d_d, d_e, dg_gates, d_f)
    dx, dng = inproj_bwd_dx(sv["x2"], p["norm_g"], p["w_in"], dy, dpieces, layer, tag)
    dwin = jnp.concatenate([matmul_acc(sv["ht"], dp, f"{tag}_{nm}") for (nm, _, _), dp in zip(BWD_PIECES, dpieces)], axis=1)
    grads = dict(norm_g=dng, w_in=dwin, fox_f_bias=dbias, fox_q_norm=dqw, fox_k_norm=dkw, lb=dlb, hgrn_out_norm=dwn,
                 pool_wbd=dwbd, pool_scale=dpscale, mem_norm_g=dmg, mem_w_kv=jnp.concatenate([dwk, dwv], axis=1),
                 mem_q_norm=dmqw, mem_k_norm=dmkw, w_out=dwout)
    return dx, grads


def _tile4(w):
    return jnp.tile(w, (1, NH))[:, None, :]


def prepare_params(norm_g, w_in_p, fox_f_bias, fox_q_norm, fox_k_norm, hgrn_lb_logits, hgrn_out_norm, pool_w, pool_scale,
                   mem_norm_g, mem_w_kv, mem_q_norm, mem_k_norm, w_out):
    p1 = jax.nn.sigmoid(hgrn_lb_logits[1] - hgrn_lb_logits[0])
    lb = jnp.stack([jnp.zeros_like(p1), jnp.clip(p1, 0.0, 1.0 - 1e-6)])
    eye = jnp.eye(4, dtype=F32)
    wbd = jnp.einsum("lgcd,gh->lgchd", pool_w, eye).reshape(2, GW, GW)
    return dict(norm_g=norm_g[:, None, :], w_in=w_in_p, fox_f_bias=jnp.pad(fox_f_bias, ((0, 0), (0, 124)))[:, None, :],
                fox_q_norm=_tile4(fox_q_norm), fox_k_norm=_tile4(fox_k_norm), lb=lb[:, None, :],
                hgrn_out_norm=hgrn_out_norm[:, None, :], pool_wbd=wbd, pool_scale=pool_scale[:, None, :],
                mem_norm_g=mem_norm_g[:, None, :], mem_w_kv=mem_w_kv, mem_q_norm=_tile4(mem_q_norm),
                mem_k_norm=_tile4(mem_k_norm), w_out=w_out)


def local_step(x, mem, tgt, p):
    bl, s, _ = x.shape
    x2, mem2, tgt2 = x.reshape(bl * s, D_MODEL), mem.reshape(bl * N_MEM, D_MODEL), tgt.reshape(bl * s, D_MODEL)
    y0, sv0 = layer_fwd(x2, mem2, p, 0, bl, s)
    y1, sv1 = layer_fwd(y0, mem2, p, 1, bl, s)
    lpart, dy = loss_head(y1, tgt2)
    dx1, g1 = layer_bwd(dy, mem2, p, sv1, 1, bl, s)
    dx0, g0 = layer_bwd(dx1, mem2, p, sv0, 0, bl, s)
    return lpart[0, 0], dx0.reshape(bl, s, D_MODEL), g0, g1


_ANY = pl.BlockSpec(memory_space=pl.ANY)


def _me_and_peers():
    x, y, c = lax.axis_index("x"), lax.axis_index("y"), lax.axis_index("c")
    peers = []
    for k in range(1, N_DEV):
        px = 1 - x if (k >> 2) & 1 else x
        py = 1 - y if (k >> 1) & 1 else y
        pc = 1 - c if k & 1 else c
        peers.append(((px, py, pc), 4 * px + 2 * py + pc))
    return 4 * x + 2 * y + c, peers


def all_gather_rows(xs, tag):
    nl, r, c = xs.shape

    def body(x_ref, o_ref, send_sems, recv_sems, local_sem):
        x, y, cc = lax.axis_index("x"), lax.axis_index("y"), lax.axis_index("c")
        me, sibling = (x, y, cc), (x, y, 1 - cc)
        chips = [(1 - x, y), (x, 1 - y), (1 - x, 1 - y)]

        def rows(px, py, pc):
            return o_ref.at[:, pl.ds((4 * px + 2 * py + pc) * r, r), :]

        def copy(k, block, to, src=None):
            return pltpu.make_async_remote_copy(src_ref=rows(*block) if src is None else src, dst_ref=rows(*block),
                                                send_sem=send_sems.at[k], recv_sem=recv_sems.at[k], device_id=to,
                                                device_id_type=pl.DeviceIdType.MESH)

        mine = pltpu.make_async_copy(x_ref, rows(*me), local_sem)
        mine.start()
        first = [copy(0, me, sibling, src=x_ref)] + [copy(1 + j, me, (*chip, cc), src=x_ref) for j, chip in enumerate(chips)]
        for cp in first:
            cp.start()
        passed = [copy(4 + j, (*chip, cc), sibling) for j, chip in enumerate(chips)]
        for j, chip in enumerate(chips):
            copy(1 + j, (*chip, cc), me).wait_recv()
            passed[j].start()
        copy(0, sibling, me).wait_recv()
        for j, chip in enumerate(chips):
            copy(4 + j, (*chip, 1 - cc), me).wait_recv()
        for cp in first + passed:
            cp.wait_send()
        mine.wait()

    return pl.pallas_call(
        body, name=f"all_gather_{tag}", in_specs=[_ANY], out_specs=_ANY, out_shape=_sds((nl, N_DEV * r, c), xs.dtype),
        scratch_shapes=[pltpu.SemaphoreType.DMA((N_DEV - 1,)), pltpu.SemaphoreType.DMA((N_DEV - 1,)), pltpu.SemaphoreType.DMA],
    )(xs)


def exchange_cores(part, tag):
    nl, _, _, r, c = part.shape

    def body(p_ref, theirs_ref, send_sems, recv_sems):
        x, y, cc = lax.axis_index("x"), lax.axis_index("y"), lax.axis_index("c")
        copies = []
        for l in range(nl):
            for q in range(4):
                k = l * 4 + q
                copies.append(pltpu.make_async_remote_copy(
                    src_ref=p_ref.at[l, q, pl.ds(1 - cc, 1)], dst_ref=theirs_ref.at[l, q], send_sem=send_sems.at[k],
                    recv_sem=recv_sems.at[k], device_id=(x, y, 1 - cc), device_id_type=pl.DeviceIdType.MESH))
        for cp in copies:
            cp.start()
        for cp in copies:
            cp.wait()

    nsem = pltpu.SemaphoreType.DMA((nl * 4,))
    return pl.pallas_call(
        body, name=f"exchange_cores_{tag}", in_specs=[_ANY], out_specs=_ANY, out_shape=_sds((nl, 4, 1, r, c), part.dtype),
        scratch_shapes=[nsem, nsem],
    )(part)


def add_core_halves(part5, theirs, core, tag):
    nl, _, _, r, c = part5.shape
    tr = 64 if r % 64 == 0 else 32

    def body(core_ref, a_ref, b_ref, o_ref):
        o_ref[...] = _cast(a_ref[...] + b_ref[...])

    blk = lambda which: pl.BlockSpec((None, None, None, tr, c), lambda l, q, i, cref: (l, q, cref[0] if which else 0, i, 0))
    return pl.pallas_call(
        body, name=f"add_core_halves_{tag}", out_shape=_sds((nl, 4, 1, r, c), _MMT),
        grid_spec=pltpu.PrefetchScalarGridSpec(num_scalar_prefetch=1, grid=(nl, 4, r // tr), in_specs=[blk(True), blk(False)],
                                               out_specs=blk(False)),
        compiler_params=pltpu.CompilerParams(dimension_semantics=("arbitrary",) * 3, vmem_limit_bytes=VMEM_LIMIT_BYTES),
    )(core, part5, theirs)


def exchange_chips(s4, tag):
    nl, _, _, r, c = s4.shape

    def body(s_ref, o_ref, send_sems, recv_sems, local_sem):
        x, y, cc = lax.axis_index("x"), lax.axis_index("y"), lax.axis_index("c")
        local = pltpu.make_async_copy(s_ref.at[:, pl.ds(2 * x + y, 1)], o_ref.at[0], local_sem)
        local.start()
        copies = []
        for k in range(1, 4):
            px = 1 - x if (k >> 1) & 1 else x
            py = 1 - y if k & 1 else y
            copies.append(pltpu.make_async_remote_copy(
                src_ref=s_ref.at[:, pl.ds(2 * px + py, 1)], dst_ref=o_ref.at[k], send_sem=send_sems.at[k - 1],
                recv_sem=recv_sems.at[k - 1], device_id=(px, py, cc), device_id_type=pl.DeviceIdType.MESH))
        for cp in copies:
            cp.start()
        for cp in copies:
            cp.wait()
        local.wait()

    return pl.pallas_call(
        body, name=f"exchange_chips_{tag}", in_specs=[_ANY], out_specs=_ANY, out_shape=_sds((4, nl, 1, 1, r, c), s4.dtype),
        scratch_shapes=[pltpu.SemaphoreType.DMA((3,)), pltpu.SemaphoreType.DMA((3,)), pltpu.SemaphoreType.DMA],
    )(s4)


def exchange_row_blocks(part, tag):
    nl, r8, c = part.shape
    r = r8 // N_DEV

    def body(p_ref, o_ref, send_sems, recv_sems, local_sem):
        me, peers = _me_and_peers()
        rows = lambda idx: p_ref.at[:, pl.ds(idx * r, r), :]
        mine = pltpu.make_async_copy(rows(me), o_ref.at[0], local_sem)
        mine.start()
        copies = [pltpu.make_async_remote_copy(src_ref=rows(idx), dst_ref=o_ref.at[k + 1], send_sem=send_sems.at[k],
                                               recv_sem=recv_sems.at[k], device_id=dev, device_id_type=pl.DeviceIdType.MESH)
                  for k, (dev, idx) in enumerate(peers)]
        for cp in copies:
            cp.start()
        for cp in copies:
            cp.wait()
        mine.wait()

    return pl.pallas_call(
        body, name=f"exchange_{tag}", in_specs=[_ANY], out_specs=_ANY, out_shape=_sds((N_DEV, nl, r, c), part.dtype),
        scratch_shapes=[pltpu.SemaphoreType.DMA((N_DEV - 1,)), pltpu.SemaphoreType.DMA((N_DEV - 1,)), pltpu.SemaphoreType.DMA],
    )(part)


def _row_tile(rows):
    if rows <= 512 and rows % 64:
        return rows
    for t in (64, 40, 32, 16, 8):
        if rows % t == 0:
            return t
    return rows


def sum_slots(slots, tag):
    ns, rows, c = slots.shape
    tr = _row_tile(rows)

    def body(s_ref, o_ref):
        acc = s_ref[0].astype(F32)
        for k in range(1, ns):
            acc = acc + s_ref[k].astype(F32)
        o_ref[...] = acc

    return _call(body, f"sum_slots_{tag}", (rows // tr,), [pl.BlockSpec((ns, tr, c), lambda i: (0, i, 0))],
                 pl.BlockSpec((tr, c), lambda i: (i, 0)), _sds((rows, c)))(slots)


def _adamw(w, g, m, v):
    m = ADAM_B1 * m + (1.0 - ADAM_B1) * g
    v = ADAM_B2 * v + (1.0 - ADAM_B2) * (g * g)
    m_hat = m / (1.0 - ADAM_B1 ** ADAM_STEP)
    v_hat = v / (1.0 - ADAM_B2 ** ADAM_STEP)
    delta = -ADAM_LR * (m_hat / (jnp.sqrt(v_hat) + ADAM_EPS) + ADAM_WD * w)
    return delta, m, v


def adam_update(w, m, v, g, tag, slots=False):
    rows, c = w.shape
    tr = _row_tile(rows)
    ns = g.shape[0] if slots else 0

    def body(w_ref, m_ref, v_ref, g_ref, go_ref, d_ref, mo_ref, vo_ref):
        if slots:
            g = g_ref[0].astype(F32)
            for k in range(1, ns):
                g = g + g_ref[k].astype(F32)
        else:
            g = g_ref[...]
        d, mn, vn = _adamw(w_ref[...], g, m_ref[...], v_ref[...])
        go_ref[...] = g
        d_ref[...] = d
        mo_ref[...] = mn
        vo_ref[...] = vn

    blk = pl.BlockSpec((tr, c), lambda i: (i, 0))
    gspec = pl.BlockSpec((ns, tr, c), lambda i: (0, i, 0)) if slots else blk
    return _call(body, f"adam_{tag}", (rows // tr,), [blk, blk, blk, gspec], [blk] * 4, [_sds((rows, c))] * 4)(w, m, v, g)


_SMALL = (("norm_g", (2, 1024)), ("fox_f_bias", (2, 4)), ("fox_q_norm", (2, 64)), ("fox_k_norm", (2, 64)),
          ("hgrn_lb_logits", (2, 256)), ("hgrn_out_norm", (2, 256)), ("pool_w", (2, 4, 64, 64)), ("pool_scale", (2, 256)),
          ("mem_norm_g", (2, 1024)), ("mem_q_norm", (2, 64)), ("mem_k_norm", (2, 64)))
_SLAB_ROWS = 312


def pack_small(d):
    flat = jnp.concatenate([d[n].reshape(-1) for n, _ in _SMALL])
    return jnp.pad(flat, (0, _SLAB_ROWS * 128 - flat.shape[0])).reshape(_SLAB_ROWS, 128)


def unpack_small(slab):
    flat, out, off = slab.reshape(-1), {}, 0
    for n, shp in _SMALL:
        size = 1
        for e in shp:
            size *= e
        out[n] = flat[off:off + size].reshape(shp)
        off += size
    return out


def small_grads(g0, g1, lb_logits):
    st = lambda f: jnp.stack([f(g0), f(g1)])
    heads = lambda a: a.reshape(NH, HD).sum(0)
    p1 = jax.nn.sigmoid(lb_logits[1] - lb_logits[0])
    inside = (p1 > 0.0) & (p1 < 1.0 - 1e-6)
    dl1 = jnp.where(inside, g1["lb"][0] * p1 * (1.0 - p1), 0.0)
    diag = lambda a: jnp.stack([a.reshape(4, HD, 4, HD)[i, :, i, :] for i in range(4)])
    return dict(norm_g=st(lambda g: g["norm_g"][0]), fox_f_bias=st(lambda g: g["fox_f_bias"][0, :NH]),
                fox_q_norm=st(lambda g: heads(g["fox_q_norm"])), fox_k_norm=st(lambda g: heads(g["fox_k_norm"])),
                hgrn_lb_logits=jnp.stack([-dl1, dl1]), hgrn_out_norm=st(lambda g: g["hgrn_out_norm"][0]),
                pool_w=st(lambda g: diag(g["pool_wbd"])), pool_scale=st(lambda g: g["pool_scale"][0]),
                mem_norm_g=st(lambda g: g["mem_norm_g"][0]), mem_q_norm=st(lambda g: heads(g["mem_q_norm"])),
                mem_k_norm=st(lambda g: heads(g["mem_k_norm"])))


def kernel(x, mem, norm_g, w_in, fox_f_bias, fox_q_norm, fox_k_norm, hgrn_lb_logits, hgrn_out_norm, pool_w, pool_scale, mem_norm_g, mem_w_kv, mem_q_norm, mem_k_norm, w_out, loss_target, m_norm_g, m_w_in, m_fox_f_bias, m_fox_q_norm, m_fox_k_norm, m_hgrn_lb_logits, m_hgrn_out_norm, m_pool_w, m_pool_scale, m_mem_norm_g, m_mem_w_kv, m_mem_q_norm, m_mem_k_norm, m_w_out, v_norm_g, v_w_in, v_fox_f_bias, v_fox_q_norm, v_fox_k_norm, v_hgrn_lb_logits, v_hgrn_out_norm, v_pool_w, v_pool_scale, v_mem_norm_g, v_mem_w_kv, v_mem_q_norm, v_mem_k_norm, v_w_out):
    given = dict(norm_g=(norm_g, m_norm_g, v_norm_g), w_in=(w_in, m_w_in, v_w_in), fox_f_bias=(fox_f_bias, m_fox_f_bias, v_fox_f_bias),
                 fox_q_norm=(fox_q_norm, m_fox_q_norm, v_fox_q_norm), fox_k_norm=(fox_k_norm, m_fox_k_norm, v_fox_k_norm),
                 hgrn_lb_logits=(hgrn_lb_logits, m_hgrn_lb_logits, v_hgrn_lb_logits),
                 hgrn_out_norm=(hgrn_out_norm, m_hgrn_out_norm, v_hgrn_out_norm), pool_w=(pool_w, m_pool_w, v_pool_w),
                 pool_scale=(pool_scale, m_pool_scale, v_pool_scale), mem_norm_g=(mem_norm_g, m_mem_norm_g, v_mem_norm_g),
                 mem_w_kv=(mem_w_kv, m_mem_w_kv, v_mem_w_kv), mem_q_norm=(mem_q_norm, m_mem_q_norm, v_mem_q_norm),
                 mem_k_norm=(mem_k_norm, m_mem_k_norm, v_mem_k_norm), w_out=(w_out, m_w_out, v_w_out))
    order = ("norm_g", "w_in", "fox_f_bias", "fox_q_norm", "fox_k_norm", "hgrn_lb_logits", "hgrn_out_norm", "pool_w",
             "pool_scale", "mem_norm_g", "mem_w_kv", "mem_q_norm", "mem_k_norm", "w_out")

    w_in_full = all_gather_rows(_cast(permute_cols(w_in)), "w_in")
    w_out_full = all_gather_rows(_cast(w_out), "w_out")
    w_kv_full = all_gather_rows(_cast(mem_w_kv), "w_kv")
    p = prepare_params(norm_g, w_in_full, fox_f_bias, fox_q_norm, fox_k_norm, hgrn_lb_logits, hgrn_out_norm, pool_w,
                       pool_scale, mem_norm_g, w_kv_full, mem_q_norm, mem_k_norm, w_out_full)

    loss_part, grad_x, g0, g1 = local_step(x, mem, loss_target, p)
    loss = lax.psum(loss_part, ("x", "y", "c"))

    res = {}
    core = lax.axis_index("c").astype(jnp.int32).reshape(1)

    def sharded(name, g2, unperm=False, two_stage=True):
        w, m, v = given[name]
        nl, r, c = w.shape
        cp = g2.shape[-1]
        if two_stage:
            part5 = g2.reshape(nl, 4, 2, r, cp)
            s4 = add_core_halves(part5, exchange_cores(part5, name), core, name)
            slots = exchange_chips(s4, name).reshape(4, nl * r, cp)
        else:
            slots = exchange_row_blocks(g2, name).reshape(N_DEV, nl * r, cp)
        if unperm:
            g = sum_slots(slots, name).reshape(nl, r, cp)
            out = adam_update(w.reshape(nl * r, c), m.reshape(nl * r, c), v.reshape(nl * r, c),
                              unpermute_cols(g).reshape(nl * r, c), name)
        else:
            out = adam_update(w.reshape(nl * r, c), m.reshape(nl * r, c), v.reshape(nl * r, c), slots, name, slots=True)
        res[name] = tuple(o.reshape(nl, r, c) for o in out)

    sharded("w_in", jnp.stack([g0["w_in"], g1["w_in"]]), unperm=True)
    sharded("w_out", jnp.stack([g0["w_out"], g1["w_out"]]))
    sharded("mem_w_kv", _cast(jnp.stack([g0["mem_w_kv"], g1["mem_w_kv"]])), two_stage=False)

    gsmall = pack_small(small_grads(g0, g1, hgrn_lb_logits))
    gathered = all_gather_rows(gsmall[None], "small").reshape(N_DEV, _SLAB_ROWS, 128)
    slabs = adam_update(*[pack_small({n: given[n][j] for n, _ in _SMALL}) for j in range(3)], gathered, "small", slots=True)
    small = [unpack_small(sl) for sl in slabs]
    for n, _ in _SMALL:
        res[n] = tuple(small[j][n] for j in range(4))

    return (loss, grad_x, *[res[n][0] for n in order], *[res[n][1] for n in order], *[res[n][2] for n in order],
            *[res[n][3] for n in order])
```

```python
import functools

import jax
import jax.numpy as jnp
from jax import lax
from jax.experimental import pallas as pl
from jax.experimental.pallas import tpu as pltpu

F32 = jnp.float32
BF = jnp.bfloat16
_MMT = BF

D_MODEL = 1024
GW = 256
HD = 64
NH = 4
CH = 64
N_MEM = 256
D_IN = 4100
D_INP = 4224
D_MIX = 1280
EPS = 1e-6
NEG_BIG = -1e30
LB_FLOOR = 1e-30
SCALE = HD ** -0.5
TQ = 256
N_DEV = 8
VMEM_LIMIT_BYTES = 56 * 1024 * 1024

ADAM_LR = 0.001
ADAM_B1 = 0.9
ADAM_B2 = 0.999
ADAM_EPS = 1e-08
ADAM_WD = 0.01
ADAM_STEP = 10

PIECES = (("A", 0, 768), ("B", 768, 768), ("C", 1536, 768), ("D", 2304, 256), ("E", 2560, 256),
          ("G", 2816, 1280), ("F", 4096, 128))
BWD_PIECES = (("A", 0, 768), ("Bq", 768, 256), ("Bk", 1024, 256), ("Bv", 1280, 256), ("C", 1536, 768),
              ("D", 2304, 256), ("E", 2560, 256), ("G", 2816, 1280), ("F", 4096, 128))
_ORIG = dict(fq=(0, 256), fk=(256, 512), fv=(512, 768), fg=(768, 1024), ff=(1024, 1028), sq=(1028, 1284),
             sk=(1284, 1540), sv=(1540, 1796), sg=(1796, 2052), hq=(2052, 2308), hf=(2308, 2564),
             hi=(2564, 2820), hg=(2820, 3076), pv=(3076, 3332), pg=(3332, 3588), mq=(3588, 3844), mg=(3844, 4100))
_PERM_ORDER = ("fq", "fk", "fv", "sq", "sk", "sv", "hq", "hf", "hi", "pv", "mq", "fg", "sg", "hg", "pg", "mg", "ff")
_ORIG_ORDER = ("fq", "fk", "fv", "fg", "ff", "sq", "sk", "sv", "sg", "hq", "hf", "hi", "hg", "pv", "pg", "mq", "mg")


def permute_cols(w):
    parts = [w[..., _ORIG[n][0]:_ORIG[n][1]] for n in _PERM_ORDER]
    parts.append(jnp.zeros(w.shape[:-1] + (D_INP - D_IN,), w.dtype))
    return jnp.concatenate(parts, axis=-1)


def unpermute_cols(g):
    start, off = {}, 0
    for n in _PERM_ORDER:
        start[n] = off
        off += _ORIG[n][1] - _ORIG[n][0]
    return jnp.concatenate([g[..., start[n]:start[n] + _ORIG[n][1] - _ORIG[n][0]] for n in _ORIG_ORDER], axis=-1)


def _cast(a):
    return a.astype(_MMT)


def _dg(a, b, ca, cb):
    return lax.dot_general(a, b, (((ca,), (cb,)), ((), ())), preferred_element_type=F32)


@jax.custom_vjp
def mm(a, b):
    return _dg(_cast(a), _cast(b), 1, 0)


@jax.custom_vjp
def mm_nt(a, b):
    return _dg(_cast(a), _cast(b), 1, 1)


@jax.custom_vjp
def mm_tn(a, b):
    return _dg(_cast(a), _cast(b), 0, 0)


mm.defvjp(lambda a, b: (mm(a, b), (a, b)),
          lambda r, g: (mm_nt(g, r[1]).astype(r[0].dtype), mm_tn(r[0], g).astype(r[1].dtype)))
mm_nt.defvjp(lambda a, b: (mm_nt(a, b), (a, b)),
             lambda r, g: (mm(g, r[1]).astype(r[0].dtype), mm_tn(g, r[0]).astype(r[1].dtype)))
mm_tn.defvjp(lambda a, b: (mm_tn(a, b), (a, b)),
             lambda r, g: (mm_nt(r[1], g).astype(r[0].dtype), mm(r[0], g).astype(r[1].dtype)))


def _split(a):
    hi = a.astype(_MMT)
    lo = (a - hi.astype(F32)).astype(_MMT)
    return hi, lo


@jax.custom_vjp
def xr(a, c, ct):
    hi, lo = _split(a)
    cc = _cast(c)
    return _dg(hi, cc, 1, 0) + _dg(lo, cc, 1, 0)


@jax.custom_vjp
def xl(c, ct, a):
    hi, lo = _split(a)
    cc = _cast(c)
    return _dg(cc, hi, 1, 0) + _dg(cc, lo, 1, 0)


xr.defvjp(lambda a, c, ct: (xr(a, c, ct), (c, ct)),
          lambda r, g: (xr(g, r[1], r[0]), jnp.zeros_like(r[0]), jnp.zeros_like(r[1])))
xl.defvjp(lambda c, ct, a: (xl(c, ct, a), (c, ct)),
          lambda r, g: (jnp.zeros_like(r[0]), jnp.zeros_like(r[1]), xl(r[1], r[0], g)))


@functools.partial(jax.custom_vjp, nondiff_argnums=(3,))
def xl_rows(c, ct, a, n):
    y = xl(c, ct, a)
    rows = y.shape[0] // n
    return tuple(y[i * rows:(i + 1) * rows] for i in range(n))


xl_rows.defvjp(lambda c, ct, a, n: (xl_rows(c, ct, a, n), (c, ct)),
               lambda n, r, g: (jnp.zeros_like(r[0]), jnp.zeros_like(r[1]), xl(r[1], r[0], jnp.concatenate(g, axis=0))))


def _iota(shape, dim):
    return lax.broadcasted_iota(jnp.int32, shape, dim)


def _hmask(h, n=GW):
    lane = _iota((1, n), 1)
    return ((lane >= h * HD) & (lane < (h + 1) * HD)).astype(F32)


def _bdmask(n=GW):
    return ((_iota((n, n), 0) >> 6) == (_iota((n, n), 1) >> 6)).astype(F32)


def _tri(n, kind="le"):
    r, c = _iota((n, n), 0), _iota((n, n), 1)
    return {"le": c <= r, "ge": c >= r, "gt": c > r, "lt": c < r}[kind].astype(F32)


def _onehot_lane(h, n=128):
    return (_iota((1, n), 1) == h).astype(F32)


def _logsig(x):
    return jnp.minimum(x, 0.0) - jnp.log1p(jnp.exp(-jnp.abs(x)))


def _sigmoid(x):
    return 0.5 * (jnp.tanh(0.5 * x) + 1.0)


def _silu(x):
    return x * _sigmoid(x)


def _rms(x, g):
    return x * lax.rsqrt(jnp.mean(x * x, axis=-1, keepdims=True) + EPS) * g


def _headrms(x, w, bd64):
    ms = xr(x * x, bd64, bd64)
    return x * lax.rsqrt(ms + EPS) * w


def _call(body, name, grid, in_specs, out_specs, out_shape, scratch=()):
    return pl.pallas_call(
        body, name=name, grid=grid, in_specs=in_specs, out_specs=out_specs, out_shape=out_shape,
        scratch_shapes=list(scratch),
        compiler_params=pltpu.CompilerParams(dimension_semantics=("arbitrary",) * len(grid),
                                             vmem_limit_bytes=VMEM_LIMIT_BYTES))


def _sds(shape, dtype=F32):
    return jax.ShapeDtypeStruct(shape, dtype)


def _acc(ref, val, first):
    @pl.when(first)
    def _():
        ref[...] = val

    @pl.when(jnp.logical_not(first))
    def _():
        ref[...] += val


def inproj_fwd(x2, g, w, layer, tag):
    t = x2.shape[0]

    def body(x_ref, g_ref, w_ref, ht_ref, *outs):
        h = _rms(x_ref[...], g_ref[...])
        hb = _cast(h)
        ht_ref[...] = _cast(h.T)
        for (_, c0, wd), o in zip(PIECES, outs):
            o[...] = _dg(hb, _cast(w_ref[:, c0:c0 + wd]), 1, 0)

    return _call(
        body, f"inproj_fwd_{tag}", (t // TQ,),
        [pl.BlockSpec((TQ, D_MODEL), lambda i: (i, 0)),
         pl.BlockSpec((None, 1, D_MODEL), lambda i: (layer, 0, 0)),
         pl.BlockSpec((None, D_MODEL, D_INP), lambda i: (layer, 0, 0))],
        [pl.BlockSpec((D_MODEL, TQ), lambda i: (0, i))] + [pl.BlockSpec((TQ, wd), lambda i: (i, 0)) for _, _, wd in PIECES],
        [_sds((D_MODEL, t), _MMT)] + [_sds((t, wd)) for _, _, wd in PIECES],
    )(x2, g, w)


def inproj_bwd_dx(x2, g, w, dy, dpieces, layer, tag):
    t = x2.shape[0]

    def body(x_ref, g_ref, w_ref, dy_ref, *rest):
        dps, (dx_ref, dg_ref) = rest[:len(BWD_PIECES)], rest[len(BWD_PIECES):]
        dh = None
        for (_, c0, wd), dp in zip(BWD_PIECES, dps):
            part = _dg(_cast(dp[...]), _cast(w_ref[:, c0:c0 + wd]), 1, 1)
            dh = part if dh is None else dh + part
        _, vjp = jax.vjp(_rms, x_ref[...], g_ref[...])
        dx, dg = vjp(dh)
        dx_ref[...] = dy_ref[...] + dx
        _acc(dg_ref, dg, pl.program_id(0) == 0)

    return _call(
        body, f"inproj_bwd_dx_{tag}", (t // TQ,),
        [pl.BlockSpec((TQ, D_MODEL), lambda i: (i, 0)),
         pl.BlockSpec((None, 1, D_MODEL), lambda i: (layer, 0, 0)),
         pl.BlockSpec((None, D_MODEL, D_INP), lambda i: (layer, 0, 0)),
         pl.BlockSpec((TQ, D_MODEL), lambda i: (i, 0))] + [pl.BlockSpec((TQ, wd), lambda i: (i, 0)) for _, _, wd in BWD_PIECES],
        [pl.BlockSpec((TQ, D_MODEL), lambda i: (i, 0)), pl.BlockSpec((1, D_MODEL), lambda i: (0, 0))],
        [_sds((t, D_MODEL)), _sds((1, D_MODEL))],
    )(x2, g, w, dy, *dpieces)


def matmul_acc(at, b, tag):
    m, t = at.shape
    n = b.shape[1]
    tn = {1280: 640, 768: 768}.get(n, n)
    tk = 2048 if t % 2048 == 0 else (512 if t % 512 == 0 else TQ)

    def body(a_ref, b_ref, o_ref):
        _acc(o_ref, _dg(_cast(a_ref[...]), _cast(b_ref[...]), 1, 0), pl.program_id(1) == 0)

    return _call(
        body, f"matmul_acc_{tag}", (n // tn, t // tk),
        [pl.BlockSpec((m, tk), lambda j, i: (0, i)), pl.BlockSpec((tk, tn), lambda j, i: (i, j))],
        pl.BlockSpec((m, tn), lambda j, i: (0, j)),
        _sds((m, n)),
    )(at, b)


def _fox_prep_fn(q, k, ff, qw, kw, bias, carry, bd64, tri, trit, last):
    qn = _headrms(q, qw, bd64)
    kn = _headrms(k, kw, bd64)
    lf = _logsig(ff + bias)
    c = xl(tri, trit, lf) + carry
    return qn, kn, c, jnp.sum(c * last, axis=0, keepdims=True)


def _prep_consts():
    return _bdmask() * (1.0 / HD), _tri(TQ), _tri(TQ, "ge"), (_iota((TQ, 1), 0) == TQ - 1).astype(F32)


def fox_prep_fwd(pa, pf, qw, kw, bias, bl, s, layer, tag):
    nq = s // TQ

    def body(q_ref, k_ref, v_ref, f_ref, qw_ref, kw_ref, b_ref, qn_ref, kn_ref, vb_ref, cq_ref, ck_ref, carry):
        @pl.when(pl.program_id(1) == 0)
        def _():
            carry[...] = jnp.zeros_like(carry)

        qn, kn, c, cl = _fox_prep_fn(q_ref[...], k_ref[...], f_ref[...], qw_ref[...], kw_ref[...], b_ref[...],
                                     carry[...], *_prep_consts())
        carry[...] = cl
        qn_ref[...] = _cast(qn)
        kn_ref[...] = _cast(kn)
        vb_ref[...] = _cast(v_ref[...])
        cq_ref[...] = c
        ck_ref[...] = c.T[0:8, :]

    tok = lambda j: pl.BlockSpec((TQ, GW), lambda b, i: (b * nq + i, j))
    par = lambda n: pl.BlockSpec((None, 1, n), lambda b, i: (layer, 0, 0))
    return _call(
        body, f"fox_prep_fwd_{tag}", (bl, nq),
        [tok(0), tok(1), tok(2), pl.BlockSpec((TQ, 128), lambda b, i: (b * nq + i, 0)), par(GW), par(GW), par(128)],
        [tok(0), tok(0), tok(0), pl.BlockSpec((TQ, 128), lambda b, i: (b * nq + i, 0)),
         pl.BlockSpec((None, 8, TQ), lambda b, i: (b, 0, i))],
        [_sds((bl * s, GW), _MMT)] * 3 + [_sds((bl * s, 128)), _sds((bl, 8, s))],
        [pltpu.VMEM((1, 128), F32)],
    )(pa, pa, pa, pf, qw, kw, bias)


def fox_prep_bwd(pa, pf, qw, kw, bias, cq, dqn, dkn, dv, dck, bl, s, layer, tag):
    nq = s // TQ

    def body(q_ref, k_ref, f_ref, qw_ref, kw_ref, b_ref, cq_ref, cprev_ref, dqn_ref, dkn_ref, dv_ref, dck_ref,
             da_ref, df_ref, dqw_ref, dkw_ref, db_ref, dcarry):
        i = pl.program_id(1)
        first = jnp.logical_and(pl.program_id(0) == 0, i == 0)

        @pl.when(i == 0)
        def _():
            dcarry[...] = jnp.zeros_like(dcarry)

        last = (_iota((TQ, 1), 0) == TQ - 1).astype(F32)
        carry_in = jnp.where(i == nq - 1, 0.0, jnp.sum(cprev_ref[...] * last, axis=0, keepdims=True))
        consts = _prep_consts()
        _, vjp = jax.vjp(lambda *a: _fox_prep_fn(*a, *consts), q_ref[...], k_ref[...], f_ref[...], qw_ref[...],
                         kw_ref[...], b_ref[...], carry_in)
        dc = dck_ref[...].T
        dq, dk, dff, dqw, dkw, dbias, dcin = vjp((dqn_ref[...], dkn_ref[...], dc, dcarry[...]))
        dcarry[...] = dcin
        da_ref[:, 0:GW] = dq
        da_ref[:, GW:2 * GW] = dk
        da_ref[:, 2 * GW:3 * GW] = dv_ref[...]
        df_ref[...] = dff
        _acc(dqw_ref, dqw, first)
        _acc(dkw_ref, dkw, first)
        _acc(db_ref, dbias, first)

    rv = lambda b, i: b * nq + (nq - 1 - i)
    tok = lambda j: pl.BlockSpec((TQ, GW), lambda b, i: (rv(b, i), j))
    tok0 = pl.BlockSpec((TQ, GW), lambda b, i: (rv(b, i), 0))
    t128 = pl.BlockSpec((TQ, 128), lambda b, i: (rv(b, i), 0))
    prev = pl.BlockSpec((TQ, 128), lambda b, i: (jnp.maximum(rv(b, i) - 1, 0), 0))
    par = lambda n: pl.BlockSpec((None, 1, n), lambda b, i: (layer, 0, 0))
    acc = lambda n: pl.BlockSpec((1, n), lambda b, i: (0, 0))
    return _call(
        body, f"fox_prep_bwd_{tag}", (bl, nq),
        [tok(0), tok(1), t128, par(GW), par(GW), par(128), t128, prev, tok0, tok0, tok0,
         pl.BlockSpec((None, 128, TQ), lambda b, i: (b, 0, nq - 1 - i))],
        [pl.BlockSpec((TQ, 3 * GW), lambda b, i: (rv(b, i), 0)), t128, acc(GW), acc(GW), acc(128)],
        [_sds((bl * s, 3 * GW)), _sds((bl * s, 128)), _sds((1, GW)), _sds((1, GW)), _sds((1, 128))],
        [pltpu.VMEM((1, 128), F32)],
    )(pa, pa, pf, qw, kw, bias, cq, cq, dqn, dkn, dv, dck)


def _lane_pick(x, h):
    return jnp.sum(x * _onehot_lane(h), axis=-1, keepdims=True)


TA_BIG = 256
TK_FOX = 512
TK_SB = 256


def _stack_heads(x, scale=1.0):
    return _cast(jnp.concatenate([x * (_hmask(h) * scale) for h in range(NH)], axis=0))


def _stack_cols(x):
    return jnp.concatenate([_lane_pick(x, h) for h in range(NH)], axis=0)


def _spread_heads(col):
    ta = col.shape[0] // NH
    return sum(col[h * ta:(h + 1) * ta] * _hmask(h) for h in range(NH))


def _lanes_cat(w):
    ta = w.shape[0] // NH
    return jnp.concatenate([w[h * ta:(h + 1) * ta] for h in range(NH)], axis=1)


def _mask_stack(x):
    return _cast(jnp.concatenate([x * _hmask(h).astype(x.dtype) for h in range(NH)], axis=0))


def _stack_rows(i, ta):
    return i * ta + (_iota((NH * ta, 1), 0) & (ta - 1))


def _n_key_tiles(i, tk, ta):
    return lax.shift_right_logical(i * ta, tk.bit_length() - 1) + 1


def fox_attn_fwd(qn, kn, vb, cq, ck, bl, s, tag):
    TA, TK = min(TA_BIG, s), min(TK_FOX, s)
    nq, SROWS = s // TA, NH * TA

    def body(q_ref, k_ref, v_ref, cq_ref, ck_ref, o_ref, lse_ref, acc, vst):
        i = pl.program_id(1)

        @pl.when(i == 0)
        def _():
            _fill_stacked(vst, v_ref, s, TK)

        qs = _stack_heads(q_ref[...].astype(F32), SCALE)
        cqs = _stack_cols(cq_ref[...])
        row = _stack_rows(i, TA)
        acc[...] = jnp.zeros_like(acc)

        def step(j, ml):
            m, l = ml
            ks = pl.ds(pl.multiple_of(j * TK, TK), TK)
            ckb = jnp.concatenate([jnp.broadcast_to(ck_ref[h:h + 1, ks], (TA, TK)) for h in range(NH)], axis=0)
            sc = _dg(qs, k_ref[ks, :], 1, 1) + cqs - ckb
            col = j * TK + _iota((1, TK), 1)
            sc = jnp.where(col <= row, sc, NEG_BIG)
            m_new = jnp.maximum(m, jnp.max(sc, axis=-1, keepdims=True))
            alpha = jnp.exp(m - m_new)
            p = jnp.exp(sc - m_new)
            vs = vst[pl.ds(pl.multiple_of(j * NH * TK, NH * TK), NH * TK), :]
            acc[...] = _spread_heads(alpha) * acc[...] + _dg(_lanes_cat(_cast(p)), vs, 1, 0)
            return m_new, alpha * l + jnp.sum(p, axis=-1, keepdims=True)

        m, l = lax.fori_loop(0, _n_key_tiles(i, TK, TA), step, (jnp.full((SROWS, 1), NEG_BIG, F32), jnp.zeros((SROWS, 1), F32)))
        o_ref[...] = acc[...] / _spread_heads(l)
        lse_h = m + jnp.log(l)
        lse_ref[...] = sum(lse_h[h * TA:(h + 1) * TA] * _onehot_lane(h) for h in range(NH))

    tok = pl.BlockSpec((TA, GW), lambda b, i: (b * nq + i, 0))
    seq = pl.BlockSpec((s, GW), lambda b, i: (b, 0))
    t128 = pl.BlockSpec((TA, 128), lambda b, i: (b * nq + i, 0))
    return _call(
        body, f"fox_attn_fwd_{tag}", (bl, nq),
        [tok, seq, seq, t128, pl.BlockSpec((None, 8, s), lambda b, i: (b, 0, 0))],
        [tok, t128], [_sds((bl * s, GW)), _sds((bl * s, 128))],
        [pltpu.VMEM((TA, GW), F32), pltpu.VMEM((NH * s, GW), _MMT)],
    )(qn, kn, vb, cq, ck)


def fox_attn_bwd(qn, kn, vb, cq, ck, lse, do, bl, s, tag):
    TA, TK = min(TA_BIG, s), min(TK_FOX, s)
    nq, SROWS = s // TA, NH * TA

    def body(q_ref, k_ref, v_ref, cq_ref, ck_ref, lse_ref, do_ref, dq_ref, dk_ref, dv_ref, dck_ref, dqa, p_s, dp_s, kst):
        i = pl.program_id(1)

        @pl.when(i == 0)
        def _():
            dk_ref[...] = jnp.zeros_like(dk_ref)
            dv_ref[...] = jnp.zeros_like(dv_ref)
            dck_ref[...] = jnp.zeros_like(dck_ref)
            _fill_stacked(kst, k_ref, s, TK)

        qs = _stack_heads(q_ref[...].astype(F32), SCALE)
        dos = _stack_heads(do_ref[...])
        cqs, lses = _stack_cols(cq_ref[...]), _stack_cols(lse_ref[...])
        row = _stack_rows(i, TA)
        dqa[...] = jnp.zeros_like(dqa)
        nk = _n_key_tiles(i, TK, TA)

        def probs(j, delta):
            ks = pl.ds(pl.multiple_of(j * TK, TK), TK)
            ckb = jnp.concatenate([jnp.broadcast_to(ck_ref[h:h + 1, ks], (TA, TK)) for h in range(NH)], axis=0)
            sc = _dg(qs, k_ref[ks, :], 1, 1) + cqs - ckb
            col = j * TK + _iota((1, TK), 1)
            p = jnp.where(col <= row, jnp.exp(sc - lses), 0.0)
            dp = _dg(dos, v_ref[ks, :], 1, 1)
            p_s[:, ks] = p
            dp_s[:, ks] = dp
            return delta + jnp.sum(p * dp, axis=-1, keepdims=True)

        delta = lax.fori_loop(0, nk, probs, jnp.zeros((SROWS, 1), F32))

        def step(j, carry):
            ks = pl.ds(pl.multiple_of(j * TK, TK), TK)
            p = p_s[:, ks]
            ds = p * (dp_s[:, ks] - delta)
            dsb = _cast(ds)
            dqa[...] += _dg(_lanes_cat(dsb), kst[pl.ds(pl.multiple_of(j * NH * TK, NH * TK), NH * TK), :], 1, 0) * SCALE
            dk_ref[ks, :] += _dg(dsb, qs, 0, 0)
            dv_ref[ks, :] += _dg(_cast(p), dos, 0, 0)
            for h in range(NH):
                dck_ref[h:h + 1, ks] -= jnp.sum(ds[h * TA:(h + 1) * TA], axis=0, keepdims=True)
            return carry

        lax.fori_loop(0, nk, step, 0)
        dq_ref[...] = dqa[...]

    tok = pl.BlockSpec((TA, GW), lambda b, i: (b * nq + i, 0))
    seq = pl.BlockSpec((s, GW), lambda b, i: (b, 0))
    t128 = pl.BlockSpec((TA, 128), lambda b, i: (b * nq + i, 0))
    return _call(
        body, f"fox_attn_bwd_{tag}", (bl, nq),
        [tok, seq, seq, t128, pl.BlockSpec((None, 8, s), lambda b, i: (b, 0, 0)), t128, tok],
        [tok, seq, seq, pl.BlockSpec((None, 128, s), lambda b, i: (b, 0, 0))],
        [_sds((bl * s, GW)), _sds((bl * s, GW)), _sds((bl * s, GW)), _sds((bl, 128, s))],
        [pltpu.VMEM((TA, GW), F32), pltpu.VMEM((SROWS, s), F32), pltpu.VMEM((SROWS, s), F32), pltpu.VMEM((NH * s, GW), _MMT)],
    )(qn, kn, vb, cq, ck, lse, do)


def _sb_block(qh, kb, valid, upper, r_carry):
    z = _dg(qh, kb, 1, 1)
    ls = _logsig(z)
    lom = ls - z if valid is None else jnp.where(valid, ls - z, 0.0)
    between = xr(lom, upper, upper) + r_carry
    w = jnp.exp(ls + between)
    return ls, lom, (w if valid is None else jnp.where(valid, w, 0.0))


def _fill_stacked(dst, src_ref, s, tk):
    for j in range(s // tk):
        dst[j * NH * tk:(j + 1) * NH * tk, :] = _mask_stack(src_ref[j * tk:(j + 1) * tk, :])


def sb_attn_fwd(pb, bl, s, tag):
    TA, TK = TA_BIG, TK_SB
    nq, SROWS = s // TA, NH * TA

    def body(q_ref, k_ref, v_ref, o_ref, acc, vst):
        i = pl.program_id(1)

        @pl.when(i == 0)
        def _():
            _fill_stacked(vst, v_ref, s, TK)

        qs = _stack_heads(q_ref[...], SCALE)
        upper = _tri(TK, "lt")
        last = _n_key_tiles(i, TK, TA) - 1

        def step(j, r, valid):
            ks = pl.ds(pl.multiple_of(j * TK, TK), TK)
            _, lom, w = _sb_block(qs, _cast(k_ref[ks, :]), valid, upper, r)
            acc[...] += _dg(_lanes_cat(_cast(w)), vst[pl.ds(pl.multiple_of(j * NH * TK, NH * TK), NH * TK), :], 1, 0)
            return r + jnp.sum(lom, axis=-1, keepdims=True)

        acc[...] = jnp.zeros_like(acc)
        r = step(last, jnp.zeros((SROWS, 1), F32), last * TK + _iota((1, TK), 1) < _stack_rows(i, TA))
        lax.fori_loop(0, last, lambda jj, r: step(last - 1 - jj, r, None), r)
        o_ref[...] = acc[...]

    tok = lambda j: pl.BlockSpec((TA, GW), lambda b, i: (b * nq + i, j))
    seq = lambda j: pl.BlockSpec((s, GW), lambda b, i: (b, j))
    return _call(
        body, f"sb_attn_fwd_{tag}", (bl, nq), [tok(0), seq(1), seq(2)],
        pl.BlockSpec((TA, GW), lambda b, i: (b * nq + i, 0)), _sds((bl * s, GW)),
        [pltpu.VMEM((TA, GW), F32), pltpu.VMEM((NH * s, GW), _MMT)],
    )(pb, pb, pb)


def sb_attn_bwd(pb, do, bl, s, tag):
    TA, TK = TA_BIG, TK_SB
    nq, SROWS = s // TA, NH * TA

    def body(q_ref, k_ref, v_ref, do_ref, dq_ref, dk_ref, dv_ref, dqa, sig_s, nsig_s, w_s, g_s, kst):
        i = pl.program_id(1)

        @pl.when(i == 0)
        def _():
            dk_ref[...] = jnp.zeros_like(dk_ref)
            dv_ref[...] = jnp.zeros_like(dv_ref)
            _fill_stacked(kst, k_ref, s, TK)

        qs = _stack_heads(q_ref[...], SCALE)
        dos = _stack_heads(do_ref[...])
        upper = _tri(TK, "lt")
        before = _tri(TK, "gt")
        dqa[...] = jnp.zeros_like(dqa)
        last = _n_key_tiles(i, TK, TA) - 1
        diag = last * TK + _iota((1, TK), 1) < _stack_rows(i, TA)

        def weights(j, r, valid):
            ks = pl.ds(pl.multiple_of(j * TK, TK), TK)
            ls, lom, w = _sb_block(qs, _cast(k_ref[ks, :]), valid, upper, r)
            sig_s[:, ks] = _cast(jnp.exp(ls))
            nsig_s[:, ks] = _cast(jnp.exp(lom))
            w_s[:, ks] = _cast(w)
            g_s[:, ks] = _dg(dos, _cast(v_ref[ks, :]), 1, 1) * w
            return r + jnp.sum(lom, axis=-1, keepdims=True)

        r = weights(last, jnp.zeros((SROWS, 1), F32), diag)
        lax.fori_loop(0, last, lambda jj, r: weights(last - 1 - jj, r, None), r)

        def step(j, cpre, valid):
            ks = pl.ds(pl.multiple_of(j * TK, TK), TK)
            g = g_s[:, ks]
            pre = cpre + xr(g, before, before)
            dz = g * nsig_s[:, ks].astype(F32) - sig_s[:, ks].astype(F32) * pre
            dzb = _cast(dz if valid is None else jnp.where(valid, dz, 0.0))
            dqa[...] += _dg(_lanes_cat(dzb), kst[pl.ds(pl.multiple_of(j * NH * TK, NH * TK), NH * TK), :], 1, 0) * SCALE
            dk_ref[ks, :] += _dg(dzb, qs, 0, 0)
            dv_ref[ks, :] += _dg(w_s[:, ks], dos, 0, 0)
            return cpre + jnp.sum(g, axis=-1, keepdims=True)

        cpre = lax.fori_loop(0, last, lambda j, c: step(j, c, None), jnp.zeros((SROWS, 1), F32))
        step(last, cpre, diag)
        dq_ref[...] = dqa[...]

    tok = lambda j: pl.BlockSpec((TA, GW), lambda b, i: (b * nq + i, j))
    seq = lambda j: pl.BlockSpec((s, GW), lambda b, i: (b, j))
    return _call(
        body, f"sb_attn_bwd_{tag}", (bl, nq), [tok(0), seq(1), seq(2), tok(0)],
        [tok(0), seq(0), seq(0)], [_sds((bl * s, GW))] * 3,
        [pltpu.VMEM((TA, GW), F32), pltpu.VMEM((SROWS, s), _MMT), pltpu.VMEM((SROWS, s), _MMT),
         pltpu.VMEM((SROWS, s), _MMT), pltpu.VMEM((SROWS, s), F32), pltpu.VMEM((NH * s, GW), _MMT)],
    )(pb, pb, pb, do)


def _hgrn_consts():
    r, c = _iota((CH, CH), 0), _iota((CH, CH), 1)
    rr = _iota((CH, 1), 0)
    tri = (c <= r).astype(F32)
    lv = []
    for m in (8, 4, 2, 1):
        up = ((rr & (2 * m - 1)) >= m).astype(F32)
        pm = (((r & ~(2 * m - 1)) == (c & ~(2 * m - 1))) & ((r & (2 * m - 1)) >= m) & ((c & (2 * m - 1)) < m)).astype(F32)
        lv.append((up, 1.0 - up, jnp.concatenate([pm] * NH, axis=0)))

    def selection(big_axis):
        shape = (8 * CH, CH) if big_axis == 0 else (CH, 8 * CH)
        big, src = _iota(shape, big_axis), _iota(shape, 1 - big_axis)
        piece, row = big >> 6, big & (CH - 1)
        hit = None
        for li, m in enumerate((8, 4, 2, 1)):
            upper = (row & (2 * m - 1)) >= m
            q_side = (piece == 2 * li) & upper & (src == (row & ~(m - 1)) - 1)
            k_side = (piece == 2 * li + 1) & jnp.logical_not(upper) & (src == (row & ~(m - 1)) + m - 1)
            hit = (q_side | k_side) if hit is None else (hit | q_side | k_side)
        return hit.astype(F32)

    hm4 = lambda n: (((_iota((NH, 1, n), 2) & (GW - 1)) >> 6) == _iota((NH, 1, n), 0)).astype(F32)
    return dict(tri=tri, trit=tri.T, rr=rr, lv=lv, sel=selection(0), selt=selection(1), bd=_bdmask(),
                bd64=_bdmask() * (1.0 / HD), hm4={GW: hm4(GW), 3 * GW: hm4(3 * GW)})


def _hgrn_chunk_fn(hq, hf, hi, lb, wn, st, cs):
    q = _silu(hq)
    log_lb = jnp.log(jnp.maximum(lb, LB_FLOOR))
    a, bb = log_lb, jnp.log1p(-lb) + _logsig(hf)
    g = jnp.maximum(a, bb) + jnp.log1p(jnp.exp(-jnp.abs(a - bb)))
    k = (1.0 - lb) * _sigmoid(-hf)
    v = hi
    rr = cs["rr"]
    b = xl(cs["tri"], cs["trit"], g)
    row_of = lambda n: jnp.sum(b * (rr == n).astype(F32), axis=0, keepdims=True)
    o = mm_nt(q * jnp.exp(b), st)
    qs, ks = [], []
    for ib in (1, 2, 3):
        ref = row_of(16 * ib - 1)
        inq = ((rr >= 16 * ib) & (rr < 16 * ib + 16)).astype(F32)
        ink = (rr < 16 * ib).astype(F32)
        qs.append(q * jnp.exp((b - ref) * inq) * inq)
        ks.append(k * jnp.exp((ref - b) * ink) * ink)
    qcat, kcat = jnp.concatenate(qs, axis=1), jnp.concatenate(ks, axis=1)
    lvl = []
    refs = xl_rows(cs["sel"], cs["selt"], b, 8)
    for li, (up, lo, pm) in enumerate(cs["lv"]):
        qe = q * jnp.exp((b - refs[2 * li]) * up) * up
        ke = k * jnp.exp((refs[2 * li + 1] - b) * lo) * lo
        lvl.append((qe, ke, pm))
    stack = lambda x: (x[None] * cs["hm4"][x.shape[1]]).reshape(NH * CH, x.shape[1])
    a_all = mm_nt(stack(qcat), kcat)
    for qe, ke, pm4 in lvl:
        a_all = a_all + mm_nt(stack(qe), ke) * pm4
    o = o + jnp.sum(mm(a_all, v).reshape(NH, CH, GW) * cs["hm4"][GW], axis=0)
    o = o + xr(q * k, cs["bd"], cs["bd"]) * v
    b_last = row_of(CH - 1)
    st_new = st * jnp.exp(b_last) + mm_tn(v, k * jnp.exp(b_last - b)) * cs["bd"]
    return _headrms(o, wn, cs["bd64"]), st_new


def hgrn_fwd(pc, lb, wn, bl, s, layer, tag):
    nc = s // CH

    def body(q_ref, f_ref, i_ref, lb_ref, wn_ref, o_ref, st_ref, st):
        @pl.when(pl.program_id(0) == 0)
        def _():
            st[...] = jnp.zeros_like(st)

        cs = _hgrn_consts()
        for b in range(bl):
            st_ref[b] = st[b]
            o, st_new = _hgrn_chunk_fn(q_ref[b], f_ref[b], i_ref[b], lb_ref[...], wn_ref[...], st[b], cs)
            o_ref[b] = o
            st[b] = st_new

    tok = lambda j: pl.BlockSpec((bl, CH, GW), lambda c: (0, c, j))
    par = pl.BlockSpec((None, 1, GW), lambda c: (layer, 0, 0))
    pc3 = pc.reshape(bl, s, 3 * GW)
    o, states = _call(
        body, f"hgrn_fwd_{tag}", (nc,), [tok(0), tok(1), tok(2), par, par],
        [tok(0), pl.BlockSpec((bl, None, GW, GW), lambda c: (0, c, 0, 0))],
        [_sds((bl, s, GW)), _sds((bl, nc, GW, GW))],
        [pltpu.VMEM((bl, GW, GW), F32)],
    )(pc3, pc3, pc3, lb, wn)
    return o.reshape(bl * s, GW), states


def hgrn_bwd(pc, lb, wn, states, do, bl, s, layer, tag):
    nc = s // CH

    def body(q_ref, f_ref, i_ref, lb_ref, wn_ref, st_ref, do_ref, dc_ref, dlb_ref, dwn_ref, dst):
        c = pl.program_id(0)

        @pl.when(c == 0)
        def _():
            dst[...] = jnp.zeros_like(dst)

        cs = _hgrn_consts()
        dlb_sum = dwn_sum = None
        for b in range(bl):
            _, vjp = jax.vjp(lambda *a: _hgrn_chunk_fn(*a, cs), q_ref[b], f_ref[b], i_ref[b], lb_ref[...],
                             wn_ref[...], st_ref[b])
            dq, df, di, dlb, dwn, dst_in = vjp((do_ref[b], dst[b]))
            dst[b] = dst_in
            dc_ref[b, :, 0:GW] = dq
            dc_ref[b, :, GW:2 * GW] = df
            dc_ref[b, :, 2 * GW:3 * GW] = di
            dlb_sum = dlb if dlb_sum is None else dlb_sum + dlb
            dwn_sum = dwn if dwn_sum is None else dwn_sum + dwn
        _acc(dlb_ref, dlb_sum, c == 0)
        _acc(dwn_ref, dwn_sum, c == 0)

    tok = lambda j: pl.BlockSpec((bl, CH, GW), lambda c: (0, nc - 1 - c, j))
    par = pl.BlockSpec((None, 1, GW), lambda c: (layer, 0, 0))
    acc = pl.BlockSpec((1, GW), lambda c: (0, 0))
    pc3 = pc.reshape(bl, s, 3 * GW)
    dc, dlb, dwn = _call(
        body, f"hgrn_bwd_{tag}", (nc,),
        [tok(0), tok(1), tok(2), par, par, pl.BlockSpec((bl, None, GW, GW), lambda c: (0, nc - 1 - c, 0, 0)), tok(0)],
        [pl.BlockSpec((bl, CH, 3 * GW), lambda c: (0, nc - 1 - c, 0)), acc, acc],
        [_sds((bl, s, 3 * GW)), _sds((1, GW)), _sds((1, GW))],
        [pltpu.VMEM((bl, GW, GW), F32)],
    )(pc3, pc3, pc3, lb, wn, states, do.reshape(bl, s, GW))
    return dc.reshape(bl * s, 3 * GW), dlb, dwn


def _shift_rows(x, k, up):
    n = x.shape[0]
    rr = _iota((n, 1), 0)
    if up:
        return jnp.where(rr < n - k, pltpu.roll(x, n - k, 0), 0.0)
    return jnp.where(rr >= k, pltpu.roll(x, k, 0), 0.0)


def _window_sums(x, up):
    s2 = x + _shift_rows(x, 1, up)
    s4 = s2 + _shift_rows(s2, 2, up)
    s8 = s4 + _shift_rows(s4, 4, up)
    s16 = s8 + _shift_rows(s8, 8, up)
    return s2, s4, s8, s16


def _pool_div(n):
    pos = (_iota((n, 1), 0) + 1).astype(F32)
    return [jnp.minimum(pos, float(w)) for w in (2, 4, 8, 16)]


def _pool_mix(sums, scaled):
    out = None
    for gi, sw in enumerate(sums):
        part = (sw if scaled is None else sw / scaled[gi]) * _hmask(gi)
        out = part if out is None else out + part
    return out


def pool_fwd(pd, wbd, scale, bl, s, layer, tag):
    def body(u_ref, w_ref, sc_ref, o_ref):
        u = u_ref[...]
        pm = _pool_mix(_window_sums(u, False), _pool_div(s)) - u
        o_ref[...] = _dg(_cast(pm), _cast(w_ref[...]), 1, 0) * sc_ref[...]

    seq = pl.BlockSpec((s, GW), lambda b: (b, 0))
    return _call(
        body, f"pool_fwd_{tag}", (bl,),
        [seq, pl.BlockSpec((None, GW, GW), lambda b: (layer, 0, 0)), pl.BlockSpec((None, 1, GW), lambda b: (layer, 0, 0))],
        seq, _sds((bl * s, GW)),
    )(pd, wbd, scale)


def pool_bwd(pd, wbd, scale, do, bl, s, layer, tag):
    def body(u_ref, w_ref, sc_ref, do_ref, du_ref, dw_ref, dsc_ref):
        first = pl.program_id(0) == 0
        u, do = u_ref[...], do_ref[...]
        div = _pool_div(s)
        pm = _pool_mix(_window_sums(u, False), div) - u
        ypre = _dg(_cast(pm), _cast(w_ref[...]), 1, 0)
        dys = do * sc_ref[...]
        _acc(dsc_ref, jnp.sum(do * ypre, axis=0, keepdims=True), first)
        _acc(dw_ref, _dg(_cast(pm), _cast(dys), 0, 0), first)
        dpm = _dg(_cast(dys), _cast(w_ref[...]), 1, 1)
        dsc = [dpm / d for d in div]
        adj = None
        for gi in range(4):
            part = _window_sums(dsc[gi] * _hmask(gi), True)[gi]
            adj = part if adj is None else adj + part
        du_ref[...] = adj - dpm

    seq = pl.BlockSpec((s, GW), lambda b: (b, 0))
    return _call(
        body, f"pool_bwd_{tag}", (bl,),
        [seq, pl.BlockSpec((None, GW, GW), lambda b: (layer, 0, 0)), pl.BlockSpec((None, 1, GW), lambda b: (layer, 0, 0)), seq],
        [seq, pl.BlockSpec((GW, GW), lambda b: (0, 0)), pl.BlockSpec((1, GW), lambda b: (0, 0))],
        [_sds((bl * s, GW)), _sds((GW, GW)), _sds((1, GW))],
    )(pd, wbd, scale, do)


def _mem_prep_fn(mem, g, wk, wv, kw, bd64):
    mn = _rms(mem, g)
    return _headrms(mm(mn, wk), kw, bd64), mm(mn, wv)


def mem_prep_fwd(mem2, g, wkv, kw, bl, layer, tag):
    def body(m_ref, g_ref, wk_ref, wv_ref, kw_ref, k_ref, v_ref):
        k, v = _mem_prep_fn(m_ref[...], g_ref[...], wk_ref[...], wv_ref[...], kw_ref[...], _bdmask() * (1.0 / HD))
        k_ref[...] = k
        v_ref[...] = v

    blk = pl.BlockSpec((N_MEM, GW), lambda b: (b, 0))
    return _call(
        body, f"mem_prep_fwd_{tag}", (bl,),
        [pl.BlockSpec((N_MEM, D_MODEL), lambda b: (b, 0)), pl.BlockSpec((None, 1, D_MODEL), lambda b: (layer, 0, 0)),
         pl.BlockSpec((None, D_MODEL, GW), lambda b: (layer, 0, 0)), pl.BlockSpec((None, D_MODEL, GW), lambda b: (layer, 0, 1)),
         pl.BlockSpec((None, 1, GW), lambda b: (layer, 0, 0))],
        [blk, blk], [_sds((bl * N_MEM, GW))] * 2,
    )(mem2, g, wkv, wkv, kw)


def mem_prep_bwd(mem2, g, wkv, kw, dk, dv, bl, layer, tag):
    def body(m_ref, g_ref, wk_ref, wv_ref, kw_ref, dk_ref, dv_ref, dwk_ref, dwv_ref, dg_ref, dkw_ref):
        first = pl.program_id(0) == 0
        bd64 = _bdmask() * (1.0 / HD)
        _, vjp = jax.vjp(lambda g_, wk, wv, kw_: _mem_prep_fn(m_ref[...], g_, wk, wv, kw_, bd64),
                         g_ref[...], wk_ref[...].astype(F32), wv_ref[...].astype(F32), kw_ref[...])
        dg, dwk, dwv, dkw = vjp((dk_ref[...], dv_ref[...]))
        _acc(dwk_ref, dwk, first)
        _acc(dwv_ref, dwv, first)
        _acc(dg_ref, dg, first)
        _acc(dkw_ref, dkw, first)

    blk = pl.BlockSpec((N_MEM, GW), lambda b: (b, 0))
    return _call(
        body, f"mem_prep_bwd_{tag}", (bl,),
        [pl.BlockSpec((N_MEM, D_MODEL), lambda b: (b, 0)), pl.BlockSpec((None, 1, D_MODEL), lambda b: (layer, 0, 0)),
         pl.BlockSpec((None, D_MODEL, GW), lambda b: (layer, 0, 0)), pl.BlockSpec((None, D_MODEL, GW), lambda b: (layer, 0, 1)),
         pl.BlockSpec((None, 1, GW), lambda b: (layer, 0, 0)), blk, blk],
        [pl.BlockSpec((D_MODEL, GW), lambda b: (0, 0)), pl.BlockSpec((D_MODEL, GW), lambda b: (0, 0)),
         pl.BlockSpec((1, D_MODEL), lambda b: (0, 0)), pl.BlockSpec((1, GW), lambda b: (0, 0))],
        [_sds((D_MODEL, GW)), _sds((D_MODEL, GW)), _sds((1, D_MODEL)), _sds((1, GW))],
    )(mem2, g, wkv, wkv, kw, dk, dv)


def _mem_attn_fn(mq, qw, k, v, bd64):
    qn = _headrms(mq, qw, bd64)
    out = None
    for h in range(NH):
        hm = _hmask(h)
        lg = mm_nt(qn * hm, k) * SCALE
        e = jnp.exp(lg - lax.stop_gradient(jnp.max(lg, axis=-1, keepdims=True)))
        p = e / jnp.sum(e, axis=-1, keepdims=True)
        part = mm(p, v) * hm
        out = part if out is None else out + part
    return out


def mem_attn_fwd(pe, qw, k, v, bl, s, layer, tag):
    nq = s // TQ

    def body(q_ref, qw_ref, k_ref, v_ref, o_ref):
        o_ref[...] = _mem_attn_fn(q_ref[...], qw_ref[...], k_ref[...], v_ref[...], _bdmask() * (1.0 / HD))

    tok = pl.BlockSpec((TQ, GW), lambda b, i: (b * nq + i, 0))
    kv = pl.BlockSpec((N_MEM, GW), lambda b, i: (b, 0))
    return _call(
        body, f"mem_attn_fwd_{tag}", (bl, nq), [tok, pl.BlockSpec((None, 1, GW), lambda b, i: (layer, 0, 0)), kv, kv],
        tok, _sds((bl * s, GW)),
    )(pe, qw, k, v)


def mem_attn_bwd(pe, qw, k, v, do, bl, s, layer, tag):
    nq = s // TQ

    def body(q_ref, qw_ref, k_ref, v_ref, do_ref, dq_ref, dk_ref, dv_ref, dqw_ref):
        i = pl.program_id(1)
        bd64 = _bdmask() * (1.0 / HD)
        _, vjp = jax.vjp(lambda *a: _mem_attn_fn(*a, bd64), q_ref[...], qw_ref[...], k_ref[...], v_ref[...])
        dq, dqw, dk, dv = vjp(do_ref[...])
        dq_ref[...] = dq
        _acc(dk_ref, dk, i == 0)
        _acc(dv_ref, dv, i == 0)
        _acc(dqw_ref, dqw, jnp.logical_and(pl.program_id(0) == 0, i == 0))

    tok = pl.BlockSpec((TQ, GW), lambda b, i: (b * nq + i, 0))
    kv = pl.BlockSpec((N_MEM, GW), lambda b, i: (b, 0))
    return _call(
        body, f"mem_attn_bwd_{tag}", (bl, nq),
        [tok, pl.BlockSpec((None, 1, GW), lambda b, i: (layer, 0, 0)), kv, kv, tok],
        [tok, kv, kv, pl.BlockSpec((1, GW), lambda b, i: (0, 0))],
        [_sds((bl * s, GW)), _sds((bl * N_MEM, GW)), _sds((bl * N_MEM, GW)), _sds((1, GW))],
    )(pe, qw, k, v, do)


def _gate_out_fn(outs, gates, wparts):
    y = None
    for o, g, w in zip(outs, gates, wparts):
        part = mm(o * _silu(g), w)
        y = part if y is None else y + part
    return y


def outproj_fwd(x2, outs, pg, wout, layer, tag):
    t = x2.shape[0]

    def body(x_ref, oa, ob, oc, od, oe, g_ref, w_ref, y_ref):
        outs_ = [r[...] for r in (oa, ob, oc, od, oe)]
        gates = [g_ref[:, j * GW:(j + 1) * GW] for j in range(5)]
        wparts = [w_ref[j * GW:(j + 1) * GW, :] for j in range(5)]
        y_ref[...] = x_ref[...] + _gate_out_fn(outs_, gates, wparts)

    tok = pl.BlockSpec((TQ, GW), lambda i: (i, 0))
    big = pl.BlockSpec((TQ, D_MODEL), lambda i: (i, 0))
    return _call(
        body, f"outproj_fwd_{tag}", (t // TQ,),
        [big] + [tok] * 5 + [pl.BlockSpec((TQ, D_MIX), lambda i: (i, 0)),
                            pl.BlockSpec((None, D_MIX, D_MODEL), lambda i: (layer, 0, 0))],
        big, _sds((t, D_MODEL)),
    )(x2, *outs, pg, wout)


def outproj_bwd(outs, pg, wout, dy, layer, tag):
    t = dy.shape[0]

    def body(oa, ob, oc, od, oe, g_ref, w_ref, dy_ref, da, db, dc, dd, de, dg_ref, dw_ref):
        outs_ = [r[...] for r in (oa, ob, oc, od, oe)]
        gates = [g_ref[:, j * GW:(j + 1) * GW] for j in range(5)]
        wparts = [w_ref[j * GW:(j + 1) * GW, :].astype(F32) for j in range(5)]
        _, vjp = jax.vjp(_gate_out_fn, outs_, gates, wparts)
        douts, dgates, dws = vjp(dy_ref[...])
        for r, val in zip((da, db, dc, dd, de), douts):
            r[...] = val
        first = pl.program_id(0) == 0
        for j in range(5):
            dg_ref[:, j * GW:(j + 1) * GW] = dgates[j]

        @pl.when(first)
        def _():
            for j in range(5):
                dw_ref[j * GW:(j + 1) * GW, :] = dws[j]

        @pl.when(jnp.logical_not(first))
        def _():
            for j in range(5):
                dw_ref[j * GW:(j + 1) * GW, :] += dws[j]

    tok = pl.BlockSpec((TQ, GW), lambda i: (i, 0))
    return _call(
        body, f"outproj_bwd_{tag}", (t // TQ,),
        [tok] * 5 + [pl.BlockSpec((TQ, D_MIX), lambda i: (i, 0)), pl.BlockSpec((None, D_MIX, D_MODEL), lambda i: (layer, 0, 0)),
                     pl.BlockSpec((TQ, D_MODEL), lambda i: (i, 0))],
        [tok] * 5 + [pl.BlockSpec((TQ, D_MIX), lambda i: (i, 0)), pl.BlockSpec((D_MIX, D_MODEL), lambda i: (0, 0))],
        [_sds((t, GW))] * 5 + [_sds((t, D_MIX)), _sds((D_MIX, D_MODEL))],
    )(*outs, pg, wout, dy)


def loss_head(y, tgt):
    t = y.shape[0]

    def body(y_ref, t_ref, l_ref, dy_ref):
        diff = y_ref[...] - t_ref[...]
        dy_ref[...] = diff * (1.0 / D_MODEL)
        part = 0.5 * jnp.sum(jnp.sum(diff * diff, axis=-1, keepdims=True) * (1.0 / D_MODEL), axis=0, keepdims=True)
        _acc(l_ref, jnp.broadcast_to(part, (8, 128)), pl.program_id(0) == 0)

    big = pl.BlockSpec((TQ, D_MODEL), lambda i: (i, 0))
    return _call(body, "loss_head", (t // TQ,), [big, big], [pl.BlockSpec((8, 128), lambda i: (0, 0)), big],
                 [_sds((8, 128)), _sds((t, D_MODEL))])(y, tgt)


def layer_fwd(x2, mem2, p, layer, bl, s):
    tag = f"l{layer}"
    ht, pa, pb, pc, pd, pe, pg, pf = inproj_fwd(x2, p["norm_g"], p["w_in"], layer, tag)
    qn, kn, vb, cq, ck = fox_prep_fwd(pa, pf, p["fox_q_norm"], p["fox_k_norm"], p["fox_f_bias"], bl, s, layer, tag)
    oa, lse = fox_attn_fwd(qn, kn, vb, cq, ck, bl, s, tag)
    ob = sb_attn_fwd(pb, bl, s, tag)
    oc, states = hgrn_fwd(pc, p["lb"], p["hgrn_out_norm"], bl, s, layer, tag)
    od = pool_fwd(pd, p["pool_wbd"], p["pool_scale"], bl, s, layer, tag)
    mk, mv = mem_prep_fwd(mem2, p["mem_norm_g"], p["mem_w_kv"], p["mem_k_norm"], bl, layer, tag)
    oe = mem_attn_fwd(pe, p["mem_q_norm"], mk, mv, bl, s, layer, tag)
    y = outproj_fwd(x2, (oa, ob, oc, od, oe), pg, p["w_out"], layer, tag)
    saved = dict(x2=x2, ht=ht, pa=pa, pb=pb, pc=pc, pd=pd, pe=pe, pg=pg, pf=pf, qn=qn, kn=kn, vb=vb, cq=cq, ck=ck,
                 oa=oa, lse=lse, ob=ob, oc=oc, states=states, od=od, mk=mk, mv=mv, oe=oe)
    return y, saved


def layer_bwd(dy, mem2, p, sv, layer, bl, s):
    tag = f"l{layer}"
    (doa, dob, doc, dod, doe, dg_gates, dwout) = outproj_bwd((sv["oa"], sv["ob"], sv["oc"], sv["od"], sv["oe"]), sv["pg"],
                                                              p["w_out"], dy, layer, tag)
    dqn, dkn, dv, dck = fox_attn_bwd(sv["qn"], sv["kn"], sv["vb"], sv["cq"], sv["ck"], sv["lse"], doa, bl, s, tag)
    d_a, d_f, dqw, dkw, dbias = fox_prep_bwd(sv["pa"], sv["pf"], p["fox_q_norm"], p["fox_k_norm"], p["fox_f_bias"], sv["cq"],
                                             dqn, dkn, dv, dck, bl, s, layer, tag)
    dsq, dsk, dsv = sb_attn_bwd(sv["pb"], dob, bl, s, tag)
    d_c, dlb, dwn = hgrn_bwd(sv["pc"], p["lb"], p["hgrn_out_norm"], sv["states"], doc, bl, s, layer, tag)
    d_d, dwbd, dpscale = pool_bwd(sv["pd"], p["pool_wbd"], p["pool_scale"], dod, bl, s, layer, tag)
    d_e, dmk, dmv, dmqw = mem_attn_bwd(sv["pe"], p["mem_q_norm"], sv["mk"], sv["mv"], doe, bl, s, layer, tag)
    dwk, dwv, dmg, dmkw = mem_prep_bwd(mem2, p["mem_norm_g"], p["mem_w_kv"], p["mem_k_norm"], dmk, dmv, bl, layer, tag)
    dpieces = (d_a, dsq, dsk, dsv, d_c, d_d, d_e, dg_gates, d_f)
    dx, dng = inproj_bwd_dx(sv["x2"], p["norm_g"], p["w_in"], dy, dpieces, layer, tag)
    dwin = jnp.concatenate([matmul_acc(sv["ht"], dp, f"{tag}_{nm}") for (nm, _, _), dp in zip(BWD_PIECES, dpieces)], axis=1)
    grads = dict(norm_g=dng, w_in=dwin, fox_f_bias=dbias, fox_q_norm=dqw, fox_k_norm=dkw, lb=dlb, hgrn_out_norm=dwn,
                 pool_wbd=dwbd, pool_scale=dpscale, mem_norm_g=dmg, mem_w_kv=jnp.concatenate([dwk, dwv], axis=1),
                 mem_q_norm=dmqw, mem_k_norm=dmkw, w_out=dwout)
    return dx, grads


def _tile4(w):
    return jnp.tile(w, (1, NH))[:, None, :]


def prepare_params(norm_g, w_in_p, fox_f_bias, fox_q_norm, fox_k_norm, hgrn_lb_logits, hgrn_out_norm, pool_w, pool_scale,
                   mem_norm_g, mem_w_kv, mem_q_norm, mem_k_norm, w_out):
    p1 = jax.nn.sigmoid(hgrn_lb_logits[1] - hgrn_lb_logits[0])
    lb = jnp.stack([jnp.zeros_like(p1), jnp.clip(p1, 0.0, 1.0 - 1e-6)])
    eye = jnp.eye(4, dtype=F32)
    wbd = jnp.einsum("lgcd,gh->lgchd", pool_w, eye).reshape(2, GW, GW)
    return dict(norm_g=norm_g[:, None, :], w_in=w_in_p, fox_f_bias=jnp.pad(fox_f_bias, ((0, 0), (0, 124)))[:, None, :],
                fox_q_norm=_tile4(fox_q_norm), fox_k_norm=_tile4(fox_k_norm), lb=lb[:, None, :],
                hgrn_out_norm=hgrn_out_norm[:, None, :], pool_wbd=wbd, pool_scale=pool_scale[:, None, :],
                mem_norm_g=mem_norm_g[:, None, :], mem_w_kv=mem_w_kv, mem_q_norm=_tile4(mem_q_norm),
                mem_k_norm=_tile4(mem_k_norm), w_out=w_out)


def local_step(x, mem, tgt, p):
    bl, s, _ = x.shape
    x2, mem2, tgt2 = x.reshape(bl * s, D_MODEL), mem.reshape(bl * N_MEM, D_MODEL), tgt.reshape(bl * s, D_MODEL)
    y0, sv0 = layer_fwd(x2, mem2, p, 0, bl, s)
    y1, sv1 = layer_fwd(y0, mem2, p, 1, bl, s)
    lpart, dy = loss_head(y1, tgt2)
    dx1, g1 = layer_bwd(dy, mem2, p, sv1, 1, bl, s)
    dx0, g0 = layer_bwd(dx1, mem2, p, sv0, 0, bl, s)
    return lpart[0, 0], dx0.reshape(bl, s, D_MODEL), g0, g1


_ANY = pl.BlockSpec(memory_space=pl.ANY)


def _me_and_peers():
    x, y, c = lax.axis_index("x"), lax.axis_index("y"), lax.axis_index("c")
    peers = []
    for k in range(1, N_DEV):
        px = 1 - x if (k >> 2) & 1 else x
        py = 1 - y if (k >> 1) & 1 else y
        pc = 1 - c if k & 1 else c
        peers.append(((px, py, pc), 4 * px + 2 * py + pc))
    return 4 * x + 2 * y + c, peers


def all_gather_rows(xs, tag):
    nl, r, c = xs.shape

    def body(x_ref, o_ref, send_sems, recv_sems, local_sem):
        x, y, cc = lax.axis_index("x"), lax.axis_index("y"), lax.axis_index("c")
        me, sibling = (x, y, cc), (x, y, 1 - cc)
        chips = [(1 - x, y), (x, 1 - y), (1 - x, 1 - y)]

        def rows(px, py, pc):
            return o_ref.at[:, pl.ds((4 * px + 2 * py + pc) * r, r), :]

        def copy(k, block, to, src=None):
            return pltpu.make_async_remote_copy(src_ref=rows(*block) if src is None else src, dst_ref=rows(*block),
                                                send_sem=send_sems.at[k], recv_sem=recv_sems.at[k], device_id=to,
                                                device_id_type=pl.DeviceIdType.MESH)

        mine = pltpu.make_async_copy(x_ref, rows(*me), local_sem)
        mine.start()
        first = [copy(0, me, sibling, src=x_ref)] + [copy(1 + j, me, (*chip, cc), src=x_ref) for j, chip in enumerate(chips)]
        for cp in first:
            cp.start()
        passed = [copy(4 + j, (*chip, cc), sibling) for j, chip in enumerate(chips)]
        for j, chip in enumerate(chips):
            copy(1 + j, (*chip, cc), me).wait_recv()
            passed[j].start()
        copy(0, sibling, me).wait_recv()
        for j, chip in enumerate(chips):
            copy(4 + j, (*chip, 1 - cc), me).wait_recv()
        for cp in first + passed:
            cp.wait_send()
        mine.wait()

    return pl.pallas_call(
        body, name=f"all_gather_{tag}", in_specs=[_ANY], out_specs=_ANY, out_shape=_sds((nl, N_DEV * r, c), xs.dtype),
        scratch_shapes=[pltpu.SemaphoreType.DMA((N_DEV - 1,)), pltpu.SemaphoreType.DMA((N_DEV - 1,)), pltpu.SemaphoreType.DMA],
    )(xs)


def exchange_cores(part, tag):
    nl, _, _, r, c = part.shape

    def body(p_ref, theirs_ref, send_sems, recv_sems):
        x, y, cc = lax.axis_index("x"), lax.axis_index("y"), lax.axis_index("c")
        copies = []
        for l in range(nl):
            for q in range(4):
                k = l * 4 + q
                copies.append(pltpu.make_async_remote_copy(
                    src_ref=p_ref.at[l, q, pl.ds(1 - cc, 1)], dst_ref=theirs_ref.at[l, q], send_sem=send_sems.at[k],
                    recv_sem=recv_sems.at[k], device_id=(x, y, 1 - cc), device_id_type=pl.DeviceIdType.MESH))
        for cp in copies:
            cp.start()
        for cp in copies:
            cp.wait()

    nsem = pltpu.SemaphoreType.DMA((nl * 4,))
    return pl.pallas_call(
        body, name=f"exchange_cores_{tag}", in_specs=[_ANY], out_specs=_ANY, out_shape=_sds((nl, 4, 1, r, c), part.dtype),
        scratch_shapes=[nsem, nsem],
    )(part)


def add_core_halves(part5, theirs, core, tag):
    nl, _, _, r, c = part5.shape
    tr = 64 if r % 64 == 0 else 32

    def body(core_ref, a_ref, b_ref, o_ref):
        o_ref[...] = _cast(a_ref[...] + b_ref[...])

    blk = lambda which: pl.BlockSpec((None, None, None, tr, c), lambda l, q, i, cref: (l, q, cref[0] if which else 0, i, 0))
    return pl.pallas_call(
        body, name=f"add_core_halves_{tag}", out_shape=_sds((nl, 4, 1, r, c), _MMT),
        grid_spec=pltpu.PrefetchScalarGridSpec(num_scalar_prefetch=1, grid=(nl, 4, r // tr), in_specs=[blk(True), blk(False)],
                                               out_specs=blk(False)),
        compiler_params=pltpu.CompilerParams(dimension_semantics=("arbitrary",) * 3, vmem_limit_bytes=VMEM_LIMIT_BYTES),
    )(core, part5, theirs)


def exchange_chips(s4, tag):
    nl, _, _, r, c = s4.shape

    def body(s_ref, o_ref, send_sems, recv_sems, local_sem):
        x, y, cc = lax.axis_index("x"), lax.axis_index("y"), lax.axis_index("c")
        local = pltpu.make_async_copy(s_ref.at[:, pl.ds(2 * x + y, 1)], o_ref.at[0], local_sem)
        local.start()
        copies = []
        for k in range(1, 4):
            px = 1 - x if (k >> 1) & 1 else x
            py = 1 - y if k & 1 else y
            copies.append(pltpu.make_async_remote_copy(
                src_ref=s_ref.at[:, pl.ds(2 * px + py, 1)], dst_ref=o_ref.at[k], send_sem=send_sems.at[k - 1],
                recv_sem=recv_sems.at[k - 1], device_id=(px, py, cc), device_id_type=pl.DeviceIdType.MESH))
        for cp in copies:
            cp.start()
        for cp in copies:
            cp.wait()
        local.wait()

    return pl.pallas_call(
        body, name=f"exchange_chips_{tag}", in_specs=[_ANY], out_specs=_ANY, out_shape=_sds((4, nl, 1, 1, r, c), s4.dtype),
        scratch_shapes=[pltpu.SemaphoreType.DMA((3,)), pltpu.SemaphoreType.DMA((3,)), pltpu.SemaphoreType.DMA],
    )(s4)


def exchange_row_blocks(part, tag):
    nl, r8, c = part.shape
    r = r8 // N_DEV

    def body(p_ref, o_ref, send_sems, recv_sems, local_sem):
        me, peers = _me_and_peers()
        rows = lambda idx: p_ref.at[:, pl.ds(idx * r, r), :]
        mine = pltpu.make_async_copy(rows(me), o_ref.at[0], local_sem)
        mine.start()
        copies = [pltpu.make_async_remote_copy(src_ref=rows(idx), dst_ref=o_ref.at[k + 1], send_sem=send_sems.at[k],
                                               recv_sem=recv_sems.at[k], device_id=dev, device_id_type=pl.DeviceIdType.MESH)
                  for k, (dev, idx) in enumerate(peers)]
        for cp in copies:
            cp.start()
        for cp in copies:
            cp.wait()
        mine.wait()

    return pl.pallas_call(
        body, name=f"exchange_{tag}", in_specs=[_ANY], out_specs=_ANY, out_shape=_sds((N_DEV, nl, r, c), part.dtype),
        scratch_shapes=[pltpu.SemaphoreType.DMA((N_DEV - 1,)), pltpu.SemaphoreType.DMA((N_DEV - 1,)), pltpu.SemaphoreType.DMA],
    )(part)


def _row_tile(rows):
    if rows <= 512 and rows % 64:
        return rows
    for t in (64, 40, 32, 16, 8):
        if rows % t == 0:
            return t
    return rows


def sum_slots(slots, tag):
    ns, rows, c = slots.shape
    tr = _row_tile(rows)

    def body(s_ref, o_ref):
        acc = s_ref[0].astype(F32)
        for k in range(1, ns):
            acc = acc + s_ref[k].astype(F32)
        o_ref[...] = acc

    return _call(body, f"sum_slots_{tag}", (rows // tr,), [pl.BlockSpec((ns, tr, c), lambda i: (0, i, 0))],
                 pl.BlockSpec((tr, c), lambda i: (i, 0)), _sds((rows, c)))(slots)


def _adamw(w, g, m, v):
    m = ADAM_B1 * m + (1.0 - ADAM_B1) * g
    v = ADAM_B2 * v + (1.0 - ADAM_B2) * (g * g)
    m_hat = m / (1.0 - ADAM_B1 ** ADAM_STEP)
    v_hat = v / (1.0 - ADAM_B2 ** ADAM_STEP)
    delta = -ADAM_LR * (m_hat / (jnp.sqrt(v_hat) + ADAM_EPS) + ADAM_WD * w)
    return delta, m, v


def adam_update(w, m, v, g, tag, slots=False):
    rows, c = w.shape
    tr = _row_tile(rows)
    ns = g.shape[0] if slots else 0

    def body(w_ref, m_ref, v_ref, g_ref, go_ref, d_ref, mo_ref, vo_ref):
        if slots:
            g = g_ref[0].astype(F32)
            for k in range(1, ns):
                g = g + g_ref[k].astype(F32)
        else:
            g = g_ref[...]
        d, mn, vn = _adamw(w_ref[...], g, m_ref[...], v_ref[...])
        go_ref[...] = g
        d_ref[...] = d
        mo_ref[...] = mn
        vo_ref[...] = vn

    blk = pl.BlockSpec((tr, c), lambda i: (i, 0))
    gspec = pl.BlockSpec((ns, tr, c), lambda i: (0, i, 0)) if slots else blk
    return _call(body, f"adam_{tag}", (rows // tr,), [blk, blk, blk, gspec], [blk] * 4, [_sds((rows, c))] * 4)(w, m, v, g)


_SMALL = (("norm_g", (2, 1024)), ("fox_f_bias", (2, 4)), ("fox_q_norm", (2, 64)), ("fox_k_norm", (2, 64)),
          ("hgrn_lb_logits", (2, 256)), ("hgrn_out_norm", (2, 256)), ("pool_w", (2, 4, 64, 64)), ("pool_scale", (2, 256)),
          ("mem_norm_g", (2, 1024)), ("mem_q_norm", (2, 64)), ("mem_k_norm", (2, 64)))
_SLAB_ROWS = 312


def pack_small(d):
    flat = jnp.concatenate([d[n].reshape(-1) for n, _ in _SMALL])
    return jnp.pad(flat, (0, _SLAB_ROWS * 128 - flat.shape[0])).reshape(_SLAB_ROWS, 128)


def unpack_small(slab):
    flat, out, off = slab.reshape(-1), {}, 0
    for n, shp in _SMALL:
        size = 1
        for e in shp:
            size *= e
        out[n] = flat[off:off + size].reshape(shp)
        off += size
    return out


def small_grads(g0, g1, lb_logits):
    st = lambda f: jnp.stack([f(g0), f(g1)])
    heads = lambda a: a.reshape(NH, HD).sum(0)
    p1 = jax.nn.sigmoid(lb_logits[1] - lb_logits[0])
    inside = (p1 > 0.0) & (p1 < 1.0 - 1e-6)
    dl1 = jnp.where(inside, g1["lb"][0] * p1 * (1.0 - p1), 0.0)
    diag = lambda a: jnp.stack([a.reshape(4, HD, 4, HD)[i, :, i, :] for i in range(4)])
    return dict(norm_g=st(lambda g: g["norm_g"][0]), fox_f_bias=st(lambda g: g["fox_f_bias"][0, :NH]),
                fox_q_norm=st(lambda g: heads(g["fox_q_norm"])), fox_k_norm=st(lambda g: heads(g["fox_k_norm"])),
                hgrn_lb_logits=jnp.stack([-dl1, dl1]), hgrn_out_norm=st(lambda g: g["hgrn_out_norm"][0]),
                pool_w=st(lambda g: diag(g["pool_wbd"])), pool_scale=st(lambda g: g["pool_scale"][0]),
                mem_norm_g=st(lambda g: g["mem_norm_g"][0]), mem_q_norm=st(lambda g: heads(g["mem_q_norm"])),
                mem_k_norm=st(lambda g: heads(g["mem_k_norm"])))


def kernel(x, mem, norm_g, w_in, fox_f_bias, fox_q_norm, fox_k_norm, hgrn_lb_logits, hgrn_out_norm, pool_w, pool_scale, mem_norm_g, mem_w_kv, mem_q_norm, mem_k_norm, w_out, loss_target, m_norm_g, m_w_in, m_fox_f_bias, m_fox_q_norm, m_fox_k_norm, m_hgrn_lb_logits, m_hgrn_out_norm, m_pool_w, m_pool_scale, m_mem_norm_g, m_mem_w_kv, m_mem_q_norm, m_mem_k_norm, m_w_out, v_norm_g, v_w_in, v_fox_f_bias, v_fox_q_norm, v_fox_k_norm, v_hgrn_lb_logits, v_hgrn_out_norm, v_pool_w, v_pool_scale, v_mem_norm_g, v_mem_w_kv, v_mem_q_norm, v_mem_k_norm, v_w_out):
    given = dict(norm_g=(norm_g, m_norm_g, v_norm_g), w_in=(w_in, m_w_in, v_w_in), fox_f_bias=(fox_f_bias, m_fox_f_bias, v_fox_f_bias),
                 fox_q_norm=(fox_q_norm, m_fox_q_norm, v_fox_q_norm), fox_k_norm=(fox_k_norm, m_fox_k_norm, v_fox_k_norm),
                 hgrn_lb_logits=(hgrn_lb_logits, m_hgrn_lb_logits, v_hgrn_lb_logits),
                 hgrn_out_norm=(hgrn_out_norm, m_hgrn_out_norm, v_hgrn_out_norm), pool_w=(pool_w, m_pool_w, v_pool_w),
                 pool_scale=(pool_scale, m_pool_scale, v_pool_scale), mem_norm_g=(mem_norm_g, m_mem_norm_g, v_mem_norm_g),
                 mem_w_kv=(mem_w_kv, m_mem_w_kv, v_mem_w_kv), mem_q_norm=(mem_q_norm, m_mem_q_norm, v_mem_q_norm),
                 mem_k_norm=(mem_k_norm, m_mem_k_norm, v_mem_k_norm), w_out=(w_out, m_w_out, v_w_out))
    order = ("norm_g", "w_in", "fox_f_bias", "fox_q_norm", "fox_k_norm", "hgrn_lb_logits", "hgrn_out_norm", "pool_w",
             "pool_scale", "mem_norm_g", "mem_w_kv", "mem_q_norm", "mem_k_norm", "w_out")

    w_in_full = all_gather_rows(_cast(permute_cols(w_in)), "w_in")
    w_out_full = all_gather_rows(_cast(w_out), "w_out")
    w_kv_full = all_gather_rows(_cast(mem_w_kv), "w_kv")
    p = prepare_params(norm_g, w_in_full, fox_f_bias, fox_q_norm, fox_k_norm, hgrn_lb_logits, hgrn_out_norm, pool_w,
                       pool_scale, mem_norm_g, w_kv_full, mem_q_norm, mem_k_norm, w_out_full)

    loss_part, grad_x, g0, g1 = local_step(x, mem, loss_target, p)
    loss = lax.psum(loss_part, ("x", "y", "c"))

    res = {}
    core = lax.axis_index("c").astype(jnp.int32).reshape(1)

    def sharded(name, g2, unperm=False, two_stage=True):
        w, m, v = given[name]
        nl, r, c = w.shape
        cp = g2.shape[-1]
        if two_stage:
            part5 = g2.reshape(nl, 4, 2, r, cp)
            s4 = add_core_halves(part5, exchange_cores(part5, name), core, name)
            slots = exchange_chips(s4, name).reshape(4, nl * r, cp)
        else:
            slots = exchange_row_blocks(g2, name).reshape(N_DEV, nl * r, cp)
        if unperm:
            g = sum_slots(slots, name).reshape(nl, r, cp)
            out = adam_update(w.reshape(nl * r, c), m.reshape(nl * r, c), v.reshape(nl * r, c),
                              unpermute_cols(g).reshape(nl * r, c), name)
        else:
            out = adam_update(w.reshape(nl * r, c), m.reshape(nl * r, c), v.reshape(nl * r, c), slots, name, slots=True)
        res[name] = tuple(o.reshape(nl, r, c) for o in out)

    sharded("w_in", jnp.stack([g0["w_in"], g1["w_in"]]), unperm=True)
    sharded("w_out", jnp.stack([g0["w_out"], g1["w_out"]]))
    sharded("mem_w_kv", _cast(jnp.stack([g0["mem_w_kv"], g1["mem_w_kv"]])), two_stage=False)

    gsmall = pack_small(small_grads(g0, g1, hgrn_lb_logits))
    gathered = all_gather_rows(gsmall[None], "small").reshape(N_DEV, _SLAB_ROWS, 128)
    slabs = adam_update(*[pack_small({n: given[n][j] for n, _ in _SMALL}) for j in range(3)], gathered, "small", slots=True)
    small = [unpack_small(sl) for sl in slabs]
    for n, _ in _SMALL:
        res[n] = tuple(small[j][n] for j in range(4))

    return (loss, grad_x, *[res[n][0] for n in order], *[res[n][1] for n in order], *[res[n][2] for n in order],
            *[res[n][3] for n in order])
```

```python
import functools

import jax
import jax.numpy as jnp
from jax import lax
from jax.experimental import pallas as pl
from jax.experimental.pallas import tpu as pltpu

F32 = jnp.float32
BF = jnp.bfloat16
_MMT = BF

D_MODEL = 1024
GW = 256
HD = 64
NH = 4
CH = 64
N_MEM = 256
D_IN = 4100
D_INP = 4224
D_MIX = 1280
EPS = 1e-6
NEG_BIG = -1e30
LB_FLOOR = 1e-30
SCALE = HD ** -0.5
TQ = 256
N_DEV = 8
VMEM_LIMIT_BYTES = 56 * 1024 * 1024

ADAM_LR = 0.001
ADAM_B1 = 0.9
ADAM_B2 = 0.999
ADAM_EPS = 1e-08
ADAM_WD = 0.01
ADAM_STEP = 10

PIECES = (("A", 0, 768), ("B", 768, 768), ("C", 1536, 768), ("D", 2304, 256), ("E", 2560, 256),
          ("G", 2816, 1280), ("F", 4096, 128))
BWD_PIECES = (("A", 0, 768), ("Bq", 768, 256), ("Bk", 1024, 256), ("Bv", 1280, 256), ("C", 1536, 768),
              ("D", 2304, 256), ("E", 2560, 256), ("G", 2816, 1280), ("F", 4096, 128))
_ORIG = dict(fq=(0, 256), fk=(256, 512), fv=(512, 768), fg=(768, 1024), ff=(1024, 1028), sq=(1028, 1284),
             sk=(1284, 1540), sv=(1540, 1796), sg=(1796, 2052), hq=(2052, 2308), hf=(2308, 2564),
             hi=(2564, 2820), hg=(2820, 3076), pv=(3076, 3332), pg=(3332, 3588), mq=(3588, 3844), mg=(3844, 4100))
_PERM_ORDER = ("fq", "fk", "fv", "sq", "sk", "sv", "hq", "hf", "hi", "pv", "mq", "fg", "sg", "hg", "pg", "mg", "ff")
_ORIG_ORDER = ("fq", "fk", "fv", "fg", "ff", "sq", "sk", "sv", "sg", "hq", "hf", "hi", "hg", "pv", "pg", "mq", "mg")


def permute_cols(w):
    parts = [w[..., _ORIG[n][0]:_ORIG[n][1]] for n in _PERM_ORDER]
    parts.append(jnp.zeros(w.shape[:-1] + (D_INP - D_IN,), w.dtype))
    return jnp.concatenate(parts, axis=-1)


def unpermute_cols(g):
    start, off = {}, 0
    for n in _PERM_ORDER:
        start[n] = off
        off += _ORIG[n][1] - _ORIG[n][0]
    return jnp.concatenate([g[..., start[n]:start[n] + _ORIG[n][1] - _ORIG[n][0]] for n in _ORIG_ORDER], axis=-1)


def _cast(a):
    return a.astype(_MMT)


def _dg(a, b, ca, cb):
    return lax.dot_general(a, b, (((ca,), (cb,)), ((), ())), preferred_element_type=F32)


@jax.custom_vjp
def mm(a, b):
    return _dg(_cast(a), _cast(b), 1, 0)


@jax.custom_vjp
def mm_nt(a, b):
    return _dg(_cast(a), _cast(b), 1, 1)


@jax.custom_vjp
def mm_tn(a, b):
    return _dg(_cast(a), _cast(b), 0, 0)


mm.defvjp(lambda a, b: (mm(a, b), (a, b)),
          lambda r, g: (mm_nt(g, r[1]).astype(r[0].dtype), mm_tn(r[0], g).astype(r[1].dtype)))
mm_nt.defvjp(lambda a, b: (mm_nt(a, b), (a, b)),
             lambda r, g: (mm(g, r[1]).astype(r[0].dtype), mm_tn(g, r[0]).astype(r[1].dtype)))
mm_tn.defvjp(lambda a, b: (mm_tn(a, b), (a, b)),
             lambda r, g: (mm_nt(r[1], g).astype(r[0].dtype), mm(r[0], g).astype(r[1].dtype)))


def _split(a):
    hi = a.astype(_MMT)
    lo = (a - hi.astype(F32)).astype(_MMT)
    return hi, lo


@jax.custom_vjp
def xr(a, c, ct):
    hi, lo = _split(a)
    cc = _cast(c)
    return _dg(hi, cc, 1, 0) + _dg(lo, cc, 1, 0)


@jax.custom_vjp
def xl(c, ct, a):
    hi, lo = _split(a)
    cc = _cast(c)
    return _dg(cc, hi, 1, 0) + _dg(cc, lo, 1, 0)


xr.defvjp(lambda a, c, ct: (xr(a, c, ct), (c, ct)),
          lambda r, g: (xr(g, r[1], r[0]), jnp.zeros_like(r[0]), jnp.zeros_like(r[1])))
xl.defvjp(lambda c, ct, a: (xl(c, ct, a), (c, ct)),
          lambda r, g: (jnp.zeros_like(r[0]), jnp.zeros_like(r[1]), xl(r[1], r[0], g)))


def _iota(shape, dim):
    return lax.broadcasted_iota(jnp.int32, shape, dim)


def _hmask(h, n=GW):
    lane = _iota((1, n), 1)
    return ((lane >= h * HD) & (lane < (h + 1) * HD)).astype(F32)


def _bdmask(n=GW):
    return ((_iota((n, n), 0) >> 6) == (_iota((n, n), 1) >> 6)).astype(F32)


def _tri(n, kind="le"):
    r, c = _iota((n, n), 0), _iota((n, n), 1)
    return {"le": c <= r, "ge": c >= r, "gt": c > r, "lt": c < r}[kind].astype(F32)


def _onehot_lane(h, n=128):
    return (_iota((1, n), 1) == h).astype(F32)


def _logsig(x):
    return jnp.minimum(x, 0.0) - jnp.log1p(jnp.exp(-jnp.abs(x)))


def _sigmoid(x):
    return 0.5 * (jnp.tanh(0.5 * x) + 1.0)


def _silu(x):
    return x * _sigmoid(x)


def _rms(x, g):
    return x * lax.rsqrt(jnp.mean(x * x, axis=-1, keepdims=True) + EPS) * g


def _headrms(x, w, bd64):
    ms = xr(x * x, bd64, bd64)
    return x * lax.rsqrt(ms + EPS) * w


def _call(body, name, grid, in_specs, out_specs, out_shape, scratch=()):
    return pl.pallas_call(
        body, name=name, grid=grid, in_specs=in_specs, out_specs=out_specs, out_shape=out_shape,
        scratch_shapes=list(scratch),
        compiler_params=pltpu.CompilerParams(dimension_semantics=("arbitrary",) * len(grid),
                                             vmem_limit_bytes=VMEM_LIMIT_BYTES))


def _sds(shape, dtype=F32):
    return jax.ShapeDtypeStruct(shape, dtype)


def _acc(ref, val, first):
    @pl.when(first)
    def _():
        ref[...] = val

    @pl.when(jnp.logical_not(first))
    def _():
        ref[...] += val


def inproj_fwd(x2, g, w, layer, tag):
    t = x2.shape[0]

    def body(x_ref, g_ref, w_ref, ht_ref, *outs):
        h = _rms(x_ref[...], g_ref[...])
        hb = _cast(h)
        ht_ref[...] = _cast(h.T)
        for (_, c0, wd), o in zip(PIECES, outs):
            o[...] = _dg(hb, _cast(w_ref[:, c0:c0 + wd]), 1, 0)

    return _call(
        body, f"inproj_fwd_{tag}", (t // TQ,),
        [pl.BlockSpec((TQ, D_MODEL), lambda i: (i, 0)),
         pl.BlockSpec((None, 1, D_MODEL), lambda i: (layer, 0, 0)),
         pl.BlockSpec((None, D_MODEL, D_INP), lambda i: (layer, 0, 0))],
        [pl.BlockSpec((D_MODEL, TQ), lambda i: (0, i))] + [pl.BlockSpec((TQ, wd), lambda i: (i, 0)) for _, _, wd in PIECES],
        [_sds((D_MODEL, t), _MMT)] + [_sds((t, wd)) for _, _, wd in PIECES],
    )(x2, g, w)


def inproj_bwd_dx(x2, g, w, dy, dpieces, layer, tag):
    t = x2.shape[0]

    def body(x_ref, g_ref, w_ref, dy_ref, *rest):
        dps, (dx_ref, dg_ref) = rest[:len(BWD_PIECES)], rest[len(BWD_PIECES):]
        dh = None
        for (_, c0, wd), dp in zip(BWD_PIECES, dps):
            part = _dg(_cast(dp[...]), _cast(w_ref[:, c0:c0 + wd]), 1, 1)
            dh = part if dh is None else dh + part
        _, vjp = jax.vjp(_rms, x_ref[...], g_ref[...])
        dx, dg = vjp(dh)
        dx_ref[...] = dy_ref[...] + dx
        _acc(dg_ref, dg, pl.program_id(0) == 0)

    return _call(
        body, f"inproj_bwd_dx_{tag}", (t // TQ,),
        [pl.BlockSpec((TQ, D_MODEL), lambda i: (i, 0)),
         pl.BlockSpec((None, 1, D_MODEL), lambda i: (layer, 0, 0)),
         pl.BlockSpec((None, D_MODEL, D_INP), lambda i: (layer, 0, 0)),
         pl.BlockSpec((TQ, D_MODEL), lambda i: (i, 0))] + [pl.BlockSpec((TQ, wd), lambda i: (i, 0)) for _, _, wd in BWD_PIECES],
        [pl.BlockSpec((TQ, D_MODEL), lambda i: (i, 0)), pl.BlockSpec((1, D_MODEL), lambda i: (0, 0))],
        [_sds((t, D_MODEL)), _sds((1, D_MODEL))],
    )(x2, g, w, dy, *dpieces)


def matmul_acc(at, b, tag):
    m, t = at.shape
    n = b.shape[1]
    tn = {1280: 640, 768: 768}.get(n, n)
    tk = 2048 if t % 2048 == 0 else (512 if t % 512 == 0 else TQ)

    def body(a_ref, b_ref, o_ref):
        _acc(o_ref, _dg(_cast(a_ref[...]), _cast(b_ref[...]), 1, 0), pl.program_id(1) == 0)

    return _call(
        body, f"matmul_acc_{tag}", (n // tn, t // tk),
        [pl.BlockSpec((m, tk), lambda j, i: (0, i)), pl.BlockSpec((tk, tn), lambda j, i: (i, j))],
        pl.BlockSpec((m, tn), lambda j, i: (0, j)),
        _sds((m, n)),
    )(at, b)


def _fox_prep_fn(q, k, ff, qw, kw, bias, carry, bd64, tri, trit, last):
    qn = _headrms(q, qw, bd64)
    kn = _headrms(k, kw, bd64)
    lf = _logsig(ff + bias)
    c = xl(tri, trit, lf) + carry
    return qn, kn, c, jnp.sum(c * last, axis=0, keepdims=True)


def _prep_consts():
    return _bdmask() * (1.0 / HD), _tri(TQ), _tri(TQ, "ge"), (_iota((TQ, 1), 0) == TQ - 1).astype(F32)


def fox_prep_fwd(pa, pf, qw, kw, bias, bl, s, layer, tag):
    nq = s // TQ

    def body(q_ref, k_ref, v_ref, f_ref, qw_ref, kw_ref, b_ref, qn_ref, kn_ref, vb_ref, cq_ref, ck_ref, carry):
        @pl.when(pl.program_id(1) == 0)
        def _():
            carry[...] = jnp.zeros_like(carry)

        qn, kn, c, cl = _fox_prep_fn(q_ref[...], k_ref[...], f_ref[...], qw_ref[...], kw_ref[...], b_ref[...],
                                     carry[...], *_prep_consts())
        carry[...] = cl
        qn_ref[...] = _cast(qn)
        kn_ref[...] = _cast(kn)
        vb_ref[...] = _cast(v_ref[...])
        cq_ref[...] = c
        ck_ref[...] = c.T[0:8, :]

    tok = lambda j: pl.BlockSpec((TQ, GW), lambda b, i: (b * nq + i, j))
    par = lambda n: pl.BlockSpec((None, 1, n), lambda b, i: (layer, 0, 0))
    return _call(
        body, f"fox_prep_fwd_{tag}", (bl, nq),
        [tok(0), tok(1), tok(2), pl.BlockSpec((TQ, 128), lambda b, i: (b * nq + i, 0)), par(GW), par(GW), par(128)],
        [tok(0), tok(0), tok(0), pl.BlockSpec((TQ, 128), lambda b, i: (b * nq + i, 0)),
         pl.BlockSpec((None, 8, TQ), lambda b, i: (b, 0, i))],
        [_sds((bl * s, GW), _MMT)] * 3 + [_sds((bl * s, 128)), _sds((bl, 8, s))],
        [pltpu.VMEM((1, 128), F32)],
    )(pa, pa, pa, pf, qw, kw, bias)


def fox_prep_bwd(pa, pf, qw, kw, bias, cq, dqn, dkn, dv, dck, bl, s, layer, tag):
    nq = s // TQ

    def body(q_ref, k_ref, f_ref, qw_ref, kw_ref, b_ref, cq_ref, cprev_ref, dqn_ref, dkn_ref, dv_ref, dck_ref,
             da_ref, df_ref, dqw_ref, dkw_ref, db_ref, dcarry):
        i = pl.program_id(1)
        first = jnp.logical_and(pl.program_id(0) == 0, i == 0)

        @pl.when(i == 0)
        def _():
            dcarry[...] = jnp.zeros_like(dcarry)

        last = (_iota((TQ, 1), 0) == TQ - 1).astype(F32)
        carry_in = jnp.where(i == nq - 1, 0.0, jnp.sum(cprev_ref[...] * last, axis=0, keepdims=True))
        consts = _prep_consts()
        _, vjp = jax.vjp(lambda *a: _fox_prep_fn(*a, *consts), q_ref[...], k_ref[...], f_ref[...], qw_ref[...],
                         kw_ref[...], b_ref[...], carry_in)
        dc = dck_ref[...].T
        dq, dk, dff, dqw, dkw, dbias, dcin = vjp((dqn_ref[...], dkn_ref[...], dc, dcarry[...]))
        dcarry[...] = dcin
        da_ref[:, 0:GW] = dq
        da_ref[:, GW:2 * GW] = dk
        da_ref[:, 2 * GW:3 * GW] = dv_ref[...]
        df_ref[...] = dff
        _acc(dqw_ref, dqw, first)
        _acc(dkw_ref, dkw, first)
        _acc(db_ref, dbias, first)

    rv = lambda b, i: b * nq + (nq - 1 - i)
    tok = lambda j: pl.BlockSpec((TQ, GW), lambda b, i: (rv(b, i), j))
    tok0 = pl.BlockSpec((TQ, GW), lambda b, i: (rv(b, i), 0))
    t128 = pl.BlockSpec((TQ, 128), lambda b, i: (rv(b, i), 0))
    prev = pl.BlockSpec((TQ, 128), lambda b, i: (jnp.maximum(rv(b, i) - 1, 0), 0))
    par = lambda n: pl.BlockSpec((None, 1, n), lambda b, i: (layer, 0, 0))
    acc = lambda n: pl.BlockSpec((1, n), lambda b, i: (0, 0))
    return _call(
        body, f"fox_prep_bwd_{tag}", (bl, nq),
        [tok(0), tok(1), t128, par(GW), par(GW), par(128), t128, prev, tok0, tok0, tok0,
         pl.BlockSpec((None, 128, TQ), lambda b, i: (b, 0, nq - 1 - i))],
        [pl.BlockSpec((TQ, 3 * GW), lambda b, i: (rv(b, i), 0)), t128, acc(GW), acc(GW), acc(128)],
        [_sds((bl * s, 3 * GW)), _sds((bl * s, 128)), _sds((1, GW)), _sds((1, GW)), _sds((1, 128))],
        [pltpu.VMEM((1, 128), F32)],
    )(pa, pa, pf, qw, kw, bias, cq, cq, dqn, dkn, dv, dck)


def _lane_pick(x, h):
    return jnp.sum(x * _onehot_lane(h), axis=-1, keepdims=True)


TA_BIG = 256
TK_FOX = 512
TK_SB = 256


def _stack_heads(x, scale=1.0):
    return _cast(jnp.concatenate([x * (_hmask(h) * scale) for h in range(NH)], axis=0))


def _stack_cols(x):
    return jnp.concatenate([_lane_pick(x, h) for h in range(NH)], axis=0)


def _spread_heads(col):
    ta = col.shape[0] // NH
    return sum(col[h * ta:(h + 1) * ta] * _hmask(h) for h in range(NH))


def _lanes_cat(w):
    ta = w.shape[0] // NH
    return jnp.concatenate([w[h * ta:(h + 1) * ta] for h in range(NH)], axis=1)


def _mask_stack(x):
    return _cast(jnp.concatenate([x * _hmask(h).astype(x.dtype) for h in range(NH)], axis=0))


def _stack_rows(i, ta):
    return i * ta + (_iota((NH * ta, 1), 0) & (ta - 1))


def _n_key_tiles(i, tk, ta):
    return lax.shift_right_logical(i * ta, tk.bit_length() - 1) + 1


def fox_attn_fwd(qn, kn, vb, cq, ck, bl, s, tag):
    TA, TK = min(TA_BIG, s), min(TK_FOX, s)
    nq, SROWS = s // TA, NH * TA

    def body(q_ref, k_ref, v_ref, cq_ref, ck_ref, o_ref, lse_ref, acc, vst):
        i = pl.program_id(1)

        @pl.when(i == 0)
        def _():
            _fill_stacked(vst, v_ref, s, TK)

        qs = _stack_heads(q_ref[...].astype(F32), SCALE)
        cqs = _stack_cols(cq_ref[...])
        row = _stack_rows(i, TA)
        acc[...] = jnp.zeros_like(acc)

        def step(j, ml):
            m, l = ml
            ks = pl.ds(pl.multiple_of(j * TK, TK), TK)
            ckb = jnp.concatenate([jnp.broadcast_to(ck_ref[h:h + 1, ks], (TA, TK)) for h in range(NH)], axis=0)
            sc = _dg(qs, k_ref[ks, :], 1, 1) + cqs - ckb
            col = j * TK + _iota((1, TK), 1)
            sc = jnp.where(col <= row, sc, NEG_BIG)
            m_new = jnp.maximum(m, jnp.max(sc, axis=-1, keepdims=True))
            alpha = jnp.exp(m - m_new)
            p = jnp.exp(sc - m_new)
            vs = vst[pl.ds(pl.multiple_of(j * NH * TK, NH * TK), NH * TK), :]
            acc[...] = _spread_heads(alpha) * acc[...] + _dg(_lanes_cat(_cast(p)), vs, 1, 0)
            return m_new, alpha * l + jnp.sum(p, axis=-1, keepdims=True)

        m, l = lax.fori_loop(0, _n_key_tiles(i, TK, TA), step, (jnp.full((SROWS, 1), NEG_BIG, F32), jnp.zeros((SROWS, 1), F32)))
        o_ref[...] = acc[...] / _spread_heads(l)
        lse_h = m + jnp.log(l)
        lse_ref[...] = sum(lse_h[h * TA:(h + 1) * TA] * _onehot_lane(h) for h in range(NH))

    tok = pl.BlockSpec((TA, GW), lambda b, i: (b * nq + i, 0))
    seq = pl.BlockSpec((s, GW), lambda b, i: (b, 0))
    t128 = pl.BlockSpec((TA, 128), lambda b, i: (b * nq + i, 0))
    return _call(
        body, f"fox_attn_fwd_{tag}", (bl, nq),
        [tok, seq, seq, t128, pl.BlockSpec((None, 8, s), lambda b, i: (b, 0, 0))],
        [tok, t128], [_sds((bl * s, GW)), _sds((bl * s, 128))],
        [pltpu.VMEM((TA, GW), F32), pltpu.VMEM((NH * s, GW), _MMT)],
    )(qn, kn, vb, cq, ck)


def fox_attn_bwd(qn, kn, vb, cq, ck, lse, do, bl, s, tag):
    TA, TK = min(TA_BIG, s), min(TK_FOX, s)
    nq, SROWS = s // TA, NH * TA

    def body(q_ref, k_ref, v_ref, cq_ref, ck_ref, lse_ref, do_ref, dq_ref, dk_ref, dv_ref, dck_ref, dqa, p_s, dp_s, kst):
        i = pl.program_id(1)

        @pl.when(i == 0)
        def _():
            dk_ref[...] = jnp.zeros_like(dk_ref)
            dv_ref[...] = jnp.zeros_like(dv_ref)
            dck_ref[...] = jnp.zeros_like(dck_ref)
            _fill_stacked(kst, k_ref, s, TK)

        qs = _stack_heads(q_ref[...].astype(F32), SCALE)
        dos = _stack_heads(do_ref[...])
        cqs, lses = _stack_cols(cq_ref[...]), _stack_cols(lse_ref[...])
        row = _stack_rows(i, TA)
        dqa[...] = jnp.zeros_like(dqa)
        nk = _n_key_tiles(i, TK, TA)

        def probs(j, delta):
            ks = pl.ds(pl.multiple_of(j * TK, TK), TK)
            ckb = jnp.concatenate([jnp.broadcast_to(ck_ref[h:h + 1, ks], (TA, TK)) for h in range(NH)], axis=0)
            sc = _dg(qs, k_ref[ks, :], 1, 1) + cqs - ckb
            col = j * TK + _iota((1, TK), 1)
            p = jnp.where(col <= row, jnp.exp(sc - lses), 0.0)
            dp = _dg(dos, v_ref[ks, :], 1, 1)
            p_s[:, ks] = p
            dp_s[:, ks] = dp
            return delta + jnp.sum(p * dp, axis=-1, keepdims=True)

        delta = lax.fori_loop(0, nk, probs, jnp.zeros((SROWS, 1), F32))

        def step(j, carry):
            ks = pl.ds(pl.multiple_of(j * TK, TK), TK)
            p = p_s[:, ks]
            ds = p * (dp_s[:, ks] - delta)
            dsb = _cast(ds)
            dqa[...] += _dg(_lanes_cat(dsb), kst[pl.ds(pl.multiple_of(j * NH * TK, NH * TK), NH * TK), :], 1, 0) * SCALE
            dk_ref[ks, :] += _dg(dsb, qs, 0, 0)
            dv_ref[ks, :] += _dg(_cast(p), dos, 0, 0)
            for h in range(NH):
                dck_ref[h:h + 1, ks] -= jnp.sum(ds[h * TA:(h + 1) * TA], axis=0, keepdims=True)
            return carry

        lax.fori_loop(0, nk, step, 0)
        dq_ref[...] = dqa[...]

    tok = pl.BlockSpec((TA, GW), lambda b, i: (b * nq + i, 0))
    seq = pl.BlockSpec((s, GW), lambda b, i: (b, 0))
    t128 = pl.BlockSpec((TA, 128), lambda b, i: (b * nq + i, 0))
    return _call(
        body, f"fox_attn_bwd_{tag}", (bl, nq),
        [tok, seq, seq, t128, pl.BlockSpec((None, 8, s), lambda b, i: (b, 0, 0)), t128, tok],
        [tok, seq, seq, pl.BlockSpec((None, 128, s), lambda b, i: (b, 0, 0))],
        [_sds((bl * s, GW)), _sds((bl * s, GW)), _sds((bl * s, GW)), _sds((bl, 128, s))],
        [pltpu.VMEM((TA, GW), F32), pltpu.VMEM((SROWS, s), F32), pltpu.VMEM((SROWS, s), F32), pltpu.VMEM((NH * s, GW), _MMT)],
    )(qn, kn, vb, cq, ck, lse, do)


def _sb_block(qh, kb, valid, upper, r_carry):
    z = _dg(qh, kb, 1, 1)
    ls = _logsig(z)
    lom = ls - z if valid is None else jnp.where(valid, ls - z, 0.0)
    between = xr(lom, upper, upper) + r_carry
    w = jnp.exp(ls + between)
    return ls, lom, (w if valid is None else jnp.where(valid, w, 0.0))


def _fill_stacked(dst, src_ref, s, tk):
    for j in range(s // tk):
        dst[j * NH * tk:(j + 1) * NH * tk, :] = _mask_stack(src_ref[j * tk:(j + 1) * tk, :])


def sb_attn_fwd(pb, bl, s, tag):
    TA, TK = TA_BIG, TK_SB
    nq, SROWS = s // TA, NH * TA

    def body(q_ref, k_ref, v_ref, o_ref, acc, vst):
        i = pl.program_id(1)

        @pl.when(i == 0)
        def _():
            _fill_stacked(vst, v_ref, s, TK)

        qs = _stack_heads(q_ref[...], SCALE)
        upper = _tri(TK, "lt")
        last = _n_key_tiles(i, TK, TA) - 1

        def step(j, r, valid):
            ks = pl.ds(pl.multiple_of(j * TK, TK), TK)
            _, lom, w = _sb_block(qs, _cast(k_ref[ks, :]), valid, upper, r)
            acc[...] += _dg(_lanes_cat(_cast(w)), vst[pl.ds(pl.multiple_of(j * NH * TK, NH * TK), NH * TK), :], 1, 0)
            return r + jnp.sum(lom, axis=-1, keepdims=True)

        acc[...] = jnp.zeros_like(acc)
        r = step(last, jnp.zeros((SROWS, 1), F32), last * TK + _iota((1, TK), 1) < _stack_rows(i, TA))
        lax.fori_loop(0, last, lambda jj, r: step(last - 1 - jj, r, None), r)
        o_ref[...] = acc[...]

    tok = lambda j: pl.BlockSpec((TA, GW), lambda b, i: (b * nq + i, j))
    seq = lambda j: pl.BlockSpec((s, GW), lambda b, i: (b, j))
    return _call(
        body, f"sb_attn_fwd_{tag}", (bl, nq), [tok(0), seq(1), seq(2)],
        pl.BlockSpec((TA, GW), lambda b, i: (b * nq + i, 0)), _sds((bl * s, GW)),
        [pltpu.VMEM((TA, GW), F32), pltpu.VMEM((NH * s, GW), _MMT)],
    )(pb, pb, pb)


def sb_attn_bwd(pb, do, bl, s, tag):
    TA, TK = TA_BIG, TK_SB
    nq, SROWS = s // TA, NH * TA

    def body(q_ref, k_ref, v_ref, do_ref, dq_ref, dk_ref, dv_ref, dqa, sig_s, nsig_s, w_s, g_s, kst):
        i = pl.program_id(1)

        @pl.when(i == 0)
        def _():
            dk_ref[...] = jnp.zeros_like(dk_ref)
            dv_ref[...] = jnp.zeros_like(dv_ref)
            _fill_stacked(kst, k_ref, s, TK)

        qs = _stack_heads(q_ref[...], SCALE)
        dos = _stack_heads(do_ref[...])
        upper = _tri(TK, "lt")
        before = _tri(TK, "gt")
        dqa[...] = jnp.zeros_like(dqa)
        last = _n_key_tiles(i, TK, TA) - 1
        diag = last * TK + _iota((1, TK), 1) < _stack_rows(i, TA)

        def weights(j, r, valid):
            ks = pl.ds(pl.multiple_of(j * TK, TK), TK)
            ls, lom, w = _sb_block(qs, _cast(k_ref[ks, :]), valid, upper, r)
            sig_s[:, ks] = _cast(jnp.exp(ls))
            nsig_s[:, ks] = _cast(jnp.exp(lom))
            w_s[:, ks] = _cast(w)
            g_s[:, ks] = _dg(dos, _cast(v_ref[ks, :]), 1, 1) * w
            return r + jnp.sum(lom, axis=-1, keepdims=True)

        r = weights(last, jnp.zeros((SROWS, 1), F32), diag)
        lax.fori_loop(0, last, lambda jj, r: weights(last - 1 - jj, r, None), r)

        def step(j, cpre, valid):
            ks = pl.ds(pl.multiple_of(j * TK, TK), TK)
            g = g_s[:, ks]
            pre = cpre + xr(g, before, before)
            dz = g * nsig_s[:, ks].astype(F32) - sig_s[:, ks].astype(F32) * pre
            dzb = _cast(dz if valid is None else jnp.where(valid, dz, 0.0))
            dqa[...] += _dg(_lanes_cat(dzb), kst[pl.ds(pl.multiple_of(j * NH * TK, NH * TK), NH * TK), :], 1, 0) * SCALE
            dk_ref[ks, :] += _dg(dzb, qs, 0, 0)
            dv_ref[ks, :] += _dg(w_s[:, ks], dos, 0, 0)
            return cpre + jnp.sum(g, axis=-1, keepdims=True)

        cpre = lax.fori_loop(0, last, lambda j, c: step(j, c, None), jnp.zeros((SROWS, 1), F32))
        step(last, cpre, diag)
        dq_ref[...] = dqa[...]

    tok = lambda j: pl.BlockSpec((TA, GW), lambda b, i: (b * nq + i, j))
    seq = lambda j: pl.BlockSpec((s, GW), lambda b, i: (b, j))
    return _call(
        body, f"sb_attn_bwd_{tag}", (bl, nq), [tok(0), seq(1), seq(2), tok(0)],
        [tok(0), seq(0), seq(0)], [_sds((bl * s, GW))] * 3,
        [pltpu.VMEM((TA, GW), F32), pltpu.VMEM((SROWS, s), _MMT), pltpu.VMEM((SROWS, s), _MMT),
         pltpu.VMEM((SROWS, s), _MMT), pltpu.VMEM((SROWS, s), F32), pltpu.VMEM((NH * s, GW), _MMT)],
    )(pb, pb, pb, do)


def _hgrn_consts():
    r, c = _iota((CH, CH), 0), _iota((CH, CH), 1)
    rr = _iota((CH, 1), 0)
    tri = (c <= r).astype(F32)
    lv = []
    for m in (8, 4, 2, 1):
        up = ((rr & (2 * m - 1)) >= m).astype(F32)
        selq = (((r & (2 * m - 1)) >= m) & (c == (r & ~(m - 1)) - 1)).astype(F32)
        selk = (((r & (2 * m - 1)) < m) & (c == (r & ~(m - 1)) + m - 1)).astype(F32)
        pm = (((r & ~(2 * m - 1)) == (c & ~(2 * m - 1))) & ((r & (2 * m - 1)) >= m) & ((c & (2 * m - 1)) < m)).astype(F32)
        lv.append((up, 1.0 - up, selq, selq.T, selk, selk.T, jnp.concatenate([pm] * NH, axis=0)))
    hm4 = lambda n: (((_iota((NH, 1, n), 2) & (GW - 1)) >> 6) == _iota((NH, 1, n), 0)).astype(F32)
    return dict(tri=tri, trit=tri.T, rr=rr, lv=lv, bd=_bdmask(), bd64=_bdmask() * (1.0 / HD),
                hm4={GW: hm4(GW), 3 * GW: hm4(3 * GW)})


def _hgrn_chunk_fn(hq, hf, hi, lb, wn, st, cs):
    q = _silu(hq)
    log_lb = jnp.log(jnp.maximum(lb, LB_FLOOR))
    a, bb = log_lb, jnp.log1p(-lb) + _logsig(hf)
    g = jnp.maximum(a, bb) + jnp.log1p(jnp.exp(-jnp.abs(a - bb)))
    k = (1.0 - lb) * _sigmoid(-hf)
    v = hi
    rr = cs["rr"]
    b = xl(cs["tri"], cs["trit"], g)
    row_of = lambda n: jnp.sum(b * (rr == n).astype(F32), axis=0, keepdims=True)
    o = mm_nt(q * jnp.exp(b), st)
    qs, ks = [], []
    for ib in (1, 2, 3):
        ref = row_of(16 * ib - 1)
        inq = ((rr >= 16 * ib) & (rr < 16 * ib + 16)).astype(F32)
        ink = (rr < 16 * ib).astype(F32)
        qs.append(q * jnp.exp((b - ref) * inq) * inq)
        ks.append(k * jnp.exp((ref - b) * ink) * ink)
    qcat, kcat = jnp.concatenate(qs, axis=1), jnp.concatenate(ks, axis=1)
    lvl = []
    for up, lo, selq, selqt, selk, selkt, pm in cs["lv"]:
        qe = q * jnp.exp((b - xl(selq, selqt, b)) * up) * up
        ke = k * jnp.exp((xl(selk, selkt, b) - b) * lo) * lo
        lvl.append((qe, ke, pm))
    stack = lambda x: (x[None] * cs["hm4"][x.shape[1]]).reshape(NH * CH, x.shape[1])
    a_all = mm_nt(stack(qcat), kcat)
    for qe, ke, pm4 in lvl:
        a_all = a_all + mm_nt(stack(qe), ke) * pm4
    o = o + jnp.sum(mm(a_all, v).reshape(NH, CH, GW) * cs["hm4"][GW], axis=0)
    o = o + xr(q * k, cs["bd"], cs["bd"]) * v
    b_last = row_of(CH - 1)
    st_new = st * jnp.exp(b_last) + mm_tn(v, k * jnp.exp(b_last - b)) * cs["bd"]
    return _headrms(o, wn, cs["bd64"]), st_new


def hgrn_fwd(pc, lb, wn, bl, s, layer, tag):
    nc = s // CH

    def body(q_ref, f_ref, i_ref, lb_ref, wn_ref, o_ref, st_ref, st):
        @pl.when(pl.program_id(0) == 0)
        def _():
            st[...] = jnp.zeros_like(st)

        cs = _hgrn_consts()
        for b in range(bl):
            st_ref[b] = st[b]
            o, st_new = _hgrn_chunk_fn(q_ref[b], f_ref[b], i_ref[b], lb_ref[...], wn_ref[...], st[b], cs)
            o_ref[b] = o
            st[b] = st_new

    tok = lambda j: pl.BlockSpec((bl, CH, GW), lambda c: (0, c, j))
    par = pl.BlockSpec((None, 1, GW), lambda c: (layer, 0, 0))
    pc3 = pc.reshape(bl, s, 3 * GW)
    o, states = _call(
        body, f"hgrn_fwd_{tag}", (nc,), [tok(0), tok(1), tok(2), par, par],
        [tok(0), pl.BlockSpec((bl, None, GW, GW), lambda c: (0, c, 0, 0))],
        [_sds((bl, s, GW)), _sds((bl, nc, GW, GW))],
        [pltpu.VMEM((bl, GW, GW), F32)],
    )(pc3, pc3, pc3, lb, wn)
    return o.reshape(bl * s, GW), states


def hgrn_bwd(pc, lb, wn, states, do, bl, s, layer, tag):
    nc = s // CH

    def body(q_ref, f_ref, i_ref, lb_ref, wn_ref, st_ref, do_ref, dc_ref, dlb_ref, dwn_ref, dst):
        c = pl.program_id(0)

        @pl.when(c == 0)
        def _():
            dst[...] = jnp.zeros_like(dst)

        cs = _hgrn_consts()
        dlb_sum = dwn_sum = None
        for b in range(bl):
            _, vjp = jax.vjp(lambda *a: _hgrn_chunk_fn(*a, cs), q_ref[b], f_ref[b], i_ref[b], lb_ref[...],
                             wn_ref[...], st_ref[b])
            dq, df, di, dlb, dwn, dst_in = vjp((do_ref[b], dst[b]))
            dst[b] = dst_in
            dc_ref[b, :, 0:GW] = dq
            dc_ref[b, :, GW:2 * GW] = df
            dc_ref[b, :, 2 * GW:3 * GW] = di
            dlb_sum = dlb if dlb_sum is None else dlb_sum + dlb
            dwn_sum = dwn if dwn_sum is None else dwn_sum + dwn
        _acc(dlb_ref, dlb_sum, c == 0)
        _acc(dwn_ref, dwn_sum, c == 0)

    tok = lambda j: pl.BlockSpec((bl, CH, GW), lambda c: (0, nc - 1 - c, j))
    par = pl.BlockSpec((None, 1, GW), lambda c: (layer, 0, 0))
    acc = pl.BlockSpec((1, GW), lambda c: (0, 0))
    pc3 = pc.reshape(bl, s, 3 * GW)
    dc, dlb, dwn = _call(
        body, f"hgrn_bwd_{tag}", (nc,),
        [tok(0), tok(1), tok(2), par, par, pl.BlockSpec((bl, None, GW, GW), lambda c: (0, nc - 1 - c, 0, 0)), tok(0)],
        [pl.BlockSpec((bl, CH, 3 * GW), lambda c: (0, nc - 1 - c, 0)), acc, acc],
        [_sds((bl, s, 3 * GW)), _sds((1, GW)), _sds((1, GW))],
        [pltpu.VMEM((bl, GW, GW), F32)],
    )(pc3, pc3, pc3, lb, wn, states, do.reshape(bl, s, GW))
    return dc.reshape(bl * s, 3 * GW), dlb, dwn


def _shift_rows(x, k, up):
    n = x.shape[0]
    rr = _iota((n, 1), 0)
    if up:
        return jnp.where(rr < n - k, pltpu.roll(x, n - k, 0), 0.0)
    return jnp.where(rr >= k, pltpu.roll(x, k, 0), 0.0)


def _window_sums(x, up):
    s2 = x + _shift_rows(x, 1, up)
    s4 = s2 + _shift_rows(s2, 2, up)
    s8 = s4 + _shift_rows(s4, 4, up)
    s16 = s8 + _shift_rows(s8, 8, up)
    return s2, s4, s8, s16


def _pool_div(n):
    pos = (_iota((n, 1), 0) + 1).astype(F32)
    return [jnp.minimum(pos, float(w)) for w in (2, 4, 8, 16)]


def _pool_mix(sums, scaled):
    out = None
    for gi, sw in enumerate(sums):
        part = (sw if scaled is None else sw / scaled[gi]) * _hmask(gi)
        out = part if out is None else out + part
    return out


def pool_fwd(pd, wbd, scale, bl, s, layer, tag):
    def body(u_ref, w_ref, sc_ref, o_ref):
        u = u_ref[...]
        pm = _pool_mix(_window_sums(u, False), _pool_div(s)) - u
        o_ref[...] = _dg(_cast(pm), _cast(w_ref[...]), 1, 0) * sc_ref[...]

    seq = pl.BlockSpec((s, GW), lambda b: (b, 0))
    return _call(
        body, f"pool_fwd_{tag}", (bl,),
        [seq, pl.BlockSpec((None, GW, GW), lambda b: (layer, 0, 0)), pl.BlockSpec((None, 1, GW), lambda b: (layer, 0, 0))],
        seq, _sds((bl * s, GW)),
    )(pd, wbd, scale)


def pool_bwd(pd, wbd, scale, do, bl, s, layer, tag):
    def body(u_ref, w_ref, sc_ref, do_ref, du_ref, dw_ref, dsc_ref):
        first = pl.program_id(0) == 0
        u, do = u_ref[...], do_ref[...]
        div = _pool_div(s)
        pm = _pool_mix(_window_sums(u, False), div) - u
        ypre = _dg(_cast(pm), _cast(w_ref[...]), 1, 0)
        dys = do * sc_ref[...]
        _acc(dsc_ref, jnp.sum(do * ypre, axis=0, keepdims=True), first)
        _acc(dw_ref, _dg(_cast(pm), _cast(dys), 0, 0), first)
        dpm = _dg(_cast(dys), _cast(w_ref[...]), 1, 1)
        dsc = [dpm / d for d in div]
        adj = None
        for gi in range(4):
            part = _window_sums(dsc[gi] * _hmask(gi), True)[gi]
            adj = part if adj is None else adj + part
        du_ref[...] = adj - dpm

    seq = pl.BlockSpec((s, GW), lambda b: (b, 0))
    return _call(
        body, f"pool_bwd_{tag}", (bl,),
        [seq, pl.BlockSpec((None, GW, GW), lambda b: (layer, 0, 0)), pl.BlockSpec((None, 1, GW), lambda b: (layer, 0, 0)), seq],
        [seq, pl.BlockSpec((GW, GW), lambda b: (0, 0)), pl.BlockSpec((1, GW), lambda b: (0, 0))],
        [_sds((bl * s, GW)), _sds((GW, GW)), _sds((1, GW))],
    )(pd, wbd, scale, do)


def _mem_prep_fn(mem, g, wk, wv, kw, bd64):
    mn = _rms(mem, g)
    return _headrms(mm(mn, wk), kw, bd64), mm(mn, wv)


def mem_prep_fwd(mem2, g, wkv, kw, bl, layer, tag):
    def body(m_ref, g_ref, wk_ref, wv_ref, kw_ref, k_ref, v_ref):
        k, v = _mem_prep_fn(m_ref[...], g_ref[...], wk_ref[...], wv_ref[...], kw_ref[...], _bdmask() * (1.0 / HD))
        k_ref[...] = k
        v_ref[...] = v

    blk = pl.BlockSpec((N_MEM, GW), lambda b: (b, 0))
    return _call(
        body, f"mem_prep_fwd_{tag}", (bl,),
        [pl.BlockSpec((N_MEM, D_MODEL), lambda b: (b, 0)), pl.BlockSpec((None, 1, D_MODEL), lambda b: (layer, 0, 0)),
         pl.BlockSpec((None, D_MODEL, GW), lambda b: (layer, 0, 0)), pl.BlockSpec((None, D_MODEL, GW), lambda b: (layer, 0, 1)),
         pl.BlockSpec((None, 1, GW), lambda b: (layer, 0, 0))],
        [blk, blk], [_sds((bl * N_MEM, GW))] * 2,
    )(mem2, g, wkv, wkv, kw)


def mem_prep_bwd(mem2, g, wkv, kw, dk, dv, bl, layer, tag):
    def body(m_ref, g_ref, wk_ref, wv_ref, kw_ref, dk_ref, dv_ref, dwk_ref, dwv_ref, dg_ref, dkw_ref):
        first = pl.program_id(0) == 0
        bd64 = _bdmask() * (1.0 / HD)
        _, vjp = jax.vjp(lambda g_, wk, wv, kw_: _mem_prep_fn(m_ref[...], g_, wk, wv, kw_, bd64),
                         g_ref[...], wk_ref[...].astype(F32), wv_ref[...].astype(F32), kw_ref[...])
        dg, dwk, dwv, dkw = vjp((dk_ref[...], dv_ref[...]))
        _acc(dwk_ref, dwk, first)
        _acc(dwv_ref, dwv, first)
        _acc(dg_ref, dg, first)
        _acc(dkw_ref, dkw, first)

    blk = pl.BlockSpec((N_MEM, GW), lambda b: (b, 0))
    return _call(
        body, f"mem_prep_bwd_{tag}", (bl,),
        [pl.BlockSpec((N_MEM, D_MODEL), lambda b: (b, 0)), pl.BlockSpec((None, 1, D_MODEL), lambda b: (layer, 0, 0)),
         pl.BlockSpec((None, D_MODEL, GW), lambda b: (layer, 0, 0)), pl.BlockSpec((None, D_MODEL, GW), lambda b: (layer, 0, 1)),
         pl.BlockSpec((None, 1, GW), lambda b: (layer, 0, 0)), blk, blk],
        [pl.BlockSpec((D_MODEL, GW), lambda b: (0, 0)), pl.BlockSpec((D_MODEL, GW), lambda b: (0, 0)),
         pl.BlockSpec((1, D_MODEL), lambda b: (0, 0)), pl.BlockSpec((1, GW), lambda b: (0, 0))],
        [_sds((D_MODEL, GW)), _sds((D_MODEL, GW)), _sds((1, D_MODEL)), _sds((1, GW))],
    )(mem2, g, wkv, wkv, kw, dk, dv)


def _mem_attn_fn(mq, qw, k, v, bd64):
    qn = _headrms(mq, qw, bd64)
    out = None
    for h in range(NH):
        hm = _hmask(h)
        lg = mm_nt(qn * hm, k) * SCALE
        e = jnp.exp(lg - lax.stop_gradient(jnp.max(lg, axis=-1, keepdims=True)))
        p = e / jnp.sum(e, axis=-1, keepdims=True)
        part = mm(p, v) * hm
        out = part if out is None else out + part
    return out


def mem_attn_fwd(pe, qw, k, v, bl, s, layer, tag):
    nq = s // TQ

    def body(q_ref, qw_ref, k_ref, v_ref, o_ref):
        o_ref[...] = _mem_attn_fn(q_ref[...], qw_ref[...], k_ref[...], v_ref[...], _bdmask() * (1.0 / HD))

    tok = pl.BlockSpec((TQ, GW), lambda b, i: (b * nq + i, 0))
    kv = pl.BlockSpec((N_MEM, GW), lambda b, i: (b, 0))
    return _call(
        body, f"mem_attn_fwd_{tag}", (bl, nq), [tok, pl.BlockSpec((None, 1, GW), lambda b, i: (layer, 0, 0)), kv, kv],
        tok, _sds((bl * s, GW)),
    )(pe, qw, k, v)


def mem_attn_bwd(pe, qw, k, v, do, bl, s, layer, tag):
    nq = s // TQ

    def body(q_ref, qw_ref, k_ref, v_ref, do_ref, dq_ref, dk_ref, dv_ref, dqw_ref):
        i = pl.program_id(1)
        bd64 = _bdmask() * (1.0 / HD)
        _, vjp = jax.vjp(lambda *a: _mem_attn_fn(*a, bd64), q_ref[...], qw_ref[...], k_ref[...], v_ref[...])
        dq, dqw, dk, dv = vjp(do_ref[...])
        dq_ref[...] = dq
        _acc(dk_ref, dk, i == 0)
        _acc(dv_ref, dv, i == 0)
        _acc(dqw_ref, dqw, jnp.logical_and(pl.program_id(0) == 0, i == 0))

    tok = pl.BlockSpec((TQ, GW), lambda b, i: (b * nq + i, 0))
    kv = pl.BlockSpec((N_MEM, GW), lambda b, i: (b, 0))
    return _call(
        body, f"mem_attn_bwd_{tag}", (bl, nq),
        [tok, pl.BlockSpec((None, 1, GW), lambda b, i: (layer, 0, 0)), kv, kv, tok],
        [tok, kv, kv, pl.BlockSpec((1, GW), lambda b, i: (0, 0))],
        [_sds((bl * s, GW)), _sds((bl * N_MEM, GW)), _sds((bl * N_MEM, GW)), _sds((1, GW))],
    )(pe, qw, k, v, do)


def _gate_out_fn(outs, gates, wparts):
    y = None
    for o, g, w in zip(outs, gates, wparts):
        part = mm(o * _silu(g), w)
        y = part if y is None else y + part
    return y


def outproj_fwd(x2, outs, pg, wout, layer, tag):
    t = x2.shape[0]

    def body(x_ref, oa, ob, oc, od, oe, g_ref, w_ref, y_ref):
        outs_ = [r[...] for r in (oa, ob, oc, od, oe)]
        gates = [g_ref[:, j * GW:(j + 1) * GW] for j in range(5)]
        wparts = [w_ref[j * GW:(j + 1) * GW, :] for j in range(5)]
        y_ref[...] = x_ref[...] + _gate_out_fn(outs_, gates, wparts)

    tok = pl.BlockSpec((TQ, GW), lambda i: (i, 0))
    big = pl.BlockSpec((TQ, D_MODEL), lambda i: (i, 0))
    return _call(
        body, f"outproj_fwd_{tag}", (t // TQ,),
        [big] + [tok] * 5 + [pl.BlockSpec((TQ, D_MIX), lambda i: (i, 0)),
                            pl.BlockSpec((None, D_MIX, D_MODEL), lambda i: (layer, 0, 0))],
        big, _sds((t, D_MODEL)),
    )(x2, *outs, pg, wout)


def outproj_bwd(outs, pg, wout, dy, layer, tag):
    t = dy.shape[0]

    def body(oa, ob, oc, od, oe, g_ref, w_ref, dy_ref, da, db, dc, dd, de, dg_ref, dw_ref):
        outs_ = [r[...] for r in (oa, ob, oc, od, oe)]
        gates = [g_ref[:, j * GW:(j + 1) * GW] for j in range(5)]
        wparts = [w_ref[j * GW:(j + 1) * GW, :].astype(F32) for j in range(5)]
        _, vjp = jax.vjp(_gate_out_fn, outs_, gates, wparts)
        douts, dgates, dws = vjp(dy_ref[...])
        for r, val in zip((da, db, dc, dd, de), douts):
            r[...] = val
        first = pl.program_id(0) == 0
        for j in range(5):
            dg_ref[:, j * GW:(j + 1) * GW] = dgates[j]

        @pl.when(first)
        def _():
            for j in range(5):
                dw_ref[j * GW:(j + 1) * GW, :] = dws[j]

        @pl.when(jnp.logical_not(first))
        def _():
            for j in range(5):
                dw_ref[j * GW:(j + 1) * GW, :] += dws[j]

    tok = pl.BlockSpec((TQ, GW), lambda i: (i, 0))
    return _call(
        body, f"outproj_bwd_{tag}", (t // TQ,),
        [tok] * 5 + [pl.BlockSpec((TQ, D_MIX), lambda i: (i, 0)), pl.BlockSpec((None, D_MIX, D_MODEL), lambda i: (layer, 0, 0)),
                     pl.BlockSpec((TQ, D_MODEL), lambda i: (i, 0))],
        [tok] * 5 + [pl.BlockSpec((TQ, D_MIX), lambda i: (i, 0)), pl.BlockSpec((D_MIX, D_MODEL), lambda i: (0, 0))],
        [_sds((t, GW))] * 5 + [_sds((t, D_MIX)), _sds((D_MIX, D_MODEL))],
    )(*outs, pg, wout, dy)


def loss_head(y, tgt):
    t = y.shape[0]

    def body(y_ref, t_ref, l_ref, dy_ref):
        diff = y_ref[...] - t_ref[...]
        dy_ref[...] = diff * (1.0 / D_MODEL)
        part = 0.5 * jnp.sum(jnp.sum(diff * diff, axis=-1, keepdims=True) * (1.0 / D_MODEL), axis=0, keepdims=True)
        _acc(l_ref, jnp.broadcast_to(part, (8, 128)), pl.program_id(0) == 0)

    big = pl.BlockSpec((TQ, D_MODEL), lambda i: (i, 0))
    return _call(body, "loss_head", (t // TQ,), [big, big], [pl.BlockSpec((8, 128), lambda i: (0, 0)), big],
                 [_sds((8, 128)), _sds((t, D_MODEL))])(y, tgt)


def layer_fwd(x2, mem2, p, layer, bl, s):
    tag = f"l{layer}"
    ht, pa, pb, pc, pd, pe, pg, pf = inproj_fwd(x2, p["norm_g"], p["w_in"], layer, tag)
    qn, kn, vb, cq, ck = fox_prep_fwd(pa, pf, p["fox_q_norm"], p["fox_k_norm"], p["fox_f_bias"], bl, s, layer, tag)
    oa, lse = fox_attn_fwd(qn, kn, vb, cq, ck, bl, s, tag)
    ob = sb_attn_fwd(pb, bl, s, tag)
    oc, states = hgrn_fwd(pc, p["lb"], p["hgrn_out_norm"], bl, s, layer, tag)
    od = pool_fwd(pd, p["pool_wbd"], p["pool_scale"], bl, s, layer, tag)
    mk, mv = mem_prep_fwd(mem2, p["mem_norm_g"], p["mem_w_kv"], p["mem_k_norm"], bl, layer, tag)
    oe = mem_attn_fwd(pe, p["mem_q_norm"], mk, mv, bl, s, layer, tag)
    y = outproj_fwd(x2, (oa, ob, oc, od, oe), pg, p["w_out"], layer, tag)
    saved = dict(x2=x2, ht=ht, pa=pa, pb=pb, pc=pc, pd=pd, pe=pe, pg=pg, pf=pf, qn=qn, kn=kn, vb=vb, cq=cq, ck=ck,
                 oa=oa, lse=lse, ob=ob, oc=oc, states=states, od=od, mk=mk, mv=mv, oe=oe)
    return y, saved


def layer_bwd(dy, mem2, p, sv, layer, bl, s):
    tag = f"l{layer}"
    (doa, dob, doc, dod, doe, dg_gates, dwout) = outproj_bwd((sv["oa"], sv["ob"], sv["oc"], sv["od"], sv["oe"]), sv["pg"],
                                                              p["w_out"], dy, layer, tag)
    dqn, dkn, dv, dck = fox_attn_bwd(sv["qn"], sv["kn"], sv["vb"], sv["cq"], sv["ck"], sv["lse"], doa, bl, s, tag)
    d_a, d_f, dqw, dkw, dbias = fox_prep_bwd(sv["pa"], sv["pf"], p["fox_q_norm"], p["fox_k_norm"], p["fox_f_bias"], sv["cq"],
                                             dqn, dkn, dv, dck, bl, s, layer, tag)
    dsq, dsk, dsv = sb_attn_bwd(sv["pb"], dob, bl, s, tag)
    d_c, dlb, dwn = hgrn_bwd(sv["pc"], p["lb"], p["hgrn_out_norm"], sv["states"], doc, bl, s, layer, tag)
    d_d, dwbd, dpscale = pool_bwd(sv["pd"], p["pool_wbd"], p["pool_scale"], dod, bl, s, layer, tag)
    d_e, dmk, dmv, dmqw = mem_attn_bwd(sv["pe"], p["mem_q_norm"], sv["mk"], sv["mv"], doe, bl, s, layer, tag)
    dwk, dwv, dmg, dmkw = mem_prep_bwd(mem2, p["mem_norm_g"], p["mem_w_kv"], p["mem_k_norm"], dmk, dmv, bl, layer, tag)
    dpieces = (d_a, dsq, dsk, dsv, d_c, d_d, d_e, dg_gates, d_f)
    dx, dng = inproj_bwd_dx(sv["x2"], p["norm_g"], p["w_in"], dy, dpieces, layer, tag)
    dwin = jnp.concatenate([matmul_acc(sv["ht"], dp, f"{tag}_{nm}") for (nm, _, _), dp in zip(BWD_PIECES, dpieces)], axis=1)
    grads = dict(norm_g=dng, w_in=dwin, fox_f_bias=dbias, fox_q_norm=dqw, fox_k_norm=dkw, lb=dlb, hgrn_out_norm=dwn,
                 pool_wbd=dwbd, pool_scale=dpscale, mem_norm_g=dmg, mem_w_kv=jnp.concatenate([dwk, dwv], axis=1),
                 mem_q_norm=dmqw, mem_k_norm=dmkw, w_out=dwout)
    return dx, grads


def _tile4(w):
    return jnp.tile(w, (1, NH))[:, None, :]


def prepare_params(norm_g, w_in_p, fox_f_bias, fox_q_norm, fox_k_norm, hgrn_lb_logits, hgrn_out_norm, pool_w, pool_scale,
                   mem_norm_g, mem_w_kv, mem_q_norm, mem_k_norm, w_out):
    p1 = jax.nn.sigmoid(hgrn_lb_logits[1] - hgrn_lb_logits[0])
    lb = jnp.stack([jnp.zeros_like(p1), jnp.clip(p1, 0.0, 1.0 - 1e-6)])
    eye = jnp.eye(4, dtype=F32)
    wbd = jnp.einsum("lgcd,gh->lgchd", pool_w, eye).reshape(2, GW, GW)
    return dict(norm_g=norm_g[:, None, :], w_in=w_in_p, fox_f_bias=jnp.pad(fox_f_bias, ((0, 0), (0, 124)))[:, None, :],
                fox_q_norm=_tile4(fox_q_norm), fox_k_norm=_tile4(fox_k_norm), lb=lb[:, None, :],
                hgrn_out_norm=hgrn_out_norm[:, None, :], pool_wbd=wbd, pool_scale=pool_scale[:, None, :],
                mem_norm_g=mem_norm_g[:, None, :], mem_w_kv=mem_w_kv, mem_q_norm=_tile4(mem_q_norm),
                mem_k_norm=_tile4(mem_k_norm), w_out=w_out)


def local_step(x, mem, tgt, p):
    bl, s, _ = x.shape
    x2, mem2, tgt2 = x.reshape(bl * s, D_MODEL), mem.reshape(bl * N_MEM, D_MODEL), tgt.reshape(bl * s, D_MODEL)
    y0, sv0 = layer_fwd(x2, mem2, p, 0, bl, s)
    y1, sv1 = layer_fwd(y0, mem2, p, 1, bl, s)
    lpart, dy = loss_head(y1, tgt2)
    dx1, g1 = layer_bwd(dy, mem2, p, sv1, 1, bl, s)
    dx0, g0 = layer_bwd(dx1, mem2, p, sv0, 0, bl, s)
    return lpart[0, 0], dx0.reshape(bl, s, D_MODEL), g0, g1


_ANY = pl.BlockSpec(memory_space=pl.ANY)


def _me_and_peers():
    x, y, c = lax.axis_index("x"), lax.axis_index("y"), lax.axis_index("c")
    peers = []
    for k in range(1, N_DEV):
        px = 1 - x if (k >> 2) & 1 else x
        py = 1 - y if (k >> 1) & 1 else y
        pc = 1 - c if k & 1 else c
        peers.append(((px, py, pc), 4 * px + 2 * py + pc))
    return 4 * x + 2 * y + c, peers


def all_gather_rows(xs, tag):
    nl, r, c = xs.shape

    def body(x_ref, o_ref, send_sems, recv_sems, local_sem):
        x, y, cc = lax.axis_index("x"), lax.axis_index("y"), lax.axis_index("c")
        me, sibling = (x, y, cc), (x, y, 1 - cc)
        chips = [(1 - x, y), (x, 1 - y), (1 - x, 1 - y)]

        def rows(px, py, pc):
            return o_ref.at[:, pl.ds((4 * px + 2 * py + pc) * r, r), :]

        def copy(k, block, to, src=None):
            return pltpu.make_async_remote_copy(src_ref=rows(*block) if src is None else src, dst_ref=rows(*block),
                                                send_sem=send_sems.at[k], recv_sem=recv_sems.at[k], device_id=to,
                                                device_id_type=pl.DeviceIdType.MESH)

        mine = pltpu.make_async_copy(x_ref, rows(*me), local_sem)
        mine.start()
        first = [copy(0, me, sibling, src=x_ref)] + [copy(1 + j, me, (*chip, cc), src=x_ref) for j, chip in enumerate(chips)]
        for cp in first:
            cp.start()
        passed = [copy(4 + j, (*chip, cc), sibling) for j, chip in enumerate(chips)]
        for j, chip in enumerate(chips):
            copy(1 + j, (*chip, cc), me).wait_recv()
            passed[j].start()
        copy(0, sibling, me).wait_recv()
        for j, chip in enumerate(chips):
            copy(4 + j, (*chip, 1 - cc), me).wait_recv()
        for cp in first + passed:
            cp.wait_send()
        mine.wait()

    return pl.pallas_call(
        body, name=f"all_gather_{tag}", in_specs=[_ANY], out_specs=_ANY, out_shape=_sds((nl, N_DEV * r, c), xs.dtype),
        scratch_shapes=[pltpu.SemaphoreType.DMA((N_DEV - 1,)), pltpu.SemaphoreType.DMA((N_DEV - 1,)), pltpu.SemaphoreType.DMA],
    )(xs)


def exchange_cores(part, tag):
    nl, _, _, r, c = part.shape

    def body(p_ref, theirs_ref, send_sems, recv_sems):
        x, y, cc = lax.axis_index("x"), lax.axis_index("y"), lax.axis_index("c")
        copies = []
        for l in range(nl):
            for q in range(4):
                k = l * 4 + q
                copies.append(pltpu.make_async_remote_copy(
                    src_ref=p_ref.at[l, q, pl.ds(1 - cc, 1)], dst_ref=theirs_ref.at[l, q], send_sem=send_sems.at[k],
                    recv_sem=recv_sems.at[k], device_id=(x, y, 1 - cc), device_id_type=pl.DeviceIdType.MESH))
        for cp in copies:
            cp.start()
        for cp in copies:
            cp.wait()

    nsem = pltpu.SemaphoreType.DMA((nl * 4,))
    return pl.pallas_call(
        body, name=f"exchange_cores_{tag}", in_specs=[_ANY], out_specs=_ANY, out_shape=_sds((nl, 4, 1, r, c), part.dtype),
        scratch_shapes=[nsem, nsem],
    )(part)


def add_core_halves(part5, theirs, core, tag):
    nl, _, _, r, c = part5.shape
    tr = 64 if r % 64 == 0 else 32

    def body(core_ref, a_ref, b_ref, o_ref):
        o_ref[...] = _cast(a_ref[...] + b_ref[...])

    blk = lambda which: pl.BlockSpec((None, None, None, tr, c), lambda l, q, i, cref: (l, q, cref[0] if which else 0, i, 0))
    return pl.pallas_call(
        body, name=f"add_core_halves_{tag}", out_shape=_sds((nl, 4, 1, r, c), _MMT),
        grid_spec=pltpu.PrefetchScalarGridSpec(num_scalar_prefetch=1, grid=(nl, 4, r // tr), in_specs=[blk(True), blk(False)],
                                               out_specs=blk(False)),
        compiler_params=pltpu.CompilerParams(dimension_semantics=("arbitrary",) * 3, vmem_limit_bytes=VMEM_LIMIT_BYTES),
    )(core, part5, theirs)


def exchange_chips(s4, tag):
    nl, _, _, r, c = s4.shape

    def body(s_ref, o_ref, send_sems, recv_sems, local_sem):
        x, y, cc = lax.axis_index("x"), lax.axis_index("y"), lax.axis_index("c")
        local = pltpu.make_async_copy(s_ref.at[:, pl.ds(2 * x + y, 1)], o_ref.at[0], local_sem)
        local.start()
        copies = []
        for k in range(1, 4):
            px = 1 - x if (k >> 1) & 1 else x
            py = 1 - y if k & 1 else y
            copies.append(pltpu.make_async_remote_copy(
                src_ref=s_ref.at[:, pl.ds(2 * px + py, 1)], dst_ref=o_ref.at[k], send_sem=send_sems.at[k - 1],
                recv_sem=recv_sems.at[k - 1], device_id=(px, py, cc), device_id_type=pl.DeviceIdType.MESH))
        for cp in copies:
            cp.start()
        for cp in copies:
            cp.wait()
        local.wait()

    return pl.pallas_call(
        body, name=f"exchange_chips_{tag}", in_specs=[_ANY], out_specs=_ANY, out_shape=_sds((4, nl, 1, 1, r, c), s4.dtype),
        scratch_shapes=[pltpu.SemaphoreType.DMA((3,)), pltpu.SemaphoreType.DMA((3,)), pltpu.SemaphoreType.DMA],
    )(s4)


def exchange_row_blocks(part, tag):
    nl, r8, c = part.shape
    r = r8 // N_DEV

    def body(p_ref, o_ref, send_sems, recv_sems, local_sem):
        me, peers = _me_and_peers()
        rows = lambda idx: p_ref.at[:, pl.ds(idx * r, r), :]
        mine = pltpu.make_async_copy(rows(me), o_ref.at[0], local_sem)
        mine.start()
        copies = [pltpu.make_async_remote_copy(src_ref=rows(idx), dst_ref=o_ref.at[k + 1], send_sem=send_sems.at[k],
                                               recv_sem=recv_sems.at[k], device_id=dev, device_id_type=pl.DeviceIdType.MESH)
                  for k, (dev, idx) in enumerate(peers)]
        for cp in copies:
            cp.start()
        for cp in copies:
            cp.wait()
        mine.wait()

    return pl.pallas_call(
        body, name=f"exchange_{tag}", in_specs=[_ANY], out_specs=_ANY, out_shape=_sds((N_DEV, nl, r, c), part.dtype),
        scratch_shapes=[pltpu.SemaphoreType.DMA((N_DEV - 1,)), pltpu.SemaphoreType.DMA((N_DEV - 1,)), pltpu.SemaphoreType.DMA],
    )(part)


def _row_tile(rows):
    if rows <= 512 and rows % 64:
        return rows
    for t in (64, 40, 32, 16, 8):
        if rows % t == 0:
            return t
    return rows


def sum_slots(slots, tag):
    ns, rows, c = slots.shape
    tr = _row_tile(rows)

    def body(s_ref, o_ref):
        acc = s_ref[0].astype(F32)
        for k in range(1, ns):
            acc = acc + s_ref[k].astype(F32)
        o_ref[...] = acc

    return _call(body, f"sum_slots_{tag}", (rows // tr,), [pl.BlockSpec((ns, tr, c), lambda i: (0, i, 0))],
                 pl.BlockSpec((tr, c), lambda i: (i, 0)), _sds((rows, c)))(slots)


def _adamw(w, g, m, v):
    m = ADAM_B1 * m + (1.0 - ADAM_B1) * g
    v = ADAM_B2 * v + (1.0 - ADAM_B2) * (g * g)
    m_hat = m / (1.0 - ADAM_B1 ** ADAM_STEP)
    v_hat = v / (1.0 - ADAM_B2 ** ADAM_STEP)
    delta = -ADAM_LR * (m_hat / (jnp.sqrt(v_hat) + ADAM_EPS) + ADAM_WD * w)
    return delta, m, v


def adam_update(w, m, v, g, tag, slots=False):
    rows, c = w.shape
    tr = _row_tile(rows)
    ns = g.shape[0] if slots else 0

    def body(w_ref, m_ref, v_ref, g_ref, go_ref, d_ref, mo_ref, vo_ref):
        if slots:
            g = g_ref[0].astype(F32)
            for k in range(1, ns):
                g = g + g_ref[k].astype(F32)
        else:
            g = g_ref[...]
        d, mn, vn = _adamw(w_ref[...], g, m_ref[...], v_ref[...])
        go_ref[...] = g
        d_ref[...] = d
        mo_ref[...] = mn
        vo_ref[...] = vn

    blk = pl.BlockSpec((tr, c), lambda i: (i, 0))
    gspec = pl.BlockSpec((ns, tr, c), lambda i: (0, i, 0)) if slots else blk
    return _call(body, f"adam_{tag}", (rows // tr,), [blk, blk, blk, gspec], [blk] * 4, [_sds((rows, c))] * 4)(w, m, v, g)


_SMALL = (("norm_g", (2, 1024)), ("fox_f_bias", (2, 4)), ("fox_q_norm", (2, 64)), ("fox_k_norm", (2, 64)),
          ("hgrn_lb_logits", (2, 256)), ("hgrn_out_norm", (2, 256)), ("pool_w", (2, 4, 64, 64)), ("pool_scale", (2, 256)),
          ("mem_norm_g", (2, 1024)), ("mem_q_norm", (2, 64)), ("mem_k_norm", (2, 64)))
_SLAB_ROWS = 312


def pack_small(d):
    flat = jnp.concatenate([d[n].reshape(-1) for n, _ in _SMALL])
    return jnp.pad(flat, (0, _SLAB_ROWS * 128 - flat.shape[0])).reshape(_SLAB_ROWS, 128)


def unpack_small(slab):
    flat, out, off = slab.reshape(-1), {}, 0
    for n, shp in _SMALL:
        size = 1
        for e in shp:
            size *= e
        out[n] = flat[off:off + size].reshape(shp)
        off += size
    return out


def small_grads(g0, g1, lb_logits):
    st = lambda f: jnp.stack([f(g0), f(g1)])
    heads = lambda a: a.reshape(NH, HD).sum(0)
    p1 = jax.nn.sigmoid(lb_logits[1] - lb_logits[0])
    inside = (p1 > 0.0) & (p1 < 1.0 - 1e-6)
    dl1 = jnp.where(inside, g1["lb"][0] * p1 * (1.0 - p1), 0.0)
    diag = lambda a: jnp.stack([a.reshape(4, HD, 4, HD)[i, :, i, :] for i in range(4)])
    return dict(norm_g=st(lambda g: g["norm_g"][0]), fox_f_bias=st(lambda g: g["fox_f_bias"][0, :NH]),
                fox_q_norm=st(lambda g: heads(g["fox_q_norm"])), fox_k_norm=st(lambda g: heads(g["fox_k_norm"])),
                hgrn_lb_logits=jnp.stack([-dl1, dl1]), hgrn_out_norm=st(lambda g: g["hgrn_out_norm"][0]),
                pool_w=st(lambda g: diag(g["pool_wbd"])), pool_scale=st(lambda g: g["pool_scale"][0]),
                mem_norm_g=st(lambda g: g["mem_norm_g"][0]), mem_q_norm=st(lambda g: heads(g["mem_q_norm"])),
                mem_k_norm=st(lambda g: heads(g["mem_k_norm"])))


def kernel(x, mem, norm_g, w_in, fox_f_bias, fox_q_norm, fox_k_norm, hgrn_lb_logits, hgrn_out_norm, pool_w, pool_scale, mem_norm_g, mem_w_kv, mem_q_norm, mem_k_norm, w_out, loss_target, m_norm_g, m_w_in, m_fox_f_bias, m_fox_q_norm, m_fox_k_norm, m_hgrn_lb_logits, m_hgrn_out_norm, m_pool_w, m_pool_scale, m_mem_norm_g, m_mem_w_kv, m_mem_q_norm, m_mem_k_norm, m_w_out, v_norm_g, v_w_in, v_fox_f_bias, v_fox_q_norm, v_fox_k_norm, v_hgrn_lb_logits, v_hgrn_out_norm, v_pool_w, v_pool_scale, v_mem_norm_g, v_mem_w_kv, v_mem_q_norm, v_mem_k_norm, v_w_out):
    given = dict(norm_g=(norm_g, m_norm_g, v_norm_g), w_in=(w_in, m_w_in, v_w_in), fox_f_bias=(fox_f_bias, m_fox_f_bias, v_fox_f_bias),
                 fox_q_norm=(fox_q_norm, m_fox_q_norm, v_fox_q_norm), fox_k_norm=(fox_k_norm, m_fox_k_norm, v_fox_k_norm),
                 hgrn_lb_logits=(hgrn_lb_logits, m_hgrn_lb_logits, v_hgrn_lb_logits),
                 hgrn_out_norm=(hgrn_out_norm, m_hgrn_out_norm, v_hgrn_out_norm), pool_w=(pool_w, m_pool_w, v_pool_w),
                 pool_scale=(pool_scale, m_pool_scale, v_pool_scale), mem_norm_g=(mem_norm_g, m_mem_norm_g, v_mem_norm_g),
                 mem_w_kv=(mem_w_kv, m_mem_w_kv, v_mem_w_kv), mem_q_norm=(mem_q_norm, m_mem_q_norm, v_mem_q_norm),
                 mem_k_norm=(mem_k_norm, m_mem_k_norm, v_mem_k_norm), w_out=(w_out, m_w_out, v_w_out))
    order = ("norm_g", "w_in", "fox_f_bias", "fox_q_norm", "fox_k_norm", "hgrn_lb_logits", "hgrn_out_norm", "pool_w",
             "pool_scale", "mem_norm_g", "mem_w_kv", "mem_q_norm", "mem_k_norm", "w_out")

    w_in_full = all_gather_rows(_cast(permute_cols(w_in)), "w_in")
    w_out_full = all_gather_rows(_cast(w_out), "w_out")
    w_kv_full = all_gather_rows(_cast(mem_w_kv), "w_kv")
    p = prepare_params(norm_g, w_in_full, fox_f_bias, fox_q_norm, fox_k_norm, hgrn_lb_logits, hgrn_out_norm, pool_w,
                       pool_scale, mem_norm_g, w_kv_full, mem_q_norm, mem_k_norm, w_out_full)

    loss_part, grad_x, g0, g1 = local_step(x, mem, loss_target, p)
    loss = lax.psum(loss_part, ("x", "y", "c"))

    res = {}
    core = lax.axis_index("c").astype(jnp.int32).reshape(1)

    def sharded(name, g2, unperm=False, two_stage=True):
        w, m, v = given[name]
        nl, r, c = w.shape
        cp = g2.shape[-1]
        if two_stage:
            part5 = g2.reshape(nl, 4, 2, r, cp)
            s4 = add_core_halves(part5, exchange_cores(part5, name), core, name)
            slots = exchange_chips(s4, name).reshape(4, nl * r, cp)
        else:
            slots = exchange_row_blocks(g2, name).reshape(N_DEV, nl * r, cp)
        if unperm:
            g = sum_slots(slots, name).reshape(nl, r, cp)
            out = adam_update(w.reshape(nl * r, c), m.reshape(nl * r, c), v.reshape(nl * r, c),
                              unpermute_cols(g).reshape(nl * r, c), name)
        else:
            out = adam_update(w.reshape(nl * r, c), m.reshape(nl * r, c), v.reshape(nl * r, c), slots, name, slots=True)
        res[name] = tuple(o.reshape(nl, r, c) for o in out)

    sharded("w_in", jnp.stack([g0["w_in"], g1["w_in"]]), unperm=True)
    sharded("w_out", jnp.stack([g0["w_out"], g1["w_out"]]))
    sharded("mem_w_kv", _cast(jnp.stack([g0["mem_w_kv"], g1["mem_w_kv"]])), two_stage=False)

    gsmall = pack_small(small_grads(g0, g1, hgrn_lb_logits))
    gathered = all_gather_rows(gsmall[None], "small").reshape(N_DEV, _SLAB_ROWS, 128)
    slabs = adam_update(*[pack_small({n: given[n][j] for n, _ in _SMALL}) for j in range(3)], gathered, "small", slots=True)
    small = [unpack_small(sl) for sl in slabs]
    for n, _ in _SMALL:
        res[n] = tuple(small[j][n] for j in range(4))

    return (loss, grad_x, *[res[n][0] for n in order], *[res[n][1] for n in order], *[res[n][2] for n in order],
            *[res[n][3] for n in order])
```

```python
import functools

import jax
import jax.numpy as jnp
from jax import lax
from jax.experimental import pallas as pl
from jax.experimental.pallas import tpu as pltpu

F32 = jnp.float32
BF = jnp.bfloat16
_MMT = BF

D_MODEL = 1024
GW = 256
HD = 64
NH = 4
CH = 64
N_MEM = 256
D_IN = 4100
D_INP = 4224
D_MIX = 1280
EPS = 1e-6
NEG_BIG = -1e30
LB_FLOOR = 1e-30
SCALE = HD ** -0.5
TQ = 256
TM = 512


def _token_tile(n):
    return TM if n % TM == 0 else TQ
N_DEV = 8
VMEM_LIMIT_BYTES = 56 * 1024 * 1024

ADAM_LR = 0.001
ADAM_B1 = 0.9
ADAM_B2 = 0.999
ADAM_EPS = 1e-08
ADAM_WD = 0.01
ADAM_STEP = 10

PIECES = (("A", 0, 768), ("B", 768, 768), ("C", 1536, 768), ("D", 2304, 256), ("E", 2560, 256),
          ("G", 2816, 1280), ("F", 4096, 128))
BWD_PIECES = (("A", 0, 768), ("Bq", 768, 256), ("Bk", 1024, 256), ("Bv", 1280, 256), ("C", 1536, 768),
              ("D", 2304, 256), ("E", 2560, 256), ("G", 2816, 1280), ("F", 4096, 128))
_ORIG = dict(fq=(0, 256), fk=(256, 512), fv=(512, 768), fg=(768, 1024), ff=(1024, 1028), sq=(1028, 1284),
             sk=(1284, 1540), sv=(1540, 1796), sg=(1796, 2052), hq=(2052, 2308), hf=(2308, 2564),
             hi=(2564, 2820), hg=(2820, 3076), pv=(3076, 3332), pg=(3332, 3588), mq=(3588, 3844), mg=(3844, 4100))
_PERM_ORDER = ("fq", "fk", "fv", "sq", "sk", "sv", "hq", "hf", "hi", "pv", "mq", "fg", "sg", "hg", "pg", "mg", "ff")
_ORIG_ORDER = ("fq", "fk", "fv", "fg", "ff", "sq", "sk", "sv", "sg", "hq", "hf", "hi", "hg", "pv", "pg", "mq", "mg")


def permute_cols(w):
    parts = [w[..., _ORIG[n][0]:_ORIG[n][1]] for n in _PERM_ORDER]
    parts.append(jnp.zeros(w.shape[:-1] + (D_INP - D_IN,), w.dtype))
    return jnp.concatenate(parts, axis=-1)


def unpermute_cols(g):
    start, off = {}, 0
    for n in _PERM_ORDER:
        start[n] = off
        off += _ORIG[n][1] - _ORIG[n][0]
    return jnp.concatenate([g[..., start[n]:start[n] + _ORIG[n][1] - _ORIG[n][0]] for n in _ORIG_ORDER], axis=-1)


def _cast(a):
    return a.astype(_MMT)


def _dg(a, b, ca, cb):
    return lax.dot_general(a, b, (((ca,), (cb,)), ((), ())), preferred_element_type=F32)


@jax.custom_vjp
def mm(a, b):
    return _dg(_cast(a), _cast(b), 1, 0)


@jax.custom_vjp
def mm_nt(a, b):
    return _dg(_cast(a), _cast(b), 1, 1)


@jax.custom_vjp
def mm_tn(a, b):
    return _dg(_cast(a), _cast(b), 0, 0)


mm.defvjp(lambda a, b: (mm(a, b), (a, b)),
          lambda r, g: (mm_nt(g, r[1]).astype(r[0].dtype), mm_tn(r[0], g).astype(r[1].dtype)))
mm_nt.defvjp(lambda a, b: (mm_nt(a, b), (a, b)),
             lambda r, g: (mm(g, r[1]).astype(r[0].dtype), mm_tn(g, r[0]).astype(r[1].dtype)))
mm_tn.defvjp(lambda a, b: (mm_tn(a, b), (a, b)),
             lambda r, g: (mm_nt(r[1], g).astype(r[0].dtype), mm(r[0], g).astype(r[1].dtype)))


def _split(a):
    hi = a.astype(_MMT)
    lo = (a - hi.astype(F32)).astype(_MMT)
    return hi, lo


@jax.custom_vjp
def xr(a, c, ct):
    hi, lo = _split(a)
    cc = _cast(c)
    return _dg(hi, cc, 1, 0) + _dg(lo, cc, 1, 0)


@jax.custom_vjp
def xl(c, ct, a):
    hi, lo = _split(a)
    cc = _cast(c)
    return _dg(cc, hi, 1, 0) + _dg(cc, lo, 1, 0)


xr.defvjp(lambda a, c, ct: (xr(a, c, ct), (c, ct)),
          lambda r, g: (xr(g, r[1], r[0]), jnp.zeros_like(r[0]), jnp.zeros_like(r[1])))
xl.defvjp(lambda c, ct, a: (xl(c, ct, a), (c, ct)),
          lambda r, g: (jnp.zeros_like(r[0]), jnp.zeros_like(r[1]), xl(r[1], r[0], g)))


def _iota(shape, dim):
    return lax.broadcasted_iota(jnp.int32, shape, dim)


def _hmask(h, n=GW):
    lane = _iota((1, n), 1)
    return ((lane >= h * HD) & (lane < (h + 1) * HD)).astype(F32)


def _bdmask(n=GW):
    return ((_iota((n, n), 0) >> 6) == (_iota((n, n), 1) >> 6)).astype(F32)


def _tri(n, kind="le"):
    r, c = _iota((n, n), 0), _iota((n, n), 1)
    return {"le": c <= r, "ge": c >= r, "gt": c > r, "lt": c < r}[kind].astype(F32)


def _onehot_lane(h, n=128):
    return (_iota((1, n), 1) == h).astype(F32)


def _logsig(x):
    return jnp.minimum(x, 0.0) - jnp.log1p(jnp.exp(-jnp.abs(x)))


def _sigmoid(x):
    return 0.5 * (jnp.tanh(0.5 * x) + 1.0)


def _silu(x):
    return x * _sigmoid(x)


def _rms(x, g):
    return x * lax.rsqrt(jnp.mean(x * x, axis=-1, keepdims=True) + EPS) * g


def _headrms(x, w, bd64):
    ms = xr(x * x, bd64, bd64)
    return x * lax.rsqrt(ms + EPS) * w


def _call(body, name, grid, in_specs, out_specs, out_shape, scratch=()):
    return pl.pallas_call(
        body, name=name, grid=grid, in_specs=in_specs, out_specs=out_specs, out_shape=out_shape,
        scratch_shapes=list(scratch),
        compiler_params=pltpu.CompilerParams(dimension_semantics=("arbitrary",) * len(grid),
                                             vmem_limit_bytes=VMEM_LIMIT_BYTES))


def _sds(shape, dtype=F32):
    return jax.ShapeDtypeStruct(shape, dtype)


def _acc(ref, val, first):
    @pl.when(first)
    def _():
        ref[...] = val

    @pl.when(jnp.logical_not(first))
    def _():
        ref[...] += val


def inproj_fwd(x2, g, w, layer, tag):
    t = x2.shape[0]
    TQ = _token_tile(t)

    def body(x_ref, g_ref, w_ref, ht_ref, *outs):
        h = _rms(x_ref[...], g_ref[...])
        hb = _cast(h)
        ht_ref[...] = _cast(h.T)
        for (_, c0, wd), o in zip(PIECES, outs):
            o[...] = _dg(hb, _cast(w_ref[:, c0:c0 + wd]), 1, 0)

    return _call(
        body, f"inproj_fwd_{tag}", (t // TQ,),
        [pl.BlockSpec((TQ, D_MODEL), lambda i: (i, 0)),
         pl.BlockSpec((None, 1, D_MODEL), lambda i: (layer, 0, 0)),
         pl.BlockSpec((None, D_MODEL, D_INP), lambda i: (layer, 0, 0))],
        [pl.BlockSpec((D_MODEL, TQ), lambda i: (0, i))] + [pl.BlockSpec((TQ, wd), lambda i: (i, 0)) for _, _, wd in PIECES],
        [_sds((D_MODEL, t), _MMT)] + [_sds((t, wd)) for _, _, wd in PIECES],
    )(x2, g, w)


def inproj_bwd_dx(x2, g, w, dy, dpieces, layer, tag):
    t = x2.shape[0]

    def body(x_ref, g_ref, w_ref, dy_ref, *rest):
        dps, (dx_ref, dg_ref) = rest[:len(BWD_PIECES)], rest[len(BWD_PIECES):]
        dh = None
        for (_, c0, wd), dp in zip(BWD_PIECES, dps):
            part = _dg(_cast(dp[...]), _cast(w_ref[:, c0:c0 + wd]), 1, 1)
            dh = part if dh is None else dh + part
        _, vjp = jax.vjp(_rms, x_ref[...], g_ref[...])
        dx, dg = vjp(dh)
        dx_ref[...] = dy_ref[...] + dx
        _acc(dg_ref, dg, pl.program_id(0) == 0)

    return _call(
        body, f"inproj_bwd_dx_{tag}", (t // TQ,),
        [pl.BlockSpec((TQ, D_MODEL), lambda i: (i, 0)),
         pl.BlockSpec((None, 1, D_MODEL), lambda i: (layer, 0, 0)),
         pl.BlockSpec((None, D_MODEL, D_INP), lambda i: (layer, 0, 0)),
         pl.BlockSpec((TQ, D_MODEL), lambda i: (i, 0))] + [pl.BlockSpec((TQ, wd), lambda i: (i, 0)) for _, _, wd in BWD_PIECES],
        [pl.BlockSpec((TQ, D_MODEL), lambda i: (i, 0)), pl.BlockSpec((1, D_MODEL), lambda i: (0, 0))],
        [_sds((t, D_MODEL)), _sds((1, D_MODEL))],
    )(x2, g, w, dy, *dpieces)


def matmul_acc(at, b, tag):
    m, t = at.shape
    n = b.shape[1]
    tn = {1280: 640, 768: 768}.get(n, n)
    tk = 2048 if t % 2048 == 0 else (512 if t % 512 == 0 else TQ)

    def body(a_ref, b_ref, o_ref):
        _acc(o_ref, _dg(_cast(a_ref[...]), _cast(b_ref[...]), 1, 0), pl.program_id(1) == 0)

    return _call(
        body, f"matmul_acc_{tag}", (n // tn, t // tk),
        [pl.BlockSpec((m, tk), lambda j, i: (0, i)), pl.BlockSpec((tk, tn), lambda j, i: (i, j))],
        pl.BlockSpec((m, tn), lambda j, i: (0, j)),
        _sds((m, n)),
    )(at, b)


def _fox_prep_fn(q, k, ff, qw, kw, bias, carry, bd64, tri, trit, last):
    qn = _headrms(q, qw, bd64)
    kn = _headrms(k, kw, bd64)
    lf = _logsig(ff + bias)
    c = xl(tri, trit, lf) + carry
    return qn, kn, c, jnp.sum(c * last, axis=0, keepdims=True)


def _prep_consts(tq):
    return _bdmask() * (1.0 / HD), _tri(tq), _tri(tq, "ge"), (_iota((tq, 1), 0) == tq - 1).astype(F32)


def fox_prep_fwd(pa, pf, qw, kw, bias, bl, s, layer, tag):
    TQ = _token_tile(s)
    nq = s // TQ

    def body(q_ref, k_ref, v_ref, f_ref, qw_ref, kw_ref, b_ref, qn_ref, kn_ref, vb_ref, cq_ref, ck_ref, carry):
        @pl.when(pl.program_id(1) == 0)
        def _():
            carry[...] = jnp.zeros_like(carry)

        qn, kn, c, cl = _fox_prep_fn(q_ref[...], k_ref[...], f_ref[...], qw_ref[...], kw_ref[...], b_ref[...],
                                     carry[...], *_prep_consts(TQ))
        carry[...] = cl
        qn_ref[...] = _cast(qn)
        kn_ref[...] = _cast(kn)
        vb_ref[...] = _cast(v_ref[...])
        cq_ref[...] = c
        ck_ref[...] = c.T[0:8, :]

    tok = lambda j: pl.BlockSpec((TQ, GW), lambda b, i: (b * nq + i, j))
    par = lambda n: pl.BlockSpec((None, 1, n), lambda b, i: (layer, 0, 0))
    return _call(
        body, f"fox_prep_fwd_{tag}", (bl, nq),
        [tok(0), tok(1), tok(2), pl.BlockSpec((TQ, 128), lambda b, i: (b * nq + i, 0)), par(GW), par(GW), par(128)],
        [tok(0), tok(0), tok(0), pl.BlockSpec((TQ, 128), lambda b, i: (b * nq + i, 0)),
         pl.BlockSpec((None, 8, TQ), lambda b, i: (b, 0, i))],
        [_sds((bl * s, GW), _MMT)] * 3 + [_sds((bl * s, 128)), _sds((bl, 8, s))],
        [pltpu.VMEM((1, 128), F32)],
    )(pa, pa, pa, pf, qw, kw, bias)


def fox_prep_bwd(pa, pf, qw, kw, bias, cq, dqn, dkn, dv, dck, bl, s, layer, tag):
    TQ = _token_tile(s)
    nq = s // TQ

    def body(q_ref, k_ref, f_ref, qw_ref, kw_ref, b_ref, cq_ref, cprev_ref, dqn_ref, dkn_ref, dv_ref, dck_ref,
             da_ref, df_ref, dqw_ref, dkw_ref, db_ref, dcarry):
        i = pl.program_id(1)
        first = jnp.logical_and(pl.program_id(0) == 0, i == 0)

        @pl.when(i == 0)
        def _():
            dcarry[...] = jnp.zeros_like(dcarry)

        last = (_iota((TQ, 1), 0) == TQ - 1).astype(F32)
        carry_in = jnp.where(i == nq - 1, 0.0, jnp.sum(cprev_ref[...] * last, axis=0, keepdims=True))
        consts = _prep_consts(TQ)
        _, vjp = jax.vjp(lambda *a: _fox_prep_fn(*a, *consts), q_ref[...], k_ref[...], f_ref[...], qw_ref[...],
                         kw_ref[...], b_ref[...], carry_in)
        dc = dck_ref[...].T
        dq, dk, dff, dqw, dkw, dbias, dcin = vjp((dqn_ref[...], dkn_ref[...], dc, dcarry[...]))
        dcarry[...] = dcin
        da_ref[:, 0:GW] = dq
        da_ref[:, GW:2 * GW] = dk
        da_ref[:, 2 * GW:3 * GW] = dv_ref[...]
        df_ref[...] = dff
        _acc(dqw_ref, dqw, first)
        _acc(dkw_ref, dkw, first)
        _acc(db_ref, dbias, first)

    rv = lambda b, i: b * nq + (nq - 1 - i)
    tok = lambda j: pl.BlockSpec((TQ, GW), lambda b, i: (rv(b, i), j))
    tok0 = pl.BlockSpec((TQ, GW), lambda b, i: (rv(b, i), 0))
    t128 = pl.BlockSpec((TQ, 128), lambda b, i: (rv(b, i), 0))
    prev = pl.BlockSpec((TQ, 128), lambda b, i: (jnp.maximum(rv(b, i) - 1, 0), 0))
    par = lambda n: pl.BlockSpec((None, 1, n), lambda b, i: (layer, 0, 0))
    acc = lambda n: pl.BlockSpec((1, n), lambda b, i: (0, 0))
    return _call(
        body, f"fox_prep_bwd_{tag}", (bl, nq),
        [tok(0), tok(1), t128, par(GW), par(GW), par(128), t128, prev, tok0, tok0, tok0,
         pl.BlockSpec((None, 128, TQ), lambda b, i: (b, 0, nq - 1 - i))],
        [pl.BlockSpec((TQ, 3 * GW), lambda b, i: (rv(b, i), 0)), t128, acc(GW), acc(GW), acc(128)],
        [_sds((bl * s, 3 * GW)), _sds((bl * s, 128)), _sds((1, GW)), _sds((1, GW)), _sds((1, 128))],
        [pltpu.VMEM((1, 128), F32)],
    )(pa, pa, pf, qw, kw, bias, cq, cq, dqn, dkn, dv, dck)


def _lane_pick(x, h):
    return jnp.sum(x * _onehot_lane(h), axis=-1, keepdims=True)


TA_BIG = 256
TK_FOX = 512
TK_SB = 256


def _stack_heads(x, scale=1.0):
    return _cast(jnp.concatenate([x * (_hmask(h) * scale) for h in range(NH)], axis=0))


def _stack_cols(x):
    return jnp.concatenate([_lane_pick(x, h) for h in range(NH)], axis=0)


def _spread_heads(col):
    ta = col.shape[0] // NH
    return sum(col[h * ta:(h + 1) * ta] * _hmask(h) for h in range(NH))


def _lanes_cat(w):
    ta = w.shape[0] // NH
    return jnp.concatenate([w[h * ta:(h + 1) * ta] for h in range(NH)], axis=1)


def _mask_stack(x):
    return _cast(jnp.concatenate([x * _hmask(h).astype(x.dtype) for h in range(NH)], axis=0))


def _stack_rows(i, ta):
    return i * ta + (_iota((NH * ta, 1), 0) & (ta - 1))


def _n_key_tiles(i, tk, ta):
    return lax.shift_right_logical(i * ta, tk.bit_length() - 1) + 1


def fox_attn_fwd(qn, kn, vb, cq, ck, bl, s, tag):
    TA, TK = min(TA_BIG, s), min(TK_FOX, s)
    nq, SROWS = s // TA, NH * TA

    def body(q_ref, k_ref, v_ref, cq_ref, ck_ref, o_ref, lse_ref, acc, vst):
        i = pl.program_id(1)

        @pl.when(i == 0)
        def _():
            _fill_stacked(vst, v_ref, s, TK)

        qs = _stack_heads(q_ref[...].astype(F32), SCALE)
        cqs = _stack_cols(cq_ref[...])
        row = _stack_rows(i, TA)
        acc[...] = jnp.zeros_like(acc)

        def step(j, ml):
            m, l = ml
            ks = pl.ds(pl.multiple_of(j * TK, TK), TK)
            ckb = jnp.concatenate([jnp.broadcast_to(ck_ref[h:h + 1, ks], (TA, TK)) for h in range(NH)], axis=0)
            sc = _dg(qs, k_ref[ks, :], 1, 1) + cqs - ckb
            col = j * TK + _iota((1, TK), 1)
            sc = jnp.where(col <= row, sc, NEG_BIG)
            m_new = jnp.maximum(m, jnp.max(sc, axis=-1, keepdims=True))
            alpha = jnp.exp(m - m_new)
            p = jnp.exp(sc - m_new)
            vs = vst[pl.ds(pl.multiple_of(j * NH * TK, NH * TK), NH * TK), :]
            acc[...] = _spread_heads(alpha) * acc[...] + _dg(_lanes_cat(_cast(p)), vs, 1, 0)
            return m_new, alpha * l + jnp.sum(p, axis=-1, keepdims=True)

        m, l = lax.fori_loop(0, _n_key_tiles(i, TK, TA), step, (jnp.full((SROWS, 1), NEG_BIG, F32), jnp.zeros((SROWS, 1), F32)))
        o_ref[...] = acc[...] / _spread_heads(l)
        lse_h = m + jnp.log(l)
        lse_ref[...] = sum(lse_h[h * TA:(h + 1) * TA] * _onehot_lane(h) for h in range(NH))

    tok = pl.BlockSpec((TA, GW), lambda b, i: (b * nq + i, 0))
    seq = pl.BlockSpec((s, GW), lambda b, i: (b, 0))
    t128 = pl.BlockSpec((TA, 128), lambda b, i: (b * nq + i, 0))
    return _call(
        body, f"fox_attn_fwd_{tag}", (bl, nq),
        [tok, seq, seq, t128, pl.BlockSpec((None, 8, s), lambda b, i: (b, 0, 0))],
        [tok, t128], [_sds((bl * s, GW)), _sds((bl * s, 128))],
        [pltpu.VMEM((TA, GW), F32), pltpu.VMEM((NH * s, GW), _MMT)],
    )(qn, kn, vb, cq, ck)


def fox_attn_bwd(qn, kn, vb, cq, ck, lse, do, bl, s, tag):
    TA, TK = min(TA_BIG, s), min(TK_FOX, s)
    nq, SROWS = s // TA, NH * TA

    def body(q_ref, k_ref, v_ref, cq_ref, ck_ref, lse_ref, do_ref, dq_ref, dk_ref, dv_ref, dck_ref, dqa, p_s, dp_s, kst):
        i = pl.program_id(1)

        @pl.when(i == 0)
        def _():
            dk_ref[...] = jnp.zeros_like(dk_ref)
            dv_ref[...] = jnp.zeros_like(dv_ref)
            dck_ref[...] = jnp.zeros_like(dck_ref)
            _fill_stacked(kst, k_ref, s, TK)

        qs = _stack_heads(q_ref[...].astype(F32), SCALE)
        dos = _stack_heads(do_ref[...])
        cqs, lses = _stack_cols(cq_ref[...]), _stack_cols(lse_ref[...])
        row = _stack_rows(i, TA)
        dqa[...] = jnp.zeros_like(dqa)
        nk = _n_key_tiles(i, TK, TA)

        def probs(j, delta):
            ks = pl.ds(pl.multiple_of(j * TK, TK), TK)
            ckb = jnp.concatenate([jnp.broadcast_to(ck_ref[h:h + 1, ks], (TA, TK)) for h in range(NH)], axis=0)
            sc = _dg(qs, k_ref[ks, :], 1, 1) + cqs - ckb
            col = j * TK + _iota((1, TK), 1)
            p = jnp.where(col <= row, jnp.exp(sc - lses), 0.0)
            dp = _dg(dos, v_ref[ks, :], 1, 1)
            p_s[:, ks] = p
            dp_s[:, ks] = dp
            return delta + jnp.sum(p * dp, axis=-1, keepdims=True)

        delta = lax.fori_loop(0, nk, probs, jnp.zeros((SROWS, 1), F32))

        def step(j, carry):
            ks = pl.ds(pl.multiple_of(j * TK, TK), TK)
            p = p_s[:, ks]
            ds = p * (dp_s[:, ks] - delta)
            dsb = _cast(ds)
            dqa[...] += _dg(_lanes_cat(dsb), kst[pl.ds(pl.multiple_of(j * NH * TK, NH * TK), NH * TK), :], 1, 0) * SCALE
            dk_ref[ks, :] += _dg(dsb, qs, 0, 0)
            dv_ref[ks, :] += _dg(_cast(p), dos, 0, 0)
            for h in range(NH):
                dck_ref[h:h + 1, ks] -= jnp.sum(ds[h * TA:(h + 1) * TA], axis=0, keepdims=True)
            return carry

        lax.fori_loop(0, nk, step, 0)
        dq_ref[...] = dqa[...]

    tok = pl.BlockSpec((TA, GW), lambda b, i: (b * nq + i, 0))
    seq = pl.BlockSpec((s, GW), lambda b, i: (b, 0))
    t128 = pl.BlockSpec((TA, 128), lambda b, i: (b * nq + i, 0))
    return _call(
        body, f"fox_attn_bwd_{tag}", (bl, nq),
        [tok, seq, seq, t128, pl.BlockSpec((None, 8, s), lambda b, i: (b, 0, 0)), t128, tok],
        [tok, seq, seq, pl.BlockSpec((None, 128, s), lambda b, i: (b, 0, 0))],
        [_sds((bl * s, GW)), _sds((bl * s, GW)), _sds((bl * s, GW)), _sds((bl, 128, s))],
        [pltpu.VMEM((TA, GW), F32), pltpu.VMEM((SROWS, s), F32), pltpu.VMEM((SROWS, s), F32), pltpu.VMEM((NH * s, GW), _MMT)],
    )(qn, kn, vb, cq, ck, lse, do)


def _sb_block(qh, kb, valid, upper, r_carry):
    z = _dg(qh, kb, 1, 1)
    ls = _logsig(z)
    lom = ls - z if valid is None else jnp.where(valid, ls - z, 0.0)
    between = xr(lom, upper, upper) + r_carry
    w = jnp.exp(ls + between)
    return ls, lom, (w if valid is None else jnp.where(valid, w, 0.0))


def _fill_stacked(dst, src_ref, s, tk):
    for j in range(s // tk):
        dst[j * NH * tk:(j + 1) * NH * tk, :] = _mask_stack(src_ref[j * tk:(j + 1) * tk, :])


def sb_attn_fwd(pb, bl, s, tag):
    TA, TK = TA_BIG, TK_SB
    nq, SROWS = s // TA, NH * TA

    def body(q_ref, k_ref, v_ref, o_ref, acc, vst):
        i = pl.program_id(1)

        @pl.when(i == 0)
        def _():
            _fill_stacked(vst, v_ref, s, TK)

        qs = _stack_heads(q_ref[...], SCALE)
        upper = _tri(TK, "lt")
        last = _n_key_tiles(i, TK, TA) - 1

        def step(j, r, valid):
            ks = pl.ds(pl.multiple_of(j * TK, TK), TK)
            _, lom, w = _sb_block(qs, _cast(k_ref[ks, :]), valid, upper, r)
            acc[...] += _dg(_lanes_cat(_cast(w)), vst[pl.ds(pl.multiple_of(j * NH * TK, NH * TK), NH * TK), :], 1, 0)
            return r + jnp.sum(lom, axis=-1, keepdims=True)

        acc[...] = jnp.zeros_like(acc)
        r = step(last, jnp.zeros((SROWS, 1), F32), last * TK + _iota((1, TK), 1) < _stack_rows(i, TA))
        lax.fori_loop(0, last, lambda jj, r: step(last - 1 - jj, r, None), r)
        o_ref[...] = acc[...]

    tok = lambda j: pl.BlockSpec((TA, GW), lambda b, i: (b * nq + i, j))
    seq = lambda j: pl.BlockSpec((s, GW), lambda b, i: (b, j))
    return _call(
        body, f"sb_attn_fwd_{tag}", (bl, nq), [tok(0), seq(1), seq(2)],
        pl.BlockSpec((TA, GW), lambda b, i: (b * nq + i, 0)), _sds((bl * s, GW)),
        [pltpu.VMEM((TA, GW), F32), pltpu.VMEM((NH * s, GW), _MMT)],
    )(pb, pb, pb)


def sb_attn_bwd(pb, do, bl, s, tag):
    TA, TK = TA_BIG, TK_SB
    nq, SROWS = s // TA, NH * TA

    def body(q_ref, k_ref, v_ref, do_ref, dq_ref, dk_ref, dv_ref, dqa, sig_s, nsig_s, w_s, g_s, kst):
        i = pl.program_id(1)

        @pl.when(i == 0)
        def _():
            dk_ref[...] = jnp.zeros_like(dk_ref)
            dv_ref[...] = jnp.zeros_like(dv_ref)
            _fill_stacked(kst, k_ref, s, TK)

        qs = _stack_heads(q_ref[...], SCALE)
        dos = _stack_heads(do_ref[...])
        upper = _tri(TK, "lt")
        before = _tri(TK, "gt")
        dqa[...] = jnp.zeros_like(dqa)
        last = _n_key_tiles(i, TK, TA) - 1
        diag = last * TK + _iota((1, TK), 1) < _stack_rows(i, TA)

        def weights(j, r, valid):
            ks = pl.ds(pl.multiple_of(j * TK, TK), TK)
            ls, lom, w = _sb_block(qs, _cast(k_ref[ks, :]), valid, upper, r)
            sig_s[:, ks] = _cast(jnp.exp(ls))
            nsig_s[:, ks] = _cast(jnp.exp(lom))
            w_s[:, ks] = _cast(w)
            g_s[:, ks] = _dg(dos, _cast(v_ref[ks, :]), 1, 1) * w
            return r + jnp.sum(lom, axis=-1, keepdims=True)

        r = weights(last, jnp.zeros((SROWS, 1), F32), diag)
        lax.fori_loop(0, last, lambda jj, r: weights(last - 1 - jj, r, None), r)

        def step(j, cpre, valid):
            ks = pl.ds(pl.multiple_of(j * TK, TK), TK)
            g = g_s[:, ks]
            pre = cpre + xr(g, before, before)
            dz = g * nsig_s[:, ks].astype(F32) - sig_s[:, ks].astype(F32) * pre
            dzb = _cast(dz if valid is None else jnp.where(valid, dz, 0.0))
            dqa[...] += _dg(_lanes_cat(dzb), kst[pl.ds(pl.multiple_of(j * NH * TK, NH * TK), NH * TK), :], 1, 0) * SCALE
            dk_ref[ks, :] += _dg(dzb, qs, 0, 0)
            dv_ref[ks, :] += _dg(w_s[:, ks], dos, 0, 0)
            return cpre + jnp.sum(g, axis=-1, keepdims=True)

        cpre = lax.fori_loop(0, last, lambda j, c: step(j, c, None), jnp.zeros((SROWS, 1), F32))
        step(last, cpre, diag)
        dq_ref[...] = dqa[...]

    tok = lambda j: pl.BlockSpec((TA, GW), lambda b, i: (b * nq + i, j))
    seq = lambda j: pl.BlockSpec((s, GW), lambda b, i: (b, j))
    return _call(
        body, f"sb_attn_bwd_{tag}", (bl, nq), [tok(0), seq(1), seq(2), tok(0)],
        [tok(0), seq(0), seq(0)], [_sds((bl * s, GW))] * 3,
        [pltpu.VMEM((TA, GW), F32), pltpu.VMEM((SROWS, s), _MMT), pltpu.VMEM((SROWS, s), _MMT),
         pltpu.VMEM((SROWS, s), _MMT), pltpu.VMEM((SROWS, s), F32), pltpu.VMEM((NH * s, GW), _MMT)],
    )(pb, pb, pb, do)


def _hgrn_consts():
    r, c = _iota((CH, CH), 0), _iota((CH, CH), 1)
    rr = _iota((CH, 1), 0)
    tri = (c <= r).astype(F32)
    lv = []
    for m in (8, 4, 2, 1):
        up = ((rr & (2 * m - 1)) >= m).astype(F32)
        selq = (((r & (2 * m - 1)) >= m) & (c == (r & ~(m - 1)) - 1)).astype(F32)
        selk = (((r & (2 * m - 1)) < m) & (c == (r & ~(m - 1)) + m - 1)).astype(F32)
        pm = (((r & ~(2 * m - 1)) == (c & ~(2 * m - 1))) & ((r & (2 * m - 1)) >= m) & ((c & (2 * m - 1)) < m)).astype(F32)
        lv.append((up, 1.0 - up, selq, selq.T, selk, selk.T, jnp.concatenate([pm] * NH, axis=0)))
    hm4 = lambda n: (((_iota((NH, 1, n), 2) & (GW - 1)) >> 6) == _iota((NH, 1, n), 0)).astype(F32)
    return dict(tri=tri, trit=tri.T, rr=rr, lv=lv, bd=_bdmask(), bd64=_bdmask() * (1.0 / HD),
                hm4={GW: hm4(GW), 3 * GW: hm4(3 * GW)})


def _hgrn_chunk_fn(hq, hf, hi, lb, wn, st, cs):
    q = _silu(hq)
    log_lb = jnp.log(jnp.maximum(lb, LB_FLOOR))
    a, bb = log_lb, jnp.log1p(-lb) + _logsig(hf)
    g = jnp.maximum(a, bb) + jnp.log1p(jnp.exp(-jnp.abs(a - bb)))
    k = (1.0 - lb) * _sigmoid(-hf)
    v = hi
    rr = cs["rr"]
    b = xl(cs["tri"], cs["trit"], g)
    row_of = lambda n: jnp.sum(b * (rr == n).astype(F32), axis=0, keepdims=True)
    o = mm_nt(q * jnp.exp(b), st)
    qs, ks = [], []
    for ib in (1, 2, 3):
        ref = row_of(16 * ib - 1)
        inq = ((rr >= 16 * ib) & (rr < 16 * ib + 16)).astype(F32)
        ink = (rr < 16 * ib).astype(F32)
        qs.append(q * jnp.exp((b - ref) * inq) * inq)
        ks.append(k * jnp.exp((ref - b) * ink) * ink)
    qcat, kcat = jnp.concatenate(qs, axis=1), jnp.concatenate(ks, axis=1)
    lvl = []
    for up, lo, selq, selqt, selk, selkt, pm in cs["lv"]:
        qe = q * jnp.exp((b - xl(selq, selqt, b)) * up) * up
        ke = k * jnp.exp((xl(selk, selkt, b) - b) * lo) * lo
        lvl.append((qe, ke, pm))
    stack = lambda x: (x[None] * cs["hm4"][x.shape[1]]).reshape(NH * CH, x.shape[1])
    a_all = mm_nt(stack(qcat), kcat)
    for qe, ke, pm4 in lvl:
        a_all = a_all + mm_nt(stack(qe), ke) * pm4
    o = o + jnp.sum(mm(a_all, v).reshape(NH, CH, GW) * cs["hm4"][GW], axis=0)
    o = o + xr(q * k, cs["bd"], cs["bd"]) * v
    b_last = row_of(CH - 1)
    st_new = st * jnp.exp(b_last) + mm_tn(v, k * jnp.exp(b_last - b)) * cs["bd"]
    return _headrms(o, wn, cs["bd64"]), st_new


def hgrn_fwd(pc, lb, wn, bl, s, layer, tag):
    nc = s // CH

    def body(q_ref, f_ref, i_ref, lb_ref, wn_ref, o_ref, st_ref, st):
        @pl.when(pl.program_id(0) == 0)
        def _():
            st[...] = jnp.zeros_like(st)

        cs = _hgrn_consts()
        for b in range(bl):
            st_ref[b] = st[b]
            o, st_new = _hgrn_chunk_fn(q_ref[b], f_ref[b], i_ref[b], lb_ref[...], wn_ref[...], st[b], cs)
            o_ref[b] = o
            st[b] = st_new

    tok = lambda j: pl.BlockSpec((bl, CH, GW), lambda c: (0, c, j))
    par = pl.BlockSpec((None, 1, GW), lambda c: (layer, 0, 0))
    pc3 = pc.reshape(bl, s, 3 * GW)
    o, states = _call(
        body, f"hgrn_fwd_{tag}", (nc,), [tok(0), tok(1), tok(2), par, par],
        [tok(0), pl.BlockSpec((bl, None, GW, GW), lambda c: (0, c, 0, 0))],
        [_sds((bl, s, GW)), _sds((bl, nc, GW, GW))],
        [pltpu.VMEM((bl, GW, GW), F32)],
    )(pc3, pc3, pc3, lb, wn)
    return o.reshape(bl * s, GW), states


def hgrn_bwd(pc, lb, wn, states, do, bl, s, layer, tag):
    nc = s // CH

    def body(q_ref, f_ref, i_ref, lb_ref, wn_ref, st_ref, do_ref, dc_ref, dlb_ref, dwn_ref, dst):
        c = pl.program_id(0)

        @pl.when(c == 0)
        def _():
            dst[...] = jnp.zeros_like(dst)

        cs = _hgrn_consts()
        dlb_sum = dwn_sum = None
        for b in range(bl):
            _, vjp = jax.vjp(lambda *a: _hgrn_chunk_fn(*a, cs), q_ref[b], f_ref[b], i_ref[b], lb_ref[...],
                             wn_ref[...], st_ref[b])
            dq, df, di, dlb, dwn, dst_in = vjp((do_ref[b], dst[b]))
            dst[b] = dst_in
            dc_ref[b, :, 0:GW] = dq
            dc_ref[b, :, GW:2 * GW] = df
            dc_ref[b, :, 2 * GW:3 * GW] = di
            dlb_sum = dlb if dlb_sum is None else dlb_sum + dlb
            dwn_sum = dwn if dwn_sum is None else dwn_sum + dwn
        _acc(dlb_ref, dlb_sum, c == 0)
        _acc(dwn_ref, dwn_sum, c == 0)

    tok = lambda j: pl.BlockSpec((bl, CH, GW), lambda c: (0, nc - 1 - c, j))
    par = pl.BlockSpec((None, 1, GW), lambda c: (layer, 0, 0))
    acc = pl.BlockSpec((1, GW), lambda c: (0, 0))
    pc3 = pc.reshape(bl, s, 3 * GW)
    dc, dlb, dwn = _call(
        body, f"hgrn_bwd_{tag}", (nc,),
        [tok(0), tok(1), tok(2), par, par, pl.BlockSpec((bl, None, GW, GW), lambda c: (0, nc - 1 - c, 0, 0)), tok(0)],
        [pl.BlockSpec((bl, CH, 3 * GW), lambda c: (0, nc - 1 - c, 0)), acc, acc],
        [_sds((bl, s, 3 * GW)), _sds((1, GW)), _sds((1, GW))],
        [pltpu.VMEM((bl, GW, GW), F32)],
    )(pc3, pc3, pc3, lb, wn, states, do.reshape(bl, s, GW))
    return dc.reshape(bl * s, 3 * GW), dlb, dwn


def _shift_rows(x, k, up):
    n = x.shape[0]
    rr = _iota((n, 1), 0)
    if up:
        return jnp.where(rr < n - k, pltpu.roll(x, n - k, 0), 0.0)
    return jnp.where(rr >= k, pltpu.roll(x, k, 0), 0.0)


def _window_sums(x, up):
    s2 = x + _shift_rows(x, 1, up)
    s4 = s2 + _shift_rows(s2, 2, up)
    s8 = s4 + _shift_rows(s4, 4, up)
    s16 = s8 + _shift_rows(s8, 8, up)
    return s2, s4, s8, s16


def _pool_div(n):
    pos = (_iota((n, 1), 0) + 1).astype(F32)
    return [jnp.minimum(pos, float(w)) for w in (2, 4, 8, 16)]


def _pool_mix(sums, scaled):
    out = None
    for gi, sw in enumerate(sums):
        part = (sw if scaled is None else sw / scaled[gi]) * _hmask(gi)
        out = part if out is None else out + part
    return out


def pool_fwd(pd, wbd, scale, bl, s, layer, tag):
    def body(u_ref, w_ref, sc_ref, o_ref):
        u = u_ref[...]
        pm = _pool_mix(_window_sums(u, False), _pool_div(s)) - u
        o_ref[...] = _dg(_cast(pm), _cast(w_ref[...]), 1, 0) * sc_ref[...]

    seq = pl.BlockSpec((s, GW), lambda b: (b, 0))
    return _call(
        body, f"pool_fwd_{tag}", (bl,),
        [seq, pl.BlockSpec((None, GW, GW), lambda b: (layer, 0, 0)), pl.BlockSpec((None, 1, GW), lambda b: (layer, 0, 0))],
        seq, _sds((bl * s, GW)),
    )(pd, wbd, scale)


def pool_bwd(pd, wbd, scale, do, bl, s, layer, tag):
    def body(u_ref, w_ref, sc_ref, do_ref, du_ref, dw_ref, dsc_ref):
        first = pl.program_id(0) == 0
        u, do = u_ref[...], do_ref[...]
        div = _pool_div(s)
        pm = _pool_mix(_window_sums(u, False), div) - u
        ypre = _dg(_cast(pm), _cast(w_ref[...]), 1, 0)
        dys = do * sc_ref[...]
        _acc(dsc_ref, jnp.sum(do * ypre, axis=0, keepdims=True), first)
        _acc(dw_ref, _dg(_cast(pm), _cast(dys), 0, 0), first)
        dpm = _dg(_cast(dys), _cast(w_ref[...]), 1, 1)
        dsc = [dpm / d for d in div]
        adj = None
        for gi in range(4):
            part = _window_sums(dsc[gi] * _hmask(gi), True)[gi]
            adj = part if adj is None else adj + part
        du_ref[...] = adj - dpm

    seq = pl.BlockSpec((s, GW), lambda b: (b, 0))
    return _call(
        body, f"pool_bwd_{tag}", (bl,),
        [seq, pl.BlockSpec((None, GW, GW), lambda b: (layer, 0, 0)), pl.BlockSpec((None, 1, GW), lambda b: (layer, 0, 0)), seq],
        [seq, pl.BlockSpec((GW, GW), lambda b: (0, 0)), pl.BlockSpec((1, GW), lambda b: (0, 0))],
        [_sds((bl * s, GW)), _sds((GW, GW)), _sds((1, GW))],
    )(pd, wbd, scale, do)


def _mem_prep_fn(mem, g, wk, wv, kw, bd64):
    mn = _rms(mem, g)
    return _headrms(mm(mn, wk), kw, bd64), mm(mn, wv)


def mem_prep_fwd(mem2, g, wkv, kw, bl, layer, tag):
    def body(m_ref, g_ref, wk_ref, wv_ref, kw_ref, k_ref, v_ref):
        k, v = _mem_prep_fn(m_ref[...], g_ref[...], wk_ref[...], wv_ref[...], kw_ref[...], _bdmask() * (1.0 / HD))
        k_ref[...] = k
        v_ref[...] = v

    blk = pl.BlockSpec((N_MEM, GW), lambda b: (b, 0))
    return _call(
        body, f"mem_prep_fwd_{tag}", (bl,),
        [pl.BlockSpec((N_MEM, D_MODEL), lambda b: (b, 0)), pl.BlockSpec((None, 1, D_MODEL), lambda b: (layer, 0, 0)),
         pl.BlockSpec((None, D_MODEL, GW), lambda b: (layer, 0, 0)), pl.BlockSpec((None, D_MODEL, GW), lambda b: (layer, 0, 1)),
         pl.BlockSpec((None, 1, GW), lambda b: (layer, 0, 0))],
        [blk, blk], [_sds((bl * N_MEM, GW))] * 2,
    )(mem2, g, wkv, wkv, kw)


def mem_prep_bwd(mem2, g, wkv, kw, dk, dv, bl, layer, tag):
    def body(m_ref, g_ref, wk_ref, wv_ref, kw_ref, dk_ref, dv_ref, dwk_ref, dwv_ref, dg_ref, dkw_ref):
        first = pl.program_id(0) == 0
        bd64 = _bdmask() * (1.0 / HD)
        _, vjp = jax.vjp(lambda g_, wk, wv, kw_: _mem_prep_fn(m_ref[...], g_, wk, wv, kw_, bd64),
                         g_ref[...], wk_ref[...].astype(F32), wv_ref[...].astype(F32), kw_ref[...])
        dg, dwk, dwv, dkw = vjp((dk_ref[...], dv_ref[...]))
        _acc(dwk_ref, dwk, first)
        _acc(dwv_ref, dwv, first)
        _acc(dg_ref, dg, first)
        _acc(dkw_ref, dkw, first)

    blk = pl.BlockSpec((N_MEM, GW), lambda b: (b, 0))
    return _call(
        body, f"mem_prep_bwd_{tag}", (bl,),
        [pl.BlockSpec((N_MEM, D_MODEL), lambda b: (b, 0)), pl.BlockSpec((None, 1, D_MODEL), lambda b: (layer, 0, 0)),
         pl.BlockSpec((None, D_MODEL, GW), lambda b: (layer, 0, 0)), pl.BlockSpec((None, D_MODEL, GW), lambda b: (layer, 0, 1)),
         pl.BlockSpec((None, 1, GW), lambda b: (layer, 0, 0)), blk, blk],
        [pl.BlockSpec((D_MODEL, GW), lambda b: (0, 0)), pl.BlockSpec((D_MODEL, GW), lambda b: (0, 0)),
         pl.BlockSpec((1, D_MODEL), lambda b: (0, 0)), pl.BlockSpec((1, GW), lambda b: (0, 0))],
        [_sds((D_MODEL, GW)), _sds((D_MODEL, GW)), _sds((1, D_MODEL)), _sds((1, GW))],
    )(mem2, g, wkv, wkv, kw, dk, dv)


def _mem_attn_fn(mq, qw, k, v, bd64):
    qn = _headrms(mq, qw, bd64)
    out = None
    for h in range(NH):
        hm = _hmask(h)
        lg = mm_nt(qn * hm, k) * SCALE
        e = jnp.exp(lg - lax.stop_gradient(jnp.max(lg, axis=-1, keepdims=True)))
        p = e / jnp.sum(e, axis=-1, keepdims=True)
        part = mm(p, v) * hm
        out = part if out is None else out + part
    return out


def mem_attn_fwd(pe, qw, k, v, bl, s, layer, tag):
    TQ = _token_tile(s)
    nq = s // TQ

    def body(q_ref, qw_ref, k_ref, v_ref, o_ref):
        o_ref[...] = _mem_attn_fn(q_ref[...], qw_ref[...], k_ref[...], v_ref[...], _bdmask() * (1.0 / HD))

    tok = pl.BlockSpec((TQ, GW), lambda b, i: (b * nq + i, 0))
    kv = pl.BlockSpec((N_MEM, GW), lambda b, i: (b, 0))
    return _call(
        body, f"mem_attn_fwd_{tag}", (bl, nq), [tok, pl.BlockSpec((None, 1, GW), lambda b, i: (layer, 0, 0)), kv, kv],
        tok, _sds((bl * s, GW)),
    )(pe, qw, k, v)


def mem_attn_bwd(pe, qw, k, v, do, bl, s, layer, tag):
    TQ = _token_tile(s)
    nq = s // TQ

    def body(q_ref, qw_ref, k_ref, v_ref, do_ref, dq_ref, dk_ref, dv_ref, dqw_ref):
        i = pl.program_id(1)
        bd64 = _bdmask() * (1.0 / HD)
        _, vjp = jax.vjp(lambda *a: _mem_attn_fn(*a, bd64), q_ref[...], qw_ref[...], k_ref[...], v_ref[...])
        dq, dqw, dk, dv = vjp(do_ref[...])
        dq_ref[...] = dq
        _acc(dk_ref, dk, i == 0)
        _acc(dv_ref, dv, i == 0)
        _acc(dqw_ref, dqw, jnp.logical_and(pl.program_id(0) == 0, i == 0))

    tok = pl.BlockSpec((TQ, GW), lambda b, i: (b * nq + i, 0))
    kv = pl.BlockSpec((N_MEM, GW), lambda b, i: (b, 0))
    return _call(
        body, f"mem_attn_bwd_{tag}", (bl, nq),
        [tok, pl.BlockSpec((None, 1, GW), lambda b, i: (layer, 0, 0)), kv, kv, tok],
        [tok, kv, kv, pl.BlockSpec((1, GW), lambda b, i: (0, 0))],
        [_sds((bl * s, GW)), _sds((bl * N_MEM, GW)), _sds((bl * N_MEM, GW)), _sds((1, GW))],
    )(pe, qw, k, v, do)


def _gate_out_fn(outs, gates, wparts):
    y = None
    for o, g, w in zip(outs, gates, wparts):
        part = mm(o * _silu(g), w)
        y = part if y is None else y + part
    return y


def outproj_fwd(x2, outs, pg, wout, layer, tag):
    t = x2.shape[0]
    TQ = _token_tile(t)

    def body(x_ref, oa, ob, oc, od, oe, g_ref, w_ref, y_ref):
        outs_ = [r[...] for r in (oa, ob, oc, od, oe)]
        gates = [g_ref[:, j * GW:(j + 1) * GW] for j in range(5)]
        wparts = [w_ref[j * GW:(j + 1) * GW, :] for j in range(5)]
        y_ref[...] = x_ref[...] + _gate_out_fn(outs_, gates, wparts)

    tok = pl.BlockSpec((TQ, GW), lambda i: (i, 0))
    big = pl.BlockSpec((TQ, D_MODEL), lambda i: (i, 0))
    return _call(
        body, f"outproj_fwd_{tag}", (t // TQ,),
        [big] + [tok] * 5 + [pl.BlockSpec((TQ, D_MIX), lambda i: (i, 0)),
                            pl.BlockSpec((None, D_MIX, D_MODEL), lambda i: (layer, 0, 0))],
        big, _sds((t, D_MODEL)),
    )(x2, *outs, pg, wout)


def outproj_bwd(outs, pg, wout, dy, layer, tag):
    t = dy.shape[0]
    TQ = _token_tile(t)

    def body(oa, ob, oc, od, oe, g_ref, w_ref, dy_ref, da, db, dc, dd, de, dg_ref, dw_ref):
        outs_ = [r[...] for r in (oa, ob, oc, od, oe)]
        gates = [g_ref[:, j * GW:(j + 1) * GW] for j in range(5)]
        wparts = [w_ref[j * GW:(j + 1) * GW, :].astype(F32) for j in range(5)]
        _, vjp = jax.vjp(_gate_out_fn, outs_, gates, wparts)
        douts, dgates, dws = vjp(dy_ref[...])
        for r, val in zip((da, db, dc, dd, de), douts):
            r[...] = val
        first = pl.program_id(0) == 0
        for j in range(5):
            dg_ref[:, j * GW:(j + 1) * GW] = dgates[j]

        @pl.when(first)
        def _():
            for j in range(5):
                dw_ref[j * GW:(j + 1) * GW, :] = dws[j]

        @pl.when(jnp.logical_not(first))
        def _():
            for j in range(5):
                dw_ref[j * GW:(j + 1) * GW, :] += dws[j]

    tok = pl.BlockSpec((TQ, GW), lambda i: (i, 0))
    return _call(
        body, f"outproj_bwd_{tag}", (t // TQ,),
        [tok] * 5 + [pl.BlockSpec((TQ, D_MIX), lambda i: (i, 0)), pl.BlockSpec((None, D_MIX, D_MODEL), lambda i: (layer, 0, 0)),
                     pl.BlockSpec((TQ, D_MODEL), lambda i: (i, 0))],
        [tok] * 5 + [pl.BlockSpec((TQ, D_MIX), lambda i: (i, 0)), pl.BlockSpec((D_MIX, D_MODEL), lambda i: (0, 0))],
        [_sds((t, GW))] * 5 + [_sds((t, D_MIX)), _sds((D_MIX, D_MODEL))],
    )(*outs, pg, wout, dy)


def loss_head(y, tgt):
    t = y.shape[0]
    TQ = _token_tile(t)

    def body(y_ref, t_ref, l_ref, dy_ref):
        diff = y_ref[...] - t_ref[...]
        dy_ref[...] = diff * (1.0 / D_MODEL)
        part = 0.5 * jnp.sum(jnp.sum(diff * diff, axis=-1, keepdims=True) * (1.0 / D_MODEL), axis=0, keepdims=True)
        _acc(l_ref, jnp.broadcast_to(part, (8, 128)), pl.program_id(0) == 0)

    big = pl.BlockSpec((TQ, D_MODEL), lambda i: (i, 0))
    return _call(body, "loss_head", (t // TQ,), [big, big], [pl.BlockSpec((8, 128), lambda i: (0, 0)), big],
                 [_sds((8, 128)), _sds((t, D_MODEL))])(y, tgt)


def layer_fwd(x2, mem2, p, layer, bl, s):
    tag = f"l{layer}"
    ht, pa, pb, pc, pd, pe, pg, pf = inproj_fwd(x2, p["norm_g"], p["w_in"], layer, tag)
    qn, kn, vb, cq, ck = fox_prep_fwd(pa, pf, p["fox_q_norm"], p["fox_k_norm"], p["fox_f_bias"], bl, s, layer, tag)
    oa, lse = fox_attn_fwd(qn, kn, vb, cq, ck, bl, s, tag)
    ob = sb_attn_fwd(pb, bl, s, tag)
    oc, states = hgrn_fwd(pc, p["lb"], p["hgrn_out_norm"], bl, s, layer, tag)
    od = pool_fwd(pd, p["pool_wbd"], p["pool_scale"], bl, s, layer, tag)
    mk, mv = mem_prep_fwd(mem2, p["mem_norm_g"], p["mem_w_kv"], p["mem_k_norm"], bl, layer, tag)
    oe = mem_attn_fwd(pe, p["mem_q_norm"], mk, mv, bl, s, layer, tag)
    y = outproj_fwd(x2, (oa, ob, oc, od, oe), pg, p["w_out"], layer, tag)
    saved = dict(x2=x2, ht=ht, pa=pa, pb=pb, pc=pc, pd=pd, pe=pe, pg=pg, pf=pf, qn=qn, kn=kn, vb=vb, cq=cq, ck=ck,
                 oa=oa, lse=lse, ob=ob, oc=oc, states=states, od=od, mk=mk, mv=mv, oe=oe)
    return y, saved


def layer_bwd(dy, mem2, p, sv, layer, bl, s):
    tag = f"l{layer}"
    (doa, dob, doc, dod, doe, dg_gates, dwout) = outproj_bwd((sv["oa"], sv["ob"], sv["oc"], sv["od"], sv["oe"]), sv["pg"],
                                                              p["w_out"], dy, layer, tag)
    dqn, dkn, dv, dck = fox_attn_bwd(sv["qn"], sv["kn"], sv["vb"], sv["cq"], sv["ck"], sv["lse"], doa, bl, s, tag)
    d_a, d_f, dqw, dkw, dbias = fox_prep_bwd(sv["pa"], sv["pf"], p["fox_q_norm"], p["fox_k_norm"], p["fox_f_bias"], sv["cq"],
                                             dqn, dkn, dv, dck, bl, s, layer, tag)
    dsq, dsk, dsv = sb_attn_bwd(sv["pb"], dob, bl, s, tag)
    d_c, dlb, dwn = hgrn_bwd(sv["pc"], p["lb"], p["hgrn_out_norm"], sv["states"], doc, bl, s, layer, tag)
    d_d, dwbd, dpscale = pool_bwd(sv["pd"], p["pool_wbd"], p["pool_scale"], dod, bl, s, layer, tag)
    d_e, dmk, dmv, dmqw = mem_attn_bwd(sv["pe"], p["mem_q_norm"], sv["mk"], sv["mv"], doe, bl, s, layer, tag)
    dwk, dwv, dmg, dmkw = mem_prep_bwd(mem2, p["mem_norm_g"], p["mem_w_kv"], p["mem_k_norm"], dmk, dmv, bl, layer, tag)
    dpieces = (d_a, dsq, dsk, dsv, d_c, d_d, d_e, dg_gates, d_f)
    dx, dng = inproj_bwd_dx(sv["x2"], p["norm_g"], p["w_in"], dy, dpieces, layer, tag)
    dwin = jnp.concatenate([matmul_acc(sv["ht"], dp, f"{tag}_{nm}") for (nm, _, _), dp in zip(BWD_PIECES, dpieces)], axis=1)
    grads = dict(norm_g=dng, w_in=dwin, fox_f_bias=dbias, fox_q_norm=dqw, fox_k_norm=dkw, lb=dlb, hgrn_out_norm=dwn,
                 pool_wbd=dwbd, pool_scale=dpscale, mem_norm_g=dmg, mem_w_kv=jnp.concatenate([dwk, dwv], axis=1),
                 mem_q_norm=dmqw, mem_k_norm=dmkw, w_out=dwout)
    return dx, grads


def _tile4(w):
    return jnp.tile(w, (1, NH))[:, None, :]


def prepare_params(norm_g, w_in_p, fox_f_bias, fox_q_norm, fox_k_norm, hgrn_lb_logits, hgrn_out_norm, pool_w, pool_scale,
                   mem_norm_g, mem_w_kv, mem_q_norm, mem_k_norm, w_out):
    p1 = jax.nn.sigmoid(hgrn_lb_logits[1] - hgrn_lb_logits[0])
    lb = jnp.stack([jnp.zeros_like(p1), jnp.clip(p1, 0.0, 1.0 - 1e-6)])
    eye = jnp.eye(4, dtype=F32)
    wbd = jnp.einsum("lgcd,gh->lgchd", pool_w, eye).reshape(2, GW, GW)
    return dict(norm_g=norm_g[:, None, :], w_in=w_in_p, fox_f_bias=jnp.pad(fox_f_bias, ((0, 0), (0, 124)))[:, None, :],
                fox_q_norm=_tile4(fox_q_norm), fox_k_norm=_tile4(fox_k_norm), lb=lb[:, None, :],
                hgrn_out_norm=hgrn_out_norm[:, None, :], pool_wbd=wbd, pool_scale=pool_scale[:, None, :],
                mem_norm_g=mem_norm_g[:, None, :], mem_w_kv=mem_w_kv, mem_q_norm=_tile4(mem_q_norm),
                mem_k_norm=_tile4(mem_k_norm), w_out=w_out)


def local_step(x, mem, tgt, p):
    bl, s, _ = x.shape
    x2, mem2, tgt2 = x.reshape(bl * s, D_MODEL), mem.reshape(bl * N_MEM, D_MODEL), tgt.reshape(bl * s, D_MODEL)
    y0, sv0 = layer_fwd(x2, mem2, p, 0, bl, s)
    y1, sv1 = layer_fwd(y0, mem2, p, 1, bl, s)
    lpart, dy = loss_head(y1, tgt2)
    dx1, g1 = layer_bwd(dy, mem2, p, sv1, 1, bl, s)
    dx0, g0 = layer_bwd(dx1, mem2, p, sv0, 0, bl, s)
    return lpart[0, 0], dx0.reshape(bl, s, D_MODEL), g0, g1


_ANY = pl.BlockSpec(memory_space=pl.ANY)


def _me_and_peers():
    x, y, c = lax.axis_index("x"), lax.axis_index("y"), lax.axis_index("c")
    peers = []
    for k in range(1, N_DEV):
        px = 1 - x if (k >> 2) & 1 else x
        py = 1 - y if (k >> 1) & 1 else y
        pc = 1 - c if k & 1 else c
        peers.append(((px, py, pc), 4 * px + 2 * py + pc))
    return 4 * x + 2 * y + c, peers


def all_gather_rows(xs, tag):
    nl, r, c = xs.shape

    def body(x_ref, o_ref, send_sems, recv_sems, local_sem):
        x, y, cc = lax.axis_index("x"), lax.axis_index("y"), lax.axis_index("c")
        me, sibling = (x, y, cc), (x, y, 1 - cc)
        chips = [(1 - x, y), (x, 1 - y), (1 - x, 1 - y)]

        def rows(px, py, pc):
            return o_ref.at[:, pl.ds((4 * px + 2 * py + pc) * r, r), :]

        def copy(k, block, to, src=None):
            return pltpu.make_async_remote_copy(src_ref=rows(*block) if src is None else src, dst_ref=rows(*block),
                                                send_sem=send_sems.at[k], recv_sem=recv_sems.at[k], device_id=to,
                                                device_id_type=pl.DeviceIdType.MESH)

        mine = pltpu.make_async_copy(x_ref, rows(*me), local_sem)
        mine.start()
        first = [copy(0, me, sibling, src=x_ref)] + [copy(1 + j, me, (*chip, cc), src=x_ref) for j, chip in enumerate(chips)]
        for cp in first:
            cp.start()
        passed = [copy(4 + j, (*chip, cc), sibling) for j, chip in enumerate(chips)]
        for j, chip in enumerate(chips):
            copy(1 + j, (*chip, cc), me).wait_recv()
            passed[j].start()
        copy(0, sibling, me).wait_recv()
        for j, chip in enumerate(chips):
            copy(4 + j, (*chip, 1 - cc), me).wait_recv()
        for cp in first + passed:
            cp.wait_send()
        mine.wait()

    return pl.pallas_call(
        body, name=f"all_gather_{tag}", in_specs=[_ANY], out_specs=_ANY, out_shape=_sds((nl, N_DEV * r, c), xs.dtype),
        scratch_shapes=[pltpu.SemaphoreType.DMA((N_DEV - 1,)), pltpu.SemaphoreType.DMA((N_DEV - 1,)), pltpu.SemaphoreType.DMA],
    )(xs)


def exchange_cores(part, tag):
    nl, _, _, r, c = part.shape

    def body(p_ref, theirs_ref, send_sems, recv_sems):
        x, y, cc = lax.axis_index("x"), lax.axis_index("y"), lax.axis_index("c")
        copies = []
        for l in range(nl):
            for q in range(4):
                k = l * 4 + q
                copies.append(pltpu.make_async_remote_copy(
                    src_ref=p_ref.at[l, q, pl.ds(1 - cc, 1)], dst_ref=theirs_ref.at[l, q], send_sem=send_sems.at[k],
                    recv_sem=recv_sems.at[k], device_id=(x, y, 1 - cc), device_id_type=pl.DeviceIdType.MESH))
        for cp in copies:
            cp.start()
        for cp in copies:
            cp.wait()

    nsem = pltpu.SemaphoreType.DMA((nl * 4,))
    return pl.pallas_call(
        body, name=f"exchange_cores_{tag}", in_specs=[_ANY], out_specs=_ANY, out_shape=_sds((nl, 4, 1, r, c), part.dtype),
        scratch_shapes=[nsem, nsem],
    )(part)


def add_core_halves(part5, theirs, core, tag):
    nl, _, _, r, c = part5.shape
    tr = 64 if r % 64 == 0 else 32

    def body(core_ref, a_ref, b_ref, o_ref):
        o_ref[...] = _cast(a_ref[...] + b_ref[...])

    blk = lambda which: pl.BlockSpec((None, None, None, tr, c), lambda l, q, i, cref: (l, q, cref[0] if which else 0, i, 0))
    return pl.pallas_call(
        body, name=f"add_core_halves_{tag}", out_shape=_sds((nl, 4, 1, r, c), _MMT),
        grid_spec=pltpu.PrefetchScalarGridSpec(num_scalar_prefetch=1, grid=(nl, 4, r // tr), in_specs=[blk(True), blk(False)],
                                               out_specs=blk(False)),
        compiler_params=pltpu.CompilerParams(dimension_semantics=("arbitrary",) * 3, vmem_limit_bytes=VMEM_LIMIT_BYTES),
    )(core, part5, theirs)


def exchange_chips(s4, tag):
    nl, _, _, r, c = s4.shape

    def body(s_ref, o_ref, send_sems, recv_sems, local_sem):
        x, y, cc = lax.axis_index("x"), lax.axis_index("y"), lax.axis_index("c")
        local = pltpu.make_async_copy(s_ref.at[:, pl.ds(2 * x + y, 1)], o_ref.at[0], local_sem)
        local.start()
        copies = []
        for k in range(1, 4):
            px = 1 - x if (k >> 1) & 1 else x
            py = 1 - y if k & 1 else y
            copies.append(pltpu.make_async_remote_copy(
                src_ref=s_ref.at[:, pl.ds(2 * px + py, 1)], dst_ref=o_ref.at[k], send_sem=send_sems.at[k - 1],
                recv_sem=recv_sems.at[k - 1], device_id=(px, py, cc), device_id_type=pl.DeviceIdType.MESH))
        for cp in copies:
            cp.start()
        for cp in copies:
            cp.wait()
        local.wait()

    return pl.pallas_call(
        body, name=f"exchange_chips_{tag}", in_specs=[_ANY], out_specs=_ANY, out_shape=_sds((4, nl, 1, 1, r, c), s4.dtype),
        scratch_shapes=[pltpu.SemaphoreType.DMA((3,)), pltpu.SemaphoreType.DMA((3,)), pltpu.SemaphoreType.DMA],
    )(s4)


def exchange_row_blocks(part, tag):
    nl, r8, c = part.shape
    r = r8 // N_DEV

    def body(p_ref, o_ref, send_sems, recv_sems, local_sem):
        me, peers = _me_and_peers()
        rows = lambda idx: p_ref.at[:, pl.ds(idx * r, r), :]
        mine = pltpu.make_async_copy(rows(me), o_ref.at[0], local_sem)
        mine.start()
        copies = [pltpu.make_async_remote_copy(src_ref=rows(idx), dst_ref=o_ref.at[k + 1], send_sem=send_sems.at[k],
                                               recv_sem=recv_sems.at[k], device_id=dev, device_id_type=pl.DeviceIdType.MESH)
                  for k, (dev, idx) in enumerate(peers)]
        for cp in copies:
            cp.start()
        for cp in copies:
            cp.wait()
        mine.wait()

    return pl.pallas_call(
        body, name=f"exchange_{tag}", in_specs=[_ANY], out_specs=_ANY, out_shape=_sds((N_DEV, nl, r, c), part.dtype),
        scratch_shapes=[pltpu.SemaphoreType.DMA((N_DEV - 1,)), pltpu.SemaphoreType.DMA((N_DEV - 1,)), pltpu.SemaphoreType.DMA],
    )(part)


def _row_tile(rows):
    if rows <= 512 and rows % 64:
        return rows
    for t in (64, 40, 32, 16, 8):
        if rows % t == 0:
            return t
    return rows


def sum_slots(slots, tag):
    ns, rows, c = slots.shape
    tr = _row_tile(rows)

    def body(s_ref, o_ref):
        acc = s_ref[0].astype(F32)
        for k in range(1, ns):
            acc = acc + s_ref[k].astype(F32)
        o_ref[...] = acc

    return _call(body, f"sum_slots_{tag}", (rows // tr,), [pl.BlockSpec((ns, tr, c), lambda i: (0, i, 0))],
                 pl.BlockSpec((tr, c), lambda i: (i, 0)), _sds((rows, c)))(slots)


def _adamw(w, g, m, v):
    m = ADAM_B1 * m + (1.0 - ADAM_B1) * g
    v = ADAM_B2 * v + (1.0 - ADAM_B2) * (g * g)
    m_hat = m / (1.0 - ADAM_B1 ** ADAM_STEP)
    v_hat = v / (1.0 - ADAM_B2 ** ADAM_STEP)
    delta = -ADAM_LR * (m_hat / (jnp.sqrt(v_hat) + ADAM_EPS) + ADAM_WD * w)
    return delta, m, v


def adam_update(w, m, v, g, tag, slots=False):
    rows, c = w.shape
    tr = _row_tile(rows)
    ns = g.shape[0] if slots else 0

    def body(w_ref, m_ref, v_ref, g_ref, go_ref, d_ref, mo_ref, vo_ref):
        if slots:
            g = g_ref[0].astype(F32)
            for k in range(1, ns):
                g = g + g_ref[k].astype(F32)
        else:
            g = g_ref[...]
        d, mn, vn = _adamw(w_ref[...], g, m_ref[...], v_ref[...])
        go_ref[...] = g
        d_ref[...] = d
        mo_ref[...] = mn
        vo_ref[...] = vn

    blk = pl.BlockSpec((tr, c), lambda i: (i, 0))
    gspec = pl.BlockSpec((ns, tr, c), lambda i: (0, i, 0)) if slots else blk
    return _call(body, f"adam_{tag}", (rows // tr,), [blk, blk, blk, gspec], [blk] * 4, [_sds((rows, c))] * 4)(w, m, v, g)


_SMALL = (("norm_g", (2, 1024)), ("fox_f_bias", (2, 4)), ("fox_q_norm", (2, 64)), ("fox_k_norm", (2, 64)),
          ("hgrn_lb_logits", (2, 256)), ("hgrn_out_norm", (2, 256)), ("pool_w", (2, 4, 64, 64)), ("pool_scale", (2, 256)),
          ("mem_norm_g", (2, 1024)), ("mem_q_norm", (2, 64)), ("mem_k_norm", (2, 64)))
_SLAB_ROWS = 312


def pack_small(d):
    flat = jnp.concatenate([d[n].reshape(-1) for n, _ in _SMALL])
    return jnp.pad(flat, (0, _SLAB_ROWS * 128 - flat.shape[0])).reshape(_SLAB_ROWS, 128)


def unpack_small(slab):
    flat, out, off = slab.reshape(-1), {}, 0
    for n, shp in _SMALL:
        size = 1
        for e in shp:
            size *= e
        out[n] = flat[off:off + size].reshape(shp)
        off += size
    return out


def small_grads(g0, g1, lb_logits):
    st = lambda f: jnp.stack([f(g0), f(g1)])
    heads = lambda a: a.reshape(NH, HD).sum(0)
    p1 = jax.nn.sigmoid(lb_logits[1] - lb_logits[0])
    inside = (p1 > 0.0) & (p1 < 1.0 - 1e-6)
    dl1 = jnp.where(inside, g1["lb"][0] * p1 * (1.0 - p1), 0.0)
    diag = lambda a: jnp.stack([a.reshape(4, HD, 4, HD)[i, :, i, :] for i in range(4)])
    return dict(norm_g=st(lambda g: g["norm_g"][0]), fox_f_bias=st(lambda g: g["fox_f_bias"][0, :NH]),
                fox_q_norm=st(lambda g: heads(g["fox_q_norm"])), fox_k_norm=st(lambda g: heads(g["fox_k_norm"])),
                hgrn_lb_logits=jnp.stack([-dl1, dl1]), hgrn_out_norm=st(lambda g: g["hgrn_out_norm"][0]),
                pool_w=st(lambda g: diag(g["pool_wbd"])), pool_scale=st(lambda g: g["pool_scale"][0]),
                mem_norm_g=st(lambda g: g["mem_norm_g"][0]), mem_q_norm=st(lambda g: heads(g["mem_q_norm"])),
                mem_k_norm=st(lambda g: heads(g["mem_k_norm"])))


def kernel(x, mem, norm_g, w_in, fox_f_bias, fox_q_norm, fox_k_norm, hgrn_lb_logits, hgrn_out_norm, pool_w, pool_scale, mem_norm_g, mem_w_kv, mem_q_norm, mem_k_norm, w_out, loss_target, m_norm_g, m_w_in, m_fox_f_bias, m_fox_q_norm, m_fox_k_norm, m_hgrn_lb_logits, m_hgrn_out_norm, m_pool_w, m_pool_scale, m_mem_norm_g, m_mem_w_kv, m_mem_q_norm, m_mem_k_norm, m_w_out, v_norm_g, v_w_in, v_fox_f_bias, v_fox_q_norm, v_fox_k_norm, v_hgrn_lb_logits, v_hgrn_out_norm, v_pool_w, v_pool_scale, v_mem_norm_g, v_mem_w_kv, v_mem_q_norm, v_mem_k_norm, v_w_out):
    given = dict(norm_g=(norm_g, m_norm_g, v_norm_g), w_in=(w_in, m_w_in, v_w_in), fox_f_bias=(fox_f_bias, m_fox_f_bias, v_fox_f_bias),
                 fox_q_norm=(fox_q_norm, m_fox_q_norm, v_fox_q_norm), fox_k_norm=(fox_k_norm, m_fox_k_norm, v_fox_k_norm),
                 hgrn_lb_logits=(hgrn_lb_logits, m_hgrn_lb_logits, v_hgrn_lb_logits),
                 hgrn_out_norm=(hgrn_out_norm, m_hgrn_out_norm, v_hgrn_out_norm), pool_w=(pool_w, m_pool_w, v_pool_w),
                 pool_scale=(pool_scale, m_pool_scale, v_pool_scale), mem_norm_g=(mem_norm_g, m_mem_norm_g, v_mem_norm_g),
                 mem_w_kv=(mem_w_kv, m_mem_w_kv, v_mem_w_kv), mem_q_norm=(mem_q_norm, m_mem_q_norm, v_mem_q_norm),
                 mem_k_norm=(mem_k_norm, m_mem_k_norm, v_mem_k_norm), w_out=(w_out, m_w_out, v_w_out))
    order = ("norm_g", "w_in", "fox_f_bias", "fox_q_norm", "fox_k_norm", "hgrn_lb_logits", "hgrn_out_norm", "pool_w",
             "pool_scale", "mem_norm_g", "mem_w_kv", "mem_q_norm", "mem_k_norm", "w_out")

    w_in_full = all_gather_rows(_cast(permute_cols(w_in)), "w_in")
    w_out_full = all_gather_rows(_cast(w_out), "w_out")
    w_kv_full = all_gather_rows(_cast(mem_w_kv), "w_kv")
    p = prepare_params(norm_g, w_in_full, fox_f_bias, fox_q_norm, fox_k_norm, hgrn_lb_logits, hgrn_out_norm, pool_w,
                       pool_scale, mem_norm_g, w_kv_full, mem_q_norm, mem_k_norm, w_out_full)

    loss_part, grad_x, g0, g1 = local_step(x, mem, loss_target, p)
    loss = lax.psum(loss_part, ("x", "y", "c"))

    res = {}
    core = lax.axis_index("c").astype(jnp.int32).reshape(1)

    def sharded(name, g2, unperm=False, two_stage=True):
        w, m, v = given[name]
        nl, r, c = w.shape
        cp = g2.shape[-1]
        if two_stage:
            part5 = g2.reshape(nl, 4, 2, r, cp)
            s4 = add_core_halves(part5, exchange_cores(part5, name), core, name)
            slots = exchange_chips(s4, name).reshape(4, nl * r, cp)
        else:
            slots = exchange_row_blocks(g2, name).reshape(N_DEV, nl * r, cp)
        if unperm:
            g = sum_slots(slots, name).reshape(nl, r, cp)
            out = adam_update(w.reshape(nl * r, c), m.reshape(nl * r, c), v.reshape(nl * r, c),
                              unpermute_cols(g).reshape(nl * r, c), name)
        else:
            out = adam_update(w.reshape(nl * r, c), m.reshape(nl * r, c), v.reshape(nl * r, c), slots, name, slots=True)
        res[name] = tuple(o.reshape(nl, r, c) for o in out)

    sharded("w_in", jnp.stack([g0["w_in"], g1["w_in"]]), unperm=True)
    sharded("w_out", jnp.stack([g0["w_out"], g1["w_out"]]))
    sharded("mem_w_kv", _cast(jnp.stack([g0["mem_w_kv"], g1["mem_w_kv"]])), two_stage=False)

    gsmall = pack_small(small_grads(g0, g1, hgrn_lb_logits))
    gathered = all_gather_rows(gsmall[None], "small").reshape(N_DEV, _SLAB_ROWS, 128)
    slabs = adam_update(*[pack_small({n: given[n][j] for n, _ in _SMALL}) for j in range(3)], gathered, "small", slots=True)
    small = [unpack_small(sl) for sl in slabs]
    for n, _ in _SMALL:
        res[n] = tuple(small[j][n] for j in range(4))

    return (loss, grad_x, *[res[n][0] for n in order], *[res[n][1] for n in order], *[res[n][2] for n in order],
            *[res[n][3] for n in order])
```

```python
import functools

import jax
import jax.numpy as jnp
from jax import lax
from jax.experimental import pallas as pl
from jax.experimental.pallas import tpu as pltpu

F32 = jnp.float32
BF = jnp.bfloat16
_MMT = BF

D_MODEL = 1024
GW = 256
HD = 64
NH = 4
CH = 64
N_MEM = 256
D_IN = 4100
D_INP = 4224
D_MIX = 1280
EPS = 1e-6
NEG_BIG = -1e30
LB_FLOOR = 1e-30
SCALE = HD ** -0.5
TQ = 256
TM = 512


def _token_tile(n):
    return TM if n % TM == 0 else TQ
N_DEV = 8
VMEM_LIMIT_BYTES = 56 * 1024 * 1024

ADAM_LR = 0.001
ADAM_B1 = 0.9
ADAM_B2 = 0.999
ADAM_EPS = 1e-08
ADAM_WD = 0.01
ADAM_STEP = 10

PIECES = (("A", 0, 768), ("B", 768, 768), ("C", 1536, 768), ("D", 2304, 256), ("E", 2560, 256),
          ("G", 2816, 1280), ("F", 4096, 128))
BWD_PIECES = (("A", 0, 768), ("Bq", 768, 256), ("Bk", 1024, 256), ("Bv", 1280, 256), ("C", 1536, 768),
              ("D", 2304, 256), ("E", 2560, 256), ("G", 2816, 1280), ("F", 4096, 128))
_ORIG = dict(fq=(0, 256), fk=(256, 512), fv=(512, 768), fg=(768, 1024), ff=(1024, 1028), sq=(1028, 1284),
             sk=(1284, 1540), sv=(1540, 1796), sg=(1796, 2052), hq=(2052, 2308), hf=(2308, 2564),
             hi=(2564, 2820), hg=(2820, 3076), pv=(3076, 3332), pg=(3332, 3588), mq=(3588, 3844), mg=(3844, 4100))
_PERM_ORDER = ("fq", "fk", "fv", "sq", "sk", "sv", "hq", "hf", "hi", "pv", "mq", "fg", "sg", "hg", "pg", "mg", "ff")
_ORIG_ORDER = ("fq", "fk", "fv", "fg", "ff", "sq", "sk", "sv", "sg", "hq", "hf", "hi", "hg", "pv", "pg", "mq", "mg")


def permute_cols(w):
    parts = [w[..., _ORIG[n][0]:_ORIG[n][1]] for n in _PERM_ORDER]
    parts.append(jnp.zeros(w.shape[:-1] + (D_INP - D_IN,), w.dtype))
    return jnp.concatenate(parts, axis=-1)


def unpermute_cols(g):
    start, off = {}, 0
    for n in _PERM_ORDER:
        start[n] = off
        off += _ORIG[n][1] - _ORIG[n][0]
    return jnp.concatenate([g[..., start[n]:start[n] + _ORIG[n][1] - _ORIG[n][0]] for n in _ORIG_ORDER], axis=-1)


def _cast(a):
    return a.astype(_MMT)


def _dg(a, b, ca, cb):
    return lax.dot_general(a, b, (((ca,), (cb,)), ((), ())), preferred_element_type=F32)


@jax.custom_vjp
def mm(a, b):
    return _dg(_cast(a), _cast(b), 1, 0)


@jax.custom_vjp
def mm_nt(a, b):
    return _dg(_cast(a), _cast(b), 1, 1)


@jax.custom_vjp
def mm_tn(a, b):
    return _dg(_cast(a), _cast(b), 0, 0)


mm.defvjp(lambda a, b: (mm(a, b), (a, b)),
          lambda r, g: (mm_nt(g, r[1]).astype(r[0].dtype), mm_tn(r[0], g).astype(r[1].dtype)))
mm_nt.defvjp(lambda a, b: (mm_nt(a, b), (a, b)),
             lambda r, g: (mm(g, r[1]).astype(r[0].dtype), mm_tn(g, r[0]).astype(r[1].dtype)))
mm_tn.defvjp(lambda a, b: (mm_tn(a, b), (a, b)),
             lambda r, g: (mm_nt(r[1], g).astype(r[0].dtype), mm(r[0], g).astype(r[1].dtype)))


def _split(a):
    hi = a.astype(_MMT)
    lo = (a - hi.astype(F32)).astype(_MMT)
    return hi, lo


@jax.custom_vjp
def xr(a, c, ct):
    hi, lo = _split(a)
    cc = _cast(c)
    return _dg(hi, cc, 1, 0) + _dg(lo, cc, 1, 0)


@jax.custom_vjp
def xl(c, ct, a):
    hi, lo = _split(a)
    cc = _cast(c)
    return _dg(cc, hi, 1, 0) + _dg(cc, lo, 1, 0)


xr.defvjp(lambda a, c, ct: (xr(a, c, ct), (c, ct)),
          lambda r, g: (xr(g, r[1], r[0]), jnp.zeros_like(r[0]), jnp.zeros_like(r[1])))
xl.defvjp(lambda c, ct, a: (xl(c, ct, a), (c, ct)),
          lambda r, g: (jnp.zeros_like(r[0]), jnp.zeros_like(r[1]), xl(r[1], r[0], g)))


def _iota(shape, dim):
    return lax.broadcasted_iota(jnp.int32, shape, dim)


def _hmask(h, n=GW):
    lane = _iota((1, n), 1)
    return ((lane >= h * HD) & (lane < (h + 1) * HD)).astype(F32)


def _bdmask(n=GW):
    return ((_iota((n, n), 0) >> 6) == (_iota((n, n), 1) >> 6)).astype(F32)


def _tri(n, kind="le"):
    r, c = _iota((n, n), 0), _iota((n, n), 1)
    return {"le": c <= r, "ge": c >= r, "gt": c > r, "lt": c < r}[kind].astype(F32)


def _onehot_lane(h, n=128):
    return (_iota((1, n), 1) == h).astype(F32)


def _logsig(x):
    return jnp.minimum(x, 0.0) - jnp.log1p(jnp.exp(-jnp.abs(x)))


def _sigmoid(x):
    return 0.5 * (jnp.tanh(0.5 * x) + 1.0)


def _silu(x):
    return x * _sigmoid(x)


def _rms(x, g):
    return x * lax.rsqrt(jnp.mean(x * x, axis=-1, keepdims=True) + EPS) * g


def _headrms(x, w, bd64):
    ms = xr(x * x, bd64, bd64)
    return x * lax.rsqrt(ms + EPS) * w


def _call(body, name, grid, in_specs, out_specs, out_shape, scratch=()):
    return pl.pallas_call(
        body, name=name, grid=grid, in_specs=in_specs, out_specs=out_specs, out_shape=out_shape,
        scratch_shapes=list(scratch),
        compiler_params=pltpu.CompilerParams(dimension_semantics=("arbitrary",) * len(grid),
                                             vmem_limit_bytes=VMEM_LIMIT_BYTES))


def _sds(shape, dtype=F32):
    return jax.ShapeDtypeStruct(shape, dtype)


def _acc(ref, val, first):
    @pl.when(first)
    def _():
        ref[...] = val

    @pl.when(jnp.logical_not(first))
    def _():
        ref[...] += val


def inproj_fwd(x2, g, w, layer, tag):
    t = x2.shape[0]
    TQ = _token_tile(t)

    def body(x_ref, g_ref, w_ref, ht_ref, *outs):
        h = _rms(x_ref[...], g_ref[...])
        hb = _cast(h)
        ht_ref[...] = _cast(h.T)
        for (_, c0, wd), o in zip(PIECES, outs):
            o[...] = _dg(hb, _cast(w_ref[:, c0:c0 + wd]), 1, 0)

    return _call(
        body, f"inproj_fwd_{tag}", (t // TQ,),
        [pl.BlockSpec((TQ, D_MODEL), lambda i: (i, 0)),
         pl.BlockSpec((None, 1, D_MODEL), lambda i: (layer, 0, 0)),
         pl.BlockSpec((None, D_MODEL, D_INP), lambda i: (layer, 0, 0))],
        [pl.BlockSpec((D_MODEL, TQ), lambda i: (0, i))] + [pl.BlockSpec((TQ, wd), lambda i: (i, 0)) for _, _, wd in PIECES],
        [_sds((D_MODEL, t), _MMT)] + [_sds((t, wd)) for _, _, wd in PIECES],
    )(x2, g, w)


def inproj_bwd_dx(x2, g, w, dy, dpieces, layer, tag):
    t = x2.shape[0]

    def body(x_ref, g_ref, w_ref, dy_ref, *rest):
        dps, (dx_ref, dg_ref) = rest[:len(BWD_PIECES)], rest[len(BWD_PIECES):]
        dh = None
        for (_, c0, wd), dp in zip(BWD_PIECES, dps):
            part = _dg(_cast(dp[...]), _cast(w_ref[:, c0:c0 + wd]), 1, 1)
            dh = part if dh is None else dh + part
        _, vjp = jax.vjp(_rms, x_ref[...], g_ref[...])
        dx, dg = vjp(dh)
        dx_ref[...] = dy_ref[...] + dx
        _acc(dg_ref, dg, pl.program_id(0) == 0)

    return _call(
        body, f"inproj_bwd_dx_{tag}", (t // TQ,),
        [pl.BlockSpec((TQ, D_MODEL), lambda i: (i, 0)),
         pl.BlockSpec((None, 1, D_MODEL), lambda i: (layer, 0, 0)),
         pl.BlockSpec((None, D_MODEL, D_INP), lambda i: (layer, 0, 0)),
         pl.BlockSpec((TQ, D_MODEL), lambda i: (i, 0))] + [pl.BlockSpec((TQ, wd), lambda i: (i, 0)) for _, _, wd in BWD_PIECES],
        [pl.BlockSpec((TQ, D_MODEL), lambda i: (i, 0)), pl.BlockSpec((1, D_MODEL), lambda i: (0, 0))],
        [_sds((t, D_MODEL)), _sds((1, D_MODEL))],
    )(x2, g, w, dy, *dpieces)


def matmul_acc(at, b, tag):
    m, t = at.shape
    n = b.shape[1]
    tn = {1280: 640, 768: 768}.get(n, n)
    tk = 2048 if t % 2048 == 0 else (512 if t % 512 == 0 else TQ)

    def body(a_ref, b_ref, o_ref):
        _acc(o_ref, _dg(_cast(a_ref[...]), _cast(b_ref[...]), 1, 0), pl.program_id(1) == 0)

    return _call(
        body, f"matmul_acc_{tag}", (n // tn, t // tk),
        [pl.BlockSpec((m, tk), lambda j, i: (0, i)), pl.BlockSpec((tk, tn), lambda j, i: (i, j))],
        pl.BlockSpec((m, tn), lambda j, i: (0, j)),
        _sds((m, n)),
    )(at, b)


def _fox_prep_fn(q, k, ff, qw, kw, bias, carry, bd64, tri, trit, last):
    qn = _headrms(q, qw, bd64)
    kn = _headrms(k, kw, bd64)
    lf = _logsig(ff + bias)
    c = xl(tri, trit, lf) + carry
    return qn, kn, c, jnp.sum(c * last, axis=0, keepdims=True)


def _prep_consts(tq):
    return _bdmask() * (1.0 / HD), _tri(tq), _tri(tq, "ge"), (_iota((tq, 1), 0) == tq - 1).astype(F32)


def fox_prep_fwd(pa, pf, qw, kw, bias, bl, s, layer, tag):
    TQ = _token_tile(s)
    nq = s // TQ

    def body(q_ref, k_ref, v_ref, f_ref, qw_ref, kw_ref, b_ref, qn_ref, kn_ref, vb_ref, cq_ref, ck_ref, carry):
        @pl.when(pl.program_id(1) == 0)
        def _():
            carry[...] = jnp.zeros_like(carry)

        qn, kn, c, cl = _fox_prep_fn(q_ref[...], k_ref[...], f_ref[...], qw_ref[...], kw_ref[...], b_ref[...],
                                     carry[...], *_prep_consts(TQ))
        carry[...] = cl
        qn_ref[...] = _cast(qn)
        kn_ref[...] = _cast(kn)
        vb_ref[...] = _cast(v_ref[...])
        cq_ref[...] = c
        ck_ref[...] = c.T[0:8, :]

    tok = lambda j: pl.BlockSpec((TQ, GW), lambda b, i: (b * nq + i, j))
    par = lambda n: pl.BlockSpec((None, 1, n), lambda b, i: (layer, 0, 0))
    return _call(
        body, f"fox_prep_fwd_{tag}", (bl, nq),
        [tok(0), tok(1), tok(2), pl.BlockSpec((TQ, 128), lambda b, i: (b * nq + i, 0)), par(GW), par(GW), par(128)],
        [tok(0), tok(0), tok(0), pl.BlockSpec((TQ, 128), lambda b, i: (b * nq + i, 0)),
         pl.BlockSpec((None, 8, TQ), lambda b, i: (b, 0, i))],
        [_sds((bl * s, GW), _MMT)] * 3 + [_sds((bl * s, 128)), _sds((bl, 8, s))],
        [pltpu.VMEM((1, 128), F32)],
    )(pa, pa, pa, pf, qw, kw, bias)


def fox_prep_bwd(pa, pf, qw, kw, bias, cq, dqn, dkn, dv, dck, bl, s, layer, tag):
    TQ = _token_tile(s)
    nq = s // TQ

    def body(q_ref, k_ref, f_ref, qw_ref, kw_ref, b_ref, cq_ref, cprev_ref, dqn_ref, dkn_ref, dv_ref, dck_ref,
             da_ref, df_ref, dqw_ref, dkw_ref, db_ref, dcarry):
        i = pl.program_id(1)
        first = jnp.logical_and(pl.program_id(0) == 0, i == 0)

        @pl.when(i == 0)
        def _():
            dcarry[...] = jnp.zeros_like(dcarry)

        last = (_iota((TQ, 1), 0) == TQ - 1).astype(F32)
        carry_in = jnp.where(i == nq - 1, 0.0, jnp.sum(cprev_ref[...] * last, axis=0, keepdims=True))
        consts = _prep_consts(TQ)
        _, vjp = jax.vjp(lambda *a: _fox_prep_fn(*a, *consts), q_ref[...], k_ref[...], f_ref[...], qw_ref[...],
                         kw_ref[...], b_ref[...], carry_in)
        dc = dck_ref[...].T
        dq, dk, dff, dqw, dkw, dbias, dcin = vjp((dqn_ref[...], dkn_ref[...], dc, dcarry[...]))
        dcarry[...] = dcin
        da_ref[:, 0:GW] = dq
        da_ref[:, GW:2 * GW] = dk
        da_ref[:, 2 * GW:3 * GW] = dv_ref[...]
        df_ref[...] = dff
        _acc(dqw_ref, dqw, first)
        _acc(dkw_ref, dkw, first)
        _acc(db_ref, dbias, first)

    rv = lambda b, i: b * nq + (nq - 1 - i)
    tok = lambda j: pl.BlockSpec((TQ, GW), lambda b, i: (rv(b, i), j))
    tok0 = pl.BlockSpec((TQ, GW), lambda b, i: (rv(b, i), 0))
    t128 = pl.BlockSpec((TQ, 128), lambda b, i: (rv(b, i), 0))
    prev = pl.BlockSpec((TQ, 128), lambda b, i: (jnp.maximum(rv(b, i) - 1, 0), 0))
    par = lambda n: pl.BlockSpec((None, 1, n), lambda b, i: (layer, 0, 0))
    acc = lambda n: pl.BlockSpec((1, n), lambda b, i: (0, 0))
    return _call(
        body, f"fox_prep_bwd_{tag}", (bl, nq),
        [tok(0), tok(1), t128, par(GW), par(GW), par(128), t128, prev, tok0, tok0, tok0,
         pl.BlockSpec((None, 128, TQ), lambda b, i: (b, 0, nq - 1 - i))],
        [pl.BlockSpec((TQ, 3 * GW), lambda b, i: (rv(b, i), 0)), t128, acc(GW), acc(GW), acc(128)],
        [_sds((bl * s, 3 * GW)), _sds((bl * s, 128)), _sds((1, GW)), _sds((1, GW)), _sds((1, 128))],
        [pltpu.VMEM((1, 128), F32)],
    )(pa, pa, pf, qw, kw, bias, cq, cq, dqn, dkn, dv, dck)


def _lane_pick(x, h):
    return jnp.sum(x * _onehot_lane(h), axis=-1, keepdims=True)


TA_BIG = 256
TK_FOX = 512
TK_SB = 256


def _stack_heads(x, scale=1.0):
    return _cast(jnp.concatenate([x * (_hmask(h) * scale) for h in range(NH)], axis=0))


def _stack_cols(x):
    return jnp.concatenate([_lane_pick(x, h) for h in range(NH)], axis=0)


def _spread_heads(col):
    ta = col.shape[0] // NH
    return sum(col[h * ta:(h + 1) * ta] * _hmask(h) for h in range(NH))


def _lanes_cat(w):
    ta = w.shape[0] // NH
    return jnp.concatenate([w[h * ta:(h + 1) * ta] for h in range(NH)], axis=1)


def _mask_stack(x):
    return _cast(jnp.concatenate([x * _hmask(h).astype(x.dtype) for h in range(NH)], axis=0))


def _stack_rows(i, ta):
    return i * ta + (_iota((NH * ta, 1), 0) & (ta - 1))


def _n_key_tiles(i, tk, ta):
    return lax.shift_right_logical(i * ta, tk.bit_length() - 1) + 1


def fox_attn_fwd(qn, kn, vb, cq, ck, bl, s, tag):
    TA, TK = min(TA_BIG, s), min(TK_FOX, s)
    nq, SROWS = s // TA, NH * TA

    def body(q_ref, k_ref, v_ref, cq_ref, ck_ref, o_ref, lse_ref, acc, vst):
        i = pl.program_id(1)

        @pl.when(i == 0)
        def _():
            _fill_stacked(vst, v_ref, s, TK)

        qs = _stack_heads(q_ref[...].astype(F32), SCALE)
        cqs = _stack_cols(cq_ref[...])
        row = _stack_rows(i, TA)
        acc[...] = jnp.zeros_like(acc)

        def step(j, ml):
            m, l = ml
            ks = pl.ds(pl.multiple_of(j * TK, TK), TK)
            ckb = jnp.concatenate([jnp.broadcast_to(ck_ref[h:h + 1, ks], (TA, TK)) for h in range(NH)], axis=0)
            sc = _dg(qs, k_ref[ks, :], 1, 1) + cqs - ckb
            col = j * TK + _iota((1, TK), 1)
            sc = jnp.where(col <= row, sc, NEG_BIG)
            m_new = jnp.maximum(m, jnp.max(sc, axis=-1, keepdims=True))
            alpha = jnp.exp(m - m_new)
            p = jnp.exp(sc - m_new)
            vs = vst[pl.ds(pl.multiple_of(j * NH * TK, NH * TK), NH * TK), :]
            acc[...] = _spread_heads(alpha) * acc[...] + _dg(_lanes_cat(_cast(p)), vs, 1, 0)
            return m_new, alpha * l + jnp.sum(p, axis=-1, keepdims=True)

        m, l = lax.fori_loop(0, _n_key_tiles(i, TK, TA), step, (jnp.full((SROWS, 1), NEG_BIG, F32), jnp.zeros((SROWS, 1), F32)))
        o_ref[...] = acc[...] / _spread_heads(l)
        lse_h = m + jnp.log(l)
        lse_ref[...] = sum(lse_h[h * TA:(h + 1) * TA] * _onehot_lane(h) for h in range(NH))

    tok = pl.BlockSpec((TA, GW), lambda b, i: (b * nq + i, 0))
    seq = pl.BlockSpec((s, GW), lambda b, i: (b, 0))
    t128 = pl.BlockSpec((TA, 128), lambda b, i: (b * nq + i, 0))
    return _call(
        body, f"fox_attn_fwd_{tag}", (bl, nq),
        [tok, seq, seq, t128, pl.BlockSpec((None, 8, s), lambda b, i: (b, 0, 0))],
        [tok, t128], [_sds((bl * s, GW)), _sds((bl * s, 128))],
        [pltpu.VMEM((TA, GW), F32), pltpu.VMEM((NH * s, GW), _MMT)],
    )(qn, kn, vb, cq, ck)


def fox_attn_bwd(qn, kn, vb, cq, ck, lse, do, bl, s, tag):
    TA, TK = min(TA_BIG, s), min(TK_FOX, s)
    nq, SROWS = s // TA, NH * TA

    def body(q_ref, k_ref, v_ref, cq_ref, ck_ref, lse_ref, do_ref, dq_ref, dk_ref, dv_ref, dck_ref, dqa, p_s, dp_s, kst):
        i = pl.program_id(1)

        @pl.when(i == 0)
        def _():
            dk_ref[...] = jnp.zeros_like(dk_ref)
            dv_ref[...] = jnp.zeros_like(dv_ref)
            dck_ref[...] = jnp.zeros_like(dck_ref)
            _fill_stacked(kst, k_ref, s, TK)

        qs = _stack_heads(q_ref[...].astype(F32), SCALE)
        dos = _stack_heads(do_ref[...])
        cqs, lses = _stack_cols(cq_ref[...]), _stack_cols(lse_ref[...])
        row = _stack_rows(i, TA)
        dqa[...] = jnp.zeros_like(dqa)
        nk = _n_key_tiles(i, TK, TA)

        def probs(j, delta):
            ks = pl.ds(pl.multiple_of(j * TK, TK), TK)
            ckb = jnp.concatenate([jnp.broadcast_to(ck_ref[h:h + 1, ks], (TA, TK)) for h in range(NH)], axis=0)
            sc = _dg(qs, k_ref[ks, :], 1, 1) + cqs - ckb
            col = j * TK + _iota((1, TK), 1)
            p = jnp.where(col <= row, jnp.exp(sc - lses), 0.0)
            dp = _dg(dos, v_ref[ks, :], 1, 1)
            p_s[:, ks] = p
            dp_s[:, ks] = dp
            return delta + jnp.sum(p * dp, axis=-1, keepdims=True)

        delta = lax.fori_loop(0, nk, probs, jnp.zeros((SROWS, 1), F32))

        def step(j, carry):
            ks = pl.ds(pl.multiple_of(j * TK, TK), TK)
            p = p_s[:, ks]
            ds = p * (dp_s[:, ks] - delta)
            dsb = _cast(ds)
            dqa[...] += _dg(_lanes_cat(dsb), kst[pl.ds(pl.multiple_of(j * NH * TK, NH * TK), NH * TK), :], 1, 0) * SCALE
            dk_ref[ks, :] += _dg(dsb, qs, 0, 0)
            dv_ref[ks, :] += _dg(_cast(p), dos, 0, 0)
            for h in range(NH):
                dck_ref[h:h + 1, ks] -= jnp.sum(ds[h * TA:(h + 1) * TA], axis=0, keepdims=True)
            return carry

        lax.fori_loop(0, nk, step, 0)
        dq_ref[...] = dqa[...]

    tok = pl.BlockSpec((TA, GW), lambda b, i: (b * nq + i, 0))
    seq = pl.BlockSpec((s, GW), lambda b, i: (b, 0))
    t128 = pl.BlockSpec((TA, 128), lambda b, i: (b * nq + i, 0))
    return _call(
        body, f"fox_attn_bwd_{tag}", (bl, nq),
        [tok, seq, seq, t128, pl.BlockSpec((None, 8, s), lambda b, i: (b, 0, 0)), t128, tok],
        [tok, seq, seq, pl.BlockSpec((None, 128, s), lambda b, i: (b, 0, 0))],
        [_sds((bl * s, GW)), _sds((bl * s, GW)), _sds((bl * s, GW)), _sds((bl, 128, s))],
        [pltpu.VMEM((TA, GW), F32), pltpu.VMEM((SROWS, s), F32), pltpu.VMEM((SROWS, s), F32), pltpu.VMEM((NH * s, GW), _MMT)],
    )(qn, kn, vb, cq, ck, lse, do)


def _sb_block(qh, kb, valid, upper, r_carry):
    z = _dg(qh, kb, 1, 1)
    ls = _logsig(z)
    lom = ls - z if valid is None else jnp.where(valid, ls - z, 0.0)
    between = xr(lom, upper, upper) + r_carry
    w = jnp.exp(ls + between)
    return ls, lom, (w if valid is None else jnp.where(valid, w, 0.0))


def _fill_stacked(dst, src_ref, s, tk):
    for j in range(s // tk):
        dst[j * NH * tk:(j + 1) * NH * tk, :] = _mask_stack(src_ref[j * tk:(j + 1) * tk, :])


def sb_attn_fwd(pb, bl, s, tag):
    TA, TK = TA_BIG, TK_SB
    nq, SROWS = s // TA, NH * TA

    def body(q_ref, k_ref, v_ref, o_ref, acc, vst):
        i = pl.program_id(1)

        @pl.when(i == 0)
        def _():
            _fill_stacked(vst, v_ref, s, TK)

        qs = _stack_heads(q_ref[...], SCALE)
        upper = _tri(TK, "lt")
        last = _n_key_tiles(i, TK, TA) - 1

        def step(j, r, valid):
            ks = pl.ds(pl.multiple_of(j * TK, TK), TK)
            _, lom, w = _sb_block(qs, _cast(k_ref[ks, :]), valid, upper, r)
            acc[...] += _dg(_lanes_cat(_cast(w)), vst[pl.ds(pl.multiple_of(j * NH * TK, NH * TK), NH * TK), :], 1, 0)
            return r + jnp.sum(lom, axis=-1, keepdims=True)

        acc[...] = jnp.zeros_like(acc)
        r = step(last, jnp.zeros((SROWS, 1), F32), last * TK + _iota((1, TK), 1) < _stack_rows(i, TA))
        lax.fori_loop(0, last, lambda jj, r: step(last - 1 - jj, r, None), r)
        o_ref[...] = acc[...]

    tok = lambda j: pl.BlockSpec((TA, GW), lambda b, i: (b * nq + i, j))
    seq = lambda j: pl.BlockSpec((s, GW), lambda b, i: (b, j))
    return _call(
        body, f"sb_attn_fwd_{tag}", (bl, nq), [tok(0), seq(1), seq(2)],
        pl.BlockSpec((TA, GW), lambda b, i: (b * nq + i, 0)), _sds((bl * s, GW)),
        [pltpu.VMEM((TA, GW), F32), pltpu.VMEM((NH * s, GW), _MMT)],
    )(pb, pb, pb)


def sb_attn_bwd(pb, do, bl, s, tag):
    TA, TK = TA_BIG, TK_SB
    nq, SROWS = s // TA, NH * TA

    def body(q_ref, k_ref, v_ref, do_ref, dq_ref, dk_ref, dv_ref, dqa, sig_s, nsig_s, w_s, g_s, kst):
        i = pl.program_id(1)

        @pl.when(i == 0)
        def _():
            dk_ref[...] = jnp.zeros_like(dk_ref)
            dv_ref[...] = jnp.zeros_like(dv_ref)
            _fill_stacked(kst, k_ref, s, TK)

        qs = _stack_heads(q_ref[...], SCALE)
        dos = _stack_heads(do_ref[...])
        upper = _tri(TK, "lt")
        before = _tri(TK, "gt")
        dqa[...] = jnp.zeros_like(dqa)
        last = _n_key_tiles(i, TK, TA) - 1
        diag = last * TK + _iota((1, TK), 1) < _stack_rows(i, TA)

        def weights(j, r, valid):
            ks = pl.ds(pl.multiple_of(j * TK, TK), TK)
            ls, lom, w = _sb_block(qs, _cast(k_ref[ks, :]), valid, upper, r)
            sig_s[:, ks] = _cast(jnp.exp(ls))
            nsig_s[:, ks] = _cast(jnp.exp(lom))
            w_s[:, ks] = _cast(w)
            g_s[:, ks] = _dg(dos, _cast(v_ref[ks, :]), 1, 1) * w
            return r + jnp.sum(lom, axis=-1, keepdims=True)

        r = weights(last, jnp.zeros((SROWS, 1), F32), diag)
        lax.fori_loop(0, last, lambda jj, r: weights(last - 1 - jj, r, None), r)

        def step(j, cpre, valid):
            ks = pl.ds(pl.multiple_of(j * TK, TK), TK)
            g = g_s[:, ks]
            pre = cpre + xr(g, before, before)
            dz = g * nsig_s[:, ks].astype(F32) - sig_s[:, ks].astype(F32) * pre
            dzb = _cast(dz if valid is None else jnp.where(valid, dz, 0.0))
            dqa[...] += _dg(_lanes_cat(dzb), kst[pl.ds(pl.multiple_of(j * NH * TK, NH * TK), NH * TK), :], 1, 0) * SCALE
            dk_ref[ks, :] += _dg(dzb, qs, 0, 0)
            dv_ref[ks, :] += _dg(w_s[:, ks], dos, 0, 0)
            return cpre + jnp.sum(g, axis=-1, keepdims=True)

        cpre = lax.fori_loop(0, last, lambda j, c: step(j, c, None), jnp.zeros((SROWS, 1), F32))
        step(last, cpre, diag)
        dq_ref[...] = dqa[...]

    tok = lambda j: pl.BlockSpec((TA, GW), lambda b, i: (b * nq + i, j))
    seq = lambda j: pl.BlockSpec((s, GW), lambda b, i: (b, j))
    return _call(
        body, f"sb_attn_bwd_{tag}", (bl, nq), [tok(0), seq(1), seq(2), tok(0)],
        [tok(0), seq(0), seq(0)], [_sds((bl * s, GW))] * 3,
        [pltpu.VMEM((TA, GW), F32), pltpu.VMEM((SROWS, s), _MMT), pltpu.VMEM((SROWS, s), _MMT),
         pltpu.VMEM((SROWS, s), _MMT), pltpu.VMEM((SROWS, s), F32), pltpu.VMEM((NH * s, GW), _MMT)],
    )(pb, pb, pb, do)


def _hgrn_consts():
    r, c = _iota((CH, CH), 0), _iota((CH, CH), 1)
    rr = _iota((CH, 1), 0)
    tri = (c <= r).astype(F32)
    lv = []
    for m in (8, 4, 2, 1):
        up = ((rr & (2 * m - 1)) >= m).astype(F32)
        selq = (((r & (2 * m - 1)) >= m) & (c == (r & ~(m - 1)) - 1)).astype(F32)
        selk = (((r & (2 * m - 1)) < m) & (c == (r & ~(m - 1)) + m - 1)).astype(F32)
        pm = (((r & ~(2 * m - 1)) == (c & ~(2 * m - 1))) & ((r & (2 * m - 1)) >= m) & ((c & (2 * m - 1)) < m)).astype(F32)
        lv.append((up, 1.0 - up, selq, selq.T, selk, selk.T, jnp.concatenate([pm] * NH, axis=0)))
    hm4 = lambda n: (((_iota((NH, 1, n), 2) & (GW - 1)) >> 6) == _iota((NH, 1, n), 0)).astype(F32)
    return dict(tri=tri, trit=tri.T, rr=rr, lv=lv, bd=_bdmask(), bd64=_bdmask() * (1.0 / HD),
                hm4={GW: hm4(GW), 3 * GW: hm4(3 * GW)})


def _hgrn_chunk_fn(hq, hf, hi, lb, wn, st, cs):
    q = _silu(hq)
    log_lb = jnp.log(jnp.maximum(lb, LB_FLOOR))
    a, bb = log_lb, jnp.log1p(-lb) + _logsig(hf)
    g = jnp.maximum(a, bb) + jnp.log1p(jnp.exp(-jnp.abs(a - bb)))
    k = (1.0 - lb) * _sigmoid(-hf)
    v = hi
    rr = cs["rr"]
    b = xl(cs["tri"], cs["trit"], g)
    row_of = lambda n: jnp.sum(b * (rr == n).astype(F32), axis=0, keepdims=True)
    o = mm_nt(q * jnp.exp(b), st)
    qs, ks = [], []
    for ib in (1, 2, 3):
        ref = row_of(16 * ib - 1)
        inq = ((rr >= 16 * ib) & (rr < 16 * ib + 16)).astype(F32)
        ink = (rr < 16 * ib).astype(F32)
        qs.append(q * jnp.exp((b - ref) * inq) * inq)
        ks.append(k * jnp.exp((ref - b) * ink) * ink)
    qcat, kcat = jnp.concatenate(qs, axis=1), jnp.concatenate(ks, axis=1)
    lvl = []
    for up, lo, selq, selqt, selk, selkt, pm in cs["lv"]:
        qe = q * jnp.exp((b - xl(selq, selqt, b)) * up) * up
        ke = k * jnp.exp((xl(selk, selkt, b) - b) * lo) * lo
        lvl.append((qe, ke, pm))
    stack = lambda x: (x[None] * cs["hm4"][x.shape[1]]).reshape(NH * CH, x.shape[1])
    a_all = mm_nt(stack(qcat), kcat)
    for qe, ke, pm4 in lvl:
        a_all = a_all + mm_nt(stack(qe), ke) * pm4
    o = o + jnp.sum(mm(a_all, v).reshape(NH, CH, GW) * cs["hm4"][GW], axis=0)
    o = o + xr(q * k, cs["bd"], cs["bd"]) * v
    b_last = row_of(CH - 1)
    st_new = st * jnp.exp(b_last) + mm_tn(v, k * jnp.exp(b_last - b)) * cs["bd"]
    return _headrms(o, wn, cs["bd64"]), st_new


def hgrn_fwd(pc, lb, wn, bl, s, layer, tag):
    nc = s // CH

    def body(q_ref, f_ref, i_ref, lb_ref, wn_ref, o_ref, st_ref, st):
        @pl.when(pl.program_id(0) == 0)
        def _():
            st[...] = jnp.zeros_like(st)

        cs = _hgrn_consts()
        for b in range(bl):
            st_ref[b] = st[b]
            o, st_new = _hgrn_chunk_fn(q_ref[b], f_ref[b], i_ref[b], lb_ref[...], wn_ref[...], st[b], cs)
            o_ref[b] = o
            st[b] = st_new

    tok = lambda j: pl.BlockSpec((bl, CH, GW), lambda c: (0, c, j))
    par = pl.BlockSpec((None, 1, GW), lambda c: (layer, 0, 0))
    pc3 = pc.reshape(bl, s, 3 * GW)
    o, states = _call(
        body, f"hgrn_fwd_{tag}", (nc,), [tok(0), tok(1), tok(2), par, par],
        [tok(0), pl.BlockSpec((bl, None, GW, GW), lambda c: (0, c, 0, 0))],
        [_sds((bl, s, GW)), _sds((bl, nc, GW, GW))],
        [pltpu.VMEM((bl, GW, GW), F32)],
    )(pc3, pc3, pc3, lb, wn)
    return o.reshape(bl * s, GW), states


def hgrn_bwd(pc, lb, wn, states, do, bl, s, layer, tag):
    nc = s // CH

    def body(q_ref, f_ref, i_ref, lb_ref, wn_ref, st_ref, do_ref, dc_ref, dlb_ref, dwn_ref, dst):
        c = pl.program_id(0)

        @pl.when(c == 0)
        def _():
            dst[...] = jnp.zeros_like(dst)

        cs = _hgrn_consts()
        dlb_sum = dwn_sum = None
        for b in range(bl):
            _, vjp = jax.vjp(lambda *a: _hgrn_chunk_fn(*a, cs), q_ref[b], f_ref[b], i_ref[b], lb_ref[...],
                             wn_ref[...], st_ref[b])
            dq, df, di, dlb, dwn, dst_in = vjp((do_ref[b], dst[b]))
            dst[b] = dst_in
            dc_ref[b, :, 0:GW] = dq
            dc_ref[b, :, GW:2 * GW] = df
            dc_ref[b, :, 2 * GW:3 * GW] = di
            dlb_sum = dlb if dlb_sum is None else dlb_sum + dlb
            dwn_sum = dwn if dwn_sum is None else dwn_sum + dwn
        _acc(dlb_ref, dlb_sum, c == 0)
        _acc(dwn_ref, dwn_sum, c == 0)

    tok = lambda j: pl.BlockSpec((bl, CH, GW), lambda c: (0, nc - 1 - c, j))
    par = pl.BlockSpec((None, 1, GW), lambda c: (layer, 0, 0))
    acc = pl.BlockSpec((1, GW), lambda c: (0, 0))
    pc3 = pc.reshape(bl, s, 3 * GW)
    dc, dlb, dwn = _call(
        body, f"hgrn_bwd_{tag}", (nc,),
        [tok(0), tok(1), tok(2), par, par, pl.BlockSpec((bl, None, GW, GW), lambda c: (0, nc - 1 - c, 0, 0)), tok(0)],
        [pl.BlockSpec((bl, CH, 3 * GW), lambda c: (0, nc - 1 - c, 0)), acc, acc],
        [_sds((bl, s, 3 * GW)), _sds((1, GW)), _sds((1, GW))],
        [pltpu.VMEM((bl, GW, GW), F32)],
    )(pc3, pc3, pc3, lb, wn, states, do.reshape(bl, s, GW))
    return dc.reshape(bl * s, 3 * GW), dlb, dwn


def _shift_rows(x, k, up):
    n = x.shape[0]
    rr = _iota((n, 1), 0)
    if up:
        return jnp.where(rr < n - k, pltpu.roll(x, n - k, 0), 0.0)
    return jnp.where(rr >= k, pltpu.roll(x, k, 0), 0.0)


def _window_sums(x, up):
    s2 = x + _shift_rows(x, 1, up)
    s4 = s2 + _shift_rows(s2, 2, up)
    s8 = s4 + _shift_rows(s4, 4, up)
    s16 = s8 + _shift_rows(s8, 8, up)
    return s2, s4, s8, s16


def _pool_div(n):
    pos = (_iota((n, 1), 0) + 1).astype(F32)
    return [jnp.minimum(pos, float(w)) for w in (2, 4, 8, 16)]


def _pool_mix(sums, scaled):
    out = None
    for gi, sw in enumerate(sums):
        part = (sw if scaled is None else sw / scaled[gi]) * _hmask(gi)
        out = part if out is None else out + part
    return out


def pool_fwd(pd, wbd, scale, bl, s, layer, tag):
    def body(u_ref, w_ref, sc_ref, o_ref):
        u = u_ref[...]
        pm = _pool_mix(_window_sums(u, False), _pool_div(s)) - u
        o_ref[...] = _dg(_cast(pm), _cast(w_ref[...]), 1, 0) * sc_ref[...]

    seq = pl.BlockSpec((s, GW), lambda b: (b, 0))
    return _call(
        body, f"pool_fwd_{tag}", (bl,),
        [seq, pl.BlockSpec((None, GW, GW), lambda b: (layer, 0, 0)), pl.BlockSpec((None, 1, GW), lambda b: (layer, 0, 0))],
        seq, _sds((bl * s, GW)),
    )(pd, wbd, scale)


def pool_bwd(pd, wbd, scale, do, bl, s, layer, tag):
    def body(u_ref, w_ref, sc_ref, do_ref, du_ref, dw_ref, dsc_ref):
        first = pl.program_id(0) == 0
        u, do = u_ref[...], do_ref[...]
        div = _pool_div(s)
        pm = _pool_mix(_window_sums(u, False), div) - u
        ypre = _dg(_cast(pm), _cast(w_ref[...]), 1, 0)
        dys = do * sc_ref[...]
        _acc(dsc_ref, jnp.sum(do * ypre, axis=0, keepdims=True), first)
        _acc(dw_ref, _dg(_cast(pm), _cast(dys), 0, 0), first)
        dpm = _dg(_cast(dys), _cast(w_ref[...]), 1, 1)
        dsc = [dpm / d for d in div]
        adj = None
        for gi in range(4):
            part = _window_sums(dsc[gi] * _hmask(gi), True)[gi]
            adj = part if adj is None else adj + part
        du_ref[...] = adj - dpm

    seq = pl.BlockSpec((s, GW), lambda b: (b, 0))
    return _call(
        body, f"pool_bwd_{tag}", (bl,),
        [seq, pl.BlockSpec((None, GW, GW), lambda b: (layer, 0, 0)), pl.BlockSpec((None, 1, GW), lambda b: (layer, 0, 0)), seq],
        [seq, pl.BlockSpec((GW, GW), lambda b: (0, 0)), pl.BlockSpec((1, GW), lambda b: (0, 0))],
        [_sds((bl * s, GW)), _sds((GW, GW)), _sds((1, GW))],
    )(pd, wbd, scale, do)


def _mem_prep_fn(mem, g, wk, wv, kw, bd64):
    mn = _rms(mem, g)
    return _headrms(mm(mn, wk), kw, bd64), mm(mn, wv)


def mem_prep_fwd(mem2, g, wkv, kw, bl, layer, tag):
    def body(m_ref, g_ref, wk_ref, wv_ref, kw_ref, k_ref, v_ref):
        k, v = _mem_prep_fn(m_ref[...], g_ref[...], wk_ref[...], wv_ref[...], kw_ref[...], _bdmask() * (1.0 / HD))
        k_ref[...] = k
        v_ref[...] = v

    blk = pl.BlockSpec((N_MEM, GW), lambda b: (b, 0))
    return _call(
        body, f"mem_prep_fwd_{tag}", (bl,),
        [pl.BlockSpec((N_MEM, D_MODEL), lambda b: (b, 0)), pl.BlockSpec((None, 1, D_MODEL), lambda b: (layer, 0, 0)),
         pl.BlockSpec((None, D_MODEL, GW), lambda b: (layer, 0, 0)), pl.BlockSpec((None, D_MODEL, GW), lambda b: (layer, 0, 1)),
         pl.BlockSpec((None, 1, GW), lambda b: (layer, 0, 0))],
        [blk, blk], [_sds((bl * N_MEM, GW))] * 2,
    )(mem2, g, wkv, wkv, kw)


def mem_prep_bwd(mem2, g, wkv, kw, dk, dv, bl, layer, tag):
    def body(m_ref, g_ref, wk_ref, wv_ref, kw_ref, dk_ref, dv_ref, dwk_ref, dwv_ref, dg_ref, dkw_ref):
        first = pl.program_id(0) == 0
        bd64 = _bdmask() * (1.0 / HD)
        _, vjp = jax.vjp(lambda g_, wk, wv, kw_: _mem_prep_fn(m_ref[...], g_, wk, wv, kw_, bd64),
                         g_ref[...], wk_ref[...].astype(F32), wv_ref[...].astype(F32), kw_ref[...])
        dg, dwk, dwv, dkw = vjp((dk_ref[...], dv_ref[...]))
        _acc(dwk_ref, dwk, first)
        _acc(dwv_ref, dwv, first)
        _acc(dg_ref, dg, first)
        _acc(dkw_ref, dkw, first)

    blk = pl.BlockSpec((N_MEM, GW), lambda b: (b, 0))
    return _call(
        body, f"mem_prep_bwd_{tag}", (bl,),
        [pl.BlockSpec((N_MEM, D_MODEL), lambda b: (b, 0)), pl.BlockSpec((None, 1, D_MODEL), lambda b: (layer, 0, 0)),
         pl.BlockSpec((None, D_MODEL, GW), lambda b: (layer, 0, 0)), pl.BlockSpec((None, D_MODEL, GW), lambda b: (layer, 0, 1)),
         pl.BlockSpec((None, 1, GW), lambda b: (layer, 0, 0)), blk, blk],
        [pl.BlockSpec((D_MODEL, GW), lambda b: (0, 0)), pl.BlockSpec((D_MODEL, GW), lambda b: (0, 0)),
         pl.BlockSpec((1, D_MODEL), lambda b: (0, 0)), pl.BlockSpec((1, GW), lambda b: (0, 0))],
        [_sds((D_MODEL, GW)), _sds((D_MODEL, GW)), _sds((1, D_MODEL)), _sds((1, GW))],
    )(mem2, g, wkv, wkv, kw, dk, dv)


def _mem_attn_fn(mq, qw, k, v, bd64):
    qn = _headrms(mq, qw, bd64)
    out = None
    for h in range(NH):
        hm = _hmask(h)
        lg = mm_nt(qn * hm, k) * SCALE
        e = jnp.exp(lg - lax.stop_gradient(jnp.max(lg, axis=-1, keepdims=True)))
        p = e / jnp.sum(e, axis=-1, keepdims=True)
        part = mm(p, v) * hm
        out = part if out is None else out + part
    return out


def mem_attn_fwd(pe, qw, k, v, bl, s, layer, tag):
    TQ = _token_tile(s)
    nq = s // TQ

    def body(q_ref, qw_ref, k_ref, v_ref, o_ref):
        o_ref[...] = _mem_attn_fn(q_ref[...], qw_ref[...], k_ref[...], v_ref[...], _bdmask() * (1.0 / HD))

    tok = pl.BlockSpec((TQ, GW), lambda b, i: (b * nq + i, 0))
    kv = pl.BlockSpec((N_MEM, GW), lambda b, i: (b, 0))
    return _call(
        body, f"mem_attn_fwd_{tag}", (bl, nq), [tok, pl.BlockSpec((None, 1, GW), lambda b, i: (layer, 0, 0)), kv, kv],
        tok, _sds((bl * s, GW)),
    )(pe, qw, k, v)


def mem_attn_bwd(pe, qw, k, v, do, bl, s, layer, tag):
    TQ = _token_tile(s)
    nq = s // TQ

    def body(q_ref, qw_ref, k_ref, v_ref, do_ref, dq_ref, dk_ref, dv_ref, dqw_ref):
        i = pl.program_id(1)
        bd64 = _bdmask() * (1.0 / HD)
        _, vjp = jax.vjp(lambda *a: _mem_attn_fn(*a, bd64), q_ref[...], qw_ref[...], k_ref[...], v_ref[...])
        dq, dqw, dk, dv = vjp(do_ref[...])
        dq_ref[...] = dq
        _acc(dk_ref, dk, i == 0)
        _acc(dv_ref, dv, i == 0)
        _acc(dqw_ref, dqw, jnp.logical_and(pl.program_id(0) == 0, i == 0))

    tok = pl.BlockSpec((TQ, GW), lambda b, i: (b * nq + i, 0))
    kv = pl.BlockSpec((N_MEM, GW), lambda b, i: (b, 0))
    return _call(
        body, f"mem_attn_bwd_{tag}", (bl, nq),
        [tok, pl.BlockSpec((None, 1, GW), lambda b, i: (layer, 0, 0)), kv, kv, tok],
        [tok, kv, kv, pl.BlockSpec((1, GW), lambda b, i: (0, 0))],
        [_sds((bl * s, GW)), _sds((bl * N_MEM, GW)), _sds((bl * N_MEM, GW)), _sds((1, GW))],
    )(pe, qw, k, v, do)


def _gate_out_fn(outs, gates, wparts):
    y = None
    for o, g, w in zip(outs, gates, wparts):
        part = mm(o * _silu(g), w)
        y = part if y is None else y + part
    return y


def outproj_fwd(x2, outs, pg, wout, layer, tag):
    t = x2.shape[0]
    TQ = _token_tile(t)

    def body(x_ref, oa, ob, oc, od, oe, g_ref, w_ref, y_ref):
        outs_ = [r[...] for r in (oa, ob, oc, od, oe)]
        gates = [g_ref[:, j * GW:(j + 1) * GW] for j in range(5)]
        wparts = [w_ref[j * GW:(j + 1) * GW, :] for j in range(5)]
        y_ref[...] = x_ref[...] + _gate_out_fn(outs_, gates, wparts)

    tok = pl.BlockSpec((TQ, GW), lambda i: (i, 0))
    big = pl.BlockSpec((TQ, D_MODEL), lambda i: (i, 0))
    return _call(
        body, f"outproj_fwd_{tag}", (t // TQ,),
        [big] + [tok] * 5 + [pl.BlockSpec((TQ, D_MIX), lambda i: (i, 0)),
                            pl.BlockSpec((None, D_MIX, D_MODEL), lambda i: (layer, 0, 0))],
        big, _sds((t, D_MODEL)),
    )(x2, *outs, pg, wout)


def outproj_bwd(outs, pg, wout, dy, layer, tag):
    t = dy.shape[0]
    TQ = _token_tile(t)

    def body(oa, ob, oc, od, oe, g_ref, w_ref, dy_ref, da, db, dc, dd, de, dg_ref, dw_ref):
        outs_ = [r[...] for r in (oa, ob, oc, od, oe)]
        gates = [g_ref[:, j * GW:(j + 1) * GW] for j in range(5)]
        wparts = [w_ref[j * GW:(j + 1) * GW, :].astype(F32) for j in range(5)]
        _, vjp = jax.vjp(_gate_out_fn, outs_, gates, wparts)
        douts, dgates, dws = vjp(dy_ref[...])
        for r, val in zip((da, db, dc, dd, de), douts):
            r[...] = val
        first = pl.program_id(0) == 0
        for j in range(5):
            dg_ref[:, j * GW:(j + 1) * GW] = dgates[j]

        @pl.when(first)
        def _():
            for j in range(5):
                dw_ref[j * GW:(j + 1) * GW, :] = dws[j]

        @pl.when(jnp.logical_not(first))
        def _():
            for j in range(5):
                dw_ref[j * GW:(j + 1) * GW, :] += dws[j]

    tok = pl.BlockSpec((TQ, GW), lambda i: (i, 0))
    return _call(
        body, f"outproj_bwd_{tag}", (t // TQ,),
        [tok] * 5 + [pl.BlockSpec((TQ, D_MIX), lambda i: (i, 0)), pl.BlockSpec((None, D_MIX, D_MODEL), lambda i: (layer, 0, 0)),
                     pl.BlockSpec((TQ, D_MODEL), lambda i: (i, 0))],
        [tok] * 5 + [pl.BlockSpec((TQ, D_MIX), lambda i: (i, 0)), pl.BlockSpec((D_MIX, D_MODEL), lambda i: (0, 0))],
        [_sds((t, GW))] * 5 + [_sds((t, D_MIX)), _sds((D_MIX, D_MODEL))],
    )(*outs, pg, wout, dy)


def loss_head(y, tgt):
    t = y.shape[0]
    TQ = _token_tile(t)

    def body(y_ref, t_ref, l_ref, dy_ref):
        diff = y_ref[...] - t_ref[...]
        dy_ref[...] = diff * (1.0 / D_MODEL)
        part = 0.5 * jnp.sum(jnp.sum(diff * diff, axis=-1, keepdims=True) * (1.0 / D_MODEL), axis=0, keepdims=True)
        _acc(l_ref, jnp.broadcast_to(part, (8, 128)), pl.program_id(0) == 0)

    big = pl.BlockSpec((TQ, D_MODEL), lambda i: (i, 0))
    return _call(body, "loss_head", (t // TQ,), [big, big], [pl.BlockSpec((8, 128), lambda i: (0, 0)), big],
                 [_sds((8, 128)), _sds((t, D_MODEL))])(y, tgt)


def layer_fwd(x2, mem2, p, layer, bl, s):
    tag = f"l{layer}"
    ht, pa, pb, pc, pd, pe, pg, pf = inproj_fwd(x2, p["norm_g"], p["w_in"], layer, tag)
    qn, kn, vb, cq, ck = fox_prep_fwd(pa, pf, p["fox_q_norm"], p["fox_k_norm"], p["fox_f_bias"], bl, s, layer, tag)
    oa, lse = fox_attn_fwd(qn, kn, vb, cq, ck, bl, s, tag)
    ob = sb_attn_fwd(pb, bl, s, tag)
    oc, states = hgrn_fwd(pc, p["lb"], p["hgrn_out_norm"], bl, s, layer, tag)
    od = pool_fwd(pd, p["pool_wbd"], p["pool_scale"], bl, s, layer, tag)
    mk, mv = mem_prep_fwd(mem2, p["mem_norm_g"], p["mem_w_kv"], p["mem_k_norm"], bl, layer, tag)
    oe = mem_attn_fwd(pe, p["mem_q_norm"], mk, mv, bl, s, layer, tag)
    y = outproj_fwd(x2, (oa, ob, oc, od, oe), pg, p["w_out"], layer, tag)
    saved = dict(x2=x2, ht=ht, pa=pa, pb=pb, pc=pc, pd=pd, pe=pe, pg=pg, pf=pf, qn=qn, kn=kn, vb=vb, cq=cq, ck=ck,
                 oa=oa, lse=lse, ob=ob, oc=oc, states=states, od=od, mk=mk, mv=mv, oe=oe)
    return y, saved


def layer_bwd(dy, mem2, p, sv, layer, bl, s):
    tag = f"l{layer}"
    (doa, dob, doc, dod, doe, dg_gates, dwout) = outproj_bwd((sv["oa"], sv["ob"], sv["oc"], sv["od"], sv["oe"]), sv["pg"],
                                                              p["w_out"], dy, layer, tag)
    dqn, dkn, dv, dck = fox_attn_bwd(sv["qn"], sv["kn"], sv["vb"], sv["cq"], sv["ck"], sv["lse"], doa, bl, s, tag)
    d_a, d_f, dqw, dkw, dbias = fox_prep_bwd(sv["pa"], sv["pf"], p["fox_q_norm"], p["fox_k_norm"], p["fox_f_bias"], sv["cq"],
                                             dqn, dkn, dv, dck, bl, s, layer, tag)
    dsq, dsk, dsv = sb_attn_bwd(sv["pb"], dob, bl, s, tag)
    d_c, dlb, dwn = hgrn_bwd(sv["pc"], p["lb"], p["hgrn_out_norm"], sv["states"], doc, bl, s, layer, tag)
    d_d, dwbd, dpscale = pool_bwd(sv["pd"], p["pool_wbd"], p["pool_scale"], dod, bl, s, layer, tag)
    d_e, dmk, dmv, dmqw = mem_attn_bwd(sv["pe"], p["mem_q_norm"], sv["mk"], sv["mv"], doe, bl, s, layer, tag)
    dwk, dwv, dmg, dmkw = mem_prep_bwd(mem2, p["mem_norm_g"], p["mem_w_kv"], p["mem_k_norm"], dmk, dmv, bl, layer, tag)
    dpieces = (d_a, dsq, dsk, dsv, d_c, d_d, d_e, dg_gates, d_f)
    dx, dng = inproj_bwd_dx(sv["x2"], p["norm_g"], p["w_in"], dy, dpieces, layer, tag)
    dwin = jnp.concatenate([matmul_acc(sv["ht"], dp, f"{tag}_{nm}") for (nm, _, _), dp in zip(BWD_PIECES, dpieces)], axis=1)
    grads = dict(norm_g=dng, w_in=dwin, fox_f_bias=dbias, fox_q_norm=dqw, fox_k_norm=dkw, lb=dlb, hgrn_out_norm=dwn,
                 pool_wbd=dwbd, pool_scale=dpscale, mem_norm_g=dmg, mem_w_kv=jnp.concatenate([dwk, dwv], axis=1),
                 mem_q_norm=dmqw, mem_k_norm=dmkw, w_out=dwout)
    return dx, grads


def _tile4(w):
    return jnp.tile(w, (1, NH))[:, None, :]


def prepare_params(norm_g, w_in_p, fox_f_bias, fox_q_norm, fox_k_norm, hgrn_lb_logits, hgrn_out_norm, pool_w, pool_scale,
                   mem_norm_g, mem_w_kv, mem_q_norm, mem_k_norm, w_out):
    p1 = jax.nn.sigmoid(hgrn_lb_logits[1] - hgrn_lb_logits[0])
    lb = jnp.stack([jnp.zeros_like(p1), jnp.clip(p1, 0.0, 1.0 - 1e-6)])
    eye = jnp.eye(4, dtype=F32)
    wbd = jnp.einsum("lgcd,gh->lgchd", pool_w, eye).reshape(2, GW, GW)
    return dict(norm_g=norm_g[:, None, :], w_in=w_in_p, fox_f_bias=jnp.pad(fox_f_bias, ((0, 0), (0, 124)))[:, None, :],
                fox_q_norm=_tile4(fox_q_norm), fox_k_norm=_tile4(fox_k_norm), lb=lb[:, None, :],
                hgrn_out_norm=hgrn_out_norm[:, None, :], pool_wbd=wbd, pool_scale=pool_scale[:, None, :],
                mem_norm_g=mem_norm_g[:, None, :], mem_w_kv=mem_w_kv, mem_q_norm=_tile4(mem_q_norm),
                mem_k_norm=_tile4(mem_k_norm), w_out=w_out)


def local_step(x, mem, tgt, p):
    bl, s, _ = x.shape
    x2, mem2, tgt2 = x.reshape(bl * s, D_MODEL), mem.reshape(bl * N_MEM, D_MODEL), tgt.reshape(bl * s, D_MODEL)
    y0, sv0 = layer_fwd(x2, mem2, p, 0, bl, s)
    y1, sv1 = layer_fwd(y0, mem2, p, 1, bl, s)
    lpart, dy = loss_head(y1, tgt2)
    dx1, g1 = layer_bwd(dy, mem2, p, sv1, 1, bl, s)
    dx0, g0 = layer_bwd(dx1, mem2, p, sv0, 0, bl, s)
    return lpart[0, 0], dx0.reshape(bl, s, D_MODEL), g0, g1


_ANY = pl.BlockSpec(memory_space=pl.ANY)


def all_gather_rows(xss, tag):
    n = len(xss)

    def body(*refs):
        x_refs, o_refs, (send_sems, recv_sems, local_sems) = refs[:n], refs[n:2 * n], refs[2 * n:]
        x, y, cc = lax.axis_index("x"), lax.axis_index("y"), lax.axis_index("c")
        me, sibling = (x, y, cc), (x, y, 1 - cc)
        chips = [(1 - x, y), (x, 1 - y), (1 - x, 1 - y)]

        def rows(a, px, py, pc):
            r = xss[a].shape[1]
            return o_refs[a].at[:, pl.ds((4 * px + 2 * py + pc) * r, r), :]

        def copy(a, k, block, to, src=None):
            return pltpu.make_async_remote_copy(src_ref=rows(a, *block) if src is None else src, dst_ref=rows(a, *block),
                                                send_sem=send_sems.at[7 * a + k], recv_sem=recv_sems.at[7 * a + k], device_id=to,
                                                device_id_type=pl.DeviceIdType.MESH)

        mine = [pltpu.make_async_copy(x_refs[a], rows(a, *me), local_sems.at[a]) for a in range(n)]
        first = []
        for a in range(n):
            first += [copy(a, 0, me, sibling, src=x_refs[a])] + [copy(a, 1 + j, me, (*chip, cc), src=x_refs[a])
                                                                 for j, chip in enumerate(chips)]
        for cp in mine + first:
            cp.start()
        passed = []
        for j, chip in enumerate(chips):
            for a in range(n):
                copy(a, 1 + j, (*chip, cc), me).wait_recv()
                passed.append(copy(a, 4 + j, (*chip, cc), sibling))
                passed[-1].start()
        for a in range(n):
            copy(a, 0, sibling, me).wait_recv()
        for j, chip in enumerate(chips):
            for a in range(n):
                copy(a, 4 + j, (*chip, 1 - cc), me).wait_recv()
        for cp in first + passed:
            cp.wait_send()
        for cp in mine:
            cp.wait()

    nsem = pltpu.SemaphoreType.DMA((7 * n,))
    return pl.pallas_call(
        body, name=f"all_gather_{tag}", in_specs=[_ANY] * n, out_specs=[_ANY] * n,
        out_shape=[_sds((xs.shape[0], N_DEV * xs.shape[1], xs.shape[2]), xs.dtype) for xs in xss],
        scratch_shapes=[nsem, nsem, pltpu.SemaphoreType.DMA((n,))],
    )(*xss)


def exchange_cores(parts, tag):
    n = len(parts)
    counts = [p.shape[0] * 4 for p in parts]

    def body(*refs):
        p_refs, t_refs, (send_sems, recv_sems) = refs[:n], refs[n:2 * n], refs[2 * n:]
        x, y, cc = lax.axis_index("x"), lax.axis_index("y"), lax.axis_index("c")
        copies, k = [], 0
        for a in range(n):
            for l in range(parts[a].shape[0]):
                for q in range(4):
                    copies.append(pltpu.make_async_remote_copy(
                        src_ref=p_refs[a].at[l, q, pl.ds(1 - cc, 1)], dst_ref=t_refs[a].at[l, q], send_sem=send_sems.at[k],
                        recv_sem=recv_sems.at[k], device_id=(x, y, 1 - cc), device_id_type=pl.DeviceIdType.MESH))
                    k += 1
        for cp in copies:
            cp.start()
        for cp in copies:
            cp.wait()

    nsem = pltpu.SemaphoreType.DMA((sum(counts),))
    return pl.pallas_call(
        body, name=f"exchange_cores_{tag}", in_specs=[_ANY] * n, out_specs=[_ANY] * n,
        out_shape=[_sds((p.shape[0], 4, 1, p.shape[3], p.shape[4]), p.dtype) for p in parts],
        scratch_shapes=[nsem, nsem],
    )(*parts)


def add_core_halves(part5, theirs, core, tag):
    nl, _, _, r, c = part5.shape
    tr = r if r * c <= 256 * 1024 else 64

    def body(core_ref, a_ref, b_ref, o_ref):
        o_ref[...] = _cast(a_ref[...] + b_ref[...])

    blk = lambda which: pl.BlockSpec((None, None, None, tr, c), lambda l, q, i, cref: (l, q, cref[0] if which else 0, i, 0))
    return pl.pallas_call(
        body, name=f"add_core_halves_{tag}", out_shape=_sds((nl, 4, 1, r, c), _MMT),
        grid_spec=pltpu.PrefetchScalarGridSpec(num_scalar_prefetch=1, grid=(nl, 4, r // tr), in_specs=[blk(True), blk(False)],
                                               out_specs=blk(False)),
        compiler_params=pltpu.CompilerParams(dimension_semantics=("arbitrary",) * 3, vmem_limit_bytes=VMEM_LIMIT_BYTES),
    )(core, part5, theirs)


def exchange_chips(s4s, tag):
    n = len(s4s)

    def body(*refs):
        s_refs, o_refs, (send_sems, recv_sems, local_sems) = refs[:n], refs[n:2 * n], refs[2 * n:]
        x, y, cc = lax.axis_index("x"), lax.axis_index("y"), lax.axis_index("c")
        copies = [pltpu.make_async_copy(s_refs[a].at[:, pl.ds(2 * x + y, 1)], o_refs[a].at[0], local_sems.at[a]) for a in range(n)]
        for k in range(1, 4):
            px = 1 - x if (k >> 1) & 1 else x
            py = 1 - y if k & 1 else y
            for a in range(n):
                copies.append(pltpu.make_async_remote_copy(
                    src_ref=s_refs[a].at[:, pl.ds(2 * px + py, 1)], dst_ref=o_refs[a].at[k], send_sem=send_sems.at[3 * a + k - 1],
                    recv_sem=recv_sems.at[3 * a + k - 1], device_id=(px, py, cc), device_id_type=pl.DeviceIdType.MESH))
        for cp in copies:
            cp.start()
        for cp in copies:
            cp.wait()

    nsem = pltpu.SemaphoreType.DMA((3 * n,))
    return pl.pallas_call(
        body, name=f"exchange_chips_{tag}", in_specs=[_ANY] * n, out_specs=[_ANY] * n,
        out_shape=[_sds((4, s.shape[0], 1, 1, s.shape[3], s.shape[4]), s.dtype) for s in s4s],
        scratch_shapes=[nsem, nsem, pltpu.SemaphoreType.DMA((n,))],
    )(*s4s)


def _row_tile(rows):
    if rows <= 512 and rows % 64:
        return rows
    for t in (64, 40, 32, 16, 8):
        if rows % t == 0:
            return t
    return rows


def sum_slots(slots, tag):
    ns, rows, c = slots.shape
    tr = _row_tile(rows)

    def body(s_ref, o_ref):
        acc = s_ref[0].astype(F32)
        for k in range(1, ns):
            acc = acc + s_ref[k].astype(F32)
        o_ref[...] = acc

    return _call(body, f"sum_slots_{tag}", (rows // tr,), [pl.BlockSpec((ns, tr, c), lambda i: (0, i, 0))],
                 pl.BlockSpec((tr, c), lambda i: (i, 0)), _sds((rows, c)))(slots)


def _adamw(w, g, m, v):
    m = ADAM_B1 * m + (1.0 - ADAM_B1) * g
    v = ADAM_B2 * v + (1.0 - ADAM_B2) * (g * g)
    m_hat = m / (1.0 - ADAM_B1 ** ADAM_STEP)
    v_hat = v / (1.0 - ADAM_B2 ** ADAM_STEP)
    delta = -ADAM_LR * (m_hat / (jnp.sqrt(v_hat) + ADAM_EPS) + ADAM_WD * w)
    return delta, m, v


def adam_update(w, m, v, g, tag, slots=False):
    rows, c = w.shape
    tr = _row_tile(rows)
    ns = g.shape[0] if slots else 0

    def body(w_ref, m_ref, v_ref, g_ref, go_ref, d_ref, mo_ref, vo_ref):
        if slots:
            g = g_ref[0].astype(F32)
            for k in range(1, ns):
                g = g + g_ref[k].astype(F32)
        else:
            g = g_ref[...]
        d, mn, vn = _adamw(w_ref[...], g, m_ref[...], v_ref[...])
        go_ref[...] = g
        d_ref[...] = d
        mo_ref[...] = mn
        vo_ref[...] = vn

    blk = pl.BlockSpec((tr, c), lambda i: (i, 0))
    gspec = pl.BlockSpec((ns, tr, c), lambda i: (0, i, 0)) if slots else blk
    return _call(body, f"adam_{tag}", (rows // tr,), [blk, blk, blk, gspec], [blk] * 4, [_sds((rows, c))] * 4)(w, m, v, g)


_SMALL = (("norm_g", (2, 1024)), ("fox_f_bias", (2, 4)), ("fox_q_norm", (2, 64)), ("fox_k_norm", (2, 64)),
          ("hgrn_lb_logits", (2, 256)), ("hgrn_out_norm", (2, 256)), ("pool_w", (2, 4, 64, 64)), ("pool_scale", (2, 256)),
          ("mem_norm_g", (2, 1024)), ("mem_q_norm", (2, 64)), ("mem_k_norm", (2, 64)))
_SLAB_ROWS = 312


def pack_small(d):
    flat = jnp.concatenate([d[n].reshape(-1) for n, _ in _SMALL])
    return jnp.pad(flat, (0, _SLAB_ROWS * 128 - flat.shape[0])).reshape(_SLAB_ROWS, 128)


def unpack_small(slab):
    flat, out, off = slab.reshape(-1), {}, 0
    for n, shp in _SMALL:
        size = 1
        for e in shp:
            size *= e
        out[n] = flat[off:off + size].reshape(shp)
        off += size
    return out


def small_grads(g0, g1, lb_logits):
    st = lambda f: jnp.stack([f(g0), f(g1)])
    heads = lambda a: a.reshape(NH, HD).sum(0)
    p1 = jax.nn.sigmoid(lb_logits[1] - lb_logits[0])
    inside = (p1 > 0.0) & (p1 < 1.0 - 1e-6)
    dl1 = jnp.where(inside, g1["lb"][0] * p1 * (1.0 - p1), 0.0)
    diag = lambda a: jnp.stack([a.reshape(4, HD, 4, HD)[i, :, i, :] for i in range(4)])
    return dict(norm_g=st(lambda g: g["norm_g"][0]), fox_f_bias=st(lambda g: g["fox_f_bias"][0, :NH]),
                fox_q_norm=st(lambda g: heads(g["fox_q_norm"])), fox_k_norm=st(lambda g: heads(g["fox_k_norm"])),
                hgrn_lb_logits=jnp.stack([-dl1, dl1]), hgrn_out_norm=st(lambda g: g["hgrn_out_norm"][0]),
                pool_w=st(lambda g: diag(g["pool_wbd"])), pool_scale=st(lambda g: g["pool_scale"][0]),
                mem_norm_g=st(lambda g: g["mem_norm_g"][0]), mem_q_norm=st(lambda g: heads(g["mem_q_norm"])),
                mem_k_norm=st(lambda g: heads(g["mem_k_norm"])))


def kernel(x, mem, norm_g, w_in, fox_f_bias, fox_q_norm, fox_k_norm, hgrn_lb_logits, hgrn_out_norm, pool_w, pool_scale, mem_norm_g, mem_w_kv, mem_q_norm, mem_k_norm, w_out, loss_target, m_norm_g, m_w_in, m_fox_f_bias, m_fox_q_norm, m_fox_k_norm, m_hgrn_lb_logits, m_hgrn_out_norm, m_pool_w, m_pool_scale, m_mem_norm_g, m_mem_w_kv, m_mem_q_norm, m_mem_k_norm, m_w_out, v_norm_g, v_w_in, v_fox_f_bias, v_fox_q_norm, v_fox_k_norm, v_hgrn_lb_logits, v_hgrn_out_norm, v_pool_w, v_pool_scale, v_mem_norm_g, v_mem_w_kv, v_mem_q_norm, v_mem_k_norm, v_w_out):
    given = dict(norm_g=(norm_g, m_norm_g, v_norm_g), w_in=(w_in, m_w_in, v_w_in), fox_f_bias=(fox_f_bias, m_fox_f_bias, v_fox_f_bias),
                 fox_q_norm=(fox_q_norm, m_fox_q_norm, v_fox_q_norm), fox_k_norm=(fox_k_norm, m_fox_k_norm, v_fox_k_norm),
                 hgrn_lb_logits=(hgrn_lb_logits, m_hgrn_lb_logits, v_hgrn_lb_logits),
                 hgrn_out_norm=(hgrn_out_norm, m_hgrn_out_norm, v_hgrn_out_norm), pool_w=(pool_w, m_pool_w, v_pool_w),
                 pool_scale=(pool_scale, m_pool_scale, v_pool_scale), mem_norm_g=(mem_norm_g, m_mem_norm_g, v_mem_norm_g),
                 mem_w_kv=(mem_w_kv, m_mem_w_kv, v_mem_w_kv), mem_q_norm=(mem_q_norm, m_mem_q_norm, v_mem_q_norm),
                 mem_k_norm=(mem_k_norm, m_mem_k_norm, v_mem_k_norm), w_out=(w_out, m_w_out, v_w_out))
    order = ("norm_g", "w_in", "fox_f_bias", "fox_q_norm", "fox_k_norm", "hgrn_lb_logits", "hgrn_out_norm", "pool_w",
             "pool_scale", "mem_norm_g", "mem_w_kv", "mem_q_norm", "mem_k_norm", "w_out")

    w_in_full, w_out_full, w_kv_full = all_gather_rows([_cast(permute_cols(w_in)), _cast(w_out), _cast(mem_w_kv)], "weights")
    p = prepare_params(norm_g, w_in_full, fox_f_bias, fox_q_norm, fox_k_norm, hgrn_lb_logits, hgrn_out_norm, pool_w,
                       pool_scale, mem_norm_g, w_kv_full, mem_q_norm, mem_k_norm, w_out_full)

    loss_part, grad_x, g0, g1 = local_step(x, mem, loss_target, p)
    loss = lax.psum(loss_part, ("x", "y", "c"))

    res = {}
    core = lax.axis_index("c").astype(jnp.int32).reshape(1)

    names = ("w_in", "w_out", "mem_w_kv")
    part5 = []
    for name in names:
        nl, r, _ = given[name][0].shape
        g2 = jnp.stack([g0[name], g1[name]])
        part5.append(g2.reshape(nl, 4, 2, r, g2.shape[-1]))
    s4 = [add_core_halves(p5, th, core, name) for name, p5, th in zip(names, part5, exchange_cores(part5, "grads"))]
    for name, sl in zip(names, exchange_chips(s4, "grads")):
        w, m, v = given[name]
        nl, r, c = w.shape
        slots = sl.reshape(4, nl * r, sl.shape[-1])
        if name == "w_in":
            g = sum_slots(slots, name).reshape(nl, r, -1)
            out = adam_update(w.reshape(nl * r, c), m.reshape(nl * r, c), v.reshape(nl * r, c),
                              unpermute_cols(g).reshape(nl * r, c), name)
        else:
            out = adam_update(w.reshape(nl * r, c), m.reshape(nl * r, c), v.reshape(nl * r, c), slots, name, slots=True)
        res[name] = tuple(o.reshape(nl, r, c) for o in out)

    gsmall = pack_small(small_grads(g0, g1, hgrn_lb_logits))
    gathered = all_gather_rows([gsmall[None]], "small")[0].reshape(N_DEV, _SLAB_ROWS, 128)
    slabs = adam_update(*[pack_small({n: given[n][j] for n, _ in _SMALL}) for j in range(3)], gathered, "small", slots=True)
    small = [unpack_small(sl) for sl in slabs]
    for n, _ in _SMALL:
        res[n] = tuple(small[j][n] for j in range(4))

    return (loss, grad_x, *[res[n][0] for n in order], *[res[n][1] for n in order], *[res[n][2] for n in order],
            *[res[n][3] for n in order])
```

```python
import functools

import jax
import jax.numpy as jnp
from jax import lax
from jax.experimental import pallas as pl
from jax.experimental.pallas import tpu as pltpu

F32 = jnp.float32
BF = jnp.bfloat16
_MMT = BF

D_MODEL = 1024
GW = 256
HD = 64
NH = 4
CH = 64
N_MEM = 256
D_IN = 4100
D_INP = 4224
D_MIX = 1280
EPS = 1e-6
NEG_BIG = -1e30
LB_FLOOR = 1e-30
SCALE = HD ** -0.5
TQ = 256
TM = 512


def _token_tile(n):
    return TM if n % TM == 0 else TQ
N_DEV = 8
VMEM_LIMIT_BYTES = 56 * 1024 * 1024

ADAM_LR = 0.001
ADAM_B1 = 0.9
ADAM_B2 = 0.999
ADAM_EPS = 1e-08
ADAM_WD = 0.01
ADAM_STEP = 10

PIECES = (("A", 0, 768), ("B", 768, 768), ("C", 1536, 768), ("D", 2304, 256), ("E", 2560, 256),
          ("G", 2816, 1280), ("F", 4096, 128))
BWD_PIECES = (("A", 0, 768), ("Bq", 768, 256), ("Bk", 1024, 256), ("Bv", 1280, 256), ("C", 1536, 768),
              ("D", 2304, 256), ("E", 2560, 256), ("G", 2816, 1280), ("F", 4096, 128))
_ORIG = dict(fq=(0, 256), fk=(256, 512), fv=(512, 768), fg=(768, 1024), ff=(1024, 1028), sq=(1028, 1284),
             sk=(1284, 1540), sv=(1540, 1796), sg=(1796, 2052), hq=(2052, 2308), hf=(2308, 2564),
             hi=(2564, 2820), hg=(2820, 3076), pv=(3076, 3332), pg=(3332, 3588), mq=(3588, 3844), mg=(3844, 4100))
_PERM_ORDER = ("fq", "fk", "fv", "sq", "sk", "sv", "hq", "hf", "hi", "pv", "mq", "fg", "sg", "hg", "pg", "mg", "ff")
_ORIG_ORDER = ("fq", "fk", "fv", "fg", "ff", "sq", "sk", "sv", "sg", "hq", "hf", "hi", "hg", "pv", "pg", "mq", "mg")


def permute_cols(w):
    parts = [w[..., _ORIG[n][0]:_ORIG[n][1]] for n in _PERM_ORDER]
    parts.append(jnp.zeros(w.shape[:-1] + (D_INP - D_IN,), w.dtype))
    return jnp.concatenate(parts, axis=-1)


def unpermute_cols(g):
    start, off = {}, 0
    for n in _PERM_ORDER:
        start[n] = off
        off += _ORIG[n][1] - _ORIG[n][0]
    return jnp.concatenate([g[..., start[n]:start[n] + _ORIG[n][1] - _ORIG[n][0]] for n in _ORIG_ORDER], axis=-1)


def _cast(a):
    return a.astype(_MMT)


def _dg(a, b, ca, cb):
    return lax.dot_general(a, b, (((ca,), (cb,)), ((), ())), preferred_element_type=F32)


@jax.custom_vjp
def mm(a, b):
    return _dg(_cast(a), _cast(b), 1, 0)


@jax.custom_vjp
def mm_nt(a, b):
    return _dg(_cast(a), _cast(b), 1, 1)


@jax.custom_vjp
def mm_tn(a, b):
    return _dg(_cast(a), _cast(b), 0, 0)


mm.defvjp(lambda a, b: (mm(a, b), (a, b)),
          lambda r, g: (mm_nt(g, r[1]).astype(r[0].dtype), mm_tn(r[0], g).astype(r[1].dtype)))
mm_nt.defvjp(lambda a, b: (mm_nt(a, b), (a, b)),
             lambda r, g: (mm(g, r[1]).astype(r[0].dtype), mm_tn(g, r[0]).astype(r[1].dtype)))
mm_tn.defvjp(lambda a, b: (mm_tn(a, b), (a, b)),
             lambda r, g: (mm_nt(r[1], g).astype(r[0].dtype), mm(r[0], g).astype(r[1].dtype)))


def _split(a):
    hi = a.astype(_MMT)
    lo = (a - hi.astype(F32)).astype(_MMT)
    return hi, lo


@jax.custom_vjp
def xr(a, c, ct):
    hi, lo = _split(a)
    cc = _cast(c)
    return _dg(hi, cc, 1, 0) + _dg(lo, cc, 1, 0)


@jax.custom_vjp
def xl(c, ct, a):
    hi, lo = _split(a)
    cc = _cast(c)
    return _dg(cc, hi, 1, 0) + _dg(cc, lo, 1, 0)


xr.defvjp(lambda a, c, ct: (xr(a, c, ct), (c, ct)),
          lambda r, g: (xr(g, r[1], r[0]), jnp.zeros_like(r[0]), jnp.zeros_like(r[1])))
xl.defvjp(lambda c, ct, a: (xl(c, ct, a), (c, ct)),
          lambda r, g: (jnp.zeros_like(r[0]), jnp.zeros_like(r[1]), xl(r[1], r[0], g)))


def _iota(shape, dim):
    return lax.broadcasted_iota(jnp.int32, shape, dim)


def _hmask(h, n=GW):
    lane = _iota((1, n), 1)
    return ((lane >= h * HD) & (lane < (h + 1) * HD)).astype(F32)


def _bdmask(n=GW):
    return ((_iota((n, n), 0) >> 6) == (_iota((n, n), 1) >> 6)).astype(F32)


def _tri(n, kind="le"):
    r, c = _iota((n, n), 0), _iota((n, n), 1)
    return {"le": c <= r, "ge": c >= r, "gt": c > r, "lt": c < r}[kind].astype(F32)


def _onehot_lane(h, n=128):
    return (_iota((1, n), 1) == h).astype(F32)


def _logsig(x):
    return jnp.minimum(x, 0.0) - jnp.log1p(jnp.exp(-jnp.abs(x)))


def _sigmoid(x):
    return 0.5 * (jnp.tanh(0.5 * x) + 1.0)


def _silu(x):
    return x * _sigmoid(x)


def _rms(x, g):
    return x * lax.rsqrt(jnp.mean(x * x, axis=-1, keepdims=True) + EPS) * g


def _headrms(x, w, bd64):
    ms = xr(x * x, bd64, bd64)
    return x * lax.rsqrt(ms + EPS) * w


def _call(body, name, grid, in_specs, out_specs, out_shape, scratch=()):
    return pl.pallas_call(
        body, name=name, grid=grid, in_specs=in_specs, out_specs=out_specs, out_shape=out_shape,
        scratch_shapes=list(scratch),
        compiler_params=pltpu.CompilerParams(dimension_semantics=("arbitrary",) * len(grid),
                                             vmem_limit_bytes=VMEM_LIMIT_BYTES))


def _sds(shape, dtype=F32):
    return jax.ShapeDtypeStruct(shape, dtype)


def _acc(ref, val, first):
    @pl.when(first)
    def _():
        ref[...] = val

    @pl.when(jnp.logical_not(first))
    def _():
        ref[...] += val


def inproj_fwd(x2, g, w, layer, tag):
    t = x2.shape[0]
    TQ = _token_tile(t)

    def body(x_ref, g_ref, w_ref, ht_ref, *outs):
        h = _rms(x_ref[...], g_ref[...])
        hb = _cast(h)
        ht_ref[...] = _cast(h.T)
        for (_, c0, wd), o in zip(PIECES, outs):
            o[...] = _dg(hb, _cast(w_ref[:, c0:c0 + wd]), 1, 0)

    return _call(
        body, f"inproj_fwd_{tag}", (t // TQ,),
        [pl.BlockSpec((TQ, D_MODEL), lambda i: (i, 0)),
         pl.BlockSpec((None, 1, D_MODEL), lambda i: (layer, 0, 0)),
         pl.BlockSpec((None, D_MODEL, D_INP), lambda i: (layer, 0, 0))],
        [pl.BlockSpec((D_MODEL, TQ), lambda i: (0, i))] + [pl.BlockSpec((TQ, wd), lambda i: (i, 0)) for _, _, wd in PIECES],
        [_sds((D_MODEL, t), _MMT)] + [_sds((t, wd)) for _, _, wd in PIECES],
    )(x2, g, w)


def inproj_bwd_dx(x2, g, w, dy, dpieces, layer, tag):
    t = x2.shape[0]
    TQ = _token_tile(t)

    def body(x_ref, g_ref, w_ref, dy_ref, *rest):
        dps, (dx_ref, dg_ref) = rest[:len(BWD_PIECES)], rest[len(BWD_PIECES):]
        dh = None
        for (_, c0, wd), dp in zip(BWD_PIECES, dps):
            part = _dg(_cast(dp[...]), _cast(w_ref[:, c0:c0 + wd]), 1, 1)
            dh = part if dh is None else dh + part
        _, vjp = jax.vjp(_rms, x_ref[...], g_ref[...])
        dx, dg = vjp(dh)
        dx_ref[...] = dy_ref[...] + dx
        _acc(dg_ref, dg, pl.program_id(0) == 0)

    return _call(
        body, f"inproj_bwd_dx_{tag}", (t // TQ,),
        [pl.BlockSpec((TQ, D_MODEL), lambda i: (i, 0)),
         pl.BlockSpec((None, 1, D_MODEL), lambda i: (layer, 0, 0)),
         pl.BlockSpec((None, D_MODEL, D_INP), lambda i: (layer, 0, 0), pipeline_mode=pl.Buffered(1)),
         pl.BlockSpec((TQ, D_MODEL), lambda i: (i, 0))] + [pl.BlockSpec((TQ, wd), lambda i: (i, 0)) for _, _, wd in BWD_PIECES],
        [pl.BlockSpec((TQ, D_MODEL), lambda i: (i, 0)), pl.BlockSpec((1, D_MODEL), lambda i: (0, 0))],
        [_sds((t, D_MODEL)), _sds((1, D_MODEL))],
    )(x2, g, w, dy, *dpieces)


def matmul_acc(at, b, tag):
    m, t = at.shape
    n = b.shape[1]
    tn = {1280: 640, 768: 768}.get(n, n)
    tk = 2048 if t % 2048 == 0 else (512 if t % 512 == 0 else TQ)

    def body(a_ref, b_ref, o_ref):
        _acc(o_ref, _dg(_cast(a_ref[...]), _cast(b_ref[...]), 1, 0), pl.program_id(1) == 0)

    return _call(
        body, f"matmul_acc_{tag}", (n // tn, t // tk),
        [pl.BlockSpec((m, tk), lambda j, i: (0, i)), pl.BlockSpec((tk, tn), lambda j, i: (i, j))],
        pl.BlockSpec((m, tn), lambda j, i: (0, j)),
        _sds((m, n)),
    )(at, b)


def _fox_prep_fn(q, k, ff, qw, kw, bias, carry, bd64, tri, trit, last):
    qn = _headrms(q, qw, bd64)
    kn = _headrms(k, kw, bd64)
    lf = _logsig(ff + bias)
    c = xl(tri, trit, lf) + carry
    return qn, kn, c, jnp.sum(c * last, axis=0, keepdims=True)


def _prep_consts(tq):
    return _bdmask() * (1.0 / HD), _tri(tq), _tri(tq, "ge"), (_iota((tq, 1), 0) == tq - 1).astype(F32)


def fox_prep_fwd(pa, pf, qw, kw, bias, bl, s, layer, tag):
    TQ = _token_tile(s)
    nq = s // TQ

    def body(q_ref, k_ref, v_ref, f_ref, qw_ref, kw_ref, b_ref, qn_ref, kn_ref, vb_ref, cq_ref, ck_ref, carry):
        @pl.when(pl.program_id(1) == 0)
        def _():
            carry[...] = jnp.zeros_like(carry)

        qn, kn, c, cl = _fox_prep_fn(q_ref[...], k_ref[...], f_ref[...], qw_ref[...], kw_ref[...], b_ref[...],
                                     carry[...], *_prep_consts(TQ))
        carry[...] = cl
        qn_ref[...] = _cast(qn)
        kn_ref[...] = _cast(kn)
        vb_ref[...] = _cast(v_ref[...])
        cq_ref[...] = c
        ck_ref[...] = c.T[0:8, :]

    tok = lambda j: pl.BlockSpec((TQ, GW), lambda b, i: (b * nq + i, j))
    par = lambda n: pl.BlockSpec((None, 1, n), lambda b, i: (layer, 0, 0))
    return _call(
        body, f"fox_prep_fwd_{tag}", (bl, nq),
        [tok(0), tok(1), tok(2), pl.BlockSpec((TQ, 128), lambda b, i: (b * nq + i, 0)), par(GW), par(GW), par(128)],
        [tok(0), tok(0), tok(0), pl.BlockSpec((TQ, 128), lambda b, i: (b * nq + i, 0)),
         pl.BlockSpec((None, 8, TQ), lambda b, i: (b, 0, i))],
        [_sds((bl * s, GW), _MMT)] * 3 + [_sds((bl * s, 128)), _sds((bl, 8, s))],
        [pltpu.VMEM((1, 128), F32)],
    )(pa, pa, pa, pf, qw, kw, bias)


def fox_prep_bwd(pa, pf, qw, kw, bias, cq, dqn, dkn, dv, dck, bl, s, layer, tag):
    TQ = _token_tile(s)
    nq = s // TQ

    def body(q_ref, k_ref, f_ref, qw_ref, kw_ref, b_ref, cq_ref, cprev_ref, dqn_ref, dkn_ref, dv_ref, dck_ref,
             da_ref, df_ref, dqw_ref, dkw_ref, db_ref, dcarry):
        i = pl.program_id(1)
        first = jnp.logical_and(pl.program_id(0) == 0, i == 0)

        @pl.when(i == 0)
        def _():
            dcarry[...] = jnp.zeros_like(dcarry)

        last = (_iota((TQ, 1), 0) == TQ - 1).astype(F32)
        carry_in = jnp.where(i == nq - 1, 0.0, jnp.sum(cprev_ref[...] * last, axis=0, keepdims=True))
        consts = _prep_consts(TQ)
        _, vjp = jax.vjp(lambda *a: _fox_prep_fn(*a, *consts), q_ref[...], k_ref[...], f_ref[...], qw_ref[...],
                         kw_ref[...], b_ref[...], carry_in)
        dc = dck_ref[...].T
        dq, dk, dff, dqw, dkw, dbias, dcin = vjp((dqn_ref[...], dkn_ref[...], dc, dcarry[...]))
        dcarry[...] = dcin
        da_ref[:, 0:GW] = dq
        da_ref[:, GW:2 * GW] = dk
        da_ref[:, 2 * GW:3 * GW] = dv_ref[...]
        df_ref[...] = dff
        _acc(dqw_ref, dqw, first)
        _acc(dkw_ref, dkw, first)
        _acc(db_ref, dbias, first)

    rv = lambda b, i: b * nq + (nq - 1 - i)
    tok = lambda j: pl.BlockSpec((TQ, GW), lambda b, i: (rv(b, i), j))
    tok0 = pl.BlockSpec((TQ, GW), lambda b, i: (rv(b, i), 0))
    t128 = pl.BlockSpec((TQ, 128), lambda b, i: (rv(b, i), 0))
    prev = pl.BlockSpec((TQ, 128), lambda b, i: (jnp.maximum(rv(b, i) - 1, 0), 0))
    par = lambda n: pl.BlockSpec((None, 1, n), lambda b, i: (layer, 0, 0))
    acc = lambda n: pl.BlockSpec((1, n), lambda b, i: (0, 0))
    return _call(
        body, f"fox_prep_bwd_{tag}", (bl, nq),
        [tok(0), tok(1), t128, par(GW), par(GW), par(128), t128, prev, tok0, tok0, tok0,
         pl.BlockSpec((None, 128, TQ), lambda b, i: (b, 0, nq - 1 - i))],
        [pl.BlockSpec((TQ, 3 * GW), lambda b, i: (rv(b, i), 0)), t128, acc(GW), acc(GW), acc(128)],
        [_sds((bl * s, 3 * GW)), _sds((bl * s, 128)), _sds((1, GW)), _sds((1, GW)), _sds((1, 128))],
        [pltpu.VMEM((1, 128), F32)],
    )(pa, pa, pf, qw, kw, bias, cq, cq, dqn, dkn, dv, dck)


def _lane_pick(x, h):
    return jnp.sum(x * _onehot_lane(h), axis=-1, keepdims=True)


TA_BIG = 256
TK_FOX = 512
TK_SB = 256


def _stack_heads(x, scale=1.0):
    return _cast(jnp.concatenate([x * (_hmask(h) * scale) for h in range(NH)], axis=0))


def _stack_cols(x):
    return jnp.concatenate([_lane_pick(x, h) for h in range(NH)], axis=0)


def _spread_heads(col):
    ta = col.shape[0] // NH
    return sum(col[h * ta:(h + 1) * ta] * _hmask(h) for h in range(NH))


def _lanes_cat(w):
    ta = w.shape[0] // NH
    return jnp.concatenate([w[h * ta:(h + 1) * ta] for h in range(NH)], axis=1)


def _mask_stack(x):
    return _cast(jnp.concatenate([x * _hmask(h).astype(x.dtype) for h in range(NH)], axis=0))


def _stack_rows(i, ta):
    return i * ta + (_iota((NH * ta, 1), 0) & (ta - 1))


def _n_key_tiles(i, tk, ta):
    return lax.shift_right_logical(i * ta, tk.bit_length() - 1) + 1


def fox_attn_fwd(qn, kn, vb, cq, ck, bl, s, tag):
    TA, TK = min(TA_BIG, s), min(TK_FOX, s)
    nq, SROWS = s // TA, NH * TA

    def body(q_ref, k_ref, v_ref, cq_ref, ck_ref, o_ref, lse_ref, acc, vst):
        i = pl.program_id(1)

        @pl.when(i == 0)
        def _():
            _fill_stacked(vst, v_ref, s, TK)

        qs = _stack_heads(q_ref[...].astype(F32), SCALE)
        cqs = _stack_cols(cq_ref[...])
        row = _stack_rows(i, TA)
        acc[...] = jnp.zeros_like(acc)

        def step(j, ml):
            m, l = ml
            ks = pl.ds(pl.multiple_of(j * TK, TK), TK)
            ckb = jnp.concatenate([jnp.broadcast_to(ck_ref[h:h + 1, ks], (TA, TK)) for h in range(NH)], axis=0)
            sc = _dg(qs, k_ref[ks, :], 1, 1) + cqs - ckb
            col = j * TK + _iota((1, TK), 1)
            sc = jnp.where(col <= row, sc, NEG_BIG)
            m_new = jnp.maximum(m, jnp.max(sc, axis=-1, keepdims=True))
            alpha = jnp.exp(m - m_new)
            p = jnp.exp(sc - m_new)
            vs = vst[pl.ds(pl.multiple_of(j * NH * TK, NH * TK), NH * TK), :]
            acc[...] = _spread_heads(alpha) * acc[...] + _dg(_lanes_cat(_cast(p)), vs, 1, 0)
            return m_new, alpha * l + jnp.sum(p, axis=-1, keepdims=True)

        m, l = lax.fori_loop(0, _n_key_tiles(i, TK, TA), step, (jnp.full((SROWS, 1), NEG_BIG, F32), jnp.zeros((SROWS, 1), F32)))
        o_ref[...] = acc[...] / _spread_heads(l)
        lse_h = m + jnp.log(l)
        lse_ref[...] = sum(lse_h[h * TA:(h + 1) * TA] * _onehot_lane(h) for h in range(NH))

    tok = pl.BlockSpec((TA, GW), lambda b, i: (b * nq + i, 0))
    seq = pl.BlockSpec((s, GW), lambda b, i: (b, 0))
    t128 = pl.BlockSpec((TA, 128), lambda b, i: (b * nq + i, 0))
    return _call(
        body, f"fox_attn_fwd_{tag}", (bl, nq),
        [tok, seq, seq, t128, pl.BlockSpec((None, 8, s), lambda b, i: (b, 0, 0))],
        [tok, t128], [_sds((bl * s, GW)), _sds((bl * s, 128))],
        [pltpu.VMEM((TA, GW), F32), pltpu.VMEM((NH * s, GW), _MMT)],
    )(qn, kn, vb, cq, ck)


def fox_attn_bwd(qn, kn, vb, cq, ck, lse, do, bl, s, tag):
    TA, TK = min(TA_BIG, s), min(TK_FOX, s)
    nq, SROWS = s // TA, NH * TA

    def body(q_ref, k_ref, v_ref, cq_ref, ck_ref, lse_ref, do_ref, dq_ref, dk_ref, dv_ref, dck_ref, dqa, p_s, dp_s, kst):
        i = pl.program_id(1)

        @pl.when(i == 0)
        def _():
            dk_ref[...] = jnp.zeros_like(dk_ref)
            dv_ref[...] = jnp.zeros_like(dv_ref)
            dck_ref[...] = jnp.zeros_like(dck_ref)
            _fill_stacked(kst, k_ref, s, TK)

        qs = _stack_heads(q_ref[...].astype(F32), SCALE)
        dos = _stack_heads(do_ref[...])
        cqs, lses = _stack_cols(cq_ref[...]), _stack_cols(lse_ref[...])
        row = _stack_rows(i, TA)
        dqa[...] = jnp.zeros_like(dqa)
        nk = _n_key_tiles(i, TK, TA)

        def probs(j, delta):
            ks = pl.ds(pl.multiple_of(j * TK, TK), TK)
            ckb = jnp.concatenate([jnp.broadcast_to(ck_ref[h:h + 1, ks], (TA, TK)) for h in range(NH)], axis=0)
            sc = _dg(qs, k_ref[ks, :], 1, 1) + cqs - ckb
            col = j * TK + _iota((1, TK), 1)
            p = jnp.where(col <= row, jnp.exp(sc - lses), 0.0)
            dp = _dg(dos, v_ref[ks, :], 1, 1)
            p_s[:, ks] = p
            dp_s[:, ks] = dp
            return delta + jnp.sum(p * dp, axis=-1, keepdims=True)

        delta = lax.fori_loop(0, nk, probs, jnp.zeros((SROWS, 1), F32))

        def step(j, carry):
            ks = pl.ds(pl.multiple_of(j * TK, TK), TK)
            p = p_s[:, ks]
            ds = p * (dp_s[:, ks] - delta)
            dsb = _cast(ds)
            dqa[...] += _dg(_lanes_cat(dsb), kst[pl.ds(pl.multiple_of(j * NH * TK, NH * TK), NH * TK), :], 1, 0) * SCALE
            dk_ref[ks, :] += _dg(dsb, qs, 0, 0)
            dv_ref[ks, :] += _dg(_cast(p), dos, 0, 0)
            for h in range(NH):
                dck_ref[h:h + 1, ks] -= jnp.sum(ds[h * TA:(h + 1) * TA], axis=0, keepdims=True)
            return carry

        lax.fori_loop(0, nk, step, 0)
        dq_ref[...] = dqa[...]

    tok = pl.BlockSpec((TA, GW), lambda b, i: (b * nq + i, 0))
    seq = pl.BlockSpec((s, GW), lambda b, i: (b, 0))
    t128 = pl.BlockSpec((TA, 128), lambda b, i: (b * nq + i, 0))
    return _call(
        body, f"fox_attn_bwd_{tag}", (bl, nq),
        [tok, seq, seq, t128, pl.BlockSpec((None, 8, s), lambda b, i: (b, 0, 0)), t128, tok],
        [tok, seq, seq, pl.BlockSpec((None, 128, s), lambda b, i: (b, 0, 0))],
        [_sds((bl * s, GW)), _sds((bl * s, GW)), _sds((bl * s, GW)), _sds((bl, 128, s))],
        [pltpu.VMEM((TA, GW), F32), pltpu.VMEM((SROWS, s), F32), pltpu.VMEM((SROWS, s), F32), pltpu.VMEM((NH * s, GW), _MMT)],
    )(qn, kn, vb, cq, ck, lse, do)


def _sb_block(qh, kb, valid, upper, r_carry):
    z = _dg(qh, kb, 1, 1)
    ls = _logsig(z)
    lom = ls - z if valid is None else jnp.where(valid, ls - z, 0.0)
    between = xr(lom, upper, upper) + r_carry
    w = jnp.exp(ls + between)
    return ls, lom, (w if valid is None else jnp.where(valid, w, 0.0))


def _fill_stacked(dst, src_ref, s, tk):
    for j in range(s // tk):
        dst[j * NH * tk:(j + 1) * NH * tk, :] = _mask_stack(src_ref[j * tk:(j + 1) * tk, :])


def sb_attn_fwd(pb, bl, s, tag):
    TA, TK = TA_BIG, TK_SB
    nq, SROWS = s // TA, NH * TA

    def body(q_ref, k_ref, v_ref, o_ref, acc, vst):
        i = pl.program_id(1)

        @pl.when(i == 0)
        def _():
            _fill_stacked(vst, v_ref, s, TK)

        qs = _stack_heads(q_ref[...], SCALE)
        upper = _tri(TK, "lt")
        last = _n_key_tiles(i, TK, TA) - 1

        def step(j, r, valid):
            ks = pl.ds(pl.multiple_of(j * TK, TK), TK)
            _, lom, w = _sb_block(qs, _cast(k_ref[ks, :]), valid, upper, r)
            acc[...] += _dg(_lanes_cat(_cast(w)), vst[pl.ds(pl.multiple_of(j * NH * TK, NH * TK), NH * TK), :], 1, 0)
            return r + jnp.sum(lom, axis=-1, keepdims=True)

        acc[...] = jnp.zeros_like(acc)
        r = step(last, jnp.zeros((SROWS, 1), F32), last * TK + _iota((1, TK), 1) < _stack_rows(i, TA))
        lax.fori_loop(0, last, lambda jj, r: step(last - 1 - jj, r, None), r)
        o_ref[...] = acc[...]

    tok = lambda j: pl.BlockSpec((TA, GW), lambda b, i: (b * nq + i, j))
    seq = lambda j: pl.BlockSpec((s, GW), lambda b, i: (b, j))
    return _call(
        body, f"sb_attn_fwd_{tag}", (bl, nq), [tok(0), seq(1), seq(2)],
        pl.BlockSpec((TA, GW), lambda b, i: (b * nq + i, 0)), _sds((bl * s, GW)),
        [pltpu.VMEM((TA, GW), F32), pltpu.VMEM((NH * s, GW), _MMT)],
    )(pb, pb, pb)


def sb_attn_bwd(pb, do, bl, s, tag):
    TA, TK = TA_BIG, TK_SB
    nq, SROWS = s // TA, NH * TA

    def body(q_ref, k_ref, v_ref, do_ref, dq_ref, dk_ref, dv_ref, dqa, sig_s, nsig_s, w_s, g_s, kst):
        i = pl.program_id(1)

        @pl.when(i == 0)
        def _():
            dk_ref[...] = jnp.zeros_like(dk_ref)
            dv_ref[...] = jnp.zeros_like(dv_ref)
            _fill_stacked(kst, k_ref, s, TK)

        qs = _stack_heads(q_ref[...], SCALE)
        dos = _stack_heads(do_ref[...])
        upper = _tri(TK, "lt")
        before = _tri(TK, "gt")
        dqa[...] = jnp.zeros_like(dqa)
        last = _n_key_tiles(i, TK, TA) - 1
        diag = last * TK + _iota((1, TK), 1) < _stack_rows(i, TA)

        def weights(j, r, valid):
            ks = pl.ds(pl.multiple_of(j * TK, TK), TK)
            ls, lom, w = _sb_block(qs, _cast(k_ref[ks, :]), valid, upper, r)
            sig_s[:, ks] = _cast(jnp.exp(ls))
            nsig_s[:, ks] = _cast(jnp.exp(lom))
            w_s[:, ks] = _cast(w)
            g_s[:, ks] = _dg(dos, _cast(v_ref[ks, :]), 1, 1) * w
            return r + jnp.sum(lom, axis=-1, keepdims=True)

        r = weights(last, jnp.zeros((SROWS, 1), F32), diag)
        lax.fori_loop(0, last, lambda jj, r: weights(last - 1 - jj, r, None), r)

        def step(j, cpre, valid):
            ks = pl.ds(pl.multiple_of(j * TK, TK), TK)
            g = g_s[:, ks]
            pre = cpre + xr(g, before, before)
            dz = g * nsig_s[:, ks].astype(F32) - sig_s[:, ks].astype(F32) * pre
            dzb = _cast(dz if valid is None else jnp.where(valid, dz, 0.0))
            dqa[...] += _dg(_lanes_cat(dzb), kst[pl.ds(pl.multiple_of(j * NH * TK, NH * TK), NH * TK), :], 1, 0) * SCALE
            dk_ref[ks, :] += _dg(dzb, qs, 0, 0)
            dv_ref[ks, :] += _dg(w_s[:, ks], dos, 0, 0)
            return cpre + jnp.sum(g, axis=-1, keepdims=True)

        cpre = lax.fori_loop(0, last, lambda j, c: step(j, c, None), jnp.zeros((SROWS, 1), F32))
        step(last, cpre, diag)
        dq_ref[...] = dqa[...]

    tok = lambda j: pl.BlockSpec((TA, GW), lambda b, i: (b * nq + i, j))
    seq = lambda j: pl.BlockSpec((s, GW), lambda b, i: (b, j))
    return _call(
        body, f"sb_attn_bwd_{tag}", (bl, nq), [tok(0), seq(1), seq(2), tok(0)],
        [tok(0), seq(0), seq(0)], [_sds((bl * s, GW))] * 3,
        [pltpu.VMEM((TA, GW), F32), pltpu.VMEM((SROWS, s), _MMT), pltpu.VMEM((SROWS, s), _MMT),
         pltpu.VMEM((SROWS, s), _MMT), pltpu.VMEM((SROWS, s), F32), pltpu.VMEM((NH * s, GW), _MMT)],
    )(pb, pb, pb, do)


def _hgrn_consts():
    r, c = _iota((CH, CH), 0), _iota((CH, CH), 1)
    rr = _iota((CH, 1), 0)
    tri = (c <= r).astype(F32)
    lv = []
    for m in (8, 4, 2, 1):
        up = ((rr & (2 * m - 1)) >= m).astype(F32)
        selq = (((r & (2 * m - 1)) >= m) & (c == (r & ~(m - 1)) - 1)).astype(F32)
        selk = (((r & (2 * m - 1)) < m) & (c == (r & ~(m - 1)) + m - 1)).astype(F32)
        pm = (((r & ~(2 * m - 1)) == (c & ~(2 * m - 1))) & ((r & (2 * m - 1)) >= m) & ((c & (2 * m - 1)) < m)).astype(F32)
        lv.append((up, 1.0 - up, selq, selq.T, selk, selk.T, jnp.concatenate([pm] * NH, axis=0)))
    hm4 = lambda n: (((_iota((NH, 1, n), 2) & (GW - 1)) >> 6) == _iota((NH, 1, n), 0)).astype(F32)
    return dict(tri=tri, trit=tri.T, rr=rr, lv=lv, bd=_bdmask(), bd64=_bdmask() * (1.0 / HD),
                hm4={GW: hm4(GW), 3 * GW: hm4(3 * GW)})


def _hgrn_chunk_fn(hq, hf, hi, lb, wn, st, cs):
    q = _silu(hq)
    log_lb = jnp.log(jnp.maximum(lb, LB_FLOOR))
    a, bb = log_lb, jnp.log1p(-lb) + _logsig(hf)
    g = jnp.maximum(a, bb) + jnp.log1p(jnp.exp(-jnp.abs(a - bb)))
    k = (1.0 - lb) * _sigmoid(-hf)
    v = hi
    rr = cs["rr"]
    b = xl(cs["tri"], cs["trit"], g)
    row_of = lambda n: jnp.sum(b * (rr == n).astype(F32), axis=0, keepdims=True)
    o = mm_nt(q * jnp.exp(b), st)
    qs, ks = [], []
    for ib in (1, 2, 3):
        ref = row_of(16 * ib - 1)
        inq = ((rr >= 16 * ib) & (rr < 16 * ib + 16)).astype(F32)
        ink = (rr < 16 * ib).astype(F32)
        qs.append(q * jnp.exp((b - ref) * inq) * inq)
        ks.append(k * jnp.exp((ref - b) * ink) * ink)
    qcat, kcat = jnp.concatenate(qs, axis=1), jnp.concatenate(ks, axis=1)
    lvl = []
    for up, lo, selq, selqt, selk, selkt, pm in cs["lv"]:
        qe = q * jnp.exp((b - xl(selq, selqt, b)) * up) * up
        ke = k * jnp.exp((xl(selk, selkt, b) - b) * lo) * lo
        lvl.append((qe, ke, pm))
    stack = lambda x: (x[None] * cs["hm4"][x.shape[1]]).reshape(NH * CH, x.shape[1])
    a_all = mm_nt(stack(qcat), kcat)
    for qe, ke, pm4 in lvl:
        a_all = a_all + mm_nt(stack(qe), ke) * pm4
    o = o + jnp.sum(mm(a_all, v).reshape(NH, CH, GW) * cs["hm4"][GW], axis=0)
    o = o + xr(q * k, cs["bd"], cs["bd"]) * v
    b_last = row_of(CH - 1)
    st_new = st * jnp.exp(b_last) + mm_tn(v, k * jnp.exp(b_last - b)) * cs["bd"]
    return _headrms(o, wn, cs["bd64"]), st_new


def hgrn_fwd(pc, lb, wn, bl, s, layer, tag):
    nc = s // CH

    def body(q_ref, f_ref, i_ref, lb_ref, wn_ref, o_ref, st_ref, st):
        @pl.when(pl.program_id(0) == 0)
        def _():
            st[...] = jnp.zeros_like(st)

        cs = _hgrn_consts()
        for b in range(bl):
            st_ref[b] = st[b]
            o, st_new = _hgrn_chunk_fn(q_ref[b], f_ref[b], i_ref[b], lb_ref[...], wn_ref[...], st[b], cs)
            o_ref[b] = o
            st[b] = st_new

    tok = lambda j: pl.BlockSpec((bl, CH, GW), lambda c: (0, c, j))
    par = pl.BlockSpec((None, 1, GW), lambda c: (layer, 0, 0))
    pc3 = pc.reshape(bl, s, 3 * GW)
    o, states = _call(
        body, f"hgrn_fwd_{tag}", (nc,), [tok(0), tok(1), tok(2), par, par],
        [tok(0), pl.BlockSpec((bl, None, GW, GW), lambda c: (0, c, 0, 0))],
        [_sds((bl, s, GW)), _sds((bl, nc, GW, GW))],
        [pltpu.VMEM((bl, GW, GW), F32)],
    )(pc3, pc3, pc3, lb, wn)
    return o.reshape(bl * s, GW), states


def hgrn_bwd(pc, lb, wn, states, do, bl, s, layer, tag):
    nc = s // CH

    def body(q_ref, f_ref, i_ref, lb_ref, wn_ref, st_ref, do_ref, dc_ref, dlb_ref, dwn_ref, dst):
        c = pl.program_id(0)

        @pl.when(c == 0)
        def _():
            dst[...] = jnp.zeros_like(dst)

        cs = _hgrn_consts()
        dlb_sum = dwn_sum = None
        for b in range(bl):
            _, vjp = jax.vjp(lambda *a: _hgrn_chunk_fn(*a, cs), q_ref[b], f_ref[b], i_ref[b], lb_ref[...],
                             wn_ref[...], st_ref[b])
            dq, df, di, dlb, dwn, dst_in = vjp((do_ref[b], dst[b]))
            dst[b] = dst_in
            dc_ref[b, :, 0:GW] = dq
            dc_ref[b, :, GW:2 * GW] = df
            dc_ref[b, :, 2 * GW:3 * GW] = di
            dlb_sum = dlb if dlb_sum is None else dlb_sum + dlb
            dwn_sum = dwn if dwn_sum is None else dwn_sum + dwn
        _acc(dlb_ref, dlb_sum, c == 0)
        _acc(dwn_ref, dwn_sum, c == 0)

    tok = lambda j: pl.BlockSpec((bl, CH, GW), lambda c: (0, nc - 1 - c, j))
    par = pl.BlockSpec((None, 1, GW), lambda c: (layer, 0, 0))
    acc = pl.BlockSpec((1, GW), lambda c: (0, 0))
    pc3 = pc.reshape(bl, s, 3 * GW)
    dc, dlb, dwn = _call(
        body, f"hgrn_bwd_{tag}", (nc,),
        [tok(0), tok(1), tok(2), par, par, pl.BlockSpec((bl, None, GW, GW), lambda c: (0, nc - 1 - c, 0, 0)), tok(0)],
        [pl.BlockSpec((bl, CH, 3 * GW), lambda c: (0, nc - 1 - c, 0)), acc, acc],
        [_sds((bl, s, 3 * GW)), _sds((1, GW)), _sds((1, GW))],
        [pltpu.VMEM((bl, GW, GW), F32)],
    )(pc3, pc3, pc3, lb, wn, states, do.reshape(bl, s, GW))
    return dc.reshape(bl * s, 3 * GW), dlb, dwn


def _shift_rows(x, k, up):
    n = x.shape[0]
    rr = _iota((n, 1), 0)
    if up:
        return jnp.where(rr < n - k, pltpu.roll(x, n - k, 0), 0.0)
    return jnp.where(rr >= k, pltpu.roll(x, k, 0), 0.0)


def _window_sums(x, up):
    s2 = x + _shift_rows(x, 1, up)
    s4 = s2 + _shift_rows(s2, 2, up)
    s8 = s4 + _shift_rows(s4, 4, up)
    s16 = s8 + _shift_rows(s8, 8, up)
    return s2, s4, s8, s16


def _pool_div(n):
    pos = (_iota((n, 1), 0) + 1).astype(F32)
    return [jnp.minimum(pos, float(w)) for w in (2, 4, 8, 16)]


def _pool_mix(sums, scaled):
    out = None
    for gi, sw in enumerate(sums):
        part = (sw if scaled is None else sw / scaled[gi]) * _hmask(gi)
        out = part if out is None else out + part
    return out


def pool_fwd(pd, wbd, scale, bl, s, layer, tag):
    def body(u_ref, w_ref, sc_ref, o_ref):
        u = u_ref[...]
        pm = _pool_mix(_window_sums(u, False), _pool_div(s)) - u
        o_ref[...] = _dg(_cast(pm), _cast(w_ref[...]), 1, 0) * sc_ref[...]

    seq = pl.BlockSpec((s, GW), lambda b: (b, 0))
    return _call(
        body, f"pool_fwd_{tag}", (bl,),
        [seq, pl.BlockSpec((None, GW, GW), lambda b: (layer, 0, 0)), pl.BlockSpec((None, 1, GW), lambda b: (layer, 0, 0))],
        seq, _sds((bl * s, GW)),
    )(pd, wbd, scale)


def pool_bwd(pd, wbd, scale, do, bl, s, layer, tag):
    def body(u_ref, w_ref, sc_ref, do_ref, du_ref, dw_ref, dsc_ref):
        first = pl.program_id(0) == 0
        u, do = u_ref[...], do_ref[...]
        div = _pool_div(s)
        pm = _pool_mix(_window_sums(u, False), div) - u
        ypre = _dg(_cast(pm), _cast(w_ref[...]), 1, 0)
        dys = do * sc_ref[...]
        _acc(dsc_ref, jnp.sum(do * ypre, axis=0, keepdims=True), first)
        _acc(dw_ref, _dg(_cast(pm), _cast(dys), 0, 0), first)
        dpm = _dg(_cast(dys), _cast(w_ref[...]), 1, 1)
        dsc = [dpm / d for d in div]
        adj = None
        for gi in range(4):
            part = _window_sums(dsc[gi] * _hmask(gi), True)[gi]
            adj = part if adj is None else adj + part
        du_ref[...] = adj - dpm

    seq = pl.BlockSpec((s, GW), lambda b: (b, 0))
    return _call(
        body, f"pool_bwd_{tag}", (bl,),
        [seq, pl.BlockSpec((None, GW, GW), lambda b: (layer, 0, 0)), pl.BlockSpec((None, 1, GW), lambda b: (layer, 0, 0)), seq],
        [seq, pl.BlockSpec((GW, GW), lambda b: (0, 0)), pl.BlockSpec((1, GW), lambda b: (0, 0))],
        [_sds((bl * s, GW)), _sds((GW, GW)), _sds((1, GW))],
    )(pd, wbd, scale, do)


def _mem_prep_fn(mem, g, wk, wv, kw, bd64):
    mn = _rms(mem, g)
    return _headrms(mm(mn, wk), kw, bd64), mm(mn, wv)


def mem_prep_fwd(mem2, g, wkv, kw, bl, layer, tag):
    def body(m_ref, g_ref, wk_ref, wv_ref, kw_ref, k_ref, v_ref):
        k, v = _mem_prep_fn(m_ref[...], g_ref[...], wk_ref[...], wv_ref[...], kw_ref[...], _bdmask() * (1.0 / HD))
        k_ref[...] = k
        v_ref[...] = v

    blk = pl.BlockSpec((N_MEM, GW), lambda b: (b, 0))
    return _call(
        body, f"mem_prep_fwd_{tag}", (bl,),
        [pl.BlockSpec((N_MEM, D_MODEL), lambda b: (b, 0)), pl.BlockSpec((None, 1, D_MODEL), lambda b: (layer, 0, 0)),
         pl.BlockSpec((None, D_MODEL, GW), lambda b: (layer, 0, 0)), pl.BlockSpec((None, D_MODEL, GW), lambda b: (layer, 0, 1)),
         pl.BlockSpec((None, 1, GW), lambda b: (layer, 0, 0))],
        [blk, blk], [_sds((bl * N_MEM, GW))] * 2,
    )(mem2, g, wkv, wkv, kw)


def mem_prep_bwd(mem2, g, wkv, kw, dk, dv, bl, layer, tag):
    def body(m_ref, g_ref, wk_ref, wv_ref, kw_ref, dk_ref, dv_ref, dwk_ref, dwv_ref, dg_ref, dkw_ref):
        first = pl.program_id(0) == 0
        bd64 = _bdmask() * (1.0 / HD)
        _, vjp = jax.vjp(lambda g_, wk, wv, kw_: _mem_prep_fn(m_ref[...], g_, wk, wv, kw_, bd64),
                         g_ref[...], wk_ref[...].astype(F32), wv_ref[...].astype(F32), kw_ref[...])
        dg, dwk, dwv, dkw = vjp((dk_ref[...], dv_ref[...]))
        _acc(dwk_ref, dwk, first)
        _acc(dwv_ref, dwv, first)
        _acc(dg_ref, dg, first)
        _acc(dkw_ref, dkw, first)

    blk = pl.BlockSpec((N_MEM, GW), lambda b: (b, 0))
    return _call(
        body, f"mem_prep_bwd_{tag}", (bl,),
        [pl.BlockSpec((N_MEM, D_MODEL), lambda b: (b, 0)), pl.BlockSpec((None, 1, D_MODEL), lambda b: (layer, 0, 0)),
         pl.BlockSpec((None, D_MODEL, GW), lambda b: (layer, 0, 0)), pl.BlockSpec((None, D_MODEL, GW), lambda b: (layer, 0, 1)),
         pl.BlockSpec((None, 1, GW), lambda b: (layer, 0, 0)), blk, blk],
        [pl.BlockSpec((D_MODEL, GW), lambda b: (0, 0)), pl.BlockSpec((D_MODEL, GW), lambda b: (0, 0)),
         pl.BlockSpec((1, D_MODEL), lambda b: (0, 0)), pl.BlockSpec((1, GW), lambda b: (0, 0))],
        [_sds((D_MODEL, GW)), _sds((D_MODEL, GW)), _sds((1, D_MODEL)), _sds((1, GW))],
    )(mem2, g, wkv, wkv, kw, dk, dv)


def _mem_attn_fn(mq, qw, k, v, bd64):
    qn = _headrms(mq, qw, bd64)
    out = None
    for h in range(NH):
        hm = _hmask(h)
        lg = mm_nt(qn * hm, k) * SCALE
        e = jnp.exp(lg - lax.stop_gradient(jnp.max(lg, axis=-1, keepdims=True)))
        p = e / jnp.sum(e, axis=-1, keepdims=True)
        part = mm(p, v) * hm
        out = part if out is None else out + part
    return out


def mem_attn_fwd(pe, qw, k, v, bl, s, layer, tag):
    TQ = _token_tile(s)
    nq = s // TQ

    def body(q_ref, qw_ref, k_ref, v_ref, o_ref):
        o_ref[...] = _mem_attn_fn(q_ref[...], qw_ref[...], k_ref[...], v_ref[...], _bdmask() * (1.0 / HD))

    tok = pl.BlockSpec((TQ, GW), lambda b, i: (b * nq + i, 0))
    kv = pl.BlockSpec((N_MEM, GW), lambda b, i: (b, 0))
    return _call(
        body, f"mem_attn_fwd_{tag}", (bl, nq), [tok, pl.BlockSpec((None, 1, GW), lambda b, i: (layer, 0, 0)), kv, kv],
        tok, _sds((bl * s, GW)),
    )(pe, qw, k, v)


def mem_attn_bwd(pe, qw, k, v, do, bl, s, layer, tag):
    TQ = _token_tile(s)
    nq = s // TQ

    def body(q_ref, qw_ref, k_ref, v_ref, do_ref, dq_ref, dk_ref, dv_ref, dqw_ref):
        i = pl.program_id(1)
        bd64 = _bdmask() * (1.0 / HD)
        _, vjp = jax.vjp(lambda *a: _mem_attn_fn(*a, bd64), q_ref[...], qw_ref[...], k_ref[...], v_ref[...])
        dq, dqw, dk, dv = vjp(do_ref[...])
        dq_ref[...] = dq
        _acc(dk_ref, dk, i == 0)
        _acc(dv_ref, dv, i == 0)
        _acc(dqw_ref, dqw, jnp.logical_and(pl.program_id(0) == 0, i == 0))

    tok = pl.BlockSpec((TQ, GW), lambda b, i: (b * nq + i, 0))
    kv = pl.BlockSpec((N_MEM, GW), lambda b, i: (b, 0))
    return _call(
        body, f"mem_attn_bwd_{tag}", (bl, nq),
        [tok, pl.BlockSpec((None, 1, GW), lambda b, i: (layer, 0, 0)), kv, kv, tok],
        [tok, kv, kv, pl.BlockSpec((1, GW), lambda b, i: (0, 0))],
        [_sds((bl * s, GW)), _sds((bl * N_MEM, GW)), _sds((bl * N_MEM, GW)), _sds((1, GW))],
    )(pe, qw, k, v, do)


def _gate_out_fn(outs, gates, wparts):
    y = None
    for o, g, w in zip(outs, gates, wparts):
        part = mm(o * _silu(g), w)
        y = part if y is None else y + part
    return y


def outproj_fwd(x2, outs, pg, wout, layer, tag):
    t = x2.shape[0]
    TQ = _token_tile(t)

    def body(x_ref, oa, ob, oc, od, oe, g_ref, w_ref, y_ref):
        outs_ = [r[...] for r in (oa, ob, oc, od, oe)]
        gates = [g_ref[:, j * GW:(j + 1) * GW] for j in range(5)]
        wparts = [w_ref[j * GW:(j + 1) * GW, :] for j in range(5)]
        y_ref[...] = x_ref[...] + _gate_out_fn(outs_, gates, wparts)

    tok = pl.BlockSpec((TQ, GW), lambda i: (i, 0))
    big = pl.BlockSpec((TQ, D_MODEL), lambda i: (i, 0))
    return _call(
        body, f"outproj_fwd_{tag}", (t // TQ,),
        [big] + [tok] * 5 + [pl.BlockSpec((TQ, D_MIX), lambda i: (i, 0)),
                            pl.BlockSpec((None, D_MIX, D_MODEL), lambda i: (layer, 0, 0))],
        big, _sds((t, D_MODEL)),
    )(x2, *outs, pg, wout)


def outproj_bwd(outs, pg, wout, dy, layer, tag):
    t = dy.shape[0]
    TQ = _token_tile(t)

    def body(oa, ob, oc, od, oe, g_ref, w_ref, dy_ref, da, db, dc, dd, de, dg_ref, dw_ref):
        outs_ = [r[...] for r in (oa, ob, oc, od, oe)]
        gates = [g_ref[:, j * GW:(j + 1) * GW] for j in range(5)]
        wparts = [w_ref[j * GW:(j + 1) * GW, :].astype(F32) for j in range(5)]
        _, vjp = jax.vjp(_gate_out_fn, outs_, gates, wparts)
        douts, dgates, dws = vjp(dy_ref[...])
        for r, val in zip((da, db, dc, dd, de), douts):
            r[...] = val
        first = pl.program_id(0) == 0
        for j in range(5):
            dg_ref[:, j * GW:(j + 1) * GW] = dgates[j]

        @pl.when(first)
        def _():
            for j in range(5):
                dw_ref[j * GW:(j + 1) * GW, :] = dws[j]

        @pl.when(jnp.logical_not(first))
        def _():
            for j in range(5):
                dw_ref[j * GW:(j + 1) * GW, :] += dws[j]

    tok = pl.BlockSpec((TQ, GW), lambda i: (i, 0))
    return _call(
        body, f"outproj_bwd_{tag}", (t // TQ,),
        [tok] * 5 + [pl.BlockSpec((TQ, D_MIX), lambda i: (i, 0)), pl.BlockSpec((None, D_MIX, D_MODEL), lambda i: (layer, 0, 0)),
                     pl.BlockSpec((TQ, D_MODEL), lambda i: (i, 0))],
        [tok] * 5 + [pl.BlockSpec((TQ, D_MIX), lambda i: (i, 0)), pl.BlockSpec((D_MIX, D_MODEL), lambda i: (0, 0))],
        [_sds((t, GW))] * 5 + [_sds((t, D_MIX)), _sds((D_MIX, D_MODEL))],
    )(*outs, pg, wout, dy)


def loss_head(y, tgt):
    t = y.shape[0]
    TQ = _token_tile(t)

    def body(y_ref, t_ref, l_ref, dy_ref):
        diff = y_ref[...] - t_ref[...]
        dy_ref[...] = diff * (1.0 / D_MODEL)
        part = 0.5 * jnp.sum(jnp.sum(diff * diff, axis=-1, keepdims=True) * (1.0 / D_MODEL), axis=0, keepdims=True)
        _acc(l_ref, jnp.broadcast_to(part, (8, 128)), pl.program_id(0) == 0)

    big = pl.BlockSpec((TQ, D_MODEL), lambda i: (i, 0))
    return _call(body, "loss_head", (t // TQ,), [big, big], [pl.BlockSpec((8, 128), lambda i: (0, 0)), big],
                 [_sds((8, 128)), _sds((t, D_MODEL))])(y, tgt)


def layer_fwd(x2, mem2, p, layer, bl, s):
    tag = f"l{layer}"
    ht, pa, pb, pc, pd, pe, pg, pf = inproj_fwd(x2, p["norm_g"], p["w_in"], layer, tag)
    qn, kn, vb, cq, ck = fox_prep_fwd(pa, pf, p["fox_q_norm"], p["fox_k_norm"], p["fox_f_bias"], bl, s, layer, tag)
    oa, lse = fox_attn_fwd(qn, kn, vb, cq, ck, bl, s, tag)
    ob = sb_attn_fwd(pb, bl, s, tag)
    oc, states = hgrn_fwd(pc, p["lb"], p["hgrn_out_norm"], bl, s, layer, tag)
    od = pool_fwd(pd, p["pool_wbd"], p["pool_scale"], bl, s, layer, tag)
    mk, mv = mem_prep_fwd(mem2, p["mem_norm_g"], p["mem_w_kv"], p["mem_k_norm"], bl, layer, tag)
    oe = mem_attn_fwd(pe, p["mem_q_norm"], mk, mv, bl, s, layer, tag)
    y = outproj_fwd(x2, (oa, ob, oc, od, oe), pg, p["w_out"], layer, tag)
    saved = dict(x2=x2, ht=ht, pa=pa, pb=pb, pc=pc, pd=pd, pe=pe, pg=pg, pf=pf, qn=qn, kn=kn, vb=vb, cq=cq, ck=ck,
                 oa=oa, lse=lse, ob=ob, oc=oc, states=states, od=od, mk=mk, mv=mv, oe=oe)
    return y, saved


def layer_bwd(dy, mem2, p, sv, layer, bl, s):
    tag = f"l{layer}"
    (doa, dob, doc, dod, doe, dg_gates, dwout) = outproj_bwd((sv["oa"], sv["ob"], sv["oc"], sv["od"], sv["oe"]), sv["pg"],
                                                              p["w_out"], dy, layer, tag)
    dqn, dkn, dv, dck = fox_attn_bwd(sv["qn"], sv["kn"], sv["vb"], sv["cq"], sv["ck"], sv["lse"], doa, bl, s, tag)
    d_a, d_f, dqw, dkw, dbias = fox_prep_bwd(sv["pa"], sv["pf"], p["fox_q_norm"], p["fox_k_norm"], p["fox_f_bias"], sv["cq"],
                                             dqn, dkn, dv, dck, bl, s, layer, tag)
    dsq, dsk, dsv = sb_attn_bwd(sv["pb"], dob, bl, s, tag)
    d_c, dlb, dwn = hgrn_bwd(sv["pc"], p["lb"], p["hgrn_out_norm"], sv["states"], doc, bl, s, layer, tag)
    d_d, dwbd, dpscale = pool_bwd(sv["pd"], p["pool_wbd"], p["pool_scale"], dod, bl, s, layer, tag)
    d_e, dmk, dmv, dmqw = mem_attn_bwd(sv["pe"], p["mem_q_norm"], sv["mk"], sv["mv"], doe, bl, s, layer, tag)
    dwk, dwv, dmg, dmkw = mem_prep_bwd(mem2, p["mem_norm_g"], p["mem_w_kv"], p["mem_k_norm"], dmk, dmv, bl, layer, tag)
    dpieces = (d_a, dsq, dsk, dsv, d_c, d_d, d_e, dg_gates, d_f)
    dx, dng = inproj_bwd_dx(sv["x2"], p["norm_g"], p["w_in"], dy, dpieces, layer, tag)
    dwin = jnp.concatenate([matmul_acc(sv["ht"], dp, f"{tag}_{nm}") for (nm, _, _), dp in zip(BWD_PIECES, dpieces)], axis=1)
    grads = dict(norm_g=dng, w_in=dwin, fox_f_bias=dbias, fox_q_norm=dqw, fox_k_norm=dkw, lb=dlb, hgrn_out_norm=dwn,
                 pool_wbd=dwbd, pool_scale=dpscale, mem_norm_g=dmg, mem_w_kv=jnp.concatenate([dwk, dwv], axis=1),
                 mem_q_norm=dmqw, mem_k_norm=dmkw, w_out=dwout)
    return dx, grads


def _tile4(w):
    return jnp.tile(w, (1, NH))[:, None, :]


def prepare_params(norm_g, w_in_p, fox_f_bias, fox_q_norm, fox_k_norm, hgrn_lb_logits, hgrn_out_norm, pool_w, pool_scale,
                   mem_norm_g, mem_w_kv, mem_q_norm, mem_k_norm, w_out):
    p1 = jax.nn.sigmoid(hgrn_lb_logits[1] - hgrn_lb_logits[0])
    lb = jnp.stack([jnp.zeros_like(p1), jnp.clip(p1, 0.0, 1.0 - 1e-6)])
    eye = jnp.eye(4, dtype=F32)
    wbd = jnp.einsum("lgcd,gh->lgchd", pool_w, eye).reshape(2, GW, GW)
    return dict(norm_g=norm_g[:, None, :], w_in=w_in_p, fox_f_bias=jnp.pad(fox_f_bias, ((0, 0), (0, 124)))[:, None, :],
                fox_q_norm=_tile4(fox_q_norm), fox_k_norm=_tile4(fox_k_norm), lb=lb[:, None, :],
                hgrn_out_norm=hgrn_out_norm[:, None, :], pool_wbd=wbd, pool_scale=pool_scale[:, None, :],
                mem_norm_g=mem_norm_g[:, None, :], mem_w_kv=mem_w_kv, mem_q_norm=_tile4(mem_q_norm),
                mem_k_norm=_tile4(mem_k_norm), w_out=w_out)


def local_step(x, mem, tgt, p):
    bl, s, _ = x.shape
    x2, mem2, tgt2 = x.reshape(bl * s, D_MODEL), mem.reshape(bl * N_MEM, D_MODEL), tgt.reshape(bl * s, D_MODEL)
    y0, sv0 = layer_fwd(x2, mem2, p, 0, bl, s)
    y1, sv1 = layer_fwd(y0, mem2, p, 1, bl, s)
    lpart, dy = loss_head(y1, tgt2)
    dx1, g1 = layer_bwd(dy, mem2, p, sv1, 1, bl, s)
    dx0, g0 = layer_bwd(dx1, mem2, p, sv0, 0, bl, s)
    return lpart[0, 0], dx0.reshape(bl, s, D_MODEL), g0, g1


_ANY = pl.BlockSpec(memory_space=pl.ANY)


def all_gather_rows(xss, tag):
    n = len(xss)

    def body(*refs):
        x_refs, o_refs, (send_sems, recv_sems, local_sems) = refs[:n], refs[n:2 * n], refs[2 * n:]
        x, y, cc = lax.axis_index("x"), lax.axis_index("y"), lax.axis_index("c")
        me, sibling = (x, y, cc), (x, y, 1 - cc)
        chips = [(1 - x, y), (x, 1 - y), (1 - x, 1 - y)]

        def rows(a, px, py, pc):
            r = xss[a].shape[1]
            return o_refs[a].at[:, pl.ds((4 * px + 2 * py + pc) * r, r), :]

        def copy(a, k, block, to, src=None):
            return pltpu.make_async_remote_copy(src_ref=rows(a, *block) if src is None else src, dst_ref=rows(a, *block),
                                                send_sem=send_sems.at[7 * a + k], recv_sem=recv_sems.at[7 * a + k], device_id=to,
                                                device_id_type=pl.DeviceIdType.MESH)

        mine = [pltpu.make_async_copy(x_refs[a], rows(a, *me), local_sems.at[a]) for a in range(n)]
        first = []
        for a in range(n):
            first += [copy(a, 0, me, sibling, src=x_refs[a])] + [copy(a, 1 + j, me, (*chip, cc), src=x_refs[a])
                                                                 for j, chip in enumerate(chips)]
        for cp in mine + first:
            cp.start()
        passed = []
        for j, chip in enumerate(chips):
            for a in range(n):
                copy(a, 1 + j, (*chip, cc), me).wait_recv()
                passed.append(copy(a, 4 + j, (*chip, cc), sibling))
                passed[-1].start()
        for a in range(n):
            copy(a, 0, sibling, me).wait_recv()
        for j, chip in enumerate(chips):
            for a in range(n):
                copy(a, 4 + j, (*chip, 1 - cc), me).wait_recv()
        for cp in first + passed:
            cp.wait_send()
        for cp in mine:
            cp.wait()

    nsem = pltpu.SemaphoreType.DMA((7 * n,))
    return pl.pallas_call(
        body, name=f"all_gather_{tag}", in_specs=[_ANY] * n, out_specs=[_ANY] * n,
        out_shape=[_sds((xs.shape[0], N_DEV * xs.shape[1], xs.shape[2]), xs.dtype) for xs in xss],
        scratch_shapes=[nsem, nsem, pltpu.SemaphoreType.DMA((n,))],
    )(*xss)


def exchange_cores(parts, tag):
    n = len(parts)
    counts = [p.shape[0] * 4 for p in parts]

    def body(*refs):
        p_refs, t_refs, (send_sems, recv_sems) = refs[:n], refs[n:2 * n], refs[2 * n:]
        x, y, cc = lax.axis_index("x"), lax.axis_index("y"), lax.axis_index("c")
        copies, k = [], 0
        for a in range(n):
            for l in range(parts[a].shape[0]):
                for q in range(4):
                    copies.append(pltpu.make_async_remote_copy(
                        src_ref=p_refs[a].at[l, q, pl.ds(1 - cc, 1)], dst_ref=t_refs[a].at[l, q], send_sem=send_sems.at[k],
                        recv_sem=recv_sems.at[k], device_id=(x, y, 1 - cc), device_id_type=pl.DeviceIdType.MESH))
                    k += 1
        for cp in copies:
            cp.start()
        for cp in copies:
            cp.wait()

    nsem = pltpu.SemaphoreType.DMA((sum(counts),))
    return pl.pallas_call(
        body, name=f"exchange_cores_{tag}", in_specs=[_ANY] * n, out_specs=[_ANY] * n,
        out_shape=[_sds((p.shape[0], 4, 1, p.shape[3], p.shape[4]), p.dtype) for p in parts],
        scratch_shapes=[nsem, nsem],
    )(*parts)


def add_core_halves(part5, theirs, core, tag):
    nl, _, _, r, c = part5.shape
    tr = r if r * c <= 256 * 1024 else 64

    def body(core_ref, a_ref, b_ref, o_ref):
        o_ref[...] = _cast(a_ref[...] + b_ref[...])

    blk = lambda which: pl.BlockSpec((None, None, None, tr, c), lambda l, q, i, cref: (l, q, cref[0] if which else 0, i, 0))
    return pl.pallas_call(
        body, name=f"add_core_halves_{tag}", out_shape=_sds((nl, 4, 1, r, c), _MMT),
        grid_spec=pltpu.PrefetchScalarGridSpec(num_scalar_prefetch=1, grid=(nl, 4, r // tr), in_specs=[blk(True), blk(False)],
                                               out_specs=blk(False)),
        compiler_params=pltpu.CompilerParams(dimension_semantics=("arbitrary",) * 3, vmem_limit_bytes=VMEM_LIMIT_BYTES),
    )(core, part5, theirs)


def exchange_chips(s4s, tag):
    n = len(s4s)

    def body(*refs):
        s_refs, o_refs, (send_sems, recv_sems, local_sems) = refs[:n], refs[n:2 * n], refs[2 * n:]
        x, y, cc = lax.axis_index("x"), lax.axis_index("y"), lax.axis_index("c")
        copies = [pltpu.make_async_copy(s_refs[a].at[:, pl.ds(2 * x + y, 1)], o_refs[a].at[0], local_sems.at[a]) for a in range(n)]
        for k in range(1, 4):
            px = 1 - x if (k >> 1) & 1 else x
            py = 1 - y if k & 1 else y
            for a in range(n):
                copies.append(pltpu.make_async_remote_copy(
                    src_ref=s_refs[a].at[:, pl.ds(2 * px + py, 1)], dst_ref=o_refs[a].at[k], send_sem=send_sems.at[3 * a + k - 1],
                    recv_sem=recv_sems.at[3 * a + k - 1], device_id=(px, py, cc), device_id_type=pl.DeviceIdType.MESH))
        for cp in copies:
            cp.start()
        for cp in copies:
            cp.wait()

    nsem = pltpu.SemaphoreType.DMA((3 * n,))
    return pl.pallas_call(
        body, name=f"exchange_chips_{tag}", in_specs=[_ANY] * n, out_specs=[_ANY] * n,
        out_shape=[_sds((4, s.shape[0], 1, 1, s.shape[3], s.shape[4]), s.dtype) for s in s4s],
        scratch_shapes=[nsem, nsem, pltpu.SemaphoreType.DMA((n,))],
    )(*s4s)


def _row_tile(rows):
    if rows <= 512 and rows % 64:
        return rows
    for t in (64, 40, 32, 16, 8):
        if rows % t == 0:
            return t
    return rows


def sum_slots(slots, tag):
    ns, rows, c = slots.shape
    tr = _row_tile(rows)

    def body(s_ref, o_ref):
        acc = s_ref[0].astype(F32)
        for k in range(1, ns):
            acc = acc + s_ref[k].astype(F32)
        o_ref[...] = acc

    return _call(body, f"sum_slots_{tag}", (rows // tr,), [pl.BlockSpec((ns, tr, c), lambda i: (0, i, 0))],
                 pl.BlockSpec((tr, c), lambda i: (i, 0)), _sds((rows, c)))(slots)


def _adamw(w, g, m, v):
    m = ADAM_B1 * m + (1.0 - ADAM_B1) * g
    v = ADAM_B2 * v + (1.0 - ADAM_B2) * (g * g)
    m_hat = m / (1.0 - ADAM_B1 ** ADAM_STEP)
    v_hat = v / (1.0 - ADAM_B2 ** ADAM_STEP)
    delta = -ADAM_LR * (m_hat / (jnp.sqrt(v_hat) + ADAM_EPS) + ADAM_WD * w)
    return delta, m, v


def adam_update(w, m, v, g, tag, slots=False):
    rows, c = w.shape
    tr = _row_tile(rows)
    ns = g.shape[0] if slots else 0

    def body(w_ref, m_ref, v_ref, g_ref, go_ref, d_ref, mo_ref, vo_ref):
        if slots:
            g = g_ref[0].astype(F32)
            for k in range(1, ns):
                g = g + g_ref[k].astype(F32)
        else:
            g = g_ref[...]
        d, mn, vn = _adamw(w_ref[...], g, m_ref[...], v_ref[...])
        go_ref[...] = g
        d_ref[...] = d
        mo_ref[...] = mn
        vo_ref[...] = vn

    blk = pl.BlockSpec((tr, c), lambda i: (i, 0))
    gspec = pl.BlockSpec((ns, tr, c), lambda i: (0, i, 0)) if slots else blk
    return _call(body, f"adam_{tag}", (rows // tr,), [blk, blk, blk, gspec], [blk] * 4, [_sds((rows, c))] * 4)(w, m, v, g)


_SMALL = (("norm_g", (2, 1024)), ("fox_f_bias", (2, 4)), ("fox_q_norm", (2, 64)), ("fox_k_norm", (2, 64)),
          ("hgrn_lb_logits", (2, 256)), ("hgrn_out_norm", (2, 256)), ("pool_w", (2, 4, 64, 64)), ("pool_scale", (2, 256)),
          ("mem_norm_g", (2, 1024)), ("mem_q_norm", (2, 64)), ("mem_k_norm", (2, 64)))
_SLAB_ROWS = 312


def pack_small(d):
    flat = jnp.concatenate([d[n].reshape(-1) for n, _ in _SMALL])
    return jnp.pad(flat, (0, _SLAB_ROWS * 128 - flat.shape[0])).reshape(_SLAB_ROWS, 128)


def unpack_small(slab):
    flat, out, off = slab.reshape(-1), {}, 0
    for n, shp in _SMALL:
        size = 1
        for e in shp:
            size *= e
        out[n] = flat[off:off + size].reshape(shp)
        off += size
    return out


def small_grads(g0, g1, lb_logits):
    st = lambda f: jnp.stack([f(g0), f(g1)])
    heads = lambda a: a.reshape(NH, HD).sum(0)
    p1 = jax.nn.sigmoid(lb_logits[1] - lb_logits[0])
    inside = (p1 > 0.0) & (p1 < 1.0 - 1e-6)
    dl1 = jnp.where(inside, g1["lb"][0] * p1 * (1.0 - p1), 0.0)
    diag = lambda a: jnp.stack([a.reshape(4, HD, 4, HD)[i, :, i, :] for i in range(4)])
    return dict(norm_g=st(lambda g: g["norm_g"][0]), fox_f_bias=st(lambda g: g["fox_f_bias"][0, :NH]),
                fox_q_norm=st(lambda g: heads(g["fox_q_norm"])), fox_k_norm=st(lambda g: heads(g["fox_k_norm"])),
                hgrn_lb_logits=jnp.stack([-dl1, dl1]), hgrn_out_norm=st(lambda g: g["hgrn_out_norm"][0]),
                pool_w=st(lambda g: diag(g["pool_wbd"])), pool_scale=st(lambda g: g["pool_scale"][0]),
                mem_norm_g=st(lambda g: g["mem_norm_g"][0]), mem_q_norm=st(lambda g: heads(g["mem_q_norm"])),
                mem_k_norm=st(lambda g: heads(g["mem_k_norm"])))


def kernel(x, mem, norm_g, w_in, fox_f_bias, fox_q_norm, fox_k_norm, hgrn_lb_logits, hgrn_out_norm, pool_w, pool_scale, mem_norm_g, mem_w_kv, mem_q_norm, mem_k_norm, w_out, loss_target, m_norm_g, m_w_in, m_fox_f_bias, m_fox_q_norm, m_fox_k_norm, m_hgrn_lb_logits, m_hgrn_out_norm, m_pool_w, m_pool_scale, m_mem_norm_g, m_mem_w_kv, m_mem_q_norm, m_mem_k_norm, m_w_out, v_norm_g, v_w_in, v_fox_f_bias, v_fox_q_norm, v_fox_k_norm, v_hgrn_lb_logits, v_hgrn_out_norm, v_pool_w, v_pool_scale, v_mem_norm_g, v_mem_w_kv, v_mem_q_norm, v_mem_k_norm, v_w_out):
    given = dict(norm_g=(norm_g, m_norm_g, v_norm_g), w_in=(w_in, m_w_in, v_w_in), fox_f_bias=(fox_f_bias, m_fox_f_bias, v_fox_f_bias),
                 fox_q_norm=(fox_q_norm, m_fox_q_norm, v_fox_q_norm), fox_k_norm=(fox_k_norm, m_fox_k_norm, v_fox_k_norm),
                 hgrn_lb_logits=(hgrn_lb_logits, m_hgrn_lb_logits, v_hgrn_lb_logits),
                 hgrn_out_norm=(hgrn_out_norm, m_hgrn_out_norm, v_hgrn_out_norm), pool_w=(pool_w, m_pool_w, v_pool_w),
                 pool_scale=(pool_scale, m_pool_scale, v_pool_scale), mem_norm_g=(mem_norm_g, m_mem_norm_g, v_mem_norm_g),
                 mem_w_kv=(mem_w_kv, m_mem_w_kv, v_mem_w_kv), mem_q_norm=(mem_q_norm, m_mem_q_norm, v_mem_q_norm),
                 mem_k_norm=(mem_k_norm, m_mem_k_norm, v_mem_k_norm), w_out=(w_out, m_w_out, v_w_out))
    order = ("norm_g", "w_in", "fox_f_bias", "fox_q_norm", "fox_k_norm", "hgrn_lb_logits", "hgrn_out_norm", "pool_w",
             "pool_scale", "mem_norm_g", "mem_w_kv", "mem_q_norm", "mem_k_norm", "w_out")

    w_in_full, w_out_full, w_kv_full = all_gather_rows([_cast(permute_cols(w_in)), _cast(w_out), _cast(mem_w_kv)], "weights")
    p = prepare_params(norm_g, w_in_full, fox_f_bias, fox_q_norm, fox_k_norm, hgrn_lb_logits, hgrn_out_norm, pool_w,
                       pool_scale, mem_norm_g, w_kv_full, mem_q_norm, mem_k_norm, w_out_full)

    loss_part, grad_x, g0, g1 = local_step(x, mem, loss_target, p)
    loss = lax.psum(loss_part, ("x", "y", "c"))

    res = {}
    core = lax.axis_index("c").astype(jnp.int32).reshape(1)

    names = ("w_in", "w_out", "mem_w_kv")
    part5 = []
    for name in names:
        nl, r, _ = given[name][0].shape
        g2 = jnp.stack([g0[name], g1[name]])
        part5.append(g2.reshape(nl, 4, 2, r, g2.shape[-1]))
    s4 = [add_core_halves(p5, th, core, name) for name, p5, th in zip(names, part5, exchange_cores(part5, "grads"))]
    for name, sl in zip(names, exchange_chips(s4, "grads")):
        w, m, v = given[name]
        nl, r, c = w.shape
        slots = sl.reshape(4, nl * r, sl.shape[-1])
        if name == "w_in":
            g = sum_slots(slots, name).reshape(nl, r, -1)
            out = adam_update(w.reshape(nl * r, c), m.reshape(nl * r, c), v.reshape(nl * r, c),
                              unpermute_cols(g).reshape(nl * r, c), name)
        else:
            out = adam_update(w.reshape(nl * r, c), m.reshape(nl * r, c), v.reshape(nl * r, c), slots, name, slots=True)
        res[name] = tuple(o.reshape(nl, r, c) for o in out)

    gsmall = pack_small(small_grads(g0, g1, hgrn_lb_logits))
    gathered = all_gather_rows([gsmall[None]], "small")[0].reshape(N_DEV, _SLAB_ROWS, 128)
    slabs = adam_update(*[pack_small({n: given[n][j] for n, _ in _SMALL}) for j in range(3)], gathered, "small", slots=True)
    small = [unpack_small(sl) for sl in slabs]
    for n, _ in _SMALL:
        res[n] = tuple(small[j][n] for j in range(4))

    return (loss, grad_x, *[res[n][0] for n in order], *[res[n][1] for n in order], *[res[n][2] for n in order],
            *[res[n][3] for n in order])
```

```python
import functools

import jax
import jax.numpy as jnp
from jax import lax
from jax.experimental import pallas as pl
from jax.experimental.pallas import tpu as pltpu

F32 = jnp.float32
BF = jnp.bfloat16
_MMT = BF

D_MODEL = 1024
GW = 256
HD = 64
NH = 4
CH = 64
N_MEM = 256
D_IN = 4100
D_INP = 4224
D_MIX = 1280
EPS = 1e-6
NEG_BIG = -1e30
LB_FLOOR = 1e-30
SCALE = HD ** -0.5
TQ = 256
TM = 512


def _token_tile(n):
    return TM if n % TM == 0 else TQ
N_DEV = 8
VMEM_LIMIT_BYTES = 56 * 1024 * 1024

ADAM_LR = 0.001
ADAM_B1 = 0.9
ADAM_B2 = 0.999
ADAM_EPS = 1e-08
ADAM_WD = 0.01
ADAM_STEP = 10

PIECES = (("A", 0, 768), ("B", 768, 768), ("C", 1536, 768), ("D", 2304, 256), ("E", 2560, 256),
          ("G", 2816, 1280), ("F", 4096, 128))
BWD_PIECES = (("A", 0, 768), ("Bq", 768, 256), ("Bk", 1024, 256), ("Bv", 1280, 256), ("C", 1536, 768),
              ("D", 2304, 256), ("E", 2560, 256), ("G", 2816, 1280), ("F", 4096, 128))
_ORIG = dict(fq=(0, 256), fk=(256, 512), fv=(512, 768), fg=(768, 1024), ff=(1024, 1028), sq=(1028, 1284),
             sk=(1284, 1540), sv=(1540, 1796), sg=(1796, 2052), hq=(2052, 2308), hf=(2308, 2564),
             hi=(2564, 2820), hg=(2820, 3076), pv=(3076, 3332), pg=(3332, 3588), mq=(3588, 3844), mg=(3844, 4100))
_PERM_ORDER = ("fq", "fk", "fv", "sq", "sk", "sv", "hq", "hf", "hi", "pv", "mq", "fg", "sg", "hg", "pg", "mg", "ff")
_ORIG_ORDER = ("fq", "fk", "fv", "fg", "ff", "sq", "sk", "sv", "sg", "hq", "hf", "hi", "hg", "pv", "pg", "mq", "mg")


def permute_cols(w):
    parts = [w[..., _ORIG[n][0]:_ORIG[n][1]] for n in _PERM_ORDER]
    parts.append(jnp.zeros(w.shape[:-1] + (D_INP - D_IN,), w.dtype))
    return jnp.concatenate(parts, axis=-1)


def unpermute_cols(g):
    start, off = {}, 0
    for n in _PERM_ORDER:
        start[n] = off
        off += _ORIG[n][1] - _ORIG[n][0]
    return jnp.concatenate([g[..., start[n]:start[n] + _ORIG[n][1] - _ORIG[n][0]] for n in _ORIG_ORDER], axis=-1)


def _cast(a):
    return a.astype(_MMT)


def _dg(a, b, ca, cb):
    return lax.dot_general(a, b, (((ca,), (cb,)), ((), ())), preferred_element_type=F32)


@jax.custom_vjp
def mm(a, b):
    return _dg(_cast(a), _cast(b), 1, 0)


@jax.custom_vjp
def mm_nt(a, b):
    return _dg(_cast(a), _cast(b), 1, 1)


@jax.custom_vjp
def mm_tn(a, b):
    return _dg(_cast(a), _cast(b), 0, 0)


mm.defvjp(lambda a, b: (mm(a, b), (a, b)),
          lambda r, g: (mm_nt(g, r[1]).astype(r[0].dtype), mm_tn(r[0], g).astype(r[1].dtype)))
mm_nt.defvjp(lambda a, b: (mm_nt(a, b), (a, b)),
             lambda r, g: (mm(g, r[1]).astype(r[0].dtype), mm_tn(g, r[0]).astype(r[1].dtype)))
mm_tn.defvjp(lambda a, b: (mm_tn(a, b), (a, b)),
             lambda r, g: (mm_nt(r[1], g).astype(r[0].dtype), mm(r[0], g).astype(r[1].dtype)))


def _split(a):
    hi = a.astype(_MMT)
    lo = (a - hi.astype(F32)).astype(_MMT)
    return hi, lo


@jax.custom_vjp
def xr(a, c, ct):
    hi, lo = _split(a)
    cc = _cast(c)
    return _dg(hi, cc, 1, 0) + _dg(lo, cc, 1, 0)


@jax.custom_vjp
def xl(c, ct, a):
    hi, lo = _split(a)
    cc = _cast(c)
    return _dg(cc, hi, 1, 0) + _dg(cc, lo, 1, 0)


xr.defvjp(lambda a, c, ct: (xr(a, c, ct), (c, ct)),
          lambda r, g: (xr(g, r[1], r[0]), jnp.zeros_like(r[0]), jnp.zeros_like(r[1])))
xl.defvjp(lambda c, ct, a: (xl(c, ct, a), (c, ct)),
          lambda r, g: (jnp.zeros_like(r[0]), jnp.zeros_like(r[1]), xl(r[1], r[0], g)))


def _iota(shape, dim):
    return lax.broadcasted_iota(jnp.int32, shape, dim)


def _hmask(h, n=GW):
    lane = _iota((1, n), 1)
    return ((lane >= h * HD) & (lane < (h + 1) * HD)).astype(F32)


def _bdmask(n=GW):
    return ((_iota((n, n), 0) >> 6) == (_iota((n, n), 1) >> 6)).astype(F32)


def _tri(n, kind="le"):
    r, c = _iota((n, n), 0), _iota((n, n), 1)
    return {"le": c <= r, "ge": c >= r, "gt": c > r, "lt": c < r}[kind].astype(F32)


def _onehot_lane(h, n=128):
    return (_iota((1, n), 1) == h).astype(F32)


def _logsig(x):
    return jnp.minimum(x, 0.0) - jnp.log1p(jnp.exp(-jnp.abs(x)))


def _sigmoid(x):
    return 0.5 * (jnp.tanh(0.5 * x) + 1.0)


def _silu(x):
    return x * _sigmoid(x)


def _rms(x, g):
    return x * lax.rsqrt(jnp.mean(x * x, axis=-1, keepdims=True) + EPS) * g


def _headrms(x, w, bd64):
    ms = xr(x * x, bd64, bd64)
    return x * lax.rsqrt(ms + EPS) * w


def _call(body, name, grid, in_specs, out_specs, out_shape, scratch=()):
    return pl.pallas_call(
        body, name=name, grid=grid, in_specs=in_specs, out_specs=out_specs, out_shape=out_shape,
        scratch_shapes=list(scratch),
        compiler_params=pltpu.CompilerParams(dimension_semantics=("arbitrary",) * len(grid),
                                             vmem_limit_bytes=VMEM_LIMIT_BYTES))


def _sds(shape, dtype=F32):
    return jax.ShapeDtypeStruct(shape, dtype)


def _acc(ref, val, first):
    @pl.when(first)
    def _():
        ref[...] = val

    @pl.when(jnp.logical_not(first))
    def _():
        ref[...] += val


def inproj_fwd(x2, g, w, layer, tag):
    t = x2.shape[0]
    TQ = _token_tile(t)

    def body(x_ref, g_ref, w_ref, ht_ref, *outs):
        h = _rms(x_ref[...], g_ref[...])
        hb = _cast(h)
        ht_ref[...] = _cast(h.T)
        for (_, c0, wd), o in zip(PIECES, outs):
            o[...] = _dg(hb, _cast(w_ref[:, c0:c0 + wd]), 1, 0)

    return _call(
        body, f"inproj_fwd_{tag}", (t // TQ,),
        [pl.BlockSpec((TQ, D_MODEL), lambda i: (i, 0)),
         pl.BlockSpec((None, 1, D_MODEL), lambda i: (layer, 0, 0)),
         pl.BlockSpec((None, D_MODEL, D_INP), lambda i: (layer, 0, 0))],
        [pl.BlockSpec((D_MODEL, TQ), lambda i: (0, i))] + [pl.BlockSpec((TQ, wd), lambda i: (i, 0)) for _, _, wd in PIECES],
        [_sds((D_MODEL, t), _MMT)] + [_sds((t, wd)) for _, _, wd in PIECES],
    )(x2, g, w)


def inproj_bwd_dx(x2, g, w, dy, dpieces, layer, tag):
    t = x2.shape[0]
    TQ = _token_tile(t)

    def body(x_ref, g_ref, w_ref, dy_ref, *rest):
        dps, (dx_ref, dg_ref) = rest[:len(BWD_PIECES)], rest[len(BWD_PIECES):]
        dh = None
        for (_, c0, wd), dp in zip(BWD_PIECES, dps):
            part = _dg(_cast(dp[...]), _cast(w_ref[:, c0:c0 + wd]), 1, 1)
            dh = part if dh is None else dh + part
        _, vjp = jax.vjp(_rms, x_ref[...], g_ref[...])
        dx, dg = vjp(dh)
        dx_ref[...] = dy_ref[...] + dx
        _acc(dg_ref, dg, pl.program_id(0) == 0)

    return _call(
        body, f"inproj_bwd_dx_{tag}", (t // TQ,),
        [pl.BlockSpec((TQ, D_MODEL), lambda i: (i, 0)),
         pl.BlockSpec((None, 1, D_MODEL), lambda i: (layer, 0, 0)),
         pl.BlockSpec((None, D_MODEL, D_INP), lambda i: (layer, 0, 0), pipeline_mode=pl.Buffered(1)),
         pl.BlockSpec((TQ, D_MODEL), lambda i: (i, 0))] + [pl.BlockSpec((TQ, wd), lambda i: (i, 0)) for _, _, wd in BWD_PIECES],
        [pl.BlockSpec((TQ, D_MODEL), lambda i: (i, 0)), pl.BlockSpec((1, D_MODEL), lambda i: (0, 0))],
        [_sds((t, D_MODEL)), _sds((1, D_MODEL))],
    )(x2, g, w, dy, *dpieces)


def matmul_acc(at, b, tag):
    m, t = at.shape
    n = b.shape[1]
    tn = {1280: 640, 768: 768}.get(n, n)
    tk = 2048 if t % 2048 == 0 else (512 if t % 512 == 0 else TQ)

    def body(a_ref, b_ref, o_ref):
        _acc(o_ref, _dg(_cast(a_ref[...]), _cast(b_ref[...]), 1, 0), pl.program_id(1) == 0)

    return _call(
        body, f"matmul_acc_{tag}", (n // tn, t // tk),
        [pl.BlockSpec((m, tk), lambda j, i: (0, i)), pl.BlockSpec((tk, tn), lambda j, i: (i, j))],
        pl.BlockSpec((m, tn), lambda j, i: (0, j)),
        _sds((m, n)),
    )(at, b)


def _fox_prep_fn(q, k, ff, qw, kw, bias, carry, bd64, tri, trit, last):
    qn = _headrms(q, qw, bd64)
    kn = _headrms(k, kw, bd64)
    lf = _logsig(ff + bias)
    c = xl(tri, trit, lf) + carry
    return qn, kn, c, jnp.sum(c * last, axis=0, keepdims=True)


def _prep_consts(tq):
    return _bdmask() * (1.0 / HD), _tri(tq), _tri(tq, "ge"), (_iota((tq, 1), 0) == tq - 1).astype(F32)


def fox_prep_fwd(pa, pf, qw, kw, bias, bl, s, layer, tag):
    TQ = _token_tile(s)
    nq = s // TQ

    def body(q_ref, k_ref, v_ref, f_ref, qw_ref, kw_ref, b_ref, qn_ref, kn_ref, vb_ref, cq_ref, ck_ref, carry):
        @pl.when(pl.program_id(1) == 0)
        def _():
            carry[...] = jnp.zeros_like(carry)

        qn, kn, c, cl = _fox_prep_fn(q_ref[...], k_ref[...], f_ref[...], qw_ref[...], kw_ref[...], b_ref[...],
                                     carry[...], *_prep_consts(TQ))
        carry[...] = cl
        qn_ref[...] = _cast(qn)
        kn_ref[...] = _cast(kn)
        vb_ref[...] = _cast(v_ref[...])
        cq_ref[...] = c
        ck_ref[...] = c.T[0:8, :]

    tok = lambda j: pl.BlockSpec((TQ, GW), lambda b, i: (b * nq + i, j))
    par = lambda n: pl.BlockSpec((None, 1, n), lambda b, i: (layer, 0, 0))
    return _call(
        body, f"fox_prep_fwd_{tag}", (bl, nq),
        [tok(0), tok(1), tok(2), pl.BlockSpec((TQ, 128), lambda b, i: (b * nq + i, 0)), par(GW), par(GW), par(128)],
        [tok(0), tok(0), tok(0), pl.BlockSpec((TQ, 128), lambda b, i: (b * nq + i, 0)),
         pl.BlockSpec((None, 8, TQ), lambda b, i: (b, 0, i))],
        [_sds((bl * s, GW), _MMT)] * 3 + [_sds((bl * s, 128)), _sds((bl, 8, s))],
        [pltpu.VMEM((1, 128), F32)],
    )(pa, pa, pa, pf, qw, kw, bias)


def fox_prep_bwd(pa, pf, qw, kw, bias, cq, dqn, dkn, dv, dck, bl, s, layer, tag):
    TQ = _token_tile(s)
    nq = s // TQ

    def body(q_ref, k_ref, f_ref, qw_ref, kw_ref, b_ref, cq_ref, cprev_ref, dqn_ref, dkn_ref, dv_ref, dck_ref,
             da_ref, df_ref, dqw_ref, dkw_ref, db_ref, dcarry):
        i = pl.program_id(1)
        first = jnp.logical_and(pl.program_id(0) == 0, i == 0)

        @pl.when(i == 0)
        def _():
            dcarry[...] = jnp.zeros_like(dcarry)

        last = (_iota((TQ, 1), 0) == TQ - 1).astype(F32)
        carry_in = jnp.where(i == nq - 1, 0.0, jnp.sum(cprev_ref[...] * last, axis=0, keepdims=True))
        consts = _prep_consts(TQ)
        _, vjp = jax.vjp(lambda *a: _fox_prep_fn(*a, *consts), q_ref[...], k_ref[...], f_ref[...], qw_ref[...],
                         kw_ref[...], b_ref[...], carry_in)
        dc = dck_ref[...].T
        dq, dk, dff, dqw, dkw, dbias, dcin = vjp((dqn_ref[...], dkn_ref[...], dc, dcarry[...]))
        dcarry[...] = dcin
        da_ref[:, 0:GW] = dq
        da_ref[:, GW:2 * GW] = dk
        da_ref[:, 2 * GW:3 * GW] = dv_ref[...]
        df_ref[...] = dff
        _acc(dqw_ref, dqw, first)
        _acc(dkw_ref, dkw, first)
        _acc(db_ref, dbias, first)

    rv = lambda b, i: b * nq + (nq - 1 - i)
    tok = lambda j: pl.BlockSpec((TQ, GW), lambda b, i: (rv(b, i), j))
    tok0 = pl.BlockSpec((TQ, GW), lambda b, i: (rv(b, i), 0))
    t128 = pl.BlockSpec((TQ, 128), lambda b, i: (rv(b, i), 0))
    prev = pl.BlockSpec((TQ, 128), lambda b, i: (jnp.maximum(rv(b, i) - 1, 0), 0))
    par = lambda n: pl.BlockSpec((None, 1, n), lambda b, i: (layer, 0, 0))
    acc = lambda n: pl.BlockSpec((1, n), lambda b, i: (0, 0))
    return _call(
        body, f"fox_prep_bwd_{tag}", (bl, nq),
        [tok(0), tok(1), t128, par(GW), par(GW), par(128), t128, prev, tok0, tok0, tok0,
         pl.BlockSpec((None, 128, TQ), lambda b, i: (b, 0, nq - 1 - i))],
        [pl.BlockSpec((TQ, 3 * GW), lambda b, i: (rv(b, i), 0)), t128, acc(GW), acc(GW), acc(128)],
        [_sds((bl * s, 3 * GW)), _sds((bl * s, 128)), _sds((1, GW)), _sds((1, GW)), _sds((1, 128))],
        [pltpu.VMEM((1, 128), F32)],
    )(pa, pa, pf, qw, kw, bias, cq, cq, dqn, dkn, dv, dck)


def _lane_pick(x, h):
    return jnp.sum(x * _onehot_lane(h), axis=-1, keepdims=True)


TA_BIG = 256
TA_FWD = 512
TK_FOX = 512
TK_SB = 256


def _stack_heads(x, scale=1.0):
    return _cast(jnp.concatenate([x * (_hmask(h) * scale) for h in range(NH)], axis=0))


def _stack_cols(x):
    return jnp.concatenate([_lane_pick(x, h) for h in range(NH)], axis=0)


def _spread_heads(col):
    ta = col.shape[0] // NH
    return sum(col[h * ta:(h + 1) * ta] * _hmask(h) for h in range(NH))


def _lanes_cat(w):
    ta = w.shape[0] // NH
    return jnp.concatenate([w[h * ta:(h + 1) * ta] for h in range(NH)], axis=1)


def _mask_stack(x):
    return _cast(jnp.concatenate([x * _hmask(h).astype(x.dtype) for h in range(NH)], axis=0))


def _stack_rows(i, ta):
    return i * ta + (_iota((NH * ta, 1), 0) & (ta - 1))


def _n_key_tiles(i, tk, ta):
    assert ta <= tk and tk % ta == 0, "a query tile's diagonal must lie inside one key tile (only the last key tile is masked)"
    return lax.shift_right_logical(i * ta, tk.bit_length() - 1) + 1


def fox_attn_fwd(qn, kn, vb, cq, ck, bl, s, tag):
    TA, TK = min(TA_FWD, s), min(TK_FOX, s)
    nq, SROWS = s // TA, NH * TA

    def body(q_ref, k_ref, v_ref, cq_ref, ck_ref, o_ref, lse_ref, acc, vst):
        i = pl.program_id(1)

        @pl.when(i == 0)
        def _():
            _fill_stacked(vst, v_ref, s, TK)

        qs = _stack_heads(q_ref[...].astype(F32), SCALE)
        cqs = _stack_cols(cq_ref[...])
        row = _stack_rows(i, TA)
        acc[...] = jnp.zeros_like(acc)

        def step(j, ml):
            m, l = ml
            ks = pl.ds(pl.multiple_of(j * TK, TK), TK)
            ckb = jnp.concatenate([jnp.broadcast_to(ck_ref[h:h + 1, ks], (TA, TK)) for h in range(NH)], axis=0)
            sc = _dg(qs, k_ref[ks, :], 1, 1) + cqs - ckb
            col = j * TK + _iota((1, TK), 1)
            sc = jnp.where(col <= row, sc, NEG_BIG)
            m_new = jnp.maximum(m, jnp.max(sc, axis=-1, keepdims=True))
            alpha = jnp.exp(m - m_new)
            p = jnp.exp(sc - m_new)
            vs = vst[pl.ds(pl.multiple_of(j * NH * TK, NH * TK), NH * TK), :]
            acc[...] = _spread_heads(alpha) * acc[...] + _dg(_lanes_cat(_cast(p)), vs, 1, 0)
            return m_new, alpha * l + jnp.sum(p, axis=-1, keepdims=True)

        m, l = lax.fori_loop(0, _n_key_tiles(i, TK, TA), step, (jnp.full((SROWS, 1), NEG_BIG, F32), jnp.zeros((SROWS, 1), F32)))
        o_ref[...] = acc[...] / _spread_heads(l)
        lse_h = m + jnp.log(l)
        lse_ref[...] = sum(lse_h[h * TA:(h + 1) * TA] * _onehot_lane(h) for h in range(NH))

    tok = pl.BlockSpec((TA, GW), lambda b, i: (b * nq + i, 0))
    seq = pl.BlockSpec((s, GW), lambda b, i: (b, 0))
    t128 = pl.BlockSpec((TA, 128), lambda b, i: (b * nq + i, 0))
    return _call(
        body, f"fox_attn_fwd_{tag}", (bl, nq),
        [tok, seq, seq, t128, pl.BlockSpec((None, 8, s), lambda b, i: (b, 0, 0))],
        [tok, t128], [_sds((bl * s, GW)), _sds((bl * s, 128))],
        [pltpu.VMEM((TA, GW), F32), pltpu.VMEM((NH * s, GW), _MMT)],
    )(qn, kn, vb, cq, ck)


def fox_attn_bwd(qn, kn, vb, cq, ck, lse, do, bl, s, tag):
    TA, TK = min(TA_BIG, s), min(TK_FOX, s)
    nq, SROWS = s // TA, NH * TA

    def body(q_ref, k_ref, v_ref, cq_ref, ck_ref, lse_ref, do_ref, dq_ref, dk_ref, dv_ref, dck_ref, dqa, p_s, dp_s, kst):
        i = pl.program_id(1)

        @pl.when(i == 0)
        def _():
            dk_ref[...] = jnp.zeros_like(dk_ref)
            dv_ref[...] = jnp.zeros_like(dv_ref)
            dck_ref[...] = jnp.zeros_like(dck_ref)
            _fill_stacked(kst, k_ref, s, TK)

        qs = _stack_heads(q_ref[...].astype(F32), SCALE)
        dos = _stack_heads(do_ref[...])
        cqs, lses = _stack_cols(cq_ref[...]), _stack_cols(lse_ref[...])
        row = _stack_rows(i, TA)
        dqa[...] = jnp.zeros_like(dqa)
        nk = _n_key_tiles(i, TK, TA)

        def probs(j, delta):
            ks = pl.ds(pl.multiple_of(j * TK, TK), TK)
            ckb = jnp.concatenate([jnp.broadcast_to(ck_ref[h:h + 1, ks], (TA, TK)) for h in range(NH)], axis=0)
            sc = _dg(qs, k_ref[ks, :], 1, 1) + cqs - ckb
            col = j * TK + _iota((1, TK), 1)
            p = jnp.where(col <= row, jnp.exp(sc - lses), 0.0)
            dp = _dg(dos, v_ref[ks, :], 1, 1)
            p_s[:, ks] = p
            dp_s[:, ks] = dp
            return delta + jnp.sum(p * dp, axis=-1, keepdims=True)

        delta = lax.fori_loop(0, nk, probs, jnp.zeros((SROWS, 1), F32))

        def step(j, carry):
            ks = pl.ds(pl.multiple_of(j * TK, TK), TK)
            p = p_s[:, ks]
            ds = p * (dp_s[:, ks] - delta)
            dsb = _cast(ds)
            dqa[...] += _dg(_lanes_cat(dsb), kst[pl.ds(pl.multiple_of(j * NH * TK, NH * TK), NH * TK), :], 1, 0) * SCALE
            dk_ref[ks, :] += _dg(dsb, qs, 0, 0)
            dv_ref[ks, :] += _dg(_cast(p), dos, 0, 0)
            for h in range(NH):
                dck_ref[h:h + 1, ks] -= jnp.sum(ds[h * TA:(h + 1) * TA], axis=0, keepdims=True)
            return carry

        lax.fori_loop(0, nk, step, 0)
        dq_ref[...] = dqa[...]

    tok = pl.BlockSpec((TA, GW), lambda b, i: (b * nq + i, 0))
    seq = pl.BlockSpec((s, GW), lambda b, i: (b, 0))
    t128 = pl.BlockSpec((TA, 128), lambda b, i: (b * nq + i, 0))
    return _call(
        body, f"fox_attn_bwd_{tag}", (bl, nq),
        [tok, seq, seq, t128, pl.BlockSpec((None, 8, s), lambda b, i: (b, 0, 0)), t128, tok],
        [tok, seq, seq, pl.BlockSpec((None, 128, s), lambda b, i: (b, 0, 0))],
        [_sds((bl * s, GW)), _sds((bl * s, GW)), _sds((bl * s, GW)), _sds((bl, 128, s))],
        [pltpu.VMEM((TA, GW), F32), pltpu.VMEM((SROWS, s), F32), pltpu.VMEM((SROWS, s), F32), pltpu.VMEM((NH * s, GW), _MMT)],
    )(qn, kn, vb, cq, ck, lse, do)


def _sb_block(qh, kb, valid, upper, r_carry):
    z = _dg(qh, kb, 1, 1)
    ls = _logsig(z)
    lom = ls - z if valid is None else jnp.where(valid, ls - z, 0.0)
    between = xr(lom, upper, upper) + r_carry
    w = jnp.exp(ls + between)
    return ls, lom, (w if valid is None else jnp.where(valid, w, 0.0))


def _fill_stacked(dst, src_ref, s, tk):
    for j in range(s // tk):
        dst[j * NH * tk:(j + 1) * NH * tk, :] = _mask_stack(src_ref[j * tk:(j + 1) * tk, :])


def sb_attn_fwd(pb, bl, s, tag):
    TA, TK = TA_BIG, TK_SB
    nq, SROWS = s // TA, NH * TA

    def body(q_ref, k_ref, v_ref, o_ref, acc, vst):
        i = pl.program_id(1)

        @pl.when(i == 0)
        def _():
            _fill_stacked(vst, v_ref, s, TK)

        qs = _stack_heads(q_ref[...], SCALE)
        upper = _tri(TK, "lt")
        last = _n_key_tiles(i, TK, TA) - 1

        def step(j, r, valid):
            ks = pl.ds(pl.multiple_of(j * TK, TK), TK)
            _, lom, w = _sb_block(qs, _cast(k_ref[ks, :]), valid, upper, r)
            acc[...] += _dg(_lanes_cat(_cast(w)), vst[pl.ds(pl.multiple_of(j * NH * TK, NH * TK), NH * TK), :], 1, 0)
            return r + jnp.sum(lom, axis=-1, keepdims=True)

        acc[...] = jnp.zeros_like(acc)
        r = step(last, jnp.zeros((SROWS, 1), F32), last * TK + _iota((1, TK), 1) < _stack_rows(i, TA))
        lax.fori_loop(0, last, lambda jj, r: step(last - 1 - jj, r, None), r)
        o_ref[...] = acc[...]

    tok = lambda j: pl.BlockSpec((TA, GW), lambda b, i: (b * nq + i, j))
    seq = lambda j: pl.BlockSpec((s, GW), lambda b, i: (b, j))
    return _call(
        body, f"sb_attn_fwd_{tag}", (bl, nq), [tok(0), seq(1), seq(2)],
        pl.BlockSpec((TA, GW), lambda b, i: (b * nq + i, 0)), _sds((bl * s, GW)),
        [pltpu.VMEM((TA, GW), F32), pltpu.VMEM((NH * s, GW), _MMT)],
    )(pb, pb, pb)


def sb_attn_bwd(pb, do, bl, s, tag):
    TA, TK = TA_BIG, TK_SB
    nq, SROWS = s // TA, NH * TA

    def body(q_ref, k_ref, v_ref, do_ref, dq_ref, dk_ref, dv_ref, dqa, sig_s, nsig_s, w_s, g_s, kst):
        i = pl.program_id(1)

        @pl.when(i == 0)
        def _():
            dk_ref[...] = jnp.zeros_like(dk_ref)
            dv_ref[...] = jnp.zeros_like(dv_ref)
            _fill_stacked(kst, k_ref, s, TK)

        qs = _stack_heads(q_ref[...], SCALE)
        dos = _stack_heads(do_ref[...])
        upper = _tri(TK, "lt")
        before = _tri(TK, "gt")
        dqa[...] = jnp.zeros_like(dqa)
        last = _n_key_tiles(i, TK, TA) - 1
        diag = last * TK + _iota((1, TK), 1) < _stack_rows(i, TA)

        def weights(j, r, valid):
            ks = pl.ds(pl.multiple_of(j * TK, TK), TK)
            ls, lom, w = _sb_block(qs, _cast(k_ref[ks, :]), valid, upper, r)
            sig_s[:, ks] = _cast(jnp.exp(ls))
            nsig_s[:, ks] = _cast(jnp.exp(lom))
            w_s[:, ks] = _cast(w)
            g_s[:, ks] = _dg(dos, _cast(v_ref[ks, :]), 1, 1) * w
            return r + jnp.sum(lom, axis=-1, keepdims=True)

        r = weights(last, jnp.zeros((SROWS, 1), F32), diag)
        lax.fori_loop(0, last, lambda jj, r: weights(last - 1 - jj, r, None), r)

        def step(j, cpre, valid):
            ks = pl.ds(pl.multiple_of(j * TK, TK), TK)
            g = g_s[:, ks]
            pre = cpre + xr(g, before, before)
            dz = g * nsig_s[:, ks].astype(F32) - sig_s[:, ks].astype(F32) * pre
            dzb = _cast(dz if valid is None else jnp.where(valid, dz, 0.0))
            dqa[...] += _dg(_lanes_cat(dzb), kst[pl.ds(pl.multiple_of(j * NH * TK, NH * TK), NH * TK), :], 1, 0) * SCALE
            dk_ref[ks, :] += _dg(dzb, qs, 0, 0)
            dv_ref[ks, :] += _dg(w_s[:, ks], dos, 0, 0)
            return cpre + jnp.sum(g, axis=-1, keepdims=True)

        cpre = lax.fori_loop(0, last, lambda j, c: step(j, c, None), jnp.zeros((SROWS, 1), F32))
        step(last, cpre, diag)
        dq_ref[...] = dqa[...]

    tok = lambda j: pl.BlockSpec((TA, GW), lambda b, i: (b * nq + i, j))
    seq = lambda j: pl.BlockSpec((s, GW), lambda b, i: (b, j))
    return _call(
        body, f"sb_attn_bwd_{tag}", (bl, nq), [tok(0), seq(1), seq(2), tok(0)],
        [tok(0), seq(0), seq(0)], [_sds((bl * s, GW))] * 3,
        [pltpu.VMEM((TA, GW), F32), pltpu.VMEM((SROWS, s), _MMT), pltpu.VMEM((SROWS, s), _MMT),
         pltpu.VMEM((SROWS, s), _MMT), pltpu.VMEM((SROWS, s), F32), pltpu.VMEM((NH * s, GW), _MMT)],
    )(pb, pb, pb, do)


def _hgrn_consts():
    r, c = _iota((CH, CH), 0), _iota((CH, CH), 1)
    rr = _iota((CH, 1), 0)
    tri = (c <= r).astype(F32)
    lv = []
    for m in (8, 4, 2, 1):
        up = ((rr & (2 * m - 1)) >= m).astype(F32)
        selq = (((r & (2 * m - 1)) >= m) & (c == (r & ~(m - 1)) - 1)).astype(F32)
        selk = (((r & (2 * m - 1)) < m) & (c == (r & ~(m - 1)) + m - 1)).astype(F32)
        pm = (((r & ~(2 * m - 1)) == (c & ~(2 * m - 1))) & ((r & (2 * m - 1)) >= m) & ((c & (2 * m - 1)) < m)).astype(F32)
        lv.append((up, 1.0 - up, selq, selq.T, selk, selk.T, jnp.concatenate([pm] * NH, axis=0)))
    hm4 = lambda n: (((_iota((NH, 1, n), 2) & (GW - 1)) >> 6) == _iota((NH, 1, n), 0)).astype(F32)
    return dict(tri=tri, trit=tri.T, rr=rr, lv=lv, bd=_bdmask(), bd64=_bdmask() * (1.0 / HD),
                hm4={GW: hm4(GW), 3 * GW: hm4(3 * GW)})


def _hgrn_chunk_fn(hq, hf, hi, lb, wn, st, cs):
    q = _silu(hq)
    log_lb = jnp.log(jnp.maximum(lb, LB_FLOOR))
    a, bb = log_lb, jnp.log1p(-lb) + _logsig(hf)
    g = jnp.maximum(a, bb) + jnp.log1p(jnp.exp(-jnp.abs(a - bb)))
    k = (1.0 - lb) * _sigmoid(-hf)
    v = hi
    rr = cs["rr"]
    b = xl(cs["tri"], cs["trit"], g)
    row_of = lambda n: jnp.sum(b * (rr == n).astype(F32), axis=0, keepdims=True)
    o = mm_nt(q * jnp.exp(b), st)
    qs, ks = [], []
    for ib in (1, 2, 3):
        ref = row_of(16 * ib - 1)
        inq = ((rr >= 16 * ib) & (rr < 16 * ib + 16)).astype(F32)
        ink = (rr < 16 * ib).astype(F32)
        qs.append(q * jnp.exp((b - ref) * inq) * inq)
        ks.append(k * jnp.exp((ref - b) * ink) * ink)
    qcat, kcat = jnp.concatenate(qs, axis=1), jnp.concatenate(ks, axis=1)
    lvl = []
    for up, lo, selq, selqt, selk, selkt, pm in cs["lv"]:
        qe = q * jnp.exp((b - xl(selq, selqt, b)) * up) * up
        ke = k * jnp.exp((xl(selk, selkt, b) - b) * lo) * lo
        lvl.append((qe, ke, pm))
    stack = lambda x: (x[None] * cs["hm4"][x.shape[1]]).reshape(NH * CH, x.shape[1])
    a_all = mm_nt(stack(qcat), kcat)
    for qe, ke, pm4 in lvl:
        a_all = a_all + mm_nt(stack(qe), ke) * pm4
    o = o + jnp.sum(mm(a_all, v).reshape(NH, CH, GW) * cs["hm4"][GW], axis=0)
    o = o + xr(q * k, cs["bd"], cs["bd"]) * v
    b_last = row_of(CH - 1)
    st_new = st * jnp.exp(b_last) + mm_tn(v, k * jnp.exp(b_last - b)) * cs["bd"]
    return _headrms(o, wn, cs["bd64"]), st_new


def hgrn_fwd(pc, lb, wn, bl, s, layer, tag):
    nc = s // CH

    def body(q_ref, f_ref, i_ref, lb_ref, wn_ref, o_ref, st_ref, st):
        @pl.when(pl.program_id(0) == 0)
        def _():
            st[...] = jnp.zeros_like(st)

        cs = _hgrn_consts()
        for b in range(bl):
            st_ref[b] = st[b]
            o, st_new = _hgrn_chunk_fn(q_ref[b], f_ref[b], i_ref[b], lb_ref[...], wn_ref[...], st[b], cs)
            o_ref[b] = o
            st[b] = st_new

    tok = lambda j: pl.BlockSpec((bl, CH, GW), lambda c: (0, c, j))
    par = pl.BlockSpec((None, 1, GW), lambda c: (layer, 0, 0))
    pc3 = pc.reshape(bl, s, 3 * GW)
    o, states = _call(
        body, f"hgrn_fwd_{tag}", (nc,), [tok(0), tok(1), tok(2), par, par],
        [tok(0), pl.BlockSpec((bl, None, GW, GW), lambda c: (0, c, 0, 0))],
        [_sds((bl, s, GW)), _sds((bl, nc, GW, GW))],
        [pltpu.VMEM((bl, GW, GW), F32)],
    )(pc3, pc3, pc3, lb, wn)
    return o.reshape(bl * s, GW), states


def hgrn_bwd(pc, lb, wn, states, do, bl, s, layer, tag):
    nc = s // CH

    def body(q_ref, f_ref, i_ref, lb_ref, wn_ref, st_ref, do_ref, dc_ref, dlb_ref, dwn_ref, dst):
        c = pl.program_id(0)

        @pl.when(c == 0)
        def _():
            dst[...] = jnp.zeros_like(dst)

        cs = _hgrn_consts()
        dlb_sum = dwn_sum = None
        for b in range(bl):
            _, vjp = jax.vjp(lambda *a: _hgrn_chunk_fn(*a, cs), q_ref[b], f_ref[b], i_ref[b], lb_ref[...],
                             wn_ref[...], st_ref[b])
            dq, df, di, dlb, dwn, dst_in = vjp((do_ref[b], dst[b]))
            dst[b] = dst_in
            dc_ref[b, :, 0:GW] = dq
            dc_ref[b, :, GW:2 * GW] = df
            dc_ref[b, :, 2 * GW:3 * GW] = di
            dlb_sum = dlb if dlb_sum is None else dlb_sum + dlb
            dwn_sum = dwn if dwn_sum is None else dwn_sum + dwn
        _acc(dlb_ref, dlb_sum, c == 0)
        _acc(dwn_ref, dwn_sum, c == 0)

    tok = lambda j: pl.BlockSpec((bl, CH, GW), lambda c: (0, nc - 1 - c, j))
    par = pl.BlockSpec((None, 1, GW), lambda c: (layer, 0, 0))
    acc = pl.BlockSpec((1, GW), lambda c: (0, 0))
    pc3 = pc.reshape(bl, s, 3 * GW)
    dc, dlb, dwn = _call(
        body, f"hgrn_bwd_{tag}", (nc,),
        [tok(0), tok(1), tok(2), par, par, pl.BlockSpec((bl, None, GW, GW), lambda c: (0, nc - 1 - c, 0, 0)), tok(0)],
        [pl.BlockSpec((bl, CH, 3 * GW), lambda c: (0, nc - 1 - c, 0)), acc, acc],
        [_sds((bl, s, 3 * GW)), _sds((1, GW)), _sds((1, GW))],
        [pltpu.VMEM((bl, GW, GW), F32)],
    )(pc3, pc3, pc3, lb, wn, states, do.reshape(bl, s, GW))
    return dc.reshape(bl * s, 3 * GW), dlb, dwn


def _shift_rows(x, k, up):
    n = x.shape[0]
    rr = _iota((n, 1), 0)
    if up:
        return jnp.where(rr < n - k, pltpu.roll(x, n - k, 0), 0.0)
    return jnp.where(rr >= k, pltpu.roll(x, k, 0), 0.0)


def _window_sums(x, up):
    s2 = x + _shift_rows(x, 1, up)
    s4 = s2 + _shift_rows(s2, 2, up)
    s8 = s4 + _shift_rows(s4, 4, up)
    s16 = s8 + _shift_rows(s8, 8, up)
    return s2, s4, s8, s16


def _pool_div(n):
    pos = (_iota((n, 1), 0) + 1).astype(F32)
    return [jnp.minimum(pos, float(w)) for w in (2, 4, 8, 16)]


def _pool_mix(sums, scaled):
    out = None
    for gi, sw in enumerate(sums):
        part = (sw if scaled is None else sw / scaled[gi]) * _hmask(gi)
        out = part if out is None else out + part
    return out


def pool_fwd(pd, wbd, scale, bl, s, layer, tag):
    def body(u_ref, w_ref, sc_ref, o_ref):
        u = u_ref[...]
        pm = _pool_mix(_window_sums(u, False), _pool_div(s)) - u
        o_ref[...] = _dg(_cast(pm), _cast(w_ref[...]), 1, 0) * sc_ref[...]

    seq = pl.BlockSpec((s, GW), lambda b: (b, 0))
    return _call(
        body, f"pool_fwd_{tag}", (bl,),
        [seq, pl.BlockSpec((None, GW, GW), lambda b: (layer, 0, 0)), pl.BlockSpec((None, 1, GW), lambda b: (layer, 0, 0))],
        seq, _sds((bl * s, GW)),
    )(pd, wbd, scale)


def pool_bwd(pd, wbd, scale, do, bl, s, layer, tag):
    def body(u_ref, w_ref, sc_ref, do_ref, du_ref, dw_ref, dsc_ref):
        first = pl.program_id(0) == 0
        u, do = u_ref[...], do_ref[...]
        div = _pool_div(s)
        pm = _pool_mix(_window_sums(u, False), div) - u
        ypre = _dg(_cast(pm), _cast(w_ref[...]), 1, 0)
        dys = do * sc_ref[...]
        _acc(dsc_ref, jnp.sum(do * ypre, axis=0, keepdims=True), first)
        _acc(dw_ref, _dg(_cast(pm), _cast(dys), 0, 0), first)
        dpm = _dg(_cast(dys), _cast(w_ref[...]), 1, 1)
        dsc = [dpm / d for d in div]
        adj = None
        for gi in range(4):
            part = _window_sums(dsc[gi] * _hmask(gi), True)[gi]
            adj = part if adj is None else adj + part
        du_ref[...] = adj - dpm

    seq = pl.BlockSpec((s, GW), lambda b: (b, 0))
    return _call(
        body, f"pool_bwd_{tag}", (bl,),
        [seq, pl.BlockSpec((None, GW, GW), lambda b: (layer, 0, 0)), pl.BlockSpec((None, 1, GW), lambda b: (layer, 0, 0)), seq],
        [seq, pl.BlockSpec((GW, GW), lambda b: (0, 0)), pl.BlockSpec((1, GW), lambda b: (0, 0))],
        [_sds((bl * s, GW)), _sds((GW, GW)), _sds((1, GW))],
    )(pd, wbd, scale, do)


def _mem_prep_fn(mem, g, wk, wv, kw, bd64):
    mn = _rms(mem, g)
    return _headrms(mm(mn, wk), kw, bd64), mm(mn, wv)


def mem_prep_fwd(mem2, g, wkv, kw, bl, layer, tag):
    def body(m_ref, g_ref, wk_ref, wv_ref, kw_ref, k_ref, v_ref):
        k, v = _mem_prep_fn(m_ref[...], g_ref[...], wk_ref[...], wv_ref[...], kw_ref[...], _bdmask() * (1.0 / HD))
        k_ref[...] = k
        v_ref[...] = v

    blk = pl.BlockSpec((N_MEM, GW), lambda b: (b, 0))
    return _call(
        body, f"mem_prep_fwd_{tag}", (bl,),
        [pl.BlockSpec((N_MEM, D_MODEL), lambda b: (b, 0)), pl.BlockSpec((None, 1, D_MODEL), lambda b: (layer, 0, 0)),
         pl.BlockSpec((None, D_MODEL, GW), lambda b: (layer, 0, 0)), pl.BlockSpec((None, D_MODEL, GW), lambda b: (layer, 0, 1)),
         pl.BlockSpec((None, 1, GW), lambda b: (layer, 0, 0))],
        [blk, blk], [_sds((bl * N_MEM, GW))] * 2,
    )(mem2, g, wkv, wkv, kw)


def mem_prep_bwd(mem2, g, wkv, kw, dk, dv, bl, layer, tag):
    def body(m_ref, g_ref, wk_ref, wv_ref, kw_ref, dk_ref, dv_ref, dwk_ref, dwv_ref, dg_ref, dkw_ref):
        first = pl.program_id(0) == 0
        bd64 = _bdmask() * (1.0 / HD)
        _, vjp = jax.vjp(lambda g_, wk, wv, kw_: _mem_prep_fn(m_ref[...], g_, wk, wv, kw_, bd64),
                         g_ref[...], wk_ref[...].astype(F32), wv_ref[...].astype(F32), kw_ref[...])
        dg, dwk, dwv, dkw = vjp((dk_ref[...], dv_ref[...]))
        _acc(dwk_ref, dwk, first)
        _acc(dwv_ref, dwv, first)
        _acc(dg_ref, dg, first)
        _acc(dkw_ref, dkw, first)

    blk = pl.BlockSpec((N_MEM, GW), lambda b: (b, 0))
    return _call(
        body, f"mem_prep_bwd_{tag}", (bl,),
        [pl.BlockSpec((N_MEM, D_MODEL), lambda b: (b, 0)), pl.BlockSpec((None, 1, D_MODEL), lambda b: (layer, 0, 0)),
         pl.BlockSpec((None, D_MODEL, GW), lambda b: (layer, 0, 0)), pl.BlockSpec((None, D_MODEL, GW), lambda b: (layer, 0, 1)),
         pl.BlockSpec((None, 1, GW), lambda b: (layer, 0, 0)), blk, blk],
        [pl.BlockSpec((D_MODEL, GW), lambda b: (0, 0)), pl.BlockSpec((D_MODEL, GW), lambda b: (0, 0)),
         pl.BlockSpec((1, D_MODEL), lambda b: (0, 0)), pl.BlockSpec((1, GW), lambda b: (0, 0))],
        [_sds((D_MODEL, GW)), _sds((D_MODEL, GW)), _sds((1, D_MODEL)), _sds((1, GW))],
    )(mem2, g, wkv, wkv, kw, dk, dv)


def _mem_attn_fn(mq, qw, k, v, bd64):
    qn = _headrms(mq, qw, bd64)
    out = None
    for h in range(NH):
        hm = _hmask(h)
        lg = mm_nt(qn * hm, k) * SCALE
        e = jnp.exp(lg - lax.stop_gradient(jnp.max(lg, axis=-1, keepdims=True)))
        p = e / jnp.sum(e, axis=-1, keepdims=True)
        part = mm(p, v) * hm
        out = part if out is None else out + part
    return out


def mem_attn_fwd(pe, qw, k, v, bl, s, layer, tag):
    TQ = _token_tile(s)
    nq = s // TQ

    def body(q_ref, qw_ref, k_ref, v_ref, o_ref):
        o_ref[...] = _mem_attn_fn(q_ref[...], qw_ref[...], k_ref[...], v_ref[...], _bdmask() * (1.0 / HD))

    tok = pl.BlockSpec((TQ, GW), lambda b, i: (b * nq + i, 0))
    kv = pl.BlockSpec((N_MEM, GW), lambda b, i: (b, 0))
    return _call(
        body, f"mem_attn_fwd_{tag}", (bl, nq), [tok, pl.BlockSpec((None, 1, GW), lambda b, i: (layer, 0, 0)), kv, kv],
        tok, _sds((bl * s, GW)),
    )(pe, qw, k, v)


def mem_attn_bwd(pe, qw, k, v, do, bl, s, layer, tag):
    TQ = _token_tile(s)
    nq = s // TQ

    def body(q_ref, qw_ref, k_ref, v_ref, do_ref, dq_ref, dk_ref, dv_ref, dqw_ref):
        i = pl.program_id(1)
        bd64 = _bdmask() * (1.0 / HD)
        _, vjp = jax.vjp(lambda *a: _mem_attn_fn(*a, bd64), q_ref[...], qw_ref[...], k_ref[...], v_ref[...])
        dq, dqw, dk, dv = vjp(do_ref[...])
        dq_ref[...] = dq
        _acc(dk_ref, dk, i == 0)
        _acc(dv_ref, dv, i == 0)
        _acc(dqw_ref, dqw, jnp.logical_and(pl.program_id(0) == 0, i == 0))

    tok = pl.BlockSpec((TQ, GW), lambda b, i: (b * nq + i, 0))
    kv = pl.BlockSpec((N_MEM, GW), lambda b, i: (b, 0))
    return _call(
        body, f"mem_attn_bwd_{tag}", (bl, nq),
        [tok, pl.BlockSpec((None, 1, GW), lambda b, i: (layer, 0, 0)), kv, kv, tok],
        [tok, kv, kv, pl.BlockSpec((1, GW), lambda b, i: (0, 0))],
        [_sds((bl * s, GW)), _sds((bl * N_MEM, GW)), _sds((bl * N_MEM, GW)), _sds((1, GW))],
    )(pe, qw, k, v, do)


def _gate_out_fn(outs, gates, wparts):
    y = None
    for o, g, w in zip(outs, gates, wparts):
        part = mm(o * _silu(g), w)
        y = part if y is None else y + part
    return y


def outproj_fwd(x2, outs, pg, wout, layer, tag):
    t = x2.shape[0]
    TQ = _token_tile(t)

    def body(x_ref, oa, ob, oc, od, oe, g_ref, w_ref, y_ref):
        outs_ = [r[...] for r in (oa, ob, oc, od, oe)]
        gates = [g_ref[:, j * GW:(j + 1) * GW] for j in range(5)]
        wparts = [w_ref[j * GW:(j + 1) * GW, :] for j in range(5)]
        y_ref[...] = x_ref[...] + _gate_out_fn(outs_, gates, wparts)

    tok = pl.BlockSpec((TQ, GW), lambda i: (i, 0))
    big = pl.BlockSpec((TQ, D_MODEL), lambda i: (i, 0))
    return _call(
        body, f"outproj_fwd_{tag}", (t // TQ,),
        [big] + [tok] * 5 + [pl.BlockSpec((TQ, D_MIX), lambda i: (i, 0)),
                            pl.BlockSpec((None, D_MIX, D_MODEL), lambda i: (layer, 0, 0))],
        big, _sds((t, D_MODEL)),
    )(x2, *outs, pg, wout)


def outproj_bwd(outs, pg, wout, dy, layer, tag):
    t = dy.shape[0]
    TQ = _token_tile(t)

    def body(oa, ob, oc, od, oe, g_ref, w_ref, dy_ref, da, db, dc, dd, de, dg_ref, dw_ref):
        outs_ = [r[...] for r in (oa, ob, oc, od, oe)]
        gates = [g_ref[:, j * GW:(j + 1) * GW] for j in range(5)]
        wparts = [w_ref[j * GW:(j + 1) * GW, :].astype(F32) for j in range(5)]
        _, vjp = jax.vjp(_gate_out_fn, outs_, gates, wparts)
        douts, dgates, dws = vjp(dy_ref[...])
        for r, val in zip((da, db, dc, dd, de), douts):
            r[...] = val
        first = pl.program_id(0) == 0
        for j in range(5):
            dg_ref[:, j * GW:(j + 1) * GW] = dgates[j]

        @pl.when(first)
        def _():
            for j in range(5):
                dw_ref[j * GW:(j + 1) * GW, :] = dws[j]

        @pl.when(jnp.logical_not(first))
        def _():
            for j in range(5):
                dw_ref[j * GW:(j + 1) * GW, :] += dws[j]

    tok = pl.BlockSpec((TQ, GW), lambda i: (i, 0))
    return _call(
        body, f"outproj_bwd_{tag}", (t // TQ,),
        [tok] * 5 + [pl.BlockSpec((TQ, D_MIX), lambda i: (i, 0)), pl.BlockSpec((None, D_MIX, D_MODEL), lambda i: (layer, 0, 0)),
                     pl.BlockSpec((TQ, D_MODEL), lambda i: (i, 0))],
        [tok] * 5 + [pl.BlockSpec((TQ, D_MIX), lambda i: (i, 0)), pl.BlockSpec((D_MIX, D_MODEL), lambda i: (0, 0))],
        [_sds((t, GW))] * 5 + [_sds((t, D_MIX)), _sds((D_MIX, D_MODEL))],
    )(*outs, pg, wout, dy)


def loss_head(y, tgt):
    t = y.shape[0]
    TQ = _token_tile(t)

    def body(y_ref, t_ref, l_ref, dy_ref):
        diff = y_ref[...] - t_ref[...]
        dy_ref[...] = diff * (1.0 / D_MODEL)
        part = 0.5 * jnp.sum(jnp.sum(diff * diff, axis=-1, keepdims=True) * (1.0 / D_MODEL), axis=0, keepdims=True)
        _acc(l_ref, jnp.broadcast_to(part, (8, 128)), pl.program_id(0) == 0)

    big = pl.BlockSpec((TQ, D_MODEL), lambda i: (i, 0))
    return _call(body, "loss_head", (t // TQ,), [big, big], [pl.BlockSpec((8, 128), lambda i: (0, 0)), big],
                 [_sds((8, 128)), _sds((t, D_MODEL))])(y, tgt)


def layer_fwd(x2, mem2, p, layer, bl, s):
    tag = f"l{layer}"
    ht, pa, pb, pc, pd, pe, pg, pf = inproj_fwd(x2, p["norm_g"], p["w_in"], layer, tag)
    qn, kn, vb, cq, ck = fox_prep_fwd(pa, pf, p["fox_q_norm"], p["fox_k_norm"], p["fox_f_bias"], bl, s, layer, tag)
    oa, lse = fox_attn_fwd(qn, kn, vb, cq, ck, bl, s, tag)
    ob = sb_attn_fwd(pb, bl, s, tag)
    oc, states = hgrn_fwd(pc, p["lb"], p["hgrn_out_norm"], bl, s, layer, tag)
    od = pool_fwd(pd, p["pool_wbd"], p["pool_scale"], bl, s, layer, tag)
    mk, mv = mem_prep_fwd(mem2, p["mem_norm_g"], p["mem_w_kv"], p["mem_k_norm"], bl, layer, tag)
    oe = mem_attn_fwd(pe, p["mem_q_norm"], mk, mv, bl, s, layer, tag)
    y = outproj_fwd(x2, (oa, ob, oc, od, oe), pg, p["w_out"], layer, tag)
    saved = dict(x2=x2, ht=ht, pa=pa, pb=pb, pc=pc, pd=pd, pe=pe, pg=pg, pf=pf, qn=qn, kn=kn, vb=vb, cq=cq, ck=ck,
                 oa=oa, lse=lse, ob=ob, oc=oc, states=states, od=od, mk=mk, mv=mv, oe=oe)
    return y, saved


def layer_bwd(dy, mem2, p, sv, layer, bl, s):
    tag = f"l{layer}"
    (doa, dob, doc, dod, doe, dg_gates, dwout) = outproj_bwd((sv["oa"], sv["ob"], sv["oc"], sv["od"], sv["oe"]), sv["pg"],
                                                              p["w_out"], dy, layer, tag)
    dqn, dkn, dv, dck = fox_attn_bwd(sv["qn"], sv["kn"], sv["vb"], sv["cq"], sv["ck"], sv["lse"], doa, bl, s, tag)
    d_a, d_f, dqw, dkw, dbias = fox_prep_bwd(sv["pa"], sv["pf"], p["fox_q_norm"], p["fox_k_norm"], p["fox_f_bias"], sv["cq"],
                                             dqn, dkn, dv, dck, bl, s, layer, tag)
    dsq, dsk, dsv = sb_attn_bwd(sv["pb"], dob, bl, s, tag)
    d_c, dlb, dwn = hgrn_bwd(sv["pc"], p["lb"], p["hgrn_out_norm"], sv["states"], doc, bl, s, layer, tag)
    d_d, dwbd, dpscale = pool_bwd(sv["pd"], p["pool_wbd"], p["pool_scale"], dod, bl, s, layer, tag)
    d_e, dmk, dmv, dmqw = mem_attn_bwd(sv["pe"], p["mem_q_norm"], sv["mk"], sv["mv"], doe, bl, s, layer, tag)
    dwk, dwv, dmg, dmkw = mem_prep_bwd(mem2, p["mem_norm_g"], p["mem_w_kv"], p["mem_k_norm"], dmk, dmv, bl, layer, tag)
    dpieces = (d_a, dsq, dsk, dsv, d_c, d_d, d_e, dg_gates, d_f)
    dx, dng = inproj_bwd_dx(sv["x2"], p["norm_g"], p["w_in"], dy, dpieces, layer, tag)
    dwin = jnp.concatenate([matmul_acc(sv["ht"], dp, f"{tag}_{nm}") for (nm, _, _), dp in zip(BWD_PIECES, dpieces)], axis=1)
    grads = dict(norm_g=dng, w_in=dwin, fox_f_bias=dbias, fox_q_norm=dqw, fox_k_norm=dkw, lb=dlb, hgrn_out_norm=dwn,
                 pool_wbd=dwbd, pool_scale=dpscale, mem_norm_g=dmg, mem_w_kv=jnp.concatenate([dwk, dwv], axis=1),
                 mem_q_norm=dmqw, mem_k_norm=dmkw, w_out=dwout)
    return dx, grads


def _tile4(w):
    return jnp.tile(w, (1, NH))[:, None, :]


def prepare_params(norm_g, w_in_p, fox_f_bias, fox_q_norm, fox_k_norm, hgrn_lb_logits, hgrn_out_norm, pool_w, pool_scale,
                   mem_norm_g, mem_w_kv, mem_q_norm, mem_k_norm, w_out):
    p1 = jax.nn.sigmoid(hgrn_lb_logits[1] - hgrn_lb_logits[0])
    lb = jnp.stack([jnp.zeros_like(p1), jnp.clip(p1, 0.0, 1.0 - 1e-6)])
    eye = jnp.eye(4, dtype=F32)
    wbd = jnp.einsum("lgcd,gh->lgchd", pool_w, eye).reshape(2, GW, GW)
    return dict(norm_g=norm_g[:, None, :], w_in=w_in_p, fox_f_bias=jnp.pad(fox_f_bias, ((0, 0), (0, 124)))[:, None, :],
                fox_q_norm=_tile4(fox_q_norm), fox_k_norm=_tile4(fox_k_norm), lb=lb[:, None, :],
                hgrn_out_norm=hgrn_out_norm[:, None, :], pool_wbd=wbd, pool_scale=pool_scale[:, None, :],
                mem_norm_g=mem_norm_g[:, None, :], mem_w_kv=mem_w_kv, mem_q_norm=_tile4(mem_q_norm),
                mem_k_norm=_tile4(mem_k_norm), w_out=w_out)


def local_step(x, mem, tgt, p):
    bl, s, _ = x.shape
    x2, mem2, tgt2 = x.reshape(bl * s, D_MODEL), mem.reshape(bl * N_MEM, D_MODEL), tgt.reshape(bl * s, D_MODEL)
    y0, sv0 = layer_fwd(x2, mem2, p, 0, bl, s)
    y1, sv1 = layer_fwd(y0, mem2, p, 1, bl, s)
    lpart, dy = loss_head(y1, tgt2)
    dx1, g1 = layer_bwd(dy, mem2, p, sv1, 1, bl, s)
    dx0, g0 = layer_bwd(dx1, mem2, p, sv0, 0, bl, s)
    return lpart[0, 0], dx0.reshape(bl, s, D_MODEL), g0, g1


_ANY = pl.BlockSpec(memory_space=pl.ANY)


def all_gather_rows(xss, tag):
    n = len(xss)

    def body(*refs):
        x_refs, o_refs, (send_sems, recv_sems, local_sems) = refs[:n], refs[n:2 * n], refs[2 * n:]
        x, y, cc = lax.axis_index("x"), lax.axis_index("y"), lax.axis_index("c")
        me, sibling = (x, y, cc), (x, y, 1 - cc)
        chips = [(1 - x, y), (x, 1 - y), (1 - x, 1 - y)]

        def rows(a, px, py, pc):
            r = xss[a].shape[1]
            return o_refs[a].at[:, pl.ds((4 * px + 2 * py + pc) * r, r), :]

        def copy(a, k, block, to, src=None):
            return pltpu.make_async_remote_copy(src_ref=rows(a, *block) if src is None else src, dst_ref=rows(a, *block),
                                                send_sem=send_sems.at[7 * a + k], recv_sem=recv_sems.at[7 * a + k], device_id=to,
                                                device_id_type=pl.DeviceIdType.MESH)

        mine = [pltpu.make_async_copy(x_refs[a], rows(a, *me), local_sems.at[a]) for a in range(n)]
        first = []
        for a in range(n):
            first += [copy(a, 0, me, sibling, src=x_refs[a])] + [copy(a, 1 + j, me, (*chip, cc), src=x_refs[a])
                                                                 for j, chip in enumerate(chips)]
        for cp in mine + first:
            cp.start()
        passed = []
        for j, chip in enumerate(chips):
            for a in range(n):
                copy(a, 1 + j, (*chip, cc), me).wait_recv()
                passed.append(copy(a, 4 + j, (*chip, cc), sibling))
                passed[-1].start()
        for a in range(n):
            copy(a, 0, sibling, me).wait_recv()
        for j, chip in enumerate(chips):
            for a in range(n):
                copy(a, 4 + j, (*chip, 1 - cc), me).wait_recv()
        for cp in first + passed:
            cp.wait_send()
        for cp in mine:
            cp.wait()

    nsem = pltpu.SemaphoreType.DMA((7 * n,))
    return pl.pallas_call(
        body, name=f"all_gather_{tag}", in_specs=[_ANY] * n, out_specs=[_ANY] * n,
        out_shape=[_sds((xs.shape[0], N_DEV * xs.shape[1], xs.shape[2]), xs.dtype) for xs in xss],
        scratch_shapes=[nsem, nsem, pltpu.SemaphoreType.DMA((n,))],
    )(*xss)


def exchange_cores(parts, tag):
    n = len(parts)
    counts = [p.shape[0] * 4 for p in parts]

    def body(*refs):
        p_refs, t_refs, (send_sems, recv_sems) = refs[:n], refs[n:2 * n], refs[2 * n:]
        x, y, cc = lax.axis_index("x"), lax.axis_index("y"), lax.axis_index("c")
        copies, k = [], 0
        for a in range(n):
            for l in range(parts[a].shape[0]):
                for q in range(4):
                    copies.append(pltpu.make_async_remote_copy(
                        src_ref=p_refs[a].at[l, q, pl.ds(1 - cc, 1)], dst_ref=t_refs[a].at[l, q], send_sem=send_sems.at[k],
                        recv_sem=recv_sems.at[k], device_id=(x, y, 1 - cc), device_id_type=pl.DeviceIdType.MESH))
                    k += 1
        for cp in copies:
            cp.start()
        for cp in copies:
            cp.wait()

    nsem = pltpu.SemaphoreType.DMA((sum(counts),))
    return pl.pallas_call(
        body, name=f"exchange_cores_{tag}", in_specs=[_ANY] * n, out_specs=[_ANY] * n,
        out_shape=[_sds((p.shape[0], 4, 1, p.shape[3], p.shape[4]), p.dtype) for p in parts],
        scratch_shapes=[nsem, nsem],
    )(*parts)


def add_core_halves(part5, theirs, core, tag):
    nl, _, _, r, c = part5.shape
    tr = r if r * c <= 256 * 1024 else 64

    def body(core_ref, a_ref, b_ref, o_ref):
        o_ref[...] = _cast(a_ref[...] + b_ref[...])

    blk = lambda which: pl.BlockSpec((None, None, None, tr, c), lambda l, q, i, cref: (l, q, cref[0] if which else 0, i, 0))
    return pl.pallas_call(
        body, name=f"add_core_halves_{tag}", out_shape=_sds((nl, 4, 1, r, c), _MMT),
        grid_spec=pltpu.PrefetchScalarGridSpec(num_scalar_prefetch=1, grid=(nl, 4, r // tr), in_specs=[blk(True), blk(False)],
                                               out_specs=blk(False)),
        compiler_params=pltpu.CompilerParams(dimension_semantics=("arbitrary",) * 3, vmem_limit_bytes=VMEM_LIMIT_BYTES),
    )(core, part5, theirs)


def exchange_chips(s4s, tag):
    n = len(s4s)

    def body(*refs):
        s_refs, o_refs, (send_sems, recv_sems, local_sems) = refs[:n], refs[n:2 * n], refs[2 * n:]
        x, y, cc = lax.axis_index("x"), lax.axis_index("y"), lax.axis_index("c")
        copies = [pltpu.make_async_copy(s_refs[a].at[:, pl.ds(2 * x + y, 1)], o_refs[a].at[0], local_sems.at[a]) for a in range(n)]
        for k in range(1, 4):
            px = 1 - x if (k >> 1) & 1 else x
            py = 1 - y if k & 1 else y
            for a in range(n):
                copies.append(pltpu.make_async_remote_copy(
                    src_ref=s_refs[a].at[:, pl.ds(2 * px + py, 1)], dst_ref=o_refs[a].at[k], send_sem=send_sems.at[3 * a + k - 1],
                    recv_sem=recv_sems.at[3 * a + k - 1], device_id=(px, py, cc), device_id_type=pl.DeviceIdType.MESH))
        for cp in copies:
            cp.start()
        for cp in copies:
            cp.wait()

    nsem = pltpu.SemaphoreType.DMA((3 * n,))
    return pl.pallas_call(
        body, name=f"exchange_chips_{tag}", in_specs=[_ANY] * n, out_specs=[_ANY] * n,
        out_shape=[_sds((4, s.shape[0], 1, 1, s.shape[3], s.shape[4]), s.dtype) for s in s4s],
        scratch_shapes=[nsem, nsem, pltpu.SemaphoreType.DMA((n,))],
    )(*s4s)


def _row_tile(rows):
    if rows <= 512 and rows % 64:
        return rows
    for t in (64, 40, 32, 16, 8):
        if rows % t == 0:
            return t
    return rows


def sum_slots(slots, tag):
    ns, rows, c = slots.shape
    tr = _row_tile(rows)

    def body(s_ref, o_ref):
        acc = s_ref[0].astype(F32)
        for k in range(1, ns):
            acc = acc + s_ref[k].astype(F32)
        o_ref[...] = acc

    return _call(body, f"sum_slots_{tag}", (rows // tr,), [pl.BlockSpec((ns, tr, c), lambda i: (0, i, 0))],
                 pl.BlockSpec((tr, c), lambda i: (i, 0)), _sds((rows, c)))(slots)


def _adamw(w, g, m, v):
    m = ADAM_B1 * m + (1.0 - ADAM_B1) * g
    v = ADAM_B2 * v + (1.0 - ADAM_B2) * (g * g)
    m_hat = m / (1.0 - ADAM_B1 ** ADAM_STEP)
    v_hat = v / (1.0 - ADAM_B2 ** ADAM_STEP)
    delta = -ADAM_LR * (m_hat / (jnp.sqrt(v_hat) + ADAM_EPS) + ADAM_WD * w)
    return delta, m, v


def adam_update(w, m, v, g, tag, slots=False):
    rows, c = w.shape
    tr = _row_tile(rows)
    ns = g.shape[0] if slots else 0

    def body(w_ref, m_ref, v_ref, g_ref, go_ref, d_ref, mo_ref, vo_ref):
        if slots:
            g = g_ref[0].astype(F32)
            for k in range(1, ns):
                g = g + g_ref[k].astype(F32)
        else:
            g = g_ref[...]
        d, mn, vn = _adamw(w_ref[...], g, m_ref[...], v_ref[...])
        go_ref[...] = g
        d_ref[...] = d
        mo_ref[...] = mn
        vo_ref[...] = vn

    blk = pl.BlockSpec((tr, c), lambda i: (i, 0))
    gspec = pl.BlockSpec((ns, tr, c), lambda i: (0, i, 0)) if slots else blk
    return _call(body, f"adam_{tag}", (rows // tr,), [blk, blk, blk, gspec], [blk] * 4, [_sds((rows, c))] * 4)(w, m, v, g)


_SMALL = (("norm_g", (2, 1024)), ("fox_f_bias", (2, 4)), ("fox_q_norm", (2, 64)), ("fox_k_norm", (2, 64)),
          ("hgrn_lb_logits", (2, 256)), ("hgrn_out_norm", (2, 256)), ("pool_w", (2, 4, 64, 64)), ("pool_scale", (2, 256)),
          ("mem_norm_g", (2, 1024)), ("mem_q_norm", (2, 64)), ("mem_k_norm", (2, 64)))
_SLAB_ROWS = 312


def pack_small(d):
    flat = jnp.concatenate([d[n].reshape(-1) for n, _ in _SMALL])
    return jnp.pad(flat, (0, _SLAB_ROWS * 128 - flat.shape[0])).reshape(_SLAB_ROWS, 128)


def unpack_small(slab):
    flat, out, off = slab.reshape(-1), {}, 0
    for n, shp in _SMALL:
        size = 1
        for e in shp:
            size *= e
        out[n] = flat[off:off + size].reshape(shp)
        off += size
    return out


def small_grads(g0, g1, lb_logits):
    st = lambda f: jnp.stack([f(g0), f(g1)])
    heads = lambda a: a.reshape(NH, HD).sum(0)
    p1 = jax.nn.sigmoid(lb_logits[1] - lb_logits[0])
    inside = (p1 > 0.0) & (p1 < 1.0 - 1e-6)
    dl1 = jnp.where(inside, g1["lb"][0] * p1 * (1.0 - p1), 0.0)
    diag = lambda a: jnp.stack([a.reshape(4, HD, 4, HD)[i, :, i, :] for i in range(4)])
    return dict(norm_g=st(lambda g: g["norm_g"][0]), fox_f_bias=st(lambda g: g["fox_f_bias"][0, :NH]),
                fox_q_norm=st(lambda g: heads(g["fox_q_norm"])), fox_k_norm=st(lambda g: heads(g["fox_k_norm"])),
                hgrn_lb_logits=jnp.stack([-dl1, dl1]), hgrn_out_norm=st(lambda g: g["hgrn_out_norm"][0]),
                pool_w=st(lambda g: diag(g["pool_wbd"])), pool_scale=st(lambda g: g["pool_scale"][0]),
                mem_norm_g=st(lambda g: g["mem_norm_g"][0]), mem_q_norm=st(lambda g: heads(g["mem_q_norm"])),
                mem_k_norm=st(lambda g: heads(g["mem_k_norm"])))


def kernel(x, mem, norm_g, w_in, fox_f_bias, fox_q_norm, fox_k_norm, hgrn_lb_logits, hgrn_out_norm, pool_w, pool_scale, mem_norm_g, mem_w_kv, mem_q_norm, mem_k_norm, w_out, loss_target, m_norm_g, m_w_in, m_fox_f_bias, m_fox_q_norm, m_fox_k_norm, m_hgrn_lb_logits, m_hgrn_out_norm, m_pool_w, m_pool_scale, m_mem_norm_g, m_mem_w_kv, m_mem_q_norm, m_mem_k_norm, m_w_out, v_norm_g, v_w_in, v_fox_f_bias, v_fox_q_norm, v_fox_k_norm, v_hgrn_lb_logits, v_hgrn_out_norm, v_pool_w, v_pool_scale, v_mem_norm_g, v_mem_w_kv, v_mem_q_norm, v_mem_k_norm, v_w_out):
    given = dict(norm_g=(norm_g, m_norm_g, v_norm_g), w_in=(w_in, m_w_in, v_w_in), fox_f_bias=(fox_f_bias, m_fox_f_bias, v_fox_f_bias),
                 fox_q_norm=(fox_q_norm, m_fox_q_norm, v_fox_q_norm), fox_k_norm=(fox_k_norm, m_fox_k_norm, v_fox_k_norm),
                 hgrn_lb_logits=(hgrn_lb_logits, m_hgrn_lb_logits, v_hgrn_lb_logits),
                 hgrn_out_norm=(hgrn_out_norm, m_hgrn_out_norm, v_hgrn_out_norm), pool_w=(pool_w, m_pool_w, v_pool_w),
                 pool_scale=(pool_scale, m_pool_scale, v_pool_scale), mem_norm_g=(mem_norm_g, m_mem_norm_g, v_mem_norm_g),
                 mem_w_kv=(mem_w_kv, m_mem_w_kv, v_mem_w_kv), mem_q_norm=(mem_q_norm, m_mem_q_norm, v_mem_q_norm),
                 mem_k_norm=(mem_k_norm, m_mem_k_norm, v_mem_k_norm), w_out=(w_out, m_w_out, v_w_out))
    order = ("norm_g", "w_in", "fox_f_bias", "fox_q_norm", "fox_k_norm", "hgrn_lb_logits", "hgrn_out_norm", "pool_w",
             "pool_scale", "mem_norm_g", "mem_w_kv", "mem_q_norm", "mem_k_norm", "w_out")

    w_in_full, w_out_full, w_kv_full = all_gather_rows([_cast(permute_cols(w_in)), _cast(w_out), _cast(mem_w_kv)], "weights")
    p = prepare_params(norm_g, w_in_full, fox_f_bias, fox_q_norm, fox_k_norm, hgrn_lb_logits, hgrn_out_norm, pool_w,
                       pool_scale, mem_norm_g, w_kv_full, mem_q_norm, mem_k_norm, w_out_full)

    loss_part, grad_x, g0, g1 = local_step(x, mem, loss_target, p)
    loss = lax.psum(loss_part, ("x", "y", "c"))

    res = {}
    core = lax.axis_index("c").astype(jnp.int32).reshape(1)

    names = ("w_in", "w_out", "mem_w_kv")
    part5 = []
    for name in names:
        nl, r, _ = given[name][0].shape
        g2 = jnp.stack([g0[name], g1[name]])
        part5.append(g2.reshape(nl, 4, 2, r, g2.shape[-1]))
    s4 = [add_core_halves(p5, th, core, name) for name, p5, th in zip(names, part5, exchange_cores(part5, "grads"))]
    for name, sl in zip(names, exchange_chips(s4, "grads")):
        w, m, v = given[name]
        nl, r, c = w.shape
        slots = sl.reshape(4, nl * r, sl.shape[-1])
        if name == "w_in":
            g = sum_slots(slots, name).reshape(nl, r, -1)
            out = adam_update(w.reshape(nl * r, c), m.reshape(nl * r, c), v.reshape(nl * r, c),
                              unpermute_cols(g).reshape(nl * r, c), name)
        else:
            out = adam_update(w.reshape(nl * r, c), m.reshape(nl * r, c), v.reshape(nl * r, c), slots, name, slots=True)
        res[name] = tuple(o.reshape(nl, r, c) for o in out)

    gsmall = pack_small(small_grads(g0, g1, hgrn_lb_logits))
    gathered = all_gather_rows([gsmall[None]], "small")[0].reshape(N_DEV, _SLAB_ROWS, 128)
    slabs = adam_update(*[pack_small({n: given[n][j] for n, _ in _SMALL}) for j in range(3)], gathered, "small", slots=True)
    small = [unpack_small(sl) for sl in slabs]
    for n, _ in _SMALL:
        res[n] = tuple(small[j][n] for j in range(4))

    return (loss, grad_x, *[res[n][0] for n in order], *[res[n][1] for n in order], *[res[n][2] for n in order],
            *[res[n][3] for n in order])
```

```python
import functools

import jax
import jax.numpy as jnp
from jax import lax
from jax.experimental import pallas as pl
from jax.experimental.pallas import tpu as pltpu

F32 = jnp.float32
BF = jnp.bfloat16
_MMT = BF

D_MODEL = 1024
GW = 256
HD = 64
NH = 4
CH = 64
N_MEM = 256
D_IN = 4100
D_INP = 4224
D_MIX = 1280
EPS = 1e-6
NEG_BIG = -1e30
LB_FLOOR = 1e-30
SCALE = HD ** -0.5
TQ = 256
TM = 512


def _token_tile(n):
    return TM if n % TM == 0 else TQ
N_DEV = 8
VMEM_LIMIT_BYTES = 56 * 1024 * 1024

ADAM_LR = 0.001
ADAM_B1 = 0.9
ADAM_B2 = 0.999
ADAM_EPS = 1e-08
ADAM_WD = 0.01
ADAM_STEP = 10

PIECES = (("A", 0, 768), ("B", 768, 768), ("C", 1536, 768), ("D", 2304, 256), ("E", 2560, 256),
          ("G", 2816, 1280), ("F", 4096, 128))
BWD_PIECES = (("A", 0, 768), ("Bq", 768, 256), ("Bk", 1024, 256), ("Bv", 1280, 256), ("C", 1536, 768),
              ("D", 2304, 256), ("E", 2560, 256), ("G", 2816, 1280), ("F", 4096, 128))
_ORIG = dict(fq=(0, 256), fk=(256, 512), fv=(512, 768), fg=(768, 1024), ff=(1024, 1028), sq=(1028, 1284),
             sk=(1284, 1540), sv=(1540, 1796), sg=(1796, 2052), hq=(2052, 2308), hf=(2308, 2564),
             hi=(2564, 2820), hg=(2820, 3076), pv=(3076, 3332), pg=(3332, 3588), mq=(3588, 3844), mg=(3844, 4100))
_PERM_ORDER = ("fq", "fk", "fv", "sq", "sk", "sv", "hq", "hf", "hi", "pv", "mq", "fg", "sg", "hg", "pg", "mg", "ff")
_ORIG_ORDER = ("fq", "fk", "fv", "fg", "ff", "sq", "sk", "sv", "sg", "hq", "hf", "hi", "hg", "pv", "pg", "mq", "mg")


def permute_cols(w):
    parts = [w[..., _ORIG[n][0]:_ORIG[n][1]] for n in _PERM_ORDER]
    parts.append(jnp.zeros(w.shape[:-1] + (D_INP - D_IN,), w.dtype))
    return jnp.concatenate(parts, axis=-1)


def unpermute_cols(g):
    start, off = {}, 0
    for n in _PERM_ORDER:
        start[n] = off
        off += _ORIG[n][1] - _ORIG[n][0]
    return jnp.concatenate([g[..., start[n]:start[n] + _ORIG[n][1] - _ORIG[n][0]] for n in _ORIG_ORDER], axis=-1)


def _cast(a):
    return a.astype(_MMT)


def _dg(a, b, ca, cb):
    return lax.dot_general(a, b, (((ca,), (cb,)), ((), ())), preferred_element_type=F32)


@jax.custom_vjp
def mm(a, b):
    return _dg(_cast(a), _cast(b), 1, 0)


@jax.custom_vjp
def mm_nt(a, b):
    return _dg(_cast(a), _cast(b), 1, 1)


@jax.custom_vjp
def mm_tn(a, b):
    return _dg(_cast(a), _cast(b), 0, 0)


mm.defvjp(lambda a, b: (mm(a, b), (a, b)),
          lambda r, g: (mm_nt(g, r[1]).astype(r[0].dtype), mm_tn(r[0], g).astype(r[1].dtype)))
mm_nt.defvjp(lambda a, b: (mm_nt(a, b), (a, b)),
             lambda r, g: (mm(g, r[1]).astype(r[0].dtype), mm_tn(g, r[0]).astype(r[1].dtype)))
mm_tn.defvjp(lambda a, b: (mm_tn(a, b), (a, b)),
             lambda r, g: (mm_nt(r[1], g).astype(r[0].dtype), mm(r[0], g).astype(r[1].dtype)))


def _split(a):
    hi = a.astype(_MMT)
    lo = (a - hi.astype(F32)).astype(_MMT)
    return hi, lo


@jax.custom_vjp
def xr(a, c, ct):
    hi, lo = _split(a)
    cc = _cast(c)
    return _dg(hi, cc, 1, 0) + _dg(lo, cc, 1, 0)


@jax.custom_vjp
def xl(c, ct, a):
    hi, lo = _split(a)
    cc = _cast(c)
    return _dg(cc, hi, 1, 0) + _dg(cc, lo, 1, 0)


xr.defvjp(lambda a, c, ct: (xr(a, c, ct), (c, ct)),
          lambda r, g: (xr(g, r[1], r[0]), jnp.zeros_like(r[0]), jnp.zeros_like(r[1])))
xl.defvjp(lambda c, ct, a: (xl(c, ct, a), (c, ct)),
          lambda r, g: (jnp.zeros_like(r[0]), jnp.zeros_like(r[1]), xl(r[1], r[0], g)))


def _iota(shape, dim):
    return lax.broadcasted_iota(jnp.int32, shape, dim)


def _hmask(h, n=GW):
    lane = _iota((1, n), 1)
    return ((lane >= h * HD) & (lane < (h + 1) * HD)).astype(F32)


def _bdmask(n=GW):
    return ((_iota((n, n), 0) >> 6) == (_iota((n, n), 1) >> 6)).astype(F32)


def _tri(n, kind="le"):
    r, c = _iota((n, n), 0), _iota((n, n), 1)
    return {"le": c <= r, "ge": c >= r, "gt": c > r, "lt": c < r}[kind].astype(F32)


def _onehot_lane(h, n=128):
    return (_iota((1, n), 1) == h).astype(F32)


def _logsig(x):
    return jnp.minimum(x, 0.0) - jnp.log1p(jnp.exp(-jnp.abs(x)))


def _sigmoid(x):
    return 0.5 * (jnp.tanh(0.5 * x) + 1.0)


def _silu(x):
    return x * _sigmoid(x)


def _rms(x, g):
    return x * lax.rsqrt(jnp.mean(x * x, axis=-1, keepdims=True) + EPS) * g


def _headrms(x, w, bd64):
    ms = xr(x * x, bd64, bd64)
    return x * lax.rsqrt(ms + EPS) * w


def _call(body, name, grid, in_specs, out_specs, out_shape, scratch=()):
    return pl.pallas_call(
        body, name=name, grid=grid, in_specs=in_specs, out_specs=out_specs, out_shape=out_shape,
        scratch_shapes=list(scratch),
        compiler_params=pltpu.CompilerParams(dimension_semantics=("arbitrary",) * len(grid),
                                             vmem_limit_bytes=VMEM_LIMIT_BYTES))


def _sds(shape, dtype=F32):
    return jax.ShapeDtypeStruct(shape, dtype)


def _acc(ref, val, first):
    @pl.when(first)
    def _():
        ref[...] = val

    @pl.when(jnp.logical_not(first))
    def _():
        ref[...] += val


def inproj_fwd(x2, g, w, layer, tag):
    t = x2.shape[0]
    TQ = _token_tile(t)

    def body(x_ref, g_ref, w_ref, ht_ref, *outs):
        h = _rms(x_ref[...], g_ref[...])
        hb = _cast(h)
        ht_ref[...] = _cast(h.T)
        for (_, c0, wd), o in zip(PIECES, outs):
            o[...] = _dg(hb, _cast(w_ref[:, c0:c0 + wd]), 1, 0)

    return _call(
        body, f"inproj_fwd_{tag}", (t // TQ,),
        [pl.BlockSpec((TQ, D_MODEL), lambda i: (i, 0)),
         pl.BlockSpec((None, 1, D_MODEL), lambda i: (layer, 0, 0)),
         pl.BlockSpec((None, D_MODEL, D_INP), lambda i: (layer, 0, 0))],
        [pl.BlockSpec((D_MODEL, TQ), lambda i: (0, i))] + [pl.BlockSpec((TQ, wd), lambda i: (i, 0)) for _, _, wd in PIECES],
        [_sds((D_MODEL, t), _MMT)] + [_sds((t, wd)) for _, _, wd in PIECES],
    )(x2, g, w)


def inproj_bwd_dx(x2, g, w, dy, dpieces, layer, tag):
    t = x2.shape[0]
    TQ = _token_tile(t)

    def body(x_ref, g_ref, w_ref, dy_ref, *rest):
        dps, (dx_ref, dg_ref) = rest[:len(BWD_PIECES)], rest[len(BWD_PIECES):]
        dh = None
        for (_, c0, wd), dp in zip(BWD_PIECES, dps):
            part = _dg(_cast(dp[...]), _cast(w_ref[:, c0:c0 + wd]), 1, 1)
            dh = part if dh is None else dh + part
        _, vjp = jax.vjp(_rms, x_ref[...], g_ref[...])
        dx, dg = vjp(dh)
        dx_ref[...] = dy_ref[...] + dx
        _acc(dg_ref, dg, pl.program_id(0) == 0)

    return _call(
        body, f"inproj_bwd_dx_{tag}", (t // TQ,),
        [pl.BlockSpec((TQ, D_MODEL), lambda i: (i, 0)),
         pl.BlockSpec((None, 1, D_MODEL), lambda i: (layer, 0, 0)),
         pl.BlockSpec((None, D_MODEL, D_INP), lambda i: (layer, 0, 0), pipeline_mode=pl.Buffered(1)),
         pl.BlockSpec((TQ, D_MODEL), lambda i: (i, 0))] + [pl.BlockSpec((TQ, wd), lambda i: (i, 0)) for _, _, wd in BWD_PIECES],
        [pl.BlockSpec((TQ, D_MODEL), lambda i: (i, 0)), pl.BlockSpec((1, D_MODEL), lambda i: (0, 0))],
        [_sds((t, D_MODEL)), _sds((1, D_MODEL))],
    )(x2, g, w, dy, *dpieces)


def matmul_acc(at, bs, tag):
    m, t = at.shape
    widths = [b.shape[1] for b in bs]
    tk = 1024 if t % 1024 == 0 else (512 if t % 512 == 0 else TQ)

    def body(a_ref, *refs):
        b_refs, o_ref = refs[:-1], refs[-1]
        a = _cast(a_ref[...])
        first, off = pl.program_id(0) == 0, 0
        for b_ref, n in zip(b_refs, widths):
            part = _dg(a, _cast(b_ref[...]), 1, 0)

            @pl.when(first)
            def _(part=part, off=off, n=n):
                o_ref[:, off:off + n] = part

            @pl.when(jnp.logical_not(first))
            def _(part=part, off=off, n=n):
                o_ref[:, off:off + n] += part

            off += n

    return _call(
        body, f"matmul_acc_{tag}", (t // tk,),
        [pl.BlockSpec((m, tk), lambda i: (0, i))] + [pl.BlockSpec((tk, n), lambda i: (i, 0)) for n in widths],
        pl.BlockSpec((m, sum(widths)), lambda i: (0, 0)),
        _sds((m, sum(widths))),
    )(at, *bs)


def _fox_prep_fn(q, k, ff, qw, kw, bias, carry, bd64, tri, trit, last):
    qn = _headrms(q, qw, bd64)
    kn = _headrms(k, kw, bd64)
    lf = _logsig(ff + bias)
    c = xl(tri, trit, lf) + carry
    return qn, kn, c, jnp.sum(c * last, axis=0, keepdims=True)


def _prep_consts(tq):
    return _bdmask() * (1.0 / HD), _tri(tq), _tri(tq, "ge"), (_iota((tq, 1), 0) == tq - 1).astype(F32)


def fox_prep_fwd(pa, pf, qw, kw, bias, bl, s, layer, tag):
    TQ = _token_tile(s)
    nq = s // TQ

    def body(q_ref, k_ref, v_ref, f_ref, qw_ref, kw_ref, b_ref, qn_ref, kn_ref, vb_ref, cq_ref, ck_ref, carry):
        @pl.when(pl.program_id(1) == 0)
        def _():
            carry[...] = jnp.zeros_like(carry)

        qn, kn, c, cl = _fox_prep_fn(q_ref[...], k_ref[...], f_ref[...], qw_ref[...], kw_ref[...], b_ref[...],
                                     carry[...], *_prep_consts(TQ))
        carry[...] = cl
        qn_ref[...] = _cast(qn)
        kn_ref[...] = _cast(kn)
        vb_ref[...] = _cast(v_ref[...])
        cq_ref[...] = c
        ck_ref[...] = c.T[0:8, :]

    tok = lambda j: pl.BlockSpec((TQ, GW), lambda b, i: (b * nq + i, j))
    par = lambda n: pl.BlockSpec((None, 1, n), lambda b, i: (layer, 0, 0))
    return _call(
        body, f"fox_prep_fwd_{tag}", (bl, nq),
        [tok(0), tok(1), tok(2), pl.BlockSpec((TQ, 128), lambda b, i: (b * nq + i, 0)), par(GW), par(GW), par(128)],
        [tok(0), tok(0), tok(0), pl.BlockSpec((TQ, 128), lambda b, i: (b * nq + i, 0)),
         pl.BlockSpec((None, 8, TQ), lambda b, i: (b, 0, i))],
        [_sds((bl * s, GW), _MMT)] * 3 + [_sds((bl * s, 128)), _sds((bl, 8, s))],
        [pltpu.VMEM((1, 128), F32)],
    )(pa, pa, pa, pf, qw, kw, bias)


def fox_prep_bwd(pa, pf, qw, kw, bias, cq, dqn, dkn, dv, dck, bl, s, layer, tag):
    TQ = _token_tile(s)
    nq = s // TQ

    def body(q_ref, k_ref, f_ref, qw_ref, kw_ref, b_ref, cq_ref, cprev_ref, dqn_ref, dkn_ref, dv_ref, dck_ref,
             da_ref, df_ref, dqw_ref, dkw_ref, db_ref, dcarry):
        i = pl.program_id(1)
        first = jnp.logical_and(pl.program_id(0) == 0, i == 0)

        @pl.when(i == 0)
        def _():
            dcarry[...] = jnp.zeros_like(dcarry)

        last = (_iota((TQ, 1), 0) == TQ - 1).astype(F32)
        carry_in = jnp.where(i == nq - 1, 0.0, jnp.sum(cprev_ref[...] * last, axis=0, keepdims=True))
        consts = _prep_consts(TQ)
        _, vjp = jax.vjp(lambda *a: _fox_prep_fn(*a, *consts), q_ref[...], k_ref[...], f_ref[...], qw_ref[...],
                         kw_ref[...], b_ref[...], carry_in)
        dc = dck_ref[...].T
        dq, dk, dff, dqw, dkw, dbias, dcin = vjp((dqn_ref[...], dkn_ref[...], dc, dcarry[...]))
        dcarry[...] = dcin
        da_ref[:, 0:GW] = dq
        da_ref[:, GW:2 * GW] = dk
        da_ref[:, 2 * GW:3 * GW] = dv_ref[...]
        df_ref[...] = dff
        _acc(dqw_ref, dqw, first)
        _acc(dkw_ref, dkw, first)
        _acc(db_ref, dbias, first)

    rv = lambda b, i: b * nq + (nq - 1 - i)
    tok = lambda j: pl.BlockSpec((TQ, GW), lambda b, i: (rv(b, i), j))
    tok0 = pl.BlockSpec((TQ, GW), lambda b, i: (rv(b, i), 0))
    t128 = pl.BlockSpec((TQ, 128), lambda b, i: (rv(b, i), 0))
    prev = pl.BlockSpec((TQ, 128), lambda b, i: (jnp.maximum(rv(b, i) - 1, 0), 0))
    par = lambda n: pl.BlockSpec((None, 1, n), lambda b, i: (layer, 0, 0))
    acc = lambda n: pl.BlockSpec((1, n), lambda b, i: (0, 0))
    return _call(
        body, f"fox_prep_bwd_{tag}", (bl, nq),
        [tok(0), tok(1), t128, par(GW), par(GW), par(128), t128, prev, tok0, tok0, tok0,
         pl.BlockSpec((None, 128, TQ), lambda b, i: (b, 0, nq - 1 - i))],
        [pl.BlockSpec((TQ, 3 * GW), lambda b, i: (rv(b, i), 0)), t128, acc(GW), acc(GW), acc(128)],
        [_sds((bl * s, 3 * GW)), _sds((bl * s, 128)), _sds((1, GW)), _sds((1, GW)), _sds((1, 128))],
        [pltpu.VMEM((1, 128), F32)],
    )(pa, pa, pf, qw, kw, bias, cq, cq, dqn, dkn, dv, dck)


def _lane_pick(x, h):
    return jnp.sum(x * _onehot_lane(h), axis=-1, keepdims=True)


TA_BIG = 256
TA_FWD = 512
TK_FOX = 512
TK_SB = 256


def _stack_heads(x, scale=1.0):
    return _cast(jnp.concatenate([x * (_hmask(h) * scale) for h in range(NH)], axis=0))


def _stack_cols(x):
    return jnp.concatenate([_lane_pick(x, h) for h in range(NH)], axis=0)


def _spread_heads(col):
    ta = col.shape[0] // NH
    return sum(col[h * ta:(h + 1) * ta] * _hmask(h) for h in range(NH))


def _lanes_cat(w):
    ta = w.shape[0] // NH
    return jnp.concatenate([w[h * ta:(h + 1) * ta] for h in range(NH)], axis=1)


def _mask_stack(x):
    return _cast(jnp.concatenate([x * _hmask(h).astype(x.dtype) for h in range(NH)], axis=0))


def _stack_rows(i, ta):
    return i * ta + (_iota((NH * ta, 1), 0) & (ta - 1))


def _n_key_tiles(i, tk, ta):
    assert ta <= tk and tk % ta == 0, "a query tile's diagonal must lie inside one key tile (only the last key tile is masked)"
    return lax.shift_right_logical(i * ta, tk.bit_length() - 1) + 1


def fox_attn_fwd(qn, kn, vb, cq, ck, bl, s, tag):
    TA, TK = min(TA_FWD, s), min(TK_FOX, s)
    nq, SROWS = s // TA, NH * TA

    def body(q_ref, k_ref, v_ref, cq_ref, ck_ref, o_ref, lse_ref, acc, vst):
        i = pl.program_id(1)

        @pl.when(i == 0)
        def _():
            _fill_stacked(vst, v_ref, s, TK)

        qs = _stack_heads(q_ref[...].astype(F32), SCALE)
        cqs = _stack_cols(cq_ref[...])
        row = _stack_rows(i, TA)
        acc[...] = jnp.zeros_like(acc)

        def step(j, ml):
            m, l = ml
            ks = pl.ds(pl.multiple_of(j * TK, TK), TK)
            ckb = jnp.concatenate([jnp.broadcast_to(ck_ref[h:h + 1, ks], (TA, TK)) for h in range(NH)], axis=0)
            sc = _dg(qs, k_ref[ks, :], 1, 1) + cqs - ckb
            col = j * TK + _iota((1, TK), 1)
            sc = jnp.where(col <= row, sc, NEG_BIG)
            m_new = jnp.maximum(m, jnp.max(sc, axis=-1, keepdims=True))
            alpha = jnp.exp(m - m_new)
            p = jnp.exp(sc - m_new)
            vs = vst[pl.ds(pl.multiple_of(j * NH * TK, NH * TK), NH * TK), :]
            acc[...] = _spread_heads(alpha) * acc[...] + _dg(_lanes_cat(_cast(p)), vs, 1, 0)
            return m_new, alpha * l + jnp.sum(p, axis=-1, keepdims=True)

        m, l = lax.fori_loop(0, _n_key_tiles(i, TK, TA), step, (jnp.full((SROWS, 1), NEG_BIG, F32), jnp.zeros((SROWS, 1), F32)))
        o_ref[...] = acc[...] / _spread_heads(l)
        lse_h = m + jnp.log(l)
        lse_ref[...] = sum(lse_h[h * TA:(h + 1) * TA] * _onehot_lane(h) for h in range(NH))

    tok = pl.BlockSpec((TA, GW), lambda b, i: (b * nq + i, 0))
    seq = pl.BlockSpec((s, GW), lambda b, i: (b, 0))
    t128 = pl.BlockSpec((TA, 128), lambda b, i: (b * nq + i, 0))
    return _call(
        body, f"fox_attn_fwd_{tag}", (bl, nq),
        [tok, seq, seq, t128, pl.BlockSpec((None, 8, s), lambda b, i: (b, 0, 0))],
        [tok, t128], [_sds((bl * s, GW)), _sds((bl * s, 128))],
        [pltpu.VMEM((TA, GW), F32), pltpu.VMEM((NH * s, GW), _MMT)],
    )(qn, kn, vb, cq, ck)


def fox_attn_bwd(qn, kn, vb, cq, ck, lse, do, bl, s, tag):
    TA, TK = min(TA_BIG, s), min(TK_FOX, s)
    nq, SROWS = s // TA, NH * TA

    def body(q_ref, k_ref, v_ref, cq_ref, ck_ref, lse_ref, do_ref, dq_ref, dk_ref, dv_ref, dck_ref, dqa, p_s, dp_s, kst):
        i = pl.program_id(1)

        @pl.when(i == 0)
        def _():
            dk_ref[...] = jnp.zeros_like(dk_ref)
            dv_ref[...] = jnp.zeros_like(dv_ref)
            dck_ref[...] = jnp.zeros_like(dck_ref)
            _fill_stacked(kst, k_ref, s, TK)

        qs = _stack_heads(q_ref[...].astype(F32), SCALE)
        dos = _stack_heads(do_ref[...])
        cqs, lses = _stack_cols(cq_ref[...]), _stack_cols(lse_ref[...])
        row = _stack_rows(i, TA)
        dqa[...] = jnp.zeros_like(dqa)
        nk = _n_key_tiles(i, TK, TA)

        def probs(j, delta):
            ks = pl.ds(pl.multiple_of(j * TK, TK), TK)
            ckb = jnp.concatenate([jnp.broadcast_to(ck_ref[h:h + 1, ks], (TA, TK)) for h in range(NH)], axis=0)
            sc = _dg(qs, k_ref[ks, :], 1, 1) + cqs - ckb
            col = j * TK + _iota((1, TK), 1)
            p = jnp.where(col <= row, jnp.exp(sc - lses), 0.0)
            dp = _dg(dos, v_ref[ks, :], 1, 1)
            p_s[:, ks] = p
            dp_s[:, ks] = dp
            return delta + jnp.sum(p * dp, axis=-1, keepdims=True)

        delta = lax.fori_loop(0, nk, probs, jnp.zeros((SROWS, 1), F32))

        def step(j, carry):
            ks = pl.ds(pl.multiple_of(j * TK, TK), TK)
            p = p_s[:, ks]
            ds = p * (dp_s[:, ks] - delta)
            dsb = _cast(ds)
            dqa[...] += _dg(_lanes_cat(dsb), kst[pl.ds(pl.multiple_of(j * NH * TK, NH * TK), NH * TK), :], 1, 0) * SCALE
            dk_ref[ks, :] += _dg(dsb, qs, 0, 0)
            dv_ref[ks, :] += _dg(_cast(p), dos, 0, 0)
            for h in range(NH):
                dck_ref[h:h + 1, ks] -= jnp.sum(ds[h * TA:(h + 1) * TA], axis=0, keepdims=True)
            return carry

        lax.fori_loop(0, nk, step, 0)
        dq_ref[...] = dqa[...]

    tok = pl.BlockSpec((TA, GW), lambda b, i: (b * nq + i, 0))
    seq = pl.BlockSpec((s, GW), lambda b, i: (b, 0))
    t128 = pl.BlockSpec((TA, 128), lambda b, i: (b * nq + i, 0))
    return _call(
        body, f"fox_attn_bwd_{tag}", (bl, nq),
        [tok, seq, seq, t128, pl.BlockSpec((None, 8, s), lambda b, i: (b, 0, 0)), t128, tok],
        [tok, seq, seq, pl.BlockSpec((None, 128, s), lambda b, i: (b, 0, 0))],
        [_sds((bl * s, GW)), _sds((bl * s, GW)), _sds((bl * s, GW)), _sds((bl, 128, s))],
        [pltpu.VMEM((TA, GW), F32), pltpu.VMEM((SROWS, s), F32), pltpu.VMEM((SROWS, s), F32), pltpu.VMEM((NH * s, GW), _MMT)],
    )(qn, kn, vb, cq, ck, lse, do)


def _sb_block(qh, kb, valid, upper, r_carry):
    z = _dg(qh, kb, 1, 1)
    ls = _logsig(z)
    lom = ls - z if valid is None else jnp.where(valid, ls - z, 0.0)
    between = xr(lom, upper, upper) + r_carry
    w = jnp.exp(ls + between)
    return ls, lom, (w if valid is None else jnp.where(valid, w, 0.0))


def _fill_stacked(dst, src_ref, s, tk):
    for j in range(s // tk):
        dst[j * NH * tk:(j + 1) * NH * tk, :] = _mask_stack(src_ref[j * tk:(j + 1) * tk, :])


def sb_attn_fwd(pb, bl, s, tag):
    TA, TK = TA_BIG, TK_SB
    nq, SROWS = s // TA, NH * TA

    def body(q_ref, k_ref, v_ref, o_ref, acc, vst):
        i = pl.program_id(1)

        @pl.when(i == 0)
        def _():
            _fill_stacked(vst, v_ref, s, TK)

        qs = _stack_heads(q_ref[...], SCALE)
        upper = _tri(TK, "lt")
        last = _n_key_tiles(i, TK, TA) - 1

        def step(j, r, valid):
            ks = pl.ds(pl.multiple_of(j * TK, TK), TK)
            _, lom, w = _sb_block(qs, _cast(k_ref[ks, :]), valid, upper, r)
            acc[...] += _dg(_lanes_cat(_cast(w)), vst[pl.ds(pl.multiple_of(j * NH * TK, NH * TK), NH * TK), :], 1, 0)
            return r + jnp.sum(lom, axis=-1, keepdims=True)

        acc[...] = jnp.zeros_like(acc)
        r = step(last, jnp.zeros((SROWS, 1), F32), last * TK + _iota((1, TK), 1) < _stack_rows(i, TA))
        lax.fori_loop(0, last, lambda jj, r: step(last - 1 - jj, r, None), r)
        o_ref[...] = acc[...]

    tok = lambda j: pl.BlockSpec((TA, GW), lambda b, i: (b * nq + i, j))
    seq = lambda j: pl.BlockSpec((s, GW), lambda b, i: (b, j))
    return _call(
        body, f"sb_attn_fwd_{tag}", (bl, nq), [tok(0), seq(1), seq(2)],
        pl.BlockSpec((TA, GW), lambda b, i: (b * nq + i, 0)), _sds((bl * s, GW)),
        [pltpu.VMEM((TA, GW), F32), pltpu.VMEM((NH * s, GW), _MMT)],
    )(pb, pb, pb)


def sb_attn_bwd(pb, do, bl, s, tag):
    TA, TK = TA_BIG, TK_SB
    nq, SROWS = s // TA, NH * TA

    def body(q_ref, k_ref, v_ref, do_ref, dq_ref, dk_ref, dv_ref, dqa, sig_s, nsig_s, w_s, g_s, kst):
        i = pl.program_id(1)

        @pl.when(i == 0)
        def _():
            dk_ref[...] = jnp.zeros_like(dk_ref)
            dv_ref[...] = jnp.zeros_like(dv_ref)
            _fill_stacked(kst, k_ref, s, TK)

        qs = _stack_heads(q_ref[...], SCALE)
        dos = _stack_heads(do_ref[...])
        upper = _tri(TK, "lt")
        before = _tri(TK, "gt")
        dqa[...] = jnp.zeros_like(dqa)
        last = _n_key_tiles(i, TK, TA) - 1
        diag = last * TK + _iota((1, TK), 1) < _stack_rows(i, TA)

        def weights(j, r, valid):
            ks = pl.ds(pl.multiple_of(j * TK, TK), TK)
            ls, lom, w = _sb_block(qs, _cast(k_ref[ks, :]), valid, upper, r)
            sig_s[:, ks] = _cast(jnp.exp(ls))
            nsig_s[:, ks] = _cast(jnp.exp(lom))
            w_s[:, ks] = _cast(w)
            g_s[:, ks] = _dg(dos, _cast(v_ref[ks, :]), 1, 1) * w
            return r + jnp.sum(lom, axis=-1, keepdims=True)

        r = weights(last, jnp.zeros((SROWS, 1), F32), diag)
        lax.fori_loop(0, last, lambda jj, r: weights(last - 1 - jj, r, None), r)

        def step(j, cpre, valid):
            ks = pl.ds(pl.multiple_of(j * TK, TK), TK)
            g = g_s[:, ks]
            pre = cpre + xr(g, before, before)
            dz = g * nsig_s[:, ks].astype(F32) - sig_s[:, ks].astype(F32) * pre
            dzb = _cast(dz if valid is None else jnp.where(valid, dz, 0.0))
            dqa[...] += _dg(_lanes_cat(dzb), kst[pl.ds(pl.multiple_of(j * NH * TK, NH * TK), NH * TK), :], 1, 0) * SCALE
            dk_ref[ks, :] += _dg(dzb, qs, 0, 0)
            dv_ref[ks, :] += _dg(w_s[:, ks], dos, 0, 0)
            return cpre + jnp.sum(g, axis=-1, keepdims=True)

        cpre = lax.fori_loop(0, last, lambda j, c: step(j, c, None), jnp.zeros((SROWS, 1), F32))
        step(last, cpre, diag)
        dq_ref[...] = dqa[...]

    tok = lambda j: pl.BlockSpec((TA, GW), lambda b, i: (b * nq + i, j))
    seq = lambda j: pl.BlockSpec((s, GW), lambda b, i: (b, j))
    return _call(
        body, f"sb_attn_bwd_{tag}", (bl, nq), [tok(0), seq(1), seq(2), tok(0)],
        [tok(0), seq(0), seq(0)], [_sds((bl * s, GW))] * 3,
        [pltpu.VMEM((TA, GW), F32), pltpu.VMEM((SROWS, s), _MMT), pltpu.VMEM((SROWS, s), _MMT),
         pltpu.VMEM((SROWS, s), _MMT), pltpu.VMEM((SROWS, s), F32), pltpu.VMEM((NH * s, GW), _MMT)],
    )(pb, pb, pb, do)


def _hgrn_consts():
    r, c = _iota((CH, CH), 0), _iota((CH, CH), 1)
    rr = _iota((CH, 1), 0)
    tri = (c <= r).astype(F32)
    lv = []
    for m in (8, 4, 2, 1):
        up = ((rr & (2 * m - 1)) >= m).astype(F32)
        selq = (((r & (2 * m - 1)) >= m) & (c == (r & ~(m - 1)) - 1)).astype(F32)
        selk = (((r & (2 * m - 1)) < m) & (c == (r & ~(m - 1)) + m - 1)).astype(F32)
        pm = (((r & ~(2 * m - 1)) == (c & ~(2 * m - 1))) & ((r & (2 * m - 1)) >= m) & ((c & (2 * m - 1)) < m)).astype(F32)
        lv.append((up, 1.0 - up, selq, selq.T, selk, selk.T, jnp.concatenate([pm] * NH, axis=0)))
    hm4 = lambda n: (((_iota((NH, 1, n), 2) & (GW - 1)) >> 6) == _iota((NH, 1, n), 0)).astype(F32)
    return dict(tri=tri, trit=tri.T, rr=rr, lv=lv, bd=_bdmask(), bd64=_bdmask() * (1.0 / HD),
                hm4={GW: hm4(GW), 3 * GW: hm4(3 * GW)})


def _hgrn_chunk_fn(hq, hf, hi, lb, wn, st, cs):
    q = _silu(hq)
    log_lb = jnp.log(jnp.maximum(lb, LB_FLOOR))
    a, bb = log_lb, jnp.log1p(-lb) + _logsig(hf)
    g = jnp.maximum(a, bb) + jnp.log1p(jnp.exp(-jnp.abs(a - bb)))
    k = (1.0 - lb) * _sigmoid(-hf)
    v = hi
    rr = cs["rr"]
    b = xl(cs["tri"], cs["trit"], g)
    row_of = lambda n: jnp.sum(b * (rr == n).astype(F32), axis=0, keepdims=True)
    o = mm_nt(q * jnp.exp(b), st)
    qs, ks = [], []
    for ib in (1, 2, 3):
        ref = row_of(16 * ib - 1)
        inq = ((rr >= 16 * ib) & (rr < 16 * ib + 16)).astype(F32)
        ink = (rr < 16 * ib).astype(F32)
        qs.append(q * jnp.exp((b - ref) * inq) * inq)
        ks.append(k * jnp.exp((ref - b) * ink) * ink)
    qcat, kcat = jnp.concatenate(qs, axis=1), jnp.concatenate(ks, axis=1)
    lvl = []
    for up, lo, selq, selqt, selk, selkt, pm in cs["lv"]:
        qe = q * jnp.exp((b - xl(selq, selqt, b)) * up) * up
        ke = k * jnp.exp((xl(selk, selkt, b) - b) * lo) * lo
        lvl.append((qe, ke, pm))
    stack = lambda x: (x[None] * cs["hm4"][x.shape[1]]).reshape(NH * CH, x.shape[1])
    a_all = mm_nt(stack(qcat), kcat)
    for qe, ke, pm4 in lvl:
        a_all = a_all + mm_nt(stack(qe), ke) * pm4
    o = o + jnp.sum(mm(a_all, v).reshape(NH, CH, GW) * cs["hm4"][GW], axis=0)
    o = o + xr(q * k, cs["bd"], cs["bd"]) * v
    b_last = row_of(CH - 1)
    st_new = st * jnp.exp(b_last) + mm_tn(v, k * jnp.exp(b_last - b)) * cs["bd"]
    return _headrms(o, wn, cs["bd64"]), st_new


def hgrn_fwd(pc, lb, wn, bl, s, layer, tag):
    nc = s // CH

    def body(q_ref, f_ref, i_ref, lb_ref, wn_ref, o_ref, st_ref, st):
        @pl.when(pl.program_id(0) == 0)
        def _():
            st[...] = jnp.zeros_like(st)

        cs = _hgrn_consts()
        for b in range(bl):
            st_ref[b] = st[b]
            o, st_new = _hgrn_chunk_fn(q_ref[b], f_ref[b], i_ref[b], lb_ref[...], wn_ref[...], st[b], cs)
            o_ref[b] = o
            st[b] = st_new

    tok = lambda j: pl.BlockSpec((bl, CH, GW), lambda c: (0, c, j))
    par = pl.BlockSpec((None, 1, GW), lambda c: (layer, 0, 0))
    pc3 = pc.reshape(bl, s, 3 * GW)
    o, states = _call(
        body, f"hgrn_fwd_{tag}", (nc,), [tok(0), tok(1), tok(2), par, par],
        [tok(0), pl.BlockSpec((bl, None, GW, GW), lambda c: (0, c, 0, 0))],
        [_sds((bl, s, GW)), _sds((bl, nc, GW, GW))],
        [pltpu.VMEM((bl, GW, GW), F32)],
    )(pc3, pc3, pc3, lb, wn)
    return o.reshape(bl * s, GW), states


def hgrn_bwd(pc, lb, wn, states, do, bl, s, layer, tag):
    nc = s // CH

    def body(q_ref, f_ref, i_ref, lb_ref, wn_ref, st_ref, do_ref, dc_ref, dlb_ref, dwn_ref, dst):
        c = pl.program_id(0)

        @pl.when(c == 0)
        def _():
            dst[...] = jnp.zeros_like(dst)

        cs = _hgrn_consts()
        dlb_sum = dwn_sum = None
        for b in range(bl):
            _, vjp = jax.vjp(lambda *a: _hgrn_chunk_fn(*a, cs), q_ref[b], f_ref[b], i_ref[b], lb_ref[...],
                             wn_ref[...], st_ref[b])
            dq, df, di, dlb, dwn, dst_in = vjp((do_ref[b], dst[b]))
            dst[b] = dst_in
            dc_ref[b, :, 0:GW] = dq
            dc_ref[b, :, GW:2 * GW] = df
            dc_ref[b, :, 2 * GW:3 * GW] = di
            dlb_sum = dlb if dlb_sum is None else dlb_sum + dlb
            dwn_sum = dwn if dwn_sum is None else dwn_sum + dwn
        _acc(dlb_ref, dlb_sum, c == 0)
        _acc(dwn_ref, dwn_sum, c == 0)

    tok = lambda j: pl.BlockSpec((bl, CH, GW), lambda c: (0, nc - 1 - c, j))
    par = pl.BlockSpec((None, 1, GW), lambda c: (layer, 0, 0))
    acc = pl.BlockSpec((1, GW), lambda c: (0, 0))
    pc3 = pc.reshape(bl, s, 3 * GW)
    dc, dlb, dwn = _call(
        body, f"hgrn_bwd_{tag}", (nc,),
        [tok(0), tok(1), tok(2), par, par, pl.BlockSpec((bl, None, GW, GW), lambda c: (0, nc - 1 - c, 0, 0)), tok(0)],
        [pl.BlockSpec((bl, CH, 3 * GW), lambda c: (0, nc - 1 - c, 0)), acc, acc],
        [_sds((bl, s, 3 * GW)), _sds((1, GW)), _sds((1, GW))],
        [pltpu.VMEM((bl, GW, GW), F32)],
    )(pc3, pc3, pc3, lb, wn, states, do.reshape(bl, s, GW))
    return dc.reshape(bl * s, 3 * GW), dlb, dwn


def _shift_rows(x, k, up):
    n = x.shape[0]
    rr = _iota((n, 1), 0)
    if up:
        return jnp.where(rr < n - k, pltpu.roll(x, n - k, 0), 0.0)
    return jnp.where(rr >= k, pltpu.roll(x, k, 0), 0.0)


def _window_sums(x, up):
    s2 = x + _shift_rows(x, 1, up)
    s4 = s2 + _shift_rows(s2, 2, up)
    s8 = s4 + _shift_rows(s4, 4, up)
    s16 = s8 + _shift_rows(s8, 8, up)
    return s2, s4, s8, s16


def _pool_div(n):
    pos = (_iota((n, 1), 0) + 1).astype(F32)
    return [jnp.minimum(pos, float(w)) for w in (2, 4, 8, 16)]


def _pool_mix(sums, scaled):
    out = None
    for gi, sw in enumerate(sums):
        part = (sw if scaled is None else sw / scaled[gi]) * _hmask(gi)
        out = part if out is None else out + part
    return out


def pool_fwd(pd, wbd, scale, bl, s, layer, tag):
    def body(u_ref, w_ref, sc_ref, o_ref):
        u = u_ref[...]
        pm = _pool_mix(_window_sums(u, False), _pool_div(s)) - u
        o_ref[...] = _dg(_cast(pm), _cast(w_ref[...]), 1, 0) * sc_ref[...]

    seq = pl.BlockSpec((s, GW), lambda b: (b, 0))
    return _call(
        body, f"pool_fwd_{tag}", (bl,),
        [seq, pl.BlockSpec((None, GW, GW), lambda b: (layer, 0, 0)), pl.BlockSpec((None, 1, GW), lambda b: (layer, 0, 0))],
        seq, _sds((bl * s, GW)),
    )(pd, wbd, scale)


def pool_bwd(pd, wbd, scale, do, bl, s, layer, tag):
    def body(u_ref, w_ref, sc_ref, do_ref, du_ref, dw_ref, dsc_ref):
        first = pl.program_id(0) == 0
        u, do = u_ref[...], do_ref[...]
        div = _pool_div(s)
        pm = _pool_mix(_window_sums(u, False), div) - u
        ypre = _dg(_cast(pm), _cast(w_ref[...]), 1, 0)
        dys = do * sc_ref[...]
        _acc(dsc_ref, jnp.sum(do * ypre, axis=0, keepdims=True), first)
        _acc(dw_ref, _dg(_cast(pm), _cast(dys), 0, 0), first)
        dpm = _dg(_cast(dys), _cast(w_ref[...]), 1, 1)
        dsc = [dpm / d for d in div]
        adj = None
        for gi in range(4):
            part = _window_sums(dsc[gi] * _hmask(gi), True)[gi]
            adj = part if adj is None else adj + part
        du_ref[...] = adj - dpm

    seq = pl.BlockSpec((s, GW), lambda b: (b, 0))
    return _call(
        body, f"pool_bwd_{tag}", (bl,),
        [seq, pl.BlockSpec((None, GW, GW), lambda b: (layer, 0, 0)), pl.BlockSpec((None, 1, GW), lambda b: (layer, 0, 0)), seq],
        [seq, pl.BlockSpec((GW, GW), lambda b: (0, 0)), pl.BlockSpec((1, GW), lambda b: (0, 0))],
        [_sds((bl * s, GW)), _sds((GW, GW)), _sds((1, GW))],
    )(pd, wbd, scale, do)


def _mem_prep_fn(mem, g, wk, wv, kw, bd64):
    mn = _rms(mem, g)
    return _headrms(mm(mn, wk), kw, bd64), mm(mn, wv)


def mem_prep_fwd(mem2, g, wkv, kw, bl, layer, tag):
    def body(m_ref, g_ref, wk_ref, wv_ref, kw_ref, k_ref, v_ref):
        k, v = _mem_prep_fn(m_ref[...], g_ref[...], wk_ref[...], wv_ref[...], kw_ref[...], _bdmask() * (1.0 / HD))
        k_ref[...] = k
        v_ref[...] = v

    blk = pl.BlockSpec((N_MEM, GW), lambda b: (b, 0))
    return _call(
        body, f"mem_prep_fwd_{tag}", (bl,),
        [pl.BlockSpec((N_MEM, D_MODEL), lambda b: (b, 0)), pl.BlockSpec((None, 1, D_MODEL), lambda b: (layer, 0, 0)),
         pl.BlockSpec((None, D_MODEL, GW), lambda b: (layer, 0, 0)), pl.BlockSpec((None, D_MODEL, GW), lambda b: (layer, 0, 1)),
         pl.BlockSpec((None, 1, GW), lambda b: (layer, 0, 0))],
        [blk, blk], [_sds((bl * N_MEM, GW))] * 2,
    )(mem2, g, wkv, wkv, kw)


def mem_prep_bwd(mem2, g, wkv, kw, dk, dv, bl, layer, tag):
    def body(m_ref, g_ref, wk_ref, wv_ref, kw_ref, dk_ref, dv_ref, dwk_ref, dwv_ref, dg_ref, dkw_ref):
        first = pl.program_id(0) == 0
        bd64 = _bdmask() * (1.0 / HD)
        _, vjp = jax.vjp(lambda g_, wk, wv, kw_: _mem_prep_fn(m_ref[...], g_, wk, wv, kw_, bd64),
                         g_ref[...], wk_ref[...].astype(F32), wv_ref[...].astype(F32), kw_ref[...])
        dg, dwk, dwv, dkw = vjp((dk_ref[...], dv_ref[...]))
        _acc(dwk_ref, dwk, first)
        _acc(dwv_ref, dwv, first)
        _acc(dg_ref, dg, first)
        _acc(dkw_ref, dkw, first)

    blk = pl.BlockSpec((N_MEM, GW), lambda b: (b, 0))
    return _call(
        body, f"mem_prep_bwd_{tag}", (bl,),
        [pl.BlockSpec((N_MEM, D_MODEL), lambda b: (b, 0)), pl.BlockSpec((None, 1, D_MODEL), lambda b: (layer, 0, 0)),
         pl.BlockSpec((None, D_MODEL, GW), lambda b: (layer, 0, 0)), pl.BlockSpec((None, D_MODEL, GW), lambda b: (layer, 0, 1)),
         pl.BlockSpec((None, 1, GW), lambda b: (layer, 0, 0)), blk, blk],
        [pl.BlockSpec((D_MODEL, GW), lambda b: (0, 0)), pl.BlockSpec((D_MODEL, GW), lambda b: (0, 0)),
         pl.BlockSpec((1, D_MODEL), lambda b: (0, 0)), pl.BlockSpec((1, GW), lambda b: (0, 0))],
        [_sds((D_MODEL, GW)), _sds((D_MODEL, GW)), _sds((1, D_MODEL)), _sds((1, GW))],
    )(mem2, g, wkv, wkv, kw, dk, dv)


def _mem_attn_fn(mq, qw, k, v, bd64):
    qn = _headrms(mq, qw, bd64)
    out = None
    for h in range(NH):
        hm = _hmask(h)
        lg = mm_nt(qn * hm, k) * SCALE
        e = jnp.exp(lg - lax.stop_gradient(jnp.max(lg, axis=-1, keepdims=True)))
        p = e / jnp.sum(e, axis=-1, keepdims=True)
        part = mm(p, v) * hm
        out = part if out is None else out + part
    return out


def mem_attn_fwd(pe, qw, k, v, bl, s, layer, tag):
    TQ = _token_tile(s)
    nq = s // TQ

    def body(q_ref, qw_ref, k_ref, v_ref, o_ref):
        o_ref[...] = _mem_attn_fn(q_ref[...], qw_ref[...], k_ref[...], v_ref[...], _bdmask() * (1.0 / HD))

    tok = pl.BlockSpec((TQ, GW), lambda b, i: (b * nq + i, 0))
    kv = pl.BlockSpec((N_MEM, GW), lambda b, i: (b, 0))
    return _call(
        body, f"mem_attn_fwd_{tag}", (bl, nq), [tok, pl.BlockSpec((None, 1, GW), lambda b, i: (layer, 0, 0)), kv, kv],
        tok, _sds((bl * s, GW)),
    )(pe, qw, k, v)


def mem_attn_bwd(pe, qw, k, v, do, bl, s, layer, tag):
    TQ = _token_tile(s)
    nq = s // TQ

    def body(q_ref, qw_ref, k_ref, v_ref, do_ref, dq_ref, dk_ref, dv_ref, dqw_ref):
        i = pl.program_id(1)
        bd64 = _bdmask() * (1.0 / HD)
        _, vjp = jax.vjp(lambda *a: _mem_attn_fn(*a, bd64), q_ref[...], qw_ref[...], k_ref[...], v_ref[...])
        dq, dqw, dk, dv = vjp(do_ref[...])
        dq_ref[...] = dq
        _acc(dk_ref, dk, i == 0)
        _acc(dv_ref, dv, i == 0)
        _acc(dqw_ref, dqw, jnp.logical_and(pl.program_id(0) == 0, i == 0))

    tok = pl.BlockSpec((TQ, GW), lambda b, i: (b * nq + i, 0))
    kv = pl.BlockSpec((N_MEM, GW), lambda b, i: (b, 0))
    return _call(
        body, f"mem_attn_bwd_{tag}", (bl, nq),
        [tok, pl.BlockSpec((None, 1, GW), lambda b, i: (layer, 0, 0)), kv, kv, tok],
        [tok, kv, kv, pl.BlockSpec((1, GW), lambda b, i: (0, 0))],
        [_sds((bl * s, GW)), _sds((bl * N_MEM, GW)), _sds((bl * N_MEM, GW)), _sds((1, GW))],
    )(pe, qw, k, v, do)


def _gate_out_fn(outs, gates, wparts):
    y = None
    for o, g, w in zip(outs, gates, wparts):
        part = mm(o * _silu(g), w)
        y = part if y is None else y + part
    return y


def outproj_fwd(x2, outs, pg, wout, layer, tag):
    t = x2.shape[0]
    TQ = _token_tile(t)

    def body(x_ref, oa, ob, oc, od, oe, g_ref, w_ref, y_ref):
        outs_ = [r[...] for r in (oa, ob, oc, od, oe)]
        gates = [g_ref[:, j * GW:(j + 1) * GW] for j in range(5)]
        wparts = [w_ref[j * GW:(j + 1) * GW, :] for j in range(5)]
        y_ref[...] = x_ref[...] + _gate_out_fn(outs_, gates, wparts)

    tok = pl.BlockSpec((TQ, GW), lambda i: (i, 0))
    big = pl.BlockSpec((TQ, D_MODEL), lambda i: (i, 0))
    return _call(
        body, f"outproj_fwd_{tag}", (t // TQ,),
        [big] + [tok] * 5 + [pl.BlockSpec((TQ, D_MIX), lambda i: (i, 0)),
                            pl.BlockSpec((None, D_MIX, D_MODEL), lambda i: (layer, 0, 0))],
        big, _sds((t, D_MODEL)),
    )(x2, *outs, pg, wout)


def outproj_bwd(outs, pg, wout, dy, layer, tag):
    t = dy.shape[0]
    TQ = _token_tile(t)

    def body(oa, ob, oc, od, oe, g_ref, w_ref, dy_ref, da, db, dc, dd, de, dg_ref, dw_ref):
        outs_ = [r[...] for r in (oa, ob, oc, od, oe)]
        gates = [g_ref[:, j * GW:(j + 1) * GW] for j in range(5)]
        wparts = [w_ref[j * GW:(j + 1) * GW, :].astype(F32) for j in range(5)]
        _, vjp = jax.vjp(_gate_out_fn, outs_, gates, wparts)
        douts, dgates, dws = vjp(dy_ref[...])
        for r, val in zip((da, db, dc, dd, de), douts):
            r[...] = val
        first = pl.program_id(0) == 0
        for j in range(5):
            dg_ref[:, j * GW:(j + 1) * GW] = dgates[j]

        @pl.when(first)
        def _():
            for j in range(5):
                dw_ref[j * GW:(j + 1) * GW, :] = dws[j]

        @pl.when(jnp.logical_not(first))
        def _():
            for j in range(5):
                dw_ref[j * GW:(j + 1) * GW, :] += dws[j]

    tok = pl.BlockSpec((TQ, GW), lambda i: (i, 0))
    return _call(
        body, f"outproj_bwd_{tag}", (t // TQ,),
        [tok] * 5 + [pl.BlockSpec((TQ, D_MIX), lambda i: (i, 0)), pl.BlockSpec((None, D_MIX, D_MODEL), lambda i: (layer, 0, 0)),
                     pl.BlockSpec((TQ, D_MODEL), lambda i: (i, 0))],
        [tok] * 5 + [pl.BlockSpec((TQ, D_MIX), lambda i: (i, 0)), pl.BlockSpec((D_MIX, D_MODEL), lambda i: (0, 0))],
        [_sds((t, GW))] * 5 + [_sds((t, D_MIX)), _sds((D_MIX, D_MODEL))],
    )(*outs, pg, wout, dy)


def loss_head(y, tgt):
    t = y.shape[0]
    TQ = _token_tile(t)

    def body(y_ref, t_ref, l_ref, dy_ref):
        diff = y_ref[...] - t_ref[...]
        dy_ref[...] = diff * (1.0 / D_MODEL)
        part = 0.5 * jnp.sum(jnp.sum(diff * diff, axis=-1, keepdims=True) * (1.0 / D_MODEL), axis=0, keepdims=True)
        _acc(l_ref, jnp.broadcast_to(part, (8, 128)), pl.program_id(0) == 0)

    big = pl.BlockSpec((TQ, D_MODEL), lambda i: (i, 0))
    return _call(body, "loss_head", (t // TQ,), [big, big], [pl.BlockSpec((8, 128), lambda i: (0, 0)), big],
                 [_sds((8, 128)), _sds((t, D_MODEL))])(y, tgt)


def layer_fwd(x2, mem2, p, layer, bl, s):
    tag = f"l{layer}"
    ht, pa, pb, pc, pd, pe, pg, pf = inproj_fwd(x2, p["norm_g"], p["w_in"], layer, tag)
    qn, kn, vb, cq, ck = fox_prep_fwd(pa, pf, p["fox_q_norm"], p["fox_k_norm"], p["fox_f_bias"], bl, s, layer, tag)
    oa, lse = fox_attn_fwd(qn, kn, vb, cq, ck, bl, s, tag)
    ob = sb_attn_fwd(pb, bl, s, tag)
    oc, states = hgrn_fwd(pc, p["lb"], p["hgrn_out_norm"], bl, s, layer, tag)
    od = pool_fwd(pd, p["pool_wbd"], p["pool_scale"], bl, s, layer, tag)
    mk, mv = mem_prep_fwd(mem2, p["mem_norm_g"], p["mem_w_kv"], p["mem_k_norm"], bl, layer, tag)
    oe = mem_attn_fwd(pe, p["mem_q_norm"], mk, mv, bl, s, layer, tag)
    y = outproj_fwd(x2, (oa, ob, oc, od, oe), pg, p["w_out"], layer, tag)
    saved = dict(x2=x2, ht=ht, pa=pa, pb=pb, pc=pc, pd=pd, pe=pe, pg=pg, pf=pf, qn=qn, kn=kn, vb=vb, cq=cq, ck=ck,
                 oa=oa, lse=lse, ob=ob, oc=oc, states=states, od=od, mk=mk, mv=mv, oe=oe)
    return y, saved


def layer_bwd(dy, mem2, p, sv, layer, bl, s):
    tag = f"l{layer}"
    (doa, dob, doc, dod, doe, dg_gates, dwout) = outproj_bwd((sv["oa"], sv["ob"], sv["oc"], sv["od"], sv["oe"]), sv["pg"],
                                                              p["w_out"], dy, layer, tag)
    dqn, dkn, dv, dck = fox_attn_bwd(sv["qn"], sv["kn"], sv["vb"], sv["cq"], sv["ck"], sv["lse"], doa, bl, s, tag)
    d_a, d_f, dqw, dkw, dbias = fox_prep_bwd(sv["pa"], sv["pf"], p["fox_q_norm"], p["fox_k_norm"], p["fox_f_bias"], sv["cq"],
                                             dqn, dkn, dv, dck, bl, s, layer, tag)
    dsq, dsk, dsv = sb_attn_bwd(sv["pb"], dob, bl, s, tag)
    d_c, dlb, dwn = hgrn_bwd(sv["pc"], p["lb"], p["hgrn_out_norm"], sv["states"], doc, bl, s, layer, tag)
    d_d, dwbd, dpscale = pool_bwd(sv["pd"], p["pool_wbd"], p["pool_scale"], dod, bl, s, layer, tag)
    d_e, dmk, dmv, dmqw = mem_attn_bwd(sv["pe"], p["mem_q_norm"], sv["mk"], sv["mv"], doe, bl, s, layer, tag)
    dwk, dwv, dmg, dmkw = mem_prep_bwd(mem2, p["mem_norm_g"], p["mem_w_kv"], p["mem_k_norm"], dmk, dmv, bl, layer, tag)
    dpieces = (d_a, dsq, dsk, dsv, d_c, d_d, d_e, dg_gates, d_f)
    dx, dng = inproj_bwd_dx(sv["x2"], p["norm_g"], p["w_in"], dy, dpieces, layer, tag)
    groups = ((0, 4), (4, 7), (7, 9))
    dwin = jnp.concatenate([matmul_acc(sv["ht"], dpieces[a:b], f"{tag}_{a}") for a, b in groups], axis=1)
    grads = dict(norm_g=dng, w_in=dwin, fox_f_bias=dbias, fox_q_norm=dqw, fox_k_norm=dkw, lb=dlb, hgrn_out_norm=dwn,
                 pool_wbd=dwbd, pool_scale=dpscale, mem_norm_g=dmg, mem_w_kv=jnp.concatenate([dwk, dwv], axis=1),
                 mem_q_norm=dmqw, mem_k_norm=dmkw, w_out=dwout)
    return dx, grads


def _tile4(w):
    return jnp.tile(w, (1, NH))[:, None, :]


def prepare_params(norm_g, w_in_p, fox_f_bias, fox_q_norm, fox_k_norm, hgrn_lb_logits, hgrn_out_norm, pool_w, pool_scale,
                   mem_norm_g, mem_w_kv, mem_q_norm, mem_k_norm, w_out):
    p1 = jax.nn.sigmoid(hgrn_lb_logits[1] - hgrn_lb_logits[0])
    lb = jnp.stack([jnp.zeros_like(p1), jnp.clip(p1, 0.0, 1.0 - 1e-6)])
    eye = jnp.eye(4, dtype=F32)
    wbd = jnp.einsum("lgcd,gh->lgchd", pool_w, eye).reshape(2, GW, GW)
    return dict(norm_g=norm_g[:, None, :], w_in=w_in_p, fox_f_bias=jnp.pad(fox_f_bias, ((0, 0), (0, 124)))[:, None, :],
                fox_q_norm=_tile4(fox_q_norm), fox_k_norm=_tile4(fox_k_norm), lb=lb[:, None, :],
                hgrn_out_norm=hgrn_out_norm[:, None, :], pool_wbd=wbd, pool_scale=pool_scale[:, None, :],
                mem_norm_g=mem_norm_g[:, None, :], mem_w_kv=mem_w_kv, mem_q_norm=_tile4(mem_q_norm),
                mem_k_norm=_tile4(mem_k_norm), w_out=w_out)


def local_step(x, mem, tgt, p):
    bl, s, _ = x.shape
    x2, mem2, tgt2 = x.reshape(bl * s, D_MODEL), mem.reshape(bl * N_MEM, D_MODEL), tgt.reshape(bl * s, D_MODEL)
    y0, sv0 = layer_fwd(x2, mem2, p, 0, bl, s)
    y1, sv1 = layer_fwd(y0, mem2, p, 1, bl, s)
    lpart, dy = loss_head(y1, tgt2)
    dx1, g1 = layer_bwd(dy, mem2, p, sv1, 1, bl, s)
    dx0, g0 = layer_bwd(dx1, mem2, p, sv0, 0, bl, s)
    return lpart[0, 0], dx0.reshape(bl, s, D_MODEL), g0, g1


_ANY = pl.BlockSpec(memory_space=pl.ANY)


def all_gather_rows(xss, tag):
    n = len(xss)

    def body(*refs):
        x_refs, o_refs, (send_sems, recv_sems, local_sems) = refs[:n], refs[n:2 * n], refs[2 * n:]
        x, y, cc = lax.axis_index("x"), lax.axis_index("y"), lax.axis_index("c")
        me, sibling = (x, y, cc), (x, y, 1 - cc)
        chips = [(1 - x, y), (x, 1 - y), (1 - x, 1 - y)]

        def rows(a, px, py, pc):
            r = xss[a].shape[1]
            return o_refs[a].at[:, pl.ds((4 * px + 2 * py + pc) * r, r), :]

        def copy(a, k, block, to, src=None):
            return pltpu.make_async_remote_copy(src_ref=rows(a, *block) if src is None else src, dst_ref=rows(a, *block),
                                                send_sem=send_sems.at[7 * a + k], recv_sem=recv_sems.at[7 * a + k], device_id=to,
                                                device_id_type=pl.DeviceIdType.MESH)

        mine = [pltpu.make_async_copy(x_refs[a], rows(a, *me), local_sems.at[a]) for a in range(n)]
        first = []
        for a in range(n):
            first += [copy(a, 0, me, sibling, src=x_refs[a])] + [copy(a, 1 + j, me, (*chip, cc), src=x_refs[a])
                                                                 for j, chip in enumerate(chips)]
        for cp in mine + first:
            cp.start()
        passed = []
        for j, chip in enumerate(chips):
            for a in range(n):
                copy(a, 1 + j, (*chip, cc), me).wait_recv()
                passed.append(copy(a, 4 + j, (*chip, cc), sibling))
                passed[-1].start()
        for a in range(n):
            copy(a, 0, sibling, me).wait_recv()
        for j, chip in enumerate(chips):
            for a in range(n):
                copy(a, 4 + j, (*chip, 1 - cc), me).wait_recv()
        for cp in first + passed:
            cp.wait_send()
        for cp in mine:
            cp.wait()

    nsem = pltpu.SemaphoreType.DMA((7 * n,))
    return pl.pallas_call(
        body, name=f"all_gather_{tag}", in_specs=[_ANY] * n, out_specs=[_ANY] * n,
        out_shape=[_sds((xs.shape[0], N_DEV * xs.shape[1], xs.shape[2]), xs.dtype) for xs in xss],
        scratch_shapes=[nsem, nsem, pltpu.SemaphoreType.DMA((n,))],
    )(*xss)


def exchange_cores(parts, tag):
    n = len(parts)
    counts = [p.shape[0] * 4 for p in parts]

    def body(*refs):
        p_refs, t_refs, (send_sems, recv_sems) = refs[:n], refs[n:2 * n], refs[2 * n:]
        x, y, cc = lax.axis_index("x"), lax.axis_index("y"), lax.axis_index("c")
        copies, k = [], 0
        for a in range(n):
            for l in range(parts[a].shape[0]):
                for q in range(4):
                    copies.append(pltpu.make_async_remote_copy(
                        src_ref=p_refs[a].at[l, q, pl.ds(1 - cc, 1)], dst_ref=t_refs[a].at[l, q], send_sem=send_sems.at[k],
                        recv_sem=recv_sems.at[k], device_id=(x, y, 1 - cc), device_id_type=pl.DeviceIdType.MESH))
                    k += 1
        for cp in copies:
            cp.start()
        for cp in copies:
            cp.wait()

    nsem = pltpu.SemaphoreType.DMA((sum(counts),))
    return pl.pallas_call(
        body, name=f"exchange_cores_{tag}", in_specs=[_ANY] * n, out_specs=[_ANY] * n,
        out_shape=[_sds((p.shape[0], 4, 1, p.shape[3], p.shape[4]), p.dtype) for p in parts],
        scratch_shapes=[nsem, nsem],
    )(*parts)


def add_core_halves(part5, theirs, core, tag):
    nl, _, _, r, c = part5.shape
    tr = r if r * c <= 256 * 1024 else 64

    def body(core_ref, a_ref, b_ref, o_ref):
        o_ref[...] = _cast(a_ref[...] + b_ref[...])

    blk = lambda which: pl.BlockSpec((None, None, None, tr, c), lambda l, q, i, cref: (l, q, cref[0] if which else 0, i, 0))
    return pl.pallas_call(
        body, name=f"add_core_halves_{tag}", out_shape=_sds((nl, 4, 1, r, c), _MMT),
        grid_spec=pltpu.PrefetchScalarGridSpec(num_scalar_prefetch=1, grid=(nl, 4, r // tr), in_specs=[blk(True), blk(False)],
                                               out_specs=blk(False)),
        compiler_params=pltpu.CompilerParams(dimension_semantics=("arbitrary",) * 3, vmem_limit_bytes=VMEM_LIMIT_BYTES),
    )(core, part5, theirs)


def exchange_chips(s4s, tag):
    n = len(s4s)

    def body(*refs):
        s_refs, o_refs, (send_sems, recv_sems, local_sems) = refs[:n], refs[n:2 * n], refs[2 * n:]
        x, y, cc = lax.axis_index("x"), lax.axis_index("y"), lax.axis_index("c")
        copies = [pltpu.make_async_copy(s_refs[a].at[:, pl.ds(2 * x + y, 1)], o_refs[a].at[0], local_sems.at[a]) for a in range(n)]
        for k in range(1, 4):
            px = 1 - x if (k >> 1) & 1 else x
            py = 1 - y if k & 1 else y
            for a in range(n):
                copies.append(pltpu.make_async_remote_copy(
                    src_ref=s_refs[a].at[:, pl.ds(2 * px + py, 1)], dst_ref=o_refs[a].at[k], send_sem=send_sems.at[3 * a + k - 1],
                    recv_sem=recv_sems.at[3 * a + k - 1], device_id=(px, py, cc), device_id_type=pl.DeviceIdType.MESH))
        for cp in copies:
            cp.start()
        for cp in copies:
            cp.wait()

    nsem = pltpu.SemaphoreType.DMA((3 * n,))
    return pl.pallas_call(
        body, name=f"exchange_chips_{tag}", in_specs=[_ANY] * n, out_specs=[_ANY] * n,
        out_shape=[_sds((4, s.shape[0], 1, 1, s.shape[3], s.shape[4]), s.dtype) for s in s4s],
        scratch_shapes=[nsem, nsem, pltpu.SemaphoreType.DMA((n,))],
    )(*s4s)


def _row_tile(rows):
    if rows <= 512 and rows % 64:
        return rows
    for t in (64, 40, 32, 16, 8):
        if rows % t == 0:
            return t
    return rows


def sum_slots(slots, tag):
    ns, rows, c = slots.shape
    tr = _row_tile(rows)

    def body(s_ref, o_ref):
        acc = s_ref[0].astype(F32)
        for k in range(1, ns):
            acc = acc + s_ref[k].astype(F32)
        o_ref[...] = acc

    return _call(body, f"sum_slots_{tag}", (rows // tr,), [pl.BlockSpec((ns, tr, c), lambda i: (0, i, 0))],
                 pl.BlockSpec((tr, c), lambda i: (i, 0)), _sds((rows, c)))(slots)


def _adamw(w, g, m, v):
    m = ADAM_B1 * m + (1.0 - ADAM_B1) * g
    v = ADAM_B2 * v + (1.0 - ADAM_B2) * (g * g)
    m_hat = m / (1.0 - ADAM_B1 ** ADAM_STEP)
    v_hat = v / (1.0 - ADAM_B2 ** ADAM_STEP)
    delta = -ADAM_LR * (m_hat / (jnp.sqrt(v_hat) + ADAM_EPS) + ADAM_WD * w)
    return delta, m, v


def adam_update(w, m, v, g, tag, slots=False):
    rows, c = w.shape
    tr = _row_tile(rows)
    ns = g.shape[0] if slots else 0

    def body(w_ref, m_ref, v_ref, g_ref, go_ref, d_ref, mo_ref, vo_ref):
        if slots:
            g = g_ref[0].astype(F32)
            for k in range(1, ns):
                g = g + g_ref[k].astype(F32)
        else:
            g = g_ref[...]
        d, mn, vn = _adamw(w_ref[...], g, m_ref[...], v_ref[...])
        go_ref[...] = g
        d_ref[...] = d
        mo_ref[...] = mn
        vo_ref[...] = vn

    blk = pl.BlockSpec((tr, c), lambda i: (i, 0))
    gspec = pl.BlockSpec((ns, tr, c), lambda i: (0, i, 0)) if slots else blk
    return _call(body, f"adam_{tag}", (rows // tr,), [blk, blk, blk, gspec], [blk] * 4, [_sds((rows, c))] * 4)(w, m, v, g)


_SMALL = (("norm_g", (2, 1024)), ("fox_f_bias", (2, 4)), ("fox_q_norm", (2, 64)), ("fox_k_norm", (2, 64)),
          ("hgrn_lb_logits", (2, 256)), ("hgrn_out_norm", (2, 256)), ("pool_w", (2, 4, 64, 64)), ("pool_scale", (2, 256)),
          ("mem_norm_g", (2, 1024)), ("mem_q_norm", (2, 64)), ("mem_k_norm", (2, 64)))
_SLAB_ROWS = 312


def pack_small(d):
    flat = jnp.concatenate([d[n].reshape(-1) for n, _ in _SMALL])
    return jnp.pad(flat, (0, _SLAB_ROWS * 128 - flat.shape[0])).reshape(_SLAB_ROWS, 128)


def unpack_small(slab):
    flat, out, off = slab.reshape(-1), {}, 0
    for n, shp in _SMALL:
        size = 1
        for e in shp:
            size *= e
        out[n] = flat[off:off + size].reshape(shp)
        off += size
    return out


def small_grads(g0, g1, lb_logits):
    st = lambda f: jnp.stack([f(g0), f(g1)])
    heads = lambda a: a.reshape(NH, HD).sum(0)
    p1 = jax.nn.sigmoid(lb_logits[1] - lb_logits[0])
    inside = (p1 > 0.0) & (p1 < 1.0 - 1e-6)
    dl1 = jnp.where(inside, g1["lb"][0] * p1 * (1.0 - p1), 0.0)
    diag = lambda a: jnp.stack([a.reshape(4, HD, 4, HD)[i, :, i, :] for i in range(4)])
    return dict(norm_g=st(lambda g: g["norm_g"][0]), fox_f_bias=st(lambda g: g["fox_f_bias"][0, :NH]),
                fox_q_norm=st(lambda g: heads(g["fox_q_norm"])), fox_k_norm=st(lambda g: heads(g["fox_k_norm"])),
                hgrn_lb_logits=jnp.stack([-dl1, dl1]), hgrn_out_norm=st(lambda g: g["hgrn_out_norm"][0]),
                pool_w=st(lambda g: diag(g["pool_wbd"])), pool_scale=st(lambda g: g["pool_scale"][0]),
                mem_norm_g=st(lambda g: g["mem_norm_g"][0]), mem_q_norm=st(lambda g: heads(g["mem_q_norm"])),
                mem_k_norm=st(lambda g: heads(g["mem_k_norm"])))


def kernel(x, mem, norm_g, w_in, fox_f_bias, fox_q_norm, fox_k_norm, hgrn_lb_logits, hgrn_out_norm, pool_w, pool_scale, mem_norm_g, mem_w_kv, mem_q_norm, mem_k_norm, w_out, loss_target, m_norm_g, m_w_in, m_fox_f_bias, m_fox_q_norm, m_fox_k_norm, m_hgrn_lb_logits, m_hgrn_out_norm, m_pool_w, m_pool_scale, m_mem_norm_g, m_mem_w_kv, m_mem_q_norm, m_mem_k_norm, m_w_out, v_norm_g, v_w_in, v_fox_f_bias, v_fox_q_norm, v_fox_k_norm, v_hgrn_lb_logits, v_hgrn_out_norm, v_pool_w, v_pool_scale, v_mem_norm_g, v_mem_w_kv, v_mem_q_norm, v_mem_k_norm, v_w_out):
    given = dict(norm_g=(norm_g, m_norm_g, v_norm_g), w_in=(w_in, m_w_in, v_w_in), fox_f_bias=(fox_f_bias, m_fox_f_bias, v_fox_f_bias),
                 fox_q_norm=(fox_q_norm, m_fox_q_norm, v_fox_q_norm), fox_k_norm=(fox_k_norm, m_fox_k_norm, v_fox_k_norm),
                 hgrn_lb_logits=(hgrn_lb_logits, m_hgrn_lb_logits, v_hgrn_lb_logits),
                 hgrn_out_norm=(hgrn_out_norm, m_hgrn_out_norm, v_hgrn_out_norm), pool_w=(pool_w, m_pool_w, v_pool_w),
                 pool_scale=(pool_scale, m_pool_scale, v_pool_scale), mem_norm_g=(mem_norm_g, m_mem_norm_g, v_mem_norm_g),
                 mem_w_kv=(mem_w_kv, m_mem_w_kv, v_mem_w_kv), mem_q_norm=(mem_q_norm, m_mem_q_norm, v_mem_q_norm),
                 mem_k_norm=(mem_k_norm, m_mem_k_norm, v_mem_k_norm), w_out=(w_out, m_w_out, v_w_out))
    order = ("norm_g", "w_in", "fox_f_bias", "fox_q_norm", "fox_k_norm", "hgrn_lb_logits", "hgrn_out_norm", "pool_w",
             "pool_scale", "mem_norm_g", "mem_w_kv", "mem_q_norm", "mem_k_norm", "w_out")

    w_in_full, w_out_full, w_kv_full = all_gather_rows([_cast(permute_cols(w_in)), _cast(w_out), _cast(mem_w_kv)], "weights")
    p = prepare_params(norm_g, w_in_full, fox_f_bias, fox_q_norm, fox_k_norm, hgrn_lb_logits, hgrn_out_norm, pool_w,
                       pool_scale, mem_norm_g, w_kv_full, mem_q_norm, mem_k_norm, w_out_full)

    loss_part, grad_x, g0, g1 = local_step(x, mem, loss_target, p)
    loss = lax.psum(loss_part, ("x", "y", "c"))

    res = {}
    core = lax.axis_index("c").astype(jnp.int32).reshape(1)

    names = ("w_in", "w_out", "mem_w_kv")
    part5 = []
    for name in names:
        nl, r, _ = given[name][0].shape
        g2 = jnp.stack([g0[name], g1[name]])
        part5.append(g2.reshape(nl, 4, 2, r, g2.shape[-1]))
    s4 = [add_core_halves(p5, th, core, name) for name, p5, th in zip(names, part5, exchange_cores(part5, "grads"))]
    for name, sl in zip(names, exchange_chips(s4, "grads")):
        w, m, v = given[name]
        nl, r, c = w.shape
        slots = sl.reshape(4, nl * r, sl.shape[-1])
        if name == "w_in":
            g = sum_slots(slots, name).reshape(nl, r, -1)
            out = adam_update(w.reshape(nl * r, c), m.reshape(nl * r, c), v.reshape(nl * r, c),
                              unpermute_cols(g).reshape(nl * r, c), name)
        else:
            out = adam_update(w.reshape(nl * r, c), m.reshape(nl * r, c), v.reshape(nl * r, c), slots, name, slots=True)
        res[name] = tuple(o.reshape(nl, r, c) for o in out)

    gsmall = pack_small(small_grads(g0, g1, hgrn_lb_logits))
    gathered = all_gather_rows([gsmall[None]], "small")[0].reshape(N_DEV, _SLAB_ROWS, 128)
    slabs = adam_update(*[pack_small({n: given[n][j] for n, _ in _SMALL}) for j in range(3)], gathered, "small", slots=True)
    small = [unpack_small(sl) for sl in slabs]
    for n, _ in _SMALL:
        res[n] = tuple(small[j][n] for j in range(4))

    return (loss, grad_x, *[res[n][0] for n in order], *[res[n][1] for n in order], *[res[n][2] for n in order],
            *[res[n][3] for n in order])
```

```python
import functools

import jax
import jax.numpy as jnp
from jax import lax
from jax.experimental import pallas as pl
from jax.experimental.pallas import tpu as pltpu

F32 = jnp.float32
BF = jnp.bfloat16
_MMT = BF

D_MODEL = 1024
GW = 256
HD = 64
NH = 4
CH = 64
N_MEM = 256
D_IN = 4100
D_INP = 4224
D_MIX = 1280
EPS = 1e-6
NEG_BIG = -1e30
LB_FLOOR = 1e-30
SCALE = HD ** -0.5
TQ = 256
TM = 512


def _token_tile(n):
    return TM if n % TM == 0 else TQ
N_DEV = 8
VMEM_LIMIT_BYTES = 56 * 1024 * 1024

ADAM_LR = 0.001
ADAM_B1 = 0.9
ADAM_B2 = 0.999
ADAM_EPS = 1e-08
ADAM_WD = 0.01
ADAM_STEP = 10

PIECES = (("A", 0, 768), ("B", 768, 768), ("C", 1536, 768), ("D", 2304, 256), ("E", 2560, 256),
          ("G", 2816, 1280), ("F", 4096, 128))
BWD_PIECES = (("A", 0, 768), ("Bq", 768, 256), ("Bk", 1024, 256), ("Bv", 1280, 256), ("C", 1536, 768),
              ("D", 2304, 256), ("E", 2560, 256), ("G", 2816, 1280), ("F", 4096, 128))
_ORIG = dict(fq=(0, 256), fk=(256, 512), fv=(512, 768), fg=(768, 1024), ff=(1024, 1028), sq=(1028, 1284),
             sk=(1284, 1540), sv=(1540, 1796), sg=(1796, 2052), hq=(2052, 2308), hf=(2308, 2564),
             hi=(2564, 2820), hg=(2820, 3076), pv=(3076, 3332), pg=(3332, 3588), mq=(3588, 3844), mg=(3844, 4100))
_PERM_ORDER = ("fq", "fk", "fv", "sq", "sk", "sv", "hq", "hf", "hi", "pv", "mq", "fg", "sg", "hg", "pg", "mg", "ff")
_ORIG_ORDER = ("fq", "fk", "fv", "fg", "ff", "sq", "sk", "sv", "sg", "hq", "hf", "hi", "hg", "pv", "pg", "mq", "mg")


def permute_cols(w):
    parts = [w[..., _ORIG[n][0]:_ORIG[n][1]] for n in _PERM_ORDER]
    parts.append(jnp.zeros(w.shape[:-1] + (D_INP - D_IN,), w.dtype))
    return jnp.concatenate(parts, axis=-1)


def unpermute_cols(g):
    start, off = {}, 0
    for n in _PERM_ORDER:
        start[n] = off
        off += _ORIG[n][1] - _ORIG[n][0]
    return jnp.concatenate([g[..., start[n]:start[n] + _ORIG[n][1] - _ORIG[n][0]] for n in _ORIG_ORDER], axis=-1)


def _cast(a):
    return a.astype(_MMT)


def _dg(a, b, ca, cb):
    return lax.dot_general(a, b, (((ca,), (cb,)), ((), ())), preferred_element_type=F32)


@jax.custom_vjp
def mm(a, b):
    return _dg(_cast(a), _cast(b), 1, 0)


@jax.custom_vjp
def mm_nt(a, b):
    return _dg(_cast(a), _cast(b), 1, 1)


@jax.custom_vjp
def mm_tn(a, b):
    return _dg(_cast(a), _cast(b), 0, 0)


mm.defvjp(lambda a, b: (mm(a, b), (a, b)),
          lambda r, g: (mm_nt(g, r[1]).astype(r[0].dtype), mm_tn(r[0], g).astype(r[1].dtype)))
mm_nt.defvjp(lambda a, b: (mm_nt(a, b), (a, b)),
             lambda r, g: (mm(g, r[1]).astype(r[0].dtype), mm_tn(g, r[0]).astype(r[1].dtype)))
mm_tn.defvjp(lambda a, b: (mm_tn(a, b), (a, b)),
             lambda r, g: (mm_nt(r[1], g).astype(r[0].dtype), mm(r[0], g).astype(r[1].dtype)))


def _split(a):
    hi = a.astype(_MMT)
    lo = (a - hi.astype(F32)).astype(_MMT)
    return hi, lo


@jax.custom_vjp
def xr(a, c, ct):
    hi, lo = _split(a)
    cc = _cast(c)
    return _dg(hi, cc, 1, 0) + _dg(lo, cc, 1, 0)


@jax.custom_vjp
def xl(c, ct, a):
    hi, lo = _split(a)
    cc = _cast(c)
    return _dg(cc, hi, 1, 0) + _dg(cc, lo, 1, 0)


xr.defvjp(lambda a, c, ct: (xr(a, c, ct), (c, ct)),
          lambda r, g: (xr(g, r[1], r[0]), jnp.zeros_like(r[0]), jnp.zeros_like(r[1])))
xl.defvjp(lambda c, ct, a: (xl(c, ct, a), (c, ct)),
          lambda r, g: (jnp.zeros_like(r[0]), jnp.zeros_like(r[1]), xl(r[1], r[0], g)))


def _iota(shape, dim):
    return lax.broadcasted_iota(jnp.int32, shape, dim)


def _hmask(h, n=GW):
    lane = _iota((1, n), 1)
    return ((lane >= h * HD) & (lane < (h + 1) * HD)).astype(F32)


def _bdmask(n=GW):
    return ((_iota((n, n), 0) >> 6) == (_iota((n, n), 1) >> 6)).astype(F32)


def _tri(n, kind="le"):
    r, c = _iota((n, n), 0), _iota((n, n), 1)
    return {"le": c <= r, "ge": c >= r, "gt": c > r, "lt": c < r}[kind].astype(F32)


def _onehot_lane(h, n=128):
    return (_iota((1, n), 1) == h).astype(F32)


def _logsig(x):
    return jnp.minimum(x, 0.0) - jnp.log1p(jnp.exp(-jnp.abs(x)))


def _sigmoid(x):
    return 0.5 * (jnp.tanh(0.5 * x) + 1.0)


def _silu(x):
    return x * _sigmoid(x)


def _rms(x, g):
    return x * lax.rsqrt(jnp.mean(x * x, axis=-1, keepdims=True) + EPS) * g


def _headrms(x, w, bd64):
    ms = xr(x * x, bd64, bd64)
    return x * lax.rsqrt(ms + EPS) * w


def _call(body, name, grid, in_specs, out_specs, out_shape, scratch=()):
    return pl.pallas_call(
        body, name=name, grid=grid, in_specs=in_specs, out_specs=out_specs, out_shape=out_shape,
        scratch_shapes=list(scratch),
        compiler_params=pltpu.CompilerParams(dimension_semantics=("arbitrary",) * len(grid),
                                             vmem_limit_bytes=VMEM_LIMIT_BYTES))


def _sds(shape, dtype=F32):
    return jax.ShapeDtypeStruct(shape, dtype)


def _acc(ref, val, first):
    @pl.when(first)
    def _():
        ref[...] = val

    @pl.when(jnp.logical_not(first))
    def _():
        ref[...] += val


def inproj_fwd(x2, g, w, layer, tag):
    t = x2.shape[0]
    TQ = _token_tile(t)

    def body(x_ref, g_ref, w_ref, ht_ref, *outs):
        h = _rms(x_ref[...], g_ref[...])
        hb = _cast(h)
        ht_ref[...] = _cast(h.T)
        for (_, c0, wd), o in zip(PIECES, outs):
            o[...] = _dg(hb, _cast(w_ref[:, c0:c0 + wd]), 1, 0)

    return _call(
        body, f"inproj_fwd_{tag}", (t // TQ,),
        [pl.BlockSpec((TQ, D_MODEL), lambda i: (i, 0)),
         pl.BlockSpec((None, 1, D_MODEL), lambda i: (layer, 0, 0)),
         pl.BlockSpec((None, D_MODEL, D_INP), lambda i: (layer, 0, 0))],
        [pl.BlockSpec((D_MODEL, TQ), lambda i: (0, i))] + [pl.BlockSpec((TQ, wd), lambda i: (i, 0)) for _, _, wd in PIECES],
        [_sds((D_MODEL, t), _MMT)] + [_sds((t, wd)) for _, _, wd in PIECES],
    )(x2, g, w)


def inproj_bwd_dx(x2, g, w, dy, dpieces, layer, tag):
    t = x2.shape[0]
    TQ = _token_tile(t)

    def body(x_ref, g_ref, w_ref, dy_ref, *rest):
        dps, (dx_ref, dg_ref) = rest[:len(BWD_PIECES)], rest[len(BWD_PIECES):]
        dh = None
        for (_, c0, wd), dp in zip(BWD_PIECES, dps):
            part = _dg(_cast(dp[...]), _cast(w_ref[:, c0:c0 + wd]), 1, 1)
            dh = part if dh is None else dh + part
        _, vjp = jax.vjp(_rms, x_ref[...], g_ref[...])
        dx, dg = vjp(dh)
        dx_ref[...] = dy_ref[...] + dx
        _acc(dg_ref, dg, pl.program_id(0) == 0)

    return _call(
        body, f"inproj_bwd_dx_{tag}", (t // TQ,),
        [pl.BlockSpec((TQ, D_MODEL), lambda i: (i, 0)),
         pl.BlockSpec((None, 1, D_MODEL), lambda i: (layer, 0, 0)),
         pl.BlockSpec((None, D_MODEL, D_INP), lambda i: (layer, 0, 0), pipeline_mode=pl.Buffered(1)),
         pl.BlockSpec((TQ, D_MODEL), lambda i: (i, 0))] + [pl.BlockSpec((TQ, wd), lambda i: (i, 0)) for _, _, wd in BWD_PIECES],
        [pl.BlockSpec((TQ, D_MODEL), lambda i: (i, 0)), pl.BlockSpec((1, D_MODEL), lambda i: (0, 0))],
        [_sds((t, D_MODEL)), _sds((1, D_MODEL))],
    )(x2, g, w, dy, *dpieces)


def matmul_acc(at, bs, tag):
    m, t = at.shape
    widths = [b.shape[1] for b in bs]
    tk = 1024 if t % 1024 == 0 else (512 if t % 512 == 0 else TQ)

    def body(a_ref, *refs):
        b_refs, o_ref = refs[:-1], refs[-1]
        a = _cast(a_ref[...])
        first, off = pl.program_id(0) == 0, 0
        for b_ref, n in zip(b_refs, widths):
            part = _dg(a, _cast(b_ref[...]), 1, 0)

            @pl.when(first)
            def _(part=part, off=off, n=n):
                o_ref[:, off:off + n] = part

            @pl.when(jnp.logical_not(first))
            def _(part=part, off=off, n=n):
                o_ref[:, off:off + n] += part

            off += n

    return _call(
        body, f"matmul_acc_{tag}", (t // tk,),
        [pl.BlockSpec((m, tk), lambda i: (0, i))] + [pl.BlockSpec((tk, n), lambda i: (i, 0)) for n in widths],
        pl.BlockSpec((m, sum(widths)), lambda i: (0, 0)),
        _sds((m, sum(widths))),
    )(at, *bs)


def _fox_prep_fn(q, k, ff, qw, kw, bias, carry, bd64, tri, trit, last):
    qn = _headrms(q, qw, bd64)
    kn = _headrms(k, kw, bd64)
    lf = _logsig(ff + bias)
    c = xl(tri, trit, lf) + carry
    return qn, kn, c, jnp.sum(c * last, axis=0, keepdims=True)


def _prep_consts(tq):
    return _bdmask() * (1.0 / HD), _tri(tq), _tri(tq, "ge"), (_iota((tq, 1), 0) == tq - 1).astype(F32)


def fox_prep_fwd(pa, pf, qw, kw, bias, bl, s, layer, tag):
    TQ = _token_tile(s)
    nq = s // TQ

    def body(q_ref, k_ref, v_ref, f_ref, qw_ref, kw_ref, b_ref, qn_ref, kn_ref, vb_ref, cq_ref, ck_ref, carry):
        @pl.when(pl.program_id(1) == 0)
        def _():
            carry[...] = jnp.zeros_like(carry)

        qn, kn, c, cl = _fox_prep_fn(q_ref[...], k_ref[...], f_ref[...], qw_ref[...], kw_ref[...], b_ref[...],
                                     carry[...], *_prep_consts(TQ))
        carry[...] = cl
        qn_ref[...] = _cast(qn)
        kn_ref[...] = _cast(kn)
        vb_ref[...] = _cast(v_ref[...])
        cq_ref[...] = c
        ck_ref[...] = c.T[0:8, :]

    tok = lambda j: pl.BlockSpec((TQ, GW), lambda b, i: (b * nq + i, j))
    par = lambda n: pl.BlockSpec((None, 1, n), lambda b, i: (layer, 0, 0))
    return _call(
        body, f"fox_prep_fwd_{tag}", (bl, nq),
        [tok(0), tok(1), tok(2), pl.BlockSpec((TQ, 128), lambda b, i: (b * nq + i, 0)), par(GW), par(GW), par(128)],
        [tok(0), tok(0), tok(0), pl.BlockSpec((TQ, 128), lambda b, i: (b * nq + i, 0)),
         pl.BlockSpec((None, 8, TQ), lambda b, i: (b, 0, i))],
        [_sds((bl * s, GW), _MMT)] * 3 + [_sds((bl * s, 128)), _sds((bl, 8, s))],
        [pltpu.VMEM((1, 128), F32)],
    )(pa, pa, pa, pf, qw, kw, bias)


def fox_prep_bwd(pa, pf, qw, kw, bias, cq, dqn, dkn, dv, dck, bl, s, layer, tag):
    TQ = _token_tile(s)
    nq = s // TQ

    def body(q_ref, k_ref, f_ref, qw_ref, kw_ref, b_ref, cq_ref, cprev_ref, dqn_ref, dkn_ref, dv_ref, dck_ref,
             da_ref, df_ref, dqw_ref, dkw_ref, db_ref, dcarry):
        i = pl.program_id(1)
        first = jnp.logical_and(pl.program_id(0) == 0, i == 0)

        @pl.when(i == 0)
        def _():
            dcarry[...] = jnp.zeros_like(dcarry)

        last = (_iota((TQ, 1), 0) == TQ - 1).astype(F32)
        carry_in = jnp.where(i == nq - 1, 0.0, jnp.sum(cprev_ref[...] * last, axis=0, keepdims=True))
        consts = _prep_consts(TQ)
        _, vjp = jax.vjp(lambda *a: _fox_prep_fn(*a, *consts), q_ref[...], k_ref[...], f_ref[...], qw_ref[...],
                         kw_ref[...], b_ref[...], carry_in)
        dc = dck_ref[...].T
        dq, dk, dff, dqw, dkw, dbias, dcin = vjp((dqn_ref[...], dkn_ref[...], dc, dcarry[...]))
        dcarry[...] = dcin
        da_ref[:, 0:GW] = _cast(dq)
        da_ref[:, GW:2 * GW] = _cast(dk)
        da_ref[:, 2 * GW:3 * GW] = _cast(dv_ref[...])
        df_ref[...] = _cast(dff)
        _acc(dqw_ref, dqw, first)
        _acc(dkw_ref, dkw, first)
        _acc(db_ref, dbias, first)

    rv = lambda b, i: b * nq + (nq - 1 - i)
    tok = lambda j: pl.BlockSpec((TQ, GW), lambda b, i: (rv(b, i), j))
    tok0 = pl.BlockSpec((TQ, GW), lambda b, i: (rv(b, i), 0))
    t128 = pl.BlockSpec((TQ, 128), lambda b, i: (rv(b, i), 0))
    prev = pl.BlockSpec((TQ, 128), lambda b, i: (jnp.maximum(rv(b, i) - 1, 0), 0))
    par = lambda n: pl.BlockSpec((None, 1, n), lambda b, i: (layer, 0, 0))
    acc = lambda n: pl.BlockSpec((1, n), lambda b, i: (0, 0))
    return _call(
        body, f"fox_prep_bwd_{tag}", (bl, nq),
        [tok(0), tok(1), t128, par(GW), par(GW), par(128), t128, prev, tok0, tok0, tok0,
         pl.BlockSpec((None, 128, TQ), lambda b, i: (b, 0, nq - 1 - i))],
        [pl.BlockSpec((TQ, 3 * GW), lambda b, i: (rv(b, i), 0)), t128, acc(GW), acc(GW), acc(128)],
        [_sds((bl * s, 3 * GW), _MMT), _sds((bl * s, 128), _MMT), _sds((1, GW)), _sds((1, GW)), _sds((1, 128))],
        [pltpu.VMEM((1, 128), F32)],
    )(pa, pa, pf, qw, kw, bias, cq, cq, dqn, dkn, dv, dck)


def _lane_pick(x, h):
    return jnp.sum(x * _onehot_lane(h), axis=-1, keepdims=True)


TA_BIG = 256
TA_FWD = 512
TK_FOX = 512
TK_SB = 256


def _stack_heads(x, scale=1.0):
    return _cast(jnp.concatenate([x * (_hmask(h) * scale) for h in range(NH)], axis=0))


def _stack_cols(x):
    return jnp.concatenate([_lane_pick(x, h) for h in range(NH)], axis=0)


def _spread_heads(col):
    ta = col.shape[0] // NH
    return sum(col[h * ta:(h + 1) * ta] * _hmask(h) for h in range(NH))


def _lanes_cat(w):
    ta = w.shape[0] // NH
    return jnp.concatenate([w[h * ta:(h + 1) * ta] for h in range(NH)], axis=1)


def _mask_stack(x):
    return _cast(jnp.concatenate([x * _hmask(h).astype(x.dtype) for h in range(NH)], axis=0))


def _stack_rows(i, ta):
    return i * ta + (_iota((NH * ta, 1), 0) & (ta - 1))


def _n_key_tiles(i, tk, ta):
    assert ta <= tk and tk % ta == 0, "a query tile's diagonal must lie inside one key tile (only the last key tile is masked)"
    return lax.shift_right_logical(i * ta, tk.bit_length() - 1) + 1


def fox_attn_fwd(qn, kn, vb, cq, ck, bl, s, tag):
    TA, TK = min(TA_FWD, s), min(TK_FOX, s)
    nq, SROWS = s // TA, NH * TA

    def body(q_ref, k_ref, v_ref, cq_ref, ck_ref, o_ref, lse_ref, acc, vst):
        i = pl.program_id(1)

        @pl.when(i == 0)
        def _():
            _fill_stacked(vst, v_ref, s, TK)

        qs = _stack_heads(q_ref[...].astype(F32), SCALE)
        cqs = _stack_cols(cq_ref[...])
        row = _stack_rows(i, TA)
        acc[...] = jnp.zeros_like(acc)

        def step(j, ml):
            m, l = ml
            ks = pl.ds(pl.multiple_of(j * TK, TK), TK)
            ckb = jnp.concatenate([jnp.broadcast_to(ck_ref[h:h + 1, ks], (TA, TK)) for h in range(NH)], axis=0)
            sc = _dg(qs, k_ref[ks, :], 1, 1) + cqs - ckb
            col = j * TK + _iota((1, TK), 1)
            sc = jnp.where(col <= row, sc, NEG_BIG)
            m_new = jnp.maximum(m, jnp.max(sc, axis=-1, keepdims=True))
            alpha = jnp.exp(m - m_new)
            p = jnp.exp(sc - m_new)
            vs = vst[pl.ds(pl.multiple_of(j * NH * TK, NH * TK), NH * TK), :]
            acc[...] = _spread_heads(alpha) * acc[...] + _dg(_lanes_cat(_cast(p)), vs, 1, 0)
            return m_new, alpha * l + jnp.sum(p, axis=-1, keepdims=True)

        m, l = lax.fori_loop(0, _n_key_tiles(i, TK, TA), step, (jnp.full((SROWS, 1), NEG_BIG, F32), jnp.zeros((SROWS, 1), F32)))
        o_ref[...] = acc[...] / _spread_heads(l)
        lse_h = m + jnp.log(l)
        lse_ref[...] = sum(lse_h[h * TA:(h + 1) * TA] * _onehot_lane(h) for h in range(NH))

    tok = pl.BlockSpec((TA, GW), lambda b, i: (b * nq + i, 0))
    seq = pl.BlockSpec((s, GW), lambda b, i: (b, 0))
    t128 = pl.BlockSpec((TA, 128), lambda b, i: (b * nq + i, 0))
    return _call(
        body, f"fox_attn_fwd_{tag}", (bl, nq),
        [tok, seq, seq, t128, pl.BlockSpec((None, 8, s), lambda b, i: (b, 0, 0))],
        [tok, t128], [_sds((bl * s, GW)), _sds((bl * s, 128))],
        [pltpu.VMEM((TA, GW), F32), pltpu.VMEM((NH * s, GW), _MMT)],
    )(qn, kn, vb, cq, ck)


def fox_attn_bwd(qn, kn, vb, cq, ck, lse, do, bl, s, tag):
    TA, TK = min(TA_BIG, s), min(TK_FOX, s)
    nq, SROWS = s // TA, NH * TA

    def body(q_ref, k_ref, v_ref, cq_ref, ck_ref, lse_ref, do_ref, dq_ref, dk_ref, dv_ref, dck_ref, dqa, p_s, dp_s, kst):
        i = pl.program_id(1)

        @pl.when(i == 0)
        def _():
            dk_ref[...] = jnp.zeros_like(dk_ref)
            dv_ref[...] = jnp.zeros_like(dv_ref)
            dck_ref[...] = jnp.zeros_like(dck_ref)
            _fill_stacked(kst, k_ref, s, TK)

        qs = _stack_heads(q_ref[...].astype(F32), SCALE)
        dos = _stack_heads(do_ref[...])
        cqs, lses = _stack_cols(cq_ref[...]), _stack_cols(lse_ref[...])
        row = _stack_rows(i, TA)
        dqa[...] = jnp.zeros_like(dqa)
        nk = _n_key_tiles(i, TK, TA)

        def probs(j, delta):
            ks = pl.ds(pl.multiple_of(j * TK, TK), TK)
            ckb = jnp.concatenate([jnp.broadcast_to(ck_ref[h:h + 1, ks], (TA, TK)) for h in range(NH)], axis=0)
            sc = _dg(qs, k_ref[ks, :], 1, 1) + cqs - ckb
            col = j * TK + _iota((1, TK), 1)
            p = jnp.where(col <= row, jnp.exp(sc - lses), 0.0)
            dp = _dg(dos, v_ref[ks, :], 1, 1)
            p_s[:, ks] = p
            dp_s[:, ks] = dp
            return delta + jnp.sum(p * dp, axis=-1, keepdims=True)

        delta = lax.fori_loop(0, nk, probs, jnp.zeros((SROWS, 1), F32))

        def step(j, carry):
            ks = pl.ds(pl.multiple_of(j * TK, TK), TK)
            p = p_s[:, ks]
            ds = p * (dp_s[:, ks] - delta)
            dsb = _cast(ds)
            dqa[...] += _dg(_lanes_cat(dsb), kst[pl.ds(pl.multiple_of(j * NH * TK, NH * TK), NH * TK), :], 1, 0) * SCALE
            dk_ref[ks, :] += _dg(dsb, qs, 0, 0)
            dv_ref[ks, :] += _dg(_cast(p), dos, 0, 0)
            for h in range(NH):
                dck_ref[h:h + 1, ks] -= jnp.sum(ds[h * TA:(h + 1) * TA], axis=0, keepdims=True)
            return carry

        lax.fori_loop(0, nk, step, 0)
        dq_ref[...] = dqa[...]

    tok = pl.BlockSpec((TA, GW), lambda b, i: (b * nq + i, 0))
    seq = pl.BlockSpec((s, GW), lambda b, i: (b, 0))
    t128 = pl.BlockSpec((TA, 128), lambda b, i: (b * nq + i, 0))
    return _call(
        body, f"fox_attn_bwd_{tag}", (bl, nq),
        [tok, seq, seq, t128, pl.BlockSpec((None, 8, s), lambda b, i: (b, 0, 0)), t128, tok],
        [tok, seq, seq, pl.BlockSpec((None, 128, s), lambda b, i: (b, 0, 0))],
        [_sds((bl * s, GW)), _sds((bl * s, GW)), _sds((bl * s, GW)), _sds((bl, 128, s))],
        [pltpu.VMEM((TA, GW), F32), pltpu.VMEM((SROWS, s), F32), pltpu.VMEM((SROWS, s), F32), pltpu.VMEM((NH * s, GW), _MMT)],
    )(qn, kn, vb, cq, ck, lse, do)


def _sb_block(qh, kb, valid, upper, r_carry):
    z = _dg(qh, kb, 1, 1)
    ls = _logsig(z)
    lom = ls - z if valid is None else jnp.where(valid, ls - z, 0.0)
    between = xr(lom, upper, upper) + r_carry
    w = jnp.exp(ls + between)
    return ls, lom, (w if valid is None else jnp.where(valid, w, 0.0))


def _fill_stacked(dst, src_ref, s, tk):
    for j in range(s // tk):
        dst[j * NH * tk:(j + 1) * NH * tk, :] = _mask_stack(src_ref[j * tk:(j + 1) * tk, :])


def sb_attn_fwd(pb, bl, s, tag):
    TA, TK = TA_BIG, TK_SB
    nq, SROWS = s // TA, NH * TA

    def body(q_ref, k_ref, v_ref, o_ref, acc, vst):
        i = pl.program_id(1)

        @pl.when(i == 0)
        def _():
            _fill_stacked(vst, v_ref, s, TK)

        qs = _stack_heads(q_ref[...], SCALE)
        upper = _tri(TK, "lt")
        last = _n_key_tiles(i, TK, TA) - 1

        def step(j, r, valid):
            ks = pl.ds(pl.multiple_of(j * TK, TK), TK)
            _, lom, w = _sb_block(qs, _cast(k_ref[ks, :]), valid, upper, r)
            acc[...] += _dg(_lanes_cat(_cast(w)), vst[pl.ds(pl.multiple_of(j * NH * TK, NH * TK), NH * TK), :], 1, 0)
            return r + jnp.sum(lom, axis=-1, keepdims=True)

        acc[...] = jnp.zeros_like(acc)
        r = step(last, jnp.zeros((SROWS, 1), F32), last * TK + _iota((1, TK), 1) < _stack_rows(i, TA))
        lax.fori_loop(0, last, lambda jj, r: step(last - 1 - jj, r, None), r)
        o_ref[...] = acc[...]

    tok = lambda j: pl.BlockSpec((TA, GW), lambda b, i: (b * nq + i, j))
    seq = lambda j: pl.BlockSpec((s, GW), lambda b, i: (b, j))
    return _call(
        body, f"sb_attn_fwd_{tag}", (bl, nq), [tok(0), seq(1), seq(2)],
        pl.BlockSpec((TA, GW), lambda b, i: (b * nq + i, 0)), _sds((bl * s, GW)),
        [pltpu.VMEM((TA, GW), F32), pltpu.VMEM((NH * s, GW), _MMT)],
    )(pb, pb, pb)


def sb_attn_bwd(pb, do, bl, s, tag):
    TA, TK = TA_BIG, TK_SB
    nq, SROWS = s // TA, NH * TA

    def body(q_ref, k_ref, v_ref, do_ref, dq_ref, dk_ref, dv_ref, dqa, sig_s, nsig_s, w_s, g_s, kst):
        i = pl.program_id(1)

        @pl.when(i == 0)
        def _():
            dk_ref[...] = jnp.zeros_like(dk_ref)
            dv_ref[...] = jnp.zeros_like(dv_ref)
            _fill_stacked(kst, k_ref, s, TK)

        qs = _stack_heads(q_ref[...], SCALE)
        dos = _stack_heads(do_ref[...])
        upper = _tri(TK, "lt")
        before = _tri(TK, "gt")
        dqa[...] = jnp.zeros_like(dqa)
        last = _n_key_tiles(i, TK, TA) - 1
        diag = last * TK + _iota((1, TK), 1) < _stack_rows(i, TA)

        def weights(j, r, valid):
            ks = pl.ds(pl.multiple_of(j * TK, TK), TK)
            ls, lom, w = _sb_block(qs, _cast(k_ref[ks, :]), valid, upper, r)
            sig_s[:, ks] = _cast(jnp.exp(ls))
            nsig_s[:, ks] = _cast(jnp.exp(lom))
            w_s[:, ks] = _cast(w)
            g_s[:, ks] = _dg(dos, _cast(v_ref[ks, :]), 1, 1) * w
            return r + jnp.sum(lom, axis=-1, keepdims=True)

        r = weights(last, jnp.zeros((SROWS, 1), F32), diag)
        lax.fori_loop(0, last, lambda jj, r: weights(last - 1 - jj, r, None), r)

        def step(j, cpre, valid):
            ks = pl.ds(pl.multiple_of(j * TK, TK), TK)
            g = g_s[:, ks]
            pre = cpre + xr(g, before, before)
            dz = g * nsig_s[:, ks].astype(F32) - sig_s[:, ks].astype(F32) * pre
            dzb = _cast(dz if valid is None else jnp.where(valid, dz, 0.0))
            dqa[...] += _dg(_lanes_cat(dzb), kst[pl.ds(pl.multiple_of(j * NH * TK, NH * TK), NH * TK), :], 1, 0) * SCALE
            dk_ref[ks, :] += _dg(dzb, qs, 0, 0)
            dv_ref[ks, :] += _dg(w_s[:, ks], dos, 0, 0)
            return cpre + jnp.sum(g, axis=-1, keepdims=True)

        cpre = lax.fori_loop(0, last, lambda j, c: step(j, c, None), jnp.zeros((SROWS, 1), F32))
        step(last, cpre, diag)
        dq_ref[...] = dqa[...]

    tok = lambda j: pl.BlockSpec((TA, GW), lambda b, i: (b * nq + i, j))
    seq = lambda j: pl.BlockSpec((s, GW), lambda b, i: (b, j))
    return _call(
        body, f"sb_attn_bwd_{tag}", (bl, nq), [tok(0), seq(1), seq(2), tok(0)],
        [tok(0), seq(0), seq(0)], [_sds((bl * s, GW))] * 3,
        [pltpu.VMEM((TA, GW), F32), pltpu.VMEM((SROWS, s), _MMT), pltpu.VMEM((SROWS, s), _MMT),
         pltpu.VMEM((SROWS, s), _MMT), pltpu.VMEM((SROWS, s), F32), pltpu.VMEM((NH * s, GW), _MMT)],
    )(pb, pb, pb, do)


def _hgrn_consts():
    r, c = _iota((CH, CH), 0), _iota((CH, CH), 1)
    rr = _iota((CH, 1), 0)
    tri = (c <= r).astype(F32)
    lv = []
    for m in (8, 4, 2, 1):
        up = ((rr & (2 * m - 1)) >= m).astype(F32)
        selq = (((r & (2 * m - 1)) >= m) & (c == (r & ~(m - 1)) - 1)).astype(F32)
        selk = (((r & (2 * m - 1)) < m) & (c == (r & ~(m - 1)) + m - 1)).astype(F32)
        pm = (((r & ~(2 * m - 1)) == (c & ~(2 * m - 1))) & ((r & (2 * m - 1)) >= m) & ((c & (2 * m - 1)) < m)).astype(F32)
        lv.append((up, 1.0 - up, selq, selq.T, selk, selk.T, jnp.concatenate([pm] * NH, axis=0)))
    hm4 = lambda n: (((_iota((NH, 1, n), 2) & (GW - 1)) >> 6) == _iota((NH, 1, n), 0)).astype(F32)
    return dict(tri=tri, trit=tri.T, rr=rr, lv=lv, bd=_bdmask(), bd64=_bdmask() * (1.0 / HD),
                hm4={GW: hm4(GW), 3 * GW: hm4(3 * GW)})


def _hgrn_chunk_fn(hq, hf, hi, lb, wn, st, cs):
    q = _silu(hq)
    log_lb = jnp.log(jnp.maximum(lb, LB_FLOOR))
    a, bb = log_lb, jnp.log1p(-lb) + _logsig(hf)
    g = jnp.maximum(a, bb) + jnp.log1p(jnp.exp(-jnp.abs(a - bb)))
    k = (1.0 - lb) * _sigmoid(-hf)
    v = hi
    rr = cs["rr"]
    b = xl(cs["tri"], cs["trit"], g)
    row_of = lambda n: jnp.sum(b * (rr == n).astype(F32), axis=0, keepdims=True)
    o = mm_nt(q * jnp.exp(b), st)
    qs, ks = [], []
    for ib in (1, 2, 3):
        ref = row_of(16 * ib - 1)
        inq = ((rr >= 16 * ib) & (rr < 16 * ib + 16)).astype(F32)
        ink = (rr < 16 * ib).astype(F32)
        qs.append(q * jnp.exp((b - ref) * inq) * inq)
        ks.append(k * jnp.exp((ref - b) * ink) * ink)
    qcat, kcat = jnp.concatenate(qs, axis=1), jnp.concatenate(ks, axis=1)
    lvl = []
    for up, lo, selq, selqt, selk, selkt, pm in cs["lv"]:
        qe = q * jnp.exp((b - xl(selq, selqt, b)) * up) * up
        ke = k * jnp.exp((xl(selk, selkt, b) - b) * lo) * lo
        lvl.append((qe, ke, pm))
    stack = lambda x: (x[None] * cs["hm4"][x.shape[1]]).reshape(NH * CH, x.shape[1])
    a_all = mm_nt(stack(qcat), kcat)
    for qe, ke, pm4 in lvl:
        a_all = a_all + mm_nt(stack(qe), ke) * pm4
    o = o + jnp.sum(mm(a_all, v).reshape(NH, CH, GW) * cs["hm4"][GW], axis=0)
    o = o + xr(q * k, cs["bd"], cs["bd"]) * v
    b_last = row_of(CH - 1)
    st_new = st * jnp.exp(b_last) + mm_tn(v, k * jnp.exp(b_last - b)) * cs["bd"]
    return _headrms(o, wn, cs["bd64"]), st_new


def hgrn_fwd(pc, lb, wn, bl, s, layer, tag):
    nc = s // CH

    def body(q_ref, f_ref, i_ref, lb_ref, wn_ref, o_ref, st_ref, st):
        @pl.when(pl.program_id(0) == 0)
        def _():
            st[...] = jnp.zeros_like(st)

        cs = _hgrn_consts()
        for b in range(bl):
            st_ref[b] = st[b]
            o, st_new = _hgrn_chunk_fn(q_ref[b], f_ref[b], i_ref[b], lb_ref[...], wn_ref[...], st[b], cs)
            o_ref[b] = o
            st[b] = st_new

    tok = lambda j: pl.BlockSpec((bl, CH, GW), lambda c: (0, c, j))
    par = pl.BlockSpec((None, 1, GW), lambda c: (layer, 0, 0))
    pc3 = pc.reshape(bl, s, 3 * GW)
    o, states = _call(
        body, f"hgrn_fwd_{tag}", (nc,), [tok(0), tok(1), tok(2), par, par],
        [tok(0), pl.BlockSpec((bl, None, GW, GW), lambda c: (0, c, 0, 0))],
        [_sds((bl, s, GW)), _sds((bl, nc, GW, GW))],
        [pltpu.VMEM((bl, GW, GW), F32)],
    )(pc3, pc3, pc3, lb, wn)
    return o.reshape(bl * s, GW), states


def hgrn_bwd(pc, lb, wn, states, do, bl, s, layer, tag):
    nc = s // CH

    def body(q_ref, f_ref, i_ref, lb_ref, wn_ref, st_ref, do_ref, dc_ref, dlb_ref, dwn_ref, dst):
        c = pl.program_id(0)

        @pl.when(c == 0)
        def _():
            dst[...] = jnp.zeros_like(dst)

        cs = _hgrn_consts()
        dlb_sum = dwn_sum = None
        for b in range(bl):
            _, vjp = jax.vjp(lambda *a: _hgrn_chunk_fn(*a, cs), q_ref[b], f_ref[b], i_ref[b], lb_ref[...],
                             wn_ref[...], st_ref[b])
            dq, df, di, dlb, dwn, dst_in = vjp((do_ref[b], dst[b]))
            dst[b] = dst_in
            dc_ref[b, :, 0:GW] = _cast(dq)
            dc_ref[b, :, GW:2 * GW] = _cast(df)
            dc_ref[b, :, 2 * GW:3 * GW] = _cast(di)
            dlb_sum = dlb if dlb_sum is None else dlb_sum + dlb
            dwn_sum = dwn if dwn_sum is None else dwn_sum + dwn
        _acc(dlb_ref, dlb_sum, c == 0)
        _acc(dwn_ref, dwn_sum, c == 0)

    tok = lambda j: pl.BlockSpec((bl, CH, GW), lambda c: (0, nc - 1 - c, j))
    par = pl.BlockSpec((None, 1, GW), lambda c: (layer, 0, 0))
    acc = pl.BlockSpec((1, GW), lambda c: (0, 0))
    pc3 = pc.reshape(bl, s, 3 * GW)
    dc, dlb, dwn = _call(
        body, f"hgrn_bwd_{tag}", (nc,),
        [tok(0), tok(1), tok(2), par, par, pl.BlockSpec((bl, None, GW, GW), lambda c: (0, nc - 1 - c, 0, 0)), tok(0)],
        [pl.BlockSpec((bl, CH, 3 * GW), lambda c: (0, nc - 1 - c, 0)), acc, acc],
        [_sds((bl, s, 3 * GW), _MMT), _sds((1, GW)), _sds((1, GW))],
        [pltpu.VMEM((bl, GW, GW), F32)],
    )(pc3, pc3, pc3, lb, wn, states, do.reshape(bl, s, GW))
    return dc.reshape(bl * s, 3 * GW), dlb, dwn


def _shift_rows(x, k, up):
    n = x.shape[0]
    rr = _iota((n, 1), 0)
    if up:
        return jnp.where(rr < n - k, pltpu.roll(x, n - k, 0), 0.0)
    return jnp.where(rr >= k, pltpu.roll(x, k, 0), 0.0)


def _window_sums(x, up):
    s2 = x + _shift_rows(x, 1, up)
    s4 = s2 + _shift_rows(s2, 2, up)
    s8 = s4 + _shift_rows(s4, 4, up)
    s16 = s8 + _shift_rows(s8, 8, up)
    return s2, s4, s8, s16


def _pool_div(n):
    pos = (_iota((n, 1), 0) + 1).astype(F32)
    return [jnp.minimum(pos, float(w)) for w in (2, 4, 8, 16)]


def _pool_mix(sums, scaled):
    out = None
    for gi, sw in enumerate(sums):
        part = (sw if scaled is None else sw / scaled[gi]) * _hmask(gi)
        out = part if out is None else out + part
    return out


def pool_fwd(pd, wbd, scale, bl, s, layer, tag):
    def body(u_ref, w_ref, sc_ref, o_ref):
        u = u_ref[...]
        pm = _pool_mix(_window_sums(u, False), _pool_div(s)) - u
        o_ref[...] = _dg(_cast(pm), _cast(w_ref[...]), 1, 0) * sc_ref[...]

    seq = pl.BlockSpec((s, GW), lambda b: (b, 0))
    return _call(
        body, f"pool_fwd_{tag}", (bl,),
        [seq, pl.BlockSpec((None, GW, GW), lambda b: (layer, 0, 0)), pl.BlockSpec((None, 1, GW), lambda b: (layer, 0, 0))],
        seq, _sds((bl * s, GW)),
    )(pd, wbd, scale)


def pool_bwd(pd, wbd, scale, do, bl, s, layer, tag):
    def body(u_ref, w_ref, sc_ref, do_ref, du_ref, dw_ref, dsc_ref):
        first = pl.program_id(0) == 0
        u, do = u_ref[...], do_ref[...]
        div = _pool_div(s)
        pm = _pool_mix(_window_sums(u, False), div) - u
        ypre = _dg(_cast(pm), _cast(w_ref[...]), 1, 0)
        dys = do * sc_ref[...]
        _acc(dsc_ref, jnp.sum(do * ypre, axis=0, keepdims=True), first)
        _acc(dw_ref, _dg(_cast(pm), _cast(dys), 0, 0), first)
        dpm = _dg(_cast(dys), _cast(w_ref[...]), 1, 1)
        dsc = [dpm / d for d in div]
        adj = None
        for gi in range(4):
            part = _window_sums(dsc[gi] * _hmask(gi), True)[gi]
            adj = part if adj is None else adj + part
        du_ref[...] = _cast(adj - dpm)

    seq = pl.BlockSpec((s, GW), lambda b: (b, 0))
    return _call(
        body, f"pool_bwd_{tag}", (bl,),
        [seq, pl.BlockSpec((None, GW, GW), lambda b: (layer, 0, 0)), pl.BlockSpec((None, 1, GW), lambda b: (layer, 0, 0)), seq],
        [seq, pl.BlockSpec((GW, GW), lambda b: (0, 0)), pl.BlockSpec((1, GW), lambda b: (0, 0))],
        [_sds((bl * s, GW), _MMT), _sds((GW, GW)), _sds((1, GW))],
    )(pd, wbd, scale, do)


def _mem_prep_fn(mem, g, wk, wv, kw, bd64):
    mn = _rms(mem, g)
    return _headrms(mm(mn, wk), kw, bd64), mm(mn, wv)


def mem_prep_fwd(mem2, g, wkv, kw, bl, layer, tag):
    def body(m_ref, g_ref, wk_ref, wv_ref, kw_ref, k_ref, v_ref):
        k, v = _mem_prep_fn(m_ref[...], g_ref[...], wk_ref[...], wv_ref[...], kw_ref[...], _bdmask() * (1.0 / HD))
        k_ref[...] = k
        v_ref[...] = v

    blk = pl.BlockSpec((N_MEM, GW), lambda b: (b, 0))
    return _call(
        body, f"mem_prep_fwd_{tag}", (bl,),
        [pl.BlockSpec((N_MEM, D_MODEL), lambda b: (b, 0)), pl.BlockSpec((None, 1, D_MODEL), lambda b: (layer, 0, 0)),
         pl.BlockSpec((None, D_MODEL, GW), lambda b: (layer, 0, 0)), pl.BlockSpec((None, D_MODEL, GW), lambda b: (layer, 0, 1)),
         pl.BlockSpec((None, 1, GW), lambda b: (layer, 0, 0))],
        [blk, blk], [_sds((bl * N_MEM, GW))] * 2,
    )(mem2, g, wkv, wkv, kw)


def mem_prep_bwd(mem2, g, wkv, kw, dk, dv, bl, layer, tag):
    def body(m_ref, g_ref, wk_ref, wv_ref, kw_ref, dk_ref, dv_ref, dwk_ref, dwv_ref, dg_ref, dkw_ref):
        first = pl.program_id(0) == 0
        bd64 = _bdmask() * (1.0 / HD)
        _, vjp = jax.vjp(lambda g_, wk, wv, kw_: _mem_prep_fn(m_ref[...], g_, wk, wv, kw_, bd64),
                         g_ref[...], wk_ref[...].astype(F32), wv_ref[...].astype(F32), kw_ref[...])
        dg, dwk, dwv, dkw = vjp((dk_ref[...], dv_ref[...]))
        _acc(dwk_ref, dwk, first)
        _acc(dwv_ref, dwv, first)
        _acc(dg_ref, dg, first)
        _acc(dkw_ref, dkw, first)

    blk = pl.BlockSpec((N_MEM, GW), lambda b: (b, 0))
    return _call(
        body, f"mem_prep_bwd_{tag}", (bl,),
        [pl.BlockSpec((N_MEM, D_MODEL), lambda b: (b, 0)), pl.BlockSpec((None, 1, D_MODEL), lambda b: (layer, 0, 0)),
         pl.BlockSpec((None, D_MODEL, GW), lambda b: (layer, 0, 0)), pl.BlockSpec((None, D_MODEL, GW), lambda b: (layer, 0, 1)),
         pl.BlockSpec((None, 1, GW), lambda b: (layer, 0, 0)), blk, blk],
        [pl.BlockSpec((D_MODEL, GW), lambda b: (0, 0)), pl.BlockSpec((D_MODEL, GW), lambda b: (0, 0)),
         pl.BlockSpec((1, D_MODEL), lambda b: (0, 0)), pl.BlockSpec((1, GW), lambda b: (0, 0))],
        [_sds((D_MODEL, GW)), _sds((D_MODEL, GW)), _sds((1, D_MODEL)), _sds((1, GW))],
    )(mem2, g, wkv, wkv, kw, dk, dv)


def _mem_attn_fn(mq, qw, k, v, bd64):
    qn = _headrms(mq, qw, bd64)
    out = None
    for h in range(NH):
        hm = _hmask(h)
        lg = mm_nt(qn * hm, k) * SCALE
        e = jnp.exp(lg - lax.stop_gradient(jnp.max(lg, axis=-1, keepdims=True)))
        p = e / jnp.sum(e, axis=-1, keepdims=True)
        part = mm(p, v) * hm
        out = part if out is None else out + part
    return out


def mem_attn_fwd(pe, qw, k, v, bl, s, layer, tag):
    TQ = _token_tile(s)
    nq = s // TQ

    def body(q_ref, qw_ref, k_ref, v_ref, o_ref):
        o_ref[...] = _mem_attn_fn(q_ref[...], qw_ref[...], k_ref[...], v_ref[...], _bdmask() * (1.0 / HD))

    tok = pl.BlockSpec((TQ, GW), lambda b, i: (b * nq + i, 0))
    kv = pl.BlockSpec((N_MEM, GW), lambda b, i: (b, 0))
    return _call(
        body, f"mem_attn_fwd_{tag}", (bl, nq), [tok, pl.BlockSpec((None, 1, GW), lambda b, i: (layer, 0, 0)), kv, kv],
        tok, _sds((bl * s, GW)),
    )(pe, qw, k, v)


def mem_attn_bwd(pe, qw, k, v, do, bl, s, layer, tag):
    TQ = _token_tile(s)
    nq = s // TQ

    def body(q_ref, qw_ref, k_ref, v_ref, do_ref, dq_ref, dk_ref, dv_ref, dqw_ref):
        i = pl.program_id(1)
        bd64 = _bdmask() * (1.0 / HD)
        _, vjp = jax.vjp(lambda *a: _mem_attn_fn(*a, bd64), q_ref[...], qw_ref[...], k_ref[...], v_ref[...])
        dq, dqw, dk, dv = vjp(do_ref[...])
        dq_ref[...] = _cast(dq)
        _acc(dk_ref, dk, i == 0)
        _acc(dv_ref, dv, i == 0)
        _acc(dqw_ref, dqw, jnp.logical_and(pl.program_id(0) == 0, i == 0))

    tok = pl.BlockSpec((TQ, GW), lambda b, i: (b * nq + i, 0))
    kv = pl.BlockSpec((N_MEM, GW), lambda b, i: (b, 0))
    return _call(
        body, f"mem_attn_bwd_{tag}", (bl, nq),
        [tok, pl.BlockSpec((None, 1, GW), lambda b, i: (layer, 0, 0)), kv, kv, tok],
        [tok, kv, kv, pl.BlockSpec((1, GW), lambda b, i: (0, 0))],
        [_sds((bl * s, GW), _MMT), _sds((bl * N_MEM, GW)), _sds((bl * N_MEM, GW)), _sds((1, GW))],
    )(pe, qw, k, v, do)


def _gate_out_fn(outs, gates, wparts):
    y = None
    for o, g, w in zip(outs, gates, wparts):
        part = mm(o * _silu(g), w)
        y = part if y is None else y + part
    return y


def outproj_fwd(x2, outs, pg, wout, layer, tag):
    t = x2.shape[0]
    TQ = _token_tile(t)

    def body(x_ref, oa, ob, oc, od, oe, g_ref, w_ref, y_ref):
        outs_ = [r[...] for r in (oa, ob, oc, od, oe)]
        gates = [g_ref[:, j * GW:(j + 1) * GW] for j in range(5)]
        wparts = [w_ref[j * GW:(j + 1) * GW, :] for j in range(5)]
        y_ref[...] = x_ref[...] + _gate_out_fn(outs_, gates, wparts)

    tok = pl.BlockSpec((TQ, GW), lambda i: (i, 0))
    big = pl.BlockSpec((TQ, D_MODEL), lambda i: (i, 0))
    return _call(
        body, f"outproj_fwd_{tag}", (t // TQ,),
        [big] + [tok] * 5 + [pl.BlockSpec((TQ, D_MIX), lambda i: (i, 0)),
                            pl.BlockSpec((None, D_MIX, D_MODEL), lambda i: (layer, 0, 0))],
        big, _sds((t, D_MODEL)),
    )(x2, *outs, pg, wout)


def outproj_bwd(outs, pg, wout, dy, layer, tag):
    t = dy.shape[0]
    TQ = _token_tile(t)

    def body(oa, ob, oc, od, oe, g_ref, w_ref, dy_ref, da, db, dc, dd, de, dg_ref, dw_ref):
        outs_ = [r[...] for r in (oa, ob, oc, od, oe)]
        gates = [g_ref[:, j * GW:(j + 1) * GW] for j in range(5)]
        wparts = [w_ref[j * GW:(j + 1) * GW, :].astype(F32) for j in range(5)]
        _, vjp = jax.vjp(_gate_out_fn, outs_, gates, wparts)
        douts, dgates, dws = vjp(dy_ref[...])
        for r, val in zip((da, db, dc, dd, de), douts):
            r[...] = val
        first = pl.program_id(0) == 0
        for j in range(5):
            dg_ref[:, j * GW:(j + 1) * GW] = _cast(dgates[j])

        @pl.when(first)
        def _():
            for j in range(5):
                dw_ref[j * GW:(j + 1) * GW, :] = dws[j]

        @pl.when(jnp.logical_not(first))
        def _():
            for j in range(5):
                dw_ref[j * GW:(j + 1) * GW, :] += dws[j]

    tok = pl.BlockSpec((TQ, GW), lambda i: (i, 0))
    return _call(
        body, f"outproj_bwd_{tag}", (t // TQ,),
        [tok] * 5 + [pl.BlockSpec((TQ, D_MIX), lambda i: (i, 0)), pl.BlockSpec((None, D_MIX, D_MODEL), lambda i: (layer, 0, 0)),
                     pl.BlockSpec((TQ, D_MODEL), lambda i: (i, 0))],
        [tok] * 5 + [pl.BlockSpec((TQ, D_MIX), lambda i: (i, 0)), pl.BlockSpec((D_MIX, D_MODEL), lambda i: (0, 0))],
        [_sds((t, GW))] * 5 + [_sds((t, D_MIX), _MMT), _sds((D_MIX, D_MODEL))],
    )(*outs, pg, wout, dy)


def loss_head(y, tgt):
    t = y.shape[0]
    TQ = _token_tile(t)

    def body(y_ref, t_ref, l_ref, dy_ref):
        diff = y_ref[...] - t_ref[...]
        dy_ref[...] = diff * (1.0 / D_MODEL)
        part = 0.5 * jnp.sum(jnp.sum(diff * diff, axis=-1, keepdims=True) * (1.0 / D_MODEL), axis=0, keepdims=True)
        _acc(l_ref, jnp.broadcast_to(part, (8, 128)), pl.program_id(0) == 0)

    big = pl.BlockSpec((TQ, D_MODEL), lambda i: (i, 0))
    return _call(body, "loss_head", (t // TQ,), [big, big], [pl.BlockSpec((8, 128), lambda i: (0, 0)), big],
                 [_sds((8, 128)), _sds((t, D_MODEL))])(y, tgt)


def layer_fwd(x2, mem2, p, layer, bl, s):
    tag = f"l{layer}"
    ht, pa, pb, pc, pd, pe, pg, pf = inproj_fwd(x2, p["norm_g"], p["w_in"], layer, tag)
    qn, kn, vb, cq, ck = fox_prep_fwd(pa, pf, p["fox_q_norm"], p["fox_k_norm"], p["fox_f_bias"], bl, s, layer, tag)
    oa, lse = fox_attn_fwd(qn, kn, vb, cq, ck, bl, s, tag)
    ob = sb_attn_fwd(pb, bl, s, tag)
    oc, states = hgrn_fwd(pc, p["lb"], p["hgrn_out_norm"], bl, s, layer, tag)
    od = pool_fwd(pd, p["pool_wbd"], p["pool_scale"], bl, s, layer, tag)
    mk, mv = mem_prep_fwd(mem2, p["mem_norm_g"], p["mem_w_kv"], p["mem_k_norm"], bl, layer, tag)
    oe = mem_attn_fwd(pe, p["mem_q_norm"], mk, mv, bl, s, layer, tag)
    y = outproj_fwd(x2, (oa, ob, oc, od, oe), pg, p["w_out"], layer, tag)
    saved = dict(x2=x2, ht=ht, pa=pa, pb=pb, pc=pc, pd=pd, pe=pe, pg=pg, pf=pf, qn=qn, kn=kn, vb=vb, cq=cq, ck=ck,
                 oa=oa, lse=lse, ob=ob, oc=oc, states=states, od=od, mk=mk, mv=mv, oe=oe)
    return y, saved


def layer_bwd(dy, mem2, p, sv, layer, bl, s):
    tag = f"l{layer}"
    (doa, dob, doc, dod, doe, dg_gates, dwout) = outproj_bwd((sv["oa"], sv["ob"], sv["oc"], sv["od"], sv["oe"]), sv["pg"],
                                                              p["w_out"], dy, layer, tag)
    dqn, dkn, dv, dck = fox_attn_bwd(sv["qn"], sv["kn"], sv["vb"], sv["cq"], sv["ck"], sv["lse"], doa, bl, s, tag)
    d_a, d_f, dqw, dkw, dbias = fox_prep_bwd(sv["pa"], sv["pf"], p["fox_q_norm"], p["fox_k_norm"], p["fox_f_bias"], sv["cq"],
                                             dqn, dkn, dv, dck, bl, s, layer, tag)
    dsq, dsk, dsv = sb_attn_bwd(sv["pb"], dob, bl, s, tag)
    d_c, dlb, dwn = hgrn_bwd(sv["pc"], p["lb"], p["hgrn_out_norm"], sv["states"], doc, bl, s, layer, tag)
    d_d, dwbd, dpscale = pool_bwd(sv["pd"], p["pool_wbd"], p["pool_scale"], dod, bl, s, layer, tag)
    d_e, dmk, dmv, dmqw = mem_attn_bwd(sv["pe"], p["mem_q_norm"], sv["mk"], sv["mv"], doe, bl, s, layer, tag)
    dwk, dwv, dmg, dmkw = mem_prep_bwd(mem2, p["mem_norm_g"], p["mem_w_kv"], p["mem_k_norm"], dmk, dmv, bl, layer, tag)
    dpieces = (d_a, dsq, dsk, dsv, d_c, d_d, d_e, dg_gates, d_f)
    dx, dng = inproj_bwd_dx(sv["x2"], p["norm_g"], p["w_in"], dy, dpieces, layer, tag)
    groups = ((0, 4), (4, 7), (7, 9))
    dwin = jnp.concatenate([matmul_acc(sv["ht"], dpieces[a:b], f"{tag}_{a}") for a, b in groups], axis=1)
    grads = dict(norm_g=dng, w_in=dwin, fox_f_bias=dbias, fox_q_norm=dqw, fox_k_norm=dkw, lb=dlb, hgrn_out_norm=dwn,
                 pool_wbd=dwbd, pool_scale=dpscale, mem_norm_g=dmg, mem_w_kv=jnp.concatenate([dwk, dwv], axis=1),
                 mem_q_norm=dmqw, mem_k_norm=dmkw, w_out=dwout)
    return dx, grads


def _tile4(w):
    return jnp.tile(w, (1, NH))[:, None, :]


def prepare_params(norm_g, w_in_p, fox_f_bias, fox_q_norm, fox_k_norm, hgrn_lb_logits, hgrn_out_norm, pool_w, pool_scale,
                   mem_norm_g, mem_w_kv, mem_q_norm, mem_k_norm, w_out):
    p1 = jax.nn.sigmoid(hgrn_lb_logits[1] - hgrn_lb_logits[0])
    lb = jnp.stack([jnp.zeros_like(p1), jnp.clip(p1, 0.0, 1.0 - 1e-6)])
    eye = jnp.eye(4, dtype=F32)
    wbd = jnp.einsum("lgcd,gh->lgchd", pool_w, eye).reshape(2, GW, GW)
    return dict(norm_g=norm_g[:, None, :], w_in=w_in_p, fox_f_bias=jnp.pad(fox_f_bias, ((0, 0), (0, 124)))[:, None, :],
                fox_q_norm=_tile4(fox_q_norm), fox_k_norm=_tile4(fox_k_norm), lb=lb[:, None, :],
                hgrn_out_norm=hgrn_out_norm[:, None, :], pool_wbd=wbd, pool_scale=pool_scale[:, None, :],
                mem_norm_g=mem_norm_g[:, None, :], mem_w_kv=mem_w_kv, mem_q_norm=_tile4(mem_q_norm),
                mem_k_norm=_tile4(mem_k_norm), w_out=w_out)


def local_step(x, mem, tgt, p):
    bl, s, _ = x.shape
    x2, mem2, tgt2 = x.reshape(bl * s, D_MODEL), mem.reshape(bl * N_MEM, D_MODEL), tgt.reshape(bl * s, D_MODEL)
    y0, sv0 = layer_fwd(x2, mem2, p, 0, bl, s)
    y1, sv1 = layer_fwd(y0, mem2, p, 1, bl, s)
    lpart, dy = loss_head(y1, tgt2)
    dx1, g1 = layer_bwd(dy, mem2, p, sv1, 1, bl, s)
    dx0, g0 = layer_bwd(dx1, mem2, p, sv0, 0, bl, s)
    return lpart[0, 0], dx0.reshape(bl, s, D_MODEL), g0, g1


_ANY = pl.BlockSpec(memory_space=pl.ANY)


def all_gather_rows(xss, tag):
    n = len(xss)

    def body(*refs):
        x_refs, o_refs, (send_sems, recv_sems, local_sems) = refs[:n], refs[n:2 * n], refs[2 * n:]
        x, y, cc = lax.axis_index("x"), lax.axis_index("y"), lax.axis_index("c")
        me, sibling = (x, y, cc), (x, y, 1 - cc)
        chips = [(1 - x, y), (x, 1 - y), (1 - x, 1 - y)]

        def rows(a, px, py, pc):
            r = xss[a].shape[1]
            return o_refs[a].at[:, pl.ds((4 * px + 2 * py + pc) * r, r), :]

        def copy(a, k, block, to, src=None):
            return pltpu.make_async_remote_copy(src_ref=rows(a, *block) if src is None else src, dst_ref=rows(a, *block),
                                                send_sem=send_sems.at[7 * a + k], recv_sem=recv_sems.at[7 * a + k], device_id=to,
                                                device_id_type=pl.DeviceIdType.MESH)

        mine = [pltpu.make_async_copy(x_refs[a], rows(a, *me), local_sems.at[a]) for a in range(n)]
        first = []
        for a in range(n):
            first += [copy(a, 0, me, sibling, src=x_refs[a])] + [copy(a, 1 + j, me, (*chip, cc), src=x_refs[a])
                                                                 for j, chip in enumerate(chips)]
        for cp in mine + first:
            cp.start()
        passed = []
        for j, chip in enumerate(chips):
            for a in range(n):
                copy(a, 1 + j, (*chip, cc), me).wait_recv()
                passed.append(copy(a, 4 + j, (*chip, cc), sibling))
                passed[-1].start()
        for a in range(n):
            copy(a, 0, sibling, me).wait_recv()
        for j, chip in enumerate(chips):
            for a in range(n):
                copy(a, 4 + j, (*chip, 1 - cc), me).wait_recv()
        for cp in first + passed:
            cp.wait_send()
        for cp in mine:
            cp.wait()

    nsem = pltpu.SemaphoreType.DMA((7 * n,))
    return pl.pallas_call(
        body, name=f"all_gather_{tag}", in_specs=[_ANY] * n, out_specs=[_ANY] * n,
        out_shape=[_sds((xs.shape[0], N_DEV * xs.shape[1], xs.shape[2]), xs.dtype) for xs in xss],
        scratch_shapes=[nsem, nsem, pltpu.SemaphoreType.DMA((n,))],
    )(*xss)


def exchange_cores(parts, tag):
    n = len(parts)
    counts = [p.shape[0] * 4 for p in parts]

    def body(*refs):
        p_refs, t_refs, (send_sems, recv_sems) = refs[:n], refs[n:2 * n], refs[2 * n:]
        x, y, cc = lax.axis_index("x"), lax.axis_index("y"), lax.axis_index("c")
        copies, k = [], 0
        for a in range(n):
            for l in range(parts[a].shape[0]):
                for q in range(4):
                    copies.append(pltpu.make_async_remote_copy(
                        src_ref=p_refs[a].at[l, q, pl.ds(1 - cc, 1)], dst_ref=t_refs[a].at[l, q], send_sem=send_sems.at[k],
                        recv_sem=recv_sems.at[k], device_id=(x, y, 1 - cc), device_id_type=pl.DeviceIdType.MESH))
                    k += 1
        for cp in copies:
            cp.start()
        for cp in copies:
            cp.wait()

    nsem = pltpu.SemaphoreType.DMA((sum(counts),))
    return pl.pallas_call(
        body, name=f"exchange_cores_{tag}", in_specs=[_ANY] * n, out_specs=[_ANY] * n,
        out_shape=[_sds((p.shape[0], 4, 1, p.shape[3], p.shape[4]), p.dtype) for p in parts],
        scratch_shapes=[nsem, nsem],
    )(*parts)


def add_core_halves(part5, theirs, core, tag):
    nl, _, _, r, c = part5.shape
    tr = r if r * c <= 256 * 1024 else 64

    def body(core_ref, a_ref, b_ref, o_ref):
        o_ref[...] = _cast(a_ref[...] + b_ref[...])

    blk = lambda which: pl.BlockSpec((None, None, None, tr, c), lambda l, q, i, cref: (l, q, cref[0] if which else 0, i, 0))
    return pl.pallas_call(
        body, name=f"add_core_halves_{tag}", out_shape=_sds((nl, 4, 1, r, c), _MMT),
        grid_spec=pltpu.PrefetchScalarGridSpec(num_scalar_prefetch=1, grid=(nl, 4, r // tr), in_specs=[blk(True), blk(False)],
                                               out_specs=blk(False)),
        compiler_params=pltpu.CompilerParams(dimension_semantics=("arbitrary",) * 3, vmem_limit_bytes=VMEM_LIMIT_BYTES),
    )(core, part5, theirs)


def exchange_chips(s4s, tag):
    n = len(s4s)

    def body(*refs):
        s_refs, o_refs, (send_sems, recv_sems, local_sems) = refs[:n], refs[n:2 * n], refs[2 * n:]
        x, y, cc = lax.axis_index("x"), lax.axis_index("y"), lax.axis_index("c")
        copies = [pltpu.make_async_copy(s_refs[a].at[:, pl.ds(2 * x + y, 1)], o_refs[a].at[0], local_sems.at[a]) for a in range(n)]
        for k in range(1, 4):
            px = 1 - x if (k >> 1) & 1 else x
            py = 1 - y if k & 1 else y
            for a in range(n):
                copies.append(pltpu.make_async_remote_copy(
                    src_ref=s_refs[a].at[:, pl.ds(2 * px + py, 1)], dst_ref=o_refs[a].at[k], send_sem=send_sems.at[3 * a + k - 1],
                    recv_sem=recv_sems.at[3 * a + k - 1], device_id=(px, py, cc), device_id_type=pl.DeviceIdType.MESH))
        for cp in copies:
            cp.start()
        for cp in copies:
            cp.wait()

    nsem = pltpu.SemaphoreType.DMA((3 * n,))
    return pl.pallas_call(
        body, name=f"exchange_chips_{tag}", in_specs=[_ANY] * n, out_specs=[_ANY] * n,
        out_shape=[_sds((4, s.shape[0], 1, 1, s.shape[3], s.shape[4]), s.dtype) for s in s4s],
        scratch_shapes=[nsem, nsem, pltpu.SemaphoreType.DMA((n,))],
    )(*s4s)


def _row_tile(rows):
    if rows <= 512 and rows % 64:
        return rows
    for t in (64, 40, 32, 16, 8):
        if rows % t == 0:
            return t
    return rows


def sum_slots(slots, tag):
    ns, rows, c = slots.shape
    tr = _row_tile(rows)

    def body(s_ref, o_ref):
        acc = s_ref[0].astype(F32)
        for k in range(1, ns):
            acc = acc + s_ref[k].astype(F32)
        o_ref[...] = acc

    return _call(body, f"sum_slots_{tag}", (rows // tr,), [pl.BlockSpec((ns, tr, c), lambda i: (0, i, 0))],
                 pl.BlockSpec((tr, c), lambda i: (i, 0)), _sds((rows, c)))(slots)


def _adamw(w, g, m, v):
    m = ADAM_B1 * m + (1.0 - ADAM_B1) * g
    v = ADAM_B2 * v + (1.0 - ADAM_B2) * (g * g)
    m_hat = m / (1.0 - ADAM_B1 ** ADAM_STEP)
    v_hat = v / (1.0 - ADAM_B2 ** ADAM_STEP)
    delta = -ADAM_LR * (m_hat / (jnp.sqrt(v_hat) + ADAM_EPS) + ADAM_WD * w)
    return delta, m, v


def adam_update(w, m, v, g, tag, slots=False):
    rows, c = w.shape
    tr = _row_tile(rows)
    ns = g.shape[0] if slots else 0

    def body(w_ref, m_ref, v_ref, g_ref, go_ref, d_ref, mo_ref, vo_ref):
        if slots:
            g = g_ref[0].astype(F32)
            for k in range(1, ns):
                g = g + g_ref[k].astype(F32)
        else:
            g = g_ref[...]
        d, mn, vn = _adamw(w_ref[...], g, m_ref[...], v_ref[...])
        go_ref[...] = g
        d_ref[...] = d
        mo_ref[...] = mn
        vo_ref[...] = vn

    blk = pl.BlockSpec((tr, c), lambda i: (i, 0))
    gspec = pl.BlockSpec((ns, tr, c), lambda i: (0, i, 0)) if slots else blk
    return _call(body, f"adam_{tag}", (rows // tr,), [blk, blk, blk, gspec], [blk] * 4, [_sds((rows, c))] * 4)(w, m, v, g)


_SMALL = (("norm_g", (2, 1024)), ("fox_f_bias", (2, 4)), ("fox_q_norm", (2, 64)), ("fox_k_norm", (2, 64)),
          ("hgrn_lb_logits", (2, 256)), ("hgrn_out_norm", (2, 256)), ("pool_w", (2, 4, 64, 64)), ("pool_scale", (2, 256)),
          ("mem_norm_g", (2, 1024)), ("mem_q_norm", (2, 64)), ("mem_k_norm", (2, 64)))
_SLAB_ROWS = 312


def pack_small(d):
    flat = jnp.concatenate([d[n].reshape(-1) for n, _ in _SMALL])
    return jnp.pad(flat, (0, _SLAB_ROWS * 128 - flat.shape[0])).reshape(_SLAB_ROWS, 128)


def unpack_small(slab):
    flat, out, off = slab.reshape(-1), {}, 0
    for n, shp in _SMALL:
        size = 1
        for e in shp:
            size *= e
        out[n] = flat[off:off + size].reshape(shp)
        off += size
    return out


def small_grads(g0, g1, lb_logits):
    st = lambda f: jnp.stack([f(g0), f(g1)])
    heads = lambda a: a.reshape(NH, HD).sum(0)
    p1 = jax.nn.sigmoid(lb_logits[1] - lb_logits[0])
    inside = (p1 > 0.0) & (p1 < 1.0 - 1e-6)
    dl1 = jnp.where(inside, g1["lb"][0] * p1 * (1.0 - p1), 0.0)
    diag = lambda a: jnp.stack([a.reshape(4, HD, 4, HD)[i, :, i, :] for i in range(4)])
    return dict(norm_g=st(lambda g: g["norm_g"][0]), fox_f_bias=st(lambda g: g["fox_f_bias"][0, :NH]),
                fox_q_norm=st(lambda g: heads(g["fox_q_norm"])), fox_k_norm=st(lambda g: heads(g["fox_k_norm"])),
                hgrn_lb_logits=jnp.stack([-dl1, dl1]), hgrn_out_norm=st(lambda g: g["hgrn_out_norm"][0]),
                pool_w=st(lambda g: diag(g["pool_wbd"])), pool_scale=st(lambda g: g["pool_scale"][0]),
                mem_norm_g=st(lambda g: g["mem_norm_g"][0]), mem_q_norm=st(lambda g: heads(g["mem_q_norm"])),
                mem_k_norm=st(lambda g: heads(g["mem_k_norm"])))


def kernel(x, mem, norm_g, w_in, fox_f_bias, fox_q_norm, fox_k_norm, hgrn_lb_logits, hgrn_out_norm, pool_w, pool_scale, mem_norm_g, mem_w_kv, mem_q_norm, mem_k_norm, w_out, loss_target, m_norm_g, m_w_in, m_fox_f_bias, m_fox_q_norm, m_fox_k_norm, m_hgrn_lb_logits, m_hgrn_out_norm, m_pool_w, m_pool_scale, m_mem_norm_g, m_mem_w_kv, m_mem_q_norm, m_mem_k_norm, m_w_out, v_norm_g, v_w_in, v_fox_f_bias, v_fox_q_norm, v_fox_k_norm, v_hgrn_lb_logits, v_hgrn_out_norm, v_pool_w, v_pool_scale, v_mem_norm_g, v_mem_w_kv, v_mem_q_norm, v_mem_k_norm, v_w_out):
    given = dict(norm_g=(norm_g, m_norm_g, v_norm_g), w_in=(w_in, m_w_in, v_w_in), fox_f_bias=(fox_f_bias, m_fox_f_bias, v_fox_f_bias),
                 fox_q_norm=(fox_q_norm, m_fox_q_norm, v_fox_q_norm), fox_k_norm=(fox_k_norm, m_fox_k_norm, v_fox_k_norm),
                 hgrn_lb_logits=(hgrn_lb_logits, m_hgrn_lb_logits, v_hgrn_lb_logits),
                 hgrn_out_norm=(hgrn_out_norm, m_hgrn_out_norm, v_hgrn_out_norm), pool_w=(pool_w, m_pool_w, v_pool_w),
                 pool_scale=(pool_scale, m_pool_scale, v_pool_scale), mem_norm_g=(mem_norm_g, m_mem_norm_g, v_mem_norm_g),
                 mem_w_kv=(mem_w_kv, m_mem_w_kv, v_mem_w_kv), mem_q_norm=(mem_q_norm, m_mem_q_norm, v_mem_q_norm),
                 mem_k_norm=(mem_k_norm, m_mem_k_norm, v_mem_k_norm), w_out=(w_out, m_w_out, v_w_out))
    order = ("norm_g", "w_in", "fox_f_bias", "fox_q_norm", "fox_k_norm", "hgrn_lb_logits", "hgrn_out_norm", "pool_w",
             "pool_scale", "mem_norm_g", "mem_w_kv", "mem_q_norm", "mem_k_norm", "w_out")

    w_in_full, w_out_full, w_kv_full = all_gather_rows([_cast(permute_cols(w_in)), _cast(w_out), _cast(mem_w_kv)], "weights")
    p = prepare_params(norm_g, w_in_full, fox_f_bias, fox_q_norm, fox_k_norm, hgrn_lb_logits, hgrn_out_norm, pool_w,
                       pool_scale, mem_norm_g, w_kv_full, mem_q_norm, mem_k_norm, w_out_full)

    loss_part, grad_x, g0, g1 = local_step(x, mem, loss_target, p)
    loss = lax.psum(loss_part, ("x", "y", "c"))

    res = {}
    core = lax.axis_index("c").astype(jnp.int32).reshape(1)

    names = ("w_in", "w_out", "mem_w_kv")
    part5 = []
    for name in names:
        nl, r, _ = given[name][0].shape
        g2 = jnp.stack([g0[name], g1[name]])
        part5.append(g2.reshape(nl, 4, 2, r, g2.shape[-1]))
    s4 = [add_core_halves(p5, th, core, name) for name, p5, th in zip(names, part5, exchange_cores(part5, "grads"))]
    for name, sl in zip(names, exchange_chips(s4, "grads")):
        w, m, v = given[name]
        nl, r, c = w.shape
        slots = sl.reshape(4, nl * r, sl.shape[-1])
        if name == "w_in":
            g = sum_slots(slots, name).reshape(nl, r, -1)
            out = adam_update(w.reshape(nl * r, c), m.reshape(nl * r, c), v.reshape(nl * r, c),
                              unpermute_cols(g).reshape(nl * r, c), name)
        else:
            out = adam_update(w.reshape(nl * r, c), m.reshape(nl * r, c), v.reshape(nl * r, c), slots, name, slots=True)
        res[name] = tuple(o.reshape(nl, r, c) for o in out)

    gsmall = pack_small(small_grads(g0, g1, hgrn_lb_logits))
    gathered = all_gather_rows([gsmall[None]], "small")[0].reshape(N_DEV, _SLAB_ROWS, 128)
    slabs = adam_update(*[pack_small({n: given[n][j] for n, _ in _SMALL}) for j in range(3)], gathered, "small", slots=True)
    small = [unpack_small(sl) for sl in slabs]
    for n, _ in _SMALL:
        res[n] = tuple(small[j][n] for j in range(4))

    return (loss, grad_x, *[res[n][0] for n in order], *[res[n][1] for n in order], *[res[n][2] for n in order],
            *[res[n][3] for n in order])
```

```python
import functools

import jax
import jax.numpy as jnp
from jax import lax
from jax.experimental import pallas as pl
from jax.experimental.pallas import tpu as pltpu

F32 = jnp.float32
BF = jnp.bfloat16
_MMT = BF

D_MODEL = 1024
GW = 256
HD = 64
NH = 4
CH = 64
N_MEM = 256
D_IN = 4100
D_INP = 4224
D_MIX = 1280
EPS = 1e-6
NEG_BIG = -1e30
LB_FLOOR = 1e-30
SCALE = HD ** -0.5
TQ = 256
TM = 512


def _token_tile(n):
    return TM if n % TM == 0 else TQ
N_DEV = 8
VMEM_LIMIT_BYTES = 56 * 1024 * 1024

ADAM_LR = 0.001
ADAM_B1 = 0.9
ADAM_B2 = 0.999
ADAM_EPS = 1e-08
ADAM_WD = 0.01
ADAM_STEP = 10

PIECES = (("A", 0, 768), ("B", 768, 768), ("C", 1536, 768), ("D", 2304, 256), ("E", 2560, 256),
          ("G", 2816, 1280), ("F", 4096, 128))
BWD_PIECES = (("A", 0, 768), ("Bq", 768, 256), ("Bk", 1024, 256), ("Bv", 1280, 256), ("C", 1536, 768),
              ("D", 2304, 256), ("E", 2560, 256), ("G", 2816, 1280), ("F", 4096, 128))
_ORIG = dict(fq=(0, 256), fk=(256, 512), fv=(512, 768), fg=(768, 1024), ff=(1024, 1028), sq=(1028, 1284),
             sk=(1284, 1540), sv=(1540, 1796), sg=(1796, 2052), hq=(2052, 2308), hf=(2308, 2564),
             hi=(2564, 2820), hg=(2820, 3076), pv=(3076, 3332), pg=(3332, 3588), mq=(3588, 3844), mg=(3844, 4100))
_PERM_ORDER = ("fq", "fk", "fv", "sq", "sk", "sv", "hq", "hf", "hi", "pv", "mq", "fg", "sg", "hg", "pg", "mg", "ff")
_ORIG_ORDER = ("fq", "fk", "fv", "fg", "ff", "sq", "sk", "sv", "sg", "hq", "hf", "hi", "hg", "pv", "pg", "mq", "mg")


def permute_cols(w):
    parts = [w[..., _ORIG[n][0]:_ORIG[n][1]] for n in _PERM_ORDER]
    parts.append(jnp.zeros(w.shape[:-1] + (D_INP - D_IN,), w.dtype))
    return jnp.concatenate(parts, axis=-1)


def unpermute_cols(g):
    start, off = {}, 0
    for n in _PERM_ORDER:
        start[n] = off
        off += _ORIG[n][1] - _ORIG[n][0]
    return jnp.concatenate([g[..., start[n]:start[n] + _ORIG[n][1] - _ORIG[n][0]] for n in _ORIG_ORDER], axis=-1)


def _cast(a):
    return a.astype(_MMT)


def _dg(a, b, ca, cb):
    return lax.dot_general(a, b, (((ca,), (cb,)), ((), ())), preferred_element_type=F32)


@jax.custom_vjp
def mm(a, b):
    return _dg(_cast(a), _cast(b), 1, 0)


@jax.custom_vjp
def mm_nt(a, b):
    return _dg(_cast(a), _cast(b), 1, 1)


@jax.custom_vjp
def mm_tn(a, b):
    return _dg(_cast(a), _cast(b), 0, 0)


mm.defvjp(lambda a, b: (mm(a, b), (a, b)),
          lambda r, g: (mm_nt(g, r[1]).astype(r[0].dtype), mm_tn(r[0], g).astype(r[1].dtype)))
mm_nt.defvjp(lambda a, b: (mm_nt(a, b), (a, b)),
             lambda r, g: (mm(g, r[1]).astype(r[0].dtype), mm_tn(g, r[0]).astype(r[1].dtype)))
mm_tn.defvjp(lambda a, b: (mm_tn(a, b), (a, b)),
             lambda r, g: (mm_nt(r[1], g).astype(r[0].dtype), mm(r[0], g).astype(r[1].dtype)))


def _split(a):
    hi = a.astype(_MMT)
    lo = (a - hi.astype(F32)).astype(_MMT)
    return hi, lo


@jax.custom_vjp
def xr(a, c, ct):
    hi, lo = _split(a)
    cc = _cast(c)
    return _dg(hi, cc, 1, 0) + _dg(lo, cc, 1, 0)


@jax.custom_vjp
def xl(c, ct, a):
    hi, lo = _split(a)
    cc = _cast(c)
    return _dg(cc, hi, 1, 0) + _dg(cc, lo, 1, 0)


xr.defvjp(lambda a, c, ct: (xr(a, c, ct), (c, ct)),
          lambda r, g: (xr(g, r[1], r[0]), jnp.zeros_like(r[0]), jnp.zeros_like(r[1])))
xl.defvjp(lambda c, ct, a: (xl(c, ct, a), (c, ct)),
          lambda r, g: (jnp.zeros_like(r[0]), jnp.zeros_like(r[1]), xl(r[1], r[0], g)))


def _iota(shape, dim):
    return lax.broadcasted_iota(jnp.int32, shape, dim)


def _hmask(h, n=GW):
    lane = _iota((1, n), 1)
    return ((lane >= h * HD) & (lane < (h + 1) * HD)).astype(F32)


def _bdmask(n=GW):
    return ((_iota((n, n), 0) >> 6) == (_iota((n, n), 1) >> 6)).astype(F32)


def _tri(n, kind="le"):
    r, c = _iota((n, n), 0), _iota((n, n), 1)
    return {"le": c <= r, "ge": c >= r, "gt": c > r, "lt": c < r}[kind].astype(F32)


def _onehot_lane(h, n=128):
    return (_iota((1, n), 1) == h).astype(F32)


def _logsig(x):
    return jnp.minimum(x, 0.0) - jnp.log1p(jnp.exp(-jnp.abs(x)))


def _sigmoid(x):
    return 0.5 * (jnp.tanh(0.5 * x) + 1.0)


def _silu(x):
    return x * _sigmoid(x)


def _rms(x, g):
    return x * lax.rsqrt(jnp.mean(x * x, axis=-1, keepdims=True) + EPS) * g


def _headrms(x, w, bd64):
    ms = xr(x * x, bd64, bd64)
    return x * lax.rsqrt(ms + EPS) * w


def _call(body, name, grid, in_specs, out_specs, out_shape, scratch=()):
    return pl.pallas_call(
        body, name=name, grid=grid, in_specs=in_specs, out_specs=out_specs, out_shape=out_shape,
        scratch_shapes=list(scratch),
        compiler_params=pltpu.CompilerParams(dimension_semantics=("arbitrary",) * len(grid),
                                             vmem_limit_bytes=VMEM_LIMIT_BYTES))


def _sds(shape, dtype=F32):
    return jax.ShapeDtypeStruct(shape, dtype)


def _acc(ref, val, first):
    @pl.when(first)
    def _():
        ref[...] = val

    @pl.when(jnp.logical_not(first))
    def _():
        ref[...] += val


def inproj_fwd(x2, g, w, layer, tag):
    t = x2.shape[0]
    TQ = _token_tile(t)

    def body(x_ref, g_ref, w_ref, ht_ref, *outs):
        h = _rms(x_ref[...], g_ref[...])
        hb = _cast(h)
        ht_ref[...] = _cast(h.T)
        for (_, c0, wd), o in zip(PIECES, outs):
            o[...] = _dg(hb, _cast(w_ref[:, c0:c0 + wd]), 1, 0)

    return _call(
        body, f"inproj_fwd_{tag}", (t // TQ,),
        [pl.BlockSpec((TQ, D_MODEL), lambda i: (i, 0)),
         pl.BlockSpec((None, 1, D_MODEL), lambda i: (layer, 0, 0)),
         pl.BlockSpec((None, D_MODEL, D_INP), lambda i: (layer, 0, 0))],
        [pl.BlockSpec((D_MODEL, TQ), lambda i: (0, i))] + [pl.BlockSpec((TQ, wd), lambda i: (i, 0)) for _, _, wd in PIECES],
        [_sds((D_MODEL, t), _MMT)] + [_sds((t, wd)) for _, _, wd in PIECES],
    )(x2, g, w)


def inproj_bwd_dx(x2, g, w, dy, dpieces, layer, tag):
    t = x2.shape[0]
    TQ = _token_tile(t)

    def body(x_ref, g_ref, w_ref, dy_ref, *rest):
        dps, (dx_ref, dg_ref) = rest[:len(BWD_PIECES)], rest[len(BWD_PIECES):]
        dh = None
        for (_, c0, wd), dp in zip(BWD_PIECES, dps):
            part = _dg(_cast(dp[...]), _cast(w_ref[:, c0:c0 + wd]), 1, 1)
            dh = part if dh is None else dh + part
        _, vjp = jax.vjp(_rms, x_ref[...], g_ref[...])
        dx, dg = vjp(dh)
        dx_ref[...] = dy_ref[...] + dx
        _acc(dg_ref, dg, pl.program_id(0) == 0)

    return _call(
        body, f"inproj_bwd_dx_{tag}", (t // TQ,),
        [pl.BlockSpec((TQ, D_MODEL), lambda i: (i, 0)),
         pl.BlockSpec((None, 1, D_MODEL), lambda i: (layer, 0, 0)),
         pl.BlockSpec((None, D_MODEL, D_INP), lambda i: (layer, 0, 0), pipeline_mode=pl.Buffered(1)),
         pl.BlockSpec((TQ, D_MODEL), lambda i: (i, 0))] + [pl.BlockSpec((TQ, wd), lambda i: (i, 0)) for _, _, wd in BWD_PIECES],
        [pl.BlockSpec((TQ, D_MODEL), lambda i: (i, 0)), pl.BlockSpec((1, D_MODEL), lambda i: (0, 0))],
        [_sds((t, D_MODEL)), _sds((1, D_MODEL))],
    )(x2, g, w, dy, *dpieces)


def matmul_acc(at, bs, tag):
    m, t = at.shape
    widths = [b.shape[1] for b in bs]
    tk = 2048 if t % 2048 == 0 else (512 if t % 512 == 0 else TQ)

    def body(a_ref, *refs):
        b_refs, o_ref = refs[:-1], refs[-1]
        a = _cast(a_ref[...])
        first, off = pl.program_id(0) == 0, 0
        for b_ref, n in zip(b_refs, widths):
            part = _dg(a, _cast(b_ref[...]), 1, 0)

            @pl.when(first)
            def _(part=part, off=off, n=n):
                o_ref[:, off:off + n] = part

            @pl.when(jnp.logical_not(first))
            def _(part=part, off=off, n=n):
                o_ref[:, off:off + n] += part

            off += n

    return _call(
        body, f"matmul_acc_{tag}", (t // tk,),
        [pl.BlockSpec((m, tk), lambda i: (0, i))] + [pl.BlockSpec((tk, n), lambda i: (i, 0)) for n in widths],
        pl.BlockSpec((m, sum(widths)), lambda i: (0, 0)),
        _sds((m, sum(widths))),
    )(at, *bs)


def _fox_prep_fn(q, k, ff, qw, kw, bias, carry, bd64, tri, trit, last):
    qn = _headrms(q, qw, bd64)
    kn = _headrms(k, kw, bd64)
    lf = _logsig(ff + bias)
    c = xl(tri, trit, lf) + carry
    return qn, kn, c, jnp.sum(c * last, axis=0, keepdims=True)


def _prep_consts(tq):
    return _bdmask() * (1.0 / HD), _tri(tq), _tri(tq, "ge"), (_iota((tq, 1), 0) == tq - 1).astype(F32)


def fox_prep_fwd(pa, pf, qw, kw, bias, bl, s, layer, tag):
    TQ = _token_tile(s)
    nq = s // TQ

    def body(q_ref, k_ref, v_ref, f_ref, qw_ref, kw_ref, b_ref, qn_ref, kn_ref, vb_ref, cq_ref, ck_ref, carry):
        @pl.when(pl.program_id(1) == 0)
        def _():
            carry[...] = jnp.zeros_like(carry)

        qn, kn, c, cl = _fox_prep_fn(q_ref[...], k_ref[...], f_ref[...], qw_ref[...], kw_ref[...], b_ref[...],
                                     carry[...], *_prep_consts(TQ))
        carry[...] = cl
        qn_ref[...] = _cast(qn)
        kn_ref[...] = _cast(kn)
        vb_ref[...] = _cast(v_ref[...])
        cq_ref[...] = c
        ck_ref[...] = c.T[0:8, :]

    tok = lambda j: pl.BlockSpec((TQ, GW), lambda b, i: (b * nq + i, j))
    par = lambda n: pl.BlockSpec((None, 1, n), lambda b, i: (layer, 0, 0))
    return _call(
        body, f"fox_prep_fwd_{tag}", (bl, nq),
        [tok(0), tok(1), tok(2), pl.BlockSpec((TQ, 128), lambda b, i: (b * nq + i, 0)), par(GW), par(GW), par(128)],
        [tok(0), tok(0), tok(0), pl.BlockSpec((TQ, 128), lambda b, i: (b * nq + i, 0)),
         pl.BlockSpec((None, 8, TQ), lambda b, i: (b, 0, i))],
        [_sds((bl * s, GW), _MMT)] * 3 + [_sds((bl * s, 128)), _sds((bl, 8, s))],
        [pltpu.VMEM((1, 128), F32)],
    )(pa, pa, pa, pf, qw, kw, bias)


def fox_prep_bwd(pa, pf, qw, kw, bias, cq, dqn, dkn, dv, dck, bl, s, layer, tag):
    TQ = _token_tile(s)
    nq = s // TQ

    def body(q_ref, k_ref, f_ref, qw_ref, kw_ref, b_ref, cq_ref, cprev_ref, dqn_ref, dkn_ref, dv_ref, dck_ref,
             da_ref, df_ref, dqw_ref, dkw_ref, db_ref, dcarry):
        i = pl.program_id(1)
        first = jnp.logical_and(pl.program_id(0) == 0, i == 0)

        @pl.when(i == 0)
        def _():
            dcarry[...] = jnp.zeros_like(dcarry)

        last = (_iota((TQ, 1), 0) == TQ - 1).astype(F32)
        carry_in = jnp.where(i == nq - 1, 0.0, jnp.sum(cprev_ref[...] * last, axis=0, keepdims=True))
        consts = _prep_consts(TQ)
        _, vjp = jax.vjp(lambda *a: _fox_prep_fn(*a, *consts), q_ref[...], k_ref[...], f_ref[...], qw_ref[...],
                         kw_ref[...], b_ref[...], carry_in)
        dc = dck_ref[...].T
        dq, dk, dff, dqw, dkw, dbias, dcin = vjp((dqn_ref[...], dkn_ref[...], dc, dcarry[...]))
        dcarry[...] = dcin
        da_ref[:, 0:GW] = _cast(dq)
        da_ref[:, GW:2 * GW] = _cast(dk)
        da_ref[:, 2 * GW:3 * GW] = _cast(dv_ref[...])
        df_ref[...] = _cast(dff)
        _acc(dqw_ref, dqw, first)
        _acc(dkw_ref, dkw, first)
        _acc(db_ref, dbias, first)

    rv = lambda b, i: b * nq + (nq - 1 - i)
    tok = lambda j: pl.BlockSpec((TQ, GW), lambda b, i: (rv(b, i), j))
    tok0 = pl.BlockSpec((TQ, GW), lambda b, i: (rv(b, i), 0))
    t128 = pl.BlockSpec((TQ, 128), lambda b, i: (rv(b, i), 0))
    prev = pl.BlockSpec((TQ, 128), lambda b, i: (jnp.maximum(rv(b, i) - 1, 0), 0))
    par = lambda n: pl.BlockSpec((None, 1, n), lambda b, i: (layer, 0, 0))
    acc = lambda n: pl.BlockSpec((1, n), lambda b, i: (0, 0))
    return _call(
        body, f"fox_prep_bwd_{tag}", (bl, nq),
        [tok(0), tok(1), t128, par(GW), par(GW), par(128), t128, prev, tok0, tok0, tok0,
         pl.BlockSpec((None, 128, TQ), lambda b, i: (b, 0, nq - 1 - i))],
        [pl.BlockSpec((TQ, 3 * GW), lambda b, i: (rv(b, i), 0)), t128, acc(GW), acc(GW), acc(128)],
        [_sds((bl * s, 3 * GW), _MMT), _sds((bl * s, 128), _MMT), _sds((1, GW)), _sds((1, GW)), _sds((1, 128))],
        [pltpu.VMEM((1, 128), F32)],
    )(pa, pa, pf, qw, kw, bias, cq, cq, dqn, dkn, dv, dck)


def _lane_pick(x, h):
    return jnp.sum(x * _onehot_lane(h), axis=-1, keepdims=True)


TA_BIG = 256
TA_FWD = 512
TK_FOX = 512
TK_SB = 256


def _stack_heads(x, scale=1.0):
    return _cast(jnp.concatenate([x * (_hmask(h) * scale) for h in range(NH)], axis=0))


def _stack_cols(x):
    return jnp.concatenate([_lane_pick(x, h) for h in range(NH)], axis=0)


def _spread_heads(col):
    ta = col.shape[0] // NH
    return sum(col[h * ta:(h + 1) * ta] * _hmask(h) for h in range(NH))


def _lanes_cat(w):
    ta = w.shape[0] // NH
    return jnp.concatenate([w[h * ta:(h + 1) * ta] for h in range(NH)], axis=1)


def _mask_stack(x):
    return _cast(jnp.concatenate([x * _hmask(h).astype(x.dtype) for h in range(NH)], axis=0))


def _stack_rows(i, ta):
    return i * ta + (_iota((NH * ta, 1), 0) & (ta - 1))


def _n_key_tiles(i, tk, ta):
    assert ta <= tk and tk % ta == 0, "a query tile's diagonal must lie inside one key tile (only the last key tile is masked)"
    return lax.shift_right_logical(i * ta, tk.bit_length() - 1) + 1


def fox_attn_fwd(qn, kn, vb, cq, ck, bl, s, tag):
    TA, TK = min(TA_FWD, s), min(TK_FOX, s)
    nq, SROWS = s // TA, NH * TA

    def body(q_ref, k_ref, v_ref, cq_ref, ck_ref, o_ref, lse_ref, acc, vst):
        i = pl.program_id(1)

        @pl.when(i == 0)
        def _():
            _fill_stacked(vst, v_ref, s, TK)

        qs = _stack_heads(q_ref[...].astype(F32), SCALE)
        cqs = _stack_cols(cq_ref[...])
        row = _stack_rows(i, TA)
        acc[...] = jnp.zeros_like(acc)

        def step(j, ml):
            m, l = ml
            ks = pl.ds(pl.multiple_of(j * TK, TK), TK)
            ckb = jnp.concatenate([jnp.broadcast_to(ck_ref[h:h + 1, ks], (TA, TK)) for h in range(NH)], axis=0)
            sc = _dg(qs, k_ref[ks, :], 1, 1) + cqs - ckb
            col = j * TK + _iota((1, TK), 1)
            sc = jnp.where(col <= row, sc, NEG_BIG)
            m_new = jnp.maximum(m, jnp.max(sc, axis=-1, keepdims=True))
            alpha = jnp.exp(m - m_new)
            p = jnp.exp(sc - m_new)
            vs = vst[pl.ds(pl.multiple_of(j * NH * TK, NH * TK), NH * TK), :]
            acc[...] = _spread_heads(alpha) * acc[...] + _dg(_lanes_cat(_cast(p)), vs, 1, 0)
            return m_new, alpha * l + jnp.sum(p, axis=-1, keepdims=True)

        m, l = lax.fori_loop(0, _n_key_tiles(i, TK, TA), step, (jnp.full((SROWS, 1), NEG_BIG, F32), jnp.zeros((SROWS, 1), F32)))
        o_ref[...] = acc[...] / _spread_heads(l)
        lse_h = m + jnp.log(l)
        lse_ref[...] = sum(lse_h[h * TA:(h + 1) * TA] * _onehot_lane(h) for h in range(NH))

    tok = pl.BlockSpec((TA, GW), lambda b, i: (b * nq + i, 0))
    seq = pl.BlockSpec((s, GW), lambda b, i: (b, 0))
    t128 = pl.BlockSpec((TA, 128), lambda b, i: (b * nq + i, 0))
    return _call(
        body, f"fox_attn_fwd_{tag}", (bl, nq),
        [tok, seq, seq, t128, pl.BlockSpec((None, 8, s), lambda b, i: (b, 0, 0))],
        [tok, t128], [_sds((bl * s, GW)), _sds((bl * s, 128))],
        [pltpu.VMEM((TA, GW), F32), pltpu.VMEM((NH * s, GW), _MMT)],
    )(qn, kn, vb, cq, ck)


def fox_attn_bwd(qn, kn, vb, cq, ck, lse, do, bl, s, tag):
    TA, TK = min(TA_BIG, s), min(TK_FOX, s)
    nq, SROWS = s // TA, NH * TA

    def body(q_ref, k_ref, v_ref, cq_ref, ck_ref, lse_ref, do_ref, dq_ref, dk_ref, dv_ref, dck_ref, dqa, p_s, dp_s, kst):
        i = pl.program_id(1)

        @pl.when(i == 0)
        def _():
            dk_ref[...] = jnp.zeros_like(dk_ref)
            dv_ref[...] = jnp.zeros_like(dv_ref)
            dck_ref[...] = jnp.zeros_like(dck_ref)
            _fill_stacked(kst, k_ref, s, TK)

        qs = _stack_heads(q_ref[...].astype(F32), SCALE)
        dos = _stack_heads(do_ref[...])
        cqs, lses = _stack_cols(cq_ref[...]), _stack_cols(lse_ref[...])
        row = _stack_rows(i, TA)
        dqa[...] = jnp.zeros_like(dqa)
        nk = _n_key_tiles(i, TK, TA)

        def probs(j, delta):
            ks = pl.ds(pl.multiple_of(j * TK, TK), TK)
            ckb = jnp.concatenate([jnp.broadcast_to(ck_ref[h:h + 1, ks], (TA, TK)) for h in range(NH)], axis=0)
            sc = _dg(qs, k_ref[ks, :], 1, 1) + cqs - ckb
            col = j * TK + _iota((1, TK), 1)
            p = jnp.where(col <= row, jnp.exp(sc - lses), 0.0)
            dp = _dg(dos, v_ref[ks, :], 1, 1)
            p_s[:, ks] = p
            dp_s[:, ks] = dp
            return delta + jnp.sum(p * dp, axis=-1, keepdims=True)

        delta = lax.fori_loop(0, nk, probs, jnp.zeros((SROWS, 1), F32))

        def step(j, carry):
            ks = pl.ds(pl.multiple_of(j * TK, TK), TK)
            p = p_s[:, ks]
            ds = p * (dp_s[:, ks] - delta)
            dsb = _cast(ds)
            dqa[...] += _dg(_lanes_cat(dsb), kst[pl.ds(pl.multiple_of(j * NH * TK, NH * TK), NH * TK), :], 1, 0) * SCALE
            dk_ref[ks, :] += _dg(dsb, qs, 0, 0)
            dv_ref[ks, :] += _dg(_cast(p), dos, 0, 0)
            for h in range(NH):
                dck_ref[h:h + 1, ks] -= jnp.sum(ds[h * TA:(h + 1) * TA], axis=0, keepdims=True)
            return carry

        lax.fori_loop(0, nk, step, 0)
        dq_ref[...] = dqa[...]

    tok = pl.BlockSpec((TA, GW), lambda b, i: (b * nq + i, 0))
    seq = pl.BlockSpec((s, GW), lambda b, i: (b, 0))
    t128 = pl.BlockSpec((TA, 128), lambda b, i: (b * nq + i, 0))
    return _call(
        body, f"fox_attn_bwd_{tag}", (bl, nq),
        [tok, seq, seq, t128, pl.BlockSpec((None, 8, s), lambda b, i: (b, 0, 0)), t128, tok],
        [tok, seq, seq, pl.BlockSpec((None, 128, s), lambda b, i: (b, 0, 0))],
        [_sds((bl * s, GW)), _sds((bl * s, GW)), _sds((bl * s, GW)), _sds((bl, 128, s))],
        [pltpu.VMEM((TA, GW), F32), pltpu.VMEM((SROWS, s), F32), pltpu.VMEM((SROWS, s), F32), pltpu.VMEM((NH * s, GW), _MMT)],
    )(qn, kn, vb, cq, ck, lse, do)


def _sb_block(qh, kb, valid, upper, r_carry):
    z = _dg(qh, kb, 1, 1)
    ls = _logsig(z)
    lom = ls - z if valid is None else jnp.where(valid, ls - z, 0.0)
    between = xr(lom, upper, upper) + r_carry
    w = jnp.exp(ls + between)
    return ls, lom, (w if valid is None else jnp.where(valid, w, 0.0))


def _fill_stacked(dst, src_ref, s, tk):
    for j in range(s // tk):
        dst[j * NH * tk:(j + 1) * NH * tk, :] = _mask_stack(src_ref[j * tk:(j + 1) * tk, :])


def sb_attn_fwd(pb, bl, s, tag):
    TA, TK = TA_BIG, TK_SB
    nq, SROWS = s // TA, NH * TA

    def body(q_ref, k_ref, v_ref, o_ref, acc, vst):
        i = pl.program_id(1)

        @pl.when(i == 0)
        def _():
            _fill_stacked(vst, v_ref, s, TK)

        qs = _stack_heads(q_ref[...], SCALE)
        upper = _tri(TK, "lt")
        last = _n_key_tiles(i, TK, TA) - 1

        def step(j, r, valid):
            ks = pl.ds(pl.multiple_of(j * TK, TK), TK)
            _, lom, w = _sb_block(qs, _cast(k_ref[ks, :]), valid, upper, r)
            acc[...] += _dg(_lanes_cat(_cast(w)), vst[pl.ds(pl.multiple_of(j * NH * TK, NH * TK), NH * TK), :], 1, 0)
            return r + jnp.sum(lom, axis=-1, keepdims=True)

        acc[...] = jnp.zeros_like(acc)
        r = step(last, jnp.zeros((SROWS, 1), F32), last * TK + _iota((1, TK), 1) < _stack_rows(i, TA))
        lax.fori_loop(0, last, lambda jj, r: step(last - 1 - jj, r, None), r)
        o_ref[...] = acc[...]

    tok = lambda j: pl.BlockSpec((TA, GW), lambda b, i: (b * nq + i, j))
    seq = lambda j: pl.BlockSpec((s, GW), lambda b, i: (b, j))
    return _call(
        body, f"sb_attn_fwd_{tag}", (bl, nq), [tok(0), seq(1), seq(2)],
        pl.BlockSpec((TA, GW), lambda b, i: (b * nq + i, 0)), _sds((bl * s, GW)),
        [pltpu.VMEM((TA, GW), F32), pltpu.VMEM((NH * s, GW), _MMT)],
    )(pb, pb, pb)


def sb_attn_bwd(pb, do, bl, s, tag):
    TA, TK = TA_BIG, TK_SB
    nq, SROWS = s // TA, NH * TA

    def body(q_ref, k_ref, v_ref, do_ref, dq_ref, dk_ref, dv_ref, dqa, sig_s, nsig_s, w_s, g_s, kst):
        i = pl.program_id(1)

        @pl.when(i == 0)
        def _():
            dk_ref[...] = jnp.zeros_like(dk_ref)
            dv_ref[...] = jnp.zeros_like(dv_ref)
            _fill_stacked(kst, k_ref, s, TK)

        qs = _stack_heads(q_ref[...], SCALE)
        dos = _stack_heads(do_ref[...])
        upper = _tri(TK, "lt")
        before = _tri(TK, "gt")
        dqa[...] = jnp.zeros_like(dqa)
        last = _n_key_tiles(i, TK, TA) - 1
        diag = last * TK + _iota((1, TK), 1) < _stack_rows(i, TA)

        def weights(j, r, valid):
            ks = pl.ds(pl.multiple_of(j * TK, TK), TK)
            ls, lom, w = _sb_block(qs, _cast(k_ref[ks, :]), valid, upper, r)
            sig_s[:, ks] = _cast(jnp.exp(ls))
            nsig_s[:, ks] = _cast(jnp.exp(lom))
            w_s[:, ks] = _cast(w)
            g_s[:, ks] = _dg(dos, _cast(v_ref[ks, :]), 1, 1) * w
            return r + jnp.sum(lom, axis=-1, keepdims=True)

        r = weights(last, jnp.zeros((SROWS, 1), F32), diag)
        lax.fori_loop(0, last, lambda jj, r: weights(last - 1 - jj, r, None), r)

        def step(j, cpre, valid):
            ks = pl.ds(pl.multiple_of(j * TK, TK), TK)
            g = g_s[:, ks]
            pre = cpre + xr(g, before, before)
            dz = g * nsig_s[:, ks].astype(F32) - sig_s[:, ks].astype(F32) * pre
            dzb = _cast(dz if valid is None else jnp.where(valid, dz, 0.0))
            dqa[...] += _dg(_lanes_cat(dzb), kst[pl.ds(pl.multiple_of(j * NH * TK, NH * TK), NH * TK), :], 1, 0) * SCALE
            dk_ref[ks, :] += _dg(dzb, qs, 0, 0)
            dv_ref[ks, :] += _dg(w_s[:, ks], dos, 0, 0)
            return cpre + jnp.sum(g, axis=-1, keepdims=True)

        cpre = lax.fori_loop(0, last, lambda j, c: step(j, c, None), jnp.zeros((SROWS, 1), F32))
        step(last, cpre, diag)
        dq_ref[...] = dqa[...]

    tok = lambda j: pl.BlockSpec((TA, GW), lambda b, i: (b * nq + i, j))
    seq = lambda j: pl.BlockSpec((s, GW), lambda b, i: (b, j))
    return _call(
        body, f"sb_attn_bwd_{tag}", (bl, nq), [tok(0), seq(1), seq(2), tok(0)],
        [tok(0), seq(0), seq(0)], [_sds((bl * s, GW))] * 3,
        [pltpu.VMEM((TA, GW), F32), pltpu.VMEM((SROWS, s), _MMT), pltpu.VMEM((SROWS, s), _MMT),
         pltpu.VMEM((SROWS, s), _MMT), pltpu.VMEM((SROWS, s), F32), pltpu.VMEM((NH * s, GW), _MMT)],
    )(pb, pb, pb, do)


def _hgrn_consts():
    r, c = _iota((CH, CH), 0), _iota((CH, CH), 1)
    rr = _iota((CH, 1), 0)
    tri = (c <= r).astype(F32)
    lv = []
    for m in (8, 4, 2, 1):
        up = ((rr & (2 * m - 1)) >= m).astype(F32)
        selq = (((r & (2 * m - 1)) >= m) & (c == (r & ~(m - 1)) - 1)).astype(F32)
        selk = (((r & (2 * m - 1)) < m) & (c == (r & ~(m - 1)) + m - 1)).astype(F32)
        pm = (((r & ~(2 * m - 1)) == (c & ~(2 * m - 1))) & ((r & (2 * m - 1)) >= m) & ((c & (2 * m - 1)) < m)).astype(F32)
        lv.append((up, 1.0 - up, selq, selq.T, selk, selk.T, jnp.concatenate([pm] * NH, axis=0)))
    hm4 = lambda n: (((_iota((NH, 1, n), 2) & (GW - 1)) >> 6) == _iota((NH, 1, n), 0)).astype(F32)
    return dict(tri=tri, trit=tri.T, rr=rr, lv=lv, bd=_bdmask(), bd64=_bdmask() * (1.0 / HD),
                hm4={GW: hm4(GW), 3 * GW: hm4(3 * GW)})


def _hgrn_chunk_fn(hq, hf, hi, lb, wn, st, cs):
    q = _silu(hq)
    log_lb = jnp.log(jnp.maximum(lb, LB_FLOOR))
    a, bb = log_lb, jnp.log1p(-lb) + _logsig(hf)
    g = jnp.maximum(a, bb) + jnp.log1p(jnp.exp(-jnp.abs(a - bb)))
    k = (1.0 - lb) * _sigmoid(-hf)
    v = hi
    rr = cs["rr"]
    b = xl(cs["tri"], cs["trit"], g)
    row_of = lambda n: jnp.sum(b * (rr == n).astype(F32), axis=0, keepdims=True)
    o = mm_nt(q * jnp.exp(b), st)
    qs, ks = [], []
    for ib in (1, 2, 3):
        ref = row_of(16 * ib - 1)
        inq = ((rr >= 16 * ib) & (rr < 16 * ib + 16)).astype(F32)
        ink = (rr < 16 * ib).astype(F32)
        qs.append(q * jnp.exp((b - ref) * inq) * inq)
        ks.append(k * jnp.exp((ref - b) * ink) * ink)
    qcat, kcat = jnp.concatenate(qs, axis=1), jnp.concatenate(ks, axis=1)
    lvl = []
    for up, lo, selq, selqt, selk, selkt, pm in cs["lv"]:
        qe = q * jnp.exp((b - xl(selq, selqt, b)) * up) * up
        ke = k * jnp.exp((xl(selk, selkt, b) - b) * lo) * lo
        lvl.append((qe, ke, pm))
    stack = lambda x: (x[None] * cs["hm4"][x.shape[1]]).reshape(NH * CH, x.shape[1])
    a_all = mm_nt(stack(qcat), kcat)
    for qe, ke, pm4 in lvl:
        a_all = a_all + mm_nt(stack(qe), ke) * pm4
    o = o + jnp.sum(mm(a_all, v).reshape(NH, CH, GW) * cs["hm4"][GW], axis=0)
    o = o + xr(q * k, cs["bd"], cs["bd"]) * v
    b_last = row_of(CH - 1)
    st_new = st * jnp.exp(b_last) + mm_tn(v, k * jnp.exp(b_last - b)) * cs["bd"]
    return _headrms(o, wn, cs["bd64"]), st_new


def hgrn_fwd(pc, lb, wn, bl, s, layer, tag):
    nc = s // CH

    def body(q_ref, f_ref, i_ref, lb_ref, wn_ref, o_ref, st_ref, st):
        @pl.when(pl.program_id(0) == 0)
        def _():
            st[...] = jnp.zeros_like(st)

        cs = _hgrn_consts()
        for b in range(bl):
            st_ref[b] = st[b]
            o, st_new = _hgrn_chunk_fn(q_ref[b], f_ref[b], i_ref[b], lb_ref[...], wn_ref[...], st[b], cs)
            o_ref[b] = o
            st[b] = st_new

    tok = lambda j: pl.BlockSpec((bl, CH, GW), lambda c: (0, c, j))
    par = pl.BlockSpec((None, 1, GW), lambda c: (layer, 0, 0))
    pc3 = pc.reshape(bl, s, 3 * GW)
    o, states = _call(
        body, f"hgrn_fwd_{tag}", (nc,), [tok(0), tok(1), tok(2), par, par],
        [tok(0), pl.BlockSpec((bl, None, GW, GW), lambda c: (0, c, 0, 0))],
        [_sds((bl, s, GW)), _sds((bl, nc, GW, GW))],
        [pltpu.VMEM((bl, GW, GW), F32)],
    )(pc3, pc3, pc3, lb, wn)
    return o.reshape(bl * s, GW), states


def hgrn_bwd(pc, lb, wn, states, do, bl, s, layer, tag):
    nc = s // CH

    def body(q_ref, f_ref, i_ref, lb_ref, wn_ref, st_ref, do_ref, dc_ref, dlb_ref, dwn_ref, dst):
        c = pl.program_id(0)

        @pl.when(c == 0)
        def _():
            dst[...] = jnp.zeros_like(dst)

        cs = _hgrn_consts()
        dlb_sum = dwn_sum = None
        for b in range(bl):
            _, vjp = jax.vjp(lambda *a: _hgrn_chunk_fn(*a, cs), q_ref[b], f_ref[b], i_ref[b], lb_ref[...],
                             wn_ref[...], st_ref[b])
            dq, df, di, dlb, dwn, dst_in = vjp((do_ref[b], dst[b]))
            dst[b] = dst_in
            dc_ref[b, :, 0:GW] = _cast(dq)
            dc_ref[b, :, GW:2 * GW] = _cast(df)
            dc_ref[b, :, 2 * GW:3 * GW] = _cast(di)
            dlb_sum = dlb if dlb_sum is None else dlb_sum + dlb
            dwn_sum = dwn if dwn_sum is None else dwn_sum + dwn
        _acc(dlb_ref, dlb_sum, c == 0)
        _acc(dwn_ref, dwn_sum, c == 0)

    tok = lambda j: pl.BlockSpec((bl, CH, GW), lambda c: (0, nc - 1 - c, j))
    par = pl.BlockSpec((None, 1, GW), lambda c: (layer, 0, 0))
    acc = pl.BlockSpec((1, GW), lambda c: (0, 0))
    pc3 = pc.reshape(bl, s, 3 * GW)
    dc, dlb, dwn = _call(
        body, f"hgrn_bwd_{tag}", (nc,),
        [tok(0), tok(1), tok(2), par, par, pl.BlockSpec((bl, None, GW, GW), lambda c: (0, nc - 1 - c, 0, 0)), tok(0)],
        [pl.BlockSpec((bl, CH, 3 * GW), lambda c: (0, nc - 1 - c, 0)), acc, acc],
        [_sds((bl, s, 3 * GW), _MMT), _sds((1, GW)), _sds((1, GW))],
        [pltpu.VMEM((bl, GW, GW), F32)],
    )(pc3, pc3, pc3, lb, wn, states, do.reshape(bl, s, GW))
    return dc.reshape(bl * s, 3 * GW), dlb, dwn


def _shift_rows(x, k, up):
    n = x.shape[0]
    rr = _iota((n, 1), 0)
    if up:
        return jnp.where(rr < n - k, pltpu.roll(x, n - k, 0), 0.0)
    return jnp.where(rr >= k, pltpu.roll(x, k, 0), 0.0)


def _window_sums(x, up):
    s2 = x + _shift_rows(x, 1, up)
    s4 = s2 + _shift_rows(s2, 2, up)
    s8 = s4 + _shift_rows(s4, 4, up)
    s16 = s8 + _shift_rows(s8, 8, up)
    return s2, s4, s8, s16


def _pool_div(n):
    pos = (_iota((n, 1), 0) + 1).astype(F32)
    return [jnp.minimum(pos, float(w)) for w in (2, 4, 8, 16)]


def _pool_mix(sums, scaled):
    out = None
    for gi, sw in enumerate(sums):
        part = (sw if scaled is None else sw / scaled[gi]) * _hmask(gi)
        out = part if out is None else out + part
    return out


def pool_fwd(pd, wbd, scale, bl, s, layer, tag):
    def body(u_ref, w_ref, sc_ref, o_ref):
        u = u_ref[...]
        pm = _pool_mix(_window_sums(u, False), _pool_div(s)) - u
        o_ref[...] = _dg(_cast(pm), _cast(w_ref[...]), 1, 0) * sc_ref[...]

    seq = pl.BlockSpec((s, GW), lambda b: (b, 0))
    return _call(
        body, f"pool_fwd_{tag}", (bl,),
        [seq, pl.BlockSpec((None, GW, GW), lambda b: (layer, 0, 0)), pl.BlockSpec((None, 1, GW), lambda b: (layer, 0, 0))],
        seq, _sds((bl * s, GW)),
    )(pd, wbd, scale)


def pool_bwd(pd, wbd, scale, do, bl, s, layer, tag):
    def body(u_ref, w_ref, sc_ref, do_ref, du_ref, dw_ref, dsc_ref):
        first = pl.program_id(0) == 0
        u, do = u_ref[...], do_ref[...]
        div = _pool_div(s)
        pm = _pool_mix(_window_sums(u, False), div) - u
        ypre = _dg(_cast(pm), _cast(w_ref[...]), 1, 0)
        dys = do * sc_ref[...]
        _acc(dsc_ref, jnp.sum(do * ypre, axis=0, keepdims=True), first)
        _acc(dw_ref, _dg(_cast(pm), _cast(dys), 0, 0), first)
        dpm = _dg(_cast(dys), _cast(w_ref[...]), 1, 1)
        dsc = [dpm / d for d in div]
        adj = None
        for gi in range(4):
            part = _window_sums(dsc[gi] * _hmask(gi), True)[gi]
            adj = part if adj is None else adj + part
        du_ref[...] = _cast(adj - dpm)

    seq = pl.BlockSpec((s, GW), lambda b: (b, 0))
    return _call(
        body, f"pool_bwd_{tag}", (bl,),
        [seq, pl.BlockSpec((None, GW, GW), lambda b: (layer, 0, 0)), pl.BlockSpec((None, 1, GW), lambda b: (layer, 0, 0)), seq],
        [seq, pl.BlockSpec((GW, GW), lambda b: (0, 0)), pl.BlockSpec((1, GW), lambda b: (0, 0))],
        [_sds((bl * s, GW), _MMT), _sds((GW, GW)), _sds((1, GW))],
    )(pd, wbd, scale, do)


def _mem_prep_fn(mem, g, wk, wv, kw, bd64):
    mn = _rms(mem, g)
    return _headrms(mm(mn, wk), kw, bd64), mm(mn, wv)


def mem_prep_fwd(mem2, g, wkv, kw, bl, layer, tag):
    def body(m_ref, g_ref, wk_ref, wv_ref, kw_ref, k_ref, v_ref):
        k, v = _mem_prep_fn(m_ref[...], g_ref[...], wk_ref[...], wv_ref[...], kw_ref[...], _bdmask() * (1.0 / HD))
        k_ref[...] = k
        v_ref[...] = v

    blk = pl.BlockSpec((N_MEM, GW), lambda b: (b, 0))
    return _call(
        body, f"mem_prep_fwd_{tag}", (bl,),
        [pl.BlockSpec((N_MEM, D_MODEL), lambda b: (b, 0)), pl.BlockSpec((None, 1, D_MODEL), lambda b: (layer, 0, 0)),
         pl.BlockSpec((None, D_MODEL, GW), lambda b: (layer, 0, 0)), pl.BlockSpec((None, D_MODEL, GW), lambda b: (layer, 0, 1)),
         pl.BlockSpec((None, 1, GW), lambda b: (layer, 0, 0))],
        [blk, blk], [_sds((bl * N_MEM, GW))] * 2,
    )(mem2, g, wkv, wkv, kw)


def mem_prep_bwd(mem2, g, wkv, kw, dk, dv, bl, layer, tag):
    def body(m_ref, g_ref, wk_ref, wv_ref, kw_ref, dk_ref, dv_ref, dwk_ref, dwv_ref, dg_ref, dkw_ref):
        first = pl.program_id(0) == 0
        bd64 = _bdmask() * (1.0 / HD)
        _, vjp = jax.vjp(lambda g_, wk, wv, kw_: _mem_prep_fn(m_ref[...], g_, wk, wv, kw_, bd64),
                         g_ref[...], wk_ref[...].astype(F32), wv_ref[...].astype(F32), kw_ref[...])
        dg, dwk, dwv, dkw = vjp((dk_ref[...], dv_ref[...]))
        _acc(dwk_ref, dwk, first)
        _acc(dwv_ref, dwv, first)
        _acc(dg_ref, dg, first)
        _acc(dkw_ref, dkw, first)

    blk = pl.BlockSpec((N_MEM, GW), lambda b: (b, 0))
    return _call(
        body, f"mem_prep_bwd_{tag}", (bl,),
        [pl.BlockSpec((N_MEM, D_MODEL), lambda b: (b, 0)), pl.BlockSpec((None, 1, D_MODEL), lambda b: (layer, 0, 0)),
         pl.BlockSpec((None, D_MODEL, GW), lambda b: (layer, 0, 0)), pl.BlockSpec((None, D_MODEL, GW), lambda b: (layer, 0, 1)),
         pl.BlockSpec((None, 1, GW), lambda b: (layer, 0, 0)), blk, blk],
        [pl.BlockSpec((D_MODEL, GW), lambda b: (0, 0)), pl.BlockSpec((D_MODEL, GW), lambda b: (0, 0)),
         pl.BlockSpec((1, D_MODEL), lambda b: (0, 0)), pl.BlockSpec((1, GW), lambda b: (0, 0))],
        [_sds((D_MODEL, GW)), _sds((D_MODEL, GW)), _sds((1, D_MODEL)), _sds((1, GW))],
    )(mem2, g, wkv, wkv, kw, dk, dv)


def _mem_attn_fn(mq, qw, k, v, bd64):
    qn = _headrms(mq, qw, bd64)
    out = None
    for h in range(NH):
        hm = _hmask(h)
        lg = mm_nt(qn * hm, k) * SCALE
        e = jnp.exp(lg - lax.stop_gradient(jnp.max(lg, axis=-1, keepdims=True)))
        p = e / jnp.sum(e, axis=-1, keepdims=True)
        part = mm(p, v) * hm
        out = part if out is None else out + part
    return out


def mem_attn_fwd(pe, qw, k, v, bl, s, layer, tag):
    TQ = _token_tile(s)
    nq = s // TQ

    def body(q_ref, qw_ref, k_ref, v_ref, o_ref):
        o_ref[...] = _mem_attn_fn(q_ref[...], qw_ref[...], k_ref[...], v_ref[...], _bdmask() * (1.0 / HD))

    tok = pl.BlockSpec((TQ, GW), lambda b, i: (b * nq + i, 0))
    kv = pl.BlockSpec((N_MEM, GW), lambda b, i: (b, 0))
    return _call(
        body, f"mem_attn_fwd_{tag}", (bl, nq), [tok, pl.BlockSpec((None, 1, GW), lambda b, i: (layer, 0, 0)), kv, kv],
        tok, _sds((bl * s, GW)),
    )(pe, qw, k, v)


def mem_attn_bwd(pe, qw, k, v, do, bl, s, layer, tag):
    TQ = _token_tile(s)
    nq = s // TQ

    def body(q_ref, qw_ref, k_ref, v_ref, do_ref, dq_ref, dk_ref, dv_ref, dqw_ref):
        i = pl.program_id(1)
        bd64 = _bdmask() * (1.0 / HD)
        _, vjp = jax.vjp(lambda *a: _mem_attn_fn(*a, bd64), q_ref[...], qw_ref[...], k_ref[...], v_ref[...])
        dq, dqw, dk, dv = vjp(do_ref[...])
        dq_ref[...] = _cast(dq)
        _acc(dk_ref, dk, i == 0)
        _acc(dv_ref, dv, i == 0)
        _acc(dqw_ref, dqw, jnp.logical_and(pl.program_id(0) == 0, i == 0))

    tok = pl.BlockSpec((TQ, GW), lambda b, i: (b * nq + i, 0))
    kv = pl.BlockSpec((N_MEM, GW), lambda b, i: (b, 0))
    return _call(
        body, f"mem_attn_bwd_{tag}", (bl, nq),
        [tok, pl.BlockSpec((None, 1, GW), lambda b, i: (layer, 0, 0)), kv, kv, tok],
        [tok, kv, kv, pl.BlockSpec((1, GW), lambda b, i: (0, 0))],
        [_sds((bl * s, GW), _MMT), _sds((bl * N_MEM, GW)), _sds((bl * N_MEM, GW)), _sds((1, GW))],
    )(pe, qw, k, v, do)


def _gate_out_fn(outs, gates, wparts):
    y = None
    for o, g, w in zip(outs, gates, wparts):
        part = mm(o * _silu(g), w)
        y = part if y is None else y + part
    return y


def outproj_fwd(x2, outs, pg, wout, layer, tag):
    t = x2.shape[0]
    TQ = _token_tile(t)

    def body(x_ref, oa, ob, oc, od, oe, g_ref, w_ref, y_ref):
        outs_ = [r[...] for r in (oa, ob, oc, od, oe)]
        gates = [g_ref[:, j * GW:(j + 1) * GW] for j in range(5)]
        wparts = [w_ref[j * GW:(j + 1) * GW, :] for j in range(5)]
        y_ref[...] = x_ref[...] + _gate_out_fn(outs_, gates, wparts)

    tok = pl.BlockSpec((TQ, GW), lambda i: (i, 0))
    big = pl.BlockSpec((TQ, D_MODEL), lambda i: (i, 0))
    return _call(
        body, f"outproj_fwd_{tag}", (t // TQ,),
        [big] + [tok] * 5 + [pl.BlockSpec((TQ, D_MIX), lambda i: (i, 0)),
                            pl.BlockSpec((None, D_MIX, D_MODEL), lambda i: (layer, 0, 0))],
        big, _sds((t, D_MODEL)),
    )(x2, *outs, pg, wout)


def outproj_bwd(outs, pg, wout, dy, layer, tag):
    t = dy.shape[0]
    TQ = _token_tile(t)

    def body(oa, ob, oc, od, oe, g_ref, w_ref, dy_ref, da, db, dc, dd, de, dg_ref, dw_ref):
        outs_ = [r[...] for r in (oa, ob, oc, od, oe)]
        gates = [g_ref[:, j * GW:(j + 1) * GW] for j in range(5)]
        wparts = [w_ref[j * GW:(j + 1) * GW, :].astype(F32) for j in range(5)]
        _, vjp = jax.vjp(_gate_out_fn, outs_, gates, wparts)
        douts, dgates, dws = vjp(dy_ref[...])
        for r, val in zip((da, db, dc, dd, de), douts):
            r[...] = val
        first = pl.program_id(0) == 0
        for j in range(5):
            dg_ref[:, j * GW:(j + 1) * GW] = _cast(dgates[j])

        @pl.when(first)
        def _():
            for j in range(5):
                dw_ref[j * GW:(j + 1) * GW, :] = dws[j]

        @pl.when(jnp.logical_not(first))
        def _():
            for j in range(5):
                dw_ref[j * GW:(j + 1) * GW, :] += dws[j]

    tok = pl.BlockSpec((TQ, GW), lambda i: (i, 0))
    return _call(
        body, f"outproj_bwd_{tag}", (t // TQ,),
        [tok] * 5 + [pl.BlockSpec((TQ, D_MIX), lambda i: (i, 0)), pl.BlockSpec((None, D_MIX, D_MODEL), lambda i: (layer, 0, 0)),
                     pl.BlockSpec((TQ, D_MODEL), lambda i: (i, 0))],
        [tok] * 5 + [pl.BlockSpec((TQ, D_MIX), lambda i: (i, 0)), pl.BlockSpec((D_MIX, D_MODEL), lambda i: (0, 0))],
        [_sds((t, GW))] * 5 + [_sds((t, D_MIX), _MMT), _sds((D_MIX, D_MODEL))],
    )(*outs, pg, wout, dy)


def loss_head(y, tgt):
    t = y.shape[0]
    TQ = _token_tile(t)

    def body(y_ref, t_ref, l_ref, dy_ref):
        diff = y_ref[...] - t_ref[...]
        dy_ref[...] = diff * (1.0 / D_MODEL)
        part = 0.5 * jnp.sum(jnp.sum(diff * diff, axis=-1, keepdims=True) * (1.0 / D_MODEL), axis=0, keepdims=True)
        _acc(l_ref, jnp.broadcast_to(part, (8, 128)), pl.program_id(0) == 0)

    big = pl.BlockSpec((TQ, D_MODEL), lambda i: (i, 0))
    return _call(body, "loss_head", (t // TQ,), [big, big], [pl.BlockSpec((8, 128), lambda i: (0, 0)), big],
                 [_sds((8, 128)), _sds((t, D_MODEL))])(y, tgt)


def layer_fwd(x2, mem2, p, layer, bl, s):
    tag = f"l{layer}"
    ht, pa, pb, pc, pd, pe, pg, pf = inproj_fwd(x2, p["norm_g"], p["w_in"], layer, tag)
    qn, kn, vb, cq, ck = fox_prep_fwd(pa, pf, p["fox_q_norm"], p["fox_k_norm"], p["fox_f_bias"], bl, s, layer, tag)
    oa, lse = fox_attn_fwd(qn, kn, vb, cq, ck, bl, s, tag)
    ob = sb_attn_fwd(pb, bl, s, tag)
    oc, states = hgrn_fwd(pc, p["lb"], p["hgrn_out_norm"], bl, s, layer, tag)
    od = pool_fwd(pd, p["pool_wbd"], p["pool_scale"], bl, s, layer, tag)
    mk, mv = mem_prep_fwd(mem2, p["mem_norm_g"], p["mem_w_kv"], p["mem_k_norm"], bl, layer, tag)
    oe = mem_attn_fwd(pe, p["mem_q_norm"], mk, mv, bl, s, layer, tag)
    y = outproj_fwd(x2, (oa, ob, oc, od, oe), pg, p["w_out"], layer, tag)
    saved = dict(x2=x2, ht=ht, pa=pa, pb=pb, pc=pc, pd=pd, pe=pe, pg=pg, pf=pf, qn=qn, kn=kn, vb=vb, cq=cq, ck=ck,
                 oa=oa, lse=lse, ob=ob, oc=oc, states=states, od=od, mk=mk, mv=mv, oe=oe)
    return y, saved


def layer_bwd(dy, mem2, p, sv, layer, bl, s):
    tag = f"l{layer}"
    (doa, dob, doc, dod, doe, dg_gates, dwout) = outproj_bwd((sv["oa"], sv["ob"], sv["oc"], sv["od"], sv["oe"]), sv["pg"],
                                                              p["w_out"], dy, layer, tag)
    dqn, dkn, dv, dck = fox_attn_bwd(sv["qn"], sv["kn"], sv["vb"], sv["cq"], sv["ck"], sv["lse"], doa, bl, s, tag)
    d_a, d_f, dqw, dkw, dbias = fox_prep_bwd(sv["pa"], sv["pf"], p["fox_q_norm"], p["fox_k_norm"], p["fox_f_bias"], sv["cq"],
                                             dqn, dkn, dv, dck, bl, s, layer, tag)
    dsq, dsk, dsv = sb_attn_bwd(sv["pb"], dob, bl, s, tag)
    d_c, dlb, dwn = hgrn_bwd(sv["pc"], p["lb"], p["hgrn_out_norm"], sv["states"], doc, bl, s, layer, tag)
    d_d, dwbd, dpscale = pool_bwd(sv["pd"], p["pool_wbd"], p["pool_scale"], dod, bl, s, layer, tag)
    d_e, dmk, dmv, dmqw = mem_attn_bwd(sv["pe"], p["mem_q_norm"], sv["mk"], sv["mv"], doe, bl, s, layer, tag)
    dwk, dwv, dmg, dmkw = mem_prep_bwd(mem2, p["mem_norm_g"], p["mem_w_kv"], p["mem_k_norm"], dmk, dmv, bl, layer, tag)
    dpieces = (d_a, dsq, dsk, dsv, d_c, d_d, d_e, dg_gates, d_f)
    dx, dng = inproj_bwd_dx(sv["x2"], p["norm_g"], p["w_in"], dy, dpieces, layer, tag)
    groups = ((0, 4), (4, 7), (7, 9))
    dwin = jnp.concatenate([matmul_acc(sv["ht"], dpieces[a:b], f"{tag}_{a}") for a, b in groups], axis=1)
    grads = dict(norm_g=dng, w_in=dwin, fox_f_bias=dbias, fox_q_norm=dqw, fox_k_norm=dkw, lb=dlb, hgrn_out_norm=dwn,
                 pool_wbd=dwbd, pool_scale=dpscale, mem_norm_g=dmg, mem_w_kv=jnp.concatenate([dwk, dwv], axis=1),
                 mem_q_norm=dmqw, mem_k_norm=dmkw, w_out=dwout)
    return dx, grads


def _tile4(w):
    return jnp.tile(w, (1, NH))[:, None, :]


def prepare_params(norm_g, w_in_p, fox_f_bias, fox_q_norm, fox_k_norm, hgrn_lb_logits, hgrn_out_norm, pool_w, pool_scale,
                   mem_norm_g, mem_w_kv, mem_q_norm, mem_k_norm, w_out):
    p1 = jax.nn.sigmoid(hgrn_lb_logits[1] - hgrn_lb_logits[0])
    lb = jnp.stack([jnp.zeros_like(p1), jnp.clip(p1, 0.0, 1.0 - 1e-6)])
    eye = jnp.eye(4, dtype=F32)
    wbd = jnp.einsum("lgcd,gh->lgchd", pool_w, eye).reshape(2, GW, GW)
    return dict(norm_g=norm_g[:, None, :], w_in=w_in_p, fox_f_bias=jnp.pad(fox_f_bias, ((0, 0), (0, 124)))[:, None, :],
                fox_q_norm=_tile4(fox_q_norm), fox_k_norm=_tile4(fox_k_norm), lb=lb[:, None, :],
                hgrn_out_norm=hgrn_out_norm[:, None, :], pool_wbd=wbd, pool_scale=pool_scale[:, None, :],
                mem_norm_g=mem_norm_g[:, None, :], mem_w_kv=mem_w_kv, mem_q_norm=_tile4(mem_q_norm),
                mem_k_norm=_tile4(mem_k_norm), w_out=w_out)


def local_step(x, mem, tgt, p):
    bl, s, _ = x.shape
    x2, mem2, tgt2 = x.reshape(bl * s, D_MODEL), mem.reshape(bl * N_MEM, D_MODEL), tgt.reshape(bl * s, D_MODEL)
    y0, sv0 = layer_fwd(x2, mem2, p, 0, bl, s)
    y1, sv1 = layer_fwd(y0, mem2, p, 1, bl, s)
    lpart, dy = loss_head(y1, tgt2)
    dx1, g1 = layer_bwd(dy, mem2, p, sv1, 1, bl, s)
    dx0, g0 = layer_bwd(dx1, mem2, p, sv0, 0, bl, s)
    return lpart[0, 0], dx0.reshape(bl, s, D_MODEL), g0, g1


_ANY = pl.BlockSpec(memory_space=pl.ANY)


def all_gather_rows(xss, tag):
    n = len(xss)

    def body(*refs):
        x_refs, o_refs, (send_sems, recv_sems, local_sems) = refs[:n], refs[n:2 * n], refs[2 * n:]
        x, y, cc = lax.axis_index("x"), lax.axis_index("y"), lax.axis_index("c")
        me, sibling = (x, y, cc), (x, y, 1 - cc)
        chips = [(1 - x, y), (x, 1 - y), (1 - x, 1 - y)]

        def rows(a, px, py, pc):
            r = xss[a].shape[1]
            return o_refs[a].at[:, pl.ds((4 * px + 2 * py + pc) * r, r), :]

        def copy(a, k, block, to, src=None):
            return pltpu.make_async_remote_copy(src_ref=rows(a, *block) if src is None else src, dst_ref=rows(a, *block),
                                                send_sem=send_sems.at[7 * a + k], recv_sem=recv_sems.at[7 * a + k], device_id=to,
                                                device_id_type=pl.DeviceIdType.MESH)

        mine = [pltpu.make_async_copy(x_refs[a], rows(a, *me), local_sems.at[a]) for a in range(n)]
        first = []
        for a in range(n):
            first += [copy(a, 0, me, sibling, src=x_refs[a])] + [copy(a, 1 + j, me, (*chip, cc), src=x_refs[a])
                                                                 for j, chip in enumerate(chips)]
        for cp in mine + first:
            cp.start()
        passed = []
        for j, chip in enumerate(chips):
            for a in range(n):
                copy(a, 1 + j, (*chip, cc), me).wait_recv()
                passed.append(copy(a, 4 + j, (*chip, cc), sibling))
                passed[-1].start()
        for a in range(n):
            copy(a, 0, sibling, me).wait_recv()
        for j, chip in enumerate(chips):
            for a in range(n):
                copy(a, 4 + j, (*chip, 1 - cc), me).wait_recv()
        for cp in first + passed:
            cp.wait_send()
        for cp in mine:
            cp.wait()

    nsem = pltpu.SemaphoreType.DMA((7 * n,))
    return pl.pallas_call(
        body, name=f"all_gather_{tag}", in_specs=[_ANY] * n, out_specs=[_ANY] * n,
        out_shape=[_sds((xs.shape[0], N_DEV * xs.shape[1], xs.shape[2]), xs.dtype) for xs in xss],
        scratch_shapes=[nsem, nsem, pltpu.SemaphoreType.DMA((n,))],
    )(*xss)


def exchange_cores(parts, tag):
    n = len(parts)
    counts = [p.shape[0] * 4 for p in parts]

    def body(*refs):
        p_refs, t_refs, (send_sems, recv_sems) = refs[:n], refs[n:2 * n], refs[2 * n:]
        x, y, cc = lax.axis_index("x"), lax.axis_index("y"), lax.axis_index("c")
        copies, k = [], 0
        for a in range(n):
            for l in range(parts[a].shape[0]):
                for q in range(4):
                    copies.append(pltpu.make_async_remote_copy(
                        src_ref=p_refs[a].at[l, q, pl.ds(1 - cc, 1)], dst_ref=t_refs[a].at[l, q], send_sem=send_sems.at[k],
                        recv_sem=recv_sems.at[k], device_id=(x, y, 1 - cc), device_id_type=pl.DeviceIdType.MESH))
                    k += 1
        for cp in copies:
            cp.start()
        for cp in copies:
            cp.wait()

    nsem = pltpu.SemaphoreType.DMA((sum(counts),))
    return pl.pallas_call(
        body, name=f"exchange_cores_{tag}", in_specs=[_ANY] * n, out_specs=[_ANY] * n,
        out_shape=[_sds((p.shape[0], 4, 1, p.shape[3], p.shape[4]), p.dtype) for p in parts],
        scratch_shapes=[nsem, nsem],
    )(*parts)


def add_core_halves(part5, theirs, core, tag):
    nl, _, _, r, c = part5.shape
    tr = r if r * c <= 256 * 1024 else 64

    def body(core_ref, a_ref, b_ref, o_ref):
        o_ref[...] = _cast(a_ref[...] + b_ref[...])

    blk = lambda which: pl.BlockSpec((None, None, None, tr, c), lambda l, q, i, cref: (l, q, cref[0] if which else 0, i, 0))
    return pl.pallas_call(
        body, name=f"add_core_halves_{tag}", out_shape=_sds((nl, 4, 1, r, c), _MMT),
        grid_spec=pltpu.PrefetchScalarGridSpec(num_scalar_prefetch=1, grid=(nl, 4, r // tr), in_specs=[blk(True), blk(False)],
                                               out_specs=blk(False)),
        compiler_params=pltpu.CompilerParams(dimension_semantics=("arbitrary",) * 3, vmem_limit_bytes=VMEM_LIMIT_BYTES),
    )(core, part5, theirs)


def exchange_chips(s4s, tag):
    n = len(s4s)

    def body(*refs):
        s_refs, o_refs, (send_sems, recv_sems, local_sems) = refs[:n], refs[n:2 * n], refs[2 * n:]
        x, y, cc = lax.axis_index("x"), lax.axis_index("y"), lax.axis_index("c")
        copies = [pltpu.make_async_copy(s_refs[a].at[:, pl.ds(2 * x + y, 1)], o_refs[a].at[0], local_sems.at[a]) for a in range(n)]
        for k in range(1, 4):
            px = 1 - x if (k >> 1) & 1 else x
            py = 1 - y if k & 1 else y
            for a in range(n):
                copies.append(pltpu.make_async_remote_copy(
                    src_ref=s_refs[a].at[:, pl.ds(2 * px + py, 1)], dst_ref=o_refs[a].at[k], send_sem=send_sems.at[3 * a + k - 1],
                    recv_sem=recv_sems.at[3 * a + k - 1], device_id=(px, py, cc), device_id_type=pl.DeviceIdType.MESH))
        for cp in copies:
            cp.start()
        for cp in copies:
            cp.wait()

    nsem = pltpu.SemaphoreType.DMA((3 * n,))
    return pl.pallas_call(
        body, name=f"exchange_chips_{tag}", in_specs=[_ANY] * n, out_specs=[_ANY] * n,
        out_shape=[_sds((4, s.shape[0], 1, 1, s.shape[3], s.shape[4]), s.dtype) for s in s4s],
        scratch_shapes=[nsem, nsem, pltpu.SemaphoreType.DMA((n,))],
    )(*s4s)


def _row_tile(rows):
    if rows <= 512 and rows % 64:
        return rows
    for t in (64, 40, 32, 16, 8):
        if rows % t == 0:
            return t
    return rows


def sum_slots(slots, tag):
    ns, rows, c = slots.shape
    tr = _row_tile(rows)

    def body(s_ref, o_ref):
        acc = s_ref[0].astype(F32)
        for k in range(1, ns):
            acc = acc + s_ref[k].astype(F32)
        o_ref[...] = acc

    return _call(body, f"sum_slots_{tag}", (rows // tr,), [pl.BlockSpec((ns, tr, c), lambda i: (0, i, 0))],
                 pl.BlockSpec((tr, c), lambda i: (i, 0)), _sds((rows, c)))(slots)


def _adamw(w, g, m, v):
    m = ADAM_B1 * m + (1.0 - ADAM_B1) * g
    v = ADAM_B2 * v + (1.0 - ADAM_B2) * (g * g)
    m_hat = m / (1.0 - ADAM_B1 ** ADAM_STEP)
    v_hat = v / (1.0 - ADAM_B2 ** ADAM_STEP)
    delta = -ADAM_LR * (m_hat / (jnp.sqrt(v_hat) + ADAM_EPS) + ADAM_WD * w)
    return delta, m, v


def adam_update(w, m, v, g, tag, slots=False):
    rows, c = w.shape
    tr = _row_tile(rows)
    ns = g.shape[0] if slots else 0

    def body(w_ref, m_ref, v_ref, g_ref, go_ref, d_ref, mo_ref, vo_ref):
        if slots:
            g = g_ref[0].astype(F32)
            for k in range(1, ns):
                g = g + g_ref[k].astype(F32)
        else:
            g = g_ref[...]
        d, mn, vn = _adamw(w_ref[...], g, m_ref[...], v_ref[...])
        go_ref[...] = g
        d_ref[...] = d
        mo_ref[...] = mn
        vo_ref[...] = vn

    blk = pl.BlockSpec((tr, c), lambda i: (i, 0))
    gspec = pl.BlockSpec((ns, tr, c), lambda i: (0, i, 0)) if slots else blk
    return _call(body, f"adam_{tag}", (rows // tr,), [blk, blk, blk, gspec], [blk] * 4, [_sds((rows, c))] * 4)(w, m, v, g)


_SMALL = (("norm_g", (2, 1024)), ("fox_f_bias", (2, 4)), ("fox_q_norm", (2, 64)), ("fox_k_norm", (2, 64)),
          ("hgrn_lb_logits", (2, 256)), ("hgrn_out_norm", (2, 256)), ("pool_w", (2, 4, 64, 64)), ("pool_scale", (2, 256)),
          ("mem_norm_g", (2, 1024)), ("mem_q_norm", (2, 64)), ("mem_k_norm", (2, 64)))
_SLAB_ROWS = 312


def pack_small(d):
    flat = jnp.concatenate([d[n].reshape(-1) for n, _ in _SMALL])
    return jnp.pad(flat, (0, _SLAB_ROWS * 128 - flat.shape[0])).reshape(_SLAB_ROWS, 128)


def unpack_small(slab):
    flat, out, off = slab.reshape(-1), {}, 0
    for n, shp in _SMALL:
        size = 1
        for e in shp:
            size *= e
        out[n] = flat[off:off + size].reshape(shp)
        off += size
    return out


def small_grads(g0, g1, lb_logits):
    st = lambda f: jnp.stack([f(g0), f(g1)])
    heads = lambda a: a.reshape(NH, HD).sum(0)
    p1 = jax.nn.sigmoid(lb_logits[1] - lb_logits[0])
    inside = (p1 > 0.0) & (p1 < 1.0 - 1e-6)
    dl1 = jnp.where(inside, g1["lb"][0] * p1 * (1.0 - p1), 0.0)
    diag = lambda a: jnp.stack([a.reshape(4, HD, 4, HD)[i, :, i, :] for i in range(4)])
    return dict(norm_g=st(lambda g: g["norm_g"][0]), fox_f_bias=st(lambda g: g["fox_f_bias"][0, :NH]),
                fox_q_norm=st(lambda g: heads(g["fox_q_norm"])), fox_k_norm=st(lambda g: heads(g["fox_k_norm"])),
                hgrn_lb_logits=jnp.stack([-dl1, dl1]), hgrn_out_norm=st(lambda g: g["hgrn_out_norm"][0]),
                pool_w=st(lambda g: diag(g["pool_wbd"])), pool_scale=st(lambda g: g["pool_scale"][0]),
                mem_norm_g=st(lambda g: g["mem_norm_g"][0]), mem_q_norm=st(lambda g: heads(g["mem_q_norm"])),
                mem_k_norm=st(lambda g: heads(g["mem_k_norm"])))


def kernel(x, mem, norm_g, w_in, fox_f_bias, fox_q_norm, fox_k_norm, hgrn_lb_logits, hgrn_out_norm, pool_w, pool_scale, mem_norm_g, mem_w_kv, mem_q_norm, mem_k_norm, w_out, loss_target, m_norm_g, m_w_in, m_fox_f_bias, m_fox_q_norm, m_fox_k_norm, m_hgrn_lb_logits, m_hgrn_out_norm, m_pool_w, m_pool_scale, m_mem_norm_g, m_mem_w_kv, m_mem_q_norm, m_mem_k_norm, m_w_out, v_norm_g, v_w_in, v_fox_f_bias, v_fox_q_norm, v_fox_k_norm, v_hgrn_lb_logits, v_hgrn_out_norm, v_pool_w, v_pool_scale, v_mem_norm_g, v_mem_w_kv, v_mem_q_norm, v_mem_k_norm, v_w_out):
    given = dict(norm_g=(norm_g, m_norm_g, v_norm_g), w_in=(w_in, m_w_in, v_w_in), fox_f_bias=(fox_f_bias, m_fox_f_bias, v_fox_f_bias),
                 fox_q_norm=(fox_q_norm, m_fox_q_norm, v_fox_q_norm), fox_k_norm=(fox_k_norm, m_fox_k_norm, v_fox_k_norm),
                 hgrn_lb_logits=(hgrn_lb_logits, m_hgrn_lb_logits, v_hgrn_lb_logits),
                 hgrn_out_norm=(hgrn_out_norm, m_hgrn_out_norm, v_hgrn_out_norm), pool_w=(pool_w, m_pool_w, v_pool_w),
                 pool_scale=(pool_scale, m_pool_scale, v_pool_scale), mem_norm_g=(mem_norm_g, m_mem_norm_g, v_mem_norm_g),
                 mem_w_kv=(mem_w_kv, m_mem_w_kv, v_mem_w_kv), mem_q_norm=(mem_q_norm, m_mem_q_norm, v_mem_q_norm),
                 mem_k_norm=(mem_k_norm, m_mem_k_norm, v_mem_k_norm), w_out=(w_out, m_w_out, v_w_out))
    order = ("norm_g", "w_in", "fox_f_bias", "fox_q_norm", "fox_k_norm", "hgrn_lb_logits", "hgrn_out_norm", "pool_w",
             "pool_scale", "mem_norm_g", "mem_w_kv", "mem_q_norm", "mem_k_norm", "w_out")

    w_in_full, w_out_full, w_kv_full = all_gather_rows([_cast(permute_cols(w_in)), _cast(w_out), _cast(mem_w_kv)], "weights")
    p = prepare_params(norm_g, w_in_full, fox_f_bias, fox_q_norm, fox_k_norm, hgrn_lb_logits, hgrn_out_norm, pool_w,
                       pool_scale, mem_norm_g, w_kv_full, mem_q_norm, mem_k_norm, w_out_full)

    loss_part, grad_x, g0, g1 = local_step(x, mem, loss_target, p)
    loss = lax.psum(loss_part, ("x", "y", "c"))

    res = {}
    core = lax.axis_index("c").astype(jnp.int32).reshape(1)

    names = ("w_in", "w_out", "mem_w_kv")
    part5 = []
    for name in names:
        nl, r, _ = given[name][0].shape
        g2 = jnp.stack([g0[name], g1[name]])
        part5.append(g2.reshape(nl, 4, 2, r, g2.shape[-1]))
    s4 = [add_core_halves(p5, th, core, name) for name, p5, th in zip(names, part5, exchange_cores(part5, "grads"))]
    for name, sl in zip(names, exchange_chips(s4, "grads")):
        w, m, v = given[name]
        nl, r, c = w.shape
        slots = sl.reshape(4, nl * r, sl.shape[-1])
        if name == "w_in":
            g = sum_slots(slots, name).reshape(nl, r, -1)
            out = adam_update(w.reshape(nl * r, c), m.reshape(nl * r, c), v.reshape(nl * r, c),
                              unpermute_cols(g).reshape(nl * r, c), name)
        else:
            out = adam_update(w.reshape(nl * r, c), m.reshape(nl * r, c), v.reshape(nl * r, c), slots, name, slots=True)
        res[name] = tuple(o.reshape(nl, r, c) for o in out)

    gsmall = pack_small(small_grads(g0, g1, hgrn_lb_logits))
    gathered = all_gather_rows([gsmall[None]], "small")[0].reshape(N_DEV, _SLAB_ROWS, 128)
    slabs = adam_update(*[pack_small({n: given[n][j] for n, _ in _SMALL}) for j in range(3)], gathered, "small", slots=True)
    small = [unpack_small(sl) for sl in slabs]
    for n, _ in _SMALL:
        res[n] = tuple(small[j][n] for j in range(4))

    return (loss, grad_x, *[res[n][0] for n in order], *[res[n][1] for n in order], *[res[n][2] for n in order],
            *[res[n][3] for n in order])
```

```python
import functools

import jax
import jax.numpy as jnp
from jax import lax
from jax.experimental import pallas as pl
from jax.experimental.pallas import tpu as pltpu

F32 = jnp.float32
BF = jnp.bfloat16
_MMT = BF

D_MODEL = 1024
GW = 256
HD = 64
NH = 4
CH = 64
N_MEM = 256
D_IN = 4100
D_INP = 4224
D_MIX = 1280
EPS = 1e-6
NEG_BIG = -1e30
LB_FLOOR = 1e-30
SCALE = HD ** -0.5
TQ = 256
TM = 512


def _token_tile(n):
    return TM if n % TM == 0 else TQ
N_DEV = 8
VMEM_LIMIT_BYTES = 56 * 1024 * 1024

ADAM_LR = 0.001
ADAM_B1 = 0.9
ADAM_B2 = 0.999
ADAM_EPS = 1e-08
ADAM_WD = 0.01
ADAM_STEP = 10

PIECES = (("A", 0, 768), ("B", 768, 768), ("C", 1536, 768), ("D", 2304, 256), ("E", 2560, 256),
          ("G", 2816, 1280), ("F", 4096, 128))
BWD_PIECES = (("A", 0, 768), ("Bq", 768, 256), ("Bk", 1024, 256), ("Bv", 1280, 256), ("C", 1536, 768),
              ("D", 2304, 256), ("E", 2560, 256), ("G", 2816, 1280), ("F", 4096, 128))
_ORIG = dict(fq=(0, 256), fk=(256, 512), fv=(512, 768), fg=(768, 1024), ff=(1024, 1028), sq=(1028, 1284),
             sk=(1284, 1540), sv=(1540, 1796), sg=(1796, 2052), hq=(2052, 2308), hf=(2308, 2564),
             hi=(2564, 2820), hg=(2820, 3076), pv=(3076, 3332), pg=(3332, 3588), mq=(3588, 3844), mg=(3844, 4100))
_PERM_ORDER = ("fq", "fk", "fv", "sq", "sk", "sv", "hq", "hf", "hi", "pv", "mq", "fg", "sg", "hg", "pg", "mg", "ff")
_ORIG_ORDER = ("fq", "fk", "fv", "fg", "ff", "sq", "sk", "sv", "sg", "hq", "hf", "hi", "hg", "pv", "pg", "mq", "mg")


def permute_cols(w):
    parts = [w[..., _ORIG[n][0]:_ORIG[n][1]] for n in _PERM_ORDER]
    parts.append(jnp.zeros(w.shape[:-1] + (D_INP - D_IN,), w.dtype))
    return jnp.concatenate(parts, axis=-1)


def unpermute_cols(g):
    start, off = {}, 0
    for n in _PERM_ORDER:
        start[n] = off
        off += _ORIG[n][1] - _ORIG[n][0]
    return jnp.concatenate([g[..., start[n]:start[n] + _ORIG[n][1] - _ORIG[n][0]] for n in _ORIG_ORDER], axis=-1)


def _cast(a):
    return a.astype(_MMT)


def _dg(a, b, ca, cb):
    return lax.dot_general(a, b, (((ca,), (cb,)), ((), ())), preferred_element_type=F32)


@jax.custom_vjp
def mm(a, b):
    return _dg(_cast(a), _cast(b), 1, 0)


@jax.custom_vjp
def mm_nt(a, b):
    return _dg(_cast(a), _cast(b), 1, 1)


@jax.custom_vjp
def mm_tn(a, b):
    return _dg(_cast(a), _cast(b), 0, 0)


mm.defvjp(lambda a, b: (mm(a, b), (a, b)),
          lambda r, g: (mm_nt(g, r[1]).astype(r[0].dtype), mm_tn(r[0], g).astype(r[1].dtype)))
mm_nt.defvjp(lambda a, b: (mm_nt(a, b), (a, b)),
             lambda r, g: (mm(g, r[1]).astype(r[0].dtype), mm_tn(g, r[0]).astype(r[1].dtype)))
mm_tn.defvjp(lambda a, b: (mm_tn(a, b), (a, b)),
             lambda r, g: (mm_nt(r[1], g).astype(r[0].dtype), mm(r[0], g).astype(r[1].dtype)))


def _split(a):
    hi = a.astype(_MMT)
    lo = (a - hi.astype(F32)).astype(_MMT)
    return hi, lo


@jax.custom_vjp
def xr(a, c, ct):
    hi, lo = _split(a)
    cc = _cast(c)
    return _dg(hi, cc, 1, 0) + _dg(lo, cc, 1, 0)


@jax.custom_vjp
def xl(c, ct, a):
    hi, lo = _split(a)
    cc = _cast(c)
    return _dg(cc, hi, 1, 0) + _dg(cc, lo, 1, 0)


xr.defvjp(lambda a, c, ct: (xr(a, c, ct), (c, ct)),
          lambda r, g: (xr(g, r[1], r[0]), jnp.zeros_like(r[0]), jnp.zeros_like(r[1])))
xl.defvjp(lambda c, ct, a: (xl(c, ct, a), (c, ct)),
          lambda r, g: (jnp.zeros_like(r[0]), jnp.zeros_like(r[1]), xl(r[1], r[0], g)))


def _iota(shape, dim):
    return lax.broadcasted_iota(jnp.int32, shape, dim)


def _hmask(h, n=GW):
    lane = _iota((1, n), 1)
    return ((lane >= h * HD) & (lane < (h + 1) * HD)).astype(F32)


def _bdmask(n=GW):
    return ((_iota((n, n), 0) >> 6) == (_iota((n, n), 1) >> 6)).astype(F32)


def _tri(n, kind="le"):
    r, c = _iota((n, n), 0), _iota((n, n), 1)
    return {"le": c <= r, "ge": c >= r, "gt": c > r, "lt": c < r}[kind].astype(F32)


def _onehot_lane(h, n=128):
    return (_iota((1, n), 1) == h).astype(F32)


def _logsig(x):
    return jnp.minimum(x, 0.0) - jnp.log1p(jnp.exp(-jnp.abs(x)))


def _sigmoid(x):
    return 0.5 * (jnp.tanh(0.5 * x) + 1.0)


def _silu(x):
    return x * _sigmoid(x)


def _rms(x, g):
    return x * lax.rsqrt(jnp.mean(x * x, axis=-1, keepdims=True) + EPS) * g


def _headrms(x, w, bd64):
    ms = xr(x * x, bd64, bd64)
    return x * lax.rsqrt(ms + EPS) * w


def _call(body, name, grid, in_specs, out_specs, out_shape, scratch=()):
    return pl.pallas_call(
        body, name=name, grid=grid, in_specs=in_specs, out_specs=out_specs, out_shape=out_shape,
        scratch_shapes=list(scratch),
        compiler_params=pltpu.CompilerParams(dimension_semantics=("arbitrary",) * len(grid),
                                             vmem_limit_bytes=VMEM_LIMIT_BYTES))


def _sds(shape, dtype=F32):
    return jax.ShapeDtypeStruct(shape, dtype)


def _acc(ref, val, first):
    @pl.when(first)
    def _():
        ref[...] = val

    @pl.when(jnp.logical_not(first))
    def _():
        ref[...] += val


def inproj_fwd(x2, g, w, layer, tag):
    t = x2.shape[0]
    TQ = _token_tile(t)

    def body(x_ref, g_ref, w_ref, ht_ref, *outs):
        h = _rms(x_ref[...], g_ref[...])
        hb = _cast(h)
        ht_ref[...] = _cast(h.T)
        for (_, c0, wd), o in zip(PIECES, outs):
            o[...] = _dg(hb, _cast(w_ref[:, c0:c0 + wd]), 1, 0)

    return _call(
        body, f"inproj_fwd_{tag}", (t // TQ,),
        [pl.BlockSpec((TQ, D_MODEL), lambda i: (i, 0)),
         pl.BlockSpec((None, 1, D_MODEL), lambda i: (layer, 0, 0)),
         pl.BlockSpec((None, D_MODEL, D_INP), lambda i: (layer, 0, 0))],
        [pl.BlockSpec((D_MODEL, TQ), lambda i: (0, i))] + [pl.BlockSpec((TQ, wd), lambda i: (i, 0)) for _, _, wd in PIECES],
        [_sds((D_MODEL, t), _MMT)] + [_sds((t, wd)) for _, _, wd in PIECES],
    )(x2, g, w)


def inproj_bwd_dx(x2, g, w, dy, dpieces, layer, tag):
    t = x2.shape[0]
    TQ = _token_tile(t)

    def body(x_ref, g_ref, w_ref, dy_ref, *rest):
        dps, (dx_ref, dg_ref) = rest[:len(BWD_PIECES)], rest[len(BWD_PIECES):]
        dh = None
        for (_, c0, wd), dp in zip(BWD_PIECES, dps):
            part = _dg(_cast(dp[...]), _cast(w_ref[:, c0:c0 + wd]), 1, 1)
            dh = part if dh is None else dh + part
        _, vjp = jax.vjp(_rms, x_ref[...], g_ref[...])
        dx, dg = vjp(dh)
        dx_ref[...] = dy_ref[...] + dx
        _acc(dg_ref, dg, pl.program_id(0) == 0)

    return _call(
        body, f"inproj_bwd_dx_{tag}", (t // TQ,),
        [pl.BlockSpec((TQ, D_MODEL), lambda i: (i, 0)),
         pl.BlockSpec((None, 1, D_MODEL), lambda i: (layer, 0, 0)),
         pl.BlockSpec((None, D_MODEL, D_INP), lambda i: (layer, 0, 0), pipeline_mode=pl.Buffered(1)),
         pl.BlockSpec((TQ, D_MODEL), lambda i: (i, 0))] + [pl.BlockSpec((TQ, wd), lambda i: (i, 0)) for _, _, wd in BWD_PIECES],
        [pl.BlockSpec((TQ, D_MODEL), lambda i: (i, 0)), pl.BlockSpec((1, D_MODEL), lambda i: (0, 0))],
        [_sds((t, D_MODEL)), _sds((1, D_MODEL))],
    )(x2, g, w, dy, *dpieces)


def matmul_acc(at, bs, tag):
    m, t = at.shape
    widths = [b.shape[1] for b in bs]
    tk = 2048 if t % 2048 == 0 else (512 if t % 512 == 0 else TQ)

    def body(a_ref, *refs):
        b_refs, o_ref = refs[:-1], refs[-1]
        a = _cast(a_ref[...])
        first, off = pl.program_id(0) == 0, 0
        for b_ref, n in zip(b_refs, widths):
            part = _dg(a, _cast(b_ref[...]), 1, 0)

            @pl.when(first)
            def _(part=part, off=off, n=n):
                o_ref[:, off:off + n] = part

            @pl.when(jnp.logical_not(first))
            def _(part=part, off=off, n=n):
                o_ref[:, off:off + n] += part

            off += n

    return _call(
        body, f"matmul_acc_{tag}", (t // tk,),
        [pl.BlockSpec((m, tk), lambda i: (0, i))] + [pl.BlockSpec((tk, n), lambda i: (i, 0)) for n in widths],
        pl.BlockSpec((m, sum(widths)), lambda i: (0, 0)),
        _sds((m, sum(widths))),
    )(at, *bs)


def _fox_prep_fn(q, k, ff, qw, kw, bias, carry, bd64, tri, trit, last):
    qn = _headrms(q, qw, bd64)
    kn = _headrms(k, kw, bd64)
    lf = _logsig(ff + bias)
    c = xl(tri, trit, lf) + carry
    return qn, kn, c, jnp.sum(c * last, axis=0, keepdims=True)


def _prep_consts(tq):
    return _bdmask() * (1.0 / HD), _tri(tq), _tri(tq, "ge"), (_iota((tq, 1), 0) == tq - 1).astype(F32)


def fox_prep_fwd(pa, pf, qw, kw, bias, bl, s, layer, tag):
    TQ = _token_tile(s)
    nq = s // TQ

    def body(q_ref, k_ref, v_ref, f_ref, qw_ref, kw_ref, b_ref, qn_ref, kn_ref, vb_ref, cq_ref, ck_ref, carry):
        @pl.when(pl.program_id(1) == 0)
        def _():
            carry[...] = jnp.zeros_like(carry)

        qn, kn, c, cl = _fox_prep_fn(q_ref[...], k_ref[...], f_ref[...], qw_ref[...], kw_ref[...], b_ref[...],
                                     carry[...], *_prep_consts(TQ))
        carry[...] = cl
        qn_ref[...] = _cast(qn)
        kn_ref[...] = _cast(kn)
        vb_ref[...] = _cast(v_ref[...])
        cq_ref[...] = c
        ck_ref[...] = c.T[0:8, :]

    tok = lambda j: pl.BlockSpec((TQ, GW), lambda b, i: (b * nq + i, j))
    par = lambda n: pl.BlockSpec((None, 1, n), lambda b, i: (layer, 0, 0))
    return _call(
        body, f"fox_prep_fwd_{tag}", (bl, nq),
        [tok(0), tok(1), tok(2), pl.BlockSpec((TQ, 128), lambda b, i: (b * nq + i, 0)), par(GW), par(GW), par(128)],
        [tok(0), tok(0), tok(0), pl.BlockSpec((TQ, 128), lambda b, i: (b * nq + i, 0)),
         pl.BlockSpec((None, 8, TQ), lambda b, i: (b, 0, i))],
        [_sds((bl * s, GW), _MMT)] * 3 + [_sds((bl * s, 128)), _sds((bl, 8, s))],
        [pltpu.VMEM((1, 128), F32)],
    )(pa, pa, pa, pf, qw, kw, bias)


def fox_prep_bwd(pa, pf, qw, kw, bias, cq, dqn, dkn, dv, dck, bl, s, layer, tag):
    TQ = _token_tile(s)
    nq = s // TQ

    def body(q_ref, k_ref, f_ref, qw_ref, kw_ref, b_ref, cq_ref, cprev_ref, dqn_ref, dkn_ref, dv_ref, dck_ref,
             da_ref, df_ref, dqw_ref, dkw_ref, db_ref, dcarry):
        i = pl.program_id(1)
        first = jnp.logical_and(pl.program_id(0) == 0, i == 0)

        @pl.when(i == 0)
        def _():
            dcarry[...] = jnp.zeros_like(dcarry)

        last = (_iota((TQ, 1), 0) == TQ - 1).astype(F32)
        carry_in = jnp.where(i == nq - 1, 0.0, jnp.sum(cprev_ref[...] * last, axis=0, keepdims=True))
        consts = _prep_consts(TQ)
        _, vjp = jax.vjp(lambda *a: _fox_prep_fn(*a, *consts), q_ref[...], k_ref[...], f_ref[...], qw_ref[...],
                         kw_ref[...], b_ref[...], carry_in)
        dc = dck_ref[...].T
        dq, dk, dff, dqw, dkw, dbias, dcin = vjp((dqn_ref[...], dkn_ref[...], dc, dcarry[...]))
        dcarry[...] = dcin
        da_ref[:, 0:GW] = _cast(dq)
        da_ref[:, GW:2 * GW] = _cast(dk)
        da_ref[:, 2 * GW:3 * GW] = _cast(dv_ref[...])
        df_ref[...] = _cast(dff)
        _acc(dqw_ref, dqw, first)
        _acc(dkw_ref, dkw, first)
        _acc(db_ref, dbias, first)

    rv = lambda b, i: b * nq + (nq - 1 - i)
    tok = lambda j: pl.BlockSpec((TQ, GW), lambda b, i: (rv(b, i), j))
    tok0 = pl.BlockSpec((TQ, GW), lambda b, i: (rv(b, i), 0))
    t128 = pl.BlockSpec((TQ, 128), lambda b, i: (rv(b, i), 0))
    prev = pl.BlockSpec((TQ, 128), lambda b, i: (jnp.maximum(rv(b, i) - 1, 0), 0))
    par = lambda n: pl.BlockSpec((None, 1, n), lambda b, i: (layer, 0, 0))
    acc = lambda n: pl.BlockSpec((1, n), lambda b, i: (0, 0))
    return _call(
        body, f"fox_prep_bwd_{tag}", (bl, nq),
        [tok(0), tok(1), t128, par(GW), par(GW), par(128), t128, prev, tok0, tok0, tok0,
         pl.BlockSpec((None, 128, TQ), lambda b, i: (b, 0, nq - 1 - i))],
        [pl.BlockSpec((TQ, 3 * GW), lambda b, i: (rv(b, i), 0)), t128, acc(GW), acc(GW), acc(128)],
        [_sds((bl * s, 3 * GW), _MMT), _sds((bl * s, 128), _MMT), _sds((1, GW)), _sds((1, GW)), _sds((1, 128))],
        [pltpu.VMEM((1, 128), F32)],
    )(pa, pa, pf, qw, kw, bias, cq, cq, dqn, dkn, dv, dck)


def _lane_pick(x, h):
    return jnp.sum(x * _onehot_lane(h), axis=-1, keepdims=True)


TA_BIG = 256
TA_FWD = 512
TK_FOX = 512
TK_SB = 256


def _stack_heads(x, scale=1.0):
    return _cast(jnp.concatenate([x * (_hmask(h) * scale) for h in range(NH)], axis=0))


def _stack_cols(x):
    return jnp.concatenate([_lane_pick(x, h) for h in range(NH)], axis=0)


def _spread_heads(col):
    ta = col.shape[0] // NH
    return sum(col[h * ta:(h + 1) * ta] * _hmask(h) for h in range(NH))


def _lanes_cat(w):
    ta = w.shape[0] // NH
    return jnp.concatenate([w[h * ta:(h + 1) * ta] for h in range(NH)], axis=1)


def _mask_stack(x):
    return _cast(jnp.concatenate([x * _hmask(h).astype(x.dtype) for h in range(NH)], axis=0))


def _stack_rows(i, ta):
    return i * ta + (_iota((NH * ta, 1), 0) & (ta - 1))


def _n_key_tiles(i, tk, ta):
    assert ta <= tk and tk % ta == 0, "a query tile's diagonal must lie inside one key tile (only the last key tile is masked)"
    return lax.shift_right_logical(i * ta, tk.bit_length() - 1) + 1


def fox_attn_fwd(qn, kn, vb, cq, ck, bl, s, tag):
    TA, TK = min(TA_FWD, s), min(TK_FOX, s)
    nq, SROWS = s // TA, NH * TA

    def body(q_ref, k_ref, v_ref, cq_ref, ck_ref, o_ref, lse_ref, acc, vst):
        i = pl.program_id(1)

        @pl.when(i == 0)
        def _():
            _fill_stacked(vst, v_ref, s, TK)

        qs = _stack_heads(q_ref[...].astype(F32), SCALE)
        cqs = _stack_cols(cq_ref[...])
        row = _stack_rows(i, TA)
        acc[...] = jnp.zeros_like(acc)

        def step(j, ml):
            m, l = ml
            ks = pl.ds(pl.multiple_of(j * TK, TK), TK)
            ckb = jnp.concatenate([jnp.broadcast_to(ck_ref[h:h + 1, ks], (TA, TK)) for h in range(NH)], axis=0)
            sc = _dg(qs, k_ref[ks, :], 1, 1) + cqs - ckb
            col = j * TK + _iota((1, TK), 1)
            sc = jnp.where(col <= row, sc, NEG_BIG)
            m_new = jnp.maximum(m, jnp.max(sc, axis=-1, keepdims=True))
            alpha = jnp.exp(m - m_new)
            p = jnp.exp(sc - m_new)
            vs = vst[pl.ds(pl.multiple_of(j * NH * TK, NH * TK), NH * TK), :]
            acc[...] = _spread_heads(alpha) * acc[...] + _dg(_lanes_cat(_cast(p)), vs, 1, 0)
            return m_new, alpha * l + jnp.sum(p, axis=-1, keepdims=True)

        m, l = lax.fori_loop(0, _n_key_tiles(i, TK, TA), step, (jnp.full((SROWS, 1), NEG_BIG, F32), jnp.zeros((SROWS, 1), F32)))
        o_ref[...] = acc[...] / _spread_heads(l)
        lse_h = m + jnp.log(l)
        lse_ref[...] = sum(lse_h[h * TA:(h + 1) * TA] * _onehot_lane(h) for h in range(NH))

    tok = pl.BlockSpec((TA, GW), lambda b, i: (b * nq + i, 0))
    seq = pl.BlockSpec((s, GW), lambda b, i: (b, 0))
    t128 = pl.BlockSpec((TA, 128), lambda b, i: (b * nq + i, 0))
    return _call(
        body, f"fox_attn_fwd_{tag}", (bl, nq),
        [tok, seq, seq, t128, pl.BlockSpec((None, 8, s), lambda b, i: (b, 0, 0))],
        [tok, t128], [_sds((bl * s, GW)), _sds((bl * s, 128))],
        [pltpu.VMEM((TA, GW), F32), pltpu.VMEM((NH * s, GW), _MMT)],
    )(qn, kn, vb, cq, ck)


def fox_attn_bwd(qn, kn, vb, cq, ck, lse, do, bl, s, tag):
    TA, TK = min(TA_BIG, s), min(TK_FOX, s)
    nq, SROWS = s // TA, NH * TA

    def body(q_ref, k_ref, v_ref, cq_ref, ck_ref, lse_ref, do_ref, dq_ref, dk_ref, dv_ref, dck_ref, dqa, p_s, dp_s, kst):
        i = pl.program_id(1)

        @pl.when(i == 0)
        def _():
            dk_ref[...] = jnp.zeros_like(dk_ref)
            dv_ref[...] = jnp.zeros_like(dv_ref)
            dck_ref[...] = jnp.zeros_like(dck_ref)
            _fill_stacked(kst, k_ref, s, TK)

        qs = _stack_heads(q_ref[...].astype(F32), SCALE)
        dos = _stack_heads(do_ref[...])
        cqs, lses = _stack_cols(cq_ref[...]), _stack_cols(lse_ref[...])
        row = _stack_rows(i, TA)
        dqa[...] = jnp.zeros_like(dqa)
        nk = _n_key_tiles(i, TK, TA)

        def probs(j, delta):
            ks = pl.ds(pl.multiple_of(j * TK, TK), TK)
            ckb = jnp.concatenate([jnp.broadcast_to(ck_ref[h:h + 1, ks], (TA, TK)) for h in range(NH)], axis=0)
            sc = _dg(qs, k_ref[ks, :], 1, 1) + cqs - ckb
            col = j * TK + _iota((1, TK), 1)
            p = jnp.where(col <= row, jnp.exp(sc - lses), 0.0)
            dp = _dg(dos, v_ref[ks, :], 1, 1)
            p_s[:, ks] = p
            dp_s[:, ks] = dp
            return delta + jnp.sum(p * dp, axis=-1, keepdims=True)

        delta = lax.fori_loop(0, nk, probs, jnp.zeros((SROWS, 1), F32))

        def step(j, carry):
            ks = pl.ds(pl.multiple_of(j * TK, TK), TK)
            p = p_s[:, ks]
            ds = p * (dp_s[:, ks] - delta)
            dsb = _cast(ds)
            dqa[...] += _dg(_lanes_cat(dsb), kst[pl.ds(pl.multiple_of(j * NH * TK, NH * TK), NH * TK), :], 1, 0) * SCALE
            dk_ref[ks, :] += _dg(dsb, qs, 0, 0)
            dv_ref[ks, :] += _dg(_cast(p), dos, 0, 0)
            for h in range(NH):
                dck_ref[h:h + 1, ks] -= jnp.sum(ds[h * TA:(h + 1) * TA], axis=0, keepdims=True)
            return carry

        lax.fori_loop(0, nk, step, 0)
        dq_ref[...] = dqa[...]

    tok = pl.BlockSpec((TA, GW), lambda b, i: (b * nq + i, 0))
    seq = pl.BlockSpec((s, GW), lambda b, i: (b, 0))
    t128 = pl.BlockSpec((TA, 128), lambda b, i: (b * nq + i, 0))
    return _call(
        body, f"fox_attn_bwd_{tag}", (bl, nq),
        [tok, seq, seq, t128, pl.BlockSpec((None, 8, s), lambda b, i: (b, 0, 0)), t128, tok],
        [tok, seq, seq, pl.BlockSpec((None, 128, s), lambda b, i: (b, 0, 0))],
        [_sds((bl * s, GW)), _sds((bl * s, GW)), _sds((bl * s, GW)), _sds((bl, 128, s))],
        [pltpu.VMEM((TA, GW), F32), pltpu.VMEM((SROWS, s), F32), pltpu.VMEM((SROWS, s), F32), pltpu.VMEM((NH * s, GW), _MMT)],
    )(qn, kn, vb, cq, ck, lse, do)


def _sb_block(qh, kb, valid, upper, r_carry):
    z = _dg(qh, kb, 1, 1)
    ls = _logsig(z)
    lom = ls - z if valid is None else jnp.where(valid, ls - z, 0.0)
    between = xr(lom, upper, upper) + r_carry
    w = jnp.exp(ls + between)
    return ls, lom, (w if valid is None else jnp.where(valid, w, 0.0))


def _fill_stacked(dst, src_ref, s, tk):
    for j in range(s // tk):
        dst[j * NH * tk:(j + 1) * NH * tk, :] = _mask_stack(src_ref[j * tk:(j + 1) * tk, :])


def sb_attn_fwd(pb, bl, s, tag):
    TA, TK = TA_BIG, TK_SB
    nq, SROWS = s // TA, NH * TA

    def body(q_ref, k_ref, v_ref, o_ref, acc, vst):
        i = pl.program_id(1)

        @pl.when(i == 0)
        def _():
            _fill_stacked(vst, v_ref, s, TK)

        qs = _stack_heads(q_ref[...], SCALE)
        upper = _tri(TK, "lt")
        last = _n_key_tiles(i, TK, TA) - 1

        def step(j, r, valid):
            ks = pl.ds(pl.multiple_of(j * TK, TK), TK)
            _, lom, w = _sb_block(qs, _cast(k_ref[ks, :]), valid, upper, r)
            acc[...] += _dg(_lanes_cat(_cast(w)), vst[pl.ds(pl.multiple_of(j * NH * TK, NH * TK), NH * TK), :], 1, 0)
            return r + jnp.sum(lom, axis=-1, keepdims=True)

        acc[...] = jnp.zeros_like(acc)
        r = step(last, jnp.zeros((SROWS, 1), F32), last * TK + _iota((1, TK), 1) < _stack_rows(i, TA))
        lax.fori_loop(0, last, lambda jj, r: step(last - 1 - jj, r, None), r)
        o_ref[...] = acc[...]

    tok = lambda j: pl.BlockSpec((TA, GW), lambda b, i: (b * nq + i, j))
    seq = lambda j: pl.BlockSpec((s, GW), lambda b, i: (b, j))
    return _call(
        body, f"sb_attn_fwd_{tag}", (bl, nq), [tok(0), seq(1), seq(2)],
        pl.BlockSpec((TA, GW), lambda b, i: (b * nq + i, 0)), _sds((bl * s, GW)),
        [pltpu.VMEM((TA, GW), F32), pltpu.VMEM((NH * s, GW), _MMT)],
    )(pb, pb, pb)


def sb_attn_bwd(pb, do, bl, s, tag):
    TA, TK = TA_BIG, TK_SB
    nq, SROWS = s // TA, NH * TA

    def body(q_ref, k_ref, v_ref, do_ref, dq_ref, dk_ref, dv_ref, dqa, sig_s, nsig_s, w_s, g_s, kst):
        i = pl.program_id(1)

        @pl.when(i == 0)
        def _():
            dk_ref[...] = jnp.zeros_like(dk_ref)
            dv_ref[...] = jnp.zeros_like(dv_ref)
            _fill_stacked(kst, k_ref, s, TK)

        qs = _stack_heads(q_ref[...], SCALE)
        dos = _stack_heads(do_ref[...])
        upper = _tri(TK, "lt")
        before = _tri(TK, "gt")
        dqa[...] = jnp.zeros_like(dqa)
        last = _n_key_tiles(i, TK, TA) - 1
        diag = last * TK + _iota((1, TK), 1) < _stack_rows(i, TA)

        def weights(j, r, valid):
            ks = pl.ds(pl.multiple_of(j * TK, TK), TK)
            ls, lom, w = _sb_block(qs, _cast(k_ref[ks, :]), valid, upper, r)
            sig_s[:, ks] = _cast(jnp.exp(ls))
            nsig_s[:, ks] = _cast(jnp.exp(lom))
            w_s[:, ks] = _cast(w)
            g_s[:, ks] = _dg(dos, _cast(v_ref[ks, :]), 1, 1) * w
            return r + jnp.sum(lom, axis=-1, keepdims=True)

        r = weights(last, jnp.zeros((SROWS, 1), F32), diag)
        lax.fori_loop(0, last, lambda jj, r: weights(last - 1 - jj, r, None), r)

        def step(j, cpre, valid):
            ks = pl.ds(pl.multiple_of(j * TK, TK), TK)
            g = g_s[:, ks]
            pre = cpre + xr(g, before, before)
            dz = g * nsig_s[:, ks].astype(F32) - sig_s[:, ks].astype(F32) * pre
            dzb = _cast(dz if valid is None else jnp.where(valid, dz, 0.0))
            dqa[...] += _dg(_lanes_cat(dzb), kst[pl.ds(pl.multiple_of(j * NH * TK, NH * TK), NH * TK), :], 1, 0) * SCALE
            dk_ref[ks, :] += _dg(dzb, qs, 0, 0)
            dv_ref[ks, :] += _dg(w_s[:, ks], dos, 0, 0)
            return cpre + jnp.sum(g, axis=-1, keepdims=True)

        cpre = lax.fori_loop(0, last, lambda j, c: step(j, c, None), jnp.zeros((SROWS, 1), F32))
        step(last, cpre, diag)
        dq_ref[...] = dqa[...]

    tok = lambda j: pl.BlockSpec((TA, GW), lambda b, i: (b * nq + i, j))
    seq = lambda j: pl.BlockSpec((s, GW), lambda b, i: (b, j))
    return _call(
        body, f"sb_attn_bwd_{tag}", (bl, nq), [tok(0), seq(1), seq(2), tok(0)],
        [tok(0), seq(0), seq(0)], [_sds((bl * s, GW))] * 3,
        [pltpu.VMEM((TA, GW), F32), pltpu.VMEM((SROWS, s), _MMT), pltpu.VMEM((SROWS, s), _MMT),
         pltpu.VMEM((SROWS, s), _MMT), pltpu.VMEM((SROWS, s), F32), pltpu.VMEM((NH * s, GW), _MMT)],
    )(pb, pb, pb, do)


def _hgrn_consts():
    r, c = _iota((CH, CH), 0), _iota((CH, CH), 1)
    rr = _iota((CH, 1), 0)
    tri = (c <= r).astype(F32)
    lv = []
    for m in (8, 4, 2, 1):
        up = ((rr & (2 * m - 1)) >= m).astype(F32)
        selq = (((r & (2 * m - 1)) >= m) & (c == (r & ~(m - 1)) - 1)).astype(F32)
        selk = (((r & (2 * m - 1)) < m) & (c == (r & ~(m - 1)) + m - 1)).astype(F32)
        pm = (((r & ~(2 * m - 1)) == (c & ~(2 * m - 1))) & ((r & (2 * m - 1)) >= m) & ((c & (2 * m - 1)) < m)).astype(F32)
        lv.append((up, 1.0 - up, selq, selq.T, selk, selk.T, jnp.concatenate([pm] * NH, axis=0)))
    hm4 = lambda n: (((_iota((NH, 1, n), 2) & (GW - 1)) >> 6) == _iota((NH, 1, n), 0)).astype(F32)
    return dict(tri=tri, trit=tri.T, rr=rr, lv=lv, bd=_bdmask(), bd64=_bdmask() * (1.0 / HD),
                hm4={GW: hm4(GW), 3 * GW: hm4(3 * GW)})


def _hgrn_chunk_fn(hq, hf, hi, lb, wn, st, cs):
    q = _silu(hq)
    log_lb = jnp.log(jnp.maximum(lb, LB_FLOOR))
    a, bb = log_lb, jnp.log1p(-lb) + _logsig(hf)
    g = jnp.maximum(a, bb) + jnp.log1p(jnp.exp(-jnp.abs(a - bb)))
    k = (1.0 - lb) * _sigmoid(-hf)
    v = hi
    rr = cs["rr"]
    b = xl(cs["tri"], cs["trit"], g)
    row_of = lambda n: jnp.sum(b * (rr == n).astype(F32), axis=0, keepdims=True)
    o = mm_nt(q * jnp.exp(b), st)
    qs, ks = [], []
    for ib in (1, 2, 3):
        ref = row_of(16 * ib - 1)
        inq = ((rr >= 16 * ib) & (rr < 16 * ib + 16)).astype(F32)
        ink = (rr < 16 * ib).astype(F32)
        qs.append(q * jnp.exp((b - ref) * inq) * inq)
        ks.append(k * jnp.exp((ref - b) * ink) * ink)
    qcat, kcat = jnp.concatenate(qs, axis=1), jnp.concatenate(ks, axis=1)
    lvl = []
    for up, lo, selq, selqt, selk, selkt, pm in cs["lv"]:
        qe = q * jnp.exp((b - xl(selq, selqt, b)) * up) * up
        ke = k * jnp.exp((xl(selk, selkt, b) - b) * lo) * lo
        lvl.append((qe, ke, pm))
    stack = lambda x: (x[None] * cs["hm4"][x.shape[1]]).reshape(NH * CH, x.shape[1])
    a_all = mm_nt(stack(qcat), kcat)
    for qe, ke, pm4 in lvl:
        a_all = a_all + mm_nt(stack(qe), ke) * pm4
    o = o + jnp.sum(mm(a_all, v).reshape(NH, CH, GW) * cs["hm4"][GW], axis=0)
    o = o + xr(q * k, cs["bd"], cs["bd"]) * v
    b_last = row_of(CH - 1)
    st_new = st * jnp.exp(b_last) + mm_tn(v, k * jnp.exp(b_last - b)) * cs["bd"]
    return _headrms(o, wn, cs["bd64"]), st_new


def hgrn_fwd(pc, lb, wn, bl, s, layer, tag):
    nc = s // CH

    def body(q_ref, f_ref, i_ref, lb_ref, wn_ref, o_ref, st_ref, st):
        @pl.when(pl.program_id(0) == 0)
        def _():
            st[...] = jnp.zeros_like(st)

        cs = _hgrn_consts()
        for b in range(bl):
            st_ref[b] = st[b]
            o, st_new = _hgrn_chunk_fn(q_ref[b], f_ref[b], i_ref[b], lb_ref[...], wn_ref[...], st[b], cs)
            o_ref[b] = o
            st[b] = st_new

    tok = lambda j: pl.BlockSpec((bl, CH, GW), lambda c: (0, c, j))
    par = pl.BlockSpec((None, 1, GW), lambda c: (layer, 0, 0))
    pc3 = pc.reshape(bl, s, 3 * GW)
    o, states = _call(
        body, f"hgrn_fwd_{tag}", (nc,), [tok(0), tok(1), tok(2), par, par],
        [tok(0), pl.BlockSpec((bl, None, GW, GW), lambda c: (0, c, 0, 0))],
        [_sds((bl, s, GW)), _sds((bl, nc, GW, GW))],
        [pltpu.VMEM((bl, GW, GW), F32)],
    )(pc3, pc3, pc3, lb, wn)
    return o.reshape(bl * s, GW), states


def hgrn_bwd(pc, lb, wn, states, do, bl, s, layer, tag):
    nc = s // CH

    def body(q_ref, f_ref, i_ref, lb_ref, wn_ref, st_ref, do_ref, dc_ref, dlb_ref, dwn_ref, dst):
        c = pl.program_id(0)

        @pl.when(c == 0)
        def _():
            dst[...] = jnp.zeros_like(dst)

        cs = _hgrn_consts()
        dlb_sum = dwn_sum = None
        for b in range(bl):
            _, vjp = jax.vjp(lambda *a: _hgrn_chunk_fn(*a, cs), q_ref[b], f_ref[b], i_ref[b], lb_ref[...],
                             wn_ref[...], st_ref[b])
            dq, df, di, dlb, dwn, dst_in = vjp((do_ref[b], dst[b]))
            dst[b] = dst_in
            dc_ref[b, :, 0:GW] = _cast(dq)
            dc_ref[b, :, GW:2 * GW] = _cast(df)
            dc_ref[b, :, 2 * GW:3 * GW] = _cast(di)
            dlb_sum = dlb if dlb_sum is None else dlb_sum + dlb
            dwn_sum = dwn if dwn_sum is None else dwn_sum + dwn
        _acc(dlb_ref, dlb_sum, c == 0)
        _acc(dwn_ref, dwn_sum, c == 0)

    tok = lambda j: pl.BlockSpec((bl, CH, GW), lambda c: (0, nc - 1 - c, j))
    par = pl.BlockSpec((None, 1, GW), lambda c: (layer, 0, 0))
    acc = pl.BlockSpec((1, GW), lambda c: (0, 0))
    pc3 = pc.reshape(bl, s, 3 * GW)
    dc, dlb, dwn = _call(
        body, f"hgrn_bwd_{tag}", (nc,),
        [tok(0), tok(1), tok(2), par, par, pl.BlockSpec((bl, None, GW, GW), lambda c: (0, nc - 1 - c, 0, 0)), tok(0)],
        [pl.BlockSpec((bl, CH, 3 * GW), lambda c: (0, nc - 1 - c, 0)), acc, acc],
        [_sds((bl, s, 3 * GW), _MMT), _sds((1, GW)), _sds((1, GW))],
        [pltpu.VMEM((bl, GW, GW), F32)],
    )(pc3, pc3, pc3, lb, wn, states, do.reshape(bl, s, GW))
    return dc.reshape(bl * s, 3 * GW), dlb, dwn


def _shift_rows(x, k, up):
    n = x.shape[0]
    rr = _iota((n, 1), 0)
    if up:
        return jnp.where(rr < n - k, pltpu.roll(x, n - k, 0), 0.0)
    return jnp.where(rr >= k, pltpu.roll(x, k, 0), 0.0)


def _window_sums(x, up):
    s2 = x + _shift_rows(x, 1, up)
    s4 = s2 + _shift_rows(s2, 2, up)
    s8 = s4 + _shift_rows(s4, 4, up)
    s16 = s8 + _shift_rows(s8, 8, up)
    return s2, s4, s8, s16


def _pool_div(n):
    pos = (_iota((n, 1), 0) + 1).astype(F32)
    return [jnp.minimum(pos, float(w)) for w in (2, 4, 8, 16)]


def _pool_mix(sums, scaled):
    out = None
    for gi, sw in enumerate(sums):
        part = (sw if scaled is None else sw / scaled[gi]) * _hmask(gi)
        out = part if out is None else out + part
    return out


def pool_fwd(pd, wbd, scale, bl, s, layer, tag):
    def body(u_ref, w_ref, sc_ref, o_ref):
        u = u_ref[...]
        pm = _pool_mix(_window_sums(u, False), _pool_div(s)) - u
        o_ref[...] = _dg(_cast(pm), _cast(w_ref[...]), 1, 0) * sc_ref[...]

    seq = pl.BlockSpec((s, GW), lambda b: (b, 0))
    return _call(
        body, f"pool_fwd_{tag}", (bl,),
        [seq, pl.BlockSpec((None, GW, GW), lambda b: (layer, 0, 0)), pl.BlockSpec((None, 1, GW), lambda b: (layer, 0, 0))],
        seq, _sds((bl * s, GW)),
    )(pd, wbd, scale)


def pool_bwd(pd, wbd, scale, do, bl, s, layer, tag):
    def body(u_ref, w_ref, sc_ref, do_ref, du_ref, dw_ref, dsc_ref):
        first = pl.program_id(0) == 0
        u, do = u_ref[...], do_ref[...]
        div = _pool_div(s)
        pm = _pool_mix(_window_sums(u, False), div) - u
        ypre = _dg(_cast(pm), _cast(w_ref[...]), 1, 0)
        dys = do * sc_ref[...]
        _acc(dsc_ref, jnp.sum(do * ypre, axis=0, keepdims=True), first)
        _acc(dw_ref, _dg(_cast(pm), _cast(dys), 0, 0), first)
        dpm = _dg(_cast(dys), _cast(w_ref[...]), 1, 1)
        dsc = [dpm / d for d in div]
        adj = None
        for gi in range(4):
            part = _window_sums(dsc[gi] * _hmask(gi), True)[gi]
            adj = part if adj is None else adj + part
        du_ref[...] = _cast(adj - dpm)

    seq = pl.BlockSpec((s, GW), lambda b: (b, 0))
    return _call(
        body, f"pool_bwd_{tag}", (bl,),
        [seq, pl.BlockSpec((None, GW, GW), lambda b: (layer, 0, 0)), pl.BlockSpec((None, 1, GW), lambda b: (layer, 0, 0)), seq],
        [seq, pl.BlockSpec((GW, GW), lambda b: (0, 0)), pl.BlockSpec((1, GW), lambda b: (0, 0))],
        [_sds((bl * s, GW), _MMT), _sds((GW, GW)), _sds((1, GW))],
    )(pd, wbd, scale, do)


def _mem_prep_fn(mem, g, wk, wv, kw, bd64):
    mn = _rms(mem, g)
    return _headrms(mm(mn, wk), kw, bd64), mm(mn, wv)


def mem_prep_fwd(mem2, g, wkv, kw, bl, layer, tag):
    def body(m_ref, g_ref, wk_ref, wv_ref, kw_ref, k_ref, v_ref):
        k, v = _mem_prep_fn(m_ref[...], g_ref[...], wk_ref[...], wv_ref[...], kw_ref[...], _bdmask() * (1.0 / HD))
        k_ref[...] = k
        v_ref[...] = v

    blk = pl.BlockSpec((N_MEM, GW), lambda b: (b, 0))
    return _call(
        body, f"mem_prep_fwd_{tag}", (bl,),
        [pl.BlockSpec((N_MEM, D_MODEL), lambda b: (b, 0)), pl.BlockSpec((None, 1, D_MODEL), lambda b: (layer, 0, 0)),
         pl.BlockSpec((None, D_MODEL, GW), lambda b: (layer, 0, 0)), pl.BlockSpec((None, D_MODEL, GW), lambda b: (layer, 0, 1)),
         pl.BlockSpec((None, 1, GW), lambda b: (layer, 0, 0))],
        [blk, blk], [_sds((bl * N_MEM, GW))] * 2,
    )(mem2, g, wkv, wkv, kw)


def mem_prep_bwd(mem2, g, wkv, kw, dk, dv, bl, layer, tag):
    def body(m_ref, g_ref, wk_ref, wv_ref, kw_ref, dk_ref, dv_ref, dwk_ref, dwv_ref, dg_ref, dkw_ref):
        first = pl.program_id(0) == 0
        bd64 = _bdmask() * (1.0 / HD)
        _, vjp = jax.vjp(lambda g_, wk, wv, kw_: _mem_prep_fn(m_ref[...], g_, wk, wv, kw_, bd64),
                         g_ref[...], wk_ref[...].astype(F32), wv_ref[...].astype(F32), kw_ref[...])
        dg, dwk, dwv, dkw = vjp((dk_ref[...], dv_ref[...]))
        _acc(dwk_ref, dwk, first)
        _acc(dwv_ref, dwv, first)
        _acc(dg_ref, dg, first)
        _acc(dkw_ref, dkw, first)

    blk = pl.BlockSpec((N_MEM, GW), lambda b: (b, 0))
    return _call(
        body, f"mem_prep_bwd_{tag}", (bl,),
        [pl.BlockSpec((N_MEM, D_MODEL), lambda b: (b, 0)), pl.BlockSpec((None, 1, D_MODEL), lambda b: (layer, 0, 0)),
         pl.BlockSpec((None, D_MODEL, GW), lambda b: (layer, 0, 0)), pl.BlockSpec((None, D_MODEL, GW), lambda b: (layer, 0, 1)),
         pl.BlockSpec((None, 1, GW), lambda b: (layer, 0, 0)), blk, blk],
        [pl.BlockSpec((D_MODEL, GW), lambda b: (0, 0)), pl.BlockSpec((D_MODEL, GW), lambda b: (0, 0)),
         pl.BlockSpec((1, D_MODEL), lambda b: (0, 0)), pl.BlockSpec((1, GW), lambda b: (0, 0))],
        [_sds((D_MODEL, GW)), _sds((D_MODEL, GW)), _sds((1, D_MODEL)), _sds((1, GW))],
    )(mem2, g, wkv, wkv, kw, dk, dv)


def _mem_attn_fn(mq, qw, k, v, bd64):
    qn = _headrms(mq, qw, bd64)
    out = None
    for h in range(NH):
        hm = _hmask(h)
        lg = mm_nt(qn * hm, k) * SCALE
        e = jnp.exp(lg - lax.stop_gradient(jnp.max(lg, axis=-1, keepdims=True)))
        p = e / jnp.sum(e, axis=-1, keepdims=True)
        part = mm(p, v) * hm
        out = part if out is None else out + part
    return out


def mem_attn_fwd(pe, qw, k, v, bl, s, layer, tag):
    TQ = _token_tile(s)
    nq = s // TQ

    def body(q_ref, qw_ref, k_ref, v_ref, o_ref):
        o_ref[...] = _mem_attn_fn(q_ref[...], qw_ref[...], k_ref[...], v_ref[...], _bdmask() * (1.0 / HD))

    tok = pl.BlockSpec((TQ, GW), lambda b, i: (b * nq + i, 0))
    kv = pl.BlockSpec((N_MEM, GW), lambda b, i: (b, 0))
    return _call(
        body, f"mem_attn_fwd_{tag}", (bl, nq), [tok, pl.BlockSpec((None, 1, GW), lambda b, i: (layer, 0, 0)), kv, kv],
        tok, _sds((bl * s, GW)),
    )(pe, qw, k, v)


def mem_attn_bwd(pe, qw, k, v, do, bl, s, layer, tag):
    TQ = _token_tile(s)
    nq = s // TQ

    def body(q_ref, qw_ref, k_ref, v_ref, do_ref, dq_ref, dk_ref, dv_ref, dqw_ref):
        i = pl.program_id(1)
        bd64 = _bdmask() * (1.0 / HD)
        _, vjp = jax.vjp(lambda *a: _mem_attn_fn(*a, bd64), q_ref[...], qw_ref[...], k_ref[...], v_ref[...])
        dq, dqw, dk, dv = vjp(do_ref[...])
        dq_ref[...] = _cast(dq)
        _acc(dk_ref, dk, i == 0)
        _acc(dv_ref, dv, i == 0)
        _acc(dqw_ref, dqw, jnp.logical_and(pl.program_id(0) == 0, i == 0))

    tok = pl.BlockSpec((TQ, GW), lambda b, i: (b * nq + i, 0))
    kv = pl.BlockSpec((N_MEM, GW), lambda b, i: (b, 0))
    return _call(
        body, f"mem_attn_bwd_{tag}", (bl, nq),
        [tok, pl.BlockSpec((None, 1, GW), lambda b, i: (layer, 0, 0)), kv, kv, tok],
        [tok, kv, kv, pl.BlockSpec((1, GW), lambda b, i: (0, 0))],
        [_sds((bl * s, GW), _MMT), _sds((bl * N_MEM, GW)), _sds((bl * N_MEM, GW)), _sds((1, GW))],
    )(pe, qw, k, v, do)


def _gate_out_fn(outs, gates, wparts):
    y = None
    for o, g, w in zip(outs, gates, wparts):
        part = mm(o * _silu(g), w)
        y = part if y is None else y + part
    return y


def outproj_fwd(x2, outs, pg, wout, layer, tag, tgt=None):
    t = x2.shape[0]
    TQ = _token_tile(t)

    def body(x_ref, oa, ob, oc, od, oe, g_ref, w_ref, *rest):
        outs_ = [r[...] for r in (oa, ob, oc, od, oe)]
        gates = [g_ref[:, j * GW:(j + 1) * GW] for j in range(5)]
        wparts = [w_ref[j * GW:(j + 1) * GW, :] for j in range(5)]
        y = x_ref[...] + _gate_out_fn(outs_, gates, wparts)
        if tgt is None:
            rest[0][...] = y
            return
        t_ref, l_ref, dy_ref = rest
        diff = y - t_ref[...]
        dy_ref[...] = diff * (1.0 / D_MODEL)
        part = 0.5 * jnp.sum(jnp.sum(diff * diff, axis=-1, keepdims=True) * (1.0 / D_MODEL), axis=0, keepdims=True)
        _acc(l_ref, jnp.broadcast_to(part, (8, 128)), pl.program_id(0) == 0)

    tok = pl.BlockSpec((TQ, GW), lambda i: (i, 0))
    big = pl.BlockSpec((TQ, D_MODEL), lambda i: (i, 0))
    ins = [big] + [tok] * 5 + [pl.BlockSpec((TQ, D_MIX), lambda i: (i, 0)),
                               pl.BlockSpec((None, D_MIX, D_MODEL), lambda i: (layer, 0, 0))]
    if tgt is None:
        return _call(body, f"outproj_fwd_{tag}", (t // TQ,), ins, big, _sds((t, D_MODEL)))(x2, *outs, pg, wout)
    return _call(body, f"outproj_loss_{tag}", (t // TQ,), ins + [big], [pl.BlockSpec((8, 128), lambda i: (0, 0)), big],
                 [_sds((8, 128)), _sds((t, D_MODEL))])(x2, *outs, pg, wout, tgt)


def outproj_bwd(outs, pg, wout, dy, layer, tag):
    t = dy.shape[0]
    TQ = _token_tile(t)

    def body(oa, ob, oc, od, oe, g_ref, w_ref, dy_ref, da, db, dc, dd, de, dg_ref, dw_ref):
        outs_ = [r[...] for r in (oa, ob, oc, od, oe)]
        gates = [g_ref[:, j * GW:(j + 1) * GW] for j in range(5)]
        wparts = [w_ref[j * GW:(j + 1) * GW, :].astype(F32) for j in range(5)]
        _, vjp = jax.vjp(_gate_out_fn, outs_, gates, wparts)
        douts, dgates, dws = vjp(dy_ref[...])
        for r, val in zip((da, db, dc, dd, de), douts):
            r[...] = val
        first = pl.program_id(0) == 0
        for j in range(5):
            dg_ref[:, j * GW:(j + 1) * GW] = _cast(dgates[j])

        @pl.when(first)
        def _():
            for j in range(5):
                dw_ref[j * GW:(j + 1) * GW, :] = dws[j]

        @pl.when(jnp.logical_not(first))
        def _():
            for j in range(5):
                dw_ref[j * GW:(j + 1) * GW, :] += dws[j]

    tok = pl.BlockSpec((TQ, GW), lambda i: (i, 0))
    return _call(
        body, f"outproj_bwd_{tag}", (t // TQ,),
        [tok] * 5 + [pl.BlockSpec((TQ, D_MIX), lambda i: (i, 0)), pl.BlockSpec((None, D_MIX, D_MODEL), lambda i: (layer, 0, 0)),
                     pl.BlockSpec((TQ, D_MODEL), lambda i: (i, 0))],
        [tok] * 5 + [pl.BlockSpec((TQ, D_MIX), lambda i: (i, 0)), pl.BlockSpec((D_MIX, D_MODEL), lambda i: (0, 0))],
        [_sds((t, GW))] * 5 + [_sds((t, D_MIX), _MMT), _sds((D_MIX, D_MODEL))],
    )(*outs, pg, wout, dy)


def layer_fwd(x2, mem2, p, layer, bl, s, tgt=None):
    tag = f"l{layer}"
    ht, pa, pb, pc, pd, pe, pg, pf = inproj_fwd(x2, p["norm_g"], p["w_in"], layer, tag)
    qn, kn, vb, cq, ck = fox_prep_fwd(pa, pf, p["fox_q_norm"], p["fox_k_norm"], p["fox_f_bias"], bl, s, layer, tag)
    oa, lse = fox_attn_fwd(qn, kn, vb, cq, ck, bl, s, tag)
    ob = sb_attn_fwd(pb, bl, s, tag)
    oc, states = hgrn_fwd(pc, p["lb"], p["hgrn_out_norm"], bl, s, layer, tag)
    od = pool_fwd(pd, p["pool_wbd"], p["pool_scale"], bl, s, layer, tag)
    mk, mv = mem_prep_fwd(mem2, p["mem_norm_g"], p["mem_w_kv"], p["mem_k_norm"], bl, layer, tag)
    oe = mem_attn_fwd(pe, p["mem_q_norm"], mk, mv, bl, s, layer, tag)
    y = outproj_fwd(x2, (oa, ob, oc, od, oe), pg, p["w_out"], layer, tag, tgt)
    saved = dict(x2=x2, ht=ht, pa=pa, pb=pb, pc=pc, pd=pd, pe=pe, pg=pg, pf=pf, qn=qn, kn=kn, vb=vb, cq=cq, ck=ck,
                 oa=oa, lse=lse, ob=ob, oc=oc, states=states, od=od, mk=mk, mv=mv, oe=oe)
    return y, saved


def layer_bwd(dy, mem2, p, sv, layer, bl, s):
    tag = f"l{layer}"
    (doa, dob, doc, dod, doe, dg_gates, dwout) = outproj_bwd((sv["oa"], sv["ob"], sv["oc"], sv["od"], sv["oe"]), sv["pg"],
                                                              p["w_out"], dy, layer, tag)
    dqn, dkn, dv, dck = fox_attn_bwd(sv["qn"], sv["kn"], sv["vb"], sv["cq"], sv["ck"], sv["lse"], doa, bl, s, tag)
    d_a, d_f, dqw, dkw, dbias = fox_prep_bwd(sv["pa"], sv["pf"], p["fox_q_norm"], p["fox_k_norm"], p["fox_f_bias"], sv["cq"],
                                             dqn, dkn, dv, dck, bl, s, layer, tag)
    dsq, dsk, dsv = sb_attn_bwd(sv["pb"], dob, bl, s, tag)
    d_c, dlb, dwn = hgrn_bwd(sv["pc"], p["lb"], p["hgrn_out_norm"], sv["states"], doc, bl, s, layer, tag)
    d_d, dwbd, dpscale = pool_bwd(sv["pd"], p["pool_wbd"], p["pool_scale"], dod, bl, s, layer, tag)
    d_e, dmk, dmv, dmqw = mem_attn_bwd(sv["pe"], p["mem_q_norm"], sv["mk"], sv["mv"], doe, bl, s, layer, tag)
    dwk, dwv, dmg, dmkw = mem_prep_bwd(mem2, p["mem_norm_g"], p["mem_w_kv"], p["mem_k_norm"], dmk, dmv, bl, layer, tag)
    dpieces = (d_a, dsq, dsk, dsv, d_c, d_d, d_e, dg_gates, d_f)
    dx, dng = inproj_bwd_dx(sv["x2"], p["norm_g"], p["w_in"], dy, dpieces, layer, tag)
    groups = ((0, 4), (4, 7), (7, 9))
    dwin = jnp.concatenate([matmul_acc(sv["ht"], dpieces[a:b], f"{tag}_{a}") for a, b in groups], axis=1)
    grads = dict(norm_g=dng, w_in=dwin, fox_f_bias=dbias, fox_q_norm=dqw, fox_k_norm=dkw, lb=dlb, hgrn_out_norm=dwn,
                 pool_wbd=dwbd, pool_scale=dpscale, mem_norm_g=dmg, mem_w_kv=jnp.concatenate([dwk, dwv], axis=1),
                 mem_q_norm=dmqw, mem_k_norm=dmkw, w_out=dwout)
    return dx, grads


def _tile4(w):
    return jnp.tile(w, (1, NH))[:, None, :]


def prepare_params(norm_g, w_in_p, fox_f_bias, fox_q_norm, fox_k_norm, hgrn_lb_logits, hgrn_out_norm, pool_w, pool_scale,
                   mem_norm_g, mem_w_kv, mem_q_norm, mem_k_norm, w_out):
    p1 = jax.nn.sigmoid(hgrn_lb_logits[1] - hgrn_lb_logits[0])
    lb = jnp.stack([jnp.zeros_like(p1), jnp.clip(p1, 0.0, 1.0 - 1e-6)])
    eye = jnp.eye(4, dtype=F32)
    wbd = jnp.einsum("lgcd,gh->lgchd", pool_w, eye).reshape(2, GW, GW)
    return dict(norm_g=norm_g[:, None, :], w_in=w_in_p, fox_f_bias=jnp.pad(fox_f_bias, ((0, 0), (0, 124)))[:, None, :],
                fox_q_norm=_tile4(fox_q_norm), fox_k_norm=_tile4(fox_k_norm), lb=lb[:, None, :],
                hgrn_out_norm=hgrn_out_norm[:, None, :], pool_wbd=wbd, pool_scale=pool_scale[:, None, :],
                mem_norm_g=mem_norm_g[:, None, :], mem_w_kv=mem_w_kv, mem_q_norm=_tile4(mem_q_norm),
                mem_k_norm=_tile4(mem_k_norm), w_out=w_out)


def local_step(x, mem, tgt, p):
    bl, s, _ = x.shape
    x2, mem2, tgt2 = x.reshape(bl * s, D_MODEL), mem.reshape(bl * N_MEM, D_MODEL), tgt.reshape(bl * s, D_MODEL)
    y0, sv0 = layer_fwd(x2, mem2, p, 0, bl, s)
    (lpart, dy), sv1 = layer_fwd(y0, mem2, p, 1, bl, s, tgt2)
    dx1, g1 = layer_bwd(dy, mem2, p, sv1, 1, bl, s)
    dx0, g0 = layer_bwd(dx1, mem2, p, sv0, 0, bl, s)
    return lpart[0, 0], dx0.reshape(bl, s, D_MODEL), g0, g1


_ANY = pl.BlockSpec(memory_space=pl.ANY)


def all_gather_rows(xss, tag):
    n = len(xss)

    def body(*refs):
        x_refs, o_refs, (send_sems, recv_sems, local_sems) = refs[:n], refs[n:2 * n], refs[2 * n:]
        x, y, cc = lax.axis_index("x"), lax.axis_index("y"), lax.axis_index("c")
        me, sibling = (x, y, cc), (x, y, 1 - cc)
        chips = [(1 - x, y), (x, 1 - y), (1 - x, 1 - y)]

        def rows(a, px, py, pc):
            r = xss[a].shape[1]
            return o_refs[a].at[:, pl.ds((4 * px + 2 * py + pc) * r, r), :]

        def copy(a, k, block, to, src=None):
            return pltpu.make_async_remote_copy(src_ref=rows(a, *block) if src is None else src, dst_ref=rows(a, *block),
                                                send_sem=send_sems.at[7 * a + k], recv_sem=recv_sems.at[7 * a + k], device_id=to,
                                                device_id_type=pl.DeviceIdType.MESH)

        mine = [pltpu.make_async_copy(x_refs[a], rows(a, *me), local_sems.at[a]) for a in range(n)]
        first = []
        for a in range(n):
            first += [copy(a, 0, me, sibling, src=x_refs[a])] + [copy(a, 1 + j, me, (*chip, cc), src=x_refs[a])
                                                                 for j, chip in enumerate(chips)]
        for cp in mine + first:
            cp.start()
        passed = []
        for j, chip in enumerate(chips):
            for a in range(n):
                copy(a, 1 + j, (*chip, cc), me).wait_recv()
                passed.append(copy(a, 4 + j, (*chip, cc), sibling))
                passed[-1].start()
        for a in range(n):
            copy(a, 0, sibling, me).wait_recv()
        for j, chip in enumerate(chips):
            for a in range(n):
                copy(a, 4 + j, (*chip, 1 - cc), me).wait_recv()
        for cp in first + passed:
            cp.wait_send()
        for cp in mine:
            cp.wait()

    nsem = pltpu.SemaphoreType.DMA((7 * n,))
    return pl.pallas_call(
        body, name=f"all_gather_{tag}", in_specs=[_ANY] * n, out_specs=[_ANY] * n,
        out_shape=[_sds((xs.shape[0], N_DEV * xs.shape[1], xs.shape[2]), xs.dtype) for xs in xss],
        scratch_shapes=[nsem, nsem, pltpu.SemaphoreType.DMA((n,))],
    )(*xss)


def exchange_cores(parts, tag):
    n = len(parts)
    counts = [p.shape[0] * 4 for p in parts]

    def body(*refs):
        p_refs, t_refs, (send_sems, recv_sems) = refs[:n], refs[n:2 * n], refs[2 * n:]
        x, y, cc = lax.axis_index("x"), lax.axis_index("y"), lax.axis_index("c")
        copies, k = [], 0
        for a in range(n):
            for l in range(parts[a].shape[0]):
                for q in range(4):
                    copies.append(pltpu.make_async_remote_copy(
                        src_ref=p_refs[a].at[l, q, pl.ds(1 - cc, 1)], dst_ref=t_refs[a].at[l, q], send_sem=send_sems.at[k],
                        recv_sem=recv_sems.at[k], device_id=(x, y, 1 - cc), device_id_type=pl.DeviceIdType.MESH))
                    k += 1
        for cp in copies:
            cp.start()
        for cp in copies:
            cp.wait()

    nsem = pltpu.SemaphoreType.DMA((sum(counts),))
    return pl.pallas_call(
        body, name=f"exchange_cores_{tag}", in_specs=[_ANY] * n, out_specs=[_ANY] * n,
        out_shape=[_sds((p.shape[0], 4, 1, p.shape[3], p.shape[4]), p.dtype) for p in parts],
        scratch_shapes=[nsem, nsem],
    )(*parts)


def add_core_halves(part5, theirs, core, tag):
    nl, _, _, r, c = part5.shape
    tr = r if r * c <= 256 * 1024 else 64

    def body(core_ref, a_ref, b_ref, o_ref):
        o_ref[...] = _cast(a_ref[...] + b_ref[...])

    blk = lambda which: pl.BlockSpec((None, None, None, tr, c), lambda l, q, i, cref: (l, q, cref[0] if which else 0, i, 0))
    return pl.pallas_call(
        body, name=f"add_core_halves_{tag}", out_shape=_sds((nl, 4, 1, r, c), _MMT),
        grid_spec=pltpu.PrefetchScalarGridSpec(num_scalar_prefetch=1, grid=(nl, 4, r // tr), in_specs=[blk(True), blk(False)],
                                               out_specs=blk(False)),
        compiler_params=pltpu.CompilerParams(dimension_semantics=("arbitrary",) * 3, vmem_limit_bytes=VMEM_LIMIT_BYTES),
    )(core, part5, theirs)


def exchange_chips(s4s, tag):
    n = len(s4s)

    def body(*refs):
        s_refs, o_refs, (send_sems, recv_sems, local_sems) = refs[:n], refs[n:2 * n], refs[2 * n:]
        x, y, cc = lax.axis_index("x"), lax.axis_index("y"), lax.axis_index("c")
        copies = [pltpu.make_async_copy(s_refs[a].at[:, pl.ds(2 * x + y, 1)], o_refs[a].at[0], local_sems.at[a]) for a in range(n)]
        for k in range(1, 4):
            px = 1 - x if (k >> 1) & 1 else x
            py = 1 - y if k & 1 else y
            for a in range(n):
                copies.append(pltpu.make_async_remote_copy(
                    src_ref=s_refs[a].at[:, pl.ds(2 * px + py, 1)], dst_ref=o_refs[a].at[k], send_sem=send_sems.at[3 * a + k - 1],
                    recv_sem=recv_sems.at[3 * a + k - 1], device_id=(px, py, cc), device_id_type=pl.DeviceIdType.MESH))
        for cp in copies:
            cp.start()
        for cp in copies:
            cp.wait()

    nsem = pltpu.SemaphoreType.DMA((3 * n,))
    return pl.pallas_call(
        body, name=f"exchange_chips_{tag}", in_specs=[_ANY] * n, out_specs=[_ANY] * n,
        out_shape=[_sds((4, s.shape[0], 1, 1, s.shape[3], s.shape[4]), s.dtype) for s in s4s],
        scratch_shapes=[nsem, nsem, pltpu.SemaphoreType.DMA((n,))],
    )(*s4s)


def _row_tile(rows):
    if rows <= 512 and rows % 64:
        return rows
    for t in (64, 40, 32, 16, 8):
        if rows % t == 0:
            return t
    return rows


def sum_slots(slots, tag):
    ns, rows, c = slots.shape
    tr = _row_tile(rows)

    def body(s_ref, o_ref):
        acc = s_ref[0].astype(F32)
        for k in range(1, ns):
            acc = acc + s_ref[k].astype(F32)
        o_ref[...] = acc

    return _call(body, f"sum_slots_{tag}", (rows // tr,), [pl.BlockSpec((ns, tr, c), lambda i: (0, i, 0))],
                 pl.BlockSpec((tr, c), lambda i: (i, 0)), _sds((rows, c)))(slots)


def _adamw(w, g, m, v):
    m = ADAM_B1 * m + (1.0 - ADAM_B1) * g
    v = ADAM_B2 * v + (1.0 - ADAM_B2) * (g * g)
    m_hat = m / (1.0 - ADAM_B1 ** ADAM_STEP)
    v_hat = v / (1.0 - ADAM_B2 ** ADAM_STEP)
    delta = -ADAM_LR * (m_hat / (jnp.sqrt(v_hat) + ADAM_EPS) + ADAM_WD * w)
    return delta, m, v


def adam_update(w, m, v, g, tag, slots=False):
    rows, c = w.shape
    tr = _row_tile(rows)
    ns = g.shape[0] if slots else 0

    def body(w_ref, m_ref, v_ref, g_ref, go_ref, d_ref, mo_ref, vo_ref):
        if slots:
            g = g_ref[0].astype(F32)
            for k in range(1, ns):
                g = g + g_ref[k].astype(F32)
        else:
            g = g_ref[...]
        d, mn, vn = _adamw(w_ref[...], g, m_ref[...], v_ref[...])
        go_ref[...] = g
        d_ref[...] = d
        mo_ref[...] = mn
        vo_ref[...] = vn

    blk = pl.BlockSpec((tr, c), lambda i: (i, 0))
    gspec = pl.BlockSpec((ns, tr, c), lambda i: (0, i, 0)) if slots else blk
    return _call(body, f"adam_{tag}", (rows // tr,), [blk, blk, blk, gspec], [blk] * 4, [_sds((rows, c))] * 4)(w, m, v, g)


_SMALL = (("norm_g", (2, 1024)), ("fox_f_bias", (2, 4)), ("fox_q_norm", (2, 64)), ("fox_k_norm", (2, 64)),
          ("hgrn_lb_logits", (2, 256)), ("hgrn_out_norm", (2, 256)), ("pool_w", (2, 4, 64, 64)), ("pool_scale", (2, 256)),
          ("mem_norm_g", (2, 1024)), ("mem_q_norm", (2, 64)), ("mem_k_norm", (2, 64)))
_SLAB_ROWS = 312


def pack_small(d):
    flat = jnp.concatenate([d[n].reshape(-1) for n, _ in _SMALL])
    return jnp.pad(flat, (0, _SLAB_ROWS * 128 - flat.shape[0])).reshape(_SLAB_ROWS, 128)


def unpack_small(slab):
    flat, out, off = slab.reshape(-1), {}, 0
    for n, shp in _SMALL:
        size = 1
        for e in shp:
            size *= e
        out[n] = flat[off:off + size].reshape(shp)
        off += size
    return out


def small_grads(g0, g1, lb_logits):
    st = lambda f: jnp.stack([f(g0), f(g1)])
    heads = lambda a: a.reshape(NH, HD).sum(0)
    p1 = jax.nn.sigmoid(lb_logits[1] - lb_logits[0])
    inside = (p1 > 0.0) & (p1 < 1.0 - 1e-6)
    dl1 = jnp.where(inside, g1["lb"][0] * p1 * (1.0 - p1), 0.0)
    diag = lambda a: jnp.stack([a.reshape(4, HD, 4, HD)[i, :, i, :] for i in range(4)])
    return dict(norm_g=st(lambda g: g["norm_g"][0]), fox_f_bias=st(lambda g: g["fox_f_bias"][0, :NH]),
                fox_q_norm=st(lambda g: heads(g["fox_q_norm"])), fox_k_norm=st(lambda g: heads(g["fox_k_norm"])),
                hgrn_lb_logits=jnp.stack([-dl1, dl1]), hgrn_out_norm=st(lambda g: g["hgrn_out_norm"][0]),
                pool_w=st(lambda g: diag(g["pool_wbd"])), pool_scale=st(lambda g: g["pool_scale"][0]),
                mem_norm_g=st(lambda g: g["mem_norm_g"][0]), mem_q_norm=st(lambda g: heads(g["mem_q_norm"])),
                mem_k_norm=st(lambda g: heads(g["mem_k_norm"])))


def kernel(x, mem, norm_g, w_in, fox_f_bias, fox_q_norm, fox_k_norm, hgrn_lb_logits, hgrn_out_norm, pool_w, pool_scale, mem_norm_g, mem_w_kv, mem_q_norm, mem_k_norm, w_out, loss_target, m_norm_g, m_w_in, m_fox_f_bias, m_fox_q_norm, m_fox_k_norm, m_hgrn_lb_logits, m_hgrn_out_norm, m_pool_w, m_pool_scale, m_mem_norm_g, m_mem_w_kv, m_mem_q_norm, m_mem_k_norm, m_w_out, v_norm_g, v_w_in, v_fox_f_bias, v_fox_q_norm, v_fox_k_norm, v_hgrn_lb_logits, v_hgrn_out_norm, v_pool_w, v_pool_scale, v_mem_norm_g, v_mem_w_kv, v_mem_q_norm, v_mem_k_norm, v_w_out):
    given = dict(norm_g=(norm_g, m_norm_g, v_norm_g), w_in=(w_in, m_w_in, v_w_in), fox_f_bias=(fox_f_bias, m_fox_f_bias, v_fox_f_bias),
                 fox_q_norm=(fox_q_norm, m_fox_q_norm, v_fox_q_norm), fox_k_norm=(fox_k_norm, m_fox_k_norm, v_fox_k_norm),
                 hgrn_lb_logits=(hgrn_lb_logits, m_hgrn_lb_logits, v_hgrn_lb_logits),
                 hgrn_out_norm=(hgrn_out_norm, m_hgrn_out_norm, v_hgrn_out_norm), pool_w=(pool_w, m_pool_w, v_pool_w),
                 pool_scale=(pool_scale, m_pool_scale, v_pool_scale), mem_norm_g=(mem_norm_g, m_mem_norm_g, v_mem_norm_g),
                 mem_w_kv=(mem_w_kv, m_mem_w_kv, v_mem_w_kv), mem_q_norm=(mem_q_norm, m_mem_q_norm, v_mem_q_norm),
                 mem_k_norm=(mem_k_norm, m_mem_k_norm, v_mem_k_norm), w_out=(w_out, m_w_out, v_w_out))
    order = ("norm_g", "w_in", "fox_f_bias", "fox_q_norm", "fox_k_norm", "hgrn_lb_logits", "hgrn_out_norm", "pool_w",
             "pool_scale", "mem_norm_g", "mem_w_kv", "mem_q_norm", "mem_k_norm", "w_out")

    w_in_full, w_out_full, w_kv_full = all_gather_rows([_cast(permute_cols(w_in)), _cast(w_out), _cast(mem_w_kv)], "weights")
    p = prepare_params(norm_g, w_in_full, fox_f_bias, fox_q_norm, fox_k_norm, hgrn_lb_logits, hgrn_out_norm, pool_w,
                       pool_scale, mem_norm_g, w_kv_full, mem_q_norm, mem_k_norm, w_out_full)

    loss_part, grad_x, g0, g1 = local_step(x, mem, loss_target, p)
    loss = lax.psum(loss_part, ("x", "y", "c"))

    res = {}
    core = lax.axis_index("c").astype(jnp.int32).reshape(1)

    names = ("w_in", "w_out", "mem_w_kv")
    part5 = []
    for name in names:
        nl, r, _ = given[name][0].shape
        g2 = jnp.stack([g0[name], g1[name]])
        part5.append(g2.reshape(nl, 4, 2, r, g2.shape[-1]))
    s4 = [add_core_halves(p5, th, core, name) for name, p5, th in zip(names, part5, exchange_cores(part5, "grads"))]
    for name, sl in zip(names, exchange_chips(s4, "grads")):
        w, m, v = given[name]
        nl, r, c = w.shape
        slots = sl.reshape(4, nl * r, sl.shape[-1])
        if name == "w_in":
            g = sum_slots(slots, name).reshape(nl, r, -1)
            out = adam_update(w.reshape(nl * r, c), m.reshape(nl * r, c), v.reshape(nl * r, c),
                              unpermute_cols(g).reshape(nl * r, c), name)
        else:
            out = adam_update(w.reshape(nl * r, c), m.reshape(nl * r, c), v.reshape(nl * r, c), slots, name, slots=True)
        res[name] = tuple(o.reshape(nl, r, c) for o in out)

    gsmall = pack_small(small_grads(g0, g1, hgrn_lb_logits))
    gathered = all_gather_rows([gsmall[None]], "small")[0].reshape(N_DEV, _SLAB_ROWS, 128)
    slabs = adam_update(*[pack_small({n: given[n][j] for n, _ in _SMALL}) for j in range(3)], gathered, "small", slots=True)
    small = [unpack_small(sl) for sl in slabs]
    for n, _ in _SMALL:
        res[n] = tuple(small[j][n] for j in range(4))

    return (loss, grad_x, *[res[n][0] for n in order], *[res[n][1] for n in order], *[res[n][2] for n in order],
            *[res[n][3] for n in order])
```
